```python
import math
import jax, jax.numpy as jnp
from jax import lax
import numpy as np

D_MODEL = 1024
BATCH = 8
SEQ = 8192
DEPTH = 1

SB_HEADS = 8
SB_HEAD_DIM = 64
SB_WIDTH = SB_HEADS * SB_HEAD_DIM
MLA_HEADS = 8
MLA_NOPE_DIM = 64
MLA_ROPE_DIM = 32
MLA_QK_DIM = MLA_NOPE_DIM + MLA_ROPE_DIM
MLA_V_DIM = 64
MLA_WIDTH = MLA_HEADS * MLA_V_DIM
Q_LORA_RANK = 384
KV_LORA_RANK = 256
ROPE_THETA = 10000.0
MIX_WIDTH = SB_WIDTH + MLA_WIDTH
Q_BLOCK = 128
EPS = 1e-6
IN_SIZES = (SB_WIDTH, SB_WIDTH, SB_WIDTH, SB_WIDTH,
            Q_LORA_RANK, KV_LORA_RANK, MLA_ROPE_DIM, MLA_WIDTH)
IN_COLS = sum(IN_SIZES)

kernel_name = "hymba_stickbreaking_mla_adaln"


def _rms_norm(x, w):
    xf = x.astype(jnp.float32)
    y = xf * lax.rsqrt(jnp.mean(xf * xf, axis=-1, keepdims=True) + EPS)
    return (y * w.astype(jnp.float32)).astype(x.dtype)


def _rotate_half(x):
    x1, x2 = jnp.split(x, 2, axis=-1)
    return jnp.concatenate([-x2, x1], axis=-1)


def _to_blocks(q):
    B, H, S, d = q.shape
    return q.reshape(B, H, S // Q_BLOCK, Q_BLOCK, d).transpose(2, 0, 1, 3, 4)


def _from_blocks(o):
    nb, B, H, QB, d = o.shape
    return o.transpose(1, 2, 0, 3, 4).reshape(B, H, nb * QB, d)


def _stick_breaking_attention(q, k, v):
    S, d = q.shape[2], q.shape[3]
    key_pos = jnp.arange(S)
    inv_sqrt_d = 1.0 / math.sqrt(d)

    def block(args):
        i, qi = args
        q_pos = i * Q_BLOCK + jnp.arange(Q_BLOCK)
        z = jnp.einsum('bhqd,bhkd->bhqk', qi, k).astype(jnp.float32) * inv_sqrt_d
        strict = key_pos[None, :] < q_pos[:, None]
        log_keep = jnp.where(strict, jax.nn.log_sigmoid(-z), 0.0)
        after = lax.cumsum(log_keep, axis=3, reverse=True) - log_keep
        w = jnp.where(strict, jnp.exp(jax.nn.log_sigmoid(z) + after), 0.0)
        return jnp.einsum('bhqk,bhkd->bhqd', w.astype(v.dtype), v)

    out = lax.map(block, (jnp.arange(S // Q_BLOCK), _to_blocks(q)))
    return _from_blocks(out)


def _causal_softmax_attention(q, k, v):
    S, d = q.shape[2], q.shape[3]
    key_pos = jnp.arange(S)
    scale = 1.0 / math.sqrt(d)
    neg = jnp.finfo(jnp.float32).min

    def block(args):
        i, qi = args
        q_pos = i * Q_BLOCK + jnp.arange(Q_BLOCK)
        z = jnp.einsum('bhqd,bhkd->bhqk', qi, k).astype(jnp.float32) * scale
        causal = key_pos[None, :] <= q_pos[:, None]
        p = jax.nn.softmax(jnp.where(causal, z, neg), axis=-1)
        return jnp.einsum('bhqk,bhkd->bhqd', p.astype(v.dtype), v)

    out = lax.map(block, (jnp.arange(S // Q_BLOCK), _to_blocks(q)))
    return _from_blocks(out)


def _layer(x, c, cos, sin, w_ada, b_ada, norm_w, w_in, q_lora_norm, w_uq,
           kv_lora_norm, w_ukv, q_head_norm, k_head_norm, w_out):
    B, S, _ = x.shape
    ada = jax.nn.silu(c) @ w_ada + b_ada
    shift, scale, gate = jnp.split(ada[:, None, :], 3, axis=-1)
    h = _rms_norm(x, norm_w) * (1.0 + scale) + shift

    proj = h @ w_in
    q_sb, k_sb, v_sb, g_sb, c_q, c_kv, k_rope, g_mla = jnp.split(
        proj, np.cumsum(IN_SIZES)[:-1].tolist(), axis=-1)

    def heads_sb(t):
        return t.reshape(B, S, SB_HEADS, SB_HEAD_DIM).transpose(0, 2, 1, 3)
    o_sb = _stick_breaking_attention(heads_sb(q_sb), heads_sb(k_sb), heads_sb(v_sb))
    o_sb = o_sb.transpose(0, 2, 1, 3).reshape(B, S, SB_WIDTH) * jax.nn.silu(g_sb)

    q = (_rms_norm(c_q, q_lora_norm) @ w_uq).reshape(B, S, MLA_HEADS, MLA_QK_DIM)
    kv = (_rms_norm(c_kv, kv_lora_norm) @ w_ukv).reshape(
        B, S, MLA_HEADS, MLA_NOPE_DIM + MLA_V_DIM)
    k_nope, v_mla = kv[..., :MLA_NOPE_DIM], kv[..., MLA_NOPE_DIM:]
    k_r = jnp.broadcast_to(k_rope[:, :, None, :], (B, S, MLA_HEADS, MLA_ROPE_DIM))
    k = jnp.concatenate([k_nope, k_r], axis=-1)
    q = _rms_norm(q, q_head_norm)
    k = _rms_norm(k, k_head_norm)
    def rope(t):
        t_n, t_r = t[..., :MLA_NOPE_DIM], t[..., MLA_NOPE_DIM:]
        t_r = t_r * cos + _rotate_half(t_r) * sin
        return jnp.concatenate([t_n, t_r], axis=-1)
    q, k = rope(q), rope(k)
    o_mla = _causal_softmax_attention(q.transpose(0, 2, 1, 3), k.transpose(0, 2, 1, 3),
                                      v_mla.transpose(0, 2, 1, 3))
    o_mla = o_mla.transpose(0, 2, 1, 3).reshape(B, S, MLA_WIDTH) * jax.nn.silu(g_mla)

    mixed = jnp.concatenate([o_sb, o_mla], axis=-1)
    return x + gate * (mixed @ w_out)


def _fwd_setup_inputs(seed: int = 0) -> dict:
    key = jax.random.key(seed)
    ks = jax.random.split(key, 16)
    D = D_MODEL

    def nrm(k, shape, std):
        return jax.random.normal(k, shape, jnp.float32) * std

    def gain(k, n):
        return 1.0 + 0.01 * jax.random.normal(k, (DEPTH, n), jnp.float32)

    return {
        "x": nrm(ks[0], (BATCH, SEQ, D), 1.0),
        "c": nrm(ks[1], (BATCH, D), 1.0),
        "positions": jnp.broadcast_to(jnp.arange(SEQ, dtype=jnp.int32), (BATCH, SEQ)),
        "w_ada": nrm(ks[2], (DEPTH, D, 3 * D), 0.5 * D ** -0.5),
        "b_ada": nrm(ks[3], (DEPTH, 3 * D), 0.01),
        "norm_w": gain(ks[4], D),
        "w_in": nrm(ks[5], (DEPTH, D, IN_COLS), D ** -0.5),
        "q_lora_norm": gain(ks[6], Q_LORA_RANK),
        "w_uq": nrm(ks[7], (DEPTH, Q_LORA_RANK, MLA_HEADS * MLA_QK_DIM), Q_LORA_RANK ** -0.5),
        "kv_lora_norm": gain(ks[8], KV_LORA_RANK),
        "w_ukv": nrm(ks[9], (DEPTH, KV_LORA_RANK, MLA_HEADS * (MLA_NOPE_DIM + MLA_V_DIM)),
                     KV_LORA_RANK ** -0.5),
        "q_head_norm": gain(ks[10], MLA_QK_DIM),
        "k_head_norm": gain(ks[11], MLA_QK_DIM),
        "w_out": nrm(ks[12], (DEPTH, MIX_WIDTH, D), MIX_WIDTH ** -0.5),
    }


def _fwd_reference(x, c, positions, w_ada, b_ada, norm_w, w_in, q_lora_norm, w_uq,
              kv_lora_norm, w_ukv, q_head_norm, k_head_norm, w_out):
    inv_freq = ROPE_THETA ** (-jnp.arange(0, MLA_ROPE_DIM, 2, dtype=jnp.float32) / MLA_ROPE_DIM)
    ang = positions.astype(jnp.float32)[..., None] * inv_freq
    ang = jnp.concatenate([ang, ang], axis=-1)[:, :, None, :]
    cos = jnp.cos(ang).astype(x.dtype)
    sin = jnp.sin(ang).astype(x.dtype)
    for l in range(DEPTH):
        x = _layer(x, c, cos, sin, w_ada[l], b_ada[l], norm_w[l], w_in[l],
                   q_lora_norm[l], w_uq[l], kv_lora_norm[l], w_ukv[l],
                   q_head_norm[l], k_head_norm[l], w_out[l])
    return x


import jax as _jax
import jax.numpy as _jnp

TWIN_FORMAT = 'train_step'
FWD_PARAMS = ['x', 'c', 'positions', 'w_ada', 'b_ada', 'norm_w', 'w_in', 'q_lora_norm', 'w_uq', 'kv_lora_norm', 'w_ukv', 'q_head_norm', 'k_head_norm', 'w_out']
TWIN_WEIGHTS = ['w_ada', 'b_ada', 'norm_w', 'w_in', 'q_lora_norm', 'w_uq', 'kv_lora_norm', 'w_ukv', 'q_head_norm', 'k_head_norm', 'w_out']
TWIN_DIFF_INPUT = 'x'
TWIN_INPUTS = ['x', 'c', 'positions', 'w_ada', 'b_ada', 'norm_w', 'w_in', 'q_lora_norm', 'w_uq', 'kv_lora_norm', 'w_ukv', 'q_head_norm', 'k_head_norm', 'w_out', 'loss_target', 'm_w_ada', 'm_b_ada', 'm_norm_w', 'm_w_in', 'm_q_lora_norm', 'm_w_uq', 'm_kv_lora_norm', 'm_w_ukv', 'm_q_head_norm', 'm_k_head_norm', 'm_w_out', 'v_w_ada', 'v_b_ada', 'v_norm_w', 'v_w_in', 'v_q_lora_norm', 'v_w_uq', 'v_kv_lora_norm', 'v_w_ukv', 'v_q_head_norm', 'v_k_head_norm', 'v_w_out']
TWIN_OUTPUTS = ['loss', 'grad_x', 'grad_w_ada', 'grad_b_ada', 'grad_norm_w', 'grad_w_in', 'grad_q_lora_norm', 'grad_w_uq', 'grad_kv_lora_norm', 'grad_w_ukv', 'grad_q_head_norm', 'grad_k_head_norm', 'grad_w_out', 'delta_w_ada', 'delta_b_ada', 'delta_norm_w', 'delta_w_in', 'delta_q_lora_norm', 'delta_w_uq', 'delta_kv_lora_norm', 'delta_w_ukv', 'delta_q_head_norm', 'delta_k_head_norm', 'delta_w_out', 'new_m_w_ada', 'new_m_b_ada', 'new_m_norm_w', 'new_m_w_in', 'new_m_q_lora_norm', 'new_m_w_uq', 'new_m_kv_lora_norm', 'new_m_w_ukv', 'new_m_q_head_norm', 'new_m_k_head_norm', 'new_m_w_out', 'new_v_w_ada', 'new_v_b_ada', 'new_v_norm_w', 'new_v_w_in', 'new_v_q_lora_norm', 'new_v_w_uq', 'new_v_kv_lora_norm', 'new_v_w_ukv', 'new_v_q_head_norm', 'new_v_k_head_norm', 'new_v_w_out']
TWIN_LEAF_KINDS = {'loss': 'loss', 'grad_x': 'grad_x', 'grad_w_ada': 'grad_w', 'grad_b_ada': 'grad_w', 'grad_norm_w': 'grad_w', 'grad_w_in': 'grad_w', 'grad_q_lora_norm': 'grad_w', 'grad_w_uq': 'grad_w', 'grad_kv_lora_norm': 'grad_w', 'grad_w_ukv': 'grad_w', 'grad_q_head_norm': 'grad_w', 'grad_k_head_norm': 'grad_w', 'grad_w_out': 'grad_w', 'delta_w_ada': 'delta_w', 'delta_b_ada': 'delta_w', 'delta_norm_w': 'delta_w', 'delta_w_in': 'delta_w', 'delta_q_lora_norm': 'delta_w', 'delta_w_uq': 'delta_w', 'delta_kv_lora_norm': 'delta_w', 'delta_w_ukv': 'delta_w', 'delta_q_head_norm': 'delta_w', 'delta_k_head_norm': 'delta_w', 'delta_w_out': 'delta_w', 'new_m_w_ada': 'new_m', 'new_m_b_ada': 'new_m', 'new_m_norm_w': 'new_m', 'new_m_w_in': 'new_m', 'new_m_q_lora_norm': 'new_m', 'new_m_w_uq': 'new_m', 'new_m_kv_lora_norm': 'new_m', 'new_m_w_ukv': 'new_m', 'new_m_q_head_norm': 'new_m', 'new_m_k_head_norm': 'new_m', 'new_m_w_out': 'new_m', 'new_v_w_ada': 'new_v', 'new_v_b_ada': 'new_v', 'new_v_norm_w': 'new_v', 'new_v_w_in': 'new_v', 'new_v_q_lora_norm': 'new_v', 'new_v_w_uq': 'new_v', 'new_v_kv_lora_norm': 'new_v', 'new_v_w_ukv': 'new_v', 'new_v_q_head_norm': 'new_v', 'new_v_k_head_norm': 'new_v', 'new_v_w_out': 'new_v'}


def _forward(args):
    return _fwd_reference(*[args[k] for k in FWD_PARAMS])


def _output_shape():
    def fwd():
        inp = _fwd_setup_inputs(0)
        return _fwd_reference(*[inp[k] for k in FWD_PARAMS])
    out = _jax.eval_shape(fwd)
    return out.shape, out.dtype

N_MICROBATCH = 1
ADAM_LR = 0.001
ADAM_B1 = 0.9
ADAM_B2 = 0.999
ADAM_EPS = 1e-08
ADAM_WD = 0.01
ADAM_STEP = 10
PER_EXAMPLE_BATCH_AXIS = {'x': 0, 'c': 0, 'positions': 0, 'loss_target': 0}
SHARED_INPUTS = []
_WEIGHT_DTYPES = {'w_ada': _jnp.float32, 'b_ada': _jnp.float32, 'norm_w': _jnp.float32, 'w_in': _jnp.float32, 'q_lora_norm': _jnp.float32, 'w_uq': _jnp.float32, 'kv_lora_norm': _jnp.float32, 'w_ukv': _jnp.float32, 'q_head_norm': _jnp.float32, 'k_head_norm': _jnp.float32, 'w_out': _jnp.float32}
MOMENT_SCALE = {'w_ada': 4.222984e-01, 'b_ada': 1.168115e+00, 'norm_w': 1.540952e+00, 'w_in': 7.604319e-02, 'q_lora_norm': 1.358432e-02, 'w_uq': 9.845253e-03, 'kv_lora_norm': 3.328896e-01, 'w_ukv': 3.702255e-02, 'q_head_norm': 6.283917e-02, 'k_head_norm': 6.231891e-02, 'w_out': 5.034992e-02}


def _to_microbatches(a, axis):
    t = _jnp.moveaxis(a, axis, 0)
    t = t.reshape((N_MICROBATCH, t.shape[0] // N_MICROBATCH) + t.shape[1:])
    return _jnp.moveaxis(t, 1, axis + 1)


def setup_inputs(seed: int = 0) -> dict:
    inp = _fwd_setup_inputs(seed)
    key = _jax.random.fold_in(_jax.random.key(seed), 7919)
    shape, _ = _output_shape()
    out = dict(inp)
    out["loss_target"] = _jax.random.normal(_jax.random.fold_in(key, 0), shape, _jnp.float32)
    for i, name in enumerate(TWIN_WEIGHTS):
        w = inp[name].astype(_jnp.float32)
        if MOMENT_SCALE is None:
            s = _jnp.sqrt(_jnp.mean(_jnp.square(w)) + 1e-30)
        else:
            s = MOMENT_SCALE[name]
        km, kv = _jax.random.split(_jax.random.fold_in(key, i + 1))
        out[name] = w
        out["m_" + name] = s * _jax.random.normal(km, w.shape, _jnp.float32)
        out["v_" + name] = (s * s) * _jax.random.uniform(kv, w.shape, _jnp.float32, 0.5, 1.5)
    if N_MICROBATCH > 1:
        for name, axis in PER_EXAMPLE_BATCH_AXIS.items():
            out[name] = _to_microbatches(out[name], axis)
    return {'x': out['x'], 'c': out['c'], 'positions': out['positions'], 'w_ada': out['w_ada'], 'b_ada': out['b_ada'], 'norm_w': out['norm_w'], 'w_in': out['w_in'], 'q_lora_norm': out['q_lora_norm'], 'w_uq': out['w_uq'], 'kv_lora_norm': out['kv_lora_norm'], 'w_ukv': out['w_ukv'], 'q_head_norm': out['q_head_norm'], 'k_head_norm': out['k_head_norm'], 'w_out': out['w_out'], 'loss_target': out['loss_target'], 'm_w_ada': out['m_w_ada'], 'm_b_ada': out['m_b_ada'], 'm_norm_w': out['m_norm_w'], 'm_w_in': out['m_w_in'], 'm_q_lora_norm': out['m_q_lora_norm'], 'm_w_uq': out['m_w_uq'], 'm_kv_lora_norm': out['m_kv_lora_norm'], 'm_w_ukv': out['m_w_ukv'], 'm_q_head_norm': out['m_q_head_norm'], 'm_k_head_norm': out['m_k_head_norm'], 'm_w_out': out['m_w_out'], 'v_w_ada': out['v_w_ada'], 'v_b_ada': out['v_b_ada'], 'v_norm_w': out['v_norm_w'], 'v_w_in': out['v_w_in'], 'v_q_lora_norm': out['v_q_lora_norm'], 'v_w_uq': out['v_w_uq'], 'v_kv_lora_norm': out['v_kv_lora_norm'], 'v_w_ukv': out['v_w_ukv'], 'v_q_head_norm': out['v_q_head_norm'], 'v_k_head_norm': out['v_k_head_norm'], 'v_w_out': out['v_w_out']}


def _loss(weights, diff, rest, loss_target):
    with _jax.named_scope("forward"):
        args = {**rest, TWIN_DIFF_INPUT: diff, **{k: w.astype(_WEIGHT_DTYPES[k]) for k, w in weights.items()}}
        y = _forward(args)
    with _jax.named_scope("loss_head"):
        err = _jnp.square(y.astype(_jnp.float32) - loss_target)
        return 0.5 * _jnp.sum(_jnp.mean(err, axis=-1)) if err.ndim else 0.5 * err


def _adamw(w, g, m, v):
    m = ADAM_B1 * m + (1.0 - ADAM_B1) * g
    v = ADAM_B2 * v + (1.0 - ADAM_B2) * _jnp.square(g)
    m_hat = m / (1.0 - ADAM_B1 ** ADAM_STEP)
    v_hat = v / (1.0 - ADAM_B2 ** ADAM_STEP)
    delta = -ADAM_LR * (m_hat / (_jnp.sqrt(v_hat) + ADAM_EPS) + ADAM_WD * w)
    return delta, m, v


def reference(x, c, positions, w_ada, b_ada, norm_w, w_in, q_lora_norm, w_uq, kv_lora_norm, w_ukv, q_head_norm, k_head_norm, w_out, loss_target, m_w_ada, m_b_ada, m_norm_w, m_w_in, m_q_lora_norm, m_w_uq, m_kv_lora_norm, m_w_ukv, m_q_head_norm, m_k_head_norm, m_w_out, v_w_ada, v_b_ada, v_norm_w, v_w_in, v_q_lora_norm, v_w_uq, v_kv_lora_norm, v_w_ukv, v_q_head_norm, v_k_head_norm, v_w_out):
    given = dict(x=x, c=c, positions=positions, w_ada=w_ada, b_ada=b_ada, norm_w=norm_w, w_in=w_in, q_lora_norm=q_lora_norm, w_uq=w_uq, kv_lora_norm=kv_lora_norm, w_ukv=w_ukv, q_head_norm=q_head_norm, k_head_norm=k_head_norm, w_out=w_out, loss_target=loss_target, m_w_ada=m_w_ada, m_b_ada=m_b_ada, m_norm_w=m_norm_w, m_w_in=m_w_in, m_q_lora_norm=m_q_lora_norm, m_w_uq=m_w_uq, m_kv_lora_norm=m_kv_lora_norm, m_w_ukv=m_w_ukv, m_q_head_norm=m_q_head_norm, m_k_head_norm=m_k_head_norm, m_w_out=m_w_out, v_w_ada=v_w_ada, v_b_ada=v_b_ada, v_norm_w=v_norm_w, v_w_in=v_w_in, v_q_lora_norm=v_q_lora_norm, v_w_uq=v_w_uq, v_kv_lora_norm=v_kv_lora_norm, v_w_ukv=v_w_ukv, v_q_head_norm=v_q_head_norm, v_k_head_norm=v_k_head_norm, v_w_out=v_w_out)
    weights = {n: given[n] for n in TWIN_WEIGHTS}
    shared = {n: given[n] for n in SHARED_INPUTS}
    per_example = {n: given[n] for n in ['x', 'c', 'positions']}
    grad_fn = _jax.value_and_grad(_loss, argnums=(0, 1))

    def one_microbatch(ex, loss_target):
        ex = dict(ex)
        diff = ex.pop(TWIN_DIFF_INPUT)
        return grad_fn(weights, diff, {**shared, **ex}, loss_target)

    if N_MICROBATCH == 1:
        loss, (grad_w, grad_x) = one_microbatch(per_example, given["loss_target"])
    else:
        def body(carry, xs):
            loss_sum, grad_sum = carry
            l_k, (gw_k, gx_k) = one_microbatch(xs[0], xs[1])
            with _jax.named_scope("update"):
                return (loss_sum + l_k, _jax.tree.map(_jnp.add, grad_sum, gw_k)), gx_k

        init = (_jnp.zeros((), _jnp.float32), _jax.tree.map(_jnp.zeros_like, weights))
        (loss, grad_w), grad_x = _jax.lax.scan(body, init, (per_example, given["loss_target"]))
    with _jax.named_scope("update"):
        delta_w, new_m, new_v = {}, {}, {}
        for n in TWIN_WEIGHTS:
            delta_w[n], new_m[n], new_v[n] = _adamw(weights[n], grad_w[n], given["m_" + n], given["v_" + n])
    return (loss, grad_x, *[grad_w[n] for n in TWIN_WEIGHTS], *[delta_w[n] for n in TWIN_WEIGHTS],
            *[new_m[n] for n in TWIN_WEIGHTS], *[new_v[n] for n in TWIN_WEIGHTS])
```

```python
import functools
import math

import jax
import jax.numpy as jnp
from jax import lax
from jax.experimental import pallas as pl
from jax.experimental.pallas import tpu as pltpu

F32 = jnp.float32
BF16 = jnp.bfloat16

N_DEV = 8
D_MODEL = 1024
HEADS = 8
SB_WIDTH = 512
MLA_WIDTH = 512
Q_LORA = 384
KV_LORA = 256
ROPE = 32
NOPE = 64
MLA_QK = 96
LANES = 128
IN_COLS = 3232
IN_COLS_R = 3328
EPS = 1e-6
NEG = -1e30

ADAM_LR = 0.001
ADAM_B1 = 0.9
ADAM_B2 = 0.999
ADAM_EPS = 1e-08
ADAM_WD = 0.01
ADAM_STEP = 10

TS = 256
TQ = 256
TN_S = 512
VMEM_BIG = 56 * 1024 * 1024

PK_NORM, PK_QLN, PK_KVLN, PK_QHN, PK_KHN, PK_ADA, PK_END = 0, 1024, 1408, 1664, 1792, 1920, 4992

MESH_ID = pl.DeviceIdType.MESH


def _dot_nn(a, b):
    return lax.dot_general(a, b, (((1,), (0,)), ((), ())), preferred_element_type=F32)


def _dot_nt(a, b):
    return lax.dot_general(a, b, (((1,), (1,)), ((), ())), preferred_element_type=F32)


def _dot_tn(a, b):
    return lax.dot_general(a, b, (((0,), (0,)), ((), ())), preferred_element_type=F32)


def _split_bf16(a):
    hi = a.astype(BF16)
    lo = (a - hi.astype(F32)).astype(BF16)
    return hi, lo


def _dot3(a, b):
    ah, al = _split_bf16(a)
    bh, bl = _split_bf16(b)
    return _dot_nn(ah, bh) + _dot_nn(ah, bl) + _dot_nn(al, bh)


def _sigmoid(g):
    return 1.0 / (1.0 + jnp.exp(-g))


def _silu(g):
    return g * _sigmoid(g)


def _lane_iota(shape):
    return lax.broadcasted_iota(jnp.int32, shape, len(shape) - 1)


def _adamw(w, g, m, v):
    m = ADAM_B1 * m + (1.0 - ADAM_B1) * g
    v = ADAM_B2 * v + (1.0 - ADAM_B2) * (g * g)
    m_hat = m / (1.0 - ADAM_B1 ** ADAM_STEP)
    v_hat = v / (1.0 - ADAM_B2 ** ADAM_STEP)
    delta = -ADAM_LR * (m_hat / (jnp.sqrt(v_hat) + ADAM_EPS) + ADAM_WD * w)
    return delta, m, v


def _mesh_pos():
    return lax.axis_index("x"), lax.axis_index("y"), lax.axis_index("c")


def _peer(pos, k):
    x, y, c = pos
    return (1 - x if k & 4 else x, 1 - y if k & 2 else y, 1 - c if k & 1 else c)


def _lin(pos):
    return 4 * pos[0] + 2 * pos[1] + pos[2]


def _remote(src, dst, send_sems, recv_sems, idx, peer):
    return pltpu.make_async_remote_copy(
        src_ref=src, dst_ref=dst, send_sem=send_sems.at[idx], recv_sem=recv_sems.at[idx],
        device_id=peer, device_id_type=MESH_ID)


def _all_gather(pos, src, buf, send_sems, recv_sems, base):
    me = _lin(pos)
    sent = []
    for k in range(1, N_DEV):
        cp = _remote(src, buf.at[me], send_sems, recv_sems, base + k - 1, _peer(pos, k))
        cp.start()
        sent.append(cp)
    for k in range(1, N_DEV):
        peer = _peer(pos, k)
        _remote(src, buf.at[_lin(peer)], send_sems, recv_sems, base + k - 1, peer).wait_recv()
    for cp in sent:
        cp.wait_send()


def _all_to_all(pos, src, buf, send_sems, recv_sems, base):
    me = _lin(pos)
    sent = []
    for k in range(1, N_DEV):
        peer = _peer(pos, k)
        cp = _remote(src.at[_lin(peer)], buf.at[me], send_sems, recv_sems, base + k - 1, peer)
        cp.start()
        sent.append(cp)
    for k in range(1, N_DEV):
        peer = _peer(pos, k)
        _remote(src.at[me], buf.at[_lin(peer)], send_sems, recv_sems, base + k - 1, peer).wait_recv()
    for cp in sent:
        cp.wait_send()


def _vmem_spec():
    return pl.BlockSpec(memory_space=pltpu.VMEM)


def _any_spec():
    return pl.BlockSpec(memory_space=pl.ANY)


def _row_select(slots, n):
    r = lax.broadcasted_iota(jnp.int32, (N_DEV, n), 0)
    out = jnp.zeros((N_DEV, n), F32)
    for j in range(N_DEV):
        out = out + jnp.where(r == j, slots[j], 0.0)
    return out


def _ada_fwd(c8, w_ada, b_ada8):
    n_sh = w_ada.shape[1]

    def body(c_ref, w_ref, b_ref, out_ref, call_ref, psend_ref, precv_ref, ssem, rsem):
        pos = _mesh_pos()
        me = _lin(pos)
        call_ref[me] = c_ref[...]
        _all_gather(pos, c_ref, call_ref, ssem, rsem, 0)
        w = w_ref[...]
        for j in range(N_DEV):
            psend_ref[j] = _dot3(_silu(call_ref[j]), w)
        precv_ref[me] = psend_ref[me]
        _all_to_all(pos, psend_ref, precv_ref, ssem, rsem, N_DEV - 1)
        out_ref[...] = _row_select([precv_ref[j] for j in range(N_DEV)], n_sh) + b_ref[...]

    return pl.pallas_call(
        body, name="ada_fwd",
        out_shape=jax.ShapeDtypeStruct((N_DEV, n_sh), F32),
        in_specs=[_vmem_spec()] * 3, out_specs=_vmem_spec(),
        scratch_shapes=[
            pltpu.VMEM((N_DEV, 8, D_MODEL), F32),
            pltpu.VMEM((N_DEV, 8, n_sh), F32),
            pltpu.VMEM((N_DEV, 8, n_sh), F32),
            pltpu.SemaphoreType.DMA((2 * (N_DEV - 1),)),
            pltpu.SemaphoreType.DMA((2 * (N_DEV - 1),)),
        ],
    )(c8, w_ada, b_ada8)


def _gather_weights(shards):
    n = len(shards)

    def body(*refs):
        ins, outs = refs[:n], refs[n:2 * n]
        ssem, rsem = refs[2 * n], refs[2 * n + 1]
        pos = _mesh_pos()
        me = _lin(pos)
        for a in range(n):
            outs[a][me] = ins[a][...].astype(BF16)
        for a in range(n):
            _all_gather(pos, outs[a].at[me], outs[a], ssem, rsem, a * (N_DEV - 1))

    return pl.pallas_call(
        body, name="gather_weights",
        out_shape=[jax.ShapeDtypeStruct((N_DEV,) + s.shape, BF16) for s in shards],
        in_specs=[_vmem_spec()] * n, out_specs=[_vmem_spec()] * n,
        scratch_shapes=[
            pltpu.SemaphoreType.DMA((n * (N_DEV - 1),)),
            pltpu.SemaphoreType.DMA((n * (N_DEV - 1),)),
        ],
        compiler_params=pltpu.CompilerParams(vmem_limit_bytes=VMEM_BIG),
    )(*shards)


def _rope(t, cosf, sin_a, sin_b):
    return t * cosf + pltpu.roll(t, 112, 1) * sin_a + pltpu.roll(t, 16, 1) * sin_b


def _rope_t(d, cosf, sin_a, sin_b):
    return d * cosf + pltpu.roll(d * sin_a, 16, 1) + pltpu.roll(d * sin_b, 112, 1)


def _head_rms(t):
    return lax.rsqrt(jnp.sum(t * t, axis=1, keepdims=True) * (1.0 / MLA_QK) + EPS)


def _rms_bwd(dxhat_w, xhat, r, n):
    return r * (dxhat_w - xhat * (jnp.sum(dxhat_w * xhat, axis=1, keepdims=True) * (1.0 / n)))


def _mla_latents(cq, ckv, kr, qln, kvln, wuq, wukv):
    rq = lax.rsqrt(jnp.mean(cq * cq, axis=1, keepdims=True) + EPS)
    rkv = lax.rsqrt(jnp.mean(ckv * ckv, axis=1, keepdims=True) + EPS)
    cq_hat = cq * rq
    ckv_hat = ckv * rkv
    cqn = (cq_hat * qln).astype(BF16)
    ckvn = (ckv_hat * kvln).astype(BF16)
    q_all = _dot_nn(cqn, wuq)
    kv = _dot_nn(ckvn, wukv)
    kr64 = pltpu.roll(kr, 64, 1)
    return rq, rkv, cq_hat, ckv_hat, cqn, ckvn, q_all, kv, kr64


def _fwd_pre(x, shift, scale, norm_w, w_in_r, qln, wuq, kvln, wukv, qhn, khn, cosf, sin_a, sin_b):
    s_len = x.shape[0]

    def body(x_ref, shift_ref, scale_ref, nw_ref, w_ref, qln_ref, wuq_ref, kvln_ref, wukv_ref,
             qhn_ref, khn_ref, cos_ref, sa_ref, sb_ref,
             hb_ref, qkv_ref, gsb_ref, cq_ref, ckv_ref, gmla_ref, kr_ref, qm_ref, km_ref, vm_ref):
        xv = x_ref[...]
        r = lax.rsqrt(jnp.mean(xv * xv, axis=1, keepdims=True) + EPS)
        h = (xv * r) * nw_ref[...] * (1.0 + scale_ref[...]) + shift_ref[...]
        hb = h.astype(BF16)
        hb_ref[...] = hb
        qkv_ref[...] = _dot_nn(hb, w_ref[:, 0:1536]).astype(BF16)
        gsb_ref[...] = _dot_nn(hb, w_ref[:, 1536:2048])
        cq = _dot_nn(hb, w_ref[:, 2048:2432])
        ckv = _dot_nn(hb, w_ref[:, 2432:2688])
        gmla_ref[...] = _dot_nn(hb, w_ref[:, 2688:3200])
        kr = _dot_nn(hb, w_ref[:, 3200:3328])
        cq_ref[...] = cq
        ckv_ref[...] = ckv
        kr_ref[...] = kr
        _, _, _, _, _, _, q_all, kv, kr64 = _mla_latents(
            cq, ckv, kr, qln_ref[...], kvln_ref[...], wuq_ref[...], wukv_ref[...])
        cosf, sa, sb = cos_ref[...], sa_ref[...], sb_ref[...]
        qhn_v, khn_v = qhn_ref[...], khn_ref[...]
        low = _lane_iota((TS, LANES)) < NOPE
        for hd in range(HEADS):
            blk = slice(LANES * hd, LANES * (hd + 1))
            qb = q_all[:, blk]
            qm_ref[:, blk] = _rope(qb * _head_rms(qb) * qhn_v, cosf, sa, sb).astype(BF16)
            kb = jnp.where(low, kv[:, blk], kr64)
            km_ref[:, blk] = _rope(kb * _head_rms(kb) * khn_v, cosf, sa, sb).astype(BF16)
        for p in range(HEADS // 2):
            even = kv[:, LANES * 2 * p:LANES * (2 * p + 1)]
            odd = kv[:, LANES * (2 * p + 1):LANES * (2 * p + 2)]
            vm_ref[:, LANES * p:LANES * (p + 1)] = jnp.where(low, pltpu.roll(even, 64, 1), odd).astype(BF16)

    def tile(width):
        return pl.BlockSpec((TS, width), lambda i: (i, 0))

    def full(a):
        return pl.BlockSpec(a.shape, lambda i: (0, 0))

    out_widths = [(D_MODEL, BF16), (1536, BF16), (512, F32), (Q_LORA, F32), (KV_LORA, F32),
                  (512, F32), (LANES, F32), (1024, BF16), (1024, BF16), (512, BF16)]
    return pl.pallas_call(
        body, name="fwd_pre", grid=(s_len // TS,),
        out_shape=[jax.ShapeDtypeStruct((s_len, w), dt) for w, dt in out_widths],
        in_specs=[tile(D_MODEL), full(shift), full(scale), full(norm_w), full(w_in_r), full(qln), full(wuq),
                  full(kvln), full(wukv), full(qhn), full(khn), tile(LANES), tile(LANES), tile(LANES)],
        out_specs=[tile(w) for w, _ in out_widths],
        compiler_params=pltpu.CompilerParams(dimension_semantics=("parallel",), vmem_limit_bytes=VMEM_BIG),
    )(x, shift, scale, norm_w, w_in_r, qln, wuq, kvln, wukv, qhn, khn, cosf, sin_a, sin_b)


def _tri(strict):
    j = lax.broadcasted_iota(jnp.int32, (TQ, TQ), 0)
    s = lax.broadcasted_iota(jnp.int32, (TQ, TQ), 1)
    return (j > s if strict else j >= s).astype(BF16)


def _cum(a, tri_m):
    hi, lo = _split_bf16(a)
    return _dot_nn(hi, tri_m) + _dot_nn(lo, tri_m)


def _sb_weights(qm, kb, carry, tri_u, diag):
    z = _dot_nt(qm, kb) * 0.125
    lk = -(jnp.maximum(z, 0.0) + jnp.log(1.0 + jnp.exp(-jnp.abs(z))))
    if diag:
        t = lax.broadcasted_iota(jnp.int32, (TQ, TQ), 0)
        s = lax.broadcasted_iota(jnp.int32, (TQ, TQ), 1)
        valid = s < t
        lk = jnp.where(valid, lk, 0.0)
    logw = z + lk + _cum(lk, tri_u) + carry
    if diag:
        logw = jnp.where(valid, logw, NEG)
    return z, lk, jnp.exp(logw)


def _head_masks():
    lane = _lane_iota((1, LANES))
    return [lane < 64, lane >= 64]


def _key_rows(kj):
    return pl.ds(pl.multiple_of(kj * TQ, TQ), TQ)


def _sb_fwd(qkv):
    s_len = qkv.shape[0]

    def body(q_ref, k_ref, v_ref, o_ref):
        qi = pl.program_id(1)
        q = q_ref[...]
        tri_u = _tri(True)
        out = jnp.zeros((TQ, LANES), F32)
        for hm in _head_masks():
            qm = jnp.where(hm, q, jnp.zeros_like(q))

            def block(kj, carry, acc, diag):
                rows = _key_rows(kj)
                _, lk, w = _sb_weights(qm, k_ref[rows, :], carry, tri_u, diag)
                vm = jnp.where(hm, v_ref[rows, :], jnp.zeros((TQ, LANES), BF16))
                return carry + jnp.sum(lk, axis=1, keepdims=True), acc + _dot_nn(w.astype(BF16), vm)

            carry, acc = block(qi, jnp.zeros((TQ, 1), F32), jnp.zeros((TQ, LANES), F32), True)
            carry, acc = lax.fori_loop(
                1, qi + 1, lambda i, st: block(qi - i, st[0], st[1], False), (carry, acc))
            out = out + acc
        o_ref[...] = out

    return pl.pallas_call(
        body, name="sb_fwd", grid=(HEADS // 2, s_len // TQ),
        out_shape=jax.ShapeDtypeStruct((s_len, SB_WIDTH), F32),
        in_specs=[pl.BlockSpec((TQ, LANES), lambda p, i: (i, p)),
                  pl.BlockSpec((s_len, LANES), lambda p, i: (0, 4 + p)),
                  pl.BlockSpec((s_len, LANES), lambda p, i: (0, 8 + p))],
        out_specs=pl.BlockSpec((TQ, LANES), lambda p, i: (i, p)),
        compiler_params=pltpu.CompilerParams(dimension_semantics=("parallel", "parallel"),
                                             vmem_limit_bytes=VMEM_BIG),
    )(qkv, qkv, qkv)


def _sb_bwd(qkv, o, do):
    s_len = qkv.shape[0]

    def body(q_ref, k_ref, v_ref, o_ref, do_ref, dq_ref, dk_ref, dv_ref):
        qi = pl.program_id(1)

        @pl.when(qi == 0)
        def _():
            dk_ref[...] = jnp.zeros_like(dk_ref)
            dv_ref[...] = jnp.zeros_like(dv_ref)

        q = q_ref[...]
        do_v = do_ref[...]
        od = o_ref[...] * do_v.astype(F32)
        tri_u = _tri(True)
        tri_i = _tri(False)
        dq_out = jnp.zeros((TQ, LANES), F32)
        for hm in _head_masks():
            qm = jnp.where(hm, q, jnp.zeros_like(q))
            dom = jnp.where(hm, do_v, jnp.zeros_like(do_v))
            d_tot = jnp.sum(jnp.where(hm, od, 0.0), axis=1, keepdims=True)

            def block(kj, carry, carry_d, dq, diag):
                rows = _key_rows(kj)
                kb = k_ref[rows, :]
                _, lk, w = _sb_weights(qm, kb, carry, tri_u, diag)
                wb = w.astype(BF16)
                d_l = _dot_nt(dom, v_ref[rows, :]) * wb.astype(F32)
                before = d_tot - (_cum(d_l, tri_i) + carry_d)
                keep = jnp.exp(lk)
                dz = d_l * keep - before * (1.0 - keep)
                if diag:
                    t = lax.broadcasted_iota(jnp.int32, (TQ, TQ), 0)
                    s = lax.broadcasted_iota(jnp.int32, (TQ, TQ), 1)
                    dz = jnp.where(s < t, dz, 0.0)
                dzb = (dz * 0.125).astype(BF16)
                dk_ref[rows, :] += _dot_tn(dzb, qm)
                dv_ref[rows, :] += _dot_tn(wb, dom)
                return (carry + jnp.sum(lk, axis=1, keepdims=True),
                        carry_d + jnp.sum(d_l, axis=1, keepdims=True),
                        dq + _dot_nn(dzb, kb))

            zc = jnp.zeros((TQ, 1), F32)
            st = block(qi, zc, zc, jnp.zeros((TQ, LANES), F32), True)
            st = lax.fori_loop(1, qi + 1, lambda i, s_: block(qi - i, s_[0], s_[1], s_[2], False), st)
            dq_out = dq_out + jnp.where(hm, st[2], 0.0)
        dq_ref[...] = dq_out

    tile = pl.BlockSpec((TQ, LANES), lambda p, i: (i, p))
    col = pl.BlockSpec((s_len, LANES), lambda p, i: (0, p))
    return pl.pallas_call(
        body, name="sb_bwd", grid=(HEADS // 2, s_len // TQ),
        out_shape=[jax.ShapeDtypeStruct((s_len, SB_WIDTH), F32)] * 3,
        in_specs=[tile,
                  pl.BlockSpec((s_len, LANES), lambda p, i: (0, 4 + p)),
                  pl.BlockSpec((s_len, LANES), lambda p, i: (0, 8 + p)),
                  tile, tile],
        out_specs=[tile, col, col],
        compiler_params=pltpu.CompilerParams(dimension_semantics=("parallel", "arbitrary"),
                                             vmem_limit_bytes=VMEM_BIG),
    )(qkv, qkv, qkv, o, do)


MLA_SCALE = 1.0 / math.sqrt(MLA_QK)


def _causal(diag):
    if not diag:
        return None
    t = lax.broadcasted_iota(jnp.int32, (TQ, TQ), 0)
    s = lax.broadcasted_iota(jnp.int32, (TQ, TQ), 1)
    return s <= t


def _mla_fwd(qm, km, vm):
    s_len = qm.shape[0]

    def body(q_ref, k_ref, v_ref, o_ref, lse_ref):
        qi = pl.program_id(1)
        out = jnp.zeros((TQ, LANES), F32)
        for hh, hm in enumerate(_head_masks()):
            lanes = slice(LANES * hh, LANES * (hh + 1))
            qh = q_ref[:, lanes]

            def block(kj, m, l, acc, diag):
                rows = _key_rows(kj)
                s = _dot_nt(qh, k_ref[rows, lanes]) * MLA_SCALE
                if diag:
                    s = jnp.where(_causal(True), s, NEG)
                m_new = jnp.maximum(m, jnp.max(s, axis=1, keepdims=True))
                p = jnp.exp(s - m_new)
                a = jnp.exp(m - m_new)
                vb = jnp.where(hm, v_ref[rows, :], jnp.zeros((TQ, LANES), BF16))
                return (m_new, a * l + jnp.sum(p, axis=1, keepdims=True),
                        a * acc + _dot_nn(p.astype(BF16), vb))

            st = (jnp.full((TQ, 1), NEG, F32), jnp.zeros((TQ, 1), F32), jnp.zeros((TQ, LANES), F32))
            st = lax.fori_loop(0, qi, lambda kj, s_: block(kj, s_[0], s_[1], s_[2], False), st)
            m, l, acc = block(qi, st[0], st[1], st[2], True)
            out = out + acc / l
            lse_ref[hh] = m + jnp.log(l)
        o_ref[...] = out

    return pl.pallas_call(
        body, name="mla_fwd", grid=(HEADS // 2, s_len // TQ),
        out_shape=[jax.ShapeDtypeStruct((s_len, MLA_WIDTH), F32),
                   jax.ShapeDtypeStruct((HEADS, s_len, 1), F32)],
        in_specs=[pl.BlockSpec((TQ, 2 * LANES), lambda p, i: (i, p)),
                  pl.BlockSpec((s_len, 2 * LANES), lambda p, i: (0, p)),
                  pl.BlockSpec((s_len, LANES), lambda p, i: (0, p))],
        out_specs=[pl.BlockSpec((TQ, LANES), lambda p, i: (i, p)),
                   pl.BlockSpec((2, TQ, 1), lambda p, i: (p, i, 0))],
        compiler_params=pltpu.CompilerParams(dimension_semantics=("parallel", "parallel"),
                                             vmem_limit_bytes=VMEM_BIG),
    )(qm, km, vm)


def _mla_bwd(qm, km, vm, o, do, lse):
    s_len = qm.shape[0]

    def body(q_ref, k_ref, v_ref, o_ref, do_ref, lse_ref, dq_ref, dk_ref, dv_ref):
        qi = pl.program_id(1)

        @pl.when(qi == 0)
        def _():
            dk_ref[...] = jnp.zeros_like(dk_ref)
            dv_ref[...] = jnp.zeros_like(dv_ref)

        do_v = do_ref[...]
        od = o_ref[...] * do_v.astype(F32)
        for hh, hm in enumerate(_head_masks()):
            lanes = slice(LANES * hh, LANES * (hh + 1))
            qh = q_ref[:, lanes]
            dom = jnp.where(hm, do_v, jnp.zeros_like(do_v))
            delta = jnp.sum(jnp.where(hm, od, 0.0), axis=1, keepdims=True)
            lse_h = lse_ref[hh]

            def block(kj, dq, diag):
                rows = _key_rows(kj)
                kb = k_ref[rows, lanes]
                s = _dot_nt(qh, kb) * MLA_SCALE
                if diag:
                    s = jnp.where(_causal(True), s, NEG)
                p = jnp.exp(s - lse_h)
                dp = _dot_nt(dom, v_ref[rows, :])
                ds = (p * (dp - delta) * MLA_SCALE).astype(BF16)
                dk_ref[rows, lanes] += _dot_tn(ds, qh)
                dv_ref[rows, :] += _dot_tn(p.astype(BF16), dom)
                return dq + _dot_nn(ds, kb)

            dq = lax.fori_loop(0, qi, lambda kj, a: block(kj, a, False), jnp.zeros((TQ, LANES), F32))
            dq_ref[:, lanes] = block(qi, dq, True)

    return pl.pallas_call(
        body, name="mla_bwd", grid=(HEADS // 2, s_len // TQ),
        out_shape=[jax.ShapeDtypeStruct((s_len, HEADS * LANES), F32),
                   jax.ShapeDtypeStruct((s_len, HEADS * LANES), F32),
                   jax.ShapeDtypeStruct((s_len, MLA_WIDTH), F32)],
        in_specs=[pl.BlockSpec((TQ, 2 * LANES), lambda p, i: (i, p)),
                  pl.BlockSpec((s_len, 2 * LANES), lambda p, i: (0, p)),
                  pl.BlockSpec((s_len, LANES), lambda p, i: (0, p)),
                  pl.BlockSpec((TQ, LANES), lambda p, i: (i, p)),
                  pl.BlockSpec((TQ, LANES), lambda p, i: (i, p)),
                  pl.BlockSpec((2, TQ, 1), lambda p, i: (p, i, 0))],
        out_specs=[pl.BlockSpec((TQ, 2 * LANES), lambda p, i: (i, p)),
                   pl.BlockSpec((s_len, 2 * LANES), lambda p, i: (0, p)),
                   pl.BlockSpec((s_len, LANES), lambda p, i: (0, p))],
        compiler_params=pltpu.CompilerParams(dimension_semantics=("parallel", "arbitrary"),
                                             vmem_limit_bytes=VMEM_BIG),
    )(qm, km, vm, o, do, lse)


def _mid(o_sb, g_sb, o_mla, g_mla, x, target, gate, w_out):
    s_len = x.shape[0]

    def body(osb_ref, gsb_ref, omla_ref, gmla_ref, x_ref, t_ref, gate_ref, w_ref,
             dy_ref, dosb_ref, dgsb_ref, domla_ref, dgmla_ref, gw_ref, dgate_ref, loss_ref):
        @pl.when(pl.program_id(0) == 0)
        def _():
            gw_ref[...] = jnp.zeros_like(gw_ref)
            dgate_ref[...] = jnp.zeros_like(dgate_ref)
            loss_ref[...] = jnp.zeros_like(loss_ref)

        o1, g1, o2, g2 = osb_ref[...], gsb_ref[...], omla_ref[...], gmla_ref[...]
        s1, s2 = _sigmoid(g1), _sigmoid(g2)
        mixed = jnp.concatenate([o1 * (g1 * s1), o2 * (g2 * s2)], axis=1).astype(BF16)
        w = w_ref[...]
        gate_v = gate_ref[...]
        u = _dot_nn(mixed, w)
        err = x_ref[...] + gate_v * u - t_ref[...]
        loss_ref[...] += jnp.sum(err * err, axis=0, keepdims=True)
        dy = err * (1.0 / D_MODEL)
        dy_ref[...] = dy
        dgate_ref[...] += jnp.sum(dy * u, axis=0, keepdims=True)
        du = (dy * gate_v).astype(BF16)
        gw_ref[...] += _dot_tn(mixed, du)
        dmixed = _dot_nt(du, w)
        d1, d2 = dmixed[:, :SB_WIDTH], dmixed[:, SB_WIDTH:]
        dosb_ref[...] = (d1 * (g1 * s1)).astype(BF16)
        dgsb_ref[...] = (d1 * o1 * (s1 * (1.0 + g1 * (1.0 - s1)))).astype(BF16)
        domla_ref[...] = (d2 * (g2 * s2)).astype(BF16)
        dgmla_ref[...] = (d2 * o2 * (s2 * (1.0 + g2 * (1.0 - s2)))).astype(BF16)

    def tile(width):
        return pl.BlockSpec((TS, width), lambda i: (i, 0))

    def full(shape):
        return pl.BlockSpec(shape, lambda i: (0, 0))

    return pl.pallas_call(
        body, name="mid", grid=(s_len // TS,),
        out_shape=[jax.ShapeDtypeStruct((s_len, D_MODEL), F32)]
        + [jax.ShapeDtypeStruct((s_len, 512), BF16)] * 4
        + [jax.ShapeDtypeStruct((D_MODEL, D_MODEL), F32),
           jax.ShapeDtypeStruct((1, D_MODEL), F32), jax.ShapeDtypeStruct((1, D_MODEL), F32)],
        in_specs=[tile(512)] * 4 + [tile(D_MODEL), tile(D_MODEL), full((1, D_MODEL)), full((D_MODEL, D_MODEL))],
        out_specs=[tile(D_MODEL)] + [tile(512)] * 4
        + [full((D_MODEL, D_MODEL)), full((1, D_MODEL)), full((1, D_MODEL))],
        compiler_params=pltpu.CompilerParams(dimension_semantics=("arbitrary",), vmem_limit_bytes=VMEM_BIG),
    )(o_sb, g_sb, o_mla, g_mla, x, target, gate, w_out)


def _mla_pre_bwd(dq, dk, dv, cq, ckv, kr, qln, wuq, kvln, wukv, qhn, khn, cosf, sin_a, sin_b):
    s_len = cq.shape[0]

    def body(dq_ref, dk_ref, dv_ref, cq_ref, ckv_ref, kr_ref, qln_ref, wuq_ref, kvln_ref, wukv_ref,
             qhn_ref, khn_ref, cos_ref, sa_ref, sb_ref,
             dcq_ref, dckv_ref, dkr_ref, gwuq_ref, gwukv_ref, gqhn_ref, gkhn_ref, gqln_ref, gkvln_ref,
             dqa_ref, dkv_ref):
        @pl.when(pl.program_id(0) == 0)
        def _():
            for r_ in (gwuq_ref, gwukv_ref, gqhn_ref, gkhn_ref, gqln_ref, gkvln_ref):
                r_[...] = jnp.zeros_like(r_)

        cq, ckv = cq_ref[...], ckv_ref[...]
        qln_v, kvln_v = qln_ref[...], kvln_ref[...]
        wuq_v, wukv_v = wuq_ref[...], wukv_ref[...]
        rq, rkv, cq_hat, ckv_hat, cqn, ckvn, q_all, kv, kr64 = _mla_latents(
            cq, ckv, kr_ref[...], qln_v, kvln_v, wuq_v, wukv_v)
        cosf, sa, sb = cos_ref[...], sa_ref[...], sb_ref[...]
        qhn_v, khn_v = qhn_ref[...], khn_ref[...]
        lane = _lane_iota((TS, LANES))
        low = lane < NOPE
        g_qhn = jnp.zeros((1, LANES), F32)
        g_khn = jnp.zeros((1, LANES), F32)
        dkr64 = jnp.zeros((TS, LANES), F32)
        for hd in range(HEADS):
            blk = slice(LANES * hd, LANES * (hd + 1))
            qb = q_all[:, blk]
            r = _head_rms(qb)
            xh = qb * r
            dn = _rope_t(dq_ref[:, blk], cosf, sa, sb)
            g_qhn = g_qhn + jnp.sum(dn * xh, axis=0, keepdims=True)
            dqa_ref[:, blk] = _rms_bwd(dn * qhn_v, xh, r, MLA_QK).astype(BF16)

            kb = jnp.where(low, kv[:, blk], kr64)
            r = _head_rms(kb)
            xh = kb * r
            dn = _rope_t(dk_ref[:, blk], cosf, sa, sb)
            g_khn = g_khn + jnp.sum(dn * xh, axis=0, keepdims=True)
            dkb = _rms_bwd(dn * khn_v, xh, r, MLA_QK)
            dkr64 = dkr64 + jnp.where(low, 0.0, dkb)
            dvp = dv_ref[:, LANES * (hd // 2):LANES * (hd // 2 + 1)]
            dvh = pltpu.roll(dvp, 64, 1) if hd % 2 == 0 else dvp
            dkv_ref[:, blk] = jnp.where(low, dkb, dvh).astype(BF16)
        gqhn_ref[...] += g_qhn
        gkhn_ref[...] += g_khn
        dkr_ref[...] = pltpu.roll(dkr64, 64, 1).astype(BF16)

        dqa = dqa_ref[...]
        gwuq_ref[...] += _dot_tn(cqn, dqa)
        dcqn = _dot_nt(dqa, wuq_v)
        gqln_ref[...] += jnp.sum(dcqn * cq_hat, axis=0, keepdims=True)
        dcq_ref[...] = _rms_bwd(dcqn * qln_v, cq_hat, rq, Q_LORA).astype(BF16)

        dkv = dkv_ref[...]
        gwukv_ref[...] += _dot_tn(ckvn, dkv)
        dckvn = _dot_nt(dkv, wukv_v)
        gkvln_ref[...] += jnp.sum(dckvn * ckv_hat, axis=0, keepdims=True)
        dckv_ref[...] = _rms_bwd(dckvn * kvln_v, ckv_hat, rkv, KV_LORA).astype(BF16)

    def tile(width):
        return pl.BlockSpec((TS, width), lambda i: (i, 0))

    def full(shape):
        return pl.BlockSpec(shape, lambda i: (0, 0))

    acc_shapes = [(Q_LORA, 1024), (KV_LORA, 1024), (1, LANES), (1, LANES), (1, Q_LORA), (1, KV_LORA)]
    return pl.pallas_call(
        body, name="mla_pre_bwd", grid=(s_len // TS,),
        out_shape=[jax.ShapeDtypeStruct((s_len, Q_LORA), BF16), jax.ShapeDtypeStruct((s_len, KV_LORA), BF16),
                   jax.ShapeDtypeStruct((s_len, LANES), BF16)]
        + [jax.ShapeDtypeStruct(s, F32) for s in acc_shapes],
        in_specs=[tile(1024), tile(1024), tile(512), tile(Q_LORA), tile(KV_LORA), tile(LANES),
                  full(qln.shape), full(wuq.shape), full(kvln.shape), full(wukv.shape),
                  full(qhn.shape), full(khn.shape), tile(LANES), tile(LANES), tile(LANES)],
        out_specs=[tile(Q_LORA), tile(KV_LORA), tile(LANES)] + [full(s) for s in acc_shapes],
        scratch_shapes=[pltpu.VMEM((TS, 1024), BF16), pltpu.VMEM((TS, 1024), BF16)],
        compiler_params=pltpu.CompilerParams(dimension_semantics=("arbitrary",), vmem_limit_bytes=VMEM_BIG),
    )(dq, dk, dv, cq, ckv, kr, qln, wuq, kvln, wukv, qhn, khn, cosf, sin_a, sin_b)


def _dproj_bwd(dq_sb, dk_sb, dv_sb, dg_sb, dcq, dckv, dg_mla, dkr, w_in_r, x, dy, norm_w, scale):
    s_len = x.shape[0]

    def body(dq_ref, dk_ref, dv_ref, dg_ref, dcq_ref, dckv_ref, dgm_ref, dkr_ref, w_ref, x_ref, dy_ref,
             nw_ref, scale_ref, dp_ref, gx_ref, dshift_ref, dscale_ref, dnw_ref):
        @pl.when(pl.program_id(0) == 0)
        def _():
            for r_ in (dshift_ref, dscale_ref, dnw_ref):
                r_[...] = jnp.zeros_like(r_)

        dp_ref[:, 0:512] = dq_ref[...].astype(BF16)
        dp_ref[:, 512:1024] = dk_ref[...].astype(BF16)
        dp_ref[:, 1024:1536] = dv_ref[...].astype(BF16)
        dp_ref[:, 1536:2048] = dg_ref[...]
        dp_ref[:, 2048:2432] = dcq_ref[...]
        dp_ref[:, 2432:2688] = dckv_ref[...]
        dp_ref[:, 2688:3200] = dgm_ref[...]
        dp_ref[:, 3200:3328] = dkr_ref[...]
        dh = _dot_nt(dp_ref[...], w_ref[...])
        xv = x_ref[...]
        r = lax.rsqrt(jnp.mean(xv * xv, axis=1, keepdims=True) + EPS)
        xh = xv * r
        nw = nw_ref[...]
        dshift_ref[...] += jnp.sum(dh, axis=0, keepdims=True)
        dscale_ref[...] += jnp.sum(dh * (xh * nw), axis=0, keepdims=True)
        dxnw = dh * (1.0 + scale_ref[...])
        dnw_ref[...] += jnp.sum(dxnw * xh, axis=0, keepdims=True)
        gx_ref[...] = dy_ref[...] + _rms_bwd(dxnw * nw, xh, r, D_MODEL)

    def tile(width):
        return pl.BlockSpec((TS, width), lambda i: (i, 0))

    def full(shape):
        return pl.BlockSpec(shape, lambda i: (0, 0))

    vec = (1, D_MODEL)
    return pl.pallas_call(
        body, name="dproj_bwd", grid=(s_len // TS,),
        out_shape=[jax.ShapeDtypeStruct((s_len, IN_COLS_R), BF16), jax.ShapeDtypeStruct((s_len, D_MODEL), F32)]
        + [jax.ShapeDtypeStruct(vec, F32)] * 3,
        in_specs=[tile(512)] * 4 + [tile(Q_LORA), tile(KV_LORA), tile(512), tile(LANES),
                                    full(w_in_r.shape), tile(D_MODEL), tile(D_MODEL), full(vec), full(vec)],
        out_specs=[tile(IN_COLS_R), tile(D_MODEL)] + [full(vec)] * 3,
        compiler_params=pltpu.CompilerParams(dimension_semantics=("arbitrary",), vmem_limit_bytes=VMEM_BIG),
    )(dq_sb, dk_sb, dv_sb, dg_sb, dcq, dckv, dg_mla, dkr, w_in_r, x, dy, norm_w, scale)


def _grad_w_in(hb, dproj):
    s_len = hb.shape[0]
    n_half = IN_COLS_R // 2

    def body(h_ref, d_ref, g_ref):
        @pl.when(pl.program_id(1) == 0)
        def _():
            g_ref[...] = jnp.zeros_like(g_ref)

        g_ref[...] += _dot_tn(h_ref[...], d_ref[...])

    return pl.pallas_call(
        body, name="grad_w_in", grid=(2, s_len // TN_S),
        out_shape=jax.ShapeDtypeStruct((D_MODEL, IN_COLS_R), F32),
        in_specs=[pl.BlockSpec((TN_S, D_MODEL), lambda n, s: (s, 0)),
                  pl.BlockSpec((TN_S, n_half), lambda n, s: (s, n))],
        out_specs=pl.BlockSpec((D_MODEL, n_half), lambda n, s: (0, n)),
        compiler_params=pltpu.CompilerParams(dimension_semantics=("parallel", "arbitrary"),
                                             vmem_limit_bytes=VMEM_BIG),
    )(hb, dproj)


def _vec_exchange(gpack, ccol, wpack, mpack, vpack, w_ada, m_ada, v_ada):
    n_sh = w_ada.shape[1]

    def body(g_ref, cc_ref, wp_ref, mp_ref, vp_ref, wa_ref, ma_ref, va_ref,
             og_ref, od_ref, om_ref, ov_ref, ag_ref, ad_ref, am_ref, av_ref,
             gall_ref, call_ref, ssem, rsem):
        pos = _mesh_pos()
        me = _lin(pos)
        gall_ref[me] = g_ref[...]
        call_ref[me] = cc_ref[...]
        _all_gather(pos, g_ref, gall_ref, ssem, rsem, 0)
        _all_gather(pos, cc_ref, call_ref, ssem, rsem, N_DEV - 1)

        tot = gall_ref[0]
        for j in range(1, N_DEV):
            tot = tot + gall_ref[j]
        og_ref[...] = tot
        od_ref[...], om_ref[...], ov_ref[...] = _adamw(wp_ref[...], tot, mp_ref[...], vp_ref[...])

        ga = jnp.zeros((D_MODEL, n_sh), F32)
        for j in range(N_DEV):
            d_mine = jnp.zeros((8, n_sh), F32)
            for k in range(N_DEV):
                d_mine = d_mine + jnp.where(me == k, gall_ref[j, :, PK_ADA + n_sh * k:PK_ADA + n_sh * (k + 1)], 0.0)
            col = _silu(call_ref[j])
            ga = ga + jnp.concatenate(
                [col * d_mine[0:1, LANES * a:LANES * (a + 1)] for a in range(n_sh // LANES)], axis=1)
        ag_ref[...] = ga
        ad_ref[...], am_ref[...], av_ref[...] = _adamw(wa_ref[...], ga, ma_ref[...], va_ref[...])

    pk = jax.ShapeDtypeStruct((8, PK_END), F32)
    ada = jax.ShapeDtypeStruct((D_MODEL, n_sh), F32)
    return pl.pallas_call(
        body, name="vec_exchange",
        out_shape=[pk] * 4 + [ada] * 4,
        in_specs=[_vmem_spec()] * 8, out_specs=[_vmem_spec()] * 8,
        scratch_shapes=[
            pltpu.VMEM((N_DEV, 8, PK_END), F32),
            pltpu.VMEM((N_DEV, D_MODEL, LANES), F32),
            pltpu.SemaphoreType.DMA((2 * (N_DEV - 1),)),
            pltpu.SemaphoreType.DMA((2 * (N_DEV - 1),)),
        ],
        compiler_params=pltpu.CompilerParams(vmem_limit_bytes=VMEM_BIG),
    )(gpack, ccol, wpack, mpack, vpack, w_ada, m_ada, v_ada)


def _grad_exchange(grads):
    n = len(grads)

    def body(*refs):
        ins, outs = refs[:n], refs[n:2 * n]
        ssem, rsem, lsem = refs[2 * n], refs[2 * n + 1], refs[2 * n + 2]
        pos = _mesh_pos()
        me = _lin(pos)
        own = [pltpu.make_async_copy(ins[a].at[me], outs[a].at[me], lsem.at[a]) for a in range(n)]
        for cp in own:
            cp.start()
        for a in range(n):
            _all_to_all(pos, ins[a], outs[a], ssem, rsem, a * (N_DEV - 1))
        for cp in own:
            cp.wait()

    return pl.pallas_call(
        body, name="grad_exchange",
        out_shape=[jax.ShapeDtypeStruct(g.shape, F32) for g in grads],
        in_specs=[_any_spec()] * n, out_specs=[_any_spec()] * n,
        scratch_shapes=[
            pltpu.SemaphoreType.DMA((n * (N_DEV - 1),)),
            pltpu.SemaphoreType.DMA((n * (N_DEV - 1),)),
            pltpu.SemaphoreType.DMA((n,)),
        ],
    )(*grads)


def _adamw_reduce(name, parts, w, m, v, row_tile):
    rows, cols = w.shape

    def body(p_ref, w_ref, m_ref, v_ref, g_ref, d_ref, mo_ref, vo_ref):
        g = p_ref[0]
        for j in range(1, N_DEV):
            g = g + p_ref[j]
        g_ref[...] = g
        d_ref[...], mo_ref[...], vo_ref[...] = _adamw(w_ref[...], g, m_ref[...], v_ref[...])

    tile = pl.BlockSpec((row_tile, cols), lambda i: (i, 0))
    return pl.pallas_call(
        body, name=name, grid=(rows // row_tile,),
        out_shape=[jax.ShapeDtypeStruct((rows, cols), F32)] * 4,
        in_specs=[pl.BlockSpec((N_DEV, row_tile, cols), lambda i: (0, i, 0)), tile, tile, tile],
        out_specs=[tile] * 4,
        compiler_params=pltpu.CompilerParams(dimension_semantics=("parallel",), vmem_limit_bytes=VMEM_BIG),
    )(parts, w, m, v)


def _rope_tables(positions):
    inv_freq = 10000.0 ** (-jnp.arange(0, ROPE, 2, dtype=F32) / ROPE)
    ang = positions.astype(F32)[:, None] * inv_freq
    cos, sin = jnp.cos(ang), jnp.sin(ang)
    s_len = positions.shape[0]
    ones = jnp.ones((s_len, NOPE), F32)
    zeros = jnp.zeros((s_len, NOPE), F32)
    z16 = jnp.zeros((s_len, ROPE // 2), F32)
    pad1 = jnp.ones((s_len, LANES - MLA_QK), F32)
    pad0 = jnp.zeros((s_len, LANES - MLA_QK), F32)
    cosf = jnp.concatenate([ones, cos, cos, pad1], axis=1)
    sin_a = jnp.concatenate([zeros, -sin, z16, pad0], axis=1)
    sin_b = jnp.concatenate([zeros, z16, sin, pad0], axis=1)
    return cosf, sin_a, sin_b


def _rearrange_cols(w):
    pad = jnp.zeros((w.shape[0], IN_COLS_R - IN_COLS), w.dtype)
    return jnp.concatenate([w[:, :2688], w[:, 2720:3232], w[:, 2688:2720], pad], axis=1)


def _restore_cols(g):
    return jnp.concatenate([g[:, :2688], g[:, 3200:3232], g[:, 2688:3200]], axis=1)


def _pad_heads(w):
    rows = w.shape[0]
    w = w.reshape(rows, HEADS, MLA_QK)
    return jnp.pad(w, ((0, 0), (0, 0), (0, LANES - MLA_QK))).reshape(rows, HEADS * LANES)


def _unpad_heads(g):
    rows = g.shape[0]
    return g.reshape(rows, HEADS, LANES)[:, :, :MLA_QK].reshape(rows, HEADS * MLA_QK)


def _pad_lanes(v):
    return jnp.pad(v, ((0, 0), (0, LANES - v.shape[1])))


def _col_shards(g):
    rows = g.shape[0]
    return g.reshape(rows, N_DEV, g.shape[1] // N_DEV).transpose(1, 0, 2)


def _from_col_shards(g):
    return g.transpose(1, 0, 2).reshape(g.shape[1], N_DEV * g.shape[2])


def _pack(norm_w, qln, kvln, qhn, khn, ada):
    row = jnp.concatenate([norm_w, qln, kvln, _pad_lanes(qhn), _pad_lanes(khn), ada], axis=1)
    return jnp.broadcast_to(row, (8, PK_END))


def _unpack(p):
    row = p[0:1]
    return (row[:, PK_NORM:PK_QLN], row[:, PK_QLN:PK_KVLN], row[:, PK_KVLN:PK_QHN],
            row[:, PK_QHN:PK_QHN + MLA_QK], row[:, PK_KHN:PK_KHN + MLA_QK], row[:, PK_ADA:PK_END])


def kernel(x, c, positions, w_ada, b_ada, norm_w, w_in, q_lora_norm, w_uq, kv_lora_norm, w_ukv, q_head_norm, k_head_norm, w_out, loss_target, m_w_ada, m_b_ada, m_norm_w, m_w_in, m_q_lora_norm, m_w_uq, m_kv_lora_norm, m_w_ukv, m_q_head_norm, m_k_head_norm, m_w_out, v_w_ada, v_b_ada, v_norm_w, v_w_in, v_q_lora_norm, v_w_uq, v_kv_lora_norm, v_w_ukv, v_q_head_norm, v_k_head_norm, v_w_out):
    s_len = x.shape[1]
    x2 = x.reshape(s_len, D_MODEL)
    tgt = loss_target.reshape(s_len, D_MODEL)
    w_ada_s, w_in_s, w_uq_s, w_ukv_s, w_out_s = w_ada[0], w_in[0], w_uq[0], w_ukv[0], w_out[0]

    ada8 = _ada_fwd(jnp.broadcast_to(c, (8, D_MODEL)), w_ada_s, b_ada.reshape(N_DEV, -1))
    ada = ada8.reshape(1, 3 * D_MODEL)
    shift, scale, gate = ada[:, :D_MODEL], ada[:, D_MODEL:2 * D_MODEL], ada[:, 2 * D_MODEL:]

    g_in, g_uq, g_ukv, g_out = _gather_weights([w_in_s, w_uq_s, w_ukv_s, w_out_s])
    w_in_r = _rearrange_cols(_from_col_shards(g_in))
    wuq_p = _pad_heads(_from_col_shards(g_uq))
    wukv_f = _from_col_shards(g_ukv)
    w_out_f = g_out.reshape(D_MODEL, D_MODEL)

    cosf, sin_a, sin_b = _rope_tables(positions[0])
    qhn_p, khn_p = _pad_lanes(q_head_norm), _pad_lanes(k_head_norm)

    hb, qkv, g_sb, cq, ckv, g_mla, kr, qm, km, vm = _fwd_pre(
        x2, shift, scale, norm_w, w_in_r, q_lora_norm, wuq_p, kv_lora_norm, wukv_f, qhn_p, khn_p,
        cosf, sin_a, sin_b)
    o_sb = _sb_fwd(qkv)
    o_mla, lse = _mla_fwd(qm, km, vm)

    dy, do_sb, dg_sb, do_mla, dg_mla, gw_out, d_gate, loss_acc = _mid(
        o_sb, g_sb, o_mla, g_mla, x2, tgt, gate, w_out_f)
    loss = lax.psum(0.5 * jnp.sum(loss_acc) / D_MODEL, ("x", "y", "c"))

    dq_sb, dk_sb, dv_sb = _sb_bwd(qkv, o_sb, do_sb)
    dq_m, dk_m, dv_m = _mla_bwd(qm, km, vm, o_mla, do_mla, lse)
    dcq, dckv, dkr, gw_uq_p, gw_ukv, g_qhn, g_khn, g_qln, g_kvln = _mla_pre_bwd(
        dq_m, dk_m, dv_m, cq, ckv, kr, q_lora_norm, wuq_p, kv_lora_norm, wukv_f, qhn_p, khn_p,
        cosf, sin_a, sin_b)
    dproj, grad_x, d_shift, d_scale, g_norm_w = _dproj_bwd(
        dq_sb, dk_sb, dv_sb, dg_sb, dcq, dckv, dg_mla, dkr, w_in_r, x2, dy, norm_w, scale)
    gw_in = _restore_cols(_grad_w_in(hb, dproj))

    d_ada = jnp.concatenate([d_shift, d_scale, d_gate], axis=1)
    gpack = _pack(g_norm_w, g_qln, g_kvln, g_qhn[:, :MLA_QK], g_khn[:, :MLA_QK], d_ada)
    wpack = _pack(norm_w, q_lora_norm, kv_lora_norm, q_head_norm, k_head_norm, b_ada)
    mpack = _pack(m_norm_w, m_q_lora_norm, m_kv_lora_norm, m_q_head_norm, m_k_head_norm, m_b_ada)
    vpack = _pack(v_norm_w, v_q_lora_norm, v_kv_lora_norm, v_q_head_norm, v_k_head_norm, v_b_ada)
    ccol = jnp.broadcast_to(c.reshape(D_MODEL, 1), (D_MODEL, LANES))
    pg, pd, pm, pv, ada_g, ada_d, ada_m, ada_v = _vec_exchange(
        gpack, ccol, wpack, mpack, vpack, w_ada_s, m_w_ada[0], v_w_ada[0])

    r_in, r_uq, r_ukv, r_out = _grad_exchange([
        _col_shards(gw_in), _col_shards(_unpad_heads(gw_uq_p)), _col_shards(gw_ukv),
        gw_out.reshape(N_DEV, D_MODEL // N_DEV, D_MODEL)])
    in_g, in_d, in_m, in_v = _adamw_reduce("adamw_w_in", r_in, w_in_s, m_w_in[0], v_w_in[0], 256)
    uq_g, uq_d, uq_m, uq_v = _adamw_reduce("adamw_w_uq", r_uq, w_uq_s, m_w_uq[0], v_w_uq[0], w_uq_s.shape[0])
    ukv_g, ukv_d, ukv_m, ukv_v = _adamw_reduce(
        "adamw_w_ukv", r_ukv, w_ukv_s, m_w_ukv[0], v_w_ukv[0], w_ukv_s.shape[0])
    out_g, out_d, out_m, out_v = _adamw_reduce(
        "adamw_w_out", r_out, w_out_s, m_w_out[0], v_w_out[0], w_out_s.shape[0])

    def group(ada_t, pk, in_t, uq_t, ukv_t, out_t):
        nw, qln, kvln, qhn, khn, b = _unpack(pk)
        return (ada_t[None], b, nw, in_t[None], qln, uq_t[None], kvln, ukv_t[None], qhn, khn, out_t[None])

    return (loss, grad_x.reshape(1, s_len, D_MODEL),
            *group(ada_g, pg, in_g, uq_g, ukv_g, out_g),
            *group(ada_d, pd, in_d, uq_d, ukv_d, out_d),
            *group(ada_m, pm, in_m, uq_m, ukv_m, out_m),
            *group(ada_v, pv, in_v, uq_v, ukv_v, out_v))
```

```python
import functools
import math

import jax
import jax.numpy as jnp
from jax import lax
from jax.experimental import pallas as pl
from jax.experimental.pallas import tpu as pltpu

F32 = jnp.float32
BF16 = jnp.bfloat16

N_DEV = 8
D_MODEL = 1024
HEADS = 8
SB_WIDTH = 512
MLA_WIDTH = 512
Q_LORA = 384
KV_LORA = 256
ROPE = 32
NOPE = 64
MLA_QK = 96
LANES = 128
IN_COLS = 3232
IN_COLS_R = 3328
EPS = 1e-6
NEG = -1e30

ADAM_LR = 0.001
ADAM_B1 = 0.9
ADAM_B2 = 0.999
ADAM_EPS = 1e-08
ADAM_WD = 0.01
ADAM_STEP = 10

TS = 256
TQ = 512
TN_S = 512
VMEM_BIG = 56 * 1024 * 1024

PK_NORM, PK_QLN, PK_KVLN, PK_QHN, PK_KHN, PK_ADA, PK_END = 0, 1024, 1408, 1664, 1792, 1920, 4992

MESH_ID = pl.DeviceIdType.MESH


def _dot_nn(a, b):
    return lax.dot_general(a, b, (((1,), (0,)), ((), ())), preferred_element_type=F32)


def _dot_nt(a, b):
    return lax.dot_general(a, b, (((1,), (1,)), ((), ())), preferred_element_type=F32)


def _dot_tn(a, b):
    return lax.dot_general(a, b, (((0,), (0,)), ((), ())), preferred_element_type=F32)


def _split_bf16(a):
    hi = a.astype(BF16)
    lo = (a - hi.astype(F32)).astype(BF16)
    return hi, lo


def _dot3(a, b):
    ah, al = _split_bf16(a)
    bh, bl = _split_bf16(b)
    return _dot_nn(ah, bh) + _dot_nn(ah, bl) + _dot_nn(al, bh)


def _sigmoid(g):
    return 1.0 / (1.0 + jnp.exp(-g))


def _silu(g):
    return g * _sigmoid(g)


def _lane_iota(shape):
    return lax.broadcasted_iota(jnp.int32, shape, len(shape) - 1)


def _adamw(w, g, m, v):
    m = ADAM_B1 * m + (1.0 - ADAM_B1) * g
    v = ADAM_B2 * v + (1.0 - ADAM_B2) * (g * g)
    m_hat = m / (1.0 - ADAM_B1 ** ADAM_STEP)
    v_hat = v / (1.0 - ADAM_B2 ** ADAM_STEP)
    delta = -ADAM_LR * (m_hat / (jnp.sqrt(v_hat) + ADAM_EPS) + ADAM_WD * w)
    return delta, m, v


def _mesh_pos():
    return lax.axis_index("x"), lax.axis_index("y"), lax.axis_index("c")


def _peer(pos, k):
    x, y, c = pos
    return (1 - x if k & 4 else x, 1 - y if k & 2 else y, 1 - c if k & 1 else c)


def _lin(pos):
    return 4 * pos[0] + 2 * pos[1] + pos[2]


def _remote(src, dst, send_sems, recv_sems, idx, peer):
    return pltpu.make_async_remote_copy(
        src_ref=src, dst_ref=dst, send_sem=send_sems.at[idx], recv_sem=recv_sems.at[idx],
        device_id=peer, device_id_type=MESH_ID)


def _all_gather(pos, src, buf, send_sems, recv_sems, base):
    me = _lin(pos)
    sent = []
    for k in range(1, N_DEV):
        cp = _remote(src, buf.at[me], send_sems, recv_sems, base + k - 1, _peer(pos, k))
        cp.start()
        sent.append(cp)
    for k in range(1, N_DEV):
        peer = _peer(pos, k)
        _remote(src, buf.at[_lin(peer)], send_sems, recv_sems, base + k - 1, peer).wait_recv()
    for cp in sent:
        cp.wait_send()


def _all_to_all(pos, src, buf, send_sems, recv_sems, base):
    me = _lin(pos)
    sent = []
    for k in range(1, N_DEV):
        peer = _peer(pos, k)
        cp = _remote(src.at[_lin(peer)], buf.at[me], send_sems, recv_sems, base + k - 1, peer)
        cp.start()
        sent.append(cp)
    for k in range(1, N_DEV):
        peer = _peer(pos, k)
        _remote(src.at[me], buf.at[_lin(peer)], send_sems, recv_sems, base + k - 1, peer).wait_recv()
    for cp in sent:
        cp.wait_send()


def _vmem_spec():
    return pl.BlockSpec(memory_space=pltpu.VMEM)


def _any_spec():
    return pl.BlockSpec(memory_space=pl.ANY)


def _row_select(slots, n):
    r = lax.broadcasted_iota(jnp.int32, (N_DEV, n), 0)
    out = jnp.zeros((N_DEV, n), F32)
    for j in range(N_DEV):
        out = out + jnp.where(r == j, slots[j], 0.0)
    return out


def _ada_fwd(c8, w_ada, b_ada8):
    n_sh = w_ada.shape[1]

    def body(c_ref, w_ref, b_ref, out_ref, call_ref, psend_ref, precv_ref, ssem, rsem):
        pos = _mesh_pos()
        me = _lin(pos)
        call_ref[me] = c_ref[...]
        _all_gather(pos, c_ref, call_ref, ssem, rsem, 0)
        w = w_ref[...]
        for j in range(N_DEV):
            psend_ref[j] = _dot3(_silu(call_ref[j]), w)
        precv_ref[me] = psend_ref[me]
        _all_to_all(pos, psend_ref, precv_ref, ssem, rsem, N_DEV - 1)
        out_ref[...] = _row_select([precv_ref[j] for j in range(N_DEV)], n_sh) + b_ref[...]

    return pl.pallas_call(
        body, name="ada_fwd",
        out_shape=jax.ShapeDtypeStruct((N_DEV, n_sh), F32),
        in_specs=[_vmem_spec()] * 3, out_specs=_vmem_spec(),
        scratch_shapes=[
            pltpu.VMEM((N_DEV, 8, D_MODEL), F32),
            pltpu.VMEM((N_DEV, 8, n_sh), F32),
            pltpu.VMEM((N_DEV, 8, n_sh), F32),
            pltpu.SemaphoreType.DMA((2 * (N_DEV - 1),)),
            pltpu.SemaphoreType.DMA((2 * (N_DEV - 1),)),
        ],
    )(c8, w_ada, b_ada8)


def _gather_weights(shards):
    n = len(shards)

    def body(*refs):
        ins, outs = refs[:n], refs[n:2 * n]
        ssem, rsem = refs[2 * n], refs[2 * n + 1]
        pos = _mesh_pos()
        me = _lin(pos)
        for a in range(n):
            outs[a][me] = ins[a][...].astype(BF16)
        for a in range(n):
            _all_gather(pos, outs[a].at[me], outs[a], ssem, rsem, a * (N_DEV - 1))

    return pl.pallas_call(
        body, name="gather_weights",
        out_shape=[jax.ShapeDtypeStruct((N_DEV,) + s.shape, BF16) for s in shards],
        in_specs=[_vmem_spec()] * n, out_specs=[_vmem_spec()] * n,
        scratch_shapes=[
            pltpu.SemaphoreType.DMA((n * (N_DEV - 1),)),
            pltpu.SemaphoreType.DMA((n * (N_DEV - 1),)),
        ],
        compiler_params=pltpu.CompilerParams(vmem_limit_bytes=VMEM_BIG),
    )(*shards)


def _rope(t, cosf, sin_a, sin_b):
    return t * cosf + pltpu.roll(t, 112, 1) * sin_a + pltpu.roll(t, 16, 1) * sin_b


def _rope_t(d, cosf, sin_a, sin_b):
    return d * cosf + pltpu.roll(d * sin_a, 16, 1) + pltpu.roll(d * sin_b, 112, 1)


def _head_rms(t):
    return lax.rsqrt(jnp.sum(t * t, axis=1, keepdims=True) * (1.0 / MLA_QK) + EPS)


def _rms_bwd(dxhat_w, xhat, r, n):
    return r * (dxhat_w - xhat * (jnp.sum(dxhat_w * xhat, axis=1, keepdims=True) * (1.0 / n)))


def _mla_latents(cq, ckv, kr, qln, kvln, wuq, wukv):
    rq = lax.rsqrt(jnp.mean(cq * cq, axis=1, keepdims=True) + EPS)
    rkv = lax.rsqrt(jnp.mean(ckv * ckv, axis=1, keepdims=True) + EPS)
    cq_hat = cq * rq
    ckv_hat = ckv * rkv
    cqn = (cq_hat * qln).astype(BF16)
    ckvn = (ckv_hat * kvln).astype(BF16)
    q_all = _dot_nn(cqn, wuq)
    kv = _dot_nn(ckvn, wukv)
    kr64 = pltpu.roll(kr, 64, 1)
    return rq, rkv, cq_hat, ckv_hat, cqn, ckvn, q_all, kv, kr64


def _fwd_pre(x, shift, scale, norm_w, w_in_r, qln, wuq, kvln, wukv, qhn, khn, cosf, sin_a, sin_b):
    s_len = x.shape[0]

    def body(x_ref, shift_ref, scale_ref, nw_ref, w_ref, qln_ref, wuq_ref, kvln_ref, wukv_ref,
             qhn_ref, khn_ref, cos_ref, sa_ref, sb_ref,
             hb_ref, qkv_ref, gsb_ref, cq_ref, ckv_ref, gmla_ref, kr_ref, qm_ref, km_ref, vm_ref):
        xv = x_ref[...]
        r = lax.rsqrt(jnp.mean(xv * xv, axis=1, keepdims=True) + EPS)
        h = (xv * r) * nw_ref[...] * (1.0 + scale_ref[...]) + shift_ref[...]
        hb = h.astype(BF16)
        hb_ref[...] = hb
        qkv_ref[...] = _dot_nn(hb, w_ref[:, 0:1536]).astype(BF16)
        gsb_ref[...] = _dot_nn(hb, w_ref[:, 1536:2048])
        cq = _dot_nn(hb, w_ref[:, 2048:2432])
        ckv = _dot_nn(hb, w_ref[:, 2432:2688])
        gmla_ref[...] = _dot_nn(hb, w_ref[:, 2688:3200])
        kr = _dot_nn(hb, w_ref[:, 3200:3328])
        cq_ref[...] = cq
        ckv_ref[...] = ckv
        kr_ref[...] = kr
        _, _, _, _, _, _, q_all, kv, kr64 = _mla_latents(
            cq, ckv, kr, qln_ref[...], kvln_ref[...], wuq_ref[...], wukv_ref[...])
        cosf, sa, sb = cos_ref[...], sa_ref[...], sb_ref[...]
        qhn_v, khn_v = qhn_ref[...], khn_ref[...]
        low = _lane_iota((TS, LANES)) < NOPE
        for hd in range(HEADS):
            blk = slice(LANES * hd, LANES * (hd + 1))
            qb = q_all[:, blk]
            qm_ref[:, blk] = _rope(qb * _head_rms(qb) * qhn_v, cosf, sa, sb).astype(BF16)
            kb = jnp.where(low, kv[:, blk], kr64)
            km_ref[:, blk] = _rope(kb * _head_rms(kb) * khn_v, cosf, sa, sb).astype(BF16)
        for p in range(HEADS // 2):
            even = kv[:, LANES * 2 * p:LANES * (2 * p + 1)]
            odd = kv[:, LANES * (2 * p + 1):LANES * (2 * p + 2)]
            vm_ref[:, LANES * p:LANES * (p + 1)] = jnp.where(low, pltpu.roll(even, 64, 1), odd).astype(BF16)

    def tile(width):
        return pl.BlockSpec((TS, width), lambda i: (i, 0))

    def full(a):
        return pl.BlockSpec(a.shape, lambda i: (0, 0))

    out_widths = [(D_MODEL, BF16), (1536, BF16), (512, F32), (Q_LORA, F32), (KV_LORA, F32),
                  (512, F32), (LANES, F32), (1024, BF16), (1024, BF16), (512, BF16)]
    return pl.pallas_call(
        body, name="fwd_pre", grid=(s_len // TS,),
        out_shape=[jax.ShapeDtypeStruct((s_len, w), dt) for w, dt in out_widths],
        in_specs=[tile(D_MODEL), full(shift), full(scale), full(norm_w), full(w_in_r), full(qln), full(wuq),
                  full(kvln), full(wukv), full(qhn), full(khn), tile(LANES), tile(LANES), tile(LANES)],
        out_specs=[tile(w) for w, _ in out_widths],
        compiler_params=pltpu.CompilerParams(dimension_semantics=("parallel",), vmem_limit_bytes=VMEM_BIG),
    )(x, shift, scale, norm_w, w_in_r, qln, wuq, kvln, wukv, qhn, khn, cosf, sin_a, sin_b)


CUM_W = 256


def _tri(strict):
    j = lax.broadcasted_iota(jnp.int32, (CUM_W, CUM_W), 0)
    s = lax.broadcasted_iota(jnp.int32, (CUM_W, CUM_W), 1)
    return (j > s if strict else j >= s).astype(BF16)


def _suffix_sums(a, tri_m, carry):
    n = a.shape[1] // CUM_W
    outs = [None] * n
    for i in reversed(range(n)):
        blk = a[:, CUM_W * i:CUM_W * (i + 1)]
        hi, lo = _split_bf16(blk)
        outs[i] = _dot_nn(hi, tri_m) + _dot_nn(lo, tri_m) + carry
        carry = carry + _rowsum(blk)
    return (outs[0] if n == 1 else jnp.concatenate(outs, axis=1)), carry


def _sb_weights(qm, kb, carry, tri_u, diag):
    z = _dot_nt(qm, kb)
    nz = -z
    lk = jnp.minimum(nz, 0.0) - jnp.log(1.0 + jnp.exp(jnp.minimum(z, nz)))
    if diag:
        t = lax.broadcasted_iota(jnp.int32, (TQ, TQ), 0)
        s = lax.broadcasted_iota(jnp.int32, (TQ, TQ), 1)
        valid = s < t
        lk = jnp.where(valid, lk, 0.0)
    after, carry = _suffix_sums(lk, tri_u, carry)
    logw = z + lk + after
    if diag:
        logw = jnp.where(valid, logw, NEG)
    return lk, jnp.exp(logw), carry


SB_SCALE = 0.125


def _head_masks():
    lane = _lane_iota((1, LANES))
    return [lane < 64, lane >= 64]


def _masked(hm, a):
    return jnp.where(hm, a, jnp.zeros_like(a))


def _rowsum(a):
    return jnp.sum(a, axis=1, keepdims=True)


def _key_rows(kj):
    return pl.ds(pl.multiple_of(kj * TQ, TQ), TQ)


def _sb_fwd(qkv):
    s_len = qkv.shape[0]

    def body(q_ref, k_ref, v_ref, o_ref):
        qi = pl.program_id(1)
        q = q_ref[...]
        tri_u = _tri(True)
        masks = _head_masks()
        qms = [_masked(hm, q) * SB_SCALE for hm in masks]

        def block(kj, st, diag):
            rows = _key_rows(kj)
            kb, vb = k_ref[rows, :], v_ref[rows, :]
            carries, acc = list(st[:2]), st[2]
            for h in range(2):
                _, w, carries[h] = _sb_weights(qms[h], kb, carries[h], tri_u, diag)
                acc = acc + _dot_nn(w.astype(BF16), _masked(masks[h], vb))
            return carries[0], carries[1], acc

        zc = jnp.zeros((TQ, 1), F32)
        st = block(qi, (zc, zc, jnp.zeros((TQ, LANES), F32)), True)
        st = lax.fori_loop(1, qi + 1, lambda i, s_: block(qi - i, s_, False), st)
        o_ref[...] = st[2]

    return pl.pallas_call(
        body, name="sb_fwd", grid=(HEADS // 2, s_len // TQ),
        out_shape=jax.ShapeDtypeStruct((s_len, SB_WIDTH), F32),
        in_specs=[pl.BlockSpec((TQ, LANES), lambda p, i: (i, p)),
                  pl.BlockSpec((s_len, LANES), lambda p, i: (0, 4 + p)),
                  pl.BlockSpec((s_len, LANES), lambda p, i: (0, 8 + p))],
        out_specs=pl.BlockSpec((TQ, LANES), lambda p, i: (i, p)),
        compiler_params=pltpu.CompilerParams(dimension_semantics=("parallel", "parallel"),
                                             vmem_limit_bytes=VMEM_BIG),
    )(qkv, qkv, qkv)


def _sb_bwd(qkv, o, do):
    s_len = qkv.shape[0]

    def body(q_ref, k_ref, v_ref, o_ref, do_ref, dq_ref, dk_ref, dv_ref):
        qi = pl.program_id(1)

        @pl.when(qi == 0)
        def _():
            dk_ref[...] = jnp.zeros_like(dk_ref)
            dv_ref[...] = jnp.zeros_like(dv_ref)

        q = q_ref[...]
        do_v = do_ref[...]
        od = o_ref[...] * do_v.astype(F32)
        tri_u = _tri(True)
        tri_i = _tri(False)
        masks = _head_masks()
        qms = [_masked(hm, q) * SB_SCALE for hm in masks]
        doms = [_masked(hm, do_v) for hm in masks]
        d_tots = [_rowsum(jnp.where(hm, od, 0.0)) for hm in masks]

        def block(kj, st, diag):
            rows = _key_rows(kj)
            kb, vb = k_ref[rows, :], v_ref[rows, :]
            carries, carries_d, dqs = list(st[0:2]), list(st[2:4]), list(st[4:6])
            dk_blk = jnp.zeros((TQ, LANES), F32)
            dv_blk = jnp.zeros((TQ, LANES), F32)
            for h in range(2):
                lk, w, carries[h] = _sb_weights(qms[h], kb, carries[h], tri_u, diag)
                wb = w.astype(BF16)
                d_l = _dot_nt(doms[h], vb) * wb.astype(F32)
                from_here, carries_d[h] = _suffix_sums(d_l, tri_i, carries_d[h])
                before = d_tots[h] - from_here
                keep = jnp.exp(lk)
                dz = d_l * keep - before * (1.0 - keep)
                if diag:
                    t = lax.broadcasted_iota(jnp.int32, (TQ, TQ), 0)
                    s = lax.broadcasted_iota(jnp.int32, (TQ, TQ), 1)
                    dz = jnp.where(s < t, dz, 0.0)
                dzb = dz.astype(BF16)
                dk_blk = dk_blk + _dot_tn(dzb, qms[h])
                dv_blk = dv_blk + _dot_tn(wb, doms[h])
                dqs[h] = dqs[h] + _dot_nn(dzb, kb)
            dk_ref[rows, :] += dk_blk
            dv_ref[rows, :] += dv_blk
            return (*carries, *carries_d, *dqs)

        zc = jnp.zeros((TQ, 1), F32)
        za = jnp.zeros((TQ, LANES), F32)
        st = block(qi, (zc, zc, zc, zc, za, za), True)
        st = lax.fori_loop(1, qi + 1, lambda i, s_: block(qi - i, s_, False), st)
        dq_ref[...] = jnp.where(masks[0], st[4], st[5]) * SB_SCALE

    tile = pl.BlockSpec((TQ, LANES), lambda p, i: (i, p))
    col = pl.BlockSpec((s_len, LANES), lambda p, i: (0, p))
    return pl.pallas_call(
        body, name="sb_bwd", grid=(HEADS // 2, s_len // TQ),
        out_shape=[jax.ShapeDtypeStruct((s_len, SB_WIDTH), F32)] * 3,
        in_specs=[tile,
                  pl.BlockSpec((s_len, LANES), lambda p, i: (0, 4 + p)),
                  pl.BlockSpec((s_len, LANES), lambda p, i: (0, 8 + p)),
                  tile, tile],
        out_specs=[tile, col, col],
        compiler_params=pltpu.CompilerParams(dimension_semantics=("parallel", "arbitrary"),
                                             vmem_limit_bytes=VMEM_BIG),
    )(qkv, qkv, qkv, o, do)


MLA_SCALE = 1.0 / math.sqrt(MLA_QK)


def _causal_mask():
    t = lax.broadcasted_iota(jnp.int32, (TQ, TQ), 0)
    s = lax.broadcasted_iota(jnp.int32, (TQ, TQ), 1)
    return s <= t


def _head_lanes(h):
    return slice(LANES * h, LANES * (h + 1))


def _mla_fwd(qm, km, vm):
    s_len = qm.shape[0]

    def body(q_ref, k_ref, v_ref, o_ref, lse_ref):
        qi = pl.program_id(1)
        masks = _head_masks()
        qhs = [q_ref[:, _head_lanes(h)] for h in range(2)]

        def block(kj, st, diag):
            rows = _key_rows(kj)
            vb = v_ref[rows, :]
            ms, ls, acc = list(st[0:2]), list(st[2:4]), st[4]
            alphas, pvs = [], []
            for h in range(2):
                s = _dot_nt(qhs[h], k_ref[rows, _head_lanes(h)]) * MLA_SCALE
                if diag:
                    s = jnp.where(_causal_mask(), s, NEG)
                m_new = jnp.maximum(ms[h], jnp.max(s, axis=1, keepdims=True))
                p = jnp.exp(s - m_new)
                alphas.append(jnp.exp(ms[h] - m_new))
                ls[h] = alphas[h] * ls[h] + _rowsum(p)
                ms[h] = m_new
                pvs.append(_dot_nn(p.astype(BF16), _masked(masks[h], vb)))
            acc = jnp.where(masks[0], alphas[0], alphas[1]) * acc + pvs[0] + pvs[1]
            return (*ms, *ls, acc)

        neg = jnp.full((TQ, 1), NEG, F32)
        zc = jnp.zeros((TQ, 1), F32)
        st = (neg, neg, zc, zc, jnp.zeros((TQ, LANES), F32))
        st = lax.fori_loop(0, qi, lambda kj, s_: block(kj, s_, False), st)
        m0, m1, l0, l1, acc = block(qi, st, True)
        o_ref[...] = acc / jnp.where(masks[0], l0, l1)
        lse_ref[0] = m0 + jnp.log(l0)
        lse_ref[1] = m1 + jnp.log(l1)

    return pl.pallas_call(
        body, name="mla_fwd", grid=(HEADS // 2, s_len // TQ),
        out_shape=[jax.ShapeDtypeStruct((s_len, MLA_WIDTH), F32),
                   jax.ShapeDtypeStruct((HEADS, s_len, 1), F32)],
        in_specs=[pl.BlockSpec((TQ, 2 * LANES), lambda p, i: (i, p)),
                  pl.BlockSpec((s_len, 2 * LANES), lambda p, i: (0, p)),
                  pl.BlockSpec((s_len, LANES), lambda p, i: (0, p))],
        out_specs=[pl.BlockSpec((TQ, LANES), lambda p, i: (i, p)),
                   pl.BlockSpec((2, TQ, 1), lambda p, i: (p, i, 0))],
        compiler_params=pltpu.CompilerParams(dimension_semantics=("parallel", "parallel"),
                                             vmem_limit_bytes=VMEM_BIG),
    )(qm, km, vm)


def _mla_bwd(qm, km, vm, o, do, lse):
    s_len = qm.shape[0]

    def body(q_ref, k_ref, v_ref, o_ref, do_ref, lse_ref, dq_ref, dk_ref, dv_ref):
        qi = pl.program_id(1)

        @pl.when(qi == 0)
        def _():
            dk_ref[...] = jnp.zeros_like(dk_ref)
            dv_ref[...] = jnp.zeros_like(dv_ref)

        do_v = do_ref[...]
        od = o_ref[...] * do_v.astype(F32)
        masks = _head_masks()
        qhs = [q_ref[:, _head_lanes(h)] for h in range(2)]
        doms = [_masked(hm, do_v) for hm in masks]
        deltas = [_rowsum(jnp.where(hm, od, 0.0)) for hm in masks]
        lses = [lse_ref[h] for h in range(2)]

        def block(kj, dqs, diag):
            rows = _key_rows(kj)
            vb = v_ref[rows, :]
            dqs = list(dqs)
            dv_blk = jnp.zeros((TQ, LANES), F32)
            for h in range(2):
                kb = k_ref[rows, _head_lanes(h)]
                s = _dot_nt(qhs[h], kb) * MLA_SCALE
                if diag:
                    s = jnp.where(_causal_mask(), s, NEG)
                p = jnp.exp(s - lses[h])
                dp = _dot_nt(doms[h], vb)
                ds = (p * (dp - deltas[h]) * MLA_SCALE).astype(BF16)
                dk_ref[rows, _head_lanes(h)] += _dot_tn(ds, qhs[h])
                dv_blk = dv_blk + _dot_tn(p.astype(BF16), doms[h])
                dqs[h] = dqs[h] + _dot_nn(ds, kb)
            dv_ref[rows, :] += dv_blk
            return tuple(dqs)

        za = jnp.zeros((TQ, LANES), F32)
        dqs = lax.fori_loop(0, qi, lambda kj, a: block(kj, a, False), (za, za))
        dqs = block(qi, dqs, True)
        dq_ref[:, _head_lanes(0)] = dqs[0]
        dq_ref[:, _head_lanes(1)] = dqs[1]

    return pl.pallas_call(
        body, name="mla_bwd", grid=(HEADS // 2, s_len // TQ),
        out_shape=[jax.ShapeDtypeStruct((s_len, HEADS * LANES), F32),
                   jax.ShapeDtypeStruct((s_len, HEADS * LANES), F32),
                   jax.ShapeDtypeStruct((s_len, MLA_WIDTH), F32)],
        in_specs=[pl.BlockSpec((TQ, 2 * LANES), lambda p, i: (i, p)),
                  pl.BlockSpec((s_len, 2 * LANES), lambda p, i: (0, p)),
                  pl.BlockSpec((s_len, LANES), lambda p, i: (0, p)),
                  pl.BlockSpec((TQ, LANES), lambda p, i: (i, p)),
                  pl.BlockSpec((TQ, LANES), lambda p, i: (i, p)),
                  pl.BlockSpec((2, TQ, 1), lambda p, i: (p, i, 0))],
        out_specs=[pl.BlockSpec((TQ, 2 * LANES), lambda p, i: (i, p)),
                   pl.BlockSpec((s_len, 2 * LANES), lambda p, i: (0, p)),
                   pl.BlockSpec((s_len, LANES), lambda p, i: (0, p))],
        compiler_params=pltpu.CompilerParams(dimension_semantics=("parallel", "arbitrary"),
                                             vmem_limit_bytes=VMEM_BIG),
    )(qm, km, vm, o, do, lse)


def _mid(o_sb, g_sb, o_mla, g_mla, x, target, gate, w_out):
    s_len = x.shape[0]

    def body(osb_ref, gsb_ref, omla_ref, gmla_ref, x_ref, t_ref, gate_ref, w_ref,
             dy_ref, dosb_ref, dgsb_ref, domla_ref, dgmla_ref, gw_ref, dgate_ref, loss_ref):
        @pl.when(pl.program_id(0) == 0)
        def _():
            gw_ref[...] = jnp.zeros_like(gw_ref)
            dgate_ref[...] = jnp.zeros_like(dgate_ref)
            loss_ref[...] = jnp.zeros_like(loss_ref)

        o1, g1, o2, g2 = osb_ref[...], gsb_ref[...], omla_ref[...], gmla_ref[...]
        s1, s2 = _sigmoid(g1), _sigmoid(g2)
        mixed = jnp.concatenate([o1 * (g1 * s1), o2 * (g2 * s2)], axis=1).astype(BF16)
        w = w_ref[...]
        gate_v = gate_ref[...]
        u = _dot_nn(mixed, w)
        err = x_ref[...] + gate_v * u - t_ref[...]
        loss_ref[...] += jnp.sum(err * err, axis=0, keepdims=True)
        dy = err * (1.0 / D_MODEL)
        dy_ref[...] = dy
        dgate_ref[...] += jnp.sum(dy * u, axis=0, keepdims=True)
        du = (dy * gate_v).astype(BF16)
        gw_ref[...] += _dot_tn(mixed, du)
        dmixed = _dot_nt(du, w)
        d1, d2 = dmixed[:, :SB_WIDTH], dmixed[:, SB_WIDTH:]
        dosb_ref[...] = (d1 * (g1 * s1)).astype(BF16)
        dgsb_ref[...] = (d1 * o1 * (s1 * (1.0 + g1 * (1.0 - s1)))).astype(BF16)
        domla_ref[...] = (d2 * (g2 * s2)).astype(BF16)
        dgmla_ref[...] = (d2 * o2 * (s2 * (1.0 + g2 * (1.0 - s2)))).astype(BF16)

    def tile(width):
        return pl.BlockSpec((TS, width), lambda i: (i, 0))

    def full(shape):
        return pl.BlockSpec(shape, lambda i: (0, 0))

    return pl.pallas_call(
        body, name="mid", grid=(s_len // TS,),
        out_shape=[jax.ShapeDtypeStruct((s_len, D_MODEL), F32)]
        + [jax.ShapeDtypeStruct((s_len, 512), BF16)] * 4
        + [jax.ShapeDtypeStruct((D_MODEL, D_MODEL), F32),
           jax.ShapeDtypeStruct((1, D_MODEL), F32), jax.ShapeDtypeStruct((1, D_MODEL), F32)],
        in_specs=[tile(512)] * 4 + [tile(D_MODEL), tile(D_MODEL), full((1, D_MODEL)), full((D_MODEL, D_MODEL))],
        out_specs=[tile(D_MODEL)] + [tile(512)] * 4
        + [full((D_MODEL, D_MODEL)), full((1, D_MODEL)), full((1, D_MODEL))],
        compiler_params=pltpu.CompilerParams(dimension_semantics=("arbitrary",), vmem_limit_bytes=VMEM_BIG),
    )(o_sb, g_sb, o_mla, g_mla, x, target, gate, w_out)


def _mla_pre_bwd(dq, dk, dv, cq, ckv, kr, qln, wuq, kvln, wukv, qhn, khn, cosf, sin_a, sin_b):
    s_len = cq.shape[0]

    def body(dq_ref, dk_ref, dv_ref, cq_ref, ckv_ref, kr_ref, qln_ref, wuq_ref, kvln_ref, wukv_ref,
             qhn_ref, khn_ref, cos_ref, sa_ref, sb_ref,
             dcq_ref, dckv_ref, dkr_ref, gwuq_ref, gwukv_ref, gqhn_ref, gkhn_ref, gqln_ref, gkvln_ref,
             dqa_ref, dkv_ref):
        @pl.when(pl.program_id(0) == 0)
        def _():
            for r_ in (gwuq_ref, gwukv_ref, gqhn_ref, gkhn_ref, gqln_ref, gkvln_ref):
                r_[...] = jnp.zeros_like(r_)

        cq, ckv = cq_ref[...], ckv_ref[...]
        qln_v, kvln_v = qln_ref[...], kvln_ref[...]
        wuq_v, wukv_v = wuq_ref[...], wukv_ref[...]
        rq, rkv, cq_hat, ckv_hat, cqn, ckvn, q_all, kv, kr64 = _mla_latents(
            cq, ckv, kr_ref[...], qln_v, kvln_v, wuq_v, wukv_v)
        cosf, sa, sb = cos_ref[...], sa_ref[...], sb_ref[...]
        qhn_v, khn_v = qhn_ref[...], khn_ref[...]
        lane = _lane_iota((TS, LANES))
        low = lane < NOPE
        g_qhn = jnp.zeros((1, LANES), F32)
        g_khn = jnp.zeros((1, LANES), F32)
        dkr64 = jnp.zeros((TS, LANES), F32)
        for hd in range(HEADS):
            blk = slice(LANES * hd, LANES * (hd + 1))
            qb = q_all[:, blk]
            r = _head_rms(qb)
            xh = qb * r
            dn = _rope_t(dq_ref[:, blk], cosf, sa, sb)
            g_qhn = g_qhn + jnp.sum(dn * xh, axis=0, keepdims=True)
            dqa_ref[:, blk] = _rms_bwd(dn * qhn_v, xh, r, MLA_QK).astype(BF16)

            kb = jnp.where(low, kv[:, blk], kr64)
            r = _head_rms(kb)
            xh = kb * r
            dn = _rope_t(dk_ref[:, blk], cosf, sa, sb)
            g_khn = g_khn + jnp.sum(dn * xh, axis=0, keepdims=True)
            dkb = _rms_bwd(dn * khn_v, xh, r, MLA_QK)
            dkr64 = dkr64 + jnp.where(low, 0.0, dkb)
            dvp = dv_ref[:, LANES * (hd // 2):LANES * (hd // 2 + 1)]
            dvh = pltpu.roll(dvp, 64, 1) if hd % 2 == 0 else dvp
            dkv_ref[:, blk] = jnp.where(low, dkb, dvh).astype(BF16)
        gqhn_ref[...] += g_qhn
        gkhn_ref[...] += g_khn
        dkr_ref[...] = pltpu.roll(dkr64, 64, 1).astype(BF16)

        dqa = dqa_ref[...]
        gwuq_ref[...] += _dot_tn(cqn, dqa)
        dcqn = _dot_nt(dqa, wuq_v)
        gqln_ref[...] += jnp.sum(dcqn * cq_hat, axis=0, keepdims=True)
        dcq_ref[...] = _rms_bwd(dcqn * qln_v, cq_hat, rq, Q_LORA).astype(BF16)

        dkv = dkv_ref[...]
        gwukv_ref[...] += _dot_tn(ckvn, dkv)
        dckvn = _dot_nt(dkv, wukv_v)
        gkvln_ref[...] += jnp.sum(dckvn * ckv_hat, axis=0, keepdims=True)
        dckv_ref[...] = _rms_bwd(dckvn * kvln_v, ckv_hat, rkv, KV_LORA).astype(BF16)

    def tile(width):
        return pl.BlockSpec((TS, width), lambda i: (i, 0))

    def full(shape):
        return pl.BlockSpec(shape, lambda i: (0, 0))

    acc_shapes = [(Q_LORA, 1024), (KV_LORA, 1024), (1, LANES), (1, LANES), (1, Q_LORA), (1, KV_LORA)]
    return pl.pallas_call(
        body, name="mla_pre_bwd", grid=(s_len // TS,),
        out_shape=[jax.ShapeDtypeStruct((s_len, Q_LORA), BF16), jax.ShapeDtypeStruct((s_len, KV_LORA), BF16),
                   jax.ShapeDtypeStruct((s_len, LANES), BF16)]
        + [jax.ShapeDtypeStruct(s, F32) for s in acc_shapes],
        in_specs=[tile(1024), tile(1024), tile(512), tile(Q_LORA), tile(KV_LORA), tile(LANES),
                  full(qln.shape), full(wuq.shape), full(kvln.shape), full(wukv.shape),
                  full(qhn.shape), full(khn.shape), tile(LANES), tile(LANES), tile(LANES)],
        out_specs=[tile(Q_LORA), tile(KV_LORA), tile(LANES)] + [full(s) for s in acc_shapes],
        scratch_shapes=[pltpu.VMEM((TS, 1024), BF16), pltpu.VMEM((TS, 1024), BF16)],
        compiler_params=pltpu.CompilerParams(dimension_semantics=("arbitrary",), vmem_limit_bytes=VMEM_BIG),
    )(dq, dk, dv, cq, ckv, kr, qln, wuq, kvln, wukv, qhn, khn, cosf, sin_a, sin_b)


def _dproj_bwd(dq_sb, dk_sb, dv_sb, dg_sb, dcq, dckv, dg_mla, dkr, w_in_r, x, dy, norm_w, scale):
    s_len = x.shape[0]

    def body(dq_ref, dk_ref, dv_ref, dg_ref, dcq_ref, dckv_ref, dgm_ref, dkr_ref, w_ref, x_ref, dy_ref,
             nw_ref, scale_ref, dp_ref, gx_ref, dshift_ref, dscale_ref, dnw_ref):
        @pl.when(pl.program_id(0) == 0)
        def _():
            for r_ in (dshift_ref, dscale_ref, dnw_ref):
                r_[...] = jnp.zeros_like(r_)

        dp_ref[:, 0:512] = dq_ref[...].astype(BF16)
        dp_ref[:, 512:1024] = dk_ref[...].astype(BF16)
        dp_ref[:, 1024:1536] = dv_ref[...].astype(BF16)
        dp_ref[:, 1536:2048] = dg_ref[...]
        dp_ref[:, 2048:2432] = dcq_ref[...]
        dp_ref[:, 2432:2688] = dckv_ref[...]
        dp_ref[:, 2688:3200] = dgm_ref[...]
        dp_ref[:, 3200:3328] = dkr_ref[...]
        dh = _dot_nt(dp_ref[...], w_ref[...])
        xv = x_ref[...]
        r = lax.rsqrt(jnp.mean(xv * xv, axis=1, keepdims=True) + EPS)
        xh = xv * r
        nw = nw_ref[...]
        dshift_ref[...] += jnp.sum(dh, axis=0, keepdims=True)
        dscale_ref[...] += jnp.sum(dh * (xh * nw), axis=0, keepdims=True)
        dxnw = dh * (1.0 + scale_ref[...])
        dnw_ref[...] += jnp.sum(dxnw * xh, axis=0, keepdims=True)
        gx_ref[...] = dy_ref[...] + _rms_bwd(dxnw * nw, xh, r, D_MODEL)

    def tile(width):
        return pl.BlockSpec((TS, width), lambda i: (i, 0))

    def full(shape):
        return pl.BlockSpec(shape, lambda i: (0, 0))

    vec = (1, D_MODEL)
    return pl.pallas_call(
        body, name="dproj_bwd", grid=(s_len // TS,),
        out_shape=[jax.ShapeDtypeStruct((s_len, IN_COLS_R), BF16), jax.ShapeDtypeStruct((s_len, D_MODEL), F32)]
        + [jax.ShapeDtypeStruct(vec, F32)] * 3,
        in_specs=[tile(512)] * 4 + [tile(Q_LORA), tile(KV_LORA), tile(512), tile(LANES),
                                    full(w_in_r.shape), tile(D_MODEL), tile(D_MODEL), full(vec), full(vec)],
        out_specs=[tile(IN_COLS_R), tile(D_MODEL)] + [full(vec)] * 3,
        compiler_params=pltpu.CompilerParams(dimension_semantics=("arbitrary",), vmem_limit_bytes=VMEM_BIG),
    )(dq_sb, dk_sb, dv_sb, dg_sb, dcq, dckv, dg_mla, dkr, w_in_r, x, dy, norm_w, scale)


def _grad_w_in(hb, dproj):
    s_len = hb.shape[0]
    n_half = IN_COLS_R // 2

    def body(h_ref, d_ref, g_ref):
        @pl.when(pl.program_id(1) == 0)
        def _():
            g_ref[...] = jnp.zeros_like(g_ref)

        g_ref[...] += _dot_tn(h_ref[...], d_ref[...])

    return pl.pallas_call(
        body, name="grad_w_in", grid=(2, s_len // TN_S),
        out_shape=jax.ShapeDtypeStruct((D_MODEL, IN_COLS_R), F32),
        in_specs=[pl.BlockSpec((TN_S, D_MODEL), lambda n, s: (s, 0)),
                  pl.BlockSpec((TN_S, n_half), lambda n, s: (s, n))],
        out_specs=pl.BlockSpec((D_MODEL, n_half), lambda n, s: (0, n)),
        compiler_params=pltpu.CompilerParams(dimension_semantics=("parallel", "arbitrary"),
                                             vmem_limit_bytes=VMEM_BIG),
    )(hb, dproj)


def _vec_exchange(gpack, ccol, wpack, mpack, vpack, w_ada, m_ada, v_ada):
    n_sh = w_ada.shape[1]

    def body(g_ref, cc_ref, wp_ref, mp_ref, vp_ref, wa_ref, ma_ref, va_ref,
             og_ref, od_ref, om_ref, ov_ref, ag_ref, ad_ref, am_ref, av_ref,
             gall_ref, call_ref, ssem, rsem):
        pos = _mesh_pos()
        me = _lin(pos)
        gall_ref[me] = g_ref[...]
        call_ref[me] = cc_ref[...]
        _all_gather(pos, g_ref, gall_ref, ssem, rsem, 0)
        _all_gather(pos, cc_ref, call_ref, ssem, rsem, N_DEV - 1)

        tot = gall_ref[0]
        for j in range(1, N_DEV):
            tot = tot + gall_ref[j]
        og_ref[...] = tot
        od_ref[...], om_ref[...], ov_ref[...] = _adamw(wp_ref[...], tot, mp_ref[...], vp_ref[...])

        ga = jnp.zeros((D_MODEL, n_sh), F32)
        for j in range(N_DEV):
            d_mine = jnp.zeros((8, n_sh), F32)
            for k in range(N_DEV):
                d_mine = d_mine + jnp.where(me == k, gall_ref[j, :, PK_ADA + n_sh * k:PK_ADA + n_sh * (k + 1)], 0.0)
            col = _silu(call_ref[j])
            ga = ga + jnp.concatenate(
                [col * d_mine[0:1, LANES * a:LANES * (a + 1)] for a in range(n_sh // LANES)], axis=1)
        ag_ref[...] = ga
        ad_ref[...], am_ref[...], av_ref[...] = _adamw(wa_ref[...], ga, ma_ref[...], va_ref[...])

    pk = jax.ShapeDtypeStruct((8, PK_END), F32)
    ada = jax.ShapeDtypeStruct((D_MODEL, n_sh), F32)
    return pl.pallas_call(
        body, name="vec_exchange",
        out_shape=[pk] * 4 + [ada] * 4,
        in_specs=[_vmem_spec()] * 8, out_specs=[_vmem_spec()] * 8,
        scratch_shapes=[
            pltpu.VMEM((N_DEV, 8, PK_END), F32),
            pltpu.VMEM((N_DEV, D_MODEL, LANES), F32),
            pltpu.SemaphoreType.DMA((2 * (N_DEV - 1),)),
            pltpu.SemaphoreType.DMA((2 * (N_DEV - 1),)),
        ],
        compiler_params=pltpu.CompilerParams(vmem_limit_bytes=VMEM_BIG),
    )(gpack, ccol, wpack, mpack, vpack, w_ada, m_ada, v_ada)


def _grad_exchange(grads):
    n = len(grads)

    def body(*refs):
        ins, outs = refs[:n], refs[n:2 * n]
        ssem, rsem, lsem = refs[2 * n], refs[2 * n + 1], refs[2 * n + 2]
        pos = _mesh_pos()
        me = _lin(pos)
        own = [pltpu.make_async_copy(ins[a].at[me], outs[a].at[me], lsem.at[a]) for a in range(n)]
        for cp in own:
            cp.start()
        for a in range(n):
            _all_to_all(pos, ins[a], outs[a], ssem, rsem, a * (N_DEV - 1))
        for cp in own:
            cp.wait()

    return pl.pallas_call(
        body, name="grad_exchange",
        out_shape=[jax.ShapeDtypeStruct(g.shape, F32) for g in grads],
        in_specs=[_any_spec()] * n, out_specs=[_any_spec()] * n,
        scratch_shapes=[
            pltpu.SemaphoreType.DMA((n * (N_DEV - 1),)),
            pltpu.SemaphoreType.DMA((n * (N_DEV - 1),)),
            pltpu.SemaphoreType.DMA((n,)),
        ],
    )(*grads)


def _adamw_reduce(name, parts, w, m, v, row_tile):
    rows, cols = w.shape

    def body(p_ref, w_ref, m_ref, v_ref, g_ref, d_ref, mo_ref, vo_ref):
        g = p_ref[0]
        for j in range(1, N_DEV):
            g = g + p_ref[j]
        g_ref[...] = g
        d_ref[...], mo_ref[...], vo_ref[...] = _adamw(w_ref[...], g, m_ref[...], v_ref[...])

    tile = pl.BlockSpec((row_tile, cols), lambda i: (i, 0))
    return pl.pallas_call(
        body, name=name, grid=(rows // row_tile,),
        out_shape=[jax.ShapeDtypeStruct((rows, cols), F32)] * 4,
        in_specs=[pl.BlockSpec((N_DEV, row_tile, cols), lambda i: (0, i, 0)), tile, tile, tile],
        out_specs=[tile] * 4,
        compiler_params=pltpu.CompilerParams(dimension_semantics=("parallel",), vmem_limit_bytes=VMEM_BIG),
    )(parts, w, m, v)


def _rope_tables(positions):
    inv_freq = 10000.0 ** (-jnp.arange(0, ROPE, 2, dtype=F32) / ROPE)
    ang = positions.astype(F32)[:, None] * inv_freq
    cos, sin = jnp.cos(ang), jnp.sin(ang)
    s_len = positions.shape[0]
    ones = jnp.ones((s_len, NOPE), F32)
    zeros = jnp.zeros((s_len, NOPE), F32)
    z16 = jnp.zeros((s_len, ROPE // 2), F32)
    pad1 = jnp.ones((s_len, LANES - MLA_QK), F32)
    pad0 = jnp.zeros((s_len, LANES - MLA_QK), F32)
    cosf = jnp.concatenate([ones, cos, cos, pad1], axis=1)
    sin_a = jnp.concatenate([zeros, -sin, z16, pad0], axis=1)
    sin_b = jnp.concatenate([zeros, z16, sin, pad0], axis=1)
    return cosf, sin_a, sin_b


def _rearrange_cols(w):
    pad = jnp.zeros((w.shape[0], IN_COLS_R - IN_COLS), w.dtype)
    return jnp.concatenate([w[:, :2688], w[:, 2720:3232], w[:, 2688:2720], pad], axis=1)


def _restore_cols(g):
    return jnp.concatenate([g[:, :2688], g[:, 3200:3232], g[:, 2688:3200]], axis=1)


def _pad_heads(w):
    rows = w.shape[0]
    w = w.reshape(rows, HEADS, MLA_QK)
    return jnp.pad(w, ((0, 0), (0, 0), (0, LANES - MLA_QK))).reshape(rows, HEADS * LANES)


def _unpad_heads(g):
    rows = g.shape[0]
    return g.reshape(rows, HEADS, LANES)[:, :, :MLA_QK].reshape(rows, HEADS * MLA_QK)


def _pad_lanes(v):
    return jnp.pad(v, ((0, 0), (0, LANES - v.shape[1])))


def _col_shards(g):
    rows = g.shape[0]
    return g.reshape(rows, N_DEV, g.shape[1] // N_DEV).transpose(1, 0, 2)


def _from_col_shards(g):
    return g.transpose(1, 0, 2).reshape(g.shape[1], N_DEV * g.shape[2])


def _pack(norm_w, qln, kvln, qhn, khn, ada):
    row = jnp.concatenate([norm_w, qln, kvln, _pad_lanes(qhn), _pad_lanes(khn), ada], axis=1)
    return jnp.broadcast_to(row, (8, PK_END))


def _unpack(p):
    row = p[0:1]
    return (row[:, PK_NORM:PK_QLN], row[:, PK_QLN:PK_KVLN], row[:, PK_KVLN:PK_QHN],
            row[:, PK_QHN:PK_QHN + MLA_QK], row[:, PK_KHN:PK_KHN + MLA_QK], row[:, PK_ADA:PK_END])


def kernel(x, c, positions, w_ada, b_ada, norm_w, w_in, q_lora_norm, w_uq, kv_lora_norm, w_ukv, q_head_norm, k_head_norm, w_out, loss_target, m_w_ada, m_b_ada, m_norm_w, m_w_in, m_q_lora_norm, m_w_uq, m_kv_lora_norm, m_w_ukv, m_q_head_norm, m_k_head_norm, m_w_out, v_w_ada, v_b_ada, v_norm_w, v_w_in, v_q_lora_norm, v_w_uq, v_kv_lora_norm, v_w_ukv, v_q_head_norm, v_k_head_norm, v_w_out):
    s_len = x.shape[1]
    x2 = x.reshape(s_len, D_MODEL)
    tgt = loss_target.reshape(s_len, D_MODEL)
    w_ada_s, w_in_s, w_uq_s, w_ukv_s, w_out_s = w_ada[0], w_in[0], w_uq[0], w_ukv[0], w_out[0]

    ada8 = _ada_fwd(jnp.broadcast_to(c, (8, D_MODEL)), w_ada_s, b_ada.reshape(N_DEV, -1))
    ada = ada8.reshape(1, 3 * D_MODEL)
    shift, scale, gate = ada[:, :D_MODEL], ada[:, D_MODEL:2 * D_MODEL], ada[:, 2 * D_MODEL:]

    g_in, g_uq, g_ukv, g_out = _gather_weights([w_in_s, w_uq_s, w_ukv_s, w_out_s])
    w_in_r = _rearrange_cols(_from_col_shards(g_in))
    wuq_p = _pad_heads(_from_col_shards(g_uq))
    wukv_f = _from_col_shards(g_ukv)
    w_out_f = g_out.reshape(D_MODEL, D_MODEL)

    cosf, sin_a, sin_b = _rope_tables(positions[0])
    qhn_p, khn_p = _pad_lanes(q_head_norm), _pad_lanes(k_head_norm)

    hb, qkv, g_sb, cq, ckv, g_mla, kr, qm, km, vm = _fwd_pre(
        x2, shift, scale, norm_w, w_in_r, q_lora_norm, wuq_p, kv_lora_norm, wukv_f, qhn_p, khn_p,
        cosf, sin_a, sin_b)
    o_sb = _sb_fwd(qkv)
    o_mla, lse = _mla_fwd(qm, km, vm)

    dy, do_sb, dg_sb, do_mla, dg_mla, gw_out, d_gate, loss_acc = _mid(
        o_sb, g_sb, o_mla, g_mla, x2, tgt, gate, w_out_f)
    loss = lax.psum(0.5 * jnp.sum(loss_acc) / D_MODEL, ("x", "y", "c"))

    dq_sb, dk_sb, dv_sb = _sb_bwd(qkv, o_sb, do_sb)
    dq_m, dk_m, dv_m = _mla_bwd(qm, km, vm, o_mla, do_mla, lse)
    dcq, dckv, dkr, gw_uq_p, gw_ukv, g_qhn, g_khn, g_qln, g_kvln = _mla_pre_bwd(
        dq_m, dk_m, dv_m, cq, ckv, kr, q_lora_norm, wuq_p, kv_lora_norm, wukv_f, qhn_p, khn_p,
        cosf, sin_a, sin_b)
    dproj, grad_x, d_shift, d_scale, g_norm_w = _dproj_bwd(
        dq_sb, dk_sb, dv_sb, dg_sb, dcq, dckv, dg_mla, dkr, w_in_r, x2, dy, norm_w, scale)
    gw_in = _restore_cols(_grad_w_in(hb, dproj))

    d_ada = jnp.concatenate([d_shift, d_scale, d_gate], axis=1)
    gpack = _pack(g_norm_w, g_qln, g_kvln, g_qhn[:, :MLA_QK], g_khn[:, :MLA_QK], d_ada)
    wpack = _pack(norm_w, q_lora_norm, kv_lora_norm, q_head_norm, k_head_norm, b_ada)
    mpack = _pack(m_norm_w, m_q_lora_norm, m_kv_lora_norm, m_q_head_norm, m_k_head_norm, m_b_ada)
    vpack = _pack(v_norm_w, v_q_lora_norm, v_kv_lora_norm, v_q_head_norm, v_k_head_norm, v_b_ada)
    ccol = jnp.broadcast_to(c.reshape(D_MODEL, 1), (D_MODEL, LANES))
    pg, pd, pm, pv, ada_g, ada_d, ada_m, ada_v = _vec_exchange(
        gpack, ccol, wpack, mpack, vpack, w_ada_s, m_w_ada[0], v_w_ada[0])

    r_in, r_uq, r_ukv, r_out = _grad_exchange([
        _col_shards(gw_in), _col_shards(_unpad_heads(gw_uq_p)), _col_shards(gw_ukv),
        gw_out.reshape(N_DEV, D_MODEL // N_DEV, D_MODEL)])
    in_g, in_d, in_m, in_v = _adamw_reduce("adamw_w_in", r_in, w_in_s, m_w_in[0], v_w_in[0], 256)
    uq_g, uq_d, uq_m, uq_v = _adamw_reduce("adamw_w_uq", r_uq, w_uq_s, m_w_uq[0], v_w_uq[0], w_uq_s.shape[0])
    ukv_g, ukv_d, ukv_m, ukv_v = _adamw_reduce(
        "adamw_w_ukv", r_ukv, w_ukv_s, m_w_ukv[0], v_w_ukv[0], w_ukv_s.shape[0])
    out_g, out_d, out_m, out_v = _adamw_reduce(
        "adamw_w_out", r_out, w_out_s, m_w_out[0], v_w_out[0], w_out_s.shape[0])

    def group(ada_t, pk, in_t, uq_t, ukv_t, out_t):
        nw, qln, kvln, qhn, khn, b = _unpack(pk)
        return (ada_t[None], b, nw, in_t[None], qln, uq_t[None], kvln, ukv_t[None], qhn, khn, out_t[None])

    return (loss, grad_x.reshape(1, s_len, D_MODEL),
            *group(ada_g, pg, in_g, uq_g, ukv_g, out_g),
            *group(ada_d, pd, in_d, uq_d, ukv_d, out_d),
            *group(ada_m, pm, in_m, uq_m, ukv_m, out_m),
            *group(ada_v, pv, in_v, uq_v, ukv_v, out_v))
```

```python
import functools
import math

import jax
import jax.numpy as jnp
from jax import lax
from jax.experimental import pallas as pl
from jax.experimental.pallas import tpu as pltpu

F32 = jnp.float32
BF16 = jnp.bfloat16

N_DEV = 8
D_MODEL = 1024
HEADS = 8
SB_WIDTH = 512
MLA_WIDTH = 512
Q_LORA = 384
KV_LORA = 256
ROPE = 32
NOPE = 64
MLA_QK = 96
LANES = 128
IN_COLS = 3232
IN_COLS_R = 3328
EPS = 1e-6
NEG = -1e30

ADAM_LR = 0.001
ADAM_B1 = 0.9
ADAM_B2 = 0.999
ADAM_EPS = 1e-08
ADAM_WD = 0.01
ADAM_STEP = 10

TS = 256
TQ = 512
KEY_UNROLL = 2
TN_S = 512
VMEM_BIG = 56 * 1024 * 1024

PK_NORM, PK_QLN, PK_KVLN, PK_QHN, PK_KHN, PK_ADA, PK_END = 0, 1024, 1408, 1664, 1792, 1920, 4992

MESH_ID = pl.DeviceIdType.MESH


def _dot_nn(a, b):
    return lax.dot_general(a, b, (((1,), (0,)), ((), ())), preferred_element_type=F32)


def _dot_nt(a, b):
    return lax.dot_general(a, b, (((1,), (1,)), ((), ())), preferred_element_type=F32)


def _dot_tn(a, b):
    return lax.dot_general(a, b, (((0,), (0,)), ((), ())), preferred_element_type=F32)


def _split_bf16(a):
    hi = a.astype(BF16)
    lo = (a - hi.astype(F32)).astype(BF16)
    return hi, lo


def _dot3(a, b):
    ah, al = _split_bf16(a)
    bh, bl = _split_bf16(b)
    return _dot_nn(ah, bh) + _dot_nn(ah, bl) + _dot_nn(al, bh)


def _sigmoid(g):
    return 1.0 / (1.0 + jnp.exp(-g))


def _silu(g):
    return g * _sigmoid(g)


def _lane_iota(shape):
    return lax.broadcasted_iota(jnp.int32, shape, len(shape) - 1)


def _adamw(w, g, m, v):
    m = ADAM_B1 * m + (1.0 - ADAM_B1) * g
    v = ADAM_B2 * v + (1.0 - ADAM_B2) * (g * g)
    m_hat = m / (1.0 - ADAM_B1 ** ADAM_STEP)
    v_hat = v / (1.0 - ADAM_B2 ** ADAM_STEP)
    delta = -ADAM_LR * (m_hat / (jnp.sqrt(v_hat) + ADAM_EPS) + ADAM_WD * w)
    return delta, m, v


def _mesh_pos():
    return lax.axis_index("x"), lax.axis_index("y"), lax.axis_index("c")


def _peer(pos, k):
    x, y, c = pos
    return (1 - x if k & 4 else x, 1 - y if k & 2 else y, 1 - c if k & 1 else c)


def _lin(pos):
    return 4 * pos[0] + 2 * pos[1] + pos[2]


def _remote(src, dst, send_sems, recv_sems, idx, peer):
    return pltpu.make_async_remote_copy(
        src_ref=src, dst_ref=dst, send_sem=send_sems.at[idx], recv_sem=recv_sems.at[idx],
        device_id=peer, device_id_type=MESH_ID)


def _all_gather(pos, src, buf, send_sems, recv_sems, base):
    me = _lin(pos)
    sent = []
    for k in range(1, N_DEV):
        cp = _remote(src, buf.at[me], send_sems, recv_sems, base + k - 1, _peer(pos, k))
        cp.start()
        sent.append(cp)
    for k in range(1, N_DEV):
        peer = _peer(pos, k)
        _remote(src, buf.at[_lin(peer)], send_sems, recv_sems, base + k - 1, peer).wait_recv()
    for cp in sent:
        cp.wait_send()


def _all_to_all(pos, src, buf, send_sems, recv_sems, base):
    me = _lin(pos)
    sent = []
    for k in range(1, N_DEV):
        peer = _peer(pos, k)
        cp = _remote(src.at[_lin(peer)], buf.at[me], send_sems, recv_sems, base + k - 1, peer)
        cp.start()
        sent.append(cp)
    for k in range(1, N_DEV):
        peer = _peer(pos, k)
        _remote(src.at[me], buf.at[_lin(peer)], send_sems, recv_sems, base + k - 1, peer).wait_recv()
    for cp in sent:
        cp.wait_send()


def _vmem_spec():
    return pl.BlockSpec(memory_space=pltpu.VMEM)


def _any_spec():
    return pl.BlockSpec(memory_space=pl.ANY)


def _row_select(slots, n):
    r = lax.broadcasted_iota(jnp.int32, (N_DEV, n), 0)
    out = jnp.zeros((N_DEV, n), F32)
    for j in range(N_DEV):
        out = out + jnp.where(r == j, slots[j], 0.0)
    return out


def _ada_fwd(c8, w_ada, b_ada8):
    n_sh = w_ada.shape[1]

    def body(c_ref, w_ref, b_ref, out_ref, call_ref, psend_ref, precv_ref, ssem, rsem):
        pos = _mesh_pos()
        me = _lin(pos)
        call_ref[me] = c_ref[...]
        _all_gather(pos, c_ref, call_ref, ssem, rsem, 0)
        w = w_ref[...]
        for j in range(N_DEV):
            psend_ref[j] = _dot3(_silu(call_ref[j]), w)
        precv_ref[me] = psend_ref[me]
        _all_to_all(pos, psend_ref, precv_ref, ssem, rsem, N_DEV - 1)
        out_ref[...] = _row_select([precv_ref[j] for j in range(N_DEV)], n_sh) + b_ref[...]

    return pl.pallas_call(
        body, name="ada_fwd",
        out_shape=jax.ShapeDtypeStruct((N_DEV, n_sh), F32),
        in_specs=[_vmem_spec()] * 3, out_specs=_vmem_spec(),
        scratch_shapes=[
            pltpu.VMEM((N_DEV, 8, D_MODEL), F32),
            pltpu.VMEM((N_DEV, 8, n_sh), F32),
            pltpu.VMEM((N_DEV, 8, n_sh), F32),
            pltpu.SemaphoreType.DMA((2 * (N_DEV - 1),)),
            pltpu.SemaphoreType.DMA((2 * (N_DEV - 1),)),
        ],
    )(c8, w_ada, b_ada8)


def _gather_weights(shards):
    n = len(shards)

    def body(*refs):
        ins, outs = refs[:n], refs[n:2 * n]
        ssem, rsem = refs[2 * n], refs[2 * n + 1]
        pos = _mesh_pos()
        me = _lin(pos)
        for a in range(n):
            outs[a][me] = ins[a][...].astype(BF16)
        for a in range(n):
            _all_gather(pos, outs[a].at[me], outs[a], ssem, rsem, a * (N_DEV - 1))

    return pl.pallas_call(
        body, name="gather_weights",
        out_shape=[jax.ShapeDtypeStruct((N_DEV,) + s.shape, BF16) for s in shards],
        in_specs=[_vmem_spec()] * n, out_specs=[_vmem_spec()] * n,
        scratch_shapes=[
            pltpu.SemaphoreType.DMA((n * (N_DEV - 1),)),
            pltpu.SemaphoreType.DMA((n * (N_DEV - 1),)),
        ],
        compiler_params=pltpu.CompilerParams(vmem_limit_bytes=VMEM_BIG),
    )(*shards)


def _rope(t, cosf, sin_a, sin_b):
    return t * cosf + pltpu.roll(t, 112, 1) * sin_a + pltpu.roll(t, 16, 1) * sin_b


def _rope_t(d, cosf, sin_a, sin_b):
    return d * cosf + pltpu.roll(d * sin_a, 16, 1) + pltpu.roll(d * sin_b, 112, 1)


def _head_rms(t):
    return lax.rsqrt(jnp.sum(t * t, axis=1, keepdims=True) * (1.0 / MLA_QK) + EPS)


def _rms_bwd(dxhat_w, xhat, r, n):
    return r * (dxhat_w - xhat * (jnp.sum(dxhat_w * xhat, axis=1, keepdims=True) * (1.0 / n)))


def _mla_latents(cq, ckv, kr, qln, kvln, wuq, wukv):
    rq = lax.rsqrt(jnp.mean(cq * cq, axis=1, keepdims=True) + EPS)
    rkv = lax.rsqrt(jnp.mean(ckv * ckv, axis=1, keepdims=True) + EPS)
    cq_hat = cq * rq
    ckv_hat = ckv * rkv
    cqn = (cq_hat * qln).astype(BF16)
    ckvn = (ckv_hat * kvln).astype(BF16)
    q_all = _dot_nn(cqn, wuq)
    kv = _dot_nn(ckvn, wukv)
    kr64 = pltpu.roll(kr, 64, 1)
    return rq, rkv, cq_hat, ckv_hat, cqn, ckvn, q_all, kv, kr64


def _fwd_pre(x, shift, scale, norm_w, w_in_r, qln, wuq, kvln, wukv, qhn, khn, cosf, sin_a, sin_b):
    s_len = x.shape[0]

    def body(x_ref, shift_ref, scale_ref, nw_ref, w_ref, qln_ref, wuq_ref, kvln_ref, wukv_ref,
             qhn_ref, khn_ref, cos_ref, sa_ref, sb_ref,
             hb_ref, qkv_ref, gsb_ref, cq_ref, ckv_ref, gmla_ref, kr_ref, qm_ref, km_ref, vm_ref):
        xv = x_ref[...]
        r = lax.rsqrt(jnp.mean(xv * xv, axis=1, keepdims=True) + EPS)
        h = (xv * r) * nw_ref[...] * (1.0 + scale_ref[...]) + shift_ref[...]
        hb = h.astype(BF16)
        hb_ref[...] = hb
        qkv_ref[...] = _dot_nn(hb, w_ref[:, 0:1536]).astype(BF16)
        gsb_ref[...] = _dot_nn(hb, w_ref[:, 1536:2048])
        cq = _dot_nn(hb, w_ref[:, 2048:2432])
        ckv = _dot_nn(hb, w_ref[:, 2432:2688])
        gmla_ref[...] = _dot_nn(hb, w_ref[:, 2688:3200])
        kr = _dot_nn(hb, w_ref[:, 3200:3328])
        cq_ref[...] = cq
        ckv_ref[...] = ckv
        kr_ref[...] = kr
        _, _, _, _, _, _, q_all, kv, kr64 = _mla_latents(
            cq, ckv, kr, qln_ref[...], kvln_ref[...], wuq_ref[...], wukv_ref[...])
        cosf, sa, sb = cos_ref[...], sa_ref[...], sb_ref[...]
        qhn_v, khn_v = qhn_ref[...], khn_ref[...]
        low = _lane_iota((TS, LANES)) < NOPE
        for hd in range(HEADS):
            blk = slice(LANES * hd, LANES * (hd + 1))
            qb = q_all[:, blk]
            qm_ref[:, blk] = _rope(qb * _head_rms(qb) * qhn_v, cosf, sa, sb).astype(BF16)
            kb = jnp.where(low, kv[:, blk], kr64)
            km_ref[:, blk] = _rope(kb * _head_rms(kb) * khn_v, cosf, sa, sb).astype(BF16)
        for p in range(HEADS // 2):
            even = kv[:, LANES * 2 * p:LANES * (2 * p + 1)]
            odd = kv[:, LANES * (2 * p + 1):LANES * (2 * p + 2)]
            vm_ref[:, LANES * p:LANES * (p + 1)] = jnp.where(low, pltpu.roll(even, 64, 1), odd).astype(BF16)

    def tile(width):
        return pl.BlockSpec((TS, width), lambda i: (i, 0))

    def full(a):
        return pl.BlockSpec(a.shape, lambda i: (0, 0))

    out_widths = [(D_MODEL, BF16), (1536, BF16), (512, F32), (Q_LORA, F32), (KV_LORA, F32),
                  (512, F32), (LANES, F32), (1024, BF16), (1024, BF16), (512, BF16)]
    return pl.pallas_call(
        body, name="fwd_pre", grid=(s_len // TS,),
        out_shape=[jax.ShapeDtypeStruct((s_len, w), dt) for w, dt in out_widths],
        in_specs=[tile(D_MODEL), full(shift), full(scale), full(norm_w), full(w_in_r), full(qln), full(wuq),
                  full(kvln), full(wukv), full(qhn), full(khn), tile(LANES), tile(LANES), tile(LANES)],
        out_specs=[tile(w) for w, _ in out_widths],
        compiler_params=pltpu.CompilerParams(dimension_semantics=("parallel",), vmem_limit_bytes=VMEM_BIG),
    )(x, shift, scale, norm_w, w_in_r, qln, wuq, kvln, wukv, qhn, khn, cosf, sin_a, sin_b)


CUM_W = 256


def _tri(strict):
    j = lax.broadcasted_iota(jnp.int32, (CUM_W, CUM_W), 0)
    s = lax.broadcasted_iota(jnp.int32, (CUM_W, CUM_W), 1)
    return (j > s if strict else j >= s).astype(BF16)


def _suffix_sums(a, tri_m, carry):
    n = a.shape[1] // CUM_W
    outs = [None] * n
    for i in reversed(range(n)):
        blk = a[:, CUM_W * i:CUM_W * (i + 1)]
        hi, lo = _split_bf16(blk)
        outs[i] = _dot_nn(hi, tri_m) + _dot_nn(lo, tri_m) + carry
        carry = carry + _rowsum(blk)
    return (outs[0] if n == 1 else jnp.concatenate(outs, axis=1)), carry


def _sb_weights(qm, kb, carry, tri_u, diag):
    z = _dot_nt(qm, kb)
    nz = -z
    lk = jnp.minimum(nz, 0.0) - jnp.log(1.0 + jnp.exp(jnp.minimum(z, nz)))
    if diag:
        t = lax.broadcasted_iota(jnp.int32, (TQ, TQ), 0)
        s = lax.broadcasted_iota(jnp.int32, (TQ, TQ), 1)
        valid = s < t
        lk = jnp.where(valid, lk, 0.0)
    after, carry = _suffix_sums(lk, tri_u, carry)
    logw = z + lk + after
    if diag:
        logw = jnp.where(valid, logw, NEG)
    return lk, jnp.exp(logw), carry


SB_SCALE = 0.125


def _head_masks():
    lane = _lane_iota((1, LANES))
    return [lane < 64, lane >= 64]


def _masked(hm, a):
    return jnp.where(hm, a, jnp.zeros_like(a))


def _rowsum(a):
    return jnp.sum(a, axis=1, keepdims=True)


def _key_rows(kj):
    return pl.ds(pl.multiple_of(kj * TQ, TQ), TQ)


def _over_keys_below(qi, fn, st, ascending):
    n_full = qi // KEY_UNROLL
    n_rest = qi - n_full * KEY_UNROLL

    def group(g, s_):
        for u in range(KEY_UNROLL):
            s_ = fn(_key_rows(g * KEY_UNROLL + (u if ascending else KEY_UNROLL - 1 - u)), s_)
        return s_

    if ascending:
        st = lax.fori_loop(0, n_full, group, st)
        return lax.fori_loop(0, n_rest, lambda i, s_: fn(_key_rows(n_full * KEY_UNROLL + i), s_), st)
    st = lax.fori_loop(0, n_rest, lambda i, s_: fn(_key_rows(qi - 1 - i), s_), st)
    return lax.fori_loop(0, n_full, lambda i, s_: group(n_full - 1 - i, s_), st)


def _sb_fwd(qkv):
    s_len = qkv.shape[0]

    def body(q_ref, k_ref, v_ref, o_ref):
        qi = pl.program_id(1)
        q = q_ref[...]
        tri_u = _tri(True)
        masks = _head_masks()
        qms = [_masked(hm, q) * SB_SCALE for hm in masks]

        def block(rows, st, diag):
            kb, vb = k_ref[rows, :], v_ref[rows, :]
            carries, acc = list(st[:2]), st[2]
            for h in range(2):
                _, w, carries[h] = _sb_weights(qms[h], kb, carries[h], tri_u, diag)
                acc = acc + _dot_nn(w.astype(BF16), _masked(masks[h], vb))
            return carries[0], carries[1], acc

        zc = jnp.zeros((TQ, 1), F32)
        st = block(_key_rows(qi), (zc, zc, jnp.zeros((TQ, LANES), F32)), True)
        st = _over_keys_below(qi, lambda rows, s_: block(rows, s_, False), st, ascending=False)
        o_ref[...] = st[2]

    return pl.pallas_call(
        body, name="sb_fwd", grid=(HEADS // 2, s_len // TQ),
        out_shape=jax.ShapeDtypeStruct((s_len, SB_WIDTH), F32),
        in_specs=[pl.BlockSpec((TQ, LANES), lambda p, i: (i, p)),
                  pl.BlockSpec((s_len, LANES), lambda p, i: (0, 4 + p)),
                  pl.BlockSpec((s_len, LANES), lambda p, i: (0, 8 + p))],
        out_specs=pl.BlockSpec((TQ, LANES), lambda p, i: (i, p)),
        compiler_params=pltpu.CompilerParams(dimension_semantics=("parallel", "parallel"),
                                             vmem_limit_bytes=VMEM_BIG),
    )(qkv, qkv, qkv)


def _sb_bwd(qkv, o, do):
    s_len = qkv.shape[0]

    def body(q_ref, k_ref, v_ref, o_ref, do_ref, dq_ref, dk_ref, dv_ref):
        qi = pl.program_id(1)

        @pl.when(qi == 0)
        def _():
            dk_ref[...] = jnp.zeros_like(dk_ref)
            dv_ref[...] = jnp.zeros_like(dv_ref)

        q = q_ref[...]
        do_v = do_ref[...]
        od = o_ref[...] * do_v.astype(F32)
        tri_u = _tri(True)
        tri_i = _tri(False)
        masks = _head_masks()
        qms = [_masked(hm, q) * SB_SCALE for hm in masks]
        doms = [_masked(hm, do_v) for hm in masks]
        d_tots = [_rowsum(jnp.where(hm, od, 0.0)) for hm in masks]

        def block(rows, st, diag):
            kb, vb = k_ref[rows, :], v_ref[rows, :]
            carries, carries_d, dqs = list(st[0:2]), list(st[2:4]), list(st[4:6])
            dk_blk = jnp.zeros((TQ, LANES), F32)
            dv_blk = jnp.zeros((TQ, LANES), F32)
            for h in range(2):
                lk, w, carries[h] = _sb_weights(qms[h], kb, carries[h], tri_u, diag)
                wb = w.astype(BF16)
                d_l = _dot_nt(doms[h], vb) * wb.astype(F32)
                from_here, carries_d[h] = _suffix_sums(d_l, tri_i, carries_d[h])
                before = d_tots[h] - from_here
                keep = jnp.exp(lk)
                dz = d_l * keep - before * (1.0 - keep)
                if diag:
                    t = lax.broadcasted_iota(jnp.int32, (TQ, TQ), 0)
                    s = lax.broadcasted_iota(jnp.int32, (TQ, TQ), 1)
                    dz = jnp.where(s < t, dz, 0.0)
                dzb = dz.astype(BF16)
                dk_blk = dk_blk + _dot_tn(dzb, qms[h])
                dv_blk = dv_blk + _dot_tn(wb, doms[h])
                dqs[h] = dqs[h] + _dot_nn(dzb, kb)
            dk_ref[rows, :] += dk_blk
            dv_ref[rows, :] += dv_blk
            return (*carries, *carries_d, *dqs)

        zc = jnp.zeros((TQ, 1), F32)
        za = jnp.zeros((TQ, LANES), F32)
        st = block(_key_rows(qi), (zc, zc, zc, zc, za, za), True)
        st = _over_keys_below(qi, lambda rows, s_: block(rows, s_, False), st, ascending=False)
        dq_ref[...] = jnp.where(masks[0], st[4], st[5]) * SB_SCALE

    tile = pl.BlockSpec((TQ, LANES), lambda p, i: (i, p))
    col = pl.BlockSpec((s_len, LANES), lambda p, i: (0, p))
    return pl.pallas_call(
        body, name="sb_bwd", grid=(HEADS // 2, s_len // TQ),
        out_shape=[jax.ShapeDtypeStruct((s_len, SB_WIDTH), F32)] * 3,
        in_specs=[tile,
                  pl.BlockSpec((s_len, LANES), lambda p, i: (0, 4 + p)),
                  pl.BlockSpec((s_len, LANES), lambda p, i: (0, 8 + p)),
                  tile, tile],
        out_specs=[tile, col, col],
        compiler_params=pltpu.CompilerParams(dimension_semantics=("parallel", "arbitrary"),
                                             vmem_limit_bytes=VMEM_BIG),
    )(qkv, qkv, qkv, o, do)


MLA_SCALE = 1.0 / math.sqrt(MLA_QK)


def _causal_mask():
    t = lax.broadcasted_iota(jnp.int32, (TQ, TQ), 0)
    s = lax.broadcasted_iota(jnp.int32, (TQ, TQ), 1)
    return s <= t


def _head_lanes(h):
    return slice(LANES * h, LANES * (h + 1))


def _mla_fwd(qm, km, vm):
    s_len = qm.shape[0]

    def body(q_ref, k_ref, v_ref, o_ref, lse_ref):
        qi = pl.program_id(1)
        masks = _head_masks()
        qhs = [q_ref[:, _head_lanes(h)] for h in range(2)]

        def block(rows, st, diag):
            vb = v_ref[rows, :]
            ms, ls, acc = list(st[0:2]), list(st[2:4]), st[4]
            alphas, pvs = [], []
            for h in range(2):
                s = _dot_nt(qhs[h], k_ref[rows, _head_lanes(h)]) * MLA_SCALE
                if diag:
                    s = jnp.where(_causal_mask(), s, NEG)
                m_new = jnp.maximum(ms[h], jnp.max(s, axis=1, keepdims=True))
                p = jnp.exp(s - m_new)
                alphas.append(jnp.exp(ms[h] - m_new))
                ls[h] = alphas[h] * ls[h] + _rowsum(p)
                ms[h] = m_new
                pvs.append(_dot_nn(p.astype(BF16), _masked(masks[h], vb)))
            acc = jnp.where(masks[0], alphas[0], alphas[1]) * acc + pvs[0] + pvs[1]
            return (*ms, *ls, acc)

        neg = jnp.full((TQ, 1), NEG, F32)
        zc = jnp.zeros((TQ, 1), F32)
        st = (neg, neg, zc, zc, jnp.zeros((TQ, LANES), F32))
        st = _over_keys_below(qi, lambda rows, s_: block(rows, s_, False), st, ascending=True)
        m0, m1, l0, l1, acc = block(_key_rows(qi), st, True)
        o_ref[...] = acc / jnp.where(masks[0], l0, l1)
        lse_ref[0] = m0 + jnp.log(l0)
        lse_ref[1] = m1 + jnp.log(l1)

    return pl.pallas_call(
        body, name="mla_fwd", grid=(HEADS // 2, s_len // TQ),
        out_shape=[jax.ShapeDtypeStruct((s_len, MLA_WIDTH), F32),
                   jax.ShapeDtypeStruct((HEADS, s_len, 1), F32)],
        in_specs=[pl.BlockSpec((TQ, 2 * LANES), lambda p, i: (i, p)),
                  pl.BlockSpec((s_len, 2 * LANES), lambda p, i: (0, p)),
                  pl.BlockSpec((s_len, LANES), lambda p, i: (0, p))],
        out_specs=[pl.BlockSpec((TQ, LANES), lambda p, i: (i, p)),
                   pl.BlockSpec((2, TQ, 1), lambda p, i: (p, i, 0))],
        compiler_params=pltpu.CompilerParams(dimension_semantics=("parallel", "parallel"),
                                             vmem_limit_bytes=VMEM_BIG),
    )(qm, km, vm)


def _mla_bwd(qm, km, vm, o, do, lse):
    s_len = qm.shape[0]

    def body(q_ref, k_ref, v_ref, o_ref, do_ref, lse_ref, dq_ref, dk_ref, dv_ref):
        qi = pl.program_id(1)

        @pl.when(qi == 0)
        def _():
            dk_ref[...] = jnp.zeros_like(dk_ref)
            dv_ref[...] = jnp.zeros_like(dv_ref)

        do_v = do_ref[...]
        od = o_ref[...] * do_v.astype(F32)
        masks = _head_masks()
        qhs = [q_ref[:, _head_lanes(h)] for h in range(2)]
        doms = [_masked(hm, do_v) for hm in masks]
        deltas = [_rowsum(jnp.where(hm, od, 0.0)) for hm in masks]
        lses = [lse_ref[h] for h in range(2)]

        def block(rows, dqs, diag):
            vb = v_ref[rows, :]
            dqs = list(dqs)
            dv_blk = jnp.zeros((TQ, LANES), F32)
            for h in range(2):
                kb = k_ref[rows, _head_lanes(h)]
                s = _dot_nt(qhs[h], kb) * MLA_SCALE
                if diag:
                    s = jnp.where(_causal_mask(), s, NEG)
                p = jnp.exp(s - lses[h])
                dp = _dot_nt(doms[h], vb)
                ds = (p * (dp - deltas[h]) * MLA_SCALE).astype(BF16)
                dk_ref[rows, _head_lanes(h)] += _dot_tn(ds, qhs[h])
                dv_blk = dv_blk + _dot_tn(p.astype(BF16), doms[h])
                dqs[h] = dqs[h] + _dot_nn(ds, kb)
            dv_ref[rows, :] += dv_blk
            return tuple(dqs)

        za = jnp.zeros((TQ, LANES), F32)
        dqs = _over_keys_below(qi, lambda rows, a: block(rows, a, False), (za, za), ascending=True)
        dqs = block(_key_rows(qi), dqs, True)
        dq_ref[:, _head_lanes(0)] = dqs[0]
        dq_ref[:, _head_lanes(1)] = dqs[1]

    return pl.pallas_call(
        body, name="mla_bwd", grid=(HEADS // 2, s_len // TQ),
        out_shape=[jax.ShapeDtypeStruct((s_len, HEADS * LANES), F32),
                   jax.ShapeDtypeStruct((s_len, HEADS * LANES), F32),
                   jax.ShapeDtypeStruct((s_len, MLA_WIDTH), F32)],
        in_specs=[pl.BlockSpec((TQ, 2 * LANES), lambda p, i: (i, p)),
                  pl.BlockSpec((s_len, 2 * LANES), lambda p, i: (0, p)),
                  pl.BlockSpec((s_len, LANES), lambda p, i: (0, p)),
                  pl.BlockSpec((TQ, LANES), lambda p, i: (i, p)),
                  pl.BlockSpec((TQ, LANES), lambda p, i: (i, p)),
                  pl.BlockSpec((2, TQ, 1), lambda p, i: (p, i, 0))],
        out_specs=[pl.BlockSpec((TQ, 2 * LANES), lambda p, i: (i, p)),
                   pl.BlockSpec((s_len, 2 * LANES), lambda p, i: (0, p)),
                   pl.BlockSpec((s_len, LANES), lambda p, i: (0, p))],
        compiler_params=pltpu.CompilerParams(dimension_semantics=("parallel", "arbitrary"),
                                             vmem_limit_bytes=VMEM_BIG),
    )(qm, km, vm, o, do, lse)


def _mid(o_sb, g_sb, o_mla, g_mla, x, target, gate, w_out):
    s_len = x.shape[0]

    def body(osb_ref, gsb_ref, omla_ref, gmla_ref, x_ref, t_ref, gate_ref, w_ref,
             dy_ref, dosb_ref, dgsb_ref, domla_ref, dgmla_ref, gw_ref, dgate_ref, loss_ref):
        @pl.when(pl.program_id(0) == 0)
        def _():
            gw_ref[...] = jnp.zeros_like(gw_ref)
            dgate_ref[...] = jnp.zeros_like(dgate_ref)
            loss_ref[...] = jnp.zeros_like(loss_ref)

        o1, g1, o2, g2 = osb_ref[...], gsb_ref[...], omla_ref[...], gmla_ref[...]
        s1, s2 = _sigmoid(g1), _sigmoid(g2)
        mixed = jnp.concatenate([o1 * (g1 * s1), o2 * (g2 * s2)], axis=1).astype(BF16)
        w = w_ref[...]
        gate_v = gate_ref[...]
        u = _dot_nn(mixed, w)
        err = x_ref[...] + gate_v * u - t_ref[...]
        loss_ref[...] += jnp.sum(err * err, axis=0, keepdims=True)
        dy = err * (1.0 / D_MODEL)
        dy_ref[...] = dy
        dgate_ref[...] += jnp.sum(dy * u, axis=0, keepdims=True)
        du = (dy * gate_v).astype(BF16)
        gw_ref[...] += _dot_tn(mixed, du)
        dmixed = _dot_nt(du, w)
        d1, d2 = dmixed[:, :SB_WIDTH], dmixed[:, SB_WIDTH:]
        dosb_ref[...] = (d1 * (g1 * s1)).astype(BF16)
        dgsb_ref[...] = (d1 * o1 * (s1 * (1.0 + g1 * (1.0 - s1)))).astype(BF16)
        domla_ref[...] = (d2 * (g2 * s2)).astype(BF16)
        dgmla_ref[...] = (d2 * o2 * (s2 * (1.0 + g2 * (1.0 - s2)))).astype(BF16)

    def tile(width):
        return pl.BlockSpec((TS, width), lambda i: (i, 0))

    def full(shape):
        return pl.BlockSpec(shape, lambda i: (0, 0))

    return pl.pallas_call(
        body, name="mid", grid=(s_len // TS,),
        out_shape=[jax.ShapeDtypeStruct((s_len, D_MODEL), F32)]
        + [jax.ShapeDtypeStruct((s_len, 512), BF16)] * 4
        + [jax.ShapeDtypeStruct((D_MODEL, D_MODEL), F32),
           jax.ShapeDtypeStruct((1, D_MODEL), F32), jax.ShapeDtypeStruct((1, D_MODEL), F32)],
        in_specs=[tile(512)] * 4 + [tile(D_MODEL), tile(D_MODEL), full((1, D_MODEL)), full((D_MODEL, D_MODEL))],
        out_specs=[tile(D_MODEL)] + [tile(512)] * 4
        + [full((D_MODEL, D_MODEL)), full((1, D_MODEL)), full((1, D_MODEL))],
        compiler_params=pltpu.CompilerParams(dimension_semantics=("arbitrary",), vmem_limit_bytes=VMEM_BIG),
    )(o_sb, g_sb, o_mla, g_mla, x, target, gate, w_out)


def _mla_pre_bwd(dq, dk, dv, cq, ckv, kr, qln, wuq, kvln, wukv, qhn, khn, cosf, sin_a, sin_b):
    s_len = cq.shape[0]

    def body(dq_ref, dk_ref, dv_ref, cq_ref, ckv_ref, kr_ref, qln_ref, wuq_ref, kvln_ref, wukv_ref,
             qhn_ref, khn_ref, cos_ref, sa_ref, sb_ref,
             dcq_ref, dckv_ref, dkr_ref, gwuq_ref, gwukv_ref, gqhn_ref, gkhn_ref, gqln_ref, gkvln_ref,
             dqa_ref, dkv_ref):
        @pl.when(pl.program_id(0) == 0)
        def _():
            for r_ in (gwuq_ref, gwukv_ref, gqhn_ref, gkhn_ref, gqln_ref, gkvln_ref):
                r_[...] = jnp.zeros_like(r_)

        cq, ckv = cq_ref[...], ckv_ref[...]
        qln_v, kvln_v = qln_ref[...], kvln_ref[...]
        wuq_v, wukv_v = wuq_ref[...], wukv_ref[...]
        rq, rkv, cq_hat, ckv_hat, cqn, ckvn, q_all, kv, kr64 = _mla_latents(
            cq, ckv, kr_ref[...], qln_v, kvln_v, wuq_v, wukv_v)
        cosf, sa, sb = cos_ref[...], sa_ref[...], sb_ref[...]
        qhn_v, khn_v = qhn_ref[...], khn_ref[...]
        lane = _lane_iota((TS, LANES))
        low = lane < NOPE
        g_qhn = jnp.zeros((1, LANES), F32)
        g_khn = jnp.zeros((1, LANES), F32)
        dkr64 = jnp.zeros((TS, LANES), F32)
        for hd in range(HEADS):
            blk = slice(LANES * hd, LANES * (hd + 1))
            qb = q_all[:, blk]
            r = _head_rms(qb)
            xh = qb * r
            dn = _rope_t(dq_ref[:, blk], cosf, sa, sb)
            g_qhn = g_qhn + jnp.sum(dn * xh, axis=0, keepdims=True)
            dqa_ref[:, blk] = _rms_bwd(dn * qhn_v, xh, r, MLA_QK).astype(BF16)

            kb = jnp.where(low, kv[:, blk], kr64)
            r = _head_rms(kb)
            xh = kb * r
            dn = _rope_t(dk_ref[:, blk], cosf, sa, sb)
            g_khn = g_khn + jnp.sum(dn * xh, axis=0, keepdims=True)
            dkb = _rms_bwd(dn * khn_v, xh, r, MLA_QK)
            dkr64 = dkr64 + jnp.where(low, 0.0, dkb)
            dvp = dv_ref[:, LANES * (hd // 2):LANES * (hd // 2 + 1)]
            dvh = pltpu.roll(dvp, 64, 1) if hd % 2 == 0 else dvp
            dkv_ref[:, blk] = jnp.where(low, dkb, dvh).astype(BF16)
        gqhn_ref[...] += g_qhn
        gkhn_ref[...] += g_khn
        dkr_ref[...] = pltpu.roll(dkr64, 64, 1).astype(BF16)

        dqa = dqa_ref[...]
        gwuq_ref[...] += _dot_tn(cqn, dqa)
        dcqn = _dot_nt(dqa, wuq_v)
        gqln_ref[...] += jnp.sum(dcqn * cq_hat, axis=0, keepdims=True)
        dcq_ref[...] = _rms_bwd(dcqn * qln_v, cq_hat, rq, Q_LORA).astype(BF16)

        dkv = dkv_ref[...]
        gwukv_ref[...] += _dot_tn(ckvn, dkv)
        dckvn = _dot_nt(dkv, wukv_v)
        gkvln_ref[...] += jnp.sum(dckvn * ckv_hat, axis=0, keepdims=True)
        dckv_ref[...] = _rms_bwd(dckvn * kvln_v, ckv_hat, rkv, KV_LORA).astype(BF16)

    def tile(width):
        return pl.BlockSpec((TS, width), lambda i: (i, 0))

    def full(shape):
        return pl.BlockSpec(shape, lambda i: (0, 0))

    acc_shapes = [(Q_LORA, 1024), (KV_LORA, 1024), (1, LANES), (1, LANES), (1, Q_LORA), (1, KV_LORA)]
    return pl.pallas_call(
        body, name="mla_pre_bwd", grid=(s_len // TS,),
        out_shape=[jax.ShapeDtypeStruct((s_len, Q_LORA), BF16), jax.ShapeDtypeStruct((s_len, KV_LORA), BF16),
                   jax.ShapeDtypeStruct((s_len, LANES), BF16)]
        + [jax.ShapeDtypeStruct(s, F32) for s in acc_shapes],
        in_specs=[tile(1024), tile(1024), tile(512), tile(Q_LORA), tile(KV_LORA), tile(LANES),
                  full(qln.shape), full(wuq.shape), full(kvln.shape), full(wukv.shape),
                  full(qhn.shape), full(khn.shape), tile(LANES), tile(LANES), tile(LANES)],
        out_specs=[tile(Q_LORA), tile(KV_LORA), tile(LANES)] + [full(s) for s in acc_shapes],
        scratch_shapes=[pltpu.VMEM((TS, 1024), BF16), pltpu.VMEM((TS, 1024), BF16)],
        compiler_params=pltpu.CompilerParams(dimension_semantics=("arbitrary",), vmem_limit_bytes=VMEM_BIG),
    )(dq, dk, dv, cq, ckv, kr, qln, wuq, kvln, wukv, qhn, khn, cosf, sin_a, sin_b)


def _dproj_bwd(dq_sb, dk_sb, dv_sb, dg_sb, dcq, dckv, dg_mla, dkr, w_in_r, x, dy, norm_w, scale):
    s_len = x.shape[0]

    def body(dq_ref, dk_ref, dv_ref, dg_ref, dcq_ref, dckv_ref, dgm_ref, dkr_ref, w_ref, x_ref, dy_ref,
             nw_ref, scale_ref, dp_ref, gx_ref, dshift_ref, dscale_ref, dnw_ref):
        @pl.when(pl.program_id(0) == 0)
        def _():
            for r_ in (dshift_ref, dscale_ref, dnw_ref):
                r_[...] = jnp.zeros_like(r_)

        dp_ref[:, 0:512] = dq_ref[...].astype(BF16)
        dp_ref[:, 512:1024] = dk_ref[...].astype(BF16)
        dp_ref[:, 1024:1536] = dv_ref[...].astype(BF16)
        dp_ref[:, 1536:2048] = dg_ref[...]
        dp_ref[:, 2048:2432] = dcq_ref[...]
        dp_ref[:, 2432:2688] = dckv_ref[...]
        dp_ref[:, 2688:3200] = dgm_ref[...]
        dp_ref[:, 3200:3328] = dkr_ref[...]
        dh = _dot_nt(dp_ref[...], w_ref[...])
        xv = x_ref[...]
        r = lax.rsqrt(jnp.mean(xv * xv, axis=1, keepdims=True) + EPS)
        xh = xv * r
        nw = nw_ref[...]
        dshift_ref[...] += jnp.sum(dh, axis=0, keepdims=True)
        dscale_ref[...] += jnp.sum(dh * (xh * nw), axis=0, keepdims=True)
        dxnw = dh * (1.0 + scale_ref[...])
        dnw_ref[...] += jnp.sum(dxnw * xh, axis=0, keepdims=True)
        gx_ref[...] = dy_ref[...] + _rms_bwd(dxnw * nw, xh, r, D_MODEL)

    def tile(width):
        return pl.BlockSpec((TS, width), lambda i: (i, 0))

    def full(shape):
        return pl.BlockSpec(shape, lambda i: (0, 0))

    vec = (1, D_MODEL)
    return pl.pallas_call(
        body, name="dproj_bwd", grid=(s_len // TS,),
        out_shape=[jax.ShapeDtypeStruct((s_len, IN_COLS_R), BF16), jax.ShapeDtypeStruct((s_len, D_MODEL), F32)]
        + [jax.ShapeDtypeStruct(vec, F32)] * 3,
        in_specs=[tile(512)] * 4 + [tile(Q_LORA), tile(KV_LORA), tile(512), tile(LANES),
                                    full(w_in_r.shape), tile(D_MODEL), tile(D_MODEL), full(vec), full(vec)],
        out_specs=[tile(IN_COLS_R), tile(D_MODEL)] + [full(vec)] * 3,
        compiler_params=pltpu.CompilerParams(dimension_semantics=("arbitrary",), vmem_limit_bytes=VMEM_BIG),
    )(dq_sb, dk_sb, dv_sb, dg_sb, dcq, dckv, dg_mla, dkr, w_in_r, x, dy, norm_w, scale)


def _grad_w_in(hb, dproj):
    s_len = hb.shape[0]
    n_half = IN_COLS_R // 2

    def body(h_ref, d_ref, g_ref):
        @pl.when(pl.program_id(1) == 0)
        def _():
            g_ref[...] = jnp.zeros_like(g_ref)

        g_ref[...] += _dot_tn(h_ref[...], d_ref[...])

    return pl.pallas_call(
        body, name="grad_w_in", grid=(2, s_len // TN_S),
        out_shape=jax.ShapeDtypeStruct((D_MODEL, IN_COLS_R), F32),
        in_specs=[pl.BlockSpec((TN_S, D_MODEL), lambda n, s: (s, 0)),
                  pl.BlockSpec((TN_S, n_half), lambda n, s: (s, n))],
        out_specs=pl.BlockSpec((D_MODEL, n_half), lambda n, s: (0, n)),
        compiler_params=pltpu.CompilerParams(dimension_semantics=("parallel", "arbitrary"),
                                             vmem_limit_bytes=VMEM_BIG),
    )(hb, dproj)


def _vec_exchange(gpack, ccol, wpack, mpack, vpack, w_ada, m_ada, v_ada):
    n_sh = w_ada.shape[1]

    def body(g_ref, cc_ref, wp_ref, mp_ref, vp_ref, wa_ref, ma_ref, va_ref,
             og_ref, od_ref, om_ref, ov_ref, ag_ref, ad_ref, am_ref, av_ref,
             gall_ref, call_ref, ssem, rsem):
        pos = _mesh_pos()
        me = _lin(pos)
        gall_ref[me] = g_ref[...]
        call_ref[me] = cc_ref[...]
        _all_gather(pos, g_ref, gall_ref, ssem, rsem, 0)
        _all_gather(pos, cc_ref, call_ref, ssem, rsem, N_DEV - 1)

        tot = gall_ref[0]
        for j in range(1, N_DEV):
            tot = tot + gall_ref[j]
        og_ref[...] = tot
        od_ref[...], om_ref[...], ov_ref[...] = _adamw(wp_ref[...], tot, mp_ref[...], vp_ref[...])

        ga = jnp.zeros((D_MODEL, n_sh), F32)
        for j in range(N_DEV):
            d_mine = jnp.zeros((8, n_sh), F32)
            for k in range(N_DEV):
                d_mine = d_mine + jnp.where(me == k, gall_ref[j, :, PK_ADA + n_sh * k:PK_ADA + n_sh * (k + 1)], 0.0)
            col = _silu(call_ref[j])
            ga = ga + jnp.concatenate(
                [col * d_mine[0:1, LANES * a:LANES * (a + 1)] for a in range(n_sh // LANES)], axis=1)
        ag_ref[...] = ga
        ad_ref[...], am_ref[...], av_ref[...] = _adamw(wa_ref[...], ga, ma_ref[...], va_ref[...])

    pk = jax.ShapeDtypeStruct((8, PK_END), F32)
    ada = jax.ShapeDtypeStruct((D_MODEL, n_sh), F32)
    return pl.pallas_call(
        body, name="vec_exchange",
        out_shape=[pk] * 4 + [ada] * 4,
        in_specs=[_vmem_spec()] * 8, out_specs=[_vmem_spec()] * 8,
        scratch_shapes=[
            pltpu.VMEM((N_DEV, 8, PK_END), F32),
            pltpu.VMEM((N_DEV, D_MODEL, LANES), F32),
            pltpu.SemaphoreType.DMA((2 * (N_DEV - 1),)),
            pltpu.SemaphoreType.DMA((2 * (N_DEV - 1),)),
        ],
        compiler_params=pltpu.CompilerParams(vmem_limit_bytes=VMEM_BIG),
    )(gpack, ccol, wpack, mpack, vpack, w_ada, m_ada, v_ada)


def _grad_exchange(grads):
    n = len(grads)

    def body(*refs):
        ins, outs = refs[:n], refs[n:2 * n]
        ssem, rsem, lsem = refs[2 * n], refs[2 * n + 1], refs[2 * n + 2]
        pos = _mesh_pos()
        me = _lin(pos)
        own = [pltpu.make_async_copy(ins[a].at[me], outs[a].at[me], lsem.at[a]) for a in range(n)]
        for cp in own:
            cp.start()
        for a in range(n):
            _all_to_all(pos, ins[a], outs[a], ssem, rsem, a * (N_DEV - 1))
        for cp in own:
            cp.wait()

    return pl.pallas_call(
        body, name="grad_exchange",
        out_shape=[jax.ShapeDtypeStruct(g.shape, F32) for g in grads],
        in_specs=[_any_spec()] * n, out_specs=[_any_spec()] * n,
        scratch_shapes=[
            pltpu.SemaphoreType.DMA((n * (N_DEV - 1),)),
            pltpu.SemaphoreType.DMA((n * (N_DEV - 1),)),
            pltpu.SemaphoreType.DMA((n,)),
        ],
    )(*grads)


def _adamw_reduce(name, parts, w, m, v, row_tile):
    rows, cols = w.shape

    def body(p_ref, w_ref, m_ref, v_ref, g_ref, d_ref, mo_ref, vo_ref):
        g = p_ref[0]
        for j in range(1, N_DEV):
            g = g + p_ref[j]
        g_ref[...] = g
        d_ref[...], mo_ref[...], vo_ref[...] = _adamw(w_ref[...], g, m_ref[...], v_ref[...])

    tile = pl.BlockSpec((row_tile, cols), lambda i: (i, 0))
    return pl.pallas_call(
        body, name=name, grid=(rows // row_tile,),
        out_shape=[jax.ShapeDtypeStruct((rows, cols), F32)] * 4,
        in_specs=[pl.BlockSpec((N_DEV, row_tile, cols), lambda i: (0, i, 0)), tile, tile, tile],
        out_specs=[tile] * 4,
        compiler_params=pltpu.CompilerParams(dimension_semantics=("parallel",), vmem_limit_bytes=VMEM_BIG),
    )(parts, w, m, v)


def _rope_tables(positions):
    inv_freq = 10000.0 ** (-jnp.arange(0, ROPE, 2, dtype=F32) / ROPE)
    ang = positions.astype(F32)[:, None] * inv_freq
    cos, sin = jnp.cos(ang), jnp.sin(ang)
    s_len = positions.shape[0]
    ones = jnp.ones((s_len, NOPE), F32)
    zeros = jnp.zeros((s_len, NOPE), F32)
    z16 = jnp.zeros((s_len, ROPE // 2), F32)
    pad1 = jnp.ones((s_len, LANES - MLA_QK), F32)
    pad0 = jnp.zeros((s_len, LANES - MLA_QK), F32)
    cosf = jnp.concatenate([ones, cos, cos, pad1], axis=1)
    sin_a = jnp.concatenate([zeros, -sin, z16, pad0], axis=1)
    sin_b = jnp.concatenate([zeros, z16, sin, pad0], axis=1)
    return cosf, sin_a, sin_b


def _rearrange_cols(w):
    pad = jnp.zeros((w.shape[0], IN_COLS_R - IN_COLS), w.dtype)
    return jnp.concatenate([w[:, :2688], w[:, 2720:3232], w[:, 2688:2720], pad], axis=1)


def _restore_cols(g):
    return jnp.concatenate([g[:, :2688], g[:, 3200:3232], g[:, 2688:3200]], axis=1)


def _pad_heads(w):
    rows = w.shape[0]
    w = w.reshape(rows, HEADS, MLA_QK)
    return jnp.pad(w, ((0, 0), (0, 0), (0, LANES - MLA_QK))).reshape(rows, HEADS * LANES)


def _unpad_heads(g):
    rows = g.shape[0]
    return g.reshape(rows, HEADS, LANES)[:, :, :MLA_QK].reshape(rows, HEADS * MLA_QK)


def _pad_lanes(v):
    return jnp.pad(v, ((0, 0), (0, LANES - v.shape[1])))


def _col_shards(g):
    rows = g.shape[0]
    return g.reshape(rows, N_DEV, g.shape[1] // N_DEV).transpose(1, 0, 2)


def _from_col_shards(g):
    return g.transpose(1, 0, 2).reshape(g.shape[1], N_DEV * g.shape[2])


def _pack(norm_w, qln, kvln, qhn, khn, ada):
    row = jnp.concatenate([norm_w, qln, kvln, _pad_lanes(qhn), _pad_lanes(khn), ada], axis=1)
    return jnp.broadcast_to(row, (8, PK_END))


def _unpack(p):
    row = p[0:1]
    return (row[:, PK_NORM:PK_QLN], row[:, PK_QLN:PK_KVLN], row[:, PK_KVLN:PK_QHN],
            row[:, PK_QHN:PK_QHN + MLA_QK], row[:, PK_KHN:PK_KHN + MLA_QK], row[:, PK_ADA:PK_END])


def kernel(x, c, positions, w_ada, b_ada, norm_w, w_in, q_lora_norm, w_uq, kv_lora_norm, w_ukv, q_head_norm, k_head_norm, w_out, loss_target, m_w_ada, m_b_ada, m_norm_w, m_w_in, m_q_lora_norm, m_w_uq, m_kv_lora_norm, m_w_ukv, m_q_head_norm, m_k_head_norm, m_w_out, v_w_ada, v_b_ada, v_norm_w, v_w_in, v_q_lora_norm, v_w_uq, v_kv_lora_norm, v_w_ukv, v_q_head_norm, v_k_head_norm, v_w_out):
    s_len = x.shape[1]
    x2 = x.reshape(s_len, D_MODEL)
    tgt = loss_target.reshape(s_len, D_MODEL)
    w_ada_s, w_in_s, w_uq_s, w_ukv_s, w_out_s = w_ada[0], w_in[0], w_uq[0], w_ukv[0], w_out[0]

    ada8 = _ada_fwd(jnp.broadcast_to(c, (8, D_MODEL)), w_ada_s, b_ada.reshape(N_DEV, -1))
    ada = ada8.reshape(1, 3 * D_MODEL)
    shift, scale, gate = ada[:, :D_MODEL], ada[:, D_MODEL:2 * D_MODEL], ada[:, 2 * D_MODEL:]

    g_in, g_uq, g_ukv, g_out = _gather_weights([w_in_s, w_uq_s, w_ukv_s, w_out_s])
    w_in_r = _rearrange_cols(_from_col_shards(g_in))
    wuq_p = _pad_heads(_from_col_shards(g_uq))
    wukv_f = _from_col_shards(g_ukv)
    w_out_f = g_out.reshape(D_MODEL, D_MODEL)

    cosf, sin_a, sin_b = _rope_tables(positions[0])
    qhn_p, khn_p = _pad_lanes(q_head_norm), _pad_lanes(k_head_norm)

    hb, qkv, g_sb, cq, ckv, g_mla, kr, qm, km, vm = _fwd_pre(
        x2, shift, scale, norm_w, w_in_r, q_lora_norm, wuq_p, kv_lora_norm, wukv_f, qhn_p, khn_p,
        cosf, sin_a, sin_b)
    o_sb = _sb_fwd(qkv)
    o_mla, lse = _mla_fwd(qm, km, vm)

    dy, do_sb, dg_sb, do_mla, dg_mla, gw_out, d_gate, loss_acc = _mid(
        o_sb, g_sb, o_mla, g_mla, x2, tgt, gate, w_out_f)
    loss = lax.psum(0.5 * jnp.sum(loss_acc) / D_MODEL, ("x", "y", "c"))

    dq_sb, dk_sb, dv_sb = _sb_bwd(qkv, o_sb, do_sb)
    dq_m, dk_m, dv_m = _mla_bwd(qm, km, vm, o_mla, do_mla, lse)
    dcq, dckv, dkr, gw_uq_p, gw_ukv, g_qhn, g_khn, g_qln, g_kvln = _mla_pre_bwd(
        dq_m, dk_m, dv_m, cq, ckv, kr, q_lora_norm, wuq_p, kv_lora_norm, wukv_f, qhn_p, khn_p,
        cosf, sin_a, sin_b)
    dproj, grad_x, d_shift, d_scale, g_norm_w = _dproj_bwd(
        dq_sb, dk_sb, dv_sb, dg_sb, dcq, dckv, dg_mla, dkr, w_in_r, x2, dy, norm_w, scale)
    gw_in = _restore_cols(_grad_w_in(hb, dproj))

    d_ada = jnp.concatenate([d_shift, d_scale, d_gate], axis=1)
    gpack = _pack(g_norm_w, g_qln, g_kvln, g_qhn[:, :MLA_QK], g_khn[:, :MLA_QK], d_ada)
    wpack = _pack(norm_w, q_lora_norm, kv_lora_norm, q_head_norm, k_head_norm, b_ada)
    mpack = _pack(m_norm_w, m_q_lora_norm, m_kv_lora_norm, m_q_head_norm, m_k_head_norm, m_b_ada)
    vpack = _pack(v_norm_w, v_q_lora_norm, v_kv_lora_norm, v_q_head_norm, v_k_head_norm, v_b_ada)
    ccol = jnp.broadcast_to(c.reshape(D_MODEL, 1), (D_MODEL, LANES))
    pg, pd, pm, pv, ada_g, ada_d, ada_m, ada_v = _vec_exchange(
        gpack, ccol, wpack, mpack, vpack, w_ada_s, m_w_ada[0], v_w_ada[0])

    r_in, r_uq, r_ukv, r_out = _grad_exchange([
        _col_shards(gw_in), _col_shards(_unpad_heads(gw_uq_p)), _col_shards(gw_ukv),
        gw_out.reshape(N_DEV, D_MODEL // N_DEV, D_MODEL)])
    in_g, in_d, in_m, in_v = _adamw_reduce("adamw_w_in", r_in, w_in_s, m_w_in[0], v_w_in[0], 256)
    uq_g, uq_d, uq_m, uq_v = _adamw_reduce("adamw_w_uq", r_uq, w_uq_s, m_w_uq[0], v_w_uq[0], w_uq_s.shape[0])
    ukv_g, ukv_d, ukv_m, ukv_v = _adamw_reduce(
        "adamw_w_ukv", r_ukv, w_ukv_s, m_w_ukv[0], v_w_ukv[0], w_ukv_s.shape[0])
    out_g, out_d, out_m, out_v = _adamw_reduce(
        "adamw_w_out", r_out, w_out_s, m_w_out[0], v_w_out[0], w_out_s.shape[0])

    def group(ada_t, pk, in_t, uq_t, ukv_t, out_t):
        nw, qln, kvln, qhn, khn, b = _unpack(pk)
        return (ada_t[None], b, nw, in_t[None], qln, uq_t[None], kvln, ukv_t[None], qhn, khn, out_t[None])

    return (loss, grad_x.reshape(1, s_len, D_MODEL),
            *group(ada_g, pg, in_g, uq_g, ukv_g, out_g),
            *group(ada_d, pd, in_d, uq_d, ukv_d, out_d),
            *group(ada_m, pm, in_m, uq_m, ukv_m, out_m),
            *group(ada_v, pv, in_v, uq_v, ukv_v, out_v))
```

```python
import functools
import math

import jax
import jax.numpy as jnp
from jax import lax
from jax.experimental import pallas as pl
from jax.experimental.pallas import tpu as pltpu

F32 = jnp.float32
BF16 = jnp.bfloat16

N_DEV = 8
D_MODEL = 1024
HEADS = 8
SB_WIDTH = 512
MLA_WIDTH = 512
Q_LORA = 384
KV_LORA = 256
ROPE = 32
NOPE = 64
MLA_QK = 96
LANES = 128
IN_COLS = 3232
IN_COLS_R = 3328
EPS = 1e-6
NEG = -1e30

ADAM_LR = 0.001
ADAM_B1 = 0.9
ADAM_B2 = 0.999
ADAM_EPS = 1e-08
ADAM_WD = 0.01
ADAM_STEP = 10

TS = 256
TQ = 512
KEY_UNROLL = 2
TN_S = 512
VMEM_BIG = 56 * 1024 * 1024

PK_NORM, PK_QLN, PK_KVLN, PK_QHN, PK_KHN, PK_ADA, PK_LOSS, PK_END = 0, 1024, 1408, 1664, 1792, 1920, 4992, 6016

MESH_ID = pl.DeviceIdType.MESH


def _dot_nn(a, b):
    return lax.dot_general(a, b, (((1,), (0,)), ((), ())), preferred_element_type=F32)


def _dot_nt(a, b):
    return lax.dot_general(a, b, (((1,), (1,)), ((), ())), preferred_element_type=F32)


def _dot_tn(a, b):
    return lax.dot_general(a, b, (((0,), (0,)), ((), ())), preferred_element_type=F32)


def _split_bf16(a):
    hi = a.astype(BF16)
    lo = (a - hi.astype(F32)).astype(BF16)
    return hi, lo


def _dot3(a, b):
    ah, al = _split_bf16(a)
    bh, bl = _split_bf16(b)
    return _dot_nn(ah, bh) + _dot_nn(ah, bl) + _dot_nn(al, bh)


def _sigmoid(g):
    return 1.0 / (1.0 + jnp.exp(-g))


def _silu(g):
    return g * _sigmoid(g)


def _lane_iota(shape):
    return lax.broadcasted_iota(jnp.int32, shape, len(shape) - 1)


def _adamw(w, g, m, v):
    m = ADAM_B1 * m + (1.0 - ADAM_B1) * g
    v = ADAM_B2 * v + (1.0 - ADAM_B2) * (g * g)
    m_hat = m / (1.0 - ADAM_B1 ** ADAM_STEP)
    v_hat = v / (1.0 - ADAM_B2 ** ADAM_STEP)
    delta = -ADAM_LR * (m_hat / (jnp.sqrt(v_hat) + ADAM_EPS) + ADAM_WD * w)
    return delta, m, v


def _mesh_pos():
    return lax.axis_index("x"), lax.axis_index("y"), lax.axis_index("c")


def _peer(pos, k):
    x, y, c = pos
    return (1 - x if k & 4 else x, 1 - y if k & 2 else y, 1 - c if k & 1 else c)


def _lin(pos):
    return 4 * pos[0] + 2 * pos[1] + pos[2]


def _remote(src, dst, send_sems, recv_sems, idx, peer):
    return pltpu.make_async_remote_copy(
        src_ref=src, dst_ref=dst, send_sem=send_sems.at[idx], recv_sem=recv_sems.at[idx],
        device_id=peer, device_id_type=MESH_ID)


def _all_gather(pos, src, buf, send_sems, recv_sems, base):
    me = _lin(pos)
    sent = []
    for k in range(1, N_DEV):
        cp = _remote(src, buf.at[me], send_sems, recv_sems, base + k - 1, _peer(pos, k))
        cp.start()
        sent.append(cp)
    for k in range(1, N_DEV):
        peer = _peer(pos, k)
        _remote(src, buf.at[_lin(peer)], send_sems, recv_sems, base + k - 1, peer).wait_recv()
    for cp in sent:
        cp.wait_send()


def _all_to_all(pos, src, buf, send_sems, recv_sems, base):
    me = _lin(pos)
    sent = []
    for k in range(1, N_DEV):
        peer = _peer(pos, k)
        cp = _remote(src.at[_lin(peer)], buf.at[me], send_sems, recv_sems, base + k - 1, peer)
        cp.start()
        sent.append(cp)
    for k in range(1, N_DEV):
        peer = _peer(pos, k)
        _remote(src.at[me], buf.at[_lin(peer)], send_sems, recv_sems, base + k - 1, peer).wait_recv()
    for cp in sent:
        cp.wait_send()


def _vmem_spec():
    return pl.BlockSpec(memory_space=pltpu.VMEM)


def _any_spec():
    return pl.BlockSpec(memory_space=pl.ANY)


def _row_select(slots, n):
    r = lax.broadcasted_iota(jnp.int32, (N_DEV, n), 0)
    out = jnp.zeros((N_DEV, n), F32)
    for j in range(N_DEV):
        out = out + jnp.where(r == j, slots[j], 0.0)
    return out


def _ada_fwd(c8, w_ada, b_ada8):
    n_sh = w_ada.shape[1]

    def body(c_ref, w_ref, b_ref, out_ref, call_ref, psend_ref, precv_ref, ssem, rsem):
        pos = _mesh_pos()
        me = _lin(pos)
        call_ref[me] = c_ref[...]
        _all_gather(pos, c_ref, call_ref, ssem, rsem, 0)
        w = w_ref[...]
        for j in range(N_DEV):
            psend_ref[j] = _dot3(_silu(call_ref[j]), w)
        precv_ref[me] = psend_ref[me]
        _all_to_all(pos, psend_ref, precv_ref, ssem, rsem, N_DEV - 1)
        out_ref[...] = _row_select([precv_ref[j] for j in range(N_DEV)], n_sh) + b_ref[...]

    return pl.pallas_call(
        body, name="ada_fwd",
        out_shape=jax.ShapeDtypeStruct((N_DEV, n_sh), F32),
        in_specs=[_vmem_spec()] * 3, out_specs=_vmem_spec(),
        scratch_shapes=[
            pltpu.VMEM((N_DEV, 8, D_MODEL), F32),
            pltpu.VMEM((N_DEV, 8, n_sh), F32),
            pltpu.VMEM((N_DEV, 8, n_sh), F32),
            pltpu.SemaphoreType.DMA((2 * (N_DEV - 1),)),
            pltpu.SemaphoreType.DMA((2 * (N_DEV - 1),)),
        ],
    )(c8, w_ada, b_ada8)


def _gather_weights(shards):
    n = len(shards)

    def body(*refs):
        ins, outs = refs[:n], refs[n:2 * n]
        ssem, rsem = refs[2 * n], refs[2 * n + 1]
        pos = _mesh_pos()
        me = _lin(pos)
        for a in range(n):
            outs[a][me] = ins[a][...].astype(BF16)
        for a in range(n):
            _all_gather(pos, outs[a].at[me], outs[a], ssem, rsem, a * (N_DEV - 1))

    return pl.pallas_call(
        body, name="gather_weights",
        out_shape=[jax.ShapeDtypeStruct((N_DEV,) + s.shape, BF16) for s in shards],
        in_specs=[_vmem_spec()] * n, out_specs=[_vmem_spec()] * n,
        scratch_shapes=[
            pltpu.SemaphoreType.DMA((n * (N_DEV - 1),)),
            pltpu.SemaphoreType.DMA((n * (N_DEV - 1),)),
        ],
        compiler_params=pltpu.CompilerParams(vmem_limit_bytes=VMEM_BIG),
    )(*shards)


def _rope(t, cosf, sin_a, sin_b):
    return t * cosf + pltpu.roll(t, 112, 1) * sin_a + pltpu.roll(t, 16, 1) * sin_b


def _rope_t(d, cosf, sin_a, sin_b):
    return d * cosf + pltpu.roll(d * sin_a, 16, 1) + pltpu.roll(d * sin_b, 112, 1)


def _head_rms(t):
    return lax.rsqrt(jnp.sum(t * t, axis=1, keepdims=True) * (1.0 / MLA_QK) + EPS)


def _rms_bwd(dxhat_w, xhat, r, n):
    return r * (dxhat_w - xhat * (jnp.sum(dxhat_w * xhat, axis=1, keepdims=True) * (1.0 / n)))


def _mla_latents(cq, ckv, kr, qln, kvln, wuq, wukv):
    rq = lax.rsqrt(jnp.mean(cq * cq, axis=1, keepdims=True) + EPS)
    rkv = lax.rsqrt(jnp.mean(ckv * ckv, axis=1, keepdims=True) + EPS)
    cq_hat = cq * rq
    ckv_hat = ckv * rkv
    cqn = (cq_hat * qln).astype(BF16)
    ckvn = (ckv_hat * kvln).astype(BF16)
    q_all = _dot_nn(cqn, wuq)
    kv = _dot_nn(ckvn, wukv)
    kr64 = pltpu.roll(kr, 64, 1)
    return rq, rkv, cq_hat, ckv_hat, cqn, ckvn, q_all, kv, kr64


def _fwd_pre(x, shift, scale, norm_w, w_in_r, qln, wuq, kvln, wukv, qhn, khn, cosf, sin_a, sin_b):
    s_len = x.shape[0]

    def body(x_ref, shift_ref, scale_ref, nw_ref, w_ref, qln_ref, wuq_ref, kvln_ref, wukv_ref,
             qhn_ref, khn_ref, cos_ref, sa_ref, sb_ref,
             hb_ref, qkv_ref, gsb_ref, cq_ref, ckv_ref, gmla_ref, kr_ref, qm_ref, km_ref, vm_ref):
        xv = x_ref[...]
        r = lax.rsqrt(jnp.mean(xv * xv, axis=1, keepdims=True) + EPS)
        h = (xv * r) * nw_ref[...] * (1.0 + scale_ref[...]) + shift_ref[...]
        hb = h.astype(BF16)
        hb_ref[...] = hb
        qkv_ref[...] = _dot_nn(hb, w_ref[:, 0:1536]).astype(BF16)
        gsb_ref[...] = _dot_nn(hb, w_ref[:, 1536:2048])
        cq = _dot_nn(hb, w_ref[:, 2048:2432])
        ckv = _dot_nn(hb, w_ref[:, 2432:2688])
        gmla_ref[...] = _dot_nn(hb, w_ref[:, 2688:3200])
        kr = _dot_nn(hb, w_ref[:, 3200:3328])
        cq_ref[...] = cq
        ckv_ref[...] = ckv
        kr_ref[...] = kr
        _, _, _, _, _, _, q_all, kv, kr64 = _mla_latents(
            cq, ckv, kr, qln_ref[...], kvln_ref[...], wuq_ref[...], wukv_ref[...])
        cosf, sa, sb = cos_ref[...], sa_ref[...], sb_ref[...]
        qhn_v, khn_v = qhn_ref[...], khn_ref[...]
        low = _lane_iota((TS, LANES)) < NOPE
        for hd in range(HEADS):
            blk = slice(LANES * hd, LANES * (hd + 1))
            qb = q_all[:, blk]
            qm_ref[:, blk] = _rope(qb * _head_rms(qb) * qhn_v, cosf, sa, sb).astype(BF16)
            kb = jnp.where(low, kv[:, blk], kr64)
            km_ref[:, blk] = _rope(kb * _head_rms(kb) * khn_v, cosf, sa, sb).astype(BF16)
        for p in range(HEADS // 2):
            even = kv[:, LANES * 2 * p:LANES * (2 * p + 1)]
            odd = kv[:, LANES * (2 * p + 1):LANES * (2 * p + 2)]
            vm_ref[:, LANES * p:LANES * (p + 1)] = jnp.where(low, pltpu.roll(even, 64, 1), odd).astype(BF16)

    def tile(width):
        return pl.BlockSpec((TS, width), lambda i: (i, 0))

    def full(a):
        return pl.BlockSpec(a.shape, lambda i: (0, 0))

    out_widths = [(D_MODEL, BF16), (1536, BF16), (512, F32), (Q_LORA, F32), (KV_LORA, F32),
                  (512, F32), (LANES, F32), (1024, BF16), (1024, BF16), (512, BF16)]
    return pl.pallas_call(
        body, name="fwd_pre", grid=(s_len // TS,),
        out_shape=[jax.ShapeDtypeStruct((s_len, w), dt) for w, dt in out_widths],
        in_specs=[tile(D_MODEL), full(shift), full(scale), full(norm_w), full(w_in_r), full(qln), full(wuq),
                  full(kvln), full(wukv), full(qhn), full(khn), tile(LANES), tile(LANES), tile(LANES)],
        out_specs=[tile(w) for w, _ in out_widths],
        compiler_params=pltpu.CompilerParams(dimension_semantics=("parallel",), vmem_limit_bytes=VMEM_BIG),
    )(x, shift, scale, norm_w, w_in_r, qln, wuq, kvln, wukv, qhn, khn, cosf, sin_a, sin_b)


CUM_W = 256


def _tri(strict):
    j = lax.broadcasted_iota(jnp.int32, (CUM_W, CUM_W), 0)
    s = lax.broadcasted_iota(jnp.int32, (CUM_W, CUM_W), 1)
    return (j > s if strict else j >= s).astype(BF16)


def _suffix_sums(a, tri_m, carry):
    n = a.shape[1] // CUM_W
    outs = [None] * n
    for i in reversed(range(n)):
        blk = a[:, CUM_W * i:CUM_W * (i + 1)]
        hi, lo = _split_bf16(blk)
        outs[i] = _dot_nn(hi, tri_m) + _dot_nn(lo, tri_m) + carry
        carry = carry + _rowsum(blk)
    return (outs[0] if n == 1 else jnp.concatenate(outs, axis=1)), carry


def _sb_weights(qm, kb, carry, tri_u, diag):
    z = _dot_nt(qm, kb)
    nz = -z
    lk = jnp.minimum(nz, 0.0) - jnp.log(1.0 + jnp.exp(jnp.minimum(z, nz)))
    if diag:
        t = lax.broadcasted_iota(jnp.int32, (TQ, TQ), 0)
        s = lax.broadcasted_iota(jnp.int32, (TQ, TQ), 1)
        valid = s < t
        lk = jnp.where(valid, lk, 0.0)
    after, carry = _suffix_sums(lk, tri_u, carry)
    logw = z + lk + after
    if diag:
        logw = jnp.where(valid, logw, NEG)
    return lk, jnp.exp(logw), carry


SB_SCALE = 0.125


def _head_masks():
    lane = _lane_iota((1, LANES))
    return [lane < 64, lane >= 64]


def _masked(hm, a):
    return jnp.where(hm, a, jnp.zeros_like(a))


def _rowsum(a):
    return jnp.sum(a, axis=1, keepdims=True)


def _key_rows(kj):
    return pl.ds(pl.multiple_of(kj * TQ, TQ), TQ)


def _over_keys_below(qi, fn, st, ascending):
    n_full = qi // KEY_UNROLL
    n_rest = qi - n_full * KEY_UNROLL

    def group(g, s_):
        for u in range(KEY_UNROLL):
            s_ = fn(_key_rows(g * KEY_UNROLL + (u if ascending else KEY_UNROLL - 1 - u)), s_)
        return s_

    if ascending:
        st = lax.fori_loop(0, n_full, group, st)
        return lax.fori_loop(0, n_rest, lambda i, s_: fn(_key_rows(n_full * KEY_UNROLL + i), s_), st)
    st = lax.fori_loop(0, n_rest, lambda i, s_: fn(_key_rows(qi - 1 - i), s_), st)
    return lax.fori_loop(0, n_full, lambda i, s_: group(n_full - 1 - i, s_), st)


def _sb_fwd(qkv):
    s_len = qkv.shape[0]

    def body(q_ref, k_ref, v_ref, o_ref):
        qi = pl.program_id(1)
        q = q_ref[...]
        tri_u = _tri(True)
        masks = _head_masks()
        qms = [_masked(hm, q) * SB_SCALE for hm in masks]

        def block(rows, st, diag):
            kb, vb = k_ref[rows, :], v_ref[rows, :]
            carries, acc = list(st[:2]), st[2]
            for h in range(2):
                _, w, carries[h] = _sb_weights(qms[h], kb, carries[h], tri_u, diag)
                acc = acc + _dot_nn(w.astype(BF16), _masked(masks[h], vb))
            return carries[0], carries[1], acc

        zc = jnp.zeros((TQ, 1), F32)
        st = block(_key_rows(qi), (zc, zc, jnp.zeros((TQ, LANES), F32)), True)
        st = _over_keys_below(qi, lambda rows, s_: block(rows, s_, False), st, ascending=False)
        o_ref[...] = st[2]

    return pl.pallas_call(
        body, name="sb_fwd", grid=(HEADS // 2, s_len // TQ),
        out_shape=jax.ShapeDtypeStruct((s_len, SB_WIDTH), F32),
        in_specs=[pl.BlockSpec((TQ, LANES), lambda p, i: (i, p)),
                  pl.BlockSpec((s_len, LANES), lambda p, i: (0, 4 + p)),
                  pl.BlockSpec((s_len, LANES), lambda p, i: (0, 8 + p))],
        out_specs=pl.BlockSpec((TQ, LANES), lambda p, i: (i, p)),
        compiler_params=pltpu.CompilerParams(dimension_semantics=("parallel", "parallel"),
                                             vmem_limit_bytes=VMEM_BIG),
    )(qkv, qkv, qkv)


def _sb_bwd(qkv, o, do):
    s_len = qkv.shape[0]

    def body(q_ref, k_ref, v_ref, o_ref, do_ref, dq_ref, dk_ref, dv_ref):
        qi = pl.program_id(1)

        @pl.when(qi == 0)
        def _():
            dk_ref[...] = jnp.zeros_like(dk_ref)
            dv_ref[...] = jnp.zeros_like(dv_ref)

        q = q_ref[...]
        do_v = do_ref[...]
        od = o_ref[...] * do_v.astype(F32)
        tri_u = _tri(True)
        tri_i = _tri(False)
        masks = _head_masks()
        qms = [_masked(hm, q) * SB_SCALE for hm in masks]
        doms = [_masked(hm, do_v) for hm in masks]
        d_tots = [_rowsum(jnp.where(hm, od, 0.0)) for hm in masks]

        def block(rows, st, diag):
            kb, vb = k_ref[rows, :], v_ref[rows, :]
            carries, carries_d, dqs = list(st[0:2]), list(st[2:4]), list(st[4:6])
            dk_blk = jnp.zeros((TQ, LANES), F32)
            dv_blk = jnp.zeros((TQ, LANES), F32)
            for h in range(2):
                lk, w, carries[h] = _sb_weights(qms[h], kb, carries[h], tri_u, diag)
                wb = w.astype(BF16)
                d_l = _dot_nt(doms[h], vb) * wb.astype(F32)
                from_here, carries_d[h] = _suffix_sums(d_l, tri_i, carries_d[h])
                before = d_tots[h] - from_here
                keep = jnp.exp(lk)
                dz = d_l * keep - before * (1.0 - keep)
                if diag:
                    t = lax.broadcasted_iota(jnp.int32, (TQ, TQ), 0)
                    s = lax.broadcasted_iota(jnp.int32, (TQ, TQ), 1)
                    dz = jnp.where(s < t, dz, 0.0)
                dzb = dz.astype(BF16)
                dk_blk = dk_blk + _dot_tn(dzb, qms[h])
                dv_blk = dv_blk + _dot_tn(wb, doms[h])
                dqs[h] = dqs[h] + _dot_nn(dzb, kb)
            dk_ref[rows, :] += dk_blk
            dv_ref[rows, :] += dv_blk
            return (*carries, *carries_d, *dqs)

        zc = jnp.zeros((TQ, 1), F32)
        za = jnp.zeros((TQ, LANES), F32)
        st = block(_key_rows(qi), (zc, zc, zc, zc, za, za), True)
        st = _over_keys_below(qi, lambda rows, s_: block(rows, s_, False), st, ascending=False)
        dq_ref[...] = jnp.where(masks[0], st[4], st[5]) * SB_SCALE

    tile = pl.BlockSpec((TQ, LANES), lambda p, i: (i, p))
    col = pl.BlockSpec((s_len, LANES), lambda p, i: (0, p))
    return pl.pallas_call(
        body, name="sb_bwd", grid=(HEADS // 2, s_len // TQ),
        out_shape=[jax.ShapeDtypeStruct((s_len, SB_WIDTH), F32)] * 3,
        in_specs=[tile,
                  pl.BlockSpec((s_len, LANES), lambda p, i: (0, 4 + p)),
                  pl.BlockSpec((s_len, LANES), lambda p, i: (0, 8 + p)),
                  tile, tile],
        out_specs=[tile, col, col],
        compiler_params=pltpu.CompilerParams(dimension_semantics=("parallel", "arbitrary"),
                                             vmem_limit_bytes=VMEM_BIG),
    )(qkv, qkv, qkv, o, do)


MLA_SCALE = 1.0 / math.sqrt(MLA_QK)


def _causal_mask():
    t = lax.broadcasted_iota(jnp.int32, (TQ, TQ), 0)
    s = lax.broadcasted_iota(jnp.int32, (TQ, TQ), 1)
    return s <= t


def _head_lanes(h):
    return slice(LANES * h, LANES * (h + 1))


def _mla_fwd(qm, km, vm):
    s_len = qm.shape[0]

    def body(q_ref, k_ref, v_ref, o_ref, lse_ref):
        qi = pl.program_id(1)
        masks = _head_masks()
        qhs = [q_ref[:, _head_lanes(h)] for h in range(2)]

        def block(rows, st, diag):
            vb = v_ref[rows, :]
            ms, ls, acc = list(st[0:2]), list(st[2:4]), st[4]
            alphas, pvs = [], []
            for h in range(2):
                s = _dot_nt(qhs[h], k_ref[rows, _head_lanes(h)]) * MLA_SCALE
                if diag:
                    s = jnp.where(_causal_mask(), s, NEG)
                m_new = jnp.maximum(ms[h], jnp.max(s, axis=1, keepdims=True))
                p = jnp.exp(s - m_new)
                alphas.append(jnp.exp(ms[h] - m_new))
                ls[h] = alphas[h] * ls[h] + _rowsum(p)
                ms[h] = m_new
                pvs.append(_dot_nn(p.astype(BF16), _masked(masks[h], vb)))
            acc = jnp.where(masks[0], alphas[0], alphas[1]) * acc + pvs[0] + pvs[1]
            return (*ms, *ls, acc)

        neg = jnp.full((TQ, 1), NEG, F32)
        zc = jnp.zeros((TQ, 1), F32)
        st = (neg, neg, zc, zc, jnp.zeros((TQ, LANES), F32))
        st = _over_keys_below(qi, lambda rows, s_: block(rows, s_, False), st, ascending=True)
        m0, m1, l0, l1, acc = block(_key_rows(qi), st, True)
        o_ref[...] = acc / jnp.where(masks[0], l0, l1)
        lse_ref[0] = m0 + jnp.log(l0)
        lse_ref[1] = m1 + jnp.log(l1)

    return pl.pallas_call(
        body, name="mla_fwd", grid=(HEADS // 2, s_len // TQ),
        out_shape=[jax.ShapeDtypeStruct((s_len, MLA_WIDTH), F32),
                   jax.ShapeDtypeStruct((HEADS, s_len, 1), F32)],
        in_specs=[pl.BlockSpec((TQ, 2 * LANES), lambda p, i: (i, p)),
                  pl.BlockSpec((s_len, 2 * LANES), lambda p, i: (0, p)),
                  pl.BlockSpec((s_len, LANES), lambda p, i: (0, p))],
        out_specs=[pl.BlockSpec((TQ, LANES), lambda p, i: (i, p)),
                   pl.BlockSpec((2, TQ, 1), lambda p, i: (p, i, 0))],
        compiler_params=pltpu.CompilerParams(dimension_semantics=("parallel", "parallel"),
                                             vmem_limit_bytes=VMEM_BIG),
    )(qm, km, vm)


def _mla_bwd(qm, km, vm, o, do, lse):
    s_len = qm.shape[0]

    def body(q_ref, k_ref, v_ref, o_ref, do_ref, lse_ref, dq_ref, dk_ref, dv_ref):
        qi = pl.program_id(1)

        @pl.when(qi == 0)
        def _():
            dk_ref[...] = jnp.zeros_like(dk_ref)
            dv_ref[...] = jnp.zeros_like(dv_ref)

        do_v = do_ref[...]
        od = o_ref[...] * do_v.astype(F32)
        masks = _head_masks()
        qhs = [q_ref[:, _head_lanes(h)] for h in range(2)]
        doms = [_masked(hm, do_v) for hm in masks]
        deltas = [_rowsum(jnp.where(hm, od, 0.0)) for hm in masks]
        lses = [lse_ref[h] for h in range(2)]

        def block(rows, dqs, diag):
            vb = v_ref[rows, :]
            dqs = list(dqs)
            dv_blk = jnp.zeros((TQ, LANES), F32)
            for h in range(2):
                kb = k_ref[rows, _head_lanes(h)]
                s = _dot_nt(qhs[h], kb) * MLA_SCALE
                if diag:
                    s = jnp.where(_causal_mask(), s, NEG)
                p = jnp.exp(s - lses[h])
                dp = _dot_nt(doms[h], vb)
                ds = (p * (dp - deltas[h]) * MLA_SCALE).astype(BF16)
                dk_ref[rows, _head_lanes(h)] += _dot_tn(ds, qhs[h])
                dv_blk = dv_blk + _dot_tn(p.astype(BF16), doms[h])
                dqs[h] = dqs[h] + _dot_nn(ds, kb)
            dv_ref[rows, :] += dv_blk
            return tuple(dqs)

        za = jnp.zeros((TQ, LANES), F32)
        dqs = _over_keys_below(qi, lambda rows, a: block(rows, a, False), (za, za), ascending=True)
        dqs = block(_key_rows(qi), dqs, True)
        dq_ref[:, _head_lanes(0)] = dqs[0]
        dq_ref[:, _head_lanes(1)] = dqs[1]

    return pl.pallas_call(
        body, name="mla_bwd", grid=(HEADS // 2, s_len // TQ),
        out_shape=[jax.ShapeDtypeStruct((s_len, HEADS * LANES), F32),
                   jax.ShapeDtypeStruct((s_len, HEADS * LANES), F32),
                   jax.ShapeDtypeStruct((s_len, MLA_WIDTH), F32)],
        in_specs=[pl.BlockSpec((TQ, 2 * LANES), lambda p, i: (i, p)),
                  pl.BlockSpec((s_len, 2 * LANES), lambda p, i: (0, p)),
                  pl.BlockSpec((s_len, LANES), lambda p, i: (0, p)),
                  pl.BlockSpec((TQ, LANES), lambda p, i: (i, p)),
                  pl.BlockSpec((TQ, LANES), lambda p, i: (i, p)),
                  pl.BlockSpec((2, TQ, 1), lambda p, i: (p, i, 0))],
        out_specs=[pl.BlockSpec((TQ, 2 * LANES), lambda p, i: (i, p)),
                   pl.BlockSpec((s_len, 2 * LANES), lambda p, i: (0, p)),
                   pl.BlockSpec((s_len, LANES), lambda p, i: (0, p))],
        compiler_params=pltpu.CompilerParams(dimension_semantics=("parallel", "arbitrary"),
                                             vmem_limit_bytes=VMEM_BIG),
    )(qm, km, vm, o, do, lse)


def _mid(o_sb, g_sb, o_mla, g_mla, x, target, gate, w_out):
    s_len = x.shape[0]

    def body(osb_ref, gsb_ref, omla_ref, gmla_ref, x_ref, t_ref, gate_ref, w_ref,
             dy_ref, dosb_ref, dgsb_ref, domla_ref, dgmla_ref, gw_ref, dgate_ref, loss_ref):
        @pl.when(pl.program_id(0) == 0)
        def _():
            gw_ref[...] = jnp.zeros_like(gw_ref)
            dgate_ref[...] = jnp.zeros_like(dgate_ref)
            loss_ref[...] = jnp.zeros_like(loss_ref)

        o1, g1, o2, g2 = osb_ref[...], gsb_ref[...], omla_ref[...], gmla_ref[...]
        s1, s2 = _sigmoid(g1), _sigmoid(g2)
        mixed = jnp.concatenate([o1 * (g1 * s1), o2 * (g2 * s2)], axis=1).astype(BF16)
        w = w_ref[...]
        gate_v = gate_ref[...]
        u = _dot_nn(mixed, w)
        err = x_ref[...] + gate_v * u - t_ref[...]
        loss_ref[...] += jnp.sum(err * err, axis=0, keepdims=True)
        dy = err * (1.0 / D_MODEL)
        dy_ref[...] = dy
        dgate_ref[...] += jnp.sum(dy * u, axis=0, keepdims=True)
        du = (dy * gate_v).astype(BF16)
        gw_ref[...] += _dot_tn(mixed, du)
        dmixed = _dot_nt(du, w)
        d1, d2 = dmixed[:, :SB_WIDTH], dmixed[:, SB_WIDTH:]
        dosb_ref[...] = (d1 * (g1 * s1)).astype(BF16)
        dgsb_ref[...] = (d1 * o1 * (s1 * (1.0 + g1 * (1.0 - s1)))).astype(BF16)
        domla_ref[...] = (d2 * (g2 * s2)).astype(BF16)
        dgmla_ref[...] = (d2 * o2 * (s2 * (1.0 + g2 * (1.0 - s2)))).astype(BF16)

    def tile(width):
        return pl.BlockSpec((TS, width), lambda i: (i, 0))

    def full(shape):
        return pl.BlockSpec(shape, lambda i: (0, 0))

    return pl.pallas_call(
        body, name="mid", grid=(s_len // TS,),
        out_shape=[jax.ShapeDtypeStruct((s_len, D_MODEL), F32)]
        + [jax.ShapeDtypeStruct((s_len, 512), BF16)] * 4
        + [jax.ShapeDtypeStruct((D_MODEL, D_MODEL), F32),
           jax.ShapeDtypeStruct((1, D_MODEL), F32), jax.ShapeDtypeStruct((1, D_MODEL), F32)],
        in_specs=[tile(512)] * 4 + [tile(D_MODEL), tile(D_MODEL), full((1, D_MODEL)), full((D_MODEL, D_MODEL))],
        out_specs=[tile(D_MODEL)] + [tile(512)] * 4
        + [full((D_MODEL, D_MODEL)), full((1, D_MODEL)), full((1, D_MODEL))],
        compiler_params=pltpu.CompilerParams(dimension_semantics=("arbitrary",), vmem_limit_bytes=VMEM_BIG),
    )(o_sb, g_sb, o_mla, g_mla, x, target, gate, w_out)


def _mla_pre_bwd(dq, dk, dv, cq, ckv, kr, qln, wuq, kvln, wukv, qhn, khn, cosf, sin_a, sin_b):
    s_len = cq.shape[0]

    def body(dq_ref, dk_ref, dv_ref, cq_ref, ckv_ref, kr_ref, qln_ref, wuq_ref, kvln_ref, wukv_ref,
             qhn_ref, khn_ref, cos_ref, sa_ref, sb_ref,
             dcq_ref, dckv_ref, dkr_ref, gwuq_ref, gwukv_ref, gqhn_ref, gkhn_ref, gqln_ref, gkvln_ref,
             dqa_ref, dkv_ref):
        @pl.when(pl.program_id(0) == 0)
        def _():
            for r_ in (gwuq_ref, gwukv_ref, gqhn_ref, gkhn_ref, gqln_ref, gkvln_ref):
                r_[...] = jnp.zeros_like(r_)

        cq, ckv = cq_ref[...], ckv_ref[...]
        qln_v, kvln_v = qln_ref[...], kvln_ref[...]
        wuq_v, wukv_v = wuq_ref[...], wukv_ref[...]
        rq, rkv, cq_hat, ckv_hat, cqn, ckvn, q_all, kv, kr64 = _mla_latents(
            cq, ckv, kr_ref[...], qln_v, kvln_v, wuq_v, wukv_v)
        cosf, sa, sb = cos_ref[...], sa_ref[...], sb_ref[...]
        qhn_v, khn_v = qhn_ref[...], khn_ref[...]
        lane = _lane_iota((TS, LANES))
        low = lane < NOPE
        g_qhn = jnp.zeros((1, LANES), F32)
        g_khn = jnp.zeros((1, LANES), F32)
        dkr64 = jnp.zeros((TS, LANES), F32)
        for hd in range(HEADS):
            blk = slice(LANES * hd, LANES * (hd + 1))
            qb = q_all[:, blk]
            r = _head_rms(qb)
            xh = qb * r
            dn = _rope_t(dq_ref[:, blk], cosf, sa, sb)
            g_qhn = g_qhn + jnp.sum(dn * xh, axis=0, keepdims=True)
            dqa_ref[:, blk] = _rms_bwd(dn * qhn_v, xh, r, MLA_QK).astype(BF16)

            kb = jnp.where(low, kv[:, blk], kr64)
            r = _head_rms(kb)
            xh = kb * r
            dn = _rope_t(dk_ref[:, blk], cosf, sa, sb)
            g_khn = g_khn + jnp.sum(dn * xh, axis=0, keepdims=True)
            dkb = _rms_bwd(dn * khn_v, xh, r, MLA_QK)
            dkr64 = dkr64 + jnp.where(low, 0.0, dkb)
            dvp = dv_ref[:, LANES * (hd // 2):LANES * (hd // 2 + 1)]
            dvh = pltpu.roll(dvp, 64, 1) if hd % 2 == 0 else dvp
            dkv_ref[:, blk] = jnp.where(low, dkb, dvh).astype(BF16)
        gqhn_ref[...] += g_qhn
        gkhn_ref[...] += g_khn
        dkr_ref[...] = pltpu.roll(dkr64, 64, 1).astype(BF16)

        dqa = dqa_ref[...]
        gwuq_ref[...] += _dot_tn(cqn, dqa)
        dcqn = _dot_nt(dqa, wuq_v)
        gqln_ref[...] += jnp.sum(dcqn * cq_hat, axis=0, keepdims=True)
        dcq_ref[...] = _rms_bwd(dcqn * qln_v, cq_hat, rq, Q_LORA).astype(BF16)

        dkv = dkv_ref[...]
        gwukv_ref[...] += _dot_tn(ckvn, dkv)
        dckvn = _dot_nt(dkv, wukv_v)
        gkvln_ref[...] += jnp.sum(dckvn * ckv_hat, axis=0, keepdims=True)
        dckv_ref[...] = _rms_bwd(dckvn * kvln_v, ckv_hat, rkv, KV_LORA).astype(BF16)

    def tile(width):
        return pl.BlockSpec((TS, width), lambda i: (i, 0))

    def full(shape):
        return pl.BlockSpec(shape, lambda i: (0, 0))

    acc_shapes = [(Q_LORA, 1024), (KV_LORA, 1024), (1, LANES), (1, LANES), (1, Q_LORA), (1, KV_LORA)]
    return pl.pallas_call(
        body, name="mla_pre_bwd", grid=(s_len // TS,),
        out_shape=[jax.ShapeDtypeStruct((s_len, Q_LORA), BF16), jax.ShapeDtypeStruct((s_len, KV_LORA), BF16),
                   jax.ShapeDtypeStruct((s_len, LANES), BF16)]
        + [jax.ShapeDtypeStruct(s, F32) for s in acc_shapes],
        in_specs=[tile(1024), tile(1024), tile(512), tile(Q_LORA), tile(KV_LORA), tile(LANES),
                  full(qln.shape), full(wuq.shape), full(kvln.shape), full(wukv.shape),
                  full(qhn.shape), full(khn.shape), tile(LANES), tile(LANES), tile(LANES)],
        out_specs=[tile(Q_LORA), tile(KV_LORA), tile(LANES)] + [full(s) for s in acc_shapes],
        scratch_shapes=[pltpu.VMEM((TS, 1024), BF16), pltpu.VMEM((TS, 1024), BF16)],
        compiler_params=pltpu.CompilerParams(dimension_semantics=("arbitrary",), vmem_limit_bytes=VMEM_BIG),
    )(dq, dk, dv, cq, ckv, kr, qln, wuq, kvln, wukv, qhn, khn, cosf, sin_a, sin_b)


def _dproj_bwd(dq_sb, dk_sb, dv_sb, dg_sb, dcq, dckv, dg_mla, dkr, w_in_r, x, dy, norm_w, scale):
    s_len = x.shape[0]

    def body(dq_ref, dk_ref, dv_ref, dg_ref, dcq_ref, dckv_ref, dgm_ref, dkr_ref, w_ref, x_ref, dy_ref,
             nw_ref, scale_ref, dp_ref, gx_ref, dshift_ref, dscale_ref, dnw_ref):
        @pl.when(pl.program_id(0) == 0)
        def _():
            for r_ in (dshift_ref, dscale_ref, dnw_ref):
                r_[...] = jnp.zeros_like(r_)

        dp_ref[:, 0:512] = dq_ref[...].astype(BF16)
        dp_ref[:, 512:1024] = dk_ref[...].astype(BF16)
        dp_ref[:, 1024:1536] = dv_ref[...].astype(BF16)
        dp_ref[:, 1536:2048] = dg_ref[...]
        dp_ref[:, 2048:2432] = dcq_ref[...]
        dp_ref[:, 2432:2688] = dckv_ref[...]
        dp_ref[:, 2688:3200] = dgm_ref[...]
        dp_ref[:, 3200:3328] = dkr_ref[...]
        dh = _dot_nt(dp_ref[...], w_ref[...])
        xv = x_ref[...]
        r = lax.rsqrt(jnp.mean(xv * xv, axis=1, keepdims=True) + EPS)
        xh = xv * r
        nw = nw_ref[...]
        dshift_ref[...] += jnp.sum(dh, axis=0, keepdims=True)
        dscale_ref[...] += jnp.sum(dh * (xh * nw), axis=0, keepdims=True)
        dxnw = dh * (1.0 + scale_ref[...])
        dnw_ref[...] += jnp.sum(dxnw * xh, axis=0, keepdims=True)
        gx_ref[...] = dy_ref[...] + _rms_bwd(dxnw * nw, xh, r, D_MODEL)

    def tile(width):
        return pl.BlockSpec((TS, width), lambda i: (i, 0))

    def full(shape):
        return pl.BlockSpec(shape, lambda i: (0, 0))

    vec = (1, D_MODEL)
    return pl.pallas_call(
        body, name="dproj_bwd", grid=(s_len // TS,),
        out_shape=[jax.ShapeDtypeStruct((s_len, IN_COLS_R), BF16), jax.ShapeDtypeStruct((s_len, D_MODEL), F32)]
        + [jax.ShapeDtypeStruct(vec, F32)] * 3,
        in_specs=[tile(512)] * 4 + [tile(Q_LORA), tile(KV_LORA), tile(512), tile(LANES),
                                    full(w_in_r.shape), tile(D_MODEL), tile(D_MODEL), full(vec), full(vec)],
        out_specs=[tile(IN_COLS_R), tile(D_MODEL)] + [full(vec)] * 3,
        compiler_params=pltpu.CompilerParams(dimension_semantics=("arbitrary",), vmem_limit_bytes=VMEM_BIG),
    )(dq_sb, dk_sb, dv_sb, dg_sb, dcq, dckv, dg_mla, dkr, w_in_r, x, dy, norm_w, scale)


def _grad_w_in(hb, dproj):
    s_len = hb.shape[0]
    n_half = IN_COLS_R // 2

    def body(h_ref, d_ref, g_ref):
        @pl.when(pl.program_id(1) == 0)
        def _():
            g_ref[...] = jnp.zeros_like(g_ref)

        g_ref[...] += _dot_tn(h_ref[...], d_ref[...])

    return pl.pallas_call(
        body, name="grad_w_in", grid=(2, s_len // TN_S),
        out_shape=jax.ShapeDtypeStruct((D_MODEL, IN_COLS_R), F32),
        in_specs=[pl.BlockSpec((TN_S, D_MODEL), lambda n, s: (s, 0)),
                  pl.BlockSpec((TN_S, n_half), lambda n, s: (s, n))],
        out_specs=pl.BlockSpec((D_MODEL, n_half), lambda n, s: (0, n)),
        compiler_params=pltpu.CompilerParams(dimension_semantics=("parallel", "arbitrary"),
                                             vmem_limit_bytes=VMEM_BIG),
    )(hb, dproj)


def _vec_exchange(gpack, ccol, wpack, mpack, vpack, w_ada, m_ada, v_ada):
    n_sh = w_ada.shape[1]

    def body(g_ref, cc_ref, wp_ref, mp_ref, vp_ref, wa_ref, ma_ref, va_ref,
             og_ref, od_ref, om_ref, ov_ref, ag_ref, ad_ref, am_ref, av_ref,
             gall_ref, call_ref, ssem, rsem):
        pos = _mesh_pos()
        me = _lin(pos)
        gall_ref[me] = g_ref[...]
        call_ref[me] = cc_ref[...]
        _all_gather(pos, g_ref, gall_ref, ssem, rsem, 0)
        _all_gather(pos, cc_ref, call_ref, ssem, rsem, N_DEV - 1)

        tot = gall_ref[0]
        for j in range(1, N_DEV):
            tot = tot + gall_ref[j]
        og_ref[...] = tot
        od_ref[...], om_ref[...], ov_ref[...] = _adamw(wp_ref[...], tot, mp_ref[...], vp_ref[...])

        ga = jnp.zeros((D_MODEL, n_sh), F32)
        for j in range(N_DEV):
            d_mine = jnp.zeros((8, n_sh), F32)
            for k in range(N_DEV):
                d_mine = d_mine + jnp.where(me == k, gall_ref[j, :, PK_ADA + n_sh * k:PK_ADA + n_sh * (k + 1)], 0.0)
            col = _silu(call_ref[j])
            ga = ga + jnp.concatenate(
                [col * d_mine[0:1, LANES * a:LANES * (a + 1)] for a in range(n_sh // LANES)], axis=1)
        ag_ref[...] = ga
        ad_ref[...], am_ref[...], av_ref[...] = _adamw(wa_ref[...], ga, ma_ref[...], va_ref[...])

    pk = jax.ShapeDtypeStruct((8, PK_END), F32)
    ada = jax.ShapeDtypeStruct((D_MODEL, n_sh), F32)
    return pl.pallas_call(
        body, name="vec_exchange",
        out_shape=[pk] * 4 + [ada] * 4,
        in_specs=[_vmem_spec()] * 8, out_specs=[_vmem_spec()] * 8,
        scratch_shapes=[
            pltpu.VMEM((N_DEV, 8, PK_END), F32),
            pltpu.VMEM((N_DEV, D_MODEL, LANES), F32),
            pltpu.SemaphoreType.DMA((2 * (N_DEV - 1),)),
            pltpu.SemaphoreType.DMA((2 * (N_DEV - 1),)),
        ],
        compiler_params=pltpu.CompilerParams(vmem_limit_bytes=VMEM_BIG),
    )(gpack, ccol, wpack, mpack, vpack, w_ada, m_ada, v_ada)


def _grad_exchange(grads):
    n = len(grads)

    def body(*refs):
        ins, outs = refs[:n], refs[n:2 * n]
        ssem, rsem, lsem = refs[2 * n], refs[2 * n + 1], refs[2 * n + 2]
        pos = _mesh_pos()
        me = _lin(pos)
        own = [pltpu.make_async_copy(ins[a].at[me], outs[a].at[me], lsem.at[a]) for a in range(n)]
        for cp in own:
            cp.start()
        for a in range(n):
            _all_to_all(pos, ins[a], outs[a], ssem, rsem, a * (N_DEV - 1))
        for cp in own:
            cp.wait()

    return pl.pallas_call(
        body, name="grad_exchange",
        out_shape=[jax.ShapeDtypeStruct(g.shape, g.dtype) for g in grads],
        in_specs=[_any_spec()] * n, out_specs=[_any_spec()] * n,
        scratch_shapes=[
            pltpu.SemaphoreType.DMA((n * (N_DEV - 1),)),
            pltpu.SemaphoreType.DMA((n * (N_DEV - 1),)),
            pltpu.SemaphoreType.DMA((n,)),
        ],
    )(*grads)


def _adamw_reduce(name, parts, w, m, v, row_tile):
    rows, cols = w.shape

    def body(p_ref, w_ref, m_ref, v_ref, g_ref, d_ref, mo_ref, vo_ref):
        g = p_ref[0].astype(F32)
        for j in range(1, N_DEV):
            g = g + p_ref[j].astype(F32)
        g_ref[...] = g
        d_ref[...], mo_ref[...], vo_ref[...] = _adamw(w_ref[...], g, m_ref[...], v_ref[...])

    tile = pl.BlockSpec((row_tile, cols), lambda i: (i, 0))
    return pl.pallas_call(
        body, name=name, grid=(rows // row_tile,),
        out_shape=[jax.ShapeDtypeStruct((rows, cols), F32)] * 4,
        in_specs=[pl.BlockSpec((N_DEV, row_tile, cols), lambda i: (0, i, 0)), tile, tile, tile],
        out_specs=[tile] * 4,
        compiler_params=pltpu.CompilerParams(dimension_semantics=("parallel",), vmem_limit_bytes=VMEM_BIG),
    )(parts, w, m, v)


def _rope_tables(positions):
    inv_freq = 10000.0 ** (-jnp.arange(0, ROPE, 2, dtype=F32) / ROPE)
    ang = positions.astype(F32)[:, None] * inv_freq
    cos, sin = jnp.cos(ang), jnp.sin(ang)
    s_len = positions.shape[0]
    ones = jnp.ones((s_len, NOPE), F32)
    zeros = jnp.zeros((s_len, NOPE), F32)
    z16 = jnp.zeros((s_len, ROPE // 2), F32)
    pad1 = jnp.ones((s_len, LANES - MLA_QK), F32)
    pad0 = jnp.zeros((s_len, LANES - MLA_QK), F32)
    cosf = jnp.concatenate([ones, cos, cos, pad1], axis=1)
    sin_a = jnp.concatenate([zeros, -sin, z16, pad0], axis=1)
    sin_b = jnp.concatenate([zeros, z16, sin, pad0], axis=1)
    return cosf, sin_a, sin_b


def _rearrange_cols(w):
    pad = jnp.zeros((w.shape[0], IN_COLS_R - IN_COLS), w.dtype)
    return jnp.concatenate([w[:, :2688], w[:, 2720:3232], w[:, 2688:2720], pad], axis=1)


def _restore_cols(g):
    return jnp.concatenate([g[:, :2688], g[:, 3200:3232], g[:, 2688:3200]], axis=1)


def _pad_heads(w):
    rows = w.shape[0]
    w = w.reshape(rows, HEADS, MLA_QK)
    return jnp.pad(w, ((0, 0), (0, 0), (0, LANES - MLA_QK))).reshape(rows, HEADS * LANES)


def _unpad_heads(g):
    rows = g.shape[0]
    return g.reshape(rows, HEADS, LANES)[:, :, :MLA_QK].reshape(rows, HEADS * MLA_QK)


def _pad_lanes(v):
    return jnp.pad(v, ((0, 0), (0, LANES - v.shape[1])))


def _col_shards(g):
    rows = g.shape[0]
    return g.reshape(rows, N_DEV, g.shape[1] // N_DEV).transpose(1, 0, 2)


def _from_col_shards(g):
    return g.transpose(1, 0, 2).reshape(g.shape[1], N_DEV * g.shape[2])


def _pack(norm_w, qln, kvln, qhn, khn, ada, loss_lanes=None):
    if loss_lanes is None:
        loss_lanes = jnp.zeros((1, PK_END - PK_LOSS), F32)
    row = jnp.concatenate([norm_w, qln, kvln, _pad_lanes(qhn), _pad_lanes(khn), ada, loss_lanes], axis=1)
    return jnp.broadcast_to(row, (8, PK_END))


def _unpack(p):
    row = p[0:1]
    return (row[:, PK_NORM:PK_QLN], row[:, PK_QLN:PK_KVLN], row[:, PK_KVLN:PK_QHN],
            row[:, PK_QHN:PK_QHN + MLA_QK], row[:, PK_KHN:PK_KHN + MLA_QK], row[:, PK_ADA:PK_LOSS])


def kernel(x, c, positions, w_ada, b_ada, norm_w, w_in, q_lora_norm, w_uq, kv_lora_norm, w_ukv, q_head_norm, k_head_norm, w_out, loss_target, m_w_ada, m_b_ada, m_norm_w, m_w_in, m_q_lora_norm, m_w_uq, m_kv_lora_norm, m_w_ukv, m_q_head_norm, m_k_head_norm, m_w_out, v_w_ada, v_b_ada, v_norm_w, v_w_in, v_q_lora_norm, v_w_uq, v_kv_lora_norm, v_w_ukv, v_q_head_norm, v_k_head_norm, v_w_out):
    s_len = x.shape[1]
    x2 = x.reshape(s_len, D_MODEL)
    tgt = loss_target.reshape(s_len, D_MODEL)
    w_ada_s, w_in_s, w_uq_s, w_ukv_s, w_out_s = w_ada[0], w_in[0], w_uq[0], w_ukv[0], w_out[0]

    ada8 = _ada_fwd(jnp.broadcast_to(c, (8, D_MODEL)), w_ada_s, b_ada.reshape(N_DEV, -1))
    ada = ada8.reshape(1, 3 * D_MODEL)
    shift, scale, gate = ada[:, :D_MODEL], ada[:, D_MODEL:2 * D_MODEL], ada[:, 2 * D_MODEL:]

    g_in, g_uq, g_ukv, g_out = _gather_weights([w_in_s, w_uq_s, w_ukv_s, w_out_s])
    w_in_r = _rearrange_cols(_from_col_shards(g_in))
    wuq_p = _pad_heads(_from_col_shards(g_uq))
    wukv_f = _from_col_shards(g_ukv)
    w_out_f = g_out.reshape(D_MODEL, D_MODEL)

    cosf, sin_a, sin_b = _rope_tables(positions[0])
    qhn_p, khn_p = _pad_lanes(q_head_norm), _pad_lanes(k_head_norm)

    hb, qkv, g_sb, cq, ckv, g_mla, kr, qm, km, vm = _fwd_pre(
        x2, shift, scale, norm_w, w_in_r, q_lora_norm, wuq_p, kv_lora_norm, wukv_f, qhn_p, khn_p,
        cosf, sin_a, sin_b)
    o_sb = _sb_fwd(qkv)
    o_mla, lse = _mla_fwd(qm, km, vm)

    dy, do_sb, dg_sb, do_mla, dg_mla, gw_out, d_gate, loss_acc = _mid(
        o_sb, g_sb, o_mla, g_mla, x2, tgt, gate, w_out_f)

    dq_sb, dk_sb, dv_sb = _sb_bwd(qkv, o_sb, do_sb)
    dq_m, dk_m, dv_m = _mla_bwd(qm, km, vm, o_mla, do_mla, lse)
    dcq, dckv, dkr, gw_uq_p, gw_ukv, g_qhn, g_khn, g_qln, g_kvln = _mla_pre_bwd(
        dq_m, dk_m, dv_m, cq, ckv, kr, q_lora_norm, wuq_p, kv_lora_norm, wukv_f, qhn_p, khn_p,
        cosf, sin_a, sin_b)
    dproj, grad_x, d_shift, d_scale, g_norm_w = _dproj_bwd(
        dq_sb, dk_sb, dv_sb, dg_sb, dcq, dckv, dg_mla, dkr, w_in_r, x2, dy, norm_w, scale)
    gw_in = _restore_cols(_grad_w_in(hb, dproj))

    d_ada = jnp.concatenate([d_shift, d_scale, d_gate], axis=1)
    gpack = _pack(g_norm_w, g_qln, g_kvln, g_qhn[:, :MLA_QK], g_khn[:, :MLA_QK], d_ada, loss_acc)
    wpack = _pack(norm_w, q_lora_norm, kv_lora_norm, q_head_norm, k_head_norm, b_ada)
    mpack = _pack(m_norm_w, m_q_lora_norm, m_kv_lora_norm, m_q_head_norm, m_k_head_norm, m_b_ada)
    vpack = _pack(v_norm_w, v_q_lora_norm, v_kv_lora_norm, v_q_head_norm, v_k_head_norm, v_b_ada)
    ccol = jnp.broadcast_to(c.reshape(D_MODEL, 1), (D_MODEL, LANES))
    pg, pd, pm, pv, ada_g, ada_d, ada_m, ada_v = _vec_exchange(
        gpack, ccol, wpack, mpack, vpack, w_ada_s, m_w_ada[0], v_w_ada[0])
    loss = 0.5 * jnp.sum(pg[0, PK_LOSS:PK_END]) / D_MODEL

    r_in, r_uq, r_ukv, r_out = _grad_exchange([g.astype(BF16) for g in (
        _col_shards(gw_in), _col_shards(_unpad_heads(gw_uq_p)), _col_shards(gw_ukv),
        gw_out.reshape(N_DEV, D_MODEL // N_DEV, D_MODEL))])
    in_g, in_d, in_m, in_v = _adamw_reduce("adamw_w_in", r_in, w_in_s, m_w_in[0], v_w_in[0], 256)
    uq_g, uq_d, uq_m, uq_v = _adamw_reduce("adamw_w_uq", r_uq, w_uq_s, m_w_uq[0], v_w_uq[0], w_uq_s.shape[0])
    ukv_g, ukv_d, ukv_m, ukv_v = _adamw_reduce(
        "adamw_w_ukv", r_ukv, w_ukv_s, m_w_ukv[0], v_w_ukv[0], w_ukv_s.shape[0])
    out_g, out_d, out_m, out_v = _adamw_reduce(
        "adamw_w_out", r_out, w_out_s, m_w_out[0], v_w_out[0], w_out_s.shape[0])

    def group(ada_t, pk, in_t, uq_t, ukv_t, out_t):
        nw, qln, kvln, qhn, khn, b = _unpack(pk)
        return (ada_t[None], b, nw, in_t[None], qln, uq_t[None], kvln, ukv_t[None], qhn, khn, out_t[None])

    return (loss, grad_x.reshape(1, s_len, D_MODEL),
            *group(ada_g, pg, in_g, uq_g, ukv_g, out_g),
            *group(ada_d, pd, in_d, uq_d, ukv_d, out_d),
            *group(ada_m, pm, in_m, uq_m, ukv_m, out_m),
            *group(ada_v, pv, in_v, uq_v, ukv_v, out_v))
```

```python
import functools
import math

import jax
import jax.numpy as jnp
from jax import lax
from jax.experimental import pallas as pl
from jax.experimental.pallas import tpu as pltpu

F32 = jnp.float32
BF16 = jnp.bfloat16

N_DEV = 8
D_MODEL = 1024
HEADS = 8
SB_WIDTH = 512
MLA_WIDTH = 512
Q_LORA = 384
KV_LORA = 256
ROPE = 32
NOPE = 64
MLA_QK = 96
LANES = 128
IN_COLS = 3232
IN_COLS_R = 3328
EPS = 1e-6
NEG = -1e30

ADAM_LR = 0.001
ADAM_B1 = 0.9
ADAM_B2 = 0.999
ADAM_EPS = 1e-08
ADAM_WD = 0.01
ADAM_STEP = 10

TS = 256
TQ = 512
KEY_UNROLL = 2
TN_S = 512
VMEM_BIG = 56 * 1024 * 1024

PK_NORM, PK_QLN, PK_KVLN, PK_QHN, PK_KHN, PK_ADA, PK_LOSS, PK_END = 0, 1024, 1408, 1664, 1792, 1920, 4992, 6016

MESH_ID = pl.DeviceIdType.MESH


def _dot_nn(a, b):
    return lax.dot_general(a, b, (((1,), (0,)), ((), ())), preferred_element_type=F32)


def _dot_nt(a, b):
    return lax.dot_general(a, b, (((1,), (1,)), ((), ())), preferred_element_type=F32)


def _dot_tn(a, b):
    return lax.dot_general(a, b, (((0,), (0,)), ((), ())), preferred_element_type=F32)


def _split_bf16(a):
    hi = a.astype(BF16)
    lo = (a - hi.astype(F32)).astype(BF16)
    return hi, lo


def _dot3(a, b):
    ah, al = _split_bf16(a)
    bh, bl = _split_bf16(b)
    return _dot_nn(ah, bh) + _dot_nn(ah, bl) + _dot_nn(al, bh)


def _sigmoid(g):
    return 1.0 / (1.0 + jnp.exp(-g))


def _silu(g):
    return g * _sigmoid(g)


def _lane_iota(shape):
    return lax.broadcasted_iota(jnp.int32, shape, len(shape) - 1)


def _adamw(w, g, m, v):
    m = ADAM_B1 * m + (1.0 - ADAM_B1) * g
    v = ADAM_B2 * v + (1.0 - ADAM_B2) * (g * g)
    m_hat = m / (1.0 - ADAM_B1 ** ADAM_STEP)
    v_hat = v / (1.0 - ADAM_B2 ** ADAM_STEP)
    delta = -ADAM_LR * (m_hat / (jnp.sqrt(v_hat) + ADAM_EPS) + ADAM_WD * w)
    return delta, m, v


def _mesh_pos():
    return lax.axis_index("x"), lax.axis_index("y"), lax.axis_index("c")


def _peer(pos, k):
    x, y, c = pos
    return (1 - x if k & 4 else x, 1 - y if k & 2 else y, 1 - c if k & 1 else c)


def _lin(pos):
    return 4 * pos[0] + 2 * pos[1] + pos[2]


def _remote(src, dst, send_sems, recv_sems, idx, peer):
    return pltpu.make_async_remote_copy(
        src_ref=src, dst_ref=dst, send_sem=send_sems.at[idx], recv_sem=recv_sems.at[idx],
        device_id=peer, device_id_type=MESH_ID)


def _all_gather(pos, src, buf, send_sems, recv_sems, base):
    me = _lin(pos)
    sent = []
    for k in range(1, N_DEV):
        cp = _remote(src, buf.at[me], send_sems, recv_sems, base + k - 1, _peer(pos, k))
        cp.start()
        sent.append(cp)
    for k in range(1, N_DEV):
        peer = _peer(pos, k)
        _remote(src, buf.at[_lin(peer)], send_sems, recv_sems, base + k - 1, peer).wait_recv()
    for cp in sent:
        cp.wait_send()


def _all_to_all(pos, src, buf, send_sems, recv_sems, base):
    me = _lin(pos)
    sent = []
    for k in range(1, N_DEV):
        peer = _peer(pos, k)
        cp = _remote(src.at[_lin(peer)], buf.at[me], send_sems, recv_sems, base + k - 1, peer)
        cp.start()
        sent.append(cp)
    for k in range(1, N_DEV):
        peer = _peer(pos, k)
        _remote(src.at[me], buf.at[_lin(peer)], send_sems, recv_sems, base + k - 1, peer).wait_recv()
    for cp in sent:
        cp.wait_send()


def _vmem_spec():
    return pl.BlockSpec(memory_space=pltpu.VMEM)


def _any_spec():
    return pl.BlockSpec(memory_space=pl.ANY)


def _row_select(slots, n):
    r = lax.broadcasted_iota(jnp.int32, (N_DEV, n), 0)
    out = jnp.zeros((N_DEV, n), F32)
    for j in range(N_DEV):
        out = out + jnp.where(r == j, slots[j], 0.0)
    return out


def _ada_fwd(c8, w_ada, b_ada8):
    n_sh = w_ada.shape[1]

    def body(c_ref, w_ref, b_ref, out_ref, call_ref, psend_ref, precv_ref, ssem, rsem):
        pos = _mesh_pos()
        me = _lin(pos)
        call_ref[me] = c_ref[...]
        _all_gather(pos, c_ref, call_ref, ssem, rsem, 0)
        w = w_ref[...]
        for j in range(N_DEV):
            psend_ref[j] = _dot3(_silu(call_ref[j]), w)
        precv_ref[me] = psend_ref[me]
        _all_to_all(pos, psend_ref, precv_ref, ssem, rsem, N_DEV - 1)
        out_ref[...] = _row_select([precv_ref[j] for j in range(N_DEV)], n_sh) + b_ref[...]

    return pl.pallas_call(
        body, name="ada_fwd",
        out_shape=jax.ShapeDtypeStruct((N_DEV, n_sh), F32),
        in_specs=[_vmem_spec()] * 3, out_specs=_vmem_spec(),
        scratch_shapes=[
            pltpu.VMEM((N_DEV, 8, D_MODEL), F32),
            pltpu.VMEM((N_DEV, 8, n_sh), F32),
            pltpu.VMEM((N_DEV, 8, n_sh), F32),
            pltpu.SemaphoreType.DMA((2 * (N_DEV - 1),)),
            pltpu.SemaphoreType.DMA((2 * (N_DEV - 1),)),
        ],
    )(c8, w_ada, b_ada8)


def _gather_weights(shards):
    n = len(shards)

    def body(*refs):
        ins, outs = refs[:n], refs[n:2 * n]
        ssem, rsem = refs[2 * n], refs[2 * n + 1]
        pos = _mesh_pos()
        me = _lin(pos)
        for a in range(n):
            outs[a][me] = ins[a][...].astype(BF16)
        for a in range(n):
            _all_gather(pos, outs[a].at[me], outs[a], ssem, rsem, a * (N_DEV - 1))

    return pl.pallas_call(
        body, name="gather_weights",
        out_shape=[jax.ShapeDtypeStruct((N_DEV,) + s.shape, BF16) for s in shards],
        in_specs=[_vmem_spec()] * n, out_specs=[_vmem_spec()] * n,
        scratch_shapes=[
            pltpu.SemaphoreType.DMA((n * (N_DEV - 1),)),
            pltpu.SemaphoreType.DMA((n * (N_DEV - 1),)),
        ],
        compiler_params=pltpu.CompilerParams(vmem_limit_bytes=VMEM_BIG),
    )(*shards)


def _rope(t, cosf, sin_a, sin_b):
    return t * cosf + pltpu.roll(t, 112, 1) * sin_a + pltpu.roll(t, 16, 1) * sin_b


def _rope_t(d, cosf, sin_a, sin_b):
    return d * cosf + pltpu.roll(d * sin_a, 16, 1) + pltpu.roll(d * sin_b, 112, 1)


def _head_rms(t):
    return lax.rsqrt(jnp.sum(t * t, axis=1, keepdims=True) * (1.0 / MLA_QK) + EPS)


def _rms_bwd(dxhat_w, xhat, r, n):
    return r * (dxhat_w - xhat * (jnp.sum(dxhat_w * xhat, axis=1, keepdims=True) * (1.0 / n)))


def _mla_latents(cq, ckv, kr, qln, kvln, wuq, wukv):
    rq = lax.rsqrt(jnp.mean(cq * cq, axis=1, keepdims=True) + EPS)
    rkv = lax.rsqrt(jnp.mean(ckv * ckv, axis=1, keepdims=True) + EPS)
    cq_hat = cq * rq
    ckv_hat = ckv * rkv
    cqn = (cq_hat * qln).astype(BF16)
    ckvn = (ckv_hat * kvln).astype(BF16)
    q_all = _dot_nn(cqn, wuq)
    kv = _dot_nn(ckvn, wukv)
    kr64 = pltpu.roll(kr, 64, 1)
    return rq, rkv, cq_hat, ckv_hat, cqn, ckvn, q_all, kv, kr64


def _fwd_pre(x, shift, scale, norm_w, w_in_r, qln, wuq, kvln, wukv, qhn, khn, cosf, sin_a, sin_b):
    s_len = x.shape[0]

    def body(x_ref, shift_ref, scale_ref, nw_ref, w_ref, qln_ref, wuq_ref, kvln_ref, wukv_ref,
             qhn_ref, khn_ref, cos_ref, sa_ref, sb_ref,
             hb_ref, qkv_ref, gsb_ref, cq_ref, ckv_ref, gmla_ref, kr_ref, qm_ref, km_ref, vm_ref):
        xv = x_ref[...]
        r = lax.rsqrt(jnp.mean(xv * xv, axis=1, keepdims=True) + EPS)
        h = (xv * r) * nw_ref[...] * (1.0 + scale_ref[...]) + shift_ref[...]
        hb = h.astype(BF16)
        hb_ref[...] = hb
        qkv_ref[...] = _dot_nn(hb, w_ref[:, 0:1536]).astype(BF16)
        gsb_ref[...] = _dot_nn(hb, w_ref[:, 1536:2048])
        cq = _dot_nn(hb, w_ref[:, 2048:2432])
        ckv = _dot_nn(hb, w_ref[:, 2432:2688])
        gmla_ref[...] = _dot_nn(hb, w_ref[:, 2688:3200])
        kr = _dot_nn(hb, w_ref[:, 3200:3328])
        cq_ref[...] = cq
        ckv_ref[...] = ckv
        kr_ref[...] = kr
        _, _, _, _, _, _, q_all, kv, kr64 = _mla_latents(
            cq, ckv, kr, qln_ref[...], kvln_ref[...], wuq_ref[...], wukv_ref[...])
        cosf, sa, sb = cos_ref[...], sa_ref[...], sb_ref[...]
        qhn_v, khn_v = qhn_ref[...], khn_ref[...]
        low = _lane_iota((TS, LANES)) < NOPE
        for hd in range(HEADS):
            blk = slice(LANES * hd, LANES * (hd + 1))
            qb = q_all[:, blk]
            qm_ref[:, blk] = _rope(qb * _head_rms(qb) * qhn_v, cosf, sa, sb).astype(BF16)
            kb = jnp.where(low, kv[:, blk], kr64)
            km_ref[:, blk] = _rope(kb * _head_rms(kb) * khn_v, cosf, sa, sb).astype(BF16)
        for p in range(HEADS // 2):
            even = kv[:, LANES * 2 * p:LANES * (2 * p + 1)]
            odd = kv[:, LANES * (2 * p + 1):LANES * (2 * p + 2)]
            vm_ref[:, LANES * p:LANES * (p + 1)] = jnp.where(low, pltpu.roll(even, 64, 1), odd).astype(BF16)

    def tile(width):
        return pl.BlockSpec((TS, width), lambda i: (i, 0))

    def full(a):
        return pl.BlockSpec(a.shape, lambda i: (0, 0))

    out_widths = [(D_MODEL, BF16), (1536, BF16), (512, F32), (Q_LORA, F32), (KV_LORA, F32),
                  (512, F32), (LANES, F32), (1024, BF16), (1024, BF16), (512, BF16)]
    return pl.pallas_call(
        body, name="fwd_pre", grid=(s_len // TS,),
        out_shape=[jax.ShapeDtypeStruct((s_len, w), dt) for w, dt in out_widths],
        in_specs=[tile(D_MODEL), full(shift), full(scale), full(norm_w), full(w_in_r), full(qln), full(wuq),
                  full(kvln), full(wukv), full(qhn), full(khn), tile(LANES), tile(LANES), tile(LANES)],
        out_specs=[tile(w) for w, _ in out_widths],
        compiler_params=pltpu.CompilerParams(dimension_semantics=("parallel",), vmem_limit_bytes=VMEM_BIG),
    )(x, shift, scale, norm_w, w_in_r, qln, wuq, kvln, wukv, qhn, khn, cosf, sin_a, sin_b)


CUM_W = 256


def _tri(strict):
    j = lax.broadcasted_iota(jnp.int32, (CUM_W, CUM_W), 0)
    s = lax.broadcasted_iota(jnp.int32, (CUM_W, CUM_W), 1)
    return (j > s if strict else j >= s).astype(BF16)


def _suffix_sums(a, tri_m, carry, hi=None):
    if hi is None:
        hi = a.astype(BF16)
    n = a.shape[1] // CUM_W
    outs = [None] * n
    for i in reversed(range(n)):
        cols = slice(CUM_W * i, CUM_W * (i + 1))
        lo = (a[:, cols] - hi[:, cols].astype(F32)).astype(BF16)
        outs[i] = _dot_nn(hi[:, cols], tri_m) + _dot_nn(lo, tri_m) + carry
        carry = carry + _rowsum(a[:, cols])
    return (outs[0] if n == 1 else jnp.concatenate(outs, axis=1)), carry


def _sb_weights(qm, kb, carry, tri_u, diag):
    z = _dot_nt(qm, kb)
    nz = -z
    lk = jnp.minimum(nz, 0.0) - jnp.log(1.0 + jnp.exp(jnp.minimum(z, nz)))
    if diag:
        t = lax.broadcasted_iota(jnp.int32, (TQ, TQ), 0)
        s = lax.broadcasted_iota(jnp.int32, (TQ, TQ), 1)
        valid = s < t
        lk = jnp.where(valid, lk, 0.0)
    lk_hi = lk.astype(BF16)
    after, carry = _suffix_sums(lk, tri_u, carry, lk_hi)
    logw = z + lk + after
    if diag:
        logw = jnp.where(valid, logw, NEG)
    return lk_hi, jnp.exp(logw), carry


SB_SCALE = 0.125


def _head_masks():
    lane = _lane_iota((1, LANES))
    return [lane < 64, lane >= 64]


def _masked(hm, a):
    return jnp.where(hm, a, jnp.zeros_like(a))


def _rowsum(a):
    return jnp.sum(a, axis=1, keepdims=True)


def _key_rows(kj):
    return pl.ds(pl.multiple_of(kj * TQ, TQ), TQ)


def _over_key_tiles(count, fn, st, ascending):
    n_full = count // KEY_UNROLL
    n_rest = count - n_full * KEY_UNROLL

    def group(g, s_):
        return fn([g * KEY_UNROLL + (u if ascending else KEY_UNROLL - 1 - u) for u in range(KEY_UNROLL)], s_)

    if ascending:
        st = lax.fori_loop(0, n_full, group, st)
        return lax.fori_loop(0, n_rest, lambda i, s_: fn([n_full * KEY_UNROLL + i], s_), st)
    st = lax.fori_loop(0, n_rest, lambda i, s_: fn([count - 1 - i], s_), st)
    return lax.fori_loop(0, n_full, lambda i, s_: group(n_full - 1 - i, s_), st)


def _each_tile(block):
    def trip(tiles, st):
        for kj in tiles:
            st = block(_key_rows(kj), st, False)
        return st
    return trip


STAGE_SLOTS = 2 * KEY_UNROLL


def _stage_copies(to_hbm, hbm_refs, scr_refs, sems, pair, qi, kj):
    slot = (qi - kj) % STAGE_SLOTS
    out = []
    for h in range(2):
        for a in range(2):
            hbm, scr = hbm_refs[a].at[2 * pair + h, qi, kj], scr_refs[a].at[slot, h]
            sem = sems.at[4 * slot + 2 * h + a]
            out.append(pltpu.make_async_copy(scr, hbm, sem) if to_hbm else pltpu.make_async_copy(hbm, scr, sem))
    return out


def _sb_fwd(qkv):
    s_len = qkv.shape[0]
    nq = s_len // TQ

    def body(q_ref, k_ref, v_ref, o_ref, w_hbm, l_hbm, w_scr, l_scr, sems):
        pair, qi = pl.program_id(0), pl.program_id(1)
        q = q_ref[...]
        tri_u = _tri(True)
        masks = _head_masks()
        qms = [_masked(hm, q) * SB_SCALE for hm in masks]

        def copies(kj):
            return _stage_copies(True, (w_hbm, l_hbm), (w_scr, l_scr), sems, pair, qi, kj)

        def block(kj, st, diag):
            rows = _key_rows(kj)
            slot = (qi - kj) % STAGE_SLOTS
            kb, vb = k_ref[rows, :], v_ref[rows, :]
            carries, acc = list(st[:2]), st[2]
            for h in range(2):
                lk_hi, w, carries[h] = _sb_weights(qms[h], kb, carries[h], tri_u, diag)
                wb = w.astype(BF16)
                w_scr[slot, h] = wb
                l_scr[slot, h] = lk_hi
                acc = acc + _dot_nn(wb, _masked(masks[h], vb))
            return carries[0], carries[1], acc

        def trip(tiles, st, diag=False):
            for kj in tiles:
                @pl.when(qi - kj >= STAGE_SLOTS)
                def _():
                    for cp in copies(kj + STAGE_SLOTS):
                        cp.wait()
            for kj in tiles:
                st = block(kj, st, diag)
            for kj in tiles:
                for cp in copies(kj):
                    cp.start()
            return st

        zc = jnp.zeros((TQ, 1), F32)
        st = trip([qi], (zc, zc, jnp.zeros((TQ, LANES), F32)), True)
        st = _over_key_tiles(qi, trip, st, ascending=False)
        o_ref[...] = st[2]
        for kj in range(STAGE_SLOTS):
            @pl.when(kj <= qi)
            def _():
                for cp in copies(kj):
                    cp.wait()

    saved = jax.ShapeDtypeStruct((HEADS, nq, nq, TQ, TQ), BF16)
    stage = pltpu.VMEM((STAGE_SLOTS, 2, TQ, TQ), BF16)
    return pl.pallas_call(
        body, name="sb_fwd", grid=(HEADS // 2, nq),
        out_shape=[jax.ShapeDtypeStruct((s_len, SB_WIDTH), F32), saved, saved],
        in_specs=[pl.BlockSpec((TQ, LANES), lambda p, i: (i, p)),
                  pl.BlockSpec((s_len, LANES), lambda p, i: (0, 4 + p)),
                  pl.BlockSpec((s_len, LANES), lambda p, i: (0, 8 + p))],
        out_specs=[pl.BlockSpec((TQ, LANES), lambda p, i: (i, p)), _any_spec(), _any_spec()],
        scratch_shapes=[stage, stage, pltpu.SemaphoreType.DMA((4 * STAGE_SLOTS,))],
        compiler_params=pltpu.CompilerParams(dimension_semantics=("parallel", "parallel"),
                                             vmem_limit_bytes=VMEM_BIG),
    )(qkv, qkv, qkv)


def _sb_bwd(qkv, o, do, w_saved, l_saved):
    s_len = qkv.shape[0]

    def body(q_ref, k_ref, v_ref, o_ref, do_ref, w_hbm, l_hbm, dq_ref, dk_ref, dv_ref, w_scr, l_scr, sems):
        pair, qi = pl.program_id(0), pl.program_id(1)

        def copies(kj):
            return _stage_copies(False, (w_hbm, l_hbm), (w_scr, l_scr), sems, pair, qi, kj)

        for cp in copies(qi):
            cp.start()

        @pl.when(qi >= 1)
        def _():
            for cp in copies(qi - 1):
                cp.start()

        @pl.when(qi == 0)
        def _():
            dk_ref[...] = jnp.zeros_like(dk_ref)
            dv_ref[...] = jnp.zeros_like(dv_ref)

        q = q_ref[...]
        do_v = do_ref[...]
        od = o_ref[...] * do_v.astype(F32)
        tri_i = _tri(False)
        masks = _head_masks()
        qms = [_masked(hm, q) * SB_SCALE for hm in masks]
        doms = [_masked(hm, do_v) for hm in masks]
        d_tots = [_rowsum(jnp.where(hm, od, 0.0)) for hm in masks]

        def block(kj, st):
            rows = _key_rows(kj)
            slot = (qi - kj) % STAGE_SLOTS
            kb, vb = k_ref[rows, :], v_ref[rows, :]
            carries_d, dqs = list(st[0:2]), list(st[2:4])
            dk_blk = jnp.zeros((TQ, LANES), F32)
            dv_blk = jnp.zeros((TQ, LANES), F32)
            for h in range(2):
                wb = w_scr[slot, h]
                d_l = _dot_nt(doms[h], vb) * wb.astype(F32)
                from_here, carries_d[h] = _suffix_sums(d_l, tri_i, carries_d[h])
                before = d_tots[h] - from_here
                keep = jnp.exp(l_scr[slot, h].astype(F32))
                dzb = (d_l * keep - before * (1.0 - keep)).astype(BF16)
                dk_blk = dk_blk + _dot_tn(dzb, qms[h])
                dv_blk = dv_blk + _dot_tn(wb, doms[h])
                dqs[h] = dqs[h] + _dot_nn(dzb, kb)
            dk_ref[rows, :] += dk_blk
            dv_ref[rows, :] += dv_blk
            return (*carries_d, *dqs)

        def trip(tiles, st):
            for kj in tiles:
                @pl.when(kj >= 2)
                def _():
                    for cp in copies(kj - 2):
                        cp.start()
            for kj in tiles:
                for cp in copies(kj):
                    cp.wait()
            for kj in tiles:
                st = block(kj, st)
            return st

        zc = jnp.zeros((TQ, 1), F32)
        za = jnp.zeros((TQ, LANES), F32)
        st = _over_key_tiles(qi + 1, trip, (zc, zc, za, za), ascending=False)
        dq_ref[...] = jnp.where(masks[0], st[2], st[3]) * SB_SCALE

    tile = pl.BlockSpec((TQ, LANES), lambda p, i: (i, p))
    col = pl.BlockSpec((s_len, LANES), lambda p, i: (0, p))
    stage = pltpu.VMEM((STAGE_SLOTS, 2, TQ, TQ), BF16)
    return pl.pallas_call(
        body, name="sb_bwd", grid=(HEADS // 2, s_len // TQ),
        out_shape=[jax.ShapeDtypeStruct((s_len, SB_WIDTH), F32)] * 3,
        in_specs=[tile,
                  pl.BlockSpec((s_len, LANES), lambda p, i: (0, 4 + p)),
                  pl.BlockSpec((s_len, LANES), lambda p, i: (0, 8 + p)),
                  tile, tile, _any_spec(), _any_spec()],
        out_specs=[tile, col, col],
        scratch_shapes=[stage, stage, pltpu.SemaphoreType.DMA((4 * STAGE_SLOTS,))],
        compiler_params=pltpu.CompilerParams(dimension_semantics=("parallel", "arbitrary"),
                                             vmem_limit_bytes=VMEM_BIG),
    )(qkv, qkv, qkv, o, do, w_saved, l_saved)


MLA_SCALE = 1.0 / math.sqrt(MLA_QK)


def _causal_mask():
    t = lax.broadcasted_iota(jnp.int32, (TQ, TQ), 0)
    s = lax.broadcasted_iota(jnp.int32, (TQ, TQ), 1)
    return s <= t


def _head_lanes(h):
    return slice(LANES * h, LANES * (h + 1))


def _mla_fwd(qm, km, vm):
    s_len = qm.shape[0]

    def body(q_ref, k_ref, v_ref, o_ref, lse_ref):
        qi = pl.program_id(1)
        masks = _head_masks()
        qhs = [q_ref[:, _head_lanes(h)] for h in range(2)]

        def block(rows, st, diag):
            vb = v_ref[rows, :]
            ms, ls, acc = list(st[0:2]), list(st[2:4]), st[4]
            alphas, pvs = [], []
            for h in range(2):
                s = _dot_nt(qhs[h], k_ref[rows, _head_lanes(h)]) * MLA_SCALE
                if diag:
                    s = jnp.where(_causal_mask(), s, NEG)
                m_new = jnp.maximum(ms[h], jnp.max(s, axis=1, keepdims=True))
                p = jnp.exp(s - m_new)
                alphas.append(jnp.exp(ms[h] - m_new))
                ls[h] = alphas[h] * ls[h] + _rowsum(p)
                ms[h] = m_new
                pvs.append(_dot_nn(p.astype(BF16), _masked(masks[h], vb)))
            acc = jnp.where(masks[0], alphas[0], alphas[1]) * acc + pvs[0] + pvs[1]
            return (*ms, *ls, acc)

        neg = jnp.full((TQ, 1), NEG, F32)
        zc = jnp.zeros((TQ, 1), F32)
        st = (neg, neg, zc, zc, jnp.zeros((TQ, LANES), F32))
        st = _over_key_tiles(qi, _each_tile(block), st, ascending=True)
        m0, m1, l0, l1, acc = block(_key_rows(qi), st, True)
        o_ref[...] = acc / jnp.where(masks[0], l0, l1)
        lse_ref[0] = m0 + jnp.log(l0)
        lse_ref[1] = m1 + jnp.log(l1)

    return pl.pallas_call(
        body, name="mla_fwd", grid=(HEADS // 2, s_len // TQ),
        out_shape=[jax.ShapeDtypeStruct((s_len, MLA_WIDTH), F32),
                   jax.ShapeDtypeStruct((HEADS, s_len, 1), F32)],
        in_specs=[pl.BlockSpec((TQ, 2 * LANES), lambda p, i: (i, p)),
                  pl.BlockSpec((s_len, 2 * LANES), lambda p, i: (0, p)),
                  pl.BlockSpec((s_len, LANES), lambda p, i: (0, p))],
        out_specs=[pl.BlockSpec((TQ, LANES), lambda p, i: (i, p)),
                   pl.BlockSpec((2, TQ, 1), lambda p, i: (p, i, 0))],
        compiler_params=pltpu.CompilerParams(dimension_semantics=("parallel", "parallel"),
                                             vmem_limit_bytes=VMEM_BIG),
    )(qm, km, vm)


def _mla_bwd(qm, km, vm, o, do, lse):
    s_len = qm.shape[0]

    def body(q_ref, k_ref, v_ref, o_ref, do_ref, lse_ref, dq_ref, dk_ref, dv_ref):
        qi = pl.program_id(1)

        @pl.when(qi == 0)
        def _():
            dk_ref[...] = jnp.zeros_like(dk_ref)
            dv_ref[...] = jnp.zeros_like(dv_ref)

        do_v = do_ref[...]
        od = o_ref[...] * do_v.astype(F32)
        masks = _head_masks()
        qhs = [q_ref[:, _head_lanes(h)] for h in range(2)]
        doms = [_masked(hm, do_v) for hm in masks]
        deltas = [_rowsum(jnp.where(hm, od, 0.0)) for hm in masks]
        lses = [lse_ref[h] for h in range(2)]

        def block(rows, dqs, diag):
            vb = v_ref[rows, :]
            dqs = list(dqs)
            dv_blk = jnp.zeros((TQ, LANES), F32)
            for h in range(2):
                kb = k_ref[rows, _head_lanes(h)]
                s = _dot_nt(qhs[h], kb) * MLA_SCALE
                if diag:
                    s = jnp.where(_causal_mask(), s, NEG)
                p = jnp.exp(s - lses[h])
                dp = _dot_nt(doms[h], vb)
                ds = (p * (dp - deltas[h]) * MLA_SCALE).astype(BF16)
                dk_ref[rows, _head_lanes(h)] += _dot_tn(ds, qhs[h])
                dv_blk = dv_blk + _dot_tn(p.astype(BF16), doms[h])
                dqs[h] = dqs[h] + _dot_nn(ds, kb)
            dv_ref[rows, :] += dv_blk
            return tuple(dqs)

        za = jnp.zeros((TQ, LANES), F32)
        dqs = _over_key_tiles(qi, _each_tile(block), (za, za), ascending=True)
        dqs = block(_key_rows(qi), dqs, True)
        dq_ref[:, _head_lanes(0)] = dqs[0]
        dq_ref[:, _head_lanes(1)] = dqs[1]

    return pl.pallas_call(
        body, name="mla_bwd", grid=(HEADS // 2, s_len // TQ),
        out_shape=[jax.ShapeDtypeStruct((s_len, HEADS * LANES), F32),
                   jax.ShapeDtypeStruct((s_len, HEADS * LANES), F32),
                   jax.ShapeDtypeStruct((s_len, MLA_WIDTH), F32)],
        in_specs=[pl.BlockSpec((TQ, 2 * LANES), lambda p, i: (i, p)),
                  pl.BlockSpec((s_len, 2 * LANES), lambda p, i: (0, p)),
                  pl.BlockSpec((s_len, LANES), lambda p, i: (0, p)),
                  pl.BlockSpec((TQ, LANES), lambda p, i: (i, p)),
                  pl.BlockSpec((TQ, LANES), lambda p, i: (i, p)),
                  pl.BlockSpec((2, TQ, 1), lambda p, i: (p, i, 0))],
        out_specs=[pl.BlockSpec((TQ, 2 * LANES), lambda p, i: (i, p)),
                   pl.BlockSpec((s_len, 2 * LANES), lambda p, i: (0, p)),
                   pl.BlockSpec((s_len, LANES), lambda p, i: (0, p))],
        compiler_params=pltpu.CompilerParams(dimension_semantics=("parallel", "arbitrary"),
                                             vmem_limit_bytes=VMEM_BIG),
    )(qm, km, vm, o, do, lse)


def _mid(o_sb, g_sb, o_mla, g_mla, x, target, gate, w_out):
    s_len = x.shape[0]

    def body(osb_ref, gsb_ref, omla_ref, gmla_ref, x_ref, t_ref, gate_ref, w_ref,
             dy_ref, dosb_ref, dgsb_ref, domla_ref, dgmla_ref, gw_ref, dgate_ref, loss_ref):
        @pl.when(pl.program_id(0) == 0)
        def _():
            gw_ref[...] = jnp.zeros_like(gw_ref)
            dgate_ref[...] = jnp.zeros_like(dgate_ref)
            loss_ref[...] = jnp.zeros_like(loss_ref)

        o1, g1, o2, g2 = osb_ref[...], gsb_ref[...], omla_ref[...], gmla_ref[...]
        s1, s2 = _sigmoid(g1), _sigmoid(g2)
        mixed = jnp.concatenate([o1 * (g1 * s1), o2 * (g2 * s2)], axis=1).astype(BF16)
        w = w_ref[...]
        gate_v = gate_ref[...]
        u = _dot_nn(mixed, w)
        err = x_ref[...] + gate_v * u - t_ref[...]
        loss_ref[...] += jnp.sum(err * err, axis=0, keepdims=True)
        dy = err * (1.0 / D_MODEL)
        dy_ref[...] = dy
        dgate_ref[...] += jnp.sum(dy * u, axis=0, keepdims=True)
        du = (dy * gate_v).astype(BF16)
        gw_ref[...] += _dot_tn(mixed, du)
        dmixed = _dot_nt(du, w)
        d1, d2 = dmixed[:, :SB_WIDTH], dmixed[:, SB_WIDTH:]
        dosb_ref[...] = (d1 * (g1 * s1)).astype(BF16)
        dgsb_ref[...] = (d1 * o1 * (s1 * (1.0 + g1 * (1.0 - s1)))).astype(BF16)
        domla_ref[...] = (d2 * (g2 * s2)).astype(BF16)
        dgmla_ref[...] = (d2 * o2 * (s2 * (1.0 + g2 * (1.0 - s2)))).astype(BF16)

    def tile(width):
        return pl.BlockSpec((TS, width), lambda i: (i, 0))

    def full(shape):
        return pl.BlockSpec(shape, lambda i: (0, 0))

    return pl.pallas_call(
        body, name="mid", grid=(s_len // TS,),
        out_shape=[jax.ShapeDtypeStruct((s_len, D_MODEL), F32)]
        + [jax.ShapeDtypeStruct((s_len, 512), BF16)] * 4
        + [jax.ShapeDtypeStruct((D_MODEL, D_MODEL), F32),
           jax.ShapeDtypeStruct((1, D_MODEL), F32), jax.ShapeDtypeStruct((1, D_MODEL), F32)],
        in_specs=[tile(512)] * 4 + [tile(D_MODEL), tile(D_MODEL), full((1, D_MODEL)), full((D_MODEL, D_MODEL))],
        out_specs=[tile(D_MODEL)] + [tile(512)] * 4
        + [full((D_MODEL, D_MODEL)), full((1, D_MODEL)), full((1, D_MODEL))],
        compiler_params=pltpu.CompilerParams(dimension_semantics=("arbitrary",), vmem_limit_bytes=VMEM_BIG),
    )(o_sb, g_sb, o_mla, g_mla, x, target, gate, w_out)


def _mla_pre_bwd(dq, dk, dv, cq, ckv, kr, qln, wuq, kvln, wukv, qhn, khn, cosf, sin_a, sin_b):
    s_len = cq.shape[0]

    def body(dq_ref, dk_ref, dv_ref, cq_ref, ckv_ref, kr_ref, qln_ref, wuq_ref, kvln_ref, wukv_ref,
             qhn_ref, khn_ref, cos_ref, sa_ref, sb_ref,
             dcq_ref, dckv_ref, dkr_ref, gwuq_ref, gwukv_ref, gqhn_ref, gkhn_ref, gqln_ref, gkvln_ref,
             dqa_ref, dkv_ref):
        @pl.when(pl.program_id(0) == 0)
        def _():
            for r_ in (gwuq_ref, gwukv_ref, gqhn_ref, gkhn_ref, gqln_ref, gkvln_ref):
                r_[...] = jnp.zeros_like(r_)

        cq, ckv = cq_ref[...], ckv_ref[...]
        qln_v, kvln_v = qln_ref[...], kvln_ref[...]
        wuq_v, wukv_v = wuq_ref[...], wukv_ref[...]
        rq, rkv, cq_hat, ckv_hat, cqn, ckvn, q_all, kv, kr64 = _mla_latents(
            cq, ckv, kr_ref[...], qln_v, kvln_v, wuq_v, wukv_v)
        cosf, sa, sb = cos_ref[...], sa_ref[...], sb_ref[...]
        qhn_v, khn_v = qhn_ref[...], khn_ref[...]
        lane = _lane_iota((TS, LANES))
        low = lane < NOPE
        g_qhn = jnp.zeros((1, LANES), F32)
        g_khn = jnp.zeros((1, LANES), F32)
        dkr64 = jnp.zeros((TS, LANES), F32)
        for hd in range(HEADS):
            blk = slice(LANES * hd, LANES * (hd + 1))
            qb = q_all[:, blk]
            r = _head_rms(qb)
            xh = qb * r
            dn = _rope_t(dq_ref[:, blk], cosf, sa, sb)
            g_qhn = g_qhn + jnp.sum(dn * xh, axis=0, keepdims=True)
            dqa_ref[:, blk] = _rms_bwd(dn * qhn_v, xh, r, MLA_QK).astype(BF16)

            kb = jnp.where(low, kv[:, blk], kr64)
            r = _head_rms(kb)
            xh = kb * r
            dn = _rope_t(dk_ref[:, blk], cosf, sa, sb)
            g_khn = g_khn + jnp.sum(dn * xh, axis=0, keepdims=True)
            dkb = _rms_bwd(dn * khn_v, xh, r, MLA_QK)
            dkr64 = dkr64 + jnp.where(low, 0.0, dkb)
            dvp = dv_ref[:, LANES * (hd // 2):LANES * (hd // 2 + 1)]
            dvh = pltpu.roll(dvp, 64, 1) if hd % 2 == 0 else dvp
            dkv_ref[:, blk] = jnp.where(low, dkb, dvh).astype(BF16)
        gqhn_ref[...] += g_qhn
        gkhn_ref[...] += g_khn
        dkr_ref[...] = pltpu.roll(dkr64, 64, 1).astype(BF16)

        dqa = dqa_ref[...]
        gwuq_ref[...] += _dot_tn(cqn, dqa)
        dcqn = _dot_nt(dqa, wuq_v)
        gqln_ref[...] += jnp.sum(dcqn * cq_hat, axis=0, keepdims=True)
        dcq_ref[...] = _rms_bwd(dcqn * qln_v, cq_hat, rq, Q_LORA).astype(BF16)

        dkv = dkv_ref[...]
        gwukv_ref[...] += _dot_tn(ckvn, dkv)
        dckvn = _dot_nt(dkv, wukv_v)
        gkvln_ref[...] += jnp.sum(dckvn * ckv_hat, axis=0, keepdims=True)
        dckv_ref[...] = _rms_bwd(dckvn * kvln_v, ckv_hat, rkv, KV_LORA).astype(BF16)

    def tile(width):
        return pl.BlockSpec((TS, width), lambda i: (i, 0))

    def full(shape):
        return pl.BlockSpec(shape, lambda i: (0, 0))

    acc_shapes = [(Q_LORA, 1024), (KV_LORA, 1024), (1, LANES), (1, LANES), (1, Q_LORA), (1, KV_LORA)]
    return pl.pallas_call(
        body, name="mla_pre_bwd", grid=(s_len // TS,),
        out_shape=[jax.ShapeDtypeStruct((s_len, Q_LORA), BF16), jax.ShapeDtypeStruct((s_len, KV_LORA), BF16),
                   jax.ShapeDtypeStruct((s_len, LANES), BF16)]
        + [jax.ShapeDtypeStruct(s, F32) for s in acc_shapes],
        in_specs=[tile(1024), tile(1024), tile(512), tile(Q_LORA), tile(KV_LORA), tile(LANES),
                  full(qln.shape), full(wuq.shape), full(kvln.shape), full(wukv.shape),
                  full(qhn.shape), full(khn.shape), tile(LANES), tile(LANES), tile(LANES)],
        out_specs=[tile(Q_LORA), tile(KV_LORA), tile(LANES)] + [full(s) for s in acc_shapes],
        scratch_shapes=[pltpu.VMEM((TS, 1024), BF16), pltpu.VMEM((TS, 1024), BF16)],
        compiler_params=pltpu.CompilerParams(dimension_semantics=("arbitrary",), vmem_limit_bytes=VMEM_BIG),
    )(dq, dk, dv, cq, ckv, kr, qln, wuq, kvln, wukv, qhn, khn, cosf, sin_a, sin_b)


def _dproj_bwd(dq_sb, dk_sb, dv_sb, dg_sb, dcq, dckv, dg_mla, dkr, w_in_r, x, dy, norm_w, scale):
    s_len = x.shape[0]

    def body(dq_ref, dk_ref, dv_ref, dg_ref, dcq_ref, dckv_ref, dgm_ref, dkr_ref, w_ref, x_ref, dy_ref,
             nw_ref, scale_ref, dp_ref, gx_ref, dshift_ref, dscale_ref, dnw_ref):
        @pl.when(pl.program_id(0) == 0)
        def _():
            for r_ in (dshift_ref, dscale_ref, dnw_ref):
                r_[...] = jnp.zeros_like(r_)

        dp_ref[:, 0:512] = dq_ref[...].astype(BF16)
        dp_ref[:, 512:1024] = dk_ref[...].astype(BF16)
        dp_ref[:, 1024:1536] = dv_ref[...].astype(BF16)
        dp_ref[:, 1536:2048] = dg_ref[...]
        dp_ref[:, 2048:2432] = dcq_ref[...]
        dp_ref[:, 2432:2688] = dckv_ref[...]
        dp_ref[:, 2688:3200] = dgm_ref[...]
        dp_ref[:, 3200:3328] = dkr_ref[...]
        dh = _dot_nt(dp_ref[...], w_ref[...])
        xv = x_ref[...]
        r = lax.rsqrt(jnp.mean(xv * xv, axis=1, keepdims=True) + EPS)
        xh = xv * r
        nw = nw_ref[...]
        dshift_ref[...] += jnp.sum(dh, axis=0, keepdims=True)
        dscale_ref[...] += jnp.sum(dh * (xh * nw), axis=0, keepdims=True)
        dxnw = dh * (1.0 + scale_ref[...])
        dnw_ref[...] += jnp.sum(dxnw * xh, axis=0, keepdims=True)
        gx_ref[...] = dy_ref[...] + _rms_bwd(dxnw * nw, xh, r, D_MODEL)

    def tile(width):
        return pl.BlockSpec((TS, width), lambda i: (i, 0))

    def full(shape):
        return pl.BlockSpec(shape, lambda i: (0, 0))

    vec = (1, D_MODEL)
    return pl.pallas_call(
        body, name="dproj_bwd", grid=(s_len // TS,),
        out_shape=[jax.ShapeDtypeStruct((s_len, IN_COLS_R), BF16), jax.ShapeDtypeStruct((s_len, D_MODEL), F32)]
        + [jax.ShapeDtypeStruct(vec, F32)] * 3,
        in_specs=[tile(512)] * 4 + [tile(Q_LORA), tile(KV_LORA), tile(512), tile(LANES),
                                    full(w_in_r.shape), tile(D_MODEL), tile(D_MODEL), full(vec), full(vec)],
        out_specs=[tile(IN_COLS_R), tile(D_MODEL)] + [full(vec)] * 3,
        compiler_params=pltpu.CompilerParams(dimension_semantics=("arbitrary",), vmem_limit_bytes=VMEM_BIG),
    )(dq_sb, dk_sb, dv_sb, dg_sb, dcq, dckv, dg_mla, dkr, w_in_r, x, dy, norm_w, scale)


def _grad_w_in(hb, dproj):
    s_len = hb.shape[0]
    n_half = IN_COLS_R // 2

    def body(h_ref, d_ref, g_ref):
        @pl.when(pl.program_id(1) == 0)
        def _():
            g_ref[...] = jnp.zeros_like(g_ref)

        g_ref[...] += _dot_tn(h_ref[...], d_ref[...])

    return pl.pallas_call(
        body, name="grad_w_in", grid=(2, s_len // TN_S),
        out_shape=jax.ShapeDtypeStruct((D_MODEL, IN_COLS_R), F32),
        in_specs=[pl.BlockSpec((TN_S, D_MODEL), lambda n, s: (s, 0)),
                  pl.BlockSpec((TN_S, n_half), lambda n, s: (s, n))],
        out_specs=pl.BlockSpec((D_MODEL, n_half), lambda n, s: (0, n)),
        compiler_params=pltpu.CompilerParams(dimension_semantics=("parallel", "arbitrary"),
                                             vmem_limit_bytes=VMEM_BIG),
    )(hb, dproj)


def _vec_exchange(gpack, ccol, wpack, mpack, vpack, w_ada, m_ada, v_ada):
    n_sh = w_ada.shape[1]

    def body(g_ref, cc_ref, wp_ref, mp_ref, vp_ref, wa_ref, ma_ref, va_ref,
             og_ref, od_ref, om_ref, ov_ref, ag_ref, ad_ref, am_ref, av_ref,
             gall_ref, call_ref, ssem, rsem):
        pos = _mesh_pos()
        me = _lin(pos)
        gall_ref[me] = g_ref[...]
        call_ref[me] = cc_ref[...]
        _all_gather(pos, g_ref, gall_ref, ssem, rsem, 0)
        _all_gather(pos, cc_ref, call_ref, ssem, rsem, N_DEV - 1)

        tot = gall_ref[0]
        for j in range(1, N_DEV):
            tot = tot + gall_ref[j]
        og_ref[...] = tot
        od_ref[...], om_ref[...], ov_ref[...] = _adamw(wp_ref[...], tot, mp_ref[...], vp_ref[...])

        ga = jnp.zeros((D_MODEL, n_sh), F32)
        for j in range(N_DEV):
            d_mine = jnp.zeros((8, n_sh), F32)
            for k in range(N_DEV):
                d_mine = d_mine + jnp.where(me == k, gall_ref[j, :, PK_ADA + n_sh * k:PK_ADA + n_sh * (k + 1)], 0.0)
            col = _silu(call_ref[j])
            ga = ga + jnp.concatenate(
                [col * d_mine[0:1, LANES * a:LANES * (a + 1)] for a in range(n_sh // LANES)], axis=1)
        ag_ref[...] = ga
        ad_ref[...], am_ref[...], av_ref[...] = _adamw(wa_ref[...], ga, ma_ref[...], va_ref[...])

    pk = jax.ShapeDtypeStruct((8, PK_END), F32)
    ada = jax.ShapeDtypeStruct((D_MODEL, n_sh), F32)
    return pl.pallas_call(
        body, name="vec_exchange",
        out_shape=[pk] * 4 + [ada] * 4,
        in_specs=[_vmem_spec()] * 8, out_specs=[_vmem_spec()] * 8,
        scratch_shapes=[
            pltpu.VMEM((N_DEV, 8, PK_END), F32),
            pltpu.VMEM((N_DEV, D_MODEL, LANES), F32),
            pltpu.SemaphoreType.DMA((2 * (N_DEV - 1),)),
            pltpu.SemaphoreType.DMA((2 * (N_DEV - 1),)),
        ],
        compiler_params=pltpu.CompilerParams(vmem_limit_bytes=VMEM_BIG),
    )(gpack, ccol, wpack, mpack, vpack, w_ada, m_ada, v_ada)


def _grad_exchange(grads):
    n = len(grads)

    def body(*refs):
        ins, outs = refs[:n], refs[n:2 * n]
        ssem, rsem, lsem = refs[2 * n], refs[2 * n + 1], refs[2 * n + 2]
        pos = _mesh_pos()
        me = _lin(pos)
        own = [pltpu.make_async_copy(ins[a].at[me], outs[a].at[me], lsem.at[a]) for a in range(n)]
        for cp in own:
            cp.start()
        for a in range(n):
            _all_to_all(pos, ins[a], outs[a], ssem, rsem, a * (N_DEV - 1))
        for cp in own:
            cp.wait()

    return pl.pallas_call(
        body, name="grad_exchange",
        out_shape=[jax.ShapeDtypeStruct(g.shape, g.dtype) for g in grads],
        in_specs=[_any_spec()] * n, out_specs=[_any_spec()] * n,
        scratch_shapes=[
            pltpu.SemaphoreType.DMA((n * (N_DEV - 1),)),
            pltpu.SemaphoreType.DMA((n * (N_DEV - 1),)),
            pltpu.SemaphoreType.DMA((n,)),
        ],
    )(*grads)


def _adamw_reduce(name, parts, w, m, v, row_tile):
    rows, cols = w.shape

    def body(p_ref, w_ref, m_ref, v_ref, g_ref, d_ref, mo_ref, vo_ref):
        g = p_ref[0].astype(F32)
        for j in range(1, N_DEV):
            g = g + p_ref[j].astype(F32)
        g_ref[...] = g
        d_ref[...], mo_ref[...], vo_ref[...] = _adamw(w_ref[...], g, m_ref[...], v_ref[...])

    tile = pl.BlockSpec((row_tile, cols), lambda i: (i, 0))
    return pl.pallas_call(
        body, name=name, grid=(rows // row_tile,),
        out_shape=[jax.ShapeDtypeStruct((rows, cols), F32)] * 4,
        in_specs=[pl.BlockSpec((N_DEV, row_tile, cols), lambda i: (0, i, 0)), tile, tile, tile],
        out_specs=[tile] * 4,
        compiler_params=pltpu.CompilerParams(dimension_semantics=("parallel",), vmem_limit_bytes=VMEM_BIG),
    )(parts, w, m, v)


def _rope_tables(positions):
    inv_freq = 10000.0 ** (-jnp.arange(0, ROPE, 2, dtype=F32) / ROPE)
    ang = positions.astype(F32)[:, None] * inv_freq
    cos, sin = jnp.cos(ang), jnp.sin(ang)
    s_len = positions.shape[0]
    ones = jnp.ones((s_len, NOPE), F32)
    zeros = jnp.zeros((s_len, NOPE), F32)
    z16 = jnp.zeros((s_len, ROPE // 2), F32)
    pad1 = jnp.ones((s_len, LANES - MLA_QK), F32)
    pad0 = jnp.zeros((s_len, LANES - MLA_QK), F32)
    cosf = jnp.concatenate([ones, cos, cos, pad1], axis=1)
    sin_a = jnp.concatenate([zeros, -sin, z16, pad0], axis=1)
    sin_b = jnp.concatenate([zeros, z16, sin, pad0], axis=1)
    return cosf, sin_a, sin_b


def _rearrange_cols(w):
    pad = jnp.zeros((w.shape[0], IN_COLS_R - IN_COLS), w.dtype)
    return jnp.concatenate([w[:, :2688], w[:, 2720:3232], w[:, 2688:2720], pad], axis=1)


def _restore_cols(g):
    return jnp.concatenate([g[:, :2688], g[:, 3200:3232], g[:, 2688:3200]], axis=1)


def _pad_heads(w):
    rows = w.shape[0]
    w = w.reshape(rows, HEADS, MLA_QK)
    return jnp.pad(w, ((0, 0), (0, 0), (0, LANES - MLA_QK))).reshape(rows, HEADS * LANES)


def _unpad_heads(g):
    rows = g.shape[0]
    return g.reshape(rows, HEADS, LANES)[:, :, :MLA_QK].reshape(rows, HEADS * MLA_QK)


def _pad_lanes(v):
    return jnp.pad(v, ((0, 0), (0, LANES - v.shape[1])))


def _col_shards(g):
    rows = g.shape[0]
    return g.reshape(rows, N_DEV, g.shape[1] // N_DEV).transpose(1, 0, 2)


def _from_col_shards(g):
    return g.transpose(1, 0, 2).reshape(g.shape[1], N_DEV * g.shape[2])


def _pack(norm_w, qln, kvln, qhn, khn, ada, loss_lanes=None):
    if loss_lanes is None:
        loss_lanes = jnp.zeros((1, PK_END - PK_LOSS), F32)
    row = jnp.concatenate([norm_w, qln, kvln, _pad_lanes(qhn), _pad_lanes(khn), ada, loss_lanes], axis=1)
    return jnp.broadcast_to(row, (8, PK_END))


def _unpack(p):
    row = p[0:1]
    return (row[:, PK_NORM:PK_QLN], row[:, PK_QLN:PK_KVLN], row[:, PK_KVLN:PK_QHN],
            row[:, PK_QHN:PK_QHN + MLA_QK], row[:, PK_KHN:PK_KHN + MLA_QK], row[:, PK_ADA:PK_LOSS])


def kernel(x, c, positions, w_ada, b_ada, norm_w, w_in, q_lora_norm, w_uq, kv_lora_norm, w_ukv, q_head_norm, k_head_norm, w_out, loss_target, m_w_ada, m_b_ada, m_norm_w, m_w_in, m_q_lora_norm, m_w_uq, m_kv_lora_norm, m_w_ukv, m_q_head_norm, m_k_head_norm, m_w_out, v_w_ada, v_b_ada, v_norm_w, v_w_in, v_q_lora_norm, v_w_uq, v_kv_lora_norm, v_w_ukv, v_q_head_norm, v_k_head_norm, v_w_out):
    s_len = x.shape[1]
    x2 = x.reshape(s_len, D_MODEL)
    tgt = loss_target.reshape(s_len, D_MODEL)
    w_ada_s, w_in_s, w_uq_s, w_ukv_s, w_out_s = w_ada[0], w_in[0], w_uq[0], w_ukv[0], w_out[0]

    ada8 = _ada_fwd(jnp.broadcast_to(c, (8, D_MODEL)), w_ada_s, b_ada.reshape(N_DEV, -1))
    ada = ada8.reshape(1, 3 * D_MODEL)
    shift, scale, gate = ada[:, :D_MODEL], ada[:, D_MODEL:2 * D_MODEL], ada[:, 2 * D_MODEL:]

    g_in, g_uq, g_ukv, g_out = _gather_weights([w_in_s, w_uq_s, w_ukv_s, w_out_s])
    w_in_r = _rearrange_cols(_from_col_shards(g_in))
    wuq_p = _pad_heads(_from_col_shards(g_uq))
    wukv_f = _from_col_shards(g_ukv)
    w_out_f = g_out.reshape(D_MODEL, D_MODEL)

    cosf, sin_a, sin_b = _rope_tables(positions[0])
    qhn_p, khn_p = _pad_lanes(q_head_norm), _pad_lanes(k_head_norm)

    hb, qkv, g_sb, cq, ckv, g_mla, kr, qm, km, vm = _fwd_pre(
        x2, shift, scale, norm_w, w_in_r, q_lora_norm, wuq_p, kv_lora_norm, wukv_f, qhn_p, khn_p,
        cosf, sin_a, sin_b)
    o_sb, w_saved, l_saved = _sb_fwd(qkv)
    o_mla, lse = _mla_fwd(qm, km, vm)

    dy, do_sb, dg_sb, do_mla, dg_mla, gw_out, d_gate, loss_acc = _mid(
        o_sb, g_sb, o_mla, g_mla, x2, tgt, gate, w_out_f)

    dq_sb, dk_sb, dv_sb = _sb_bwd(qkv, o_sb, do_sb, w_saved, l_saved)
    dq_m, dk_m, dv_m = _mla_bwd(qm, km, vm, o_mla, do_mla, lse)
    dcq, dckv, dkr, gw_uq_p, gw_ukv, g_qhn, g_khn, g_qln, g_kvln = _mla_pre_bwd(
        dq_m, dk_m, dv_m, cq, ckv, kr, q_lora_norm, wuq_p, kv_lora_norm, wukv_f, qhn_p, khn_p,
        cosf, sin_a, sin_b)
    dproj, grad_x, d_shift, d_scale, g_norm_w = _dproj_bwd(
        dq_sb, dk_sb, dv_sb, dg_sb, dcq, dckv, dg_mla, dkr, w_in_r, x2, dy, norm_w, scale)
    gw_in = _restore_cols(_grad_w_in(hb, dproj))

    d_ada = jnp.concatenate([d_shift, d_scale, d_gate], axis=1)
    gpack = _pack(g_norm_w, g_qln, g_kvln, g_qhn[:, :MLA_QK], g_khn[:, :MLA_QK], d_ada, loss_acc)
    wpack = _pack(norm_w, q_lora_norm, kv_lora_norm, q_head_norm, k_head_norm, b_ada)
    mpack = _pack(m_norm_w, m_q_lora_norm, m_kv_lora_norm, m_q_head_norm, m_k_head_norm, m_b_ada)
    vpack = _pack(v_norm_w, v_q_lora_norm, v_kv_lora_norm, v_q_head_norm, v_k_head_norm, v_b_ada)
    ccol = jnp.broadcast_to(c.reshape(D_MODEL, 1), (D_MODEL, LANES))
    pg, pd, pm, pv, ada_g, ada_d, ada_m, ada_v = _vec_exchange(
        gpack, ccol, wpack, mpack, vpack, w_ada_s, m_w_ada[0], v_w_ada[0])
    loss = 0.5 * jnp.sum(pg[0, PK_LOSS:PK_END]) / D_MODEL

    r_in, r_uq, r_ukv, r_out = _grad_exchange([g.astype(BF16) for g in (
        _col_shards(gw_in), _col_shards(_unpad_heads(gw_uq_p)), _col_shards(gw_ukv),
        gw_out.reshape(N_DEV, D_MODEL // N_DEV, D_MODEL))])
    in_g, in_d, in_m, in_v = _adamw_reduce("adamw_w_in", r_in, w_in_s, m_w_in[0], v_w_in[0], 256)
    uq_g, uq_d, uq_m, uq_v = _adamw_reduce("adamw_w_uq", r_uq, w_uq_s, m_w_uq[0], v_w_uq[0], w_uq_s.shape[0])
    ukv_g, ukv_d, ukv_m, ukv_v = _adamw_reduce(
        "adamw_w_ukv", r_ukv, w_ukv_s, m_w_ukv[0], v_w_ukv[0], w_ukv_s.shape[0])
    out_g, out_d, out_m, out_v = _adamw_reduce(
        "adamw_w_out", r_out, w_out_s, m_w_out[0], v_w_out[0], w_out_s.shape[0])

    def group(ada_t, pk, in_t, uq_t, ukv_t, out_t):
        nw, qln, kvln, qhn, khn, b = _unpack(pk)
        return (ada_t[None], b, nw, in_t[None], qln, uq_t[None], kvln, ukv_t[None], qhn, khn, out_t[None])

    return (loss, grad_x.reshape(1, s_len, D_MODEL),
            *group(ada_g, pg, in_g, uq_g, ukv_g, out_g),
            *group(ada_d, pd, in_d, uq_d, ukv_d, out_d),
            *group(ada_m, pm, in_m, uq_m, ukv_m, out_m),
            *group(ada_v, pv, in_v, uq_v, ukv_v, out_v))
```

```python
import functools
import math

import jax
import jax.numpy as jnp
from jax import lax
from jax.experimental import pallas as pl
from jax.experimental.pallas import tpu as pltpu

F32 = jnp.float32
BF16 = jnp.bfloat16

N_DEV = 8
D_MODEL = 1024
HEADS = 8
SB_WIDTH = 512
MLA_WIDTH = 512
Q_LORA = 384
KV_LORA = 256
ROPE = 32
NOPE = 64
MLA_QK = 96
LANES = 128
IN_COLS = 3232
IN_COLS_R = 3328
EPS = 1e-6
NEG = -1e30

ADAM_LR = 0.001
ADAM_B1 = 0.9
ADAM_B2 = 0.999
ADAM_EPS = 1e-08
ADAM_WD = 0.01
ADAM_STEP = 10

TS = 256
TQ = 512
KEY_UNROLL = 2
TN_S = 512
VMEM_BIG = 56 * 1024 * 1024

PK_NORM, PK_QLN, PK_KVLN, PK_QHN, PK_KHN, PK_ADA, PK_LOSS, PK_END = 0, 1024, 1408, 1664, 1792, 1920, 4992, 6016

MESH_ID = pl.DeviceIdType.MESH


def _dot_nn(a, b):
    return lax.dot_general(a, b, (((1,), (0,)), ((), ())), preferred_element_type=F32)


def _dot_nt(a, b):
    return lax.dot_general(a, b, (((1,), (1,)), ((), ())), preferred_element_type=F32)


def _dot_tn(a, b):
    return lax.dot_general(a, b, (((0,), (0,)), ((), ())), preferred_element_type=F32)


def _split_bf16(a):
    hi = a.astype(BF16)
    lo = (a - hi.astype(F32)).astype(BF16)
    return hi, lo


def _dot3(a, b):
    ah, al = _split_bf16(a)
    bh, bl = _split_bf16(b)
    return _dot_nn(ah, bh) + _dot_nn(ah, bl) + _dot_nn(al, bh)


def _sigmoid(g):
    return 1.0 / (1.0 + jnp.exp(-g))


def _silu(g):
    return g * _sigmoid(g)


def _lane_iota(shape):
    return lax.broadcasted_iota(jnp.int32, shape, len(shape) - 1)


def _adamw(w, g, m, v):
    m = ADAM_B1 * m + (1.0 - ADAM_B1) * g
    v = ADAM_B2 * v + (1.0 - ADAM_B2) * (g * g)
    m_hat = m / (1.0 - ADAM_B1 ** ADAM_STEP)
    v_hat = v / (1.0 - ADAM_B2 ** ADAM_STEP)
    delta = -ADAM_LR * (m_hat / (jnp.sqrt(v_hat) + ADAM_EPS) + ADAM_WD * w)
    return delta, m, v


def _mesh_pos():
    return lax.axis_index("x"), lax.axis_index("y"), lax.axis_index("c")


def _peer(pos, k):
    x, y, c = pos
    return (1 - x if k & 4 else x, 1 - y if k & 2 else y, 1 - c if k & 1 else c)


def _lin(pos):
    return 4 * pos[0] + 2 * pos[1] + pos[2]


def _remote(src, dst, send_sems, recv_sems, idx, peer):
    return pltpu.make_async_remote_copy(
        src_ref=src, dst_ref=dst, send_sem=send_sems.at[idx], recv_sem=recv_sems.at[idx],
        device_id=peer, device_id_type=MESH_ID)


def _all_gather(pos, src, buf, send_sems, recv_sems, base):
    me = _lin(pos)
    sent = []
    for k in range(1, N_DEV):
        cp = _remote(src, buf.at[me], send_sems, recv_sems, base + k - 1, _peer(pos, k))
        cp.start()
        sent.append(cp)
    for k in range(1, N_DEV):
        peer = _peer(pos, k)
        _remote(src, buf.at[_lin(peer)], send_sems, recv_sems, base + k - 1, peer).wait_recv()
    for cp in sent:
        cp.wait_send()


def _all_to_all(pos, src, buf, send_sems, recv_sems, base):
    me = _lin(pos)
    sent = []
    for k in range(1, N_DEV):
        peer = _peer(pos, k)
        cp = _remote(src.at[_lin(peer)], buf.at[me], send_sems, recv_sems, base + k - 1, peer)
        cp.start()
        sent.append(cp)
    for k in range(1, N_DEV):
        peer = _peer(pos, k)
        _remote(src.at[me], buf.at[_lin(peer)], send_sems, recv_sems, base + k - 1, peer).wait_recv()
    for cp in sent:
        cp.wait_send()


def _vmem_spec():
    return pl.BlockSpec(memory_space=pltpu.VMEM)


def _any_spec():
    return pl.BlockSpec(memory_space=pl.ANY)


def _row_select(slots, n):
    r = lax.broadcasted_iota(jnp.int32, (N_DEV, n), 0)
    out = jnp.zeros((N_DEV, n), F32)
    for j in range(N_DEV):
        out = out + jnp.where(r == j, slots[j], 0.0)
    return out


def _ada_fwd(c8, w_ada, b_ada8):
    n_sh = w_ada.shape[1]

    def body(c_ref, w_ref, b_ref, out_ref, call_ref, psend_ref, precv_ref, ssem, rsem):
        pos = _mesh_pos()
        me = _lin(pos)
        call_ref[me] = c_ref[...]
        _all_gather(pos, c_ref, call_ref, ssem, rsem, 0)
        w = w_ref[...]
        for j in range(N_DEV):
            psend_ref[j] = _dot3(_silu(call_ref[j]), w)
        precv_ref[me] = psend_ref[me]
        _all_to_all(pos, psend_ref, precv_ref, ssem, rsem, N_DEV - 1)
        out_ref[...] = _row_select([precv_ref[j] for j in range(N_DEV)], n_sh) + b_ref[...]

    return pl.pallas_call(
        body, name="ada_fwd",
        out_shape=jax.ShapeDtypeStruct((N_DEV, n_sh), F32),
        in_specs=[_vmem_spec()] * 3, out_specs=_vmem_spec(),
        scratch_shapes=[
            pltpu.VMEM((N_DEV, 8, D_MODEL), F32),
            pltpu.VMEM((N_DEV, 8, n_sh), F32),
            pltpu.VMEM((N_DEV, 8, n_sh), F32),
            pltpu.SemaphoreType.DMA((2 * (N_DEV - 1),)),
            pltpu.SemaphoreType.DMA((2 * (N_DEV - 1),)),
        ],
    )(c8, w_ada, b_ada8)


def _gather_weights(shards):
    n = len(shards)

    def body(*refs):
        ins, outs = refs[:n], refs[n:2 * n]
        ssem, rsem = refs[2 * n], refs[2 * n + 1]
        pos = _mesh_pos()
        me = _lin(pos)
        for a in range(n):
            outs[a][me] = ins[a][...].astype(BF16)
        for a in range(n):
            _all_gather(pos, outs[a].at[me], outs[a], ssem, rsem, a * (N_DEV - 1))

    return pl.pallas_call(
        body, name="gather_weights",
        out_shape=[jax.ShapeDtypeStruct((N_DEV,) + s.shape, BF16) for s in shards],
        in_specs=[_vmem_spec()] * n, out_specs=[_vmem_spec()] * n,
        scratch_shapes=[
            pltpu.SemaphoreType.DMA((n * (N_DEV - 1),)),
            pltpu.SemaphoreType.DMA((n * (N_DEV - 1),)),
        ],
        compiler_params=pltpu.CompilerParams(vmem_limit_bytes=VMEM_BIG),
    )(*shards)


def _rope(t, cosf, sin_a, sin_b):
    return t * cosf + pltpu.roll(t, 112, 1) * sin_a + pltpu.roll(t, 16, 1) * sin_b


def _rope_t(d, cosf, sin_a, sin_b):
    return d * cosf + pltpu.roll(d * sin_a, 16, 1) + pltpu.roll(d * sin_b, 112, 1)


def _head_rms(t):
    return lax.rsqrt(jnp.sum(t * t, axis=1, keepdims=True) * (1.0 / MLA_QK) + EPS)


def _rms_bwd(dxhat_w, xhat, r, n):
    return r * (dxhat_w - xhat * (jnp.sum(dxhat_w * xhat, axis=1, keepdims=True) * (1.0 / n)))


def _mla_latents(cq, ckv, kr, qln, kvln, wuq, wukv):
    rq = lax.rsqrt(jnp.mean(cq * cq, axis=1, keepdims=True) + EPS)
    rkv = lax.rsqrt(jnp.mean(ckv * ckv, axis=1, keepdims=True) + EPS)
    cq_hat = cq * rq
    ckv_hat = ckv * rkv
    cqn = (cq_hat * qln).astype(BF16)
    ckvn = (ckv_hat * kvln).astype(BF16)
    q_all = _dot_nn(cqn, wuq)
    kv = _dot_nn(ckvn, wukv)
    kr64 = pltpu.roll(kr, 64, 1)
    return rq, rkv, cq_hat, ckv_hat, cqn, ckvn, q_all, kv, kr64


def _fwd_pre(x, shift, scale, norm_w, w_in_r, qln, wuq, kvln, wukv, qhn, khn, cosf, sin_a, sin_b):
    s_len = x.shape[0]

    def body(x_ref, shift_ref, scale_ref, nw_ref, w_ref, qln_ref, wuq_ref, kvln_ref, wukv_ref,
             qhn_ref, khn_ref, cos_ref, sa_ref, sb_ref,
             hb_ref, qkv_ref, gsb_ref, cq_ref, ckv_ref, gmla_ref, kr_ref, qm_ref, km_ref, vm_ref):
        xv = x_ref[...]
        r = lax.rsqrt(jnp.mean(xv * xv, axis=1, keepdims=True) + EPS)
        h = (xv * r) * nw_ref[...] * (1.0 + scale_ref[...]) + shift_ref[...]
        hb = h.astype(BF16)
        hb_ref[...] = hb
        qkv_ref[...] = _dot_nn(hb, w_ref[:, 0:1536]).astype(BF16)
        gsb_ref[...] = _dot_nn(hb, w_ref[:, 1536:2048])
        cq = _dot_nn(hb, w_ref[:, 2048:2432])
        ckv = _dot_nn(hb, w_ref[:, 2432:2688])
        gmla_ref[...] = _dot_nn(hb, w_ref[:, 2688:3200])
        kr = _dot_nn(hb, w_ref[:, 3200:3328])
        cq_ref[...] = cq
        ckv_ref[...] = ckv
        kr_ref[...] = kr
        _, _, _, _, _, _, q_all, kv, kr64 = _mla_latents(
            cq, ckv, kr, qln_ref[...], kvln_ref[...], wuq_ref[...], wukv_ref[...])
        cosf, sa, sb = cos_ref[...], sa_ref[...], sb_ref[...]
        qhn_v, khn_v = qhn_ref[...], khn_ref[...]
        low = _lane_iota((TS, LANES)) < NOPE
        for hd in range(HEADS):
            blk = slice(LANES * hd, LANES * (hd + 1))
            qb = q_all[:, blk]
            qm_ref[:, blk] = _rope(qb * _head_rms(qb) * qhn_v, cosf, sa, sb).astype(BF16)
            kb = jnp.where(low, kv[:, blk], kr64)
            km_ref[:, blk] = _rope(kb * _head_rms(kb) * khn_v, cosf, sa, sb).astype(BF16)
        for p in range(HEADS // 2):
            even = kv[:, LANES * 2 * p:LANES * (2 * p + 1)]
            odd = kv[:, LANES * (2 * p + 1):LANES * (2 * p + 2)]
            vm_ref[:, LANES * p:LANES * (p + 1)] = jnp.where(low, pltpu.roll(even, 64, 1), odd).astype(BF16)

    def tile(width):
        return pl.BlockSpec((TS, width), lambda i: (i, 0))

    def full(a):
        return pl.BlockSpec(a.shape, lambda i: (0, 0))

    out_widths = [(D_MODEL, BF16), (1536, BF16), (512, F32), (Q_LORA, F32), (KV_LORA, F32),
                  (512, F32), (LANES, F32), (1024, BF16), (1024, BF16), (512, BF16)]
    return pl.pallas_call(
        body, name="fwd_pre", grid=(s_len // TS,),
        out_shape=[jax.ShapeDtypeStruct((s_len, w), dt) for w, dt in out_widths],
        in_specs=[tile(D_MODEL), full(shift), full(scale), full(norm_w), full(w_in_r), full(qln), full(wuq),
                  full(kvln), full(wukv), full(qhn), full(khn), tile(LANES), tile(LANES), tile(LANES)],
        out_specs=[tile(w) for w, _ in out_widths],
        compiler_params=pltpu.CompilerParams(dimension_semantics=("parallel",), vmem_limit_bytes=VMEM_BIG),
    )(x, shift, scale, norm_w, w_in_r, qln, wuq, kvln, wukv, qhn, khn, cosf, sin_a, sin_b)


CUM_W = 256


def _tri(strict):
    j = lax.broadcasted_iota(jnp.int32, (CUM_W, CUM_W), 0)
    s = lax.broadcasted_iota(jnp.int32, (CUM_W, CUM_W), 1)
    return (j > s if strict else j >= s).astype(BF16)


def _suffix_sums(a, tri_m, carry, hi=None):
    if hi is None:
        hi = a.astype(BF16)
    n = a.shape[1] // CUM_W
    outs = [None] * n
    for i in reversed(range(n)):
        cols = slice(CUM_W * i, CUM_W * (i + 1))
        lo = (a[:, cols] - hi[:, cols].astype(F32)).astype(BF16)
        outs[i] = _dot_nn(hi[:, cols], tri_m) + _dot_nn(lo, tri_m) + carry
        carry = carry + _rowsum(a[:, cols])
    return (outs[0] if n == 1 else jnp.concatenate(outs, axis=1)), carry


def _sb_weights(qm, kb, carry, tri_u, diag):
    z = _dot_nt(qm, kb)
    nz = -z
    lk = jnp.minimum(nz, 0.0) - jnp.log(1.0 + jnp.exp(jnp.minimum(z, nz)))
    if diag:
        t = lax.broadcasted_iota(jnp.int32, (TQ, TQ), 0)
        s = lax.broadcasted_iota(jnp.int32, (TQ, TQ), 1)
        valid = s < t
        lk = jnp.where(valid, lk, 0.0)
    lk_hi = lk.astype(BF16)
    after, carry = _suffix_sums(lk, tri_u, carry, lk_hi)
    logw = z + lk + after
    if diag:
        logw = jnp.where(valid, logw, NEG)
    return lk_hi, jnp.exp(logw), carry


SB_SCALE = 0.125


def _head_masks():
    lane = _lane_iota((1, LANES))
    return [lane < 64, lane >= 64]


def _masked(hm, a):
    return jnp.where(hm, a, jnp.zeros_like(a))


def _rowsum(a):
    return jnp.sum(a, axis=1, keepdims=True)


def _key_rows(kj):
    return pl.ds(pl.multiple_of(kj * TQ, TQ), TQ)


def _over_key_tiles(count, fn, st, ascending):
    n_full = count // KEY_UNROLL
    n_rest = count - n_full * KEY_UNROLL

    def group(g, s_):
        return fn([g * KEY_UNROLL + (u if ascending else KEY_UNROLL - 1 - u) for u in range(KEY_UNROLL)], s_)

    if ascending:
        st = lax.fori_loop(0, n_full, group, st)
        return lax.fori_loop(0, n_rest, lambda i, s_: fn([n_full * KEY_UNROLL + i], s_), st)
    st = lax.fori_loop(0, n_rest, lambda i, s_: fn([count - 1 - i], s_), st)
    return lax.fori_loop(0, n_full, lambda i, s_: group(n_full - 1 - i, s_), st)


def _each_tile(block):
    def trip(tiles, st):
        for kj in tiles:
            st = block(_key_rows(kj), st, False)
        return st
    return trip


STAGE_SLOTS = 2 * KEY_UNROLL


def _stage_copies(to_hbm, hbm_refs, scr_refs, sems, pair, qi, kj):
    slot = (qi - kj) % STAGE_SLOTS
    out = []
    for h in range(2):
        for a in range(2):
            hbm, scr = hbm_refs[a].at[2 * pair + h, qi, kj], scr_refs[a].at[slot, h]
            sem = sems.at[4 * slot + 2 * h + a]
            out.append(pltpu.make_async_copy(scr, hbm, sem) if to_hbm else pltpu.make_async_copy(hbm, scr, sem))
    return out


def _sb_fwd(qkv):
    s_len = qkv.shape[0]
    nq = s_len // TQ

    def body(q_ref, k_ref, v_ref, o_ref, w_hbm, l_hbm, w_scr, l_scr, sems):
        pair, qi = pl.program_id(0), pl.program_id(1)
        q = q_ref[...]
        tri_u = _tri(True)
        masks = _head_masks()
        qms = [_masked(hm, q) * SB_SCALE for hm in masks]

        def copies(kj, of_qi=qi):
            return _stage_copies(True, (w_hbm, l_hbm), (w_scr, l_scr), sems, pair, of_qi, kj)

        def drain(of_qi):
            for kj in range(STAGE_SLOTS):
                @pl.when(kj <= of_qi)
                def _():
                    for cp in copies(kj, of_qi):
                        cp.wait()

        def block(kj, st, diag, before_staging=None):
            rows = _key_rows(kj)
            slot = (qi - kj) % STAGE_SLOTS
            kb, vb = k_ref[rows, :], v_ref[rows, :]
            carries, acc = list(st[:2]), st[2]
            staged = []
            for h in range(2):
                lk_hi, w, carries[h] = _sb_weights(qms[h], kb, carries[h], tri_u, diag)
                wb = w.astype(BF16)
                staged.append((wb, lk_hi))
                acc = acc + _dot_nn(wb, _masked(masks[h], vb))
            if before_staging is not None:
                before_staging()
            for h in range(2):
                w_scr[slot, h], l_scr[slot, h] = staged[h]
            return carries[0], carries[1], acc

        def trip(tiles, st):
            for kj in tiles:
                @pl.when(qi - kj >= STAGE_SLOTS)
                def _():
                    for cp in copies(kj + STAGE_SLOTS):
                        cp.wait()
            for kj in tiles:
                st = block(kj, st, False)
            for kj in tiles:
                for cp in copies(kj):
                    cp.start()
            return st

        def drain_previous_step():
            @pl.when(jnp.logical_or(pair > 0, qi > 0))
            def _():
                drain(jnp.where(qi == 0, nq - 1, qi - 1))

        zc = jnp.zeros((TQ, 1), F32)
        st = block(qi, (zc, zc, jnp.zeros((TQ, LANES), F32)), True, drain_previous_step)
        for cp in copies(qi):
            cp.start()
        st = _over_key_tiles(qi, trip, st, ascending=False)
        o_ref[...] = st[2]

        @pl.when(jnp.logical_and(pair == HEADS // 2 - 1, qi == nq - 1))
        def _():
            drain(qi)

    saved = jax.ShapeDtypeStruct((HEADS, nq, nq, TQ, TQ), BF16)
    stage = pltpu.VMEM((STAGE_SLOTS, 2, TQ, TQ), BF16)
    return pl.pallas_call(
        body, name="sb_fwd", grid=(HEADS // 2, nq),
        out_shape=[jax.ShapeDtypeStruct((s_len, SB_WIDTH), F32), saved, saved],
        in_specs=[pl.BlockSpec((TQ, LANES), lambda p, i: (i, p)),
                  pl.BlockSpec((s_len, LANES), lambda p, i: (0, 4 + p)),
                  pl.BlockSpec((s_len, LANES), lambda p, i: (0, 8 + p))],
        out_specs=[pl.BlockSpec((TQ, LANES), lambda p, i: (i, p)), _any_spec(), _any_spec()],
        scratch_shapes=[stage, stage, pltpu.SemaphoreType.DMA((4 * STAGE_SLOTS,))],
        compiler_params=pltpu.CompilerParams(dimension_semantics=("arbitrary", "arbitrary"),
                                             vmem_limit_bytes=VMEM_BIG),
    )(qkv, qkv, qkv)


def _prefix_sums(a, tri_m, carry):
    n = a.shape[1] // CUM_W
    outs = [None] * n
    for i in range(n):
        cols = slice(CUM_W * i, CUM_W * (i + 1))
        outs[i] = _dot_nn(a[:, cols].astype(BF16), tri_m) + carry
        carry = carry + _rowsum(a[:, cols])
    return (outs[0] if n == 1 else jnp.concatenate(outs, axis=1)), carry


def _sb_bwd(qkv, do, w_saved, l_saved):
    s_len = qkv.shape[0]

    def body(q_ref, k_ref, v_ref, do_ref, w_hbm, l_hbm, dq_ref, dk_ref, dv_ref, w_scr, l_scr, sems):
        pair, qi = pl.program_id(0), pl.program_id(1)

        nq = s_len // TQ

        def copies(kj, of_pair=pair, of_qi=qi):
            return _stage_copies(False, (w_hbm, l_hbm), (w_scr, l_scr), sems, of_pair, of_qi, kj)

        def start_first_tiles(of_pair, of_qi):
            for cp in copies(0, of_pair, of_qi):
                cp.start()

            @pl.when(of_qi >= 1)
            def _():
                for cp in copies(1, of_pair, of_qi):
                    cp.start()

        @pl.when(jnp.logical_and(pair == 0, qi == 0))
        def _():
            start_first_tiles(pair, qi)

        @pl.when(qi == 0)
        def _():
            dk_ref[...] = jnp.zeros_like(dk_ref)
            dv_ref[...] = jnp.zeros_like(dv_ref)

        q = q_ref[...]
        do_v = do_ref[...]
        j = lax.broadcasted_iota(jnp.int32, (CUM_W, CUM_W), 0)
        s = lax.broadcasted_iota(jnp.int32, (CUM_W, CUM_W), 1)
        tri_before = (j < s).astype(BF16)
        masks = _head_masks()
        qms = [_masked(hm, q) * SB_SCALE for hm in masks]
        doms = [_masked(hm, do_v) for hm in masks]

        def block(kj, st):
            rows = _key_rows(kj)
            slot = (qi - kj) % STAGE_SLOTS
            kb, vb = k_ref[rows, :], v_ref[rows, :]
            carries, dqs = list(st[0:2]), list(st[2:4])
            dk_blk = jnp.zeros((TQ, LANES), F32)
            dv_blk = jnp.zeros((TQ, LANES), F32)
            for h in range(2):
                wb = w_scr[slot, h]
                d_l = _dot_nt(doms[h], vb) * wb.astype(F32)
                before, carries[h] = _prefix_sums(d_l, tri_before, carries[h])
                keep = jnp.exp(l_scr[slot, h].astype(F32))
                dzb = (d_l * keep - before * (1.0 - keep)).astype(BF16)
                dk_blk = dk_blk + _dot_tn(dzb, qms[h])
                dv_blk = dv_blk + _dot_tn(wb, doms[h])
                dqs[h] = dqs[h] + _dot_nn(dzb, kb)
            dk_ref[rows, :] += dk_blk
            dv_ref[rows, :] += dv_blk
            return (*carries, *dqs)

        def trip(tiles, st):
            for kj in tiles:
                @pl.when(kj + 2 <= qi)
                def _():
                    for cp in copies(kj + 2):
                        cp.start()
            for kj in tiles:
                for cp in copies(kj):
                    cp.wait()
            for kj in tiles:
                st = block(kj, st)
            return st

        zc = jnp.zeros((TQ, 1), F32)
        za = jnp.zeros((TQ, LANES), F32)
        st = _over_key_tiles(qi + 1, trip, (zc, zc, za, za), ascending=True)
        dq_ref[...] = jnp.where(masks[0], st[2], st[3]) * SB_SCALE

        @pl.when(jnp.logical_or(pair < HEADS // 2 - 1, qi < nq - 1))
        def _():
            wraps = qi == nq - 1
            start_first_tiles(jnp.where(wraps, pair + 1, pair), jnp.where(wraps, 0, qi + 1))

    tile = pl.BlockSpec((TQ, LANES), lambda p, i: (i, p))
    col = pl.BlockSpec((s_len, LANES), lambda p, i: (0, p))
    stage = pltpu.VMEM((STAGE_SLOTS, 2, TQ, TQ), BF16)
    return pl.pallas_call(
        body, name="sb_bwd", grid=(HEADS // 2, s_len // TQ),
        out_shape=[jax.ShapeDtypeStruct((s_len, SB_WIDTH), F32)] * 3,
        in_specs=[tile,
                  pl.BlockSpec((s_len, LANES), lambda p, i: (0, 4 + p)),
                  pl.BlockSpec((s_len, LANES), lambda p, i: (0, 8 + p)),
                  tile, _any_spec(), _any_spec()],
        out_specs=[tile, col, col],
        scratch_shapes=[stage, stage, pltpu.SemaphoreType.DMA((4 * STAGE_SLOTS,))],
        compiler_params=pltpu.CompilerParams(dimension_semantics=("arbitrary", "arbitrary"),
                                             vmem_limit_bytes=VMEM_BIG),
    )(qkv, qkv, qkv, do, w_saved, l_saved)


MLA_SCALE = 1.0 / math.sqrt(MLA_QK)


def _causal_mask():
    t = lax.broadcasted_iota(jnp.int32, (TQ, TQ), 0)
    s = lax.broadcasted_iota(jnp.int32, (TQ, TQ), 1)
    return s <= t


def _head_lanes(h):
    return slice(LANES * h, LANES * (h + 1))


def _mla_fwd(qm, km, vm):
    s_len = qm.shape[0]

    def body(q_ref, k_ref, v_ref, o_ref, lse_ref):
        qi = pl.program_id(1)
        masks = _head_masks()
        qhs = [q_ref[:, _head_lanes(h)] for h in range(2)]

        def block(rows, st, diag):
            vb = v_ref[rows, :]
            ms, ls, acc = list(st[0:2]), list(st[2:4]), st[4]
            alphas, pvs = [], []
            for h in range(2):
                s = _dot_nt(qhs[h], k_ref[rows, _head_lanes(h)]) * MLA_SCALE
                if diag:
                    s = jnp.where(_causal_mask(), s, NEG)
                m_new = jnp.maximum(ms[h], jnp.max(s, axis=1, keepdims=True))
                p = jnp.exp(s - m_new)
                alphas.append(jnp.exp(ms[h] - m_new))
                ls[h] = alphas[h] * ls[h] + _rowsum(p)
                ms[h] = m_new
                pvs.append(_dot_nn(p.astype(BF16), _masked(masks[h], vb)))
            acc = jnp.where(masks[0], alphas[0], alphas[1]) * acc + pvs[0] + pvs[1]
            return (*ms, *ls, acc)

        neg = jnp.full((TQ, 1), NEG, F32)
        zc = jnp.zeros((TQ, 1), F32)
        st = (neg, neg, zc, zc, jnp.zeros((TQ, LANES), F32))
        st = _over_key_tiles(qi, _each_tile(block), st, ascending=True)
        m0, m1, l0, l1, acc = block(_key_rows(qi), st, True)
        o_ref[...] = acc / jnp.where(masks[0], l0, l1)
        lse_ref[0] = m0 + jnp.log(l0)
        lse_ref[1] = m1 + jnp.log(l1)

    return pl.pallas_call(
        body, name="mla_fwd", grid=(HEADS // 2, s_len // TQ),
        out_shape=[jax.ShapeDtypeStruct((s_len, MLA_WIDTH), F32),
                   jax.ShapeDtypeStruct((HEADS, s_len, 1), F32)],
        in_specs=[pl.BlockSpec((TQ, 2 * LANES), lambda p, i: (i, p)),
                  pl.BlockSpec((s_len, 2 * LANES), lambda p, i: (0, p)),
                  pl.BlockSpec((s_len, LANES), lambda p, i: (0, p))],
        out_specs=[pl.BlockSpec((TQ, LANES), lambda p, i: (i, p)),
                   pl.BlockSpec((2, TQ, 1), lambda p, i: (p, i, 0))],
        compiler_params=pltpu.CompilerParams(dimension_semantics=("parallel", "parallel"),
                                             vmem_limit_bytes=VMEM_BIG),
    )(qm, km, vm)


def _mla_bwd(qm, km, vm, o, do, lse):
    s_len = qm.shape[0]

    def body(q_ref, k_ref, v_ref, o_ref, do_ref, lse_ref, dq_ref, dk_ref, dv_ref):
        qi = pl.program_id(1)

        @pl.when(qi == 0)
        def _():
            dk_ref[...] = jnp.zeros_like(dk_ref)
            dv_ref[...] = jnp.zeros_like(dv_ref)

        do_v = do_ref[...]
        od = o_ref[...] * do_v.astype(F32)
        masks = _head_masks()
        qhs = [q_ref[:, _head_lanes(h)] for h in range(2)]
        doms = [_masked(hm, do_v) for hm in masks]
        deltas = [_rowsum(jnp.where(hm, od, 0.0)) for hm in masks]
        lses = [lse_ref[h] for h in range(2)]

        def block(rows, dqs, diag):
            vb = v_ref[rows, :]
            dqs = list(dqs)
            dv_blk = jnp.zeros((TQ, LANES), F32)
            for h in range(2):
                kb = k_ref[rows, _head_lanes(h)]
                s = _dot_nt(qhs[h], kb) * MLA_SCALE
                if diag:
                    s = jnp.where(_causal_mask(), s, NEG)
                p = jnp.exp(s - lses[h])
                dp = _dot_nt(doms[h], vb)
                ds = (p * (dp - deltas[h]) * MLA_SCALE).astype(BF16)
                dk_ref[rows, _head_lanes(h)] += _dot_tn(ds, qhs[h])
                dv_blk = dv_blk + _dot_tn(p.astype(BF16), doms[h])
                dqs[h] = dqs[h] + _dot_nn(ds, kb)
            dv_ref[rows, :] += dv_blk
            return tuple(dqs)

        za = jnp.zeros((TQ, LANES), F32)
        dqs = _over_key_tiles(qi, _each_tile(block), (za, za), ascending=True)
        dqs = block(_key_rows(qi), dqs, True)
        dq_ref[:, _head_lanes(0)] = dqs[0]
        dq_ref[:, _head_lanes(1)] = dqs[1]

    return pl.pallas_call(
        body, name="mla_bwd", grid=(HEADS // 2, s_len // TQ),
        out_shape=[jax.ShapeDtypeStruct((s_len, HEADS * LANES), F32),
                   jax.ShapeDtypeStruct((s_len, HEADS * LANES), F32),
                   jax.ShapeDtypeStruct((s_len, MLA_WIDTH), F32)],
        in_specs=[pl.BlockSpec((TQ, 2 * LANES), lambda p, i: (i, p)),
                  pl.BlockSpec((s_len, 2 * LANES), lambda p, i: (0, p)),
                  pl.BlockSpec((s_len, LANES), lambda p, i: (0, p)),
                  pl.BlockSpec((TQ, LANES), lambda p, i: (i, p)),
                  pl.BlockSpec((TQ, LANES), lambda p, i: (i, p)),
                  pl.BlockSpec((2, TQ, 1), lambda p, i: (p, i, 0))],
        out_specs=[pl.BlockSpec((TQ, 2 * LANES), lambda p, i: (i, p)),
                   pl.BlockSpec((s_len, 2 * LANES), lambda p, i: (0, p)),
                   pl.BlockSpec((s_len, LANES), lambda p, i: (0, p))],
        compiler_params=pltpu.CompilerParams(dimension_semantics=("parallel", "arbitrary"),
                                             vmem_limit_bytes=VMEM_BIG),
    )(qm, km, vm, o, do, lse)


def _mid(o_sb, g_sb, o_mla, g_mla, x, target, gate, w_out):
    s_len = x.shape[0]

    def body(osb_ref, gsb_ref, omla_ref, gmla_ref, x_ref, t_ref, gate_ref, w_ref,
             dy_ref, dosb_ref, dgsb_ref, domla_ref, dgmla_ref, gw_ref, dgate_ref, loss_ref):
        @pl.when(pl.program_id(0) == 0)
        def _():
            gw_ref[...] = jnp.zeros_like(gw_ref)
            dgate_ref[...] = jnp.zeros_like(dgate_ref)
            loss_ref[...] = jnp.zeros_like(loss_ref)

        o1, g1, o2, g2 = osb_ref[...], gsb_ref[...], omla_ref[...], gmla_ref[...]
        s1, s2 = _sigmoid(g1), _sigmoid(g2)
        mixed = jnp.concatenate([o1 * (g1 * s1), o2 * (g2 * s2)], axis=1).astype(BF16)
        w = w_ref[...]
        gate_v = gate_ref[...]
        u = _dot_nn(mixed, w)
        err = x_ref[...] + gate_v * u - t_ref[...]
        loss_ref[...] += jnp.sum(err * err, axis=0, keepdims=True)
        dy = err * (1.0 / D_MODEL)
        dy_ref[...] = dy
        dgate_ref[...] += jnp.sum(dy * u, axis=0, keepdims=True)
        du = (dy * gate_v).astype(BF16)
        gw_ref[...] += _dot_tn(mixed, du)
        dmixed = _dot_nt(du, w)
        d1, d2 = dmixed[:, :SB_WIDTH], dmixed[:, SB_WIDTH:]
        dosb_ref[...] = (d1 * (g1 * s1)).astype(BF16)
        dgsb_ref[...] = (d1 * o1 * (s1 * (1.0 + g1 * (1.0 - s1)))).astype(BF16)
        domla_ref[...] = (d2 * (g2 * s2)).astype(BF16)
        dgmla_ref[...] = (d2 * o2 * (s2 * (1.0 + g2 * (1.0 - s2)))).astype(BF16)

    def tile(width):
        return pl.BlockSpec((TS, width), lambda i: (i, 0))

    def full(shape):
        return pl.BlockSpec(shape, lambda i: (0, 0))

    return pl.pallas_call(
        body, name="mid", grid=(s_len // TS,),
        out_shape=[jax.ShapeDtypeStruct((s_len, D_MODEL), F32)]
        + [jax.ShapeDtypeStruct((s_len, 512), BF16)] * 4
        + [jax.ShapeDtypeStruct((D_MODEL, D_MODEL), F32),
           jax.ShapeDtypeStruct((1, D_MODEL), F32), jax.ShapeDtypeStruct((1, D_MODEL), F32)],
        in_specs=[tile(512)] * 4 + [tile(D_MODEL), tile(D_MODEL), full((1, D_MODEL)), full((D_MODEL, D_MODEL))],
        out_specs=[tile(D_MODEL)] + [tile(512)] * 4
        + [full((D_MODEL, D_MODEL)), full((1, D_MODEL)), full((1, D_MODEL))],
        compiler_params=pltpu.CompilerParams(dimension_semantics=("arbitrary",), vmem_limit_bytes=VMEM_BIG),
    )(o_sb, g_sb, o_mla, g_mla, x, target, gate, w_out)


def _mla_pre_bwd(dq, dk, dv, cq, ckv, kr, qln, wuq, kvln, wukv, qhn, khn, cosf, sin_a, sin_b):
    s_len = cq.shape[0]

    def body(dq_ref, dk_ref, dv_ref, cq_ref, ckv_ref, kr_ref, qln_ref, wuq_ref, kvln_ref, wukv_ref,
             qhn_ref, khn_ref, cos_ref, sa_ref, sb_ref,
             dcq_ref, dckv_ref, dkr_ref, gwuq_ref, gwukv_ref, gqhn_ref, gkhn_ref, gqln_ref, gkvln_ref,
             dqa_ref, dkv_ref):
        @pl.when(pl.program_id(0) == 0)
        def _():
            for r_ in (gwuq_ref, gwukv_ref, gqhn_ref, gkhn_ref, gqln_ref, gkvln_ref):
                r_[...] = jnp.zeros_like(r_)

        cq, ckv = cq_ref[...], ckv_ref[...]
        qln_v, kvln_v = qln_ref[...], kvln_ref[...]
        wuq_v, wukv_v = wuq_ref[...], wukv_ref[...]
        rq, rkv, cq_hat, ckv_hat, cqn, ckvn, q_all, kv, kr64 = _mla_latents(
            cq, ckv, kr_ref[...], qln_v, kvln_v, wuq_v, wukv_v)
        cosf, sa, sb = cos_ref[...], sa_ref[...], sb_ref[...]
        qhn_v, khn_v = qhn_ref[...], khn_ref[...]
        lane = _lane_iota((TS, LANES))
        low = lane < NOPE
        g_qhn = jnp.zeros((1, LANES), F32)
        g_khn = jnp.zeros((1, LANES), F32)
        dkr64 = jnp.zeros((TS, LANES), F32)
        for hd in range(HEADS):
            blk = slice(LANES * hd, LANES * (hd + 1))
            qb = q_all[:, blk]
            r = _head_rms(qb)
            xh = qb * r
            dn = _rope_t(dq_ref[:, blk], cosf, sa, sb)
            g_qhn = g_qhn + jnp.sum(dn * xh, axis=0, keepdims=True)
            dqa_ref[:, blk] = _rms_bwd(dn * qhn_v, xh, r, MLA_QK).astype(BF16)

            kb = jnp.where(low, kv[:, blk], kr64)
            r = _head_rms(kb)
            xh = kb * r
            dn = _rope_t(dk_ref[:, blk], cosf, sa, sb)
            g_khn = g_khn + jnp.sum(dn * xh, axis=0, keepdims=True)
            dkb = _rms_bwd(dn * khn_v, xh, r, MLA_QK)
            dkr64 = dkr64 + jnp.where(low, 0.0, dkb)
            dvp = dv_ref[:, LANES * (hd // 2):LANES * (hd // 2 + 1)]
            dvh = pltpu.roll(dvp, 64, 1) if hd % 2 == 0 else dvp
            dkv_ref[:, blk] = jnp.where(low, dkb, dvh).astype(BF16)
        gqhn_ref[...] += g_qhn
        gkhn_ref[...] += g_khn
        dkr_ref[...] = pltpu.roll(dkr64, 64, 1).astype(BF16)

        dqa = dqa_ref[...]
        gwuq_ref[...] += _dot_tn(cqn, dqa)
        dcqn = _dot_nt(dqa, wuq_v)
        gqln_ref[...] += jnp.sum(dcqn * cq_hat, axis=0, keepdims=True)
        dcq_ref[...] = _rms_bwd(dcqn * qln_v, cq_hat, rq, Q_LORA).astype(BF16)

        dkv = dkv_ref[...]
        gwukv_ref[...] += _dot_tn(ckvn, dkv)
        dckvn = _dot_nt(dkv, wukv_v)
        gkvln_ref[...] += jnp.sum(dckvn * ckv_hat, axis=0, keepdims=True)
        dckv_ref[...] = _rms_bwd(dckvn * kvln_v, ckv_hat, rkv, KV_LORA).astype(BF16)

    def tile(width):
        return pl.BlockSpec((TS, width), lambda i: (i, 0))

    def full(shape):
        return pl.BlockSpec(shape, lambda i: (0, 0))

    acc_shapes = [(Q_LORA, 1024), (KV_LORA, 1024), (1, LANES), (1, LANES), (1, Q_LORA), (1, KV_LORA)]
    return pl.pallas_call(
        body, name="mla_pre_bwd", grid=(s_len // TS,),
        out_shape=[jax.ShapeDtypeStruct((s_len, Q_LORA), BF16), jax.ShapeDtypeStruct((s_len, KV_LORA), BF16),
                   jax.ShapeDtypeStruct((s_len, LANES), BF16)]
        + [jax.ShapeDtypeStruct(s, F32) for s in acc_shapes],
        in_specs=[tile(1024), tile(1024), tile(512), tile(Q_LORA), tile(KV_LORA), tile(LANES),
                  full(qln.shape), full(wuq.shape), full(kvln.shape), full(wukv.shape),
                  full(qhn.shape), full(khn.shape), tile(LANES), tile(LANES), tile(LANES)],
        out_specs=[tile(Q_LORA), tile(KV_LORA), tile(LANES)] + [full(s) for s in acc_shapes],
        scratch_shapes=[pltpu.VMEM((TS, 1024), BF16), pltpu.VMEM((TS, 1024), BF16)],
        compiler_params=pltpu.CompilerParams(dimension_semantics=("arbitrary",), vmem_limit_bytes=VMEM_BIG),
    )(dq, dk, dv, cq, ckv, kr, qln, wuq, kvln, wukv, qhn, khn, cosf, sin_a, sin_b)


def _dproj_bwd(dq_sb, dk_sb, dv_sb, dg_sb, dcq, dckv, dg_mla, dkr, w_in_r, x, dy, norm_w, scale):
    s_len = x.shape[0]

    def body(dq_ref, dk_ref, dv_ref, dg_ref, dcq_ref, dckv_ref, dgm_ref, dkr_ref, w_ref, x_ref, dy_ref,
             nw_ref, scale_ref, dp_ref, gx_ref, dshift_ref, dscale_ref, dnw_ref):
        @pl.when(pl.program_id(0) == 0)
        def _():
            for r_ in (dshift_ref, dscale_ref, dnw_ref):
                r_[...] = jnp.zeros_like(r_)

        dp_ref[:, 0:512] = dq_ref[...].astype(BF16)
        dp_ref[:, 512:1024] = dk_ref[...].astype(BF16)
        dp_ref[:, 1024:1536] = dv_ref[...].astype(BF16)
        dp_ref[:, 1536:2048] = dg_ref[...]
        dp_ref[:, 2048:2432] = dcq_ref[...]
        dp_ref[:, 2432:2688] = dckv_ref[...]
        dp_ref[:, 2688:3200] = dgm_ref[...]
        dp_ref[:, 3200:3328] = dkr_ref[...]
        dh = _dot_nt(dp_ref[...], w_ref[...])
        xv = x_ref[...]
        r = lax.rsqrt(jnp.mean(xv * xv, axis=1, keepdims=True) + EPS)
        xh = xv * r
        nw = nw_ref[...]
        dshift_ref[...] += jnp.sum(dh, axis=0, keepdims=True)
        dscale_ref[...] += jnp.sum(dh * (xh * nw), axis=0, keepdims=True)
        dxnw = dh * (1.0 + scale_ref[...])
        dnw_ref[...] += jnp.sum(dxnw * xh, axis=0, keepdims=True)
        gx_ref[...] = dy_ref[...] + _rms_bwd(dxnw * nw, xh, r, D_MODEL)

    def tile(width):
        return pl.BlockSpec((TS, width), lambda i: (i, 0))

    def full(shape):
        return pl.BlockSpec(shape, lambda i: (0, 0))

    vec = (1, D_MODEL)
    return pl.pallas_call(
        body, name="dproj_bwd", grid=(s_len // TS,),
        out_shape=[jax.ShapeDtypeStruct((s_len, IN_COLS_R), BF16), jax.ShapeDtypeStruct((s_len, D_MODEL), F32)]
        + [jax.ShapeDtypeStruct(vec, F32)] * 3,
        in_specs=[tile(512)] * 4 + [tile(Q_LORA), tile(KV_LORA), tile(512), tile(LANES),
                                    full(w_in_r.shape), tile(D_MODEL), tile(D_MODEL), full(vec), full(vec)],
        out_specs=[tile(IN_COLS_R), tile(D_MODEL)] + [full(vec)] * 3,
        compiler_params=pltpu.CompilerParams(dimension_semantics=("arbitrary",), vmem_limit_bytes=VMEM_BIG),
    )(dq_sb, dk_sb, dv_sb, dg_sb, dcq, dckv, dg_mla, dkr, w_in_r, x, dy, norm_w, scale)


def _grad_w_in(hb, dproj):
    s_len = hb.shape[0]
    n_half = IN_COLS_R // 2

    def body(h_ref, d_ref, g_ref):
        @pl.when(pl.program_id(1) == 0)
        def _():
            g_ref[...] = jnp.zeros_like(g_ref)

        g_ref[...] += _dot_tn(h_ref[...], d_ref[...])

    return pl.pallas_call(
        body, name="grad_w_in", grid=(2, s_len // TN_S),
        out_shape=jax.ShapeDtypeStruct((D_MODEL, IN_COLS_R), F32),
        in_specs=[pl.BlockSpec((TN_S, D_MODEL), lambda n, s: (s, 0)),
                  pl.BlockSpec((TN_S, n_half), lambda n, s: (s, n))],
        out_specs=pl.BlockSpec((D_MODEL, n_half), lambda n, s: (0, n)),
        compiler_params=pltpu.CompilerParams(dimension_semantics=("parallel", "arbitrary"),
                                             vmem_limit_bytes=VMEM_BIG),
    )(hb, dproj)


def _vec_exchange(gpack, ccol, wpack, mpack, vpack, w_ada, m_ada, v_ada):
    n_sh = w_ada.shape[1]

    def body(g_ref, cc_ref, wp_ref, mp_ref, vp_ref, wa_ref, ma_ref, va_ref,
             og_ref, od_ref, om_ref, ov_ref, ag_ref, ad_ref, am_ref, av_ref,
             gall_ref, call_ref, ssem, rsem):
        pos = _mesh_pos()
        me = _lin(pos)
        gall_ref[me] = g_ref[...]
        call_ref[me] = cc_ref[...]
        _all_gather(pos, g_ref, gall_ref, ssem, rsem, 0)
        _all_gather(pos, cc_ref, call_ref, ssem, rsem, N_DEV - 1)

        tot = gall_ref[0]
        for j in range(1, N_DEV):
            tot = tot + gall_ref[j]
        og_ref[...] = tot
        od_ref[...], om_ref[...], ov_ref[...] = _adamw(wp_ref[...], tot, mp_ref[...], vp_ref[...])

        ga = jnp.zeros((D_MODEL, n_sh), F32)
        for j in range(N_DEV):
            d_mine = jnp.zeros((8, n_sh), F32)
            for k in range(N_DEV):
                d_mine = d_mine + jnp.where(me == k, gall_ref[j, :, PK_ADA + n_sh * k:PK_ADA + n_sh * (k + 1)], 0.0)
            col = _silu(call_ref[j])
            ga = ga + jnp.concatenate(
                [col * d_mine[0:1, LANES * a:LANES * (a + 1)] for a in range(n_sh // LANES)], axis=1)
        ag_ref[...] = ga
        ad_ref[...], am_ref[...], av_ref[...] = _adamw(wa_ref[...], ga, ma_ref[...], va_ref[...])

    pk = jax.ShapeDtypeStruct((8, PK_END), F32)
    ada = jax.ShapeDtypeStruct((D_MODEL, n_sh), F32)
    return pl.pallas_call(
        body, name="vec_exchange",
        out_shape=[pk] * 4 + [ada] * 4,
        in_specs=[_vmem_spec()] * 8, out_specs=[_vmem_spec()] * 8,
        scratch_shapes=[
            pltpu.VMEM((N_DEV, 8, PK_END), F32),
            pltpu.VMEM((N_DEV, D_MODEL, LANES), F32),
            pltpu.SemaphoreType.DMA((2 * (N_DEV - 1),)),
            pltpu.SemaphoreType.DMA((2 * (N_DEV - 1),)),
        ],
        compiler_params=pltpu.CompilerParams(vmem_limit_bytes=VMEM_BIG),
    )(gpack, ccol, wpack, mpack, vpack, w_ada, m_ada, v_ada)


def _grad_exchange(grads):
    n = len(grads)

    def body(*refs):
        ins, outs = refs[:n], refs[n:2 * n]
        ssem, rsem, lsem = refs[2 * n], refs[2 * n + 1], refs[2 * n + 2]
        pos = _mesh_pos()
        me = _lin(pos)
        own = [pltpu.make_async_copy(ins[a].at[me], outs[a].at[me], lsem.at[a]) for a in range(n)]
        for cp in own:
            cp.start()
        for a in range(n):
            _all_to_all(pos, ins[a], outs[a], ssem, rsem, a * (N_DEV - 1))
        for cp in own:
            cp.wait()

    return pl.pallas_call(
        body, name="grad_exchange",
        out_shape=[jax.ShapeDtypeStruct(g.shape, g.dtype) for g in grads],
        in_specs=[_any_spec()] * n, out_specs=[_any_spec()] * n,
        scratch_shapes=[
            pltpu.SemaphoreType.DMA((n * (N_DEV - 1),)),
            pltpu.SemaphoreType.DMA((n * (N_DEV - 1),)),
            pltpu.SemaphoreType.DMA((n,)),
        ],
    )(*grads)


def _adamw_reduce(name, parts, w, m, v, row_tile):
    rows, cols = w.shape

    def body(p_ref, w_ref, m_ref, v_ref, g_ref, d_ref, mo_ref, vo_ref):
        g = p_ref[0].astype(F32)
        for j in range(1, N_DEV):
            g = g + p_ref[j].astype(F32)
        g_ref[...] = g
        d_ref[...], mo_ref[...], vo_ref[...] = _adamw(w_ref[...], g, m_ref[...], v_ref[...])

    tile = pl.BlockSpec((row_tile, cols), lambda i: (i, 0))
    return pl.pallas_call(
        body, name=name, grid=(rows // row_tile,),
        out_shape=[jax.ShapeDtypeStruct((rows, cols), F32)] * 4,
        in_specs=[pl.BlockSpec((N_DEV, row_tile, cols), lambda i: (0, i, 0)), tile, tile, tile],
        out_specs=[tile] * 4,
        compiler_params=pltpu.CompilerParams(dimension_semantics=("parallel",), vmem_limit_bytes=VMEM_BIG),
    )(parts, w, m, v)


def _rope_tables(positions):
    inv_freq = 10000.0 ** (-jnp.arange(0, ROPE, 2, dtype=F32) / ROPE)
    ang = positions.astype(F32)[:, None] * inv_freq
    cos, sin = jnp.cos(ang), jnp.sin(ang)
    s_len = positions.shape[0]
    ones = jnp.ones((s_len, NOPE), F32)
    zeros = jnp.zeros((s_len, NOPE), F32)
    z16 = jnp.zeros((s_len, ROPE // 2), F32)
    pad1 = jnp.ones((s_len, LANES - MLA_QK), F32)
    pad0 = jnp.zeros((s_len, LANES - MLA_QK), F32)
    cosf = jnp.concatenate([ones, cos, cos, pad1], axis=1)
    sin_a = jnp.concatenate([zeros, -sin, z16, pad0], axis=1)
    sin_b = jnp.concatenate([zeros, z16, sin, pad0], axis=1)
    return cosf, sin_a, sin_b


def _rearrange_cols(w):
    pad = jnp.zeros((w.shape[0], IN_COLS_R - IN_COLS), w.dtype)
    return jnp.concatenate([w[:, :2688], w[:, 2720:3232], w[:, 2688:2720], pad], axis=1)


def _restore_cols(g):
    return jnp.concatenate([g[:, :2688], g[:, 3200:3232], g[:, 2688:3200]], axis=1)


def _pad_heads(w):
    rows = w.shape[0]
    w = w.reshape(rows, HEADS, MLA_QK)
    return jnp.pad(w, ((0, 0), (0, 0), (0, LANES - MLA_QK))).reshape(rows, HEADS * LANES)


def _unpad_heads(g):
    rows = g.shape[0]
    return g.reshape(rows, HEADS, LANES)[:, :, :MLA_QK].reshape(rows, HEADS * MLA_QK)


def _pad_lanes(v):
    return jnp.pad(v, ((0, 0), (0, LANES - v.shape[1])))


def _col_shards(g):
    rows = g.shape[0]
    return g.reshape(rows, N_DEV, g.shape[1] // N_DEV).transpose(1, 0, 2)


def _from_col_shards(g):
    return g.transpose(1, 0, 2).reshape(g.shape[1], N_DEV * g.shape[2])


def _pack(norm_w, qln, kvln, qhn, khn, ada, loss_lanes=None):
    if loss_lanes is None:
        loss_lanes = jnp.zeros((1, PK_END - PK_LOSS), F32)
    row = jnp.concatenate([norm_w, qln, kvln, _pad_lanes(qhn), _pad_lanes(khn), ada, loss_lanes], axis=1)
    return jnp.broadcast_to(row, (8, PK_END))


def _unpack(p):
    row = p[0:1]
    return (row[:, PK_NORM:PK_QLN], row[:, PK_QLN:PK_KVLN], row[:, PK_KVLN:PK_QHN],
            row[:, PK_QHN:PK_QHN + MLA_QK], row[:, PK_KHN:PK_KHN + MLA_QK], row[:, PK_ADA:PK_LOSS])


def kernel(x, c, positions, w_ada, b_ada, norm_w, w_in, q_lora_norm, w_uq, kv_lora_norm, w_ukv, q_head_norm, k_head_norm, w_out, loss_target, m_w_ada, m_b_ada, m_norm_w, m_w_in, m_q_lora_norm, m_w_uq, m_kv_lora_norm, m_w_ukv, m_q_head_norm, m_k_head_norm, m_w_out, v_w_ada, v_b_ada, v_norm_w, v_w_in, v_q_lora_norm, v_w_uq, v_kv_lora_norm, v_w_ukv, v_q_head_norm, v_k_head_norm, v_w_out):
    s_len = x.shape[1]
    x2 = x.reshape(s_len, D_MODEL)
    tgt = loss_target.reshape(s_len, D_MODEL)
    w_ada_s, w_in_s, w_uq_s, w_ukv_s, w_out_s = w_ada[0], w_in[0], w_uq[0], w_ukv[0], w_out[0]

    ada8 = _ada_fwd(jnp.broadcast_to(c, (8, D_MODEL)), w_ada_s, b_ada.reshape(N_DEV, -1))
    ada = ada8.reshape(1, 3 * D_MODEL)
    shift, scale, gate = ada[:, :D_MODEL], ada[:, D_MODEL:2 * D_MODEL], ada[:, 2 * D_MODEL:]

    g_in, g_uq, g_ukv, g_out = _gather_weights([w_in_s, w_uq_s, w_ukv_s, w_out_s])
    w_in_r = _rearrange_cols(_from_col_shards(g_in))
    wuq_p = _pad_heads(_from_col_shards(g_uq))
    wukv_f = _from_col_shards(g_ukv)
    w_out_f = g_out.reshape(D_MODEL, D_MODEL)

    cosf, sin_a, sin_b = _rope_tables(positions[0])
    qhn_p, khn_p = _pad_lanes(q_head_norm), _pad_lanes(k_head_norm)

    hb, qkv, g_sb, cq, ckv, g_mla, kr, qm, km, vm = _fwd_pre(
        x2, shift, scale, norm_w, w_in_r, q_lora_norm, wuq_p, kv_lora_norm, wukv_f, qhn_p, khn_p,
        cosf, sin_a, sin_b)
    o_sb, w_saved, l_saved = _sb_fwd(qkv)
    o_mla, lse = _mla_fwd(qm, km, vm)

    dy, do_sb, dg_sb, do_mla, dg_mla, gw_out, d_gate, loss_acc = _mid(
        o_sb, g_sb, o_mla, g_mla, x2, tgt, gate, w_out_f)

    dq_sb, dk_sb, dv_sb = _sb_bwd(qkv, do_sb, w_saved, l_saved)
    dq_m, dk_m, dv_m = _mla_bwd(qm, km, vm, o_mla, do_mla, lse)
    dcq, dckv, dkr, gw_uq_p, gw_ukv, g_qhn, g_khn, g_qln, g_kvln = _mla_pre_bwd(
        dq_m, dk_m, dv_m, cq, ckv, kr, q_lora_norm, wuq_p, kv_lora_norm, wukv_f, qhn_p, khn_p,
        cosf, sin_a, sin_b)
    dproj, grad_x, d_shift, d_scale, g_norm_w = _dproj_bwd(
        dq_sb, dk_sb, dv_sb, dg_sb, dcq, dckv, dg_mla, dkr, w_in_r, x2, dy, norm_w, scale)
    gw_in = _restore_cols(_grad_w_in(hb, dproj))

    d_ada = jnp.concatenate([d_shift, d_scale, d_gate], axis=1)
    gpack = _pack(g_norm_w, g_qln, g_kvln, g_qhn[:, :MLA_QK], g_khn[:, :MLA_QK], d_ada, loss_acc)
    wpack = _pack(norm_w, q_lora_norm, kv_lora_norm, q_head_norm, k_head_norm, b_ada)
    mpack = _pack(m_norm_w, m_q_lora_norm, m_kv_lora_norm, m_q_head_norm, m_k_head_norm, m_b_ada)
    vpack = _pack(v_norm_w, v_q_lora_norm, v_kv_lora_norm, v_q_head_norm, v_k_head_norm, v_b_ada)
    ccol = jnp.broadcast_to(c.reshape(D_MODEL, 1), (D_MODEL, LANES))
    pg, pd, pm, pv, ada_g, ada_d, ada_m, ada_v = _vec_exchange(
        gpack, ccol, wpack, mpack, vpack, w_ada_s, m_w_ada[0], v_w_ada[0])
    loss = 0.5 * jnp.sum(pg[0, PK_LOSS:PK_END]) / D_MODEL

    r_in, r_uq, r_ukv, r_out = _grad_exchange([g.astype(BF16) for g in (
        _col_shards(gw_in), _col_shards(_unpad_heads(gw_uq_p)), _col_shards(gw_ukv),
        gw_out.reshape(N_DEV, D_MODEL // N_DEV, D_MODEL))])
    in_g, in_d, in_m, in_v = _adamw_reduce("adamw_w_in", r_in, w_in_s, m_w_in[0], v_w_in[0], 256)
    uq_g, uq_d, uq_m, uq_v = _adamw_reduce("adamw_w_uq", r_uq, w_uq_s, m_w_uq[0], v_w_uq[0], w_uq_s.shape[0])
    ukv_g, ukv_d, ukv_m, ukv_v = _adamw_reduce(
        "adamw_w_ukv", r_ukv, w_ukv_s, m_w_ukv[0], v_w_ukv[0], w_ukv_s.shape[0])
    out_g, out_d, out_m, out_v = _adamw_reduce(
        "adamw_w_out", r_out, w_out_s, m_w_out[0], v_w_out[0], w_out_s.shape[0])

    def group(ada_t, pk, in_t, uq_t, ukv_t, out_t):
        nw, qln, kvln, qhn, khn, b = _unpack(pk)
        return (ada_t[None], b, nw, in_t[None], qln, uq_t[None], kvln, ukv_t[None], qhn, khn, out_t[None])

    return (loss, grad_x.reshape(1, s_len, D_MODEL),
            *group(ada_g, pg, in_g, uq_g, ukv_g, out_g),
            *group(ada_d, pd, in_d, uq_d, ukv_d, out_d),
            *group(ada_m, pm, in_m, uq_m, ukv_m, out_m),
            *group(ada_v, pv, in_v, uq_v, ukv_v, out_v))
```

```python
import functools
import math

import jax
import jax.numpy as jnp
from jax import lax
from jax.experimental import pallas as pl
from jax.experimental.pallas import tpu as pltpu

F32 = jnp.float32
BF16 = jnp.bfloat16

N_DEV = 8
D_MODEL = 1024
HEADS = 8
SB_WIDTH = 512
MLA_WIDTH = 512
Q_LORA = 384
KV_LORA = 256
ROPE = 32
NOPE = 64
MLA_QK = 96
LANES = 128
IN_COLS = 3232
IN_COLS_R = 3328
EPS = 1e-6
NEG = -1e30

ADAM_LR = 0.001
ADAM_B1 = 0.9
ADAM_B2 = 0.999
ADAM_EPS = 1e-08
ADAM_WD = 0.01
ADAM_STEP = 10

TS = 256
TQ = 512
KEY_UNROLL = 2
TN_S = 512
VMEM_BIG = 56 * 1024 * 1024

PK_NORM, PK_QLN, PK_KVLN, PK_QHN, PK_KHN, PK_ADA, PK_LOSS, PK_END = 0, 1024, 1408, 1664, 1792, 1920, 4992, 6016

MESH_ID = pl.DeviceIdType.MESH


def _dot_nn(a, b):
    return lax.dot_general(a, b, (((1,), (0,)), ((), ())), preferred_element_type=F32)


def _dot_nt(a, b):
    return lax.dot_general(a, b, (((1,), (1,)), ((), ())), preferred_element_type=F32)


def _dot_tn(a, b):
    return lax.dot_general(a, b, (((0,), (0,)), ((), ())), preferred_element_type=F32)


def _split_bf16(a):
    hi = a.astype(BF16)
    lo = (a - hi.astype(F32)).astype(BF16)
    return hi, lo


def _dot3(a, b):
    ah, al = _split_bf16(a)
    bh, bl = _split_bf16(b)
    return _dot_nn(ah, bh) + _dot_nn(ah, bl) + _dot_nn(al, bh)


def _sigmoid(g):
    return 1.0 / (1.0 + jnp.exp(-g))


def _silu(g):
    return g * _sigmoid(g)


def _lane_iota(shape):
    return lax.broadcasted_iota(jnp.int32, shape, len(shape) - 1)


def _adamw(w, g, m, v):
    m = ADAM_B1 * m + (1.0 - ADAM_B1) * g
    v = ADAM_B2 * v + (1.0 - ADAM_B2) * (g * g)
    m_hat = m / (1.0 - ADAM_B1 ** ADAM_STEP)
    v_hat = v / (1.0 - ADAM_B2 ** ADAM_STEP)
    delta = -ADAM_LR * (m_hat / (jnp.sqrt(v_hat) + ADAM_EPS) + ADAM_WD * w)
    return delta, m, v


def _mesh_pos():
    return lax.axis_index("x"), lax.axis_index("y"), lax.axis_index("c")


def _peer(pos, k):
    x, y, c = pos
    return (1 - x if k & 4 else x, 1 - y if k & 2 else y, 1 - c if k & 1 else c)


def _lin(pos):
    return 4 * pos[0] + 2 * pos[1] + pos[2]


def _remote(src, dst, send_sems, recv_sems, idx, peer):
    return pltpu.make_async_remote_copy(
        src_ref=src, dst_ref=dst, send_sem=send_sems.at[idx], recv_sem=recv_sems.at[idx],
        device_id=peer, device_id_type=MESH_ID)


def _all_gather(pos, src, buf, send_sems, recv_sems, base):
    me = _lin(pos)
    sent = []
    for k in range(1, N_DEV):
        cp = _remote(src, buf.at[me], send_sems, recv_sems, base + k - 1, _peer(pos, k))
        cp.start()
        sent.append(cp)
    for k in range(1, N_DEV):
        peer = _peer(pos, k)
        _remote(src, buf.at[_lin(peer)], send_sems, recv_sems, base + k - 1, peer).wait_recv()
    for cp in sent:
        cp.wait_send()


def _all_to_all(pos, src, buf, send_sems, recv_sems, base):
    me = _lin(pos)
    sent = []
    for k in range(1, N_DEV):
        peer = _peer(pos, k)
        cp = _remote(src.at[_lin(peer)], buf.at[me], send_sems, recv_sems, base + k - 1, peer)
        cp.start()
        sent.append(cp)
    for k in range(1, N_DEV):
        peer = _peer(pos, k)
        _remote(src.at[me], buf.at[_lin(peer)], send_sems, recv_sems, base + k - 1, peer).wait_recv()
    for cp in sent:
        cp.wait_send()


def _vmem_spec():
    return pl.BlockSpec(memory_space=pltpu.VMEM)


def _any_spec():
    return pl.BlockSpec(memory_space=pl.ANY)


def _row_select(slots, n):
    r = lax.broadcasted_iota(jnp.int32, (N_DEV, n), 0)
    out = jnp.zeros((N_DEV, n), F32)
    for j in range(N_DEV):
        out = out + jnp.where(r == j, slots[j], 0.0)
    return out


def _ada_fwd(c8, w_ada, b_ada8):
    n_sh = w_ada.shape[1]

    def body(c_ref, w_ref, b_ref, out_ref, call_ref, psend_ref, precv_ref, ssem, rsem):
        pos = _mesh_pos()
        me = _lin(pos)
        call_ref[me] = c_ref[...]
        _all_gather(pos, c_ref, call_ref, ssem, rsem, 0)
        w = w_ref[...]
        for j in range(N_DEV):
            psend_ref[j] = _dot3(_silu(call_ref[j]), w)
        precv_ref[me] = psend_ref[me]
        _all_to_all(pos, psend_ref, precv_ref, ssem, rsem, N_DEV - 1)
        out_ref[...] = _row_select([precv_ref[j] for j in range(N_DEV)], n_sh) + b_ref[...]

    return pl.pallas_call(
        body, name="ada_fwd",
        out_shape=jax.ShapeDtypeStruct((N_DEV, n_sh), F32),
        in_specs=[_vmem_spec()] * 3, out_specs=_vmem_spec(),
        scratch_shapes=[
            pltpu.VMEM((N_DEV, 8, D_MODEL), F32),
            pltpu.VMEM((N_DEV, 8, n_sh), F32),
            pltpu.VMEM((N_DEV, 8, n_sh), F32),
            pltpu.SemaphoreType.DMA((2 * (N_DEV - 1),)),
            pltpu.SemaphoreType.DMA((2 * (N_DEV - 1),)),
        ],
    )(c8, w_ada, b_ada8)


def _gather_weights(shards):
    n = len(shards)

    def body(*refs):
        ins, outs = refs[:n], refs[n:2 * n]
        ssem, rsem = refs[2 * n], refs[2 * n + 1]
        pos = _mesh_pos()
        me = _lin(pos)
        for a in range(n):
            outs[a][me] = ins[a][...].astype(BF16)
        for a in range(n):
            _all_gather(pos, outs[a].at[me], outs[a], ssem, rsem, a * (N_DEV - 1))

    return pl.pallas_call(
        body, name="gather_weights",
        out_shape=[jax.ShapeDtypeStruct((N_DEV,) + s.shape, BF16) for s in shards],
        in_specs=[_vmem_spec()] * n, out_specs=[_vmem_spec()] * n,
        scratch_shapes=[
            pltpu.SemaphoreType.DMA((n * (N_DEV - 1),)),
            pltpu.SemaphoreType.DMA((n * (N_DEV - 1),)),
        ],
        compiler_params=pltpu.CompilerParams(vmem_limit_bytes=VMEM_BIG),
    )(*shards)


def _rope(t, cosf, sin_a, sin_b):
    return t * cosf + pltpu.roll(t, 112, 1) * sin_a + pltpu.roll(t, 16, 1) * sin_b


def _rope_t(d, cosf, sin_a, sin_b):
    return d * cosf + pltpu.roll(d * sin_a, 16, 1) + pltpu.roll(d * sin_b, 112, 1)


def _head_rms(t):
    return lax.rsqrt(jnp.sum(t * t, axis=1, keepdims=True) * (1.0 / MLA_QK) + EPS)


def _rms_bwd(dxhat_w, xhat, r, n):
    return r * (dxhat_w - xhat * (jnp.sum(dxhat_w * xhat, axis=1, keepdims=True) * (1.0 / n)))


def _mla_latents(cq, ckv, kr, qln, kvln, wuq, wukv):
    rq = lax.rsqrt(jnp.mean(cq * cq, axis=1, keepdims=True) + EPS)
    rkv = lax.rsqrt(jnp.mean(ckv * ckv, axis=1, keepdims=True) + EPS)
    cq_hat = cq * rq
    ckv_hat = ckv * rkv
    cqn = (cq_hat * qln).astype(BF16)
    ckvn = (ckv_hat * kvln).astype(BF16)
    q_all = _dot_nn(cqn, wuq)
    kv = _dot_nn(ckvn, wukv)
    kr64 = pltpu.roll(kr, 64, 1)
    return rq, rkv, cq_hat, ckv_hat, cqn, ckvn, q_all, kv, kr64


def _fwd_pre(x, shift, scale, norm_w, w_in_r, qln, wuq, kvln, wukv, qhn, khn, cosf, sin_a, sin_b):
    s_len = x.shape[0]

    def body(x_ref, shift_ref, scale_ref, nw_ref, w_ref, qln_ref, wuq_ref, kvln_ref, wukv_ref,
             qhn_ref, khn_ref, cos_ref, sa_ref, sb_ref,
             hb_ref, qkv_ref, gsb_ref, cq_ref, ckv_ref, gmla_ref, kr_ref, qm_ref, km_ref, vm_ref):
        xv = x_ref[...]
        r = lax.rsqrt(jnp.mean(xv * xv, axis=1, keepdims=True) + EPS)
        h = (xv * r) * nw_ref[...] * (1.0 + scale_ref[...]) + shift_ref[...]
        hb = h.astype(BF16)
        hb_ref[...] = hb
        qkv_ref[...] = _dot_nn(hb, w_ref[:, 0:1536]).astype(BF16)
        gsb_ref[...] = _dot_nn(hb, w_ref[:, 1536:2048])
        cq = _dot_nn(hb, w_ref[:, 2048:2432])
        ckv = _dot_nn(hb, w_ref[:, 2432:2688])
        gmla_ref[...] = _dot_nn(hb, w_ref[:, 2688:3200])
        kr = _dot_nn(hb, w_ref[:, 3200:3328])
        cq_ref[...] = cq
        ckv_ref[...] = ckv
        kr_ref[...] = kr
        _, _, _, _, _, _, q_all, kv, kr64 = _mla_latents(
            cq, ckv, kr, qln_ref[...], kvln_ref[...], wuq_ref[...], wukv_ref[...])
        cosf, sa, sb = cos_ref[...], sa_ref[...], sb_ref[...]
        qhn_v, khn_v = qhn_ref[...], khn_ref[...]
        low = _lane_iota((TS, LANES)) < NOPE
        for hd in range(HEADS):
            blk = slice(LANES * hd, LANES * (hd + 1))
            qb = q_all[:, blk]
            qm_ref[:, blk] = _rope(qb * _head_rms(qb) * qhn_v, cosf, sa, sb).astype(BF16)
            kb = jnp.where(low, kv[:, blk], kr64)
            km_ref[:, blk] = _rope(kb * _head_rms(kb) * khn_v, cosf, sa, sb).astype(BF16)
        for p in range(HEADS // 2):
            even = kv[:, LANES * 2 * p:LANES * (2 * p + 1)]
            odd = kv[:, LANES * (2 * p + 1):LANES * (2 * p + 2)]
            vm_ref[:, LANES * p:LANES * (p + 1)] = jnp.where(low, pltpu.roll(even, 64, 1), odd).astype(BF16)

    def tile(width):
        return pl.BlockSpec((TS, width), lambda i: (i, 0))

    def full(a):
        return pl.BlockSpec(a.shape, lambda i: (0, 0))

    out_widths = [(D_MODEL, BF16), (1536, BF16), (512, F32), (Q_LORA, F32), (KV_LORA, F32),
                  (512, F32), (LANES, F32), (1024, BF16), (1024, BF16), (512, BF16)]
    return pl.pallas_call(
        body, name="fwd_pre", grid=(s_len // TS,),
        out_shape=[jax.ShapeDtypeStruct((s_len, w), dt) for w, dt in out_widths],
        in_specs=[tile(D_MODEL), full(shift), full(scale), full(norm_w), full(w_in_r), full(qln), full(wuq),
                  full(kvln), full(wukv), full(qhn), full(khn), tile(LANES), tile(LANES), tile(LANES)],
        out_specs=[tile(w) for w, _ in out_widths],
        compiler_params=pltpu.CompilerParams(dimension_semantics=("parallel",), vmem_limit_bytes=VMEM_BIG),
    )(x, shift, scale, norm_w, w_in_r, qln, wuq, kvln, wukv, qhn, khn, cosf, sin_a, sin_b)


CUM_W = 256


def _tri(strict):
    j = lax.broadcasted_iota(jnp.int32, (CUM_W, CUM_W), 0)
    s = lax.broadcasted_iota(jnp.int32, (CUM_W, CUM_W), 1)
    return (j > s if strict else j >= s).astype(BF16)


def _suffix_sums(a, a_bf16, tri_m, carry):
    n = a.shape[1] // CUM_W
    outs = [None] * n
    for i in reversed(range(n)):
        cols = slice(CUM_W * i, CUM_W * (i + 1))
        outs[i] = _dot_nn(a_bf16[:, cols], tri_m) + carry
        carry = carry + _rowsum(a[:, cols])
    return (outs[0] if n == 1 else jnp.concatenate(outs, axis=1)), carry


def _sb_weights(qm, kb, carry, tri_u, diag):
    z = _dot_nt(qm, kb)
    nz = -z
    lk = jnp.minimum(nz, 0.0) - jnp.log(1.0 + jnp.exp(jnp.minimum(z, nz)))
    if diag:
        t = lax.broadcasted_iota(jnp.int32, (TQ, TQ), 0)
        s = lax.broadcasted_iota(jnp.int32, (TQ, TQ), 1)
        valid = s < t
        lk = jnp.where(valid, lk, 0.0)
    lk_hi = lk.astype(BF16)
    after, carry = _suffix_sums(lk, lk_hi, tri_u, carry)
    logw = z + lk + after
    if diag:
        logw = jnp.where(valid, logw, NEG)
    return lk_hi, jnp.exp(logw), carry


SB_SCALE = 0.125


def _head_masks():
    lane = _lane_iota((1, LANES))
    return [lane < 64, lane >= 64]


def _masked(hm, a):
    return jnp.where(hm, a, jnp.zeros_like(a))


def _rowsum(a):
    return jnp.sum(a, axis=1, keepdims=True)


def _key_rows(kj):
    return pl.ds(pl.multiple_of(kj * TQ, TQ), TQ)


def _over_key_tiles(count, fn, st, ascending):
    n_full = count // KEY_UNROLL
    n_rest = count - n_full * KEY_UNROLL

    def group(g, s_):
        return fn([g * KEY_UNROLL + (u if ascending else KEY_UNROLL - 1 - u) for u in range(KEY_UNROLL)], s_)

    if ascending:
        st = lax.fori_loop(0, n_full, group, st)
        return lax.fori_loop(0, n_rest, lambda i, s_: fn([n_full * KEY_UNROLL + i], s_), st)
    st = lax.fori_loop(0, n_rest, lambda i, s_: fn([count - 1 - i], s_), st)
    return lax.fori_loop(0, n_full, lambda i, s_: group(n_full - 1 - i, s_), st)


def _each_tile(block):
    def trip(tiles, st):
        for kj in tiles:
            st = block(_key_rows(kj), st, False)
        return st
    return trip


STAGE_SLOTS = 2 * KEY_UNROLL


def _stage_copies(to_hbm, hbm_refs, scr_refs, sems, pair, qi, kj):
    slot = (qi - kj) % STAGE_SLOTS
    out = []
    for h in range(2):
        for a in range(2):
            hbm, scr = hbm_refs[a].at[2 * pair + h, qi, kj], scr_refs[a].at[slot, h]
            sem = sems.at[4 * slot + 2 * h + a]
            out.append(pltpu.make_async_copy(scr, hbm, sem) if to_hbm else pltpu.make_async_copy(hbm, scr, sem))
    return out


def _sb_fwd(qkv):
    s_len = qkv.shape[0]
    nq = s_len // TQ

    def body(q_ref, k_ref, v_ref, o_ref, w_hbm, l_hbm, w_scr, l_scr, sems):
        pair, qi = pl.program_id(0), pl.program_id(1)
        q = q_ref[...]
        tri_u = _tri(True)
        masks = _head_masks()
        qms = [_masked(hm, q) * SB_SCALE for hm in masks]

        def copies(kj, of_qi=qi):
            return _stage_copies(True, (w_hbm, l_hbm), (w_scr, l_scr), sems, pair, of_qi, kj)

        def drain(of_qi):
            for kj in range(STAGE_SLOTS):
                @pl.when(kj <= of_qi)
                def _():
                    for cp in copies(kj, of_qi):
                        cp.wait()

        def block(kj, st, diag, before_staging=None):
            rows = _key_rows(kj)
            slot = (qi - kj) % STAGE_SLOTS
            kb, vb = k_ref[rows, :], v_ref[rows, :]
            carries, acc = list(st[:2]), st[2]
            staged = []
            for h in range(2):
                lk_hi, w, carries[h] = _sb_weights(qms[h], kb, carries[h], tri_u, diag)
                wb = w.astype(BF16)
                staged.append((wb, lk_hi))
                acc = acc + _dot_nn(wb, _masked(masks[h], vb))
            if before_staging is not None:
                before_staging()
            for h in range(2):
                w_scr[slot, h], l_scr[slot, h] = staged[h]
            return carries[0], carries[1], acc

        def trip(tiles, st):
            for kj in tiles:
                @pl.when(qi - kj >= STAGE_SLOTS)
                def _():
                    for cp in copies(kj + STAGE_SLOTS):
                        cp.wait()
            for kj in tiles:
                st = block(kj, st, False)
            for kj in tiles:
                for cp in copies(kj):
                    cp.start()
            return st

        def drain_previous_step():
            @pl.when(jnp.logical_or(pair > 0, qi > 0))
            def _():
                drain(jnp.where(qi == 0, nq - 1, qi - 1))

        zc = jnp.zeros((TQ, 1), F32)
        st = block(qi, (zc, zc, jnp.zeros((TQ, LANES), F32)), True, drain_previous_step)
        for cp in copies(qi):
            cp.start()
        st = _over_key_tiles(qi, trip, st, ascending=False)
        o_ref[...] = st[2]

        @pl.when(jnp.logical_and(pair == HEADS // 2 - 1, qi == nq - 1))
        def _():
            drain(qi)

    saved = jax.ShapeDtypeStruct((HEADS, nq, nq, TQ, TQ), BF16)
    stage = pltpu.VMEM((STAGE_SLOTS, 2, TQ, TQ), BF16)
    return pl.pallas_call(
        body, name="sb_fwd", grid=(HEADS // 2, nq),
        out_shape=[jax.ShapeDtypeStruct((s_len, SB_WIDTH), F32), saved, saved],
        in_specs=[pl.BlockSpec((TQ, LANES), lambda p, i: (i, p)),
                  pl.BlockSpec((s_len, LANES), lambda p, i: (0, 4 + p)),
                  pl.BlockSpec((s_len, LANES), lambda p, i: (0, 8 + p))],
        out_specs=[pl.BlockSpec((TQ, LANES), lambda p, i: (i, p)), _any_spec(), _any_spec()],
        scratch_shapes=[stage, stage, pltpu.SemaphoreType.DMA((4 * STAGE_SLOTS,))],
        compiler_params=pltpu.CompilerParams(dimension_semantics=("arbitrary", "arbitrary"),
                                             vmem_limit_bytes=VMEM_BIG),
    )(qkv, qkv, qkv)


def _prefix_sums(a, tri_m, carry):
    n = a.shape[1] // CUM_W
    outs = [None] * n
    for i in range(n):
        cols = slice(CUM_W * i, CUM_W * (i + 1))
        outs[i] = _dot_nn(a[:, cols].astype(BF16), tri_m) + carry
        carry = carry + _rowsum(a[:, cols])
    return (outs[0] if n == 1 else jnp.concatenate(outs, axis=1)), carry


def _sb_bwd(qkv, do, w_saved, l_saved):
    s_len = qkv.shape[0]

    def body(q_ref, k_ref, v_ref, do_ref, w_hbm, l_hbm, dq_ref, dk_ref, dv_ref, w_scr, l_scr, sems):
        pair, qi = pl.program_id(0), pl.program_id(1)

        nq = s_len // TQ

        def copies(kj, of_pair=pair, of_qi=qi):
            return _stage_copies(False, (w_hbm, l_hbm), (w_scr, l_scr), sems, of_pair, of_qi, kj)

        def start_first_tiles(of_pair, of_qi):
            for cp in copies(0, of_pair, of_qi):
                cp.start()

            @pl.when(of_qi >= 1)
            def _():
                for cp in copies(1, of_pair, of_qi):
                    cp.start()

        @pl.when(jnp.logical_and(pair == 0, qi == 0))
        def _():
            start_first_tiles(pair, qi)

        @pl.when(qi == 0)
        def _():
            dk_ref[...] = jnp.zeros_like(dk_ref)
            dv_ref[...] = jnp.zeros_like(dv_ref)

        q = q_ref[...]
        do_v = do_ref[...]
        j = lax.broadcasted_iota(jnp.int32, (CUM_W, CUM_W), 0)
        s = lax.broadcasted_iota(jnp.int32, (CUM_W, CUM_W), 1)
        tri_before = (j < s).astype(BF16)
        masks = _head_masks()
        qms = [_masked(hm, q) * SB_SCALE for hm in masks]
        doms = [_masked(hm, do_v) for hm in masks]

        def block(kj, st):
            rows = _key_rows(kj)
            slot = (qi - kj) % STAGE_SLOTS
            kb, vb = k_ref[rows, :], v_ref[rows, :]
            carries, dqs = list(st[0:2]), list(st[2:4])
            dk_blk = jnp.zeros((TQ, LANES), F32)
            dv_blk = jnp.zeros((TQ, LANES), F32)
            for h in range(2):
                wb = w_scr[slot, h]
                d_l = _dot_nt(doms[h], vb) * wb.astype(F32)
                before, carries[h] = _prefix_sums(d_l, tri_before, carries[h])
                keep = jnp.exp(l_scr[slot, h].astype(F32))
                dzb = (d_l * keep - before * (1.0 - keep)).astype(BF16)
                dk_blk = dk_blk + _dot_tn(dzb, qms[h])
                dv_blk = dv_blk + _dot_tn(wb, doms[h])
                dqs[h] = dqs[h] + _dot_nn(dzb, kb)
            dk_ref[rows, :] += dk_blk
            dv_ref[rows, :] += dv_blk
            return (*carries, *dqs)

        def trip(tiles, st):
            for kj in tiles:
                @pl.when(kj + 2 <= qi)
                def _():
                    for cp in copies(kj + 2):
                        cp.start()
            for kj in tiles:
                for cp in copies(kj):
                    cp.wait()
            for kj in tiles:
                st = block(kj, st)
            return st

        zc = jnp.zeros((TQ, 1), F32)
        za = jnp.zeros((TQ, LANES), F32)
        st = _over_key_tiles(qi + 1, trip, (zc, zc, za, za), ascending=True)
        dq_ref[...] = jnp.where(masks[0], st[2], st[3]) * SB_SCALE

        @pl.when(jnp.logical_or(pair < HEADS // 2 - 1, qi < nq - 1))
        def _():
            wraps = qi == nq - 1
            start_first_tiles(jnp.where(wraps, pair + 1, pair), jnp.where(wraps, 0, qi + 1))

    tile = pl.BlockSpec((TQ, LANES), lambda p, i: (i, p))
    col = pl.BlockSpec((s_len, LANES), lambda p, i: (0, p))
    stage = pltpu.VMEM((STAGE_SLOTS, 2, TQ, TQ), BF16)
    return pl.pallas_call(
        body, name="sb_bwd", grid=(HEADS // 2, s_len // TQ),
        out_shape=[jax.ShapeDtypeStruct((s_len, SB_WIDTH), F32)] * 3,
        in_specs=[tile,
                  pl.BlockSpec((s_len, LANES), lambda p, i: (0, 4 + p)),
                  pl.BlockSpec((s_len, LANES), lambda p, i: (0, 8 + p)),
                  tile, _any_spec(), _any_spec()],
        out_specs=[tile, col, col],
        scratch_shapes=[stage, stage, pltpu.SemaphoreType.DMA((4 * STAGE_SLOTS,))],
        compiler_params=pltpu.CompilerParams(dimension_semantics=("arbitrary", "arbitrary"),
                                             vmem_limit_bytes=VMEM_BIG),
    )(qkv, qkv, qkv, do, w_saved, l_saved)


MLA_SCALE = 1.0 / math.sqrt(MLA_QK)


def _causal_mask():
    t = lax.broadcasted_iota(jnp.int32, (TQ, TQ), 0)
    s = lax.broadcasted_iota(jnp.int32, (TQ, TQ), 1)
    return s <= t


def _head_lanes(h):
    return slice(LANES * h, LANES * (h + 1))


def _mla_fwd(qm, km, vm):
    s_len = qm.shape[0]

    def body(q_ref, k_ref, v_ref, o_ref, lse_ref):
        qi = pl.program_id(1)
        masks = _head_masks()
        qhs = [q_ref[:, _head_lanes(h)] for h in range(2)]

        def block(rows, st, diag):
            vb = v_ref[rows, :]
            ms, ls, acc = list(st[0:2]), list(st[2:4]), st[4]
            alphas, pvs = [], []
            for h in range(2):
                s = _dot_nt(qhs[h], k_ref[rows, _head_lanes(h)]) * MLA_SCALE
                if diag:
                    s = jnp.where(_causal_mask(), s, NEG)
                m_new = jnp.maximum(ms[h], jnp.max(s, axis=1, keepdims=True))
                p = jnp.exp(s - m_new)
                alphas.append(jnp.exp(ms[h] - m_new))
                ls[h] = alphas[h] * ls[h] + _rowsum(p)
                ms[h] = m_new
                pvs.append(_dot_nn(p.astype(BF16), _masked(masks[h], vb)))
            acc = jnp.where(masks[0], alphas[0], alphas[1]) * acc + pvs[0] + pvs[1]
            return (*ms, *ls, acc)

        neg = jnp.full((TQ, 1), NEG, F32)
        zc = jnp.zeros((TQ, 1), F32)
        st = (neg, neg, zc, zc, jnp.zeros((TQ, LANES), F32))
        st = _over_key_tiles(qi, _each_tile(block), st, ascending=True)
        m0, m1, l0, l1, acc = block(_key_rows(qi), st, True)
        o_ref[...] = acc / jnp.where(masks[0], l0, l1)
        lse_ref[0] = m0 + jnp.log(l0)
        lse_ref[1] = m1 + jnp.log(l1)

    return pl.pallas_call(
        body, name="mla_fwd", grid=(HEADS // 2, s_len // TQ),
        out_shape=[jax.ShapeDtypeStruct((s_len, MLA_WIDTH), F32),
                   jax.ShapeDtypeStruct((HEADS, s_len, 1), F32)],
        in_specs=[pl.BlockSpec((TQ, 2 * LANES), lambda p, i: (i, p)),
                  pl.BlockSpec((s_len, 2 * LANES), lambda p, i: (0, p)),
                  pl.BlockSpec((s_len, LANES), lambda p, i: (0, p))],
        out_specs=[pl.BlockSpec((TQ, LANES), lambda p, i: (i, p)),
                   pl.BlockSpec((2, TQ, 1), lambda p, i: (p, i, 0))],
        compiler_params=pltpu.CompilerParams(dimension_semantics=("parallel", "parallel"),
                                             vmem_limit_bytes=VMEM_BIG),
    )(qm, km, vm)


def _mla_bwd(qm, km, vm, o, do, lse):
    s_len = qm.shape[0]

    def body(q_ref, k_ref, v_ref, o_ref, do_ref, lse_ref, dq_ref, dk_ref, dv_ref):
        qi = pl.program_id(1)

        @pl.when(qi == 0)
        def _():
            dk_ref[...] = jnp.zeros_like(dk_ref)
            dv_ref[...] = jnp.zeros_like(dv_ref)

        do_v = do_ref[...]
        od = o_ref[...] * do_v.astype(F32)
        masks = _head_masks()
        qhs = [q_ref[:, _head_lanes(h)] for h in range(2)]
        doms = [_masked(hm, do_v) for hm in masks]
        deltas = [_rowsum(jnp.where(hm, od, 0.0)) for hm in masks]
        lses = [lse_ref[h] for h in range(2)]

        def block(rows, dqs, diag):
            vb = v_ref[rows, :]
            dqs = list(dqs)
            dv_blk = jnp.zeros((TQ, LANES), F32)
            for h in range(2):
                kb = k_ref[rows, _head_lanes(h)]
                s = _dot_nt(qhs[h], kb) * MLA_SCALE
                if diag:
                    s = jnp.where(_causal_mask(), s, NEG)
                p = jnp.exp(s - lses[h])
                dp = _dot_nt(doms[h], vb)
                ds = (p * (dp - deltas[h]) * MLA_SCALE).astype(BF16)
                dk_ref[rows, _head_lanes(h)] += _dot_tn(ds, qhs[h])
                dv_blk = dv_blk + _dot_tn(p.astype(BF16), doms[h])
                dqs[h] = dqs[h] + _dot_nn(ds, kb)
            dv_ref[rows, :] += dv_blk
            return tuple(dqs)

        za = jnp.zeros((TQ, LANES), F32)
        dqs = _over_key_tiles(qi, _each_tile(block), (za, za), ascending=True)
        dqs = block(_key_rows(qi), dqs, True)
        dq_ref[:, _head_lanes(0)] = dqs[0]
        dq_ref[:, _head_lanes(1)] = dqs[1]

    return pl.pallas_call(
        body, name="mla_bwd", grid=(HEADS // 2, s_len // TQ),
        out_shape=[jax.ShapeDtypeStruct((s_len, HEADS * LANES), F32),
                   jax.ShapeDtypeStruct((s_len, HEADS * LANES), F32),
                   jax.ShapeDtypeStruct((s_len, MLA_WIDTH), F32)],
        in_specs=[pl.BlockSpec((TQ, 2 * LANES), lambda p, i: (i, p)),
                  pl.BlockSpec((s_len, 2 * LANES), lambda p, i: (0, p)),
                  pl.BlockSpec((s_len, LANES), lambda p, i: (0, p)),
                  pl.BlockSpec((TQ, LANES), lambda p, i: (i, p)),
                  pl.BlockSpec((TQ, LANES), lambda p, i: (i, p)),
                  pl.BlockSpec((2, TQ, 1), lambda p, i: (p, i, 0))],
        out_specs=[pl.BlockSpec((TQ, 2 * LANES), lambda p, i: (i, p)),
                   pl.BlockSpec((s_len, 2 * LANES), lambda p, i: (0, p)),
                   pl.BlockSpec((s_len, LANES), lambda p, i: (0, p))],
        compiler_params=pltpu.CompilerParams(dimension_semantics=("parallel", "arbitrary"),
                                             vmem_limit_bytes=VMEM_BIG),
    )(qm, km, vm, o, do, lse)


def _mid(o_sb, g_sb, o_mla, g_mla, x, target, gate, w_out):
    s_len = x.shape[0]

    def body(osb_ref, gsb_ref, omla_ref, gmla_ref, x_ref, t_ref, gate_ref, w_ref,
             dy_ref, dosb_ref, dgsb_ref, domla_ref, dgmla_ref, gw_ref, dgate_ref, loss_ref):
        @pl.when(pl.program_id(0) == 0)
        def _():
            gw_ref[...] = jnp.zeros_like(gw_ref)
            dgate_ref[...] = jnp.zeros_like(dgate_ref)
            loss_ref[...] = jnp.zeros_like(loss_ref)

        o1, g1, o2, g2 = osb_ref[...], gsb_ref[...], omla_ref[...], gmla_ref[...]
        s1, s2 = _sigmoid(g1), _sigmoid(g2)
        mixed = jnp.concatenate([o1 * (g1 * s1), o2 * (g2 * s2)], axis=1).astype(BF16)
        w = w_ref[...]
        gate_v = gate_ref[...]
        u = _dot_nn(mixed, w)
        err = x_ref[...] + gate_v * u - t_ref[...]
        loss_ref[...] += jnp.sum(err * err, axis=0, keepdims=True)
        dy = err * (1.0 / D_MODEL)
        dy_ref[...] = dy
        dgate_ref[...] += jnp.sum(dy * u, axis=0, keepdims=True)
        du = (dy * gate_v).astype(BF16)
        gw_ref[...] += _dot_tn(mixed, du)
        dmixed = _dot_nt(du, w)
        d1, d2 = dmixed[:, :SB_WIDTH], dmixed[:, SB_WIDTH:]
        dosb_ref[...] = (d1 * (g1 * s1)).astype(BF16)
        dgsb_ref[...] = (d1 * o1 * (s1 * (1.0 + g1 * (1.0 - s1)))).astype(BF16)
        domla_ref[...] = (d2 * (g2 * s2)).astype(BF16)
        dgmla_ref[...] = (d2 * o2 * (s2 * (1.0 + g2 * (1.0 - s2)))).astype(BF16)

    def tile(width):
        return pl.BlockSpec((TS, width), lambda i: (i, 0))

    def full(shape):
        return pl.BlockSpec(shape, lambda i: (0, 0))

    return pl.pallas_call(
        body, name="mid", grid=(s_len // TS,),
        out_shape=[jax.ShapeDtypeStruct((s_len, D_MODEL), F32)]
        + [jax.ShapeDtypeStruct((s_len, 512), BF16)] * 4
        + [jax.ShapeDtypeStruct((D_MODEL, D_MODEL), F32),
           jax.ShapeDtypeStruct((1, D_MODEL), F32), jax.ShapeDtypeStruct((1, D_MODEL), F32)],
        in_specs=[tile(512)] * 4 + [tile(D_MODEL), tile(D_MODEL), full((1, D_MODEL)), full((D_MODEL, D_MODEL))],
        out_specs=[tile(D_MODEL)] + [tile(512)] * 4
        + [full((D_MODEL, D_MODEL)), full((1, D_MODEL)), full((1, D_MODEL))],
        compiler_params=pltpu.CompilerParams(dimension_semantics=("arbitrary",), vmem_limit_bytes=VMEM_BIG),
    )(o_sb, g_sb, o_mla, g_mla, x, target, gate, w_out)


def _mla_pre_bwd(dq, dk, dv, cq, ckv, kr, qln, wuq, kvln, wukv, qhn, khn, cosf, sin_a, sin_b):
    s_len = cq.shape[0]

    def body(dq_ref, dk_ref, dv_ref, cq_ref, ckv_ref, kr_ref, qln_ref, wuq_ref, kvln_ref, wukv_ref,
             qhn_ref, khn_ref, cos_ref, sa_ref, sb_ref,
             dcq_ref, dckv_ref, dkr_ref, gwuq_ref, gwukv_ref, gqhn_ref, gkhn_ref, gqln_ref, gkvln_ref,
             dqa_ref, dkv_ref):
        @pl.when(pl.program_id(0) == 0)
        def _():
            for r_ in (gwuq_ref, gwukv_ref, gqhn_ref, gkhn_ref, gqln_ref, gkvln_ref):
                r_[...] = jnp.zeros_like(r_)

        cq, ckv = cq_ref[...], ckv_ref[...]
        qln_v, kvln_v = qln_ref[...], kvln_ref[...]
        wuq_v, wukv_v = wuq_ref[...], wukv_ref[...]
        rq, rkv, cq_hat, ckv_hat, cqn, ckvn, q_all, kv, kr64 = _mla_latents(
            cq, ckv, kr_ref[...], qln_v, kvln_v, wuq_v, wukv_v)
        cosf, sa, sb = cos_ref[...], sa_ref[...], sb_ref[...]
        qhn_v, khn_v = qhn_ref[...], khn_ref[...]
        lane = _lane_iota((TS, LANES))
        low = lane < NOPE
        g_qhn = jnp.zeros((1, LANES), F32)
        g_khn = jnp.zeros((1, LANES), F32)
        dkr64 = jnp.zeros((TS, LANES), F32)
        for hd in range(HEADS):
            blk = slice(LANES * hd, LANES * (hd + 1))
            qb = q_all[:, blk]
            r = _head_rms(qb)
            xh = qb * r
            dn = _rope_t(dq_ref[:, blk], cosf, sa, sb)
            g_qhn = g_qhn + jnp.sum(dn * xh, axis=0, keepdims=True)
            dqa_ref[:, blk] = _rms_bwd(dn * qhn_v, xh, r, MLA_QK).astype(BF16)

            kb = jnp.where(low, kv[:, blk], kr64)
            r = _head_rms(kb)
            xh = kb * r
            dn = _rope_t(dk_ref[:, blk], cosf, sa, sb)
            g_khn = g_khn + jnp.sum(dn * xh, axis=0, keepdims=True)
            dkb = _rms_bwd(dn * khn_v, xh, r, MLA_QK)
            dkr64 = dkr64 + jnp.where(low, 0.0, dkb)
            dvp = dv_ref[:, LANES * (hd // 2):LANES * (hd // 2 + 1)]
            dvh = pltpu.roll(dvp, 64, 1) if hd % 2 == 0 else dvp
            dkv_ref[:, blk] = jnp.where(low, dkb, dvh).astype(BF16)
        gqhn_ref[...] += g_qhn
        gkhn_ref[...] += g_khn
        dkr_ref[...] = pltpu.roll(dkr64, 64, 1).astype(BF16)

        dqa = dqa_ref[...]
        gwuq_ref[...] += _dot_tn(cqn, dqa)
        dcqn = _dot_nt(dqa, wuq_v)
        gqln_ref[...] += jnp.sum(dcqn * cq_hat, axis=0, keepdims=True)
        dcq_ref[...] = _rms_bwd(dcqn * qln_v, cq_hat, rq, Q_LORA).astype(BF16)

        dkv = dkv_ref[...]
        gwukv_ref[...] += _dot_tn(ckvn, dkv)
        dckvn = _dot_nt(dkv, wukv_v)
        gkvln_ref[...] += jnp.sum(dckvn * ckv_hat, axis=0, keepdims=True)
        dckv_ref[...] = _rms_bwd(dckvn * kvln_v, ckv_hat, rkv, KV_LORA).astype(BF16)

    def tile(width):
        return pl.BlockSpec((TS, width), lambda i: (i, 0))

    def full(shape):
        return pl.BlockSpec(shape, lambda i: (0, 0))

    acc_shapes = [(Q_LORA, 1024), (KV_LORA, 1024), (1, LANES), (1, LANES), (1, Q_LORA), (1, KV_LORA)]
    return pl.pallas_call(
        body, name="mla_pre_bwd", grid=(s_len // TS,),
        out_shape=[jax.ShapeDtypeStruct((s_len, Q_LORA), BF16), jax.ShapeDtypeStruct((s_len, KV_LORA), BF16),
                   jax.ShapeDtypeStruct((s_len, LANES), BF16)]
        + [jax.ShapeDtypeStruct(s, F32) for s in acc_shapes],
        in_specs=[tile(1024), tile(1024), tile(512), tile(Q_LORA), tile(KV_LORA), tile(LANES),
                  full(qln.shape), full(wuq.shape), full(kvln.shape), full(wukv.shape),
                  full(qhn.shape), full(khn.shape), tile(LANES), tile(LANES), tile(LANES)],
        out_specs=[tile(Q_LORA), tile(KV_LORA), tile(LANES)] + [full(s) for s in acc_shapes],
        scratch_shapes=[pltpu.VMEM((TS, 1024), BF16), pltpu.VMEM((TS, 1024), BF16)],
        compiler_params=pltpu.CompilerParams(dimension_semantics=("arbitrary",), vmem_limit_bytes=VMEM_BIG),
    )(dq, dk, dv, cq, ckv, kr, qln, wuq, kvln, wukv, qhn, khn, cosf, sin_a, sin_b)


def _dproj_bwd(dq_sb, dk_sb, dv_sb, dg_sb, dcq, dckv, dg_mla, dkr, w_in_r, x, dy, norm_w, scale):
    s_len = x.shape[0]

    def body(dq_ref, dk_ref, dv_ref, dg_ref, dcq_ref, dckv_ref, dgm_ref, dkr_ref, w_ref, x_ref, dy_ref,
             nw_ref, scale_ref, dp_ref, gx_ref, dshift_ref, dscale_ref, dnw_ref):
        @pl.when(pl.program_id(0) == 0)
        def _():
            for r_ in (dshift_ref, dscale_ref, dnw_ref):
                r_[...] = jnp.zeros_like(r_)

        dp_ref[:, 0:512] = dq_ref[...].astype(BF16)
        dp_ref[:, 512:1024] = dk_ref[...].astype(BF16)
        dp_ref[:, 1024:1536] = dv_ref[...].astype(BF16)
        dp_ref[:, 1536:2048] = dg_ref[...]
        dp_ref[:, 2048:2432] = dcq_ref[...]
        dp_ref[:, 2432:2688] = dckv_ref[...]
        dp_ref[:, 2688:3200] = dgm_ref[...]
        dp_ref[:, 3200:3328] = dkr_ref[...]
        dh = _dot_nt(dp_ref[...], w_ref[...])
        xv = x_ref[...]
        r = lax.rsqrt(jnp.mean(xv * xv, axis=1, keepdims=True) + EPS)
        xh = xv * r
        nw = nw_ref[...]
        dshift_ref[...] += jnp.sum(dh, axis=0, keepdims=True)
        dscale_ref[...] += jnp.sum(dh * (xh * nw), axis=0, keepdims=True)
        dxnw = dh * (1.0 + scale_ref[...])
        dnw_ref[...] += jnp.sum(dxnw * xh, axis=0, keepdims=True)
        gx_ref[...] = dy_ref[...] + _rms_bwd(dxnw * nw, xh, r, D_MODEL)

    def tile(width):
        return pl.BlockSpec((TS, width), lambda i: (i, 0))

    def full(shape):
        return pl.BlockSpec(shape, lambda i: (0, 0))

    vec = (1, D_MODEL)
    return pl.pallas_call(
        body, name="dproj_bwd", grid=(s_len // TS,),
        out_shape=[jax.ShapeDtypeStruct((s_len, IN_COLS_R), BF16), jax.ShapeDtypeStruct((s_len, D_MODEL), F32)]
        + [jax.ShapeDtypeStruct(vec, F32)] * 3,
        in_specs=[tile(512)] * 4 + [tile(Q_LORA), tile(KV_LORA), tile(512), tile(LANES),
                                    full(w_in_r.shape), tile(D_MODEL), tile(D_MODEL), full(vec), full(vec)],
        out_specs=[tile(IN_COLS_R), tile(D_MODEL)] + [full(vec)] * 3,
        compiler_params=pltpu.CompilerParams(dimension_semantics=("arbitrary",), vmem_limit_bytes=VMEM_BIG),
    )(dq_sb, dk_sb, dv_sb, dg_sb, dcq, dckv, dg_mla, dkr, w_in_r, x, dy, norm_w, scale)


def _grad_w_in(hb, dproj):
    s_len = hb.shape[0]
    n_half = IN_COLS_R // 2

    def body(h_ref, d_ref, g_ref):
        @pl.when(pl.program_id(1) == 0)
        def _():
            g_ref[...] = jnp.zeros_like(g_ref)

        g_ref[...] += _dot_tn(h_ref[...], d_ref[...])

    return pl.pallas_call(
        body, name="grad_w_in", grid=(2, s_len // TN_S),
        out_shape=jax.ShapeDtypeStruct((D_MODEL, IN_COLS_R), F32),
        in_specs=[pl.BlockSpec((TN_S, D_MODEL), lambda n, s: (s, 0)),
                  pl.BlockSpec((TN_S, n_half), lambda n, s: (s, n))],
        out_specs=pl.BlockSpec((D_MODEL, n_half), lambda n, s: (0, n)),
        compiler_params=pltpu.CompilerParams(dimension_semantics=("parallel", "arbitrary"),
                                             vmem_limit_bytes=VMEM_BIG),
    )(hb, dproj)


def _vec_exchange(gpack, ccol, wpack, mpack, vpack, w_ada, m_ada, v_ada):
    n_sh = w_ada.shape[1]

    def body(g_ref, cc_ref, wp_ref, mp_ref, vp_ref, wa_ref, ma_ref, va_ref,
             og_ref, od_ref, om_ref, ov_ref, ag_ref, ad_ref, am_ref, av_ref,
             gall_ref, call_ref, ssem, rsem):
        pos = _mesh_pos()
        me = _lin(pos)
        gall_ref[me] = g_ref[...]
        call_ref[me] = cc_ref[...]
        _all_gather(pos, g_ref, gall_ref, ssem, rsem, 0)
        _all_gather(pos, cc_ref, call_ref, ssem, rsem, N_DEV - 1)

        tot = gall_ref[0]
        for j in range(1, N_DEV):
            tot = tot + gall_ref[j]
        og_ref[...] = tot
        od_ref[...], om_ref[...], ov_ref[...] = _adamw(wp_ref[...], tot, mp_ref[...], vp_ref[...])

        ga = jnp.zeros((D_MODEL, n_sh), F32)
        for j in range(N_DEV):
            d_mine = jnp.zeros((8, n_sh), F32)
            for k in range(N_DEV):
                d_mine = d_mine + jnp.where(me == k, gall_ref[j, :, PK_ADA + n_sh * k:PK_ADA + n_sh * (k + 1)], 0.0)
            col = _silu(call_ref[j])
            ga = ga + jnp.concatenate(
                [col * d_mine[0:1, LANES * a:LANES * (a + 1)] for a in range(n_sh // LANES)], axis=1)
        ag_ref[...] = ga
        ad_ref[...], am_ref[...], av_ref[...] = _adamw(wa_ref[...], ga, ma_ref[...], va_ref[...])

    pk = jax.ShapeDtypeStruct((8, PK_END), F32)
    ada = jax.ShapeDtypeStruct((D_MODEL, n_sh), F32)
    return pl.pallas_call(
        body, name="vec_exchange",
        out_shape=[pk] * 4 + [ada] * 4,
        in_specs=[_vmem_spec()] * 8, out_specs=[_vmem_spec()] * 8,
        scratch_shapes=[
            pltpu.VMEM((N_DEV, 8, PK_END), F32),
            pltpu.VMEM((N_DEV, D_MODEL, LANES), F32),
            pltpu.SemaphoreType.DMA((2 * (N_DEV - 1),)),
            pltpu.SemaphoreType.DMA((2 * (N_DEV - 1),)),
        ],
        compiler_params=pltpu.CompilerParams(vmem_limit_bytes=VMEM_BIG),
    )(gpack, ccol, wpack, mpack, vpack, w_ada, m_ada, v_ada)


def _grad_exchange(grads):
    n = len(grads)

    def body(*refs):
        ins, outs = refs[:n], refs[n:2 * n]
        ssem, rsem, lsem = refs[2 * n], refs[2 * n + 1], refs[2 * n + 2]
        pos = _mesh_pos()
        me = _lin(pos)
        own = [pltpu.make_async_copy(ins[a].at[me], outs[a].at[me], lsem.at[a]) for a in range(n)]
        for cp in own:
            cp.start()
        for a in range(n):
            _all_to_all(pos, ins[a], outs[a], ssem, rsem, a * (N_DEV - 1))
        for cp in own:
            cp.wait()

    return pl.pallas_call(
        body, name="grad_exchange",
        out_shape=[jax.ShapeDtypeStruct(g.shape, g.dtype) for g in grads],
        in_specs=[_any_spec()] * n, out_specs=[_any_spec()] * n,
        scratch_shapes=[
            pltpu.SemaphoreType.DMA((n * (N_DEV - 1),)),
            pltpu.SemaphoreType.DMA((n * (N_DEV - 1),)),
            pltpu.SemaphoreType.DMA((n,)),
        ],
    )(*grads)


def _adamw_reduce(name, parts, w, m, v, row_tile):
    rows, cols = w.shape

    def body(p_ref, w_ref, m_ref, v_ref, g_ref, d_ref, mo_ref, vo_ref):
        g = p_ref[0].astype(F32)
        for j in range(1, N_DEV):
            g = g + p_ref[j].astype(F32)
        g_ref[...] = g
        d_ref[...], mo_ref[...], vo_ref[...] = _adamw(w_ref[...], g, m_ref[...], v_ref[...])

    tile = pl.BlockSpec((row_tile, cols), lambda i: (i, 0))
    return pl.pallas_call(
        body, name=name, grid=(rows // row_tile,),
        out_shape=[jax.ShapeDtypeStruct((rows, cols), F32)] * 4,
        in_specs=[pl.BlockSpec((N_DEV, row_tile, cols), lambda i: (0, i, 0)), tile, tile, tile],
        out_specs=[tile] * 4,
        compiler_params=pltpu.CompilerParams(dimension_semantics=("parallel",), vmem_limit_bytes=VMEM_BIG),
    )(parts, w, m, v)


def _rope_tables(positions):
    inv_freq = 10000.0 ** (-jnp.arange(0, ROPE, 2, dtype=F32) / ROPE)
    ang = positions.astype(F32)[:, None] * inv_freq
    cos, sin = jnp.cos(ang), jnp.sin(ang)
    s_len = positions.shape[0]
    ones = jnp.ones((s_len, NOPE), F32)
    zeros = jnp.zeros((s_len, NOPE), F32)
    z16 = jnp.zeros((s_len, ROPE // 2), F32)
    pad1 = jnp.ones((s_len, LANES - MLA_QK), F32)
    pad0 = jnp.zeros((s_len, LANES - MLA_QK), F32)
    cosf = jnp.concatenate([ones, cos, cos, pad1], axis=1)
    sin_a = jnp.concatenate([zeros, -sin, z16, pad0], axis=1)
    sin_b = jnp.concatenate([zeros, z16, sin, pad0], axis=1)
    return cosf, sin_a, sin_b


def _rearrange_cols(w):
    pad = jnp.zeros((w.shape[0], IN_COLS_R - IN_COLS), w.dtype)
    return jnp.concatenate([w[:, :2688], w[:, 2720:3232], w[:, 2688:2720], pad], axis=1)


def _restore_cols(g):
    return jnp.concatenate([g[:, :2688], g[:, 3200:3232], g[:, 2688:3200]], axis=1)


def _pad_heads(w):
    rows = w.shape[0]
    w = w.reshape(rows, HEADS, MLA_QK)
    return jnp.pad(w, ((0, 0), (0, 0), (0, LANES - MLA_QK))).reshape(rows, HEADS * LANES)


def _unpad_heads(g):
    rows = g.shape[0]
    return g.reshape(rows, HEADS, LANES)[:, :, :MLA_QK].reshape(rows, HEADS * MLA_QK)


def _pad_lanes(v):
    return jnp.pad(v, ((0, 0), (0, LANES - v.shape[1])))


def _col_shards(g):
    rows = g.shape[0]
    return g.reshape(rows, N_DEV, g.shape[1] // N_DEV).transpose(1, 0, 2)


def _from_col_shards(g):
    return g.transpose(1, 0, 2).reshape(g.shape[1], N_DEV * g.shape[2])


def _pack(norm_w, qln, kvln, qhn, khn, ada, loss_lanes=None):
    if loss_lanes is None:
        loss_lanes = jnp.zeros((1, PK_END - PK_LOSS), F32)
    row = jnp.concatenate([norm_w, qln, kvln, _pad_lanes(qhn), _pad_lanes(khn), ada, loss_lanes], axis=1)
    return jnp.broadcast_to(row, (8, PK_END))


def _unpack(p):
    row = p[0:1]
    return (row[:, PK_NORM:PK_QLN], row[:, PK_QLN:PK_KVLN], row[:, PK_KVLN:PK_QHN],
            row[:, PK_QHN:PK_QHN + MLA_QK], row[:, PK_KHN:PK_KHN + MLA_QK], row[:, PK_ADA:PK_LOSS])


def kernel(x, c, positions, w_ada, b_ada, norm_w, w_in, q_lora_norm, w_uq, kv_lora_norm, w_ukv, q_head_norm, k_head_norm, w_out, loss_target, m_w_ada, m_b_ada, m_norm_w, m_w_in, m_q_lora_norm, m_w_uq, m_kv_lora_norm, m_w_ukv, m_q_head_norm, m_k_head_norm, m_w_out, v_w_ada, v_b_ada, v_norm_w, v_w_in, v_q_lora_norm, v_w_uq, v_kv_lora_norm, v_w_ukv, v_q_head_norm, v_k_head_norm, v_w_out):
    s_len = x.shape[1]
    x2 = x.reshape(s_len, D_MODEL)
    tgt = loss_target.reshape(s_len, D_MODEL)
    w_ada_s, w_in_s, w_uq_s, w_ukv_s, w_out_s = w_ada[0], w_in[0], w_uq[0], w_ukv[0], w_out[0]

    ada8 = _ada_fwd(jnp.broadcast_to(c, (8, D_MODEL)), w_ada_s, b_ada.reshape(N_DEV, -1))
    ada = ada8.reshape(1, 3 * D_MODEL)
    shift, scale, gate = ada[:, :D_MODEL], ada[:, D_MODEL:2 * D_MODEL], ada[:, 2 * D_MODEL:]

    g_in, g_uq, g_ukv, g_out = _gather_weights([w_in_s, w_uq_s, w_ukv_s, w_out_s])
    w_in_r = _rearrange_cols(_from_col_shards(g_in))
    wuq_p = _pad_heads(_from_col_shards(g_uq))
    wukv_f = _from_col_shards(g_ukv)
    w_out_f = g_out.reshape(D_MODEL, D_MODEL)

    cosf, sin_a, sin_b = _rope_tables(positions[0])
    qhn_p, khn_p = _pad_lanes(q_head_norm), _pad_lanes(k_head_norm)

    hb, qkv, g_sb, cq, ckv, g_mla, kr, qm, km, vm = _fwd_pre(
        x2, shift, scale, norm_w, w_in_r, q_lora_norm, wuq_p, kv_lora_norm, wukv_f, qhn_p, khn_p,
        cosf, sin_a, sin_b)
    o_sb, w_saved, l_saved = _sb_fwd(qkv)
    o_mla, lse = _mla_fwd(qm, km, vm)

    dy, do_sb, dg_sb, do_mla, dg_mla, gw_out, d_gate, loss_acc = _mid(
        o_sb, g_sb, o_mla, g_mla, x2, tgt, gate, w_out_f)

    dq_sb, dk_sb, dv_sb = _sb_bwd(qkv, do_sb, w_saved, l_saved)
    dq_m, dk_m, dv_m = _mla_bwd(qm, km, vm, o_mla, do_mla, lse)
    dcq, dckv, dkr, gw_uq_p, gw_ukv, g_qhn, g_khn, g_qln, g_kvln = _mla_pre_bwd(
        dq_m, dk_m, dv_m, cq, ckv, kr, q_lora_norm, wuq_p, kv_lora_norm, wukv_f, qhn_p, khn_p,
        cosf, sin_a, sin_b)
    dproj, grad_x, d_shift, d_scale, g_norm_w = _dproj_bwd(
        dq_sb, dk_sb, dv_sb, dg_sb, dcq, dckv, dg_mla, dkr, w_in_r, x2, dy, norm_w, scale)
    gw_in = _restore_cols(_grad_w_in(hb, dproj))

    d_ada = jnp.concatenate([d_shift, d_scale, d_gate], axis=1)
    gpack = _pack(g_norm_w, g_qln, g_kvln, g_qhn[:, :MLA_QK], g_khn[:, :MLA_QK], d_ada, loss_acc)
    wpack = _pack(norm_w, q_lora_norm, kv_lora_norm, q_head_norm, k_head_norm, b_ada)
    mpack = _pack(m_norm_w, m_q_lora_norm, m_kv_lora_norm, m_q_head_norm, m_k_head_norm, m_b_ada)
    vpack = _pack(v_norm_w, v_q_lora_norm, v_kv_lora_norm, v_q_head_norm, v_k_head_norm, v_b_ada)
    ccol = jnp.broadcast_to(c.reshape(D_MODEL, 1), (D_MODEL, LANES))
    pg, pd, pm, pv, ada_g, ada_d, ada_m, ada_v = _vec_exchange(
        gpack, ccol, wpack, mpack, vpack, w_ada_s, m_w_ada[0], v_w_ada[0])
    loss = 0.5 * jnp.sum(pg[0, PK_LOSS:PK_END]) / D_MODEL

    r_in, r_uq, r_ukv, r_out = _grad_exchange([g.astype(BF16) for g in (
        _col_shards(gw_in), _col_shards(_unpad_heads(gw_uq_p)), _col_shards(gw_ukv),
        gw_out.reshape(N_DEV, D_MODEL // N_DEV, D_MODEL))])
    in_g, in_d, in_m, in_v = _adamw_reduce("adamw_w_in", r_in, w_in_s, m_w_in[0], v_w_in[0], 256)
    uq_g, uq_d, uq_m, uq_v = _adamw_reduce("adamw_w_uq", r_uq, w_uq_s, m_w_uq[0], v_w_uq[0], w_uq_s.shape[0])
    ukv_g, ukv_d, ukv_m, ukv_v = _adamw_reduce(
        "adamw_w_ukv", r_ukv, w_ukv_s, m_w_ukv[0], v_w_ukv[0], w_ukv_s.shape[0])
    out_g, out_d, out_m, out_v = _adamw_reduce(
        "adamw_w_out", r_out, w_out_s, m_w_out[0], v_w_out[0], w_out_s.shape[0])

    def group(ada_t, pk, in_t, uq_t, ukv_t, out_t):
        nw, qln, kvln, qhn, khn, b = _unpack(pk)
        return (ada_t[None], b, nw, in_t[None], qln, uq_t[None], kvln, ukv_t[None], qhn, khn, out_t[None])

    return (loss, grad_x.reshape(1, s_len, D_MODEL),
            *group(ada_g, pg, in_g, uq_g, ukv_g, out_g),
            *group(ada_d, pd, in_d, uq_d, ukv_d, out_d),
            *group(ada_m, pm, in_m, uq_m, ukv_m, out_m),
            *group(ada_v, pv, in_v, uq_v, ukv_v, out_v))
```

```python
import functools
import math

import jax
import jax.numpy as jnp
from jax import lax
from jax.experimental import pallas as pl
from jax.experimental.pallas import tpu as pltpu

F32 = jnp.float32
BF16 = jnp.bfloat16

N_DEV = 8
D_MODEL = 1024
HEADS = 8
SB_WIDTH = 512
MLA_WIDTH = 512
Q_LORA = 384
KV_LORA = 256
ROPE = 32
NOPE = 64
MLA_QK = 96
LANES = 128
IN_COLS = 3232
IN_COLS_R = 3328
EPS = 1e-6
NEG = -1e30

ADAM_LR = 0.001
ADAM_B1 = 0.9
ADAM_B2 = 0.999
ADAM_EPS = 1e-08
ADAM_WD = 0.01
ADAM_STEP = 10

TS = 256
TQ = 512
KEY_UNROLL = 2
TN_S = 512
VMEM_BIG = 56 * 1024 * 1024

PK_NORM, PK_QLN, PK_KVLN, PK_QHN, PK_KHN, PK_ADA, PK_LOSS, PK_END = 0, 1024, 1408, 1664, 1792, 1920, 4992, 6016

MESH_ID = pl.DeviceIdType.MESH


def _dot_nn(a, b):
    return lax.dot_general(a, b, (((1,), (0,)), ((), ())), preferred_element_type=F32)


def _dot_nt(a, b):
    return lax.dot_general(a, b, (((1,), (1,)), ((), ())), preferred_element_type=F32)


def _dot_tn(a, b):
    return lax.dot_general(a, b, (((0,), (0,)), ((), ())), preferred_element_type=F32)


def _split_bf16(a):
    hi = a.astype(BF16)
    lo = (a - hi.astype(F32)).astype(BF16)
    return hi, lo


def _dot3(a, b):
    ah, al = _split_bf16(a)
    bh, bl = _split_bf16(b)
    return _dot_nn(ah, bh) + _dot_nn(ah, bl) + _dot_nn(al, bh)


def _sigmoid(g):
    return 1.0 / (1.0 + jnp.exp(-g))


def _silu(g):
    return g * _sigmoid(g)


def _lane_iota(shape):
    return lax.broadcasted_iota(jnp.int32, shape, len(shape) - 1)


def _adamw(w, g, m, v):
    m = ADAM_B1 * m + (1.0 - ADAM_B1) * g
    v = ADAM_B2 * v + (1.0 - ADAM_B2) * (g * g)
    m_hat = m / (1.0 - ADAM_B1 ** ADAM_STEP)
    v_hat = v / (1.0 - ADAM_B2 ** ADAM_STEP)
    delta = -ADAM_LR * (m_hat / (jnp.sqrt(v_hat) + ADAM_EPS) + ADAM_WD * w)
    return delta, m, v


def _mesh_pos():
    return lax.axis_index("x"), lax.axis_index("y"), lax.axis_index("c")


def _peer(pos, k):
    x, y, c = pos
    return (1 - x if k & 4 else x, 1 - y if k & 2 else y, 1 - c if k & 1 else c)


def _lin(pos):
    return 4 * pos[0] + 2 * pos[1] + pos[2]


def _remote(src, dst, send_sems, recv_sems, idx, peer):
    return pltpu.make_async_remote_copy(
        src_ref=src, dst_ref=dst, send_sem=send_sems.at[idx], recv_sem=recv_sems.at[idx],
        device_id=peer, device_id_type=MESH_ID)


def _all_gather(pos, src, buf, send_sems, recv_sems, base):
    me = _lin(pos)
    sent = []
    for k in range(1, N_DEV):
        cp = _remote(src, buf.at[me], send_sems, recv_sems, base + k - 1, _peer(pos, k))
        cp.start()
        sent.append(cp)
    for k in range(1, N_DEV):
        peer = _peer(pos, k)
        _remote(src, buf.at[_lin(peer)], send_sems, recv_sems, base + k - 1, peer).wait_recv()
    for cp in sent:
        cp.wait_send()


def _all_to_all(pos, src, buf, send_sems, recv_sems, base):
    me = _lin(pos)
    sent = []
    for k in range(1, N_DEV):
        peer = _peer(pos, k)
        cp = _remote(src.at[_lin(peer)], buf.at[me], send_sems, recv_sems, base + k - 1, peer)
        cp.start()
        sent.append(cp)
    for k in range(1, N_DEV):
        peer = _peer(pos, k)
        _remote(src.at[me], buf.at[_lin(peer)], send_sems, recv_sems, base + k - 1, peer).wait_recv()
    for cp in sent:
        cp.wait_send()


def _vmem_spec():
    return pl.BlockSpec(memory_space=pltpu.VMEM)


def _any_spec():
    return pl.BlockSpec(memory_space=pl.ANY)


def _row_select(slots, n):
    r = lax.broadcasted_iota(jnp.int32, (N_DEV, n), 0)
    out = jnp.zeros((N_DEV, n), F32)
    for j in range(N_DEV):
        out = out + jnp.where(r == j, slots[j], 0.0)
    return out


def _ada_fwd(c8, w_ada, b_ada8):
    n_sh = w_ada.shape[1]

    def body(c_ref, w_ref, b_ref, out_ref, call_ref, psend_ref, precv_ref, ssem, rsem):
        pos = _mesh_pos()
        me = _lin(pos)
        call_ref[me] = c_ref[...]
        _all_gather(pos, c_ref, call_ref, ssem, rsem, 0)
        w = w_ref[...]
        for j in range(N_DEV):
            psend_ref[j] = _dot3(_silu(call_ref[j]), w)
        precv_ref[me] = psend_ref[me]
        _all_to_all(pos, psend_ref, precv_ref, ssem, rsem, N_DEV - 1)
        out_ref[...] = _row_select([precv_ref[j] for j in range(N_DEV)], n_sh) + b_ref[...]

    return pl.pallas_call(
        body, name="ada_fwd",
        out_shape=jax.ShapeDtypeStruct((N_DEV, n_sh), F32),
        in_specs=[_vmem_spec()] * 3, out_specs=_vmem_spec(),
        scratch_shapes=[
            pltpu.VMEM((N_DEV, 8, D_MODEL), F32),
            pltpu.VMEM((N_DEV, 8, n_sh), F32),
            pltpu.VMEM((N_DEV, 8, n_sh), F32),
            pltpu.SemaphoreType.DMA((2 * (N_DEV - 1),)),
            pltpu.SemaphoreType.DMA((2 * (N_DEV - 1),)),
        ],
    )(c8, w_ada, b_ada8)


def _gather_weights(shards):
    n = len(shards)

    def body(*refs):
        ins, outs = refs[:n], refs[n:2 * n]
        ssem, rsem = refs[2 * n], refs[2 * n + 1]
        pos = _mesh_pos()
        me = _lin(pos)
        for a in range(n):
            outs[a][me] = ins[a][...].astype(BF16)
        for a in range(n):
            _all_gather(pos, outs[a].at[me], outs[a], ssem, rsem, a * (N_DEV - 1))

    return pl.pallas_call(
        body, name="gather_weights",
        out_shape=[jax.ShapeDtypeStruct((N_DEV,) + s.shape, BF16) for s in shards],
        in_specs=[_vmem_spec()] * n, out_specs=[_vmem_spec()] * n,
        scratch_shapes=[
            pltpu.SemaphoreType.DMA((n * (N_DEV - 1),)),
            pltpu.SemaphoreType.DMA((n * (N_DEV - 1),)),
        ],
        compiler_params=pltpu.CompilerParams(vmem_limit_bytes=VMEM_BIG),
    )(*shards)


def _rope(t, cosf, sin_a, sin_b):
    return t * cosf + pltpu.roll(t, 112, 1) * sin_a + pltpu.roll(t, 16, 1) * sin_b


def _rope_t(d, cosf, sin_a, sin_b):
    return d * cosf + pltpu.roll(d * sin_a, 16, 1) + pltpu.roll(d * sin_b, 112, 1)


def _head_rms(t):
    return lax.rsqrt(jnp.sum(t * t, axis=1, keepdims=True) * (1.0 / MLA_QK) + EPS)


def _rms_bwd(dxhat_w, xhat, r, n):
    return r * (dxhat_w - xhat * (jnp.sum(dxhat_w * xhat, axis=1, keepdims=True) * (1.0 / n)))


def _mla_latents(cq, ckv, kr, qln, kvln, wuq, wukv):
    rq = lax.rsqrt(jnp.mean(cq * cq, axis=1, keepdims=True) + EPS)
    rkv = lax.rsqrt(jnp.mean(ckv * ckv, axis=1, keepdims=True) + EPS)
    cq_hat = cq * rq
    ckv_hat = ckv * rkv
    cqn = (cq_hat * qln).astype(BF16)
    ckvn = (ckv_hat * kvln).astype(BF16)
    q_all = _dot_nn(cqn, wuq)
    kv = _dot_nn(ckvn, wukv)
    kr64 = pltpu.roll(kr, 64, 1)
    return rq, rkv, cq_hat, ckv_hat, cqn, ckvn, q_all, kv, kr64


def _fwd_pre(x, shift, scale, norm_w, w_in_r, qln, wuq, kvln, wukv, qhn, khn, cosf, sin_a, sin_b):
    s_len = x.shape[0]

    def body(x_ref, shift_ref, scale_ref, nw_ref, w_ref, qln_ref, wuq_ref, kvln_ref, wukv_ref,
             qhn_ref, khn_ref, cos_ref, sa_ref, sb_ref,
             hb_ref, qkv_ref, gsb_ref, cq_ref, ckv_ref, gmla_ref, kr_ref, qm_ref, km_ref, vm_ref):
        xv = x_ref[...]
        r = lax.rsqrt(jnp.mean(xv * xv, axis=1, keepdims=True) + EPS)
        h = (xv * r) * nw_ref[...] * (1.0 + scale_ref[...]) + shift_ref[...]
        hb = h.astype(BF16)
        hb_ref[...] = hb
        qkv_ref[...] = _dot_nn(hb, w_ref[:, 0:1536]).astype(BF16)
        gsb_ref[...] = _dot_nn(hb, w_ref[:, 1536:2048])
        cq = _dot_nn(hb, w_ref[:, 2048:2432])
        ckv = _dot_nn(hb, w_ref[:, 2432:2688])
        gmla_ref[...] = _dot_nn(hb, w_ref[:, 2688:3200])
        kr = _dot_nn(hb, w_ref[:, 3200:3328])
        cq_ref[...] = cq
        ckv_ref[...] = ckv
        kr_ref[...] = kr
        _, _, _, _, _, _, q_all, kv, kr64 = _mla_latents(
            cq, ckv, kr, qln_ref[...], kvln_ref[...], wuq_ref[...], wukv_ref[...])
        cosf, sa, sb = cos_ref[...], sa_ref[...], sb_ref[...]
        qhn_v, khn_v = qhn_ref[...], khn_ref[...]
        low = _lane_iota((TS, LANES)) < NOPE
        for hd in range(HEADS):
            blk = slice(LANES * hd, LANES * (hd + 1))
            qb = q_all[:, blk]
            qm_ref[:, blk] = _rope(qb * _head_rms(qb) * qhn_v, cosf, sa, sb).astype(BF16)
            kb = jnp.where(low, kv[:, blk], kr64)
            km_ref[:, blk] = _rope(kb * _head_rms(kb) * khn_v, cosf, sa, sb).astype(BF16)
        for p in range(HEADS // 2):
            even = kv[:, LANES * 2 * p:LANES * (2 * p + 1)]
            odd = kv[:, LANES * (2 * p + 1):LANES * (2 * p + 2)]
            vm_ref[:, LANES * p:LANES * (p + 1)] = jnp.where(low, pltpu.roll(even, 64, 1), odd).astype(BF16)

    def tile(width):
        return pl.BlockSpec((TS, width), lambda i: (i, 0))

    def full(a):
        return pl.BlockSpec(a.shape, lambda i: (0, 0))

    out_widths = [(D_MODEL, BF16), (1536, BF16), (512, F32), (Q_LORA, F32), (KV_LORA, F32),
                  (512, F32), (LANES, F32), (1024, BF16), (1024, BF16), (512, BF16)]
    return pl.pallas_call(
        body, name="fwd_pre", grid=(s_len // TS,),
        out_shape=[jax.ShapeDtypeStruct((s_len, w), dt) for w, dt in out_widths],
        in_specs=[tile(D_MODEL), full(shift), full(scale), full(norm_w), full(w_in_r), full(qln), full(wuq),
                  full(kvln), full(wukv), full(qhn), full(khn), tile(LANES), tile(LANES), tile(LANES)],
        out_specs=[tile(w) for w, _ in out_widths],
        compiler_params=pltpu.CompilerParams(dimension_semantics=("parallel",), vmem_limit_bytes=VMEM_BIG),
    )(x, shift, scale, norm_w, w_in_r, qln, wuq, kvln, wukv, qhn, khn, cosf, sin_a, sin_b)


CUM_W = 256


def _tri(strict):
    j = lax.broadcasted_iota(jnp.int32, (CUM_W, CUM_W), 0)
    s = lax.broadcasted_iota(jnp.int32, (CUM_W, CUM_W), 1)
    return (j > s if strict else j >= s).astype(BF16)


def _suffix_sums(a, a_bf16, tri_m, carry):
    n = a.shape[1] // CUM_W
    outs = [None] * n
    for i in reversed(range(n)):
        cols = slice(CUM_W * i, CUM_W * (i + 1))
        outs[i] = _dot_nn(a_bf16[:, cols], tri_m) + carry
        carry = carry + _rowsum(a[:, cols])
    return (outs[0] if n == 1 else jnp.concatenate(outs, axis=1)), carry


def _sb_weights(qm, kb, carry, tri_u, diag):
    z = _dot_nt(qm, kb)
    nz = -z
    lk = jnp.minimum(nz, 0.0) - jnp.log(1.0 + jnp.exp(jnp.minimum(z, nz)))
    if diag:
        t = lax.broadcasted_iota(jnp.int32, (TQ, TQ), 0)
        s = lax.broadcasted_iota(jnp.int32, (TQ, TQ), 1)
        valid = s < t
        lk = jnp.where(valid, lk, 0.0)
    lk_hi = lk.astype(BF16)
    after, carry = _suffix_sums(lk, lk_hi, tri_u, carry)
    logw = z + lk + after
    if diag:
        logw = jnp.where(valid, logw, NEG)
    return lk_hi, jnp.exp(logw), carry


SB_SCALE = 0.125


def _head_masks():
    lane = _lane_iota((1, LANES))
    return [lane < 64, lane >= 64]


def _masked(hm, a):
    return jnp.where(hm, a, jnp.zeros_like(a))


def _rowsum(a):
    return jnp.sum(a, axis=1, keepdims=True)


def _key_rows(kj):
    return pl.ds(pl.multiple_of(kj * TQ, TQ), TQ)


def _over_key_tiles(count, fn, st, ascending):
    n_full = count // KEY_UNROLL
    n_rest = count - n_full * KEY_UNROLL

    def group(g, s_):
        return fn([g * KEY_UNROLL + (u if ascending else KEY_UNROLL - 1 - u) for u in range(KEY_UNROLL)], s_)

    if ascending:
        st = lax.fori_loop(0, n_full, group, st)
        return lax.fori_loop(0, n_rest, lambda i, s_: fn([n_full * KEY_UNROLL + i], s_), st)
    st = lax.fori_loop(0, n_rest, lambda i, s_: fn([count - 1 - i], s_), st)
    return lax.fori_loop(0, n_full, lambda i, s_: group(n_full - 1 - i, s_), st)


def _each_tile(block):
    def trip(tiles, st):
        for kj in tiles:
            st = block(_key_rows(kj), st, False)
        return st
    return trip


STAGE_SLOTS = 2 * KEY_UNROLL


def _stage_copies(to_hbm, hbm_refs, scr_refs, sems, pair, qi, kj):
    slot = (qi - kj) % STAGE_SLOTS
    out = []
    for h in range(2):
        for a in range(2):
            hbm, scr = hbm_refs[a].at[2 * pair + h, qi, kj], scr_refs[a].at[slot, h]
            sem = sems.at[4 * slot + 2 * h + a]
            out.append(pltpu.make_async_copy(scr, hbm, sem) if to_hbm else pltpu.make_async_copy(hbm, scr, sem))
    return out


def _sb_fwd(qkv):
    s_len = qkv.shape[0]
    nq = s_len // TQ

    def body(q_ref, k_ref, v_ref, o_ref, w_hbm, l_hbm, w_scr, l_scr, sems):
        pair, qi = pl.program_id(0), pl.program_id(1)
        q = q_ref[...]
        tri_u = _tri(True)
        masks = _head_masks()
        qms = [_masked(hm, q) * SB_SCALE for hm in masks]

        def copies(kj, of_qi=qi):
            return _stage_copies(True, (w_hbm, l_hbm), (w_scr, l_scr), sems, pair, of_qi, kj)

        def drain(of_qi):
            for kj in range(STAGE_SLOTS):
                @pl.when(kj <= of_qi)
                def _():
                    for cp in copies(kj, of_qi):
                        cp.wait()

        def block(kj, st, diag, before_staging=None):
            rows = _key_rows(kj)
            slot = (qi - kj) % STAGE_SLOTS
            kb, vb = k_ref[rows, :], v_ref[rows, :]
            carries, acc = list(st[:2]), st[2]
            staged = []
            for h in range(2):
                lk_hi, w, carries[h] = _sb_weights(qms[h], kb, carries[h], tri_u, diag)
                wb = w.astype(BF16)
                staged.append((wb, lk_hi))
                acc = acc + _dot_nn(wb, _masked(masks[h], vb))
            if before_staging is not None:
                before_staging()
            for h in range(2):
                w_scr[slot, h], l_scr[slot, h] = staged[h]
            return carries[0], carries[1], acc

        def trip(tiles, st):
            for kj in tiles:
                @pl.when(qi - kj >= STAGE_SLOTS)
                def _():
                    for cp in copies(kj + STAGE_SLOTS):
                        cp.wait()
            for kj in tiles:
                st = block(kj, st, False)
            for kj in tiles:
                for cp in copies(kj):
                    cp.start()
            return st

        def drain_previous_step():
            @pl.when(jnp.logical_or(pair > 0, qi > 0))
            def _():
                drain(jnp.where(qi == 0, nq - 1, qi - 1))

        zc = jnp.zeros((TQ, 1), F32)
        st = block(qi, (zc, zc, jnp.zeros((TQ, LANES), F32)), True, drain_previous_step)
        for cp in copies(qi):
            cp.start()
        st = _over_key_tiles(qi, trip, st, ascending=False)
        o_ref[...] = st[2]

        @pl.when(jnp.logical_and(pair == HEADS // 2 - 1, qi == nq - 1))
        def _():
            drain(qi)

    saved = jax.ShapeDtypeStruct((HEADS, nq, nq, TQ, TQ), BF16)
    stage = pltpu.VMEM((STAGE_SLOTS, 2, TQ, TQ), BF16)
    return pl.pallas_call(
        body, name="sb_fwd", grid=(HEADS // 2, nq),
        out_shape=[jax.ShapeDtypeStruct((s_len, SB_WIDTH), F32), saved, saved],
        in_specs=[pl.BlockSpec((TQ, LANES), lambda p, i: (i, p)),
                  pl.BlockSpec((s_len, LANES), lambda p, i: (0, 4 + p)),
                  pl.BlockSpec((s_len, LANES), lambda p, i: (0, 8 + p))],
        out_specs=[pl.BlockSpec((TQ, LANES), lambda p, i: (i, p)), _any_spec(), _any_spec()],
        scratch_shapes=[stage, stage, pltpu.SemaphoreType.DMA((4 * STAGE_SLOTS,))],
        compiler_params=pltpu.CompilerParams(dimension_semantics=("arbitrary", "arbitrary"),
                                             vmem_limit_bytes=VMEM_BIG),
    )(qkv, qkv, qkv)


def _prefix_sums(a, tri_m, carry):
    n = a.shape[1] // CUM_W
    outs = [None] * n
    for i in range(n):
        cols = slice(CUM_W * i, CUM_W * (i + 1))
        outs[i] = _dot_nn(a[:, cols].astype(BF16), tri_m) + carry
        carry = carry + _rowsum(a[:, cols])
    return (outs[0] if n == 1 else jnp.concatenate(outs, axis=1)), carry


def _head_rows(a0, a1):
    sub = lax.broadcasted_iota(jnp.int32, a0.shape, 0)
    return jnp.where(sub < NOPE, a0, a1)


def _sb_bwd(qkv, q_t, do, do_t, w_saved, l_saved):
    s_len = qkv.shape[0]
    nq = s_len // TQ

    def body(qt_ref, k_ref, v_ref, do_ref, dot_ref, w_hbm, l_hbm, dq_ref, dk_ref, dv_ref, w_scr, l_scr, sems):
        pair, qi = pl.program_id(0), pl.program_id(1)

        def copies(kj, of_pair=pair, of_qi=qi):
            return _stage_copies(False, (w_hbm, l_hbm), (w_scr, l_scr), sems, of_pair, of_qi, kj)

        def start_first_tiles(of_pair, of_qi):
            for cp in copies(0, of_pair, of_qi):
                cp.start()

            @pl.when(of_qi >= 1)
            def _():
                for cp in copies(1, of_pair, of_qi):
                    cp.start()

        @pl.when(jnp.logical_and(pair == 0, qi == 0))
        def _():
            start_first_tiles(pair, qi)

        @pl.when(qi == 0)
        def _():
            dk_ref[...] = jnp.zeros_like(dk_ref)
            dv_ref[...] = jnp.zeros_like(dv_ref)

        qt = qt_ref[...] * SB_SCALE
        dot_v = dot_ref[...]
        do_v = do_ref[...]
        j = lax.broadcasted_iota(jnp.int32, (CUM_W, CUM_W), 0)
        s = lax.broadcasted_iota(jnp.int32, (CUM_W, CUM_W), 1)
        tri_before = (j < s).astype(BF16)
        masks = _head_masks()
        doms = [_masked(hm, do_v) for hm in masks]

        def block(kj, st):
            rows = _key_rows(kj)
            slot = (qi - kj) % STAGE_SLOTS
            kb, vb = k_ref[rows, :], v_ref[rows, :]
            carries, dqs = list(st[0:2]), list(st[2:4])
            dk_t, dv_t = [], []
            for h in range(2):
                wb = w_scr[slot, h]
                d_l = _dot_nt(doms[h], vb) * wb.astype(F32)
                before, carries[h] = _prefix_sums(d_l, tri_before, carries[h])
                keep = jnp.exp(l_scr[slot, h].astype(F32))
                dzb = (d_l * keep - before * (1.0 - keep)).astype(BF16)
                dk_t.append(_dot_nn(qt, dzb))
                dv_t.append(_dot_nn(dot_v, wb))
                dqs[h] = dqs[h] + _dot_nn(dzb, kb)
            dk_ref[kj] += _head_rows(*dk_t)
            dv_ref[kj] += _head_rows(*dv_t)
            return (*carries, *dqs)

        def trip(tiles, st):
            for kj in tiles:
                @pl.when(kj + 2 <= qi)
                def _():
                    for cp in copies(kj + 2):
                        cp.start()
            for kj in tiles:
                for cp in copies(kj):
                    cp.wait()
            for kj in tiles:
                st = block(kj, st)
            return st

        zc = jnp.zeros((TQ, 1), F32)
        za = jnp.zeros((TQ, LANES), F32)
        st = _over_key_tiles(qi + 1, trip, (zc, zc, za, za), ascending=True)
        dq_ref[...] = jnp.where(masks[0], st[2], st[3]) * SB_SCALE

        @pl.when(jnp.logical_or(pair < HEADS // 2 - 1, qi < nq - 1))
        def _():
            wraps = qi == nq - 1
            start_first_tiles(jnp.where(wraps, pair + 1, pair), jnp.where(wraps, 0, qi + 1))

    tile = pl.BlockSpec((TQ, LANES), lambda p, i: (i, p))
    tile_t = pl.BlockSpec((LANES, TQ), lambda p, i: (p, i))
    col_t = pl.BlockSpec((nq, LANES, TQ), lambda p, i: (0, p, 0))
    stage = pltpu.VMEM((STAGE_SLOTS, 2, TQ, TQ), BF16)
    key_t = jax.ShapeDtypeStruct((nq, SB_WIDTH, TQ), F32)
    return pl.pallas_call(
        body, name="sb_bwd", grid=(HEADS // 2, nq),
        out_shape=[jax.ShapeDtypeStruct((s_len, SB_WIDTH), F32), key_t, key_t],
        in_specs=[tile_t,
                  pl.BlockSpec((s_len, LANES), lambda p, i: (0, 4 + p)),
                  pl.BlockSpec((s_len, LANES), lambda p, i: (0, 8 + p)),
                  tile, tile_t, _any_spec(), _any_spec()],
        out_specs=[tile, col_t, col_t],
        scratch_shapes=[stage, stage, pltpu.SemaphoreType.DMA((4 * STAGE_SLOTS,))],
        compiler_params=pltpu.CompilerParams(dimension_semantics=("arbitrary", "arbitrary"),
                                             vmem_limit_bytes=VMEM_BIG),
    )(q_t, qkv, qkv, do, do_t, w_saved, l_saved)


def _from_key_tiles(a_t):
    return a_t.transpose(0, 2, 1).reshape(a_t.shape[0] * a_t.shape[2], a_t.shape[1])


MLA_SCALE = 1.0 / math.sqrt(MLA_QK)


def _causal_mask():
    t = lax.broadcasted_iota(jnp.int32, (TQ, TQ), 0)
    s = lax.broadcasted_iota(jnp.int32, (TQ, TQ), 1)
    return s <= t


def _head_lanes(h):
    return slice(LANES * h, LANES * (h + 1))


def _mla_fwd(qm, km, vm):
    s_len = qm.shape[0]

    def body(q_ref, k_ref, v_ref, o_ref, lse_ref):
        qi = pl.program_id(1)
        masks = _head_masks()
        qhs = [q_ref[:, _head_lanes(h)] for h in range(2)]

        def block(rows, st, diag):
            vb = v_ref[rows, :]
            ms, ls, acc = list(st[0:2]), list(st[2:4]), st[4]
            alphas, pvs = [], []
            for h in range(2):
                s = _dot_nt(qhs[h], k_ref[rows, _head_lanes(h)]) * MLA_SCALE
                if diag:
                    s = jnp.where(_causal_mask(), s, NEG)
                m_new = jnp.maximum(ms[h], jnp.max(s, axis=1, keepdims=True))
                p = jnp.exp(s - m_new)
                alphas.append(jnp.exp(ms[h] - m_new))
                ls[h] = alphas[h] * ls[h] + _rowsum(p)
                ms[h] = m_new
                pvs.append(_dot_nn(p.astype(BF16), _masked(masks[h], vb)))
            acc = jnp.where(masks[0], alphas[0], alphas[1]) * acc + pvs[0] + pvs[1]
            return (*ms, *ls, acc)

        neg = jnp.full((TQ, 1), NEG, F32)
        zc = jnp.zeros((TQ, 1), F32)
        st = (neg, neg, zc, zc, jnp.zeros((TQ, LANES), F32))
        st = _over_key_tiles(qi, _each_tile(block), st, ascending=True)
        m0, m1, l0, l1, acc = block(_key_rows(qi), st, True)
        o_ref[...] = acc / jnp.where(masks[0], l0, l1)
        lse_ref[0] = m0 + jnp.log(l0)
        lse_ref[1] = m1 + jnp.log(l1)

    return pl.pallas_call(
        body, name="mla_fwd", grid=(HEADS // 2, s_len // TQ),
        out_shape=[jax.ShapeDtypeStruct((s_len, MLA_WIDTH), F32),
                   jax.ShapeDtypeStruct((HEADS, s_len, 1), F32)],
        in_specs=[pl.BlockSpec((TQ, 2 * LANES), lambda p, i: (i, p)),
                  pl.BlockSpec((s_len, 2 * LANES), lambda p, i: (0, p)),
                  pl.BlockSpec((s_len, LANES), lambda p, i: (0, p))],
        out_specs=[pl.BlockSpec((TQ, LANES), lambda p, i: (i, p)),
                   pl.BlockSpec((2, TQ, 1), lambda p, i: (p, i, 0))],
        compiler_params=pltpu.CompilerParams(dimension_semantics=("parallel", "parallel"),
                                             vmem_limit_bytes=VMEM_BIG),
    )(qm, km, vm)


def _mla_bwd(qm, qm_t, km, vm, o, do, do_t, lse):
    s_len = qm.shape[0]

    nq = s_len // TQ

    def body(q_ref, qt_ref, k_ref, v_ref, o_ref, do_ref, dot_ref, lse_ref, dq_ref, dk_ref, dv_ref):
        qi = pl.program_id(1)

        @pl.when(qi == 0)
        def _():
            dk_ref[...] = jnp.zeros_like(dk_ref)
            dv_ref[...] = jnp.zeros_like(dv_ref)

        do_v = do_ref[...]
        dot_v = dot_ref[...]
        od = o_ref[...] * do_v.astype(F32)
        masks = _head_masks()
        qhs = [q_ref[:, _head_lanes(h)] for h in range(2)]
        qts = [qt_ref[_head_lanes(h), :] for h in range(2)]
        doms = [_masked(hm, do_v) for hm in masks]
        deltas = [_rowsum(jnp.where(hm, od, 0.0)) for hm in masks]
        lses = [lse_ref[h] for h in range(2)]

        def block(kj, dqs, diag):
            rows = _key_rows(kj)
            vb = v_ref[rows, :]
            dqs = list(dqs)
            dv_t = []
            for h in range(2):
                kb = k_ref[rows, _head_lanes(h)]
                s = _dot_nt(qhs[h], kb) * MLA_SCALE
                if diag:
                    s = jnp.where(_causal_mask(), s, NEG)
                p = jnp.exp(s - lses[h])
                dp = _dot_nt(doms[h], vb)
                ds = (p * (dp - deltas[h]) * MLA_SCALE).astype(BF16)
                dk_ref[kj, _head_lanes(h), :] += _dot_nn(qts[h], ds)
                dv_t.append(_dot_nn(dot_v, p.astype(BF16)))
                dqs[h] = dqs[h] + _dot_nn(ds, kb)
            dv_ref[kj] += _head_rows(*dv_t)
            return tuple(dqs)

        def trip(tiles, dqs):
            for kj in tiles:
                dqs = block(kj, dqs, False)
            return dqs

        za = jnp.zeros((TQ, LANES), F32)
        dqs = _over_key_tiles(qi, trip, (za, za), ascending=True)
        dqs = block(qi, dqs, True)
        dq_ref[:, _head_lanes(0)] = dqs[0]
        dq_ref[:, _head_lanes(1)] = dqs[1]

    return pl.pallas_call(
        body, name="mla_bwd", grid=(HEADS // 2, nq),
        out_shape=[jax.ShapeDtypeStruct((s_len, HEADS * LANES), F32),
                   jax.ShapeDtypeStruct((nq, HEADS * LANES, TQ), F32),
                   jax.ShapeDtypeStruct((nq, MLA_WIDTH, TQ), F32)],
        in_specs=[pl.BlockSpec((TQ, 2 * LANES), lambda p, i: (i, p)),
                  pl.BlockSpec((2 * LANES, TQ), lambda p, i: (p, i)),
                  pl.BlockSpec((s_len, 2 * LANES), lambda p, i: (0, p)),
                  pl.BlockSpec((s_len, LANES), lambda p, i: (0, p)),
                  pl.BlockSpec((TQ, LANES), lambda p, i: (i, p)),
                  pl.BlockSpec((TQ, LANES), lambda p, i: (i, p)),
                  pl.BlockSpec((LANES, TQ), lambda p, i: (p, i)),
                  pl.BlockSpec((2, TQ, 1), lambda p, i: (p, i, 0))],
        out_specs=[pl.BlockSpec((TQ, 2 * LANES), lambda p, i: (i, p)),
                   pl.BlockSpec((nq, 2 * LANES, TQ), lambda p, i: (0, p, 0)),
                   pl.BlockSpec((nq, LANES, TQ), lambda p, i: (0, p, 0))],
        compiler_params=pltpu.CompilerParams(dimension_semantics=("parallel", "arbitrary"),
                                             vmem_limit_bytes=VMEM_BIG),
    )(qm, qm_t, km, vm, o, do, do_t, lse)


def _mid(o_sb, g_sb, o_mla, g_mla, x, target, gate, w_out):
    s_len = x.shape[0]

    def body(osb_ref, gsb_ref, omla_ref, gmla_ref, x_ref, t_ref, gate_ref, w_ref,
             dy_ref, dosb_ref, dgsb_ref, domla_ref, dgmla_ref, gw_ref, dgate_ref, loss_ref):
        @pl.when(pl.program_id(0) == 0)
        def _():
            gw_ref[...] = jnp.zeros_like(gw_ref)
            dgate_ref[...] = jnp.zeros_like(dgate_ref)
            loss_ref[...] = jnp.zeros_like(loss_ref)

        o1, g1, o2, g2 = osb_ref[...], gsb_ref[...], omla_ref[...], gmla_ref[...]
        s1, s2 = _sigmoid(g1), _sigmoid(g2)
        mixed = jnp.concatenate([o1 * (g1 * s1), o2 * (g2 * s2)], axis=1).astype(BF16)
        w = w_ref[...]
        gate_v = gate_ref[...]
        u = _dot_nn(mixed, w)
        err = x_ref[...] + gate_v * u - t_ref[...]
        loss_ref[...] += jnp.sum(err * err, axis=0, keepdims=True)
        dy = err * (1.0 / D_MODEL)
        dy_ref[...] = dy
        dgate_ref[...] += jnp.sum(dy * u, axis=0, keepdims=True)
        du = (dy * gate_v).astype(BF16)
        gw_ref[...] += _dot_tn(mixed, du)
        dmixed = _dot_nt(du, w)
        d1, d2 = dmixed[:, :SB_WIDTH], dmixed[:, SB_WIDTH:]
        dosb_ref[...] = (d1 * (g1 * s1)).astype(BF16)
        dgsb_ref[...] = (d1 * o1 * (s1 * (1.0 + g1 * (1.0 - s1)))).astype(BF16)
        domla_ref[...] = (d2 * (g2 * s2)).astype(BF16)
        dgmla_ref[...] = (d2 * o2 * (s2 * (1.0 + g2 * (1.0 - s2)))).astype(BF16)

    def tile(width):
        return pl.BlockSpec((TS, width), lambda i: (i, 0))

    def full(shape):
        return pl.BlockSpec(shape, lambda i: (0, 0))

    return pl.pallas_call(
        body, name="mid", grid=(s_len // TS,),
        out_shape=[jax.ShapeDtypeStruct((s_len, D_MODEL), F32)]
        + [jax.ShapeDtypeStruct((s_len, 512), BF16)] * 4
        + [jax.ShapeDtypeStruct((D_MODEL, D_MODEL), F32),
           jax.ShapeDtypeStruct((1, D_MODEL), F32), jax.ShapeDtypeStruct((1, D_MODEL), F32)],
        in_specs=[tile(512)] * 4 + [tile(D_MODEL), tile(D_MODEL), full((1, D_MODEL)), full((D_MODEL, D_MODEL))],
        out_specs=[tile(D_MODEL)] + [tile(512)] * 4
        + [full((D_MODEL, D_MODEL)), full((1, D_MODEL)), full((1, D_MODEL))],
        compiler_params=pltpu.CompilerParams(dimension_semantics=("arbitrary",), vmem_limit_bytes=VMEM_BIG),
    )(o_sb, g_sb, o_mla, g_mla, x, target, gate, w_out)


def _mla_pre_bwd(dq, dk, dv, cq, ckv, kr, qln, wuq, kvln, wukv, qhn, khn, cosf, sin_a, sin_b):
    s_len = cq.shape[0]

    def body(dq_ref, dk_ref, dv_ref, cq_ref, ckv_ref, kr_ref, qln_ref, wuq_ref, kvln_ref, wukv_ref,
             qhn_ref, khn_ref, cos_ref, sa_ref, sb_ref,
             dcq_ref, dckv_ref, dkr_ref, gwuq_ref, gwukv_ref, gqhn_ref, gkhn_ref, gqln_ref, gkvln_ref,
             dqa_ref, dkv_ref):
        @pl.when(pl.program_id(0) == 0)
        def _():
            for r_ in (gwuq_ref, gwukv_ref, gqhn_ref, gkhn_ref, gqln_ref, gkvln_ref):
                r_[...] = jnp.zeros_like(r_)

        cq, ckv = cq_ref[...], ckv_ref[...]
        qln_v, kvln_v = qln_ref[...], kvln_ref[...]
        wuq_v, wukv_v = wuq_ref[...], wukv_ref[...]
        rq, rkv, cq_hat, ckv_hat, cqn, ckvn, q_all, kv, kr64 = _mla_latents(
            cq, ckv, kr_ref[...], qln_v, kvln_v, wuq_v, wukv_v)
        cosf, sa, sb = cos_ref[...], sa_ref[...], sb_ref[...]
        qhn_v, khn_v = qhn_ref[...], khn_ref[...]
        lane = _lane_iota((TS, LANES))
        low = lane < NOPE
        g_qhn = jnp.zeros((1, LANES), F32)
        g_khn = jnp.zeros((1, LANES), F32)
        dkr64 = jnp.zeros((TS, LANES), F32)
        for hd in range(HEADS):
            blk = slice(LANES * hd, LANES * (hd + 1))
            qb = q_all[:, blk]
            r = _head_rms(qb)
            xh = qb * r
            dn = _rope_t(dq_ref[:, blk], cosf, sa, sb)
            g_qhn = g_qhn + jnp.sum(dn * xh, axis=0, keepdims=True)
            dqa_ref[:, blk] = _rms_bwd(dn * qhn_v, xh, r, MLA_QK).astype(BF16)

            kb = jnp.where(low, kv[:, blk], kr64)
            r = _head_rms(kb)
            xh = kb * r
            dn = _rope_t(dk_ref[:, blk], cosf, sa, sb)
            g_khn = g_khn + jnp.sum(dn * xh, axis=0, keepdims=True)
            dkb = _rms_bwd(dn * khn_v, xh, r, MLA_QK)
            dkr64 = dkr64 + jnp.where(low, 0.0, dkb)
            dvp = dv_ref[:, LANES * (hd // 2):LANES * (hd // 2 + 1)]
            dvh = pltpu.roll(dvp, 64, 1) if hd % 2 == 0 else dvp
            dkv_ref[:, blk] = jnp.where(low, dkb, dvh).astype(BF16)
        gqhn_ref[...] += g_qhn
        gkhn_ref[...] += g_khn
        dkr_ref[...] = pltpu.roll(dkr64, 64, 1).astype(BF16)

        dqa = dqa_ref[...]
        gwuq_ref[...] += _dot_tn(cqn, dqa)
        dcqn = _dot_nt(dqa, wuq_v)
        gqln_ref[...] += jnp.sum(dcqn * cq_hat, axis=0, keepdims=True)
        dcq_ref[...] = _rms_bwd(dcqn * qln_v, cq_hat, rq, Q_LORA).astype(BF16)

        dkv = dkv_ref[...]
        gwukv_ref[...] += _dot_tn(ckvn, dkv)
        dckvn = _dot_nt(dkv, wukv_v)
        gkvln_ref[...] += jnp.sum(dckvn * ckv_hat, axis=0, keepdims=True)
        dckv_ref[...] = _rms_bwd(dckvn * kvln_v, ckv_hat, rkv, KV_LORA).astype(BF16)

    def tile(width):
        return pl.BlockSpec((TS, width), lambda i: (i, 0))

    def full(shape):
        return pl.BlockSpec(shape, lambda i: (0, 0))

    acc_shapes = [(Q_LORA, 1024), (KV_LORA, 1024), (1, LANES), (1, LANES), (1, Q_LORA), (1, KV_LORA)]
    return pl.pallas_call(
        body, name="mla_pre_bwd", grid=(s_len // TS,),
        out_shape=[jax.ShapeDtypeStruct((s_len, Q_LORA), BF16), jax.ShapeDtypeStruct((s_len, KV_LORA), BF16),
                   jax.ShapeDtypeStruct((s_len, LANES), BF16)]
        + [jax.ShapeDtypeStruct(s, F32) for s in acc_shapes],
        in_specs=[tile(1024), tile(1024), tile(512), tile(Q_LORA), tile(KV_LORA), tile(LANES),
                  full(qln.shape), full(wuq.shape), full(kvln.shape), full(wukv.shape),
                  full(qhn.shape), full(khn.shape), tile(LANES), tile(LANES), tile(LANES)],
        out_specs=[tile(Q_LORA), tile(KV_LORA), tile(LANES)] + [full(s) for s in acc_shapes],
        scratch_shapes=[pltpu.VMEM((TS, 1024), BF16), pltpu.VMEM((TS, 1024), BF16)],
        compiler_params=pltpu.CompilerParams(dimension_semantics=("arbitrary",), vmem_limit_bytes=VMEM_BIG),
    )(dq, dk, dv, cq, ckv, kr, qln, wuq, kvln, wukv, qhn, khn, cosf, sin_a, sin_b)


def _dproj_bwd(dq_sb, dk_sb, dv_sb, dg_sb, dcq, dckv, dg_mla, dkr, w_in_r, x, dy, norm_w, scale):
    s_len = x.shape[0]

    def body(dq_ref, dk_ref, dv_ref, dg_ref, dcq_ref, dckv_ref, dgm_ref, dkr_ref, w_ref, x_ref, dy_ref,
             nw_ref, scale_ref, dp_ref, gx_ref, dshift_ref, dscale_ref, dnw_ref):
        @pl.when(pl.program_id(0) == 0)
        def _():
            for r_ in (dshift_ref, dscale_ref, dnw_ref):
                r_[...] = jnp.zeros_like(r_)

        dp_ref[:, 0:512] = dq_ref[...].astype(BF16)
        dp_ref[:, 512:1024] = dk_ref[...].astype(BF16)
        dp_ref[:, 1024:1536] = dv_ref[...].astype(BF16)
        dp_ref[:, 1536:2048] = dg_ref[...]
        dp_ref[:, 2048:2432] = dcq_ref[...]
        dp_ref[:, 2432:2688] = dckv_ref[...]
        dp_ref[:, 2688:3200] = dgm_ref[...]
        dp_ref[:, 3200:3328] = dkr_ref[...]
        dh = _dot_nt(dp_ref[...], w_ref[...])
        xv = x_ref[...]
        r = lax.rsqrt(jnp.mean(xv * xv, axis=1, keepdims=True) + EPS)
        xh = xv * r
        nw = nw_ref[...]
        dshift_ref[...] += jnp.sum(dh, axis=0, keepdims=True)
        dscale_ref[...] += jnp.sum(dh * (xh * nw), axis=0, keepdims=True)
        dxnw = dh * (1.0 + scale_ref[...])
        dnw_ref[...] += jnp.sum(dxnw * xh, axis=0, keepdims=True)
        gx_ref[...] = dy_ref[...] + _rms_bwd(dxnw * nw, xh, r, D_MODEL)

    def tile(width):
        return pl.BlockSpec((TS, width), lambda i: (i, 0))

    def full(shape):
        return pl.BlockSpec(shape, lambda i: (0, 0))

    vec = (1, D_MODEL)
    return pl.pallas_call(
        body, name="dproj_bwd", grid=(s_len // TS,),
        out_shape=[jax.ShapeDtypeStruct((s_len, IN_COLS_R), BF16), jax.ShapeDtypeStruct((s_len, D_MODEL), F32)]
        + [jax.ShapeDtypeStruct(vec, F32)] * 3,
        in_specs=[tile(512)] * 4 + [tile(Q_LORA), tile(KV_LORA), tile(512), tile(LANES),
                                    full(w_in_r.shape), tile(D_MODEL), tile(D_MODEL), full(vec), full(vec)],
        out_specs=[tile(IN_COLS_R), tile(D_MODEL)] + [full(vec)] * 3,
        compiler_params=pltpu.CompilerParams(dimension_semantics=("arbitrary",), vmem_limit_bytes=VMEM_BIG),
    )(dq_sb, dk_sb, dv_sb, dg_sb, dcq, dckv, dg_mla, dkr, w_in_r, x, dy, norm_w, scale)


def _grad_w_in(hb, dproj):
    s_len = hb.shape[0]
    n_half = IN_COLS_R // 2

    def body(h_ref, d_ref, g_ref):
        @pl.when(pl.program_id(1) == 0)
        def _():
            g_ref[...] = jnp.zeros_like(g_ref)

        g_ref[...] += _dot_tn(h_ref[...], d_ref[...])

    return pl.pallas_call(
        body, name="grad_w_in", grid=(2, s_len // TN_S),
        out_shape=jax.ShapeDtypeStruct((D_MODEL, IN_COLS_R), F32),
        in_specs=[pl.BlockSpec((TN_S, D_MODEL), lambda n, s: (s, 0)),
                  pl.BlockSpec((TN_S, n_half), lambda n, s: (s, n))],
        out_specs=pl.BlockSpec((D_MODEL, n_half), lambda n, s: (0, n)),
        compiler_params=pltpu.CompilerParams(dimension_semantics=("parallel", "arbitrary"),
                                             vmem_limit_bytes=VMEM_BIG),
    )(hb, dproj)


def _vec_exchange(gpack, ccol, wpack, mpack, vpack, w_ada, m_ada, v_ada):
    n_sh = w_ada.shape[1]

    def body(g_ref, cc_ref, wp_ref, mp_ref, vp_ref, wa_ref, ma_ref, va_ref,
             og_ref, od_ref, om_ref, ov_ref, ag_ref, ad_ref, am_ref, av_ref,
             gall_ref, call_ref, ssem, rsem):
        pos = _mesh_pos()
        me = _lin(pos)
        gall_ref[me] = g_ref[...]
        call_ref[me] = cc_ref[...]
        _all_gather(pos, g_ref, gall_ref, ssem, rsem, 0)
        _all_gather(pos, cc_ref, call_ref, ssem, rsem, N_DEV - 1)

        tot = gall_ref[0]
        for j in range(1, N_DEV):
            tot = tot + gall_ref[j]
        og_ref[...] = tot
        od_ref[...], om_ref[...], ov_ref[...] = _adamw(wp_ref[...], tot, mp_ref[...], vp_ref[...])

        ga = jnp.zeros((D_MODEL, n_sh), F32)
        for j in range(N_DEV):
            d_mine = jnp.zeros((8, n_sh), F32)
            for k in range(N_DEV):
                d_mine = d_mine + jnp.where(me == k, gall_ref[j, :, PK_ADA + n_sh * k:PK_ADA + n_sh * (k + 1)], 0.0)
            col = _silu(call_ref[j])
            ga = ga + jnp.concatenate(
                [col * d_mine[0:1, LANES * a:LANES * (a + 1)] for a in range(n_sh // LANES)], axis=1)
        ag_ref[...] = ga
        ad_ref[...], am_ref[...], av_ref[...] = _adamw(wa_ref[...], ga, ma_ref[...], va_ref[...])

    pk = jax.ShapeDtypeStruct((8, PK_END), F32)
    ada = jax.ShapeDtypeStruct((D_MODEL, n_sh), F32)
    return pl.pallas_call(
        body, name="vec_exchange",
        out_shape=[pk] * 4 + [ada] * 4,
        in_specs=[_vmem_spec()] * 8, out_specs=[_vmem_spec()] * 8,
        scratch_shapes=[
            pltpu.VMEM((N_DEV, 8, PK_END), F32),
            pltpu.VMEM((N_DEV, D_MODEL, LANES), F32),
            pltpu.SemaphoreType.DMA((2 * (N_DEV - 1),)),
            pltpu.SemaphoreType.DMA((2 * (N_DEV - 1),)),
        ],
        compiler_params=pltpu.CompilerParams(vmem_limit_bytes=VMEM_BIG),
    )(gpack, ccol, wpack, mpack, vpack, w_ada, m_ada, v_ada)


def _grad_exchange(grads):
    n = len(grads)

    def body(*refs):
        ins, outs = refs[:n], refs[n:2 * n]
        ssem, rsem, lsem = refs[2 * n], refs[2 * n + 1], refs[2 * n + 2]
        pos = _mesh_pos()
        me = _lin(pos)
        own = [pltpu.make_async_copy(ins[a].at[me], outs[a].at[me], lsem.at[a]) for a in range(n)]
        for cp in own:
            cp.start()
        for a in range(n):
            _all_to_all(pos, ins[a], outs[a], ssem, rsem, a * (N_DEV - 1))
        for cp in own:
            cp.wait()

    return pl.pallas_call(
        body, name="grad_exchange",
        out_shape=[jax.ShapeDtypeStruct(g.shape, g.dtype) for g in grads],
        in_specs=[_any_spec()] * n, out_specs=[_any_spec()] * n,
        scratch_shapes=[
            pltpu.SemaphoreType.DMA((n * (N_DEV - 1),)),
            pltpu.SemaphoreType.DMA((n * (N_DEV - 1),)),
            pltpu.SemaphoreType.DMA((n,)),
        ],
    )(*grads)


def _adamw_reduce(name, parts, w, m, v, row_tile):
    rows, cols = w.shape

    def body(p_ref, w_ref, m_ref, v_ref, g_ref, d_ref, mo_ref, vo_ref):
        g = p_ref[0].astype(F32)
        for j in range(1, N_DEV):
            g = g + p_ref[j].astype(F32)
        g_ref[...] = g
        d_ref[...], mo_ref[...], vo_ref[...] = _adamw(w_ref[...], g, m_ref[...], v_ref[...])

    tile = pl.BlockSpec((row_tile, cols), lambda i: (i, 0))
    return pl.pallas_call(
        body, name=name, grid=(rows // row_tile,),
        out_shape=[jax.ShapeDtypeStruct((rows, cols), F32)] * 4,
        in_specs=[pl.BlockSpec((N_DEV, row_tile, cols), lambda i: (0, i, 0)), tile, tile, tile],
        out_specs=[tile] * 4,
        compiler_params=pltpu.CompilerParams(dimension_semantics=("parallel",), vmem_limit_bytes=VMEM_BIG),
    )(parts, w, m, v)


def _rope_tables(positions):
    inv_freq = 10000.0 ** (-jnp.arange(0, ROPE, 2, dtype=F32) / ROPE)
    ang = positions.astype(F32)[:, None] * inv_freq
    cos, sin = jnp.cos(ang), jnp.sin(ang)
    s_len = positions.shape[0]
    ones = jnp.ones((s_len, NOPE), F32)
    zeros = jnp.zeros((s_len, NOPE), F32)
    z16 = jnp.zeros((s_len, ROPE // 2), F32)
    pad1 = jnp.ones((s_len, LANES - MLA_QK), F32)
    pad0 = jnp.zeros((s_len, LANES - MLA_QK), F32)
    cosf = jnp.concatenate([ones, cos, cos, pad1], axis=1)
    sin_a = jnp.concatenate([zeros, -sin, z16, pad0], axis=1)
    sin_b = jnp.concatenate([zeros, z16, sin, pad0], axis=1)
    return cosf, sin_a, sin_b


def _rearrange_cols(w):
    pad = jnp.zeros((w.shape[0], IN_COLS_R - IN_COLS), w.dtype)
    return jnp.concatenate([w[:, :2688], w[:, 2720:3232], w[:, 2688:2720], pad], axis=1)


def _restore_cols(g):
    return jnp.concatenate([g[:, :2688], g[:, 3200:3232], g[:, 2688:3200]], axis=1)


def _pad_heads(w):
    rows = w.shape[0]
    w = w.reshape(rows, HEADS, MLA_QK)
    return jnp.pad(w, ((0, 0), (0, 0), (0, LANES - MLA_QK))).reshape(rows, HEADS * LANES)


def _unpad_heads(g):
    rows = g.shape[0]
    return g.reshape(rows, HEADS, LANES)[:, :, :MLA_QK].reshape(rows, HEADS * MLA_QK)


def _pad_lanes(v):
    return jnp.pad(v, ((0, 0), (0, LANES - v.shape[1])))


def _col_shards(g):
    rows = g.shape[0]
    return g.reshape(rows, N_DEV, g.shape[1] // N_DEV).transpose(1, 0, 2)


def _from_col_shards(g):
    return g.transpose(1, 0, 2).reshape(g.shape[1], N_DEV * g.shape[2])


def _pack(norm_w, qln, kvln, qhn, khn, ada, loss_lanes=None):
    if loss_lanes is None:
        loss_lanes = jnp.zeros((1, PK_END - PK_LOSS), F32)
    row = jnp.concatenate([norm_w, qln, kvln, _pad_lanes(qhn), _pad_lanes(khn), ada, loss_lanes], axis=1)
    return jnp.broadcast_to(row, (8, PK_END))


def _unpack(p):
    row = p[0:1]
    return (row[:, PK_NORM:PK_QLN], row[:, PK_QLN:PK_KVLN], row[:, PK_KVLN:PK_QHN],
            row[:, PK_QHN:PK_QHN + MLA_QK], row[:, PK_KHN:PK_KHN + MLA_QK], row[:, PK_ADA:PK_LOSS])


def kernel(x, c, positions, w_ada, b_ada, norm_w, w_in, q_lora_norm, w_uq, kv_lora_norm, w_ukv, q_head_norm, k_head_norm, w_out, loss_target, m_w_ada, m_b_ada, m_norm_w, m_w_in, m_q_lora_norm, m_w_uq, m_kv_lora_norm, m_w_ukv, m_q_head_norm, m_k_head_norm, m_w_out, v_w_ada, v_b_ada, v_norm_w, v_w_in, v_q_lora_norm, v_w_uq, v_kv_lora_norm, v_w_ukv, v_q_head_norm, v_k_head_norm, v_w_out):
    s_len = x.shape[1]
    x2 = x.reshape(s_len, D_MODEL)
    tgt = loss_target.reshape(s_len, D_MODEL)
    w_ada_s, w_in_s, w_uq_s, w_ukv_s, w_out_s = w_ada[0], w_in[0], w_uq[0], w_ukv[0], w_out[0]

    ada8 = _ada_fwd(jnp.broadcast_to(c, (8, D_MODEL)), w_ada_s, b_ada.reshape(N_DEV, -1))
    ada = ada8.reshape(1, 3 * D_MODEL)
    shift, scale, gate = ada[:, :D_MODEL], ada[:, D_MODEL:2 * D_MODEL], ada[:, 2 * D_MODEL:]

    g_in, g_uq, g_ukv, g_out = _gather_weights([w_in_s, w_uq_s, w_ukv_s, w_out_s])
    w_in_r = _rearrange_cols(_from_col_shards(g_in))
    wuq_p = _pad_heads(_from_col_shards(g_uq))
    wukv_f = _from_col_shards(g_ukv)
    w_out_f = g_out.reshape(D_MODEL, D_MODEL)

    cosf, sin_a, sin_b = _rope_tables(positions[0])
    qhn_p, khn_p = _pad_lanes(q_head_norm), _pad_lanes(k_head_norm)

    hb, qkv, g_sb, cq, ckv, g_mla, kr, qm, km, vm = _fwd_pre(
        x2, shift, scale, norm_w, w_in_r, q_lora_norm, wuq_p, kv_lora_norm, wukv_f, qhn_p, khn_p,
        cosf, sin_a, sin_b)
    o_sb, w_saved, l_saved = _sb_fwd(qkv)
    o_mla, lse = _mla_fwd(qm, km, vm)

    dy, do_sb, dg_sb, do_mla, dg_mla, gw_out, d_gate, loss_acc = _mid(
        o_sb, g_sb, o_mla, g_mla, x2, tgt, gate, w_out_f)

    dq_sb, dk_sb_t, dv_sb_t = _sb_bwd(qkv, qkv[:, :SB_WIDTH].T, do_sb, do_sb.T, w_saved, l_saved)
    dk_sb, dv_sb = _from_key_tiles(dk_sb_t), _from_key_tiles(dv_sb_t)
    dq_m, dk_m_t, dv_m_t = _mla_bwd(qm, qm.T, km, vm, o_mla, do_mla, do_mla.T, lse)
    dk_m, dv_m = _from_key_tiles(dk_m_t), _from_key_tiles(dv_m_t)
    dcq, dckv, dkr, gw_uq_p, gw_ukv, g_qhn, g_khn, g_qln, g_kvln = _mla_pre_bwd(
        dq_m, dk_m, dv_m, cq, ckv, kr, q_lora_norm, wuq_p, kv_lora_norm, wukv_f, qhn_p, khn_p,
        cosf, sin_a, sin_b)
    dproj, grad_x, d_shift, d_scale, g_norm_w = _dproj_bwd(
        dq_sb, dk_sb, dv_sb, dg_sb, dcq, dckv, dg_mla, dkr, w_in_r, x2, dy, norm_w, scale)
    gw_in = _restore_cols(_grad_w_in(hb, dproj))

    d_ada = jnp.concatenate([d_shift, d_scale, d_gate], axis=1)
    gpack = _pack(g_norm_w, g_qln, g_kvln, g_qhn[:, :MLA_QK], g_khn[:, :MLA_QK], d_ada, loss_acc)
    wpack = _pack(norm_w, q_lora_norm, kv_lora_norm, q_head_norm, k_head_norm, b_ada)
    mpack = _pack(m_norm_w, m_q_lora_norm, m_kv_lora_norm, m_q_head_norm, m_k_head_norm, m_b_ada)
    vpack = _pack(v_norm_w, v_q_lora_norm, v_kv_lora_norm, v_q_head_norm, v_k_head_norm, v_b_ada)
    ccol = jnp.broadcast_to(c.reshape(D_MODEL, 1), (D_MODEL, LANES))
    pg, pd, pm, pv, ada_g, ada_d, ada_m, ada_v = _vec_exchange(
        gpack, ccol, wpack, mpack, vpack, w_ada_s, m_w_ada[0], v_w_ada[0])
    loss = 0.5 * jnp.sum(pg[0, PK_LOSS:PK_END]) / D_MODEL

    r_in, r_uq, r_ukv, r_out = _grad_exchange([g.astype(BF16) for g in (
        _col_shards(gw_in), _col_shards(_unpad_heads(gw_uq_p)), _col_shards(gw_ukv),
        gw_out.reshape(N_DEV, D_MODEL // N_DEV, D_MODEL))])
    in_g, in_d, in_m, in_v = _adamw_reduce("adamw_w_in", r_in, w_in_s, m_w_in[0], v_w_in[0], 256)
    uq_g, uq_d, uq_m, uq_v = _adamw_reduce("adamw_w_uq", r_uq, w_uq_s, m_w_uq[0], v_w_uq[0], w_uq_s.shape[0])
    ukv_g, ukv_d, ukv_m, ukv_v = _adamw_reduce(
        "adamw_w_ukv", r_ukv, w_ukv_s, m_w_ukv[0], v_w_ukv[0], w_ukv_s.shape[0])
    out_g, out_d, out_m, out_v = _adamw_reduce(
        "adamw_w_out", r_out, w_out_s, m_w_out[0], v_w_out[0], w_out_s.shape[0])

    def group(ada_t, pk, in_t, uq_t, ukv_t, out_t):
        nw, qln, kvln, qhn, khn, b = _unpack(pk)
        return (ada_t[None], b, nw, in_t[None], qln, uq_t[None], kvln, ukv_t[None], qhn, khn, out_t[None])

    return (loss, grad_x.reshape(1, s_len, D_MODEL),
            *group(ada_g, pg, in_g, uq_g, ukv_g, out_g),
            *group(ada_d, pd, in_d, uq_d, ukv_d, out_d),
            *group(ada_m, pm, in_m, uq_m, ukv_m, out_m),
            *group(ada_v, pv, in_v, uq_v, ukv_v, out_v))
```

```python
import functools
import math

import jax
import jax.numpy as jnp
from jax import lax
from jax.experimental import pallas as pl
from jax.experimental.pallas import tpu as pltpu

F32 = jnp.float32
BF16 = jnp.bfloat16

N_DEV = 8
D_MODEL = 1024
HEADS = 8
SB_WIDTH = 512
MLA_WIDTH = 512
Q_LORA = 384
KV_LORA = 256
ROPE = 32
NOPE = 64
MLA_QK = 96
LANES = 128
IN_COLS = 3232
IN_COLS_R = 3328
EPS = 1e-6
NEG = -1e30

ADAM_LR = 0.001
ADAM_B1 = 0.9
ADAM_B2 = 0.999
ADAM_EPS = 1e-08
ADAM_WD = 0.01
ADAM_STEP = 10

TS = 512
TS_FWD = 256
TQ = 512
KEY_UNROLL = 2
TN_S = 512
VMEM_BIG = 56 * 1024 * 1024

PK_NORM, PK_QLN, PK_KVLN, PK_QHN, PK_KHN, PK_ADA, PK_LOSS, PK_END = 0, 1024, 1408, 1664, 1792, 1920, 4992, 6016

MESH_ID = pl.DeviceIdType.MESH


def _dot_nn(a, b):
    return lax.dot_general(a, b, (((1,), (0,)), ((), ())), preferred_element_type=F32)


def _dot_nt(a, b):
    return lax.dot_general(a, b, (((1,), (1,)), ((), ())), preferred_element_type=F32)


def _dot_tn(a, b):
    return lax.dot_general(a, b, (((0,), (0,)), ((), ())), preferred_element_type=F32)


def _split_bf16(a):
    hi = a.astype(BF16)
    lo = (a - hi.astype(F32)).astype(BF16)
    return hi, lo


def _dot3(a, b):
    ah, al = _split_bf16(a)
    bh, bl = _split_bf16(b)
    return _dot_nn(ah, bh) + _dot_nn(ah, bl) + _dot_nn(al, bh)


def _sigmoid(g):
    return 1.0 / (1.0 + jnp.exp(-g))


def _silu(g):
    return g * _sigmoid(g)


def _lane_iota(shape):
    return lax.broadcasted_iota(jnp.int32, shape, len(shape) - 1)


def _adamw(w, g, m, v):
    m = ADAM_B1 * m + (1.0 - ADAM_B1) * g
    v = ADAM_B2 * v + (1.0 - ADAM_B2) * (g * g)
    m_hat = m / (1.0 - ADAM_B1 ** ADAM_STEP)
    v_hat = v / (1.0 - ADAM_B2 ** ADAM_STEP)
    delta = -ADAM_LR * (m_hat / (jnp.sqrt(v_hat) + ADAM_EPS) + ADAM_WD * w)
    return delta, m, v


def _mesh_pos():
    return lax.axis_index("x"), lax.axis_index("y"), lax.axis_index("c")


def _peer(pos, k):
    x, y, c = pos
    return (1 - x if k & 4 else x, 1 - y if k & 2 else y, 1 - c if k & 1 else c)


def _lin(pos):
    return 4 * pos[0] + 2 * pos[1] + pos[2]


def _remote(src, dst, send_sems, recv_sems, idx, peer):
    return pltpu.make_async_remote_copy(
        src_ref=src, dst_ref=dst, send_sem=send_sems.at[idx], recv_sem=recv_sems.at[idx],
        device_id=peer, device_id_type=MESH_ID)


def _all_gather(pos, src, buf, send_sems, recv_sems, base):
    me = _lin(pos)
    sent = []
    for k in range(1, N_DEV):
        cp = _remote(src, buf.at[me], send_sems, recv_sems, base + k - 1, _peer(pos, k))
        cp.start()
        sent.append(cp)
    for k in range(1, N_DEV):
        peer = _peer(pos, k)
        _remote(src, buf.at[_lin(peer)], send_sems, recv_sems, base + k - 1, peer).wait_recv()
    for cp in sent:
        cp.wait_send()


def _all_to_all_start(pos, src, buf, send_sems, recv_sems, base):
    me = _lin(pos)
    sent = []
    for k in range(1, N_DEV):
        peer = _peer(pos, k)
        cp = _remote(src.at[_lin(peer)], buf.at[me], send_sems, recv_sems, base + k - 1, peer)
        cp.start()
        sent.append(cp)
    return sent


def _all_to_all_wait(pos, src, buf, send_sems, recv_sems, base, sent):
    me = _lin(pos)
    for k in range(1, N_DEV):
        peer = _peer(pos, k)
        _remote(src.at[me], buf.at[_lin(peer)], send_sems, recv_sems, base + k - 1, peer).wait_recv()
    for cp in sent:
        cp.wait_send()


def _all_to_all(pos, src, buf, send_sems, recv_sems, base):
    sent = _all_to_all_start(pos, src, buf, send_sems, recv_sems, base)
    _all_to_all_wait(pos, src, buf, send_sems, recv_sems, base, sent)


def _two_level_gather(pos, bufs, send_sems, recv_sems):
    x, y, c = pos
    me, sibling = (x, y, c), (x, y, 1 - c)
    chips = [(1 - x, y), (x, 1 - y), (1 - x, 1 - y)]

    def copy(a, k, block, to):
        slot = bufs[a].at[_lin(block)]
        return _remote(slot, slot, send_sems, recv_sems, 7 * a + k, to)

    started = []
    for a in range(len(bufs)):
        first = [copy(a, 0, me, sibling)] + [copy(a, 1 + j, me, (*chip, c)) for j, chip in enumerate(chips)]
        for cp in first:
            cp.start()
        started += first
    for a in range(len(bufs)):
        for j, chip in enumerate(chips):
            copy(a, 1 + j, (*chip, c), me).wait_recv()
            passed = copy(a, 4 + j, (*chip, c), sibling)
            passed.start()
            started.append(passed)
    for a in range(len(bufs)):
        copy(a, 0, sibling, me).wait_recv()
        for j, chip in enumerate(chips):
            copy(a, 4 + j, (*chip, 1 - c), me).wait_recv()
    for cp in started:
        cp.wait_send()


def _vmem_spec():
    return pl.BlockSpec(memory_space=pltpu.VMEM)


def _any_spec():
    return pl.BlockSpec(memory_space=pl.ANY)


def _row_select(slots, n):
    r = lax.broadcasted_iota(jnp.int32, (N_DEV, n), 0)
    out = jnp.zeros((N_DEV, n), F32)
    for j in range(N_DEV):
        out = out + jnp.where(r == j, slots[j], 0.0)
    return out


def _ada_fwd(c8, w_ada, b_ada8):
    n_sh = w_ada.shape[1]

    def body(c_ref, w_ref, b_ref, out_ref, call_ref, psend_ref, precv_ref, ssem, rsem):
        pos = _mesh_pos()
        me = _lin(pos)
        call_ref[me] = c_ref[...]
        _all_gather(pos, c_ref, call_ref, ssem, rsem, 0)
        w = w_ref[...]
        for j in range(N_DEV):
            psend_ref[j] = _dot3(_silu(call_ref[j]), w)
        precv_ref[me] = psend_ref[me]
        _all_to_all(pos, psend_ref, precv_ref, ssem, rsem, N_DEV - 1)
        out_ref[...] = _row_select([precv_ref[j] for j in range(N_DEV)], n_sh) + b_ref[...]

    return pl.pallas_call(
        body, name="ada_fwd",
        out_shape=jax.ShapeDtypeStruct((N_DEV, n_sh), F32),
        in_specs=[_vmem_spec()] * 3, out_specs=_vmem_spec(),
        scratch_shapes=[
            pltpu.VMEM((N_DEV, 8, D_MODEL), F32),
            pltpu.VMEM((N_DEV, 8, n_sh), F32),
            pltpu.VMEM((N_DEV, 8, n_sh), F32),
            pltpu.SemaphoreType.DMA((2 * (N_DEV - 1),)),
            pltpu.SemaphoreType.DMA((2 * (N_DEV - 1),)),
        ],
    )(c8, w_ada, b_ada8)


def _gather_weights(shards):
    n = len(shards)

    def body(*refs):
        ins, outs = refs[:n], refs[n:2 * n]
        ssem, rsem = refs[2 * n], refs[2 * n + 1]
        pos = _mesh_pos()
        me = _lin(pos)
        for a in range(n):
            outs[a][me] = ins[a][...].astype(BF16)
        _two_level_gather(pos, outs, ssem, rsem)

    return pl.pallas_call(
        body, name="gather_weights",
        out_shape=[jax.ShapeDtypeStruct((N_DEV,) + s.shape, BF16) for s in shards],
        in_specs=[_vmem_spec()] * n, out_specs=[_vmem_spec()] * n,
        scratch_shapes=[
            pltpu.SemaphoreType.DMA((n * (N_DEV - 1),)),
            pltpu.SemaphoreType.DMA((n * (N_DEV - 1),)),
        ],
        compiler_params=pltpu.CompilerParams(vmem_limit_bytes=VMEM_BIG),
    )(*shards)


def _rope(t, cosf, sin_a, sin_b):
    return t * cosf + pltpu.roll(t, 112, 1) * sin_a + pltpu.roll(t, 16, 1) * sin_b


def _rope_t(d, cosf, sin_a, sin_b):
    return d * cosf + pltpu.roll(d * sin_a, 16, 1) + pltpu.roll(d * sin_b, 112, 1)


def _head_rms(t):
    return lax.rsqrt(jnp.sum(t * t, axis=1, keepdims=True) * (1.0 / MLA_QK) + EPS)


def _rms_bwd(dxhat_w, xhat, r, n):
    return r * (dxhat_w - xhat * (jnp.sum(dxhat_w * xhat, axis=1, keepdims=True) * (1.0 / n)))


def _mla_latents(cq, ckv, kr, qln, kvln, wuq, wukv):
    rq = lax.rsqrt(jnp.mean(cq * cq, axis=1, keepdims=True) + EPS)
    rkv = lax.rsqrt(jnp.mean(ckv * ckv, axis=1, keepdims=True) + EPS)
    cq_hat = cq * rq
    ckv_hat = ckv * rkv
    cqn = (cq_hat * qln).astype(BF16)
    ckvn = (ckv_hat * kvln).astype(BF16)
    q_all = _dot_nn(cqn, wuq)
    kv = _dot_nn(ckvn, wukv)
    kr64 = pltpu.roll(kr, 64, 1)
    return rq, rkv, cq_hat, ckv_hat, cqn, ckvn, q_all, kv, kr64


def _fwd_pre(x, shift, scale, norm_w, w_in_r, qln, wuq, kvln, wukv, qhn, khn, cosf, sin_a, sin_b):
    s_len = x.shape[0]

    def body(x_ref, shift_ref, scale_ref, nw_ref, w_ref, qln_ref, wuq_ref, kvln_ref, wukv_ref,
             qhn_ref, khn_ref, cos_ref, sa_ref, sb_ref,
             hb_ref, qkv_ref, gsb_ref, cq_ref, ckv_ref, gmla_ref, kr_ref, qm_ref, km_ref, vm_ref):
        xv = x_ref[...]
        r = lax.rsqrt(jnp.mean(xv * xv, axis=1, keepdims=True) + EPS)
        h = (xv * r) * nw_ref[...] * (1.0 + scale_ref[...]) + shift_ref[...]
        hb = h.astype(BF16)
        hb_ref[...] = hb
        qkv_ref[...] = _dot_nn(hb, w_ref[:, 0:1536]).astype(BF16)
        gsb_ref[...] = _dot_nn(hb, w_ref[:, 1536:2048])
        cq = _dot_nn(hb, w_ref[:, 2048:2432])
        ckv = _dot_nn(hb, w_ref[:, 2432:2688])
        gmla_ref[...] = _dot_nn(hb, w_ref[:, 2688:3200])
        kr = _dot_nn(hb, w_ref[:, 3200:3328])
        cq_ref[...] = cq
        ckv_ref[...] = ckv
        kr_ref[...] = kr
        _, _, _, _, _, _, q_all, kv, kr64 = _mla_latents(
            cq, ckv, kr, qln_ref[...], kvln_ref[...], wuq_ref[...], wukv_ref[...])
        cosf, sa, sb = cos_ref[...], sa_ref[...], sb_ref[...]
        qhn_v, khn_v = qhn_ref[...], khn_ref[...]
        low = _lane_iota((TS_FWD, LANES)) < NOPE
        for hd in range(HEADS):
            blk = slice(LANES * hd, LANES * (hd + 1))
            qb = q_all[:, blk]
            qm_ref[:, blk] = _rope(qb * _head_rms(qb) * qhn_v, cosf, sa, sb).astype(BF16)
            kb = jnp.where(low, kv[:, blk], kr64)
            km_ref[:, blk] = _rope(kb * _head_rms(kb) * khn_v, cosf, sa, sb).astype(BF16)
        for p in range(HEADS // 2):
            even = kv[:, LANES * 2 * p:LANES * (2 * p + 1)]
            odd = kv[:, LANES * (2 * p + 1):LANES * (2 * p + 2)]
            vm_ref[:, LANES * p:LANES * (p + 1)] = jnp.where(low, pltpu.roll(even, 64, 1), odd).astype(BF16)

    def tile(width):
        return pl.BlockSpec((TS_FWD, width), lambda i: (i, 0))

    def full(a):
        return pl.BlockSpec(a.shape, lambda i: (0, 0))

    out_widths = [(D_MODEL, BF16), (1536, BF16), (512, F32), (Q_LORA, F32), (KV_LORA, F32),
                  (512, F32), (LANES, F32), (1024, BF16), (1024, BF16), (512, BF16)]
    return pl.pallas_call(
        body, name="fwd_pre", grid=(s_len // TS_FWD,),
        out_shape=[jax.ShapeDtypeStruct((s_len, w), dt) for w, dt in out_widths],
        in_specs=[tile(D_MODEL), full(shift), full(scale), full(norm_w), full(w_in_r), full(qln), full(wuq),
                  full(kvln), full(wukv), full(qhn), full(khn), tile(LANES), tile(LANES), tile(LANES)],
        out_specs=[tile(w) for w, _ in out_widths],
        compiler_params=pltpu.CompilerParams(dimension_semantics=("parallel",), vmem_limit_bytes=VMEM_BIG),
    )(x, shift, scale, norm_w, w_in_r, qln, wuq, kvln, wukv, qhn, khn, cosf, sin_a, sin_b)


CUM_W = 256


def _tri(strict):
    j = lax.broadcasted_iota(jnp.int32, (CUM_W, CUM_W), 0)
    s = lax.broadcasted_iota(jnp.int32, (CUM_W, CUM_W), 1)
    return (j > s if strict else j >= s).astype(BF16)


def _suffix_sums(a, a_bf16, tri_m, carry):
    n = a.shape[1] // CUM_W
    outs = [None] * n
    for i in reversed(range(n)):
        cols = slice(CUM_W * i, CUM_W * (i + 1))
        outs[i] = _dot_nn(a_bf16[:, cols], tri_m) + carry
        carry = carry + _rowsum(a[:, cols])
    return (outs[0] if n == 1 else jnp.concatenate(outs, axis=1)), carry


def _sb_weights(qm, kb, carry, tri_u, diag):
    z = _dot_nt(qm, kb)
    nz = -z
    lk = jnp.minimum(nz, 0.0) - jnp.log(1.0 + jnp.exp(jnp.minimum(z, nz)))
    if diag:
        t = lax.broadcasted_iota(jnp.int32, (TQ, TQ), 0)
        s = lax.broadcasted_iota(jnp.int32, (TQ, TQ), 1)
        valid = s < t
        lk = jnp.where(valid, lk, 0.0)
    lk_hi = lk.astype(BF16)
    after, carry = _suffix_sums(lk, lk_hi, tri_u, carry)
    logw = z + lk + after
    if diag:
        logw = jnp.where(valid, logw, NEG)
    return lk_hi, jnp.exp(logw), carry


SB_SCALE = 0.125


def _head_masks():
    lane = _lane_iota((1, LANES))
    return [lane < 64, lane >= 64]


def _masked(hm, a):
    return jnp.where(hm, a, jnp.zeros_like(a))


def _rowsum(a):
    return jnp.sum(a, axis=1, keepdims=True)


def _key_rows(kj):
    return pl.ds(pl.multiple_of(kj * TQ, TQ), TQ)


def _over_key_tiles(count, fn, st, ascending):
    n_full = count // KEY_UNROLL
    n_rest = count - n_full * KEY_UNROLL

    def group(g, s_):
        return fn([g * KEY_UNROLL + (u if ascending else KEY_UNROLL - 1 - u) for u in range(KEY_UNROLL)], s_)

    if ascending:
        st = lax.fori_loop(0, n_full, group, st)
        return lax.fori_loop(0, n_rest, lambda i, s_: fn([n_full * KEY_UNROLL + i], s_), st)
    st = lax.fori_loop(0, n_rest, lambda i, s_: fn([count - 1 - i], s_), st)
    return lax.fori_loop(0, n_full, lambda i, s_: group(n_full - 1 - i, s_), st)


def _each_tile(block):
    def trip(tiles, st):
        for kj in tiles:
            st = block(_key_rows(kj), st, False)
        return st
    return trip


STAGE_SLOTS = 2 * KEY_UNROLL


def _stage_copies(to_hbm, hbm_refs, scr_refs, sems, pair, qi, kj):
    slot = (qi - kj) % STAGE_SLOTS
    out = []
    for h in range(2):
        for a in range(2):
            hbm, scr = hbm_refs[a].at[2 * pair + h, qi, kj], scr_refs[a].at[slot, h]
            sem = sems.at[4 * slot + 2 * h + a]
            out.append(pltpu.make_async_copy(scr, hbm, sem) if to_hbm else pltpu.make_async_copy(hbm, scr, sem))
    return out


def _sb_fwd(qkv):
    s_len = qkv.shape[0]
    nq = s_len // TQ

    def body(q_ref, k_ref, v_ref, o_ref, w_hbm, l_hbm, w_scr, l_scr, sems):
        pair, qi = pl.program_id(0), pl.program_id(1)
        q = q_ref[...]
        tri_u = _tri(True)
        masks = _head_masks()
        qms = [_masked(hm, q) * SB_SCALE for hm in masks]

        def copies(kj, of_qi=qi):
            return _stage_copies(True, (w_hbm, l_hbm), (w_scr, l_scr), sems, pair, of_qi, kj)

        def drain(of_qi):
            for kj in range(STAGE_SLOTS):
                @pl.when(kj <= of_qi)
                def _():
                    for cp in copies(kj, of_qi):
                        cp.wait()

        def block(kj, st, diag, before_staging=None):
            rows = _key_rows(kj)
            slot = (qi - kj) % STAGE_SLOTS
            kb, vb = k_ref[rows, :], v_ref[rows, :]
            carries, acc = list(st[:2]), st[2]
            staged = []
            for h in range(2):
                lk_hi, w, carries[h] = _sb_weights(qms[h], kb, carries[h], tri_u, diag)
                wb = w.astype(BF16)
                staged.append((wb, lk_hi))
                acc = acc + _dot_nn(wb, _masked(masks[h], vb))
            if before_staging is not None:
                before_staging()
            for h in range(2):
                w_scr[slot, h], l_scr[slot, h] = staged[h]
            return carries[0], carries[1], acc

        def trip(tiles, st):
            for kj in tiles:
                @pl.when(qi - kj >= STAGE_SLOTS)
                def _():
                    for cp in copies(kj + STAGE_SLOTS):
                        cp.wait()
            for kj in tiles:
                st = block(kj, st, False)
            for kj in tiles:
                for cp in copies(kj):
                    cp.start()
            return st

        def drain_previous_step():
            @pl.when(jnp.logical_or(pair > 0, qi > 0))
            def _():
                drain(jnp.where(qi == 0, nq - 1, qi - 1))

        zc = jnp.zeros((TQ, 1), F32)
        st = block(qi, (zc, zc, jnp.zeros((TQ, LANES), F32)), True, drain_previous_step)
        for cp in copies(qi):
            cp.start()
        st = _over_key_tiles(qi, trip, st, ascending=False)
        o_ref[...] = st[2]

        @pl.when(jnp.logical_and(pair == HEADS // 2 - 1, qi == nq - 1))
        def _():
            drain(qi)

    saved = jax.ShapeDtypeStruct((HEADS, nq, nq, TQ, TQ), BF16)
    stage = pltpu.VMEM((STAGE_SLOTS, 2, TQ, TQ), BF16)
    return pl.pallas_call(
        body, name="sb_fwd", grid=(HEADS // 2, nq),
        out_shape=[jax.ShapeDtypeStruct((s_len, SB_WIDTH), F32), saved, saved],
        in_specs=[pl.BlockSpec((TQ, LANES), lambda p, i: (i, p)),
                  pl.BlockSpec((s_len, LANES), lambda p, i: (0, 4 + p)),
                  pl.BlockSpec((s_len, LANES), lambda p, i: (0, 8 + p))],
        out_specs=[pl.BlockSpec((TQ, LANES), lambda p, i: (i, p)), _any_spec(), _any_spec()],
        scratch_shapes=[stage, stage, pltpu.SemaphoreType.DMA((4 * STAGE_SLOTS,))],
        compiler_params=pltpu.CompilerParams(dimension_semantics=("arbitrary", "arbitrary"),
                                             vmem_limit_bytes=VMEM_BIG),
    )(qkv, qkv, qkv)


def _prefix_sums(a, tri_m, carry):
    n = a.shape[1] // CUM_W
    outs = [None] * n
    for i in range(n):
        cols = slice(CUM_W * i, CUM_W * (i + 1))
        outs[i] = _dot_nn(a[:, cols].astype(BF16), tri_m) + carry
        carry = carry + _rowsum(a[:, cols])
    return (outs[0] if n == 1 else jnp.concatenate(outs, axis=1)), carry


def _head_rows(a0, a1):
    sub = lax.broadcasted_iota(jnp.int32, a0.shape, 0)
    return jnp.where(sub < NOPE, a0, a1)


def _sb_bwd(qkv, q_t, do, do_t, w_saved, l_saved):
    s_len = qkv.shape[0]
    nq = s_len // TQ

    def body(qt_ref, k_ref, v_ref, do_ref, dot_ref, w_hbm, l_hbm, dq_ref, dk_ref, dv_ref, w_scr, l_scr, sems):
        pair, qi = pl.program_id(0), pl.program_id(1)

        def copies(kj, of_pair=pair, of_qi=qi):
            return _stage_copies(False, (w_hbm, l_hbm), (w_scr, l_scr), sems, of_pair, of_qi, kj)

        def start_first_tiles(of_pair, of_qi):
            for cp in copies(0, of_pair, of_qi):
                cp.start()

            @pl.when(of_qi >= 1)
            def _():
                for cp in copies(1, of_pair, of_qi):
                    cp.start()

        @pl.when(jnp.logical_and(pair == 0, qi == 0))
        def _():
            start_first_tiles(pair, qi)

        @pl.when(qi == 0)
        def _():
            dk_ref[...] = jnp.zeros_like(dk_ref)
            dv_ref[...] = jnp.zeros_like(dv_ref)

        qt = qt_ref[...] * SB_SCALE
        dot_v = dot_ref[...]
        do_v = do_ref[...]
        j = lax.broadcasted_iota(jnp.int32, (CUM_W, CUM_W), 0)
        s = lax.broadcasted_iota(jnp.int32, (CUM_W, CUM_W), 1)
        tri_before = (j < s).astype(BF16)
        masks = _head_masks()
        doms = [_masked(hm, do_v) for hm in masks]

        def block(kj, st):
            rows = _key_rows(kj)
            slot = (qi - kj) % STAGE_SLOTS
            kb, vb = k_ref[rows, :], v_ref[rows, :]
            carries, dqs = list(st[0:2]), list(st[2:4])
            dk_t, dv_t = [], []
            for h in range(2):
                wb = w_scr[slot, h]
                d_l = _dot_nt(doms[h], vb) * wb.astype(F32)
                before, carries[h] = _prefix_sums(d_l, tri_before, carries[h])
                keep = jnp.exp(l_scr[slot, h].astype(F32))
                dzb = (d_l * keep - before * (1.0 - keep)).astype(BF16)
                dk_t.append(_dot_nn(qt, dzb))
                dv_t.append(_dot_nn(dot_v, wb))
                dqs[h] = dqs[h] + _dot_nn(dzb, kb)
            dk_ref[kj] += _head_rows(*dk_t)
            dv_ref[kj] += _head_rows(*dv_t)
            return (*carries, *dqs)

        def trip(tiles, st):
            for kj in tiles:
                @pl.when(kj + 2 <= qi)
                def _():
                    for cp in copies(kj + 2):
                        cp.start()
            for kj in tiles:
                for cp in copies(kj):
                    cp.wait()
            for kj in tiles:
                st = block(kj, st)
            return st

        zc = jnp.zeros((TQ, 1), F32)
        za = jnp.zeros((TQ, LANES), F32)
        st = _over_key_tiles(qi + 1, trip, (zc, zc, za, za), ascending=True)
        dq_ref[...] = jnp.where(masks[0], st[2], st[3]) * SB_SCALE

        @pl.when(jnp.logical_or(pair < HEADS // 2 - 1, qi < nq - 1))
        def _():
            wraps = qi == nq - 1
            start_first_tiles(jnp.where(wraps, pair + 1, pair), jnp.where(wraps, 0, qi + 1))

    tile = pl.BlockSpec((TQ, LANES), lambda p, i: (i, p))
    tile_t = pl.BlockSpec((LANES, TQ), lambda p, i: (p, i))
    col_t = pl.BlockSpec((nq, LANES, TQ), lambda p, i: (0, p, 0))
    stage = pltpu.VMEM((STAGE_SLOTS, 2, TQ, TQ), BF16)
    key_t = jax.ShapeDtypeStruct((nq, SB_WIDTH, TQ), F32)
    return pl.pallas_call(
        body, name="sb_bwd", grid=(HEADS // 2, nq),
        out_shape=[jax.ShapeDtypeStruct((s_len, SB_WIDTH), F32), key_t, key_t],
        in_specs=[tile_t,
                  pl.BlockSpec((s_len, LANES), lambda p, i: (0, 4 + p)),
                  pl.BlockSpec((s_len, LANES), lambda p, i: (0, 8 + p)),
                  tile, tile_t, _any_spec(), _any_spec()],
        out_specs=[tile, col_t, col_t],
        scratch_shapes=[stage, stage, pltpu.SemaphoreType.DMA((4 * STAGE_SLOTS,))],
        compiler_params=pltpu.CompilerParams(dimension_semantics=("arbitrary", "arbitrary"),
                                             vmem_limit_bytes=VMEM_BIG),
    )(q_t, qkv, qkv, do, do_t, w_saved, l_saved)


def _from_key_tiles(a_t):
    return a_t.transpose(0, 2, 1).reshape(a_t.shape[0] * a_t.shape[2], a_t.shape[1])


MLA_SCALE = 1.0 / math.sqrt(MLA_QK)


def _causal_mask():
    t = lax.broadcasted_iota(jnp.int32, (TQ, TQ), 0)
    s = lax.broadcasted_iota(jnp.int32, (TQ, TQ), 1)
    return s <= t


def _head_lanes(h):
    return slice(LANES * h, LANES * (h + 1))


def _mla_fwd(qm, km, vm):
    s_len = qm.shape[0]

    def body(q_ref, k_ref, v_ref, o_ref, lse_ref):
        qi = pl.program_id(1)
        masks = _head_masks()
        qhs = [q_ref[:, _head_lanes(h)] for h in range(2)]

        def block(rows, st, diag):
            vb = v_ref[rows, :]
            ms, ls, acc = list(st[0:2]), list(st[2:4]), st[4]
            alphas, pvs = [], []
            for h in range(2):
                s = _dot_nt(qhs[h], k_ref[rows, _head_lanes(h)]) * MLA_SCALE
                if diag:
                    s = jnp.where(_causal_mask(), s, NEG)
                m_new = jnp.maximum(ms[h], jnp.max(s, axis=1, keepdims=True))
                p = jnp.exp(s - m_new)
                alphas.append(jnp.exp(ms[h] - m_new))
                ls[h] = alphas[h] * ls[h] + _rowsum(p)
                ms[h] = m_new
                pvs.append(_dot_nn(p.astype(BF16), _masked(masks[h], vb)))
            acc = jnp.where(masks[0], alphas[0], alphas[1]) * acc + pvs[0] + pvs[1]
            return (*ms, *ls, acc)

        neg = jnp.full((TQ, 1), NEG, F32)
        zc = jnp.zeros((TQ, 1), F32)
        st = (neg, neg, zc, zc, jnp.zeros((TQ, LANES), F32))
        st = _over_key_tiles(qi, _each_tile(block), st, ascending=True)
        m0, m1, l0, l1, acc = block(_key_rows(qi), st, True)
        o_ref[...] = acc / jnp.where(masks[0], l0, l1)
        lse_ref[0] = m0 + jnp.log(l0)
        lse_ref[1] = m1 + jnp.log(l1)

    return pl.pallas_call(
        body, name="mla_fwd", grid=(HEADS // 2, s_len // TQ),
        out_shape=[jax.ShapeDtypeStruct((s_len, MLA_WIDTH), F32),
                   jax.ShapeDtypeStruct((HEADS, s_len, 1), F32)],
        in_specs=[pl.BlockSpec((TQ, 2 * LANES), lambda p, i: (i, p)),
                  pl.BlockSpec((s_len, 2 * LANES), lambda p, i: (0, p)),
                  pl.BlockSpec((s_len, LANES), lambda p, i: (0, p))],
        out_specs=[pl.BlockSpec((TQ, LANES), lambda p, i: (i, p)),
                   pl.BlockSpec((2, TQ, 1), lambda p, i: (p, i, 0))],
        compiler_params=pltpu.CompilerParams(dimension_semantics=("parallel", "parallel"),
                                             vmem_limit_bytes=VMEM_BIG),
    )(qm, km, vm)


def _mla_bwd(qm, qm_t, km, vm, o, do, do_t, lse):
    s_len = qm.shape[0]

    nq = s_len // TQ

    def body(q_ref, qt_ref, k_ref, v_ref, o_ref, do_ref, dot_ref, lse_ref, dq_ref, dk_ref, dv_ref):
        qi = pl.program_id(1)

        @pl.when(qi == 0)
        def _():
            dk_ref[...] = jnp.zeros_like(dk_ref)
            dv_ref[...] = jnp.zeros_like(dv_ref)

        do_v = do_ref[...]
        dot_v = dot_ref[...]
        od = o_ref[...] * do_v.astype(F32)
        masks = _head_masks()
        qhs = [q_ref[:, _head_lanes(h)] for h in range(2)]
        qts = [qt_ref[_head_lanes(h), :] for h in range(2)]
        doms = [_masked(hm, do_v) for hm in masks]
        deltas = [_rowsum(jnp.where(hm, od, 0.0)) for hm in masks]
        lses = [lse_ref[h] for h in range(2)]

        def block(kj, dqs, diag):
            rows = _key_rows(kj)
            vb = v_ref[rows, :]
            dqs = list(dqs)
            dv_t = []
            for h in range(2):
                kb = k_ref[rows, _head_lanes(h)]
                s = _dot_nt(qhs[h], kb) * MLA_SCALE
                if diag:
                    s = jnp.where(_causal_mask(), s, NEG)
                p = jnp.exp(s - lses[h])
                dp = _dot_nt(doms[h], vb)
                ds = (p * (dp - deltas[h]) * MLA_SCALE).astype(BF16)
                dk_ref[kj, _head_lanes(h), :] += _dot_nn(qts[h], ds)
                dv_t.append(_dot_nn(dot_v, p.astype(BF16)))
                dqs[h] = dqs[h] + _dot_nn(ds, kb)
            dv_ref[kj] += _head_rows(*dv_t)
            return tuple(dqs)

        def trip(tiles, dqs):
            for kj in tiles:
                dqs = block(kj, dqs, False)
            return dqs

        za = jnp.zeros((TQ, LANES), F32)
        dqs = _over_key_tiles(qi, trip, (za, za), ascending=True)
        dqs = block(qi, dqs, True)
        dq_ref[:, _head_lanes(0)] = dqs[0]
        dq_ref[:, _head_lanes(1)] = dqs[1]

    return pl.pallas_call(
        body, name="mla_bwd", grid=(HEADS // 2, nq),
        out_shape=[jax.ShapeDtypeStruct((s_len, HEADS * LANES), F32),
                   jax.ShapeDtypeStruct((nq, HEADS * LANES, TQ), F32),
                   jax.ShapeDtypeStruct((nq, MLA_WIDTH, TQ), F32)],
        in_specs=[pl.BlockSpec((TQ, 2 * LANES), lambda p, i: (i, p)),
                  pl.BlockSpec((2 * LANES, TQ), lambda p, i: (p, i)),
                  pl.BlockSpec((s_len, 2 * LANES), lambda p, i: (0, p)),
                  pl.BlockSpec((s_len, LANES), lambda p, i: (0, p)),
                  pl.BlockSpec((TQ, LANES), lambda p, i: (i, p)),
                  pl.BlockSpec((TQ, LANES), lambda p, i: (i, p)),
                  pl.BlockSpec((LANES, TQ), lambda p, i: (p, i)),
                  pl.BlockSpec((2, TQ, 1), lambda p, i: (p, i, 0))],
        out_specs=[pl.BlockSpec((TQ, 2 * LANES), lambda p, i: (i, p)),
                   pl.BlockSpec((nq, 2 * LANES, TQ), lambda p, i: (0, p, 0)),
                   pl.BlockSpec((nq, LANES, TQ), lambda p, i: (0, p, 0))],
        compiler_params=pltpu.CompilerParams(dimension_semantics=("parallel", "arbitrary"),
                                             vmem_limit_bytes=VMEM_BIG),
    )(qm, qm_t, km, vm, o, do, do_t, lse)


def _mid(o_sb, g_sb, o_mla, g_mla, x, target, gate, w_out):
    s_len = x.shape[0]

    def body(osb_ref, gsb_ref, omla_ref, gmla_ref, x_ref, t_ref, gate_ref, w_ref,
             dy_ref, dosb_ref, dgsb_ref, domla_ref, dgmla_ref, gw_ref, dgate_ref, loss_ref):
        @pl.when(pl.program_id(0) == 0)
        def _():
            gw_ref[...] = jnp.zeros_like(gw_ref)
            dgate_ref[...] = jnp.zeros_like(dgate_ref)
            loss_ref[...] = jnp.zeros_like(loss_ref)

        o1, g1, o2, g2 = osb_ref[...], gsb_ref[...], omla_ref[...], gmla_ref[...]
        s1, s2 = _sigmoid(g1), _sigmoid(g2)
        mixed = jnp.concatenate([o1 * (g1 * s1), o2 * (g2 * s2)], axis=1).astype(BF16)
        w = w_ref[...]
        gate_v = gate_ref[...]
        u = _dot_nn(mixed, w)
        err = x_ref[...] + gate_v * u - t_ref[...]
        loss_ref[...] += jnp.sum(err * err, axis=0, keepdims=True)
        dy = err * (1.0 / D_MODEL)
        dy_ref[...] = dy
        dgate_ref[...] += jnp.sum(dy * u, axis=0, keepdims=True)
        du = (dy * gate_v).astype(BF16)
        gw_ref[...] += _dot_tn(mixed, du)
        dmixed = _dot_nt(du, w)
        d1, d2 = dmixed[:, :SB_WIDTH], dmixed[:, SB_WIDTH:]
        dosb_ref[...] = (d1 * (g1 * s1)).astype(BF16)
        dgsb_ref[...] = (d1 * o1 * (s1 * (1.0 + g1 * (1.0 - s1)))).astype(BF16)
        domla_ref[...] = (d2 * (g2 * s2)).astype(BF16)
        dgmla_ref[...] = (d2 * o2 * (s2 * (1.0 + g2 * (1.0 - s2)))).astype(BF16)

    def tile(width):
        return pl.BlockSpec((TS, width), lambda i: (i, 0))

    def full(shape):
        return pl.BlockSpec(shape, lambda i: (0, 0))

    return pl.pallas_call(
        body, name="mid", grid=(s_len // TS,),
        out_shape=[jax.ShapeDtypeStruct((s_len, D_MODEL), F32)]
        + [jax.ShapeDtypeStruct((s_len, 512), BF16)] * 4
        + [jax.ShapeDtypeStruct((D_MODEL, D_MODEL), F32),
           jax.ShapeDtypeStruct((1, D_MODEL), F32), jax.ShapeDtypeStruct((1, D_MODEL), F32)],
        in_specs=[tile(512)] * 4 + [tile(D_MODEL), tile(D_MODEL), full((1, D_MODEL)), full((D_MODEL, D_MODEL))],
        out_specs=[tile(D_MODEL)] + [tile(512)] * 4
        + [full((D_MODEL, D_MODEL)), full((1, D_MODEL)), full((1, D_MODEL))],
        compiler_params=pltpu.CompilerParams(dimension_semantics=("arbitrary",), vmem_limit_bytes=VMEM_BIG),
    )(o_sb, g_sb, o_mla, g_mla, x, target, gate, w_out)


def _mla_pre_bwd(dq, dk, dv, cq, ckv, kr, qln, wuq, kvln, wukv, qhn, khn, cosf, sin_a, sin_b):
    s_len = cq.shape[0]

    def body(dq_ref, dk_ref, dv_ref, cq_ref, ckv_ref, kr_ref, qln_ref, wuq_ref, kvln_ref, wukv_ref,
             qhn_ref, khn_ref, cos_ref, sa_ref, sb_ref,
             dcq_ref, dckv_ref, dkr_ref, gwuq_ref, gwukv_ref, gqhn_ref, gkhn_ref, gqln_ref, gkvln_ref,
             dqa_ref, dkv_ref):
        @pl.when(pl.program_id(0) == 0)
        def _():
            for r_ in (gwuq_ref, gwukv_ref, gqhn_ref, gkhn_ref, gqln_ref, gkvln_ref):
                r_[...] = jnp.zeros_like(r_)

        cq, ckv = cq_ref[...], ckv_ref[...]
        qln_v, kvln_v = qln_ref[...], kvln_ref[...]
        wuq_v, wukv_v = wuq_ref[...], wukv_ref[...]
        rq, rkv, cq_hat, ckv_hat, cqn, ckvn, q_all, kv, kr64 = _mla_latents(
            cq, ckv, kr_ref[...], qln_v, kvln_v, wuq_v, wukv_v)
        cosf, sa, sb = cos_ref[...], sa_ref[...], sb_ref[...]
        qhn_v, khn_v = qhn_ref[...], khn_ref[...]
        lane = _lane_iota((TS, LANES))
        low = lane < NOPE
        g_qhn = jnp.zeros((1, LANES), F32)
        g_khn = jnp.zeros((1, LANES), F32)
        dkr64 = jnp.zeros((TS, LANES), F32)
        for hd in range(HEADS):
            blk = slice(LANES * hd, LANES * (hd + 1))
            qb = q_all[:, blk]
            r = _head_rms(qb)
            xh = qb * r
            dn = _rope_t(dq_ref[:, blk], cosf, sa, sb)
            g_qhn = g_qhn + jnp.sum(dn * xh, axis=0, keepdims=True)
            dqa_ref[:, blk] = _rms_bwd(dn * qhn_v, xh, r, MLA_QK).astype(BF16)

            kb = jnp.where(low, kv[:, blk], kr64)
            r = _head_rms(kb)
            xh = kb * r
            dn = _rope_t(dk_ref[:, blk], cosf, sa, sb)
            g_khn = g_khn + jnp.sum(dn * xh, axis=0, keepdims=True)
            dkb = _rms_bwd(dn * khn_v, xh, r, MLA_QK)
            dkr64 = dkr64 + jnp.where(low, 0.0, dkb)
            dvp = dv_ref[:, LANES * (hd // 2):LANES * (hd // 2 + 1)]
            dvh = pltpu.roll(dvp, 64, 1) if hd % 2 == 0 else dvp
            dkv_ref[:, blk] = jnp.where(low, dkb, dvh).astype(BF16)
        gqhn_ref[...] += g_qhn
        gkhn_ref[...] += g_khn
        dkr_ref[...] = pltpu.roll(dkr64, 64, 1).astype(BF16)

        dqa = dqa_ref[...]
        gwuq_ref[...] += _dot_tn(cqn, dqa)
        dcqn = _dot_nt(dqa, wuq_v)
        gqln_ref[...] += jnp.sum(dcqn * cq_hat, axis=0, keepdims=True)
        dcq_ref[...] = _rms_bwd(dcqn * qln_v, cq_hat, rq, Q_LORA).astype(BF16)

        dkv = dkv_ref[...]
        gwukv_ref[...] += _dot_tn(ckvn, dkv)
        dckvn = _dot_nt(dkv, wukv_v)
        gkvln_ref[...] += jnp.sum(dckvn * ckv_hat, axis=0, keepdims=True)
        dckv_ref[...] = _rms_bwd(dckvn * kvln_v, ckv_hat, rkv, KV_LORA).astype(BF16)

    def tile(width):
        return pl.BlockSpec((TS, width), lambda i: (i, 0))

    def full(shape):
        return pl.BlockSpec(shape, lambda i: (0, 0))

    acc_shapes = [(Q_LORA, 1024), (KV_LORA, 1024), (1, LANES), (1, LANES), (1, Q_LORA), (1, KV_LORA)]
    return pl.pallas_call(
        body, name="mla_pre_bwd", grid=(s_len // TS,),
        out_shape=[jax.ShapeDtypeStruct((s_len, Q_LORA), BF16), jax.ShapeDtypeStruct((s_len, KV_LORA), BF16),
                   jax.ShapeDtypeStruct((s_len, LANES), BF16)]
        + [jax.ShapeDtypeStruct(s, F32) for s in acc_shapes],
        in_specs=[tile(1024), tile(1024), tile(512), tile(Q_LORA), tile(KV_LORA), tile(LANES),
                  full(qln.shape), full(wuq.shape), full(kvln.shape), full(wukv.shape),
                  full(qhn.shape), full(khn.shape), tile(LANES), tile(LANES), tile(LANES)],
        out_specs=[tile(Q_LORA), tile(KV_LORA), tile(LANES)] + [full(s) for s in acc_shapes],
        scratch_shapes=[pltpu.VMEM((TS, 1024), BF16), pltpu.VMEM((TS, 1024), BF16)],
        compiler_params=pltpu.CompilerParams(dimension_semantics=("arbitrary",), vmem_limit_bytes=VMEM_BIG),
    )(dq, dk, dv, cq, ckv, kr, qln, wuq, kvln, wukv, qhn, khn, cosf, sin_a, sin_b)


def _dproj_bwd(dq_sb, dk_sb, dv_sb, dg_sb, dcq, dckv, dg_mla, dkr, w_in_r, x, dy, norm_w, scale):
    s_len = x.shape[0]

    def body(dq_ref, dk_ref, dv_ref, dg_ref, dcq_ref, dckv_ref, dgm_ref, dkr_ref, w_ref, x_ref, dy_ref,
             nw_ref, scale_ref, dp_ref, gx_ref, dshift_ref, dscale_ref, dnw_ref):
        @pl.when(pl.program_id(0) == 0)
        def _():
            for r_ in (dshift_ref, dscale_ref, dnw_ref):
                r_[...] = jnp.zeros_like(r_)

        dp_ref[:, 0:512] = dq_ref[...].astype(BF16)
        dp_ref[:, 512:1024] = dk_ref[...].astype(BF16)
        dp_ref[:, 1024:1536] = dv_ref[...].astype(BF16)
        dp_ref[:, 1536:2048] = dg_ref[...]
        dp_ref[:, 2048:2432] = dcq_ref[...]
        dp_ref[:, 2432:2688] = dckv_ref[...]
        dp_ref[:, 2688:3200] = dgm_ref[...]
        dp_ref[:, 3200:3328] = dkr_ref[...]
        dh = _dot_nt(dp_ref[...], w_ref[...])
        xv = x_ref[...]
        r = lax.rsqrt(jnp.mean(xv * xv, axis=1, keepdims=True) + EPS)
        xh = xv * r
        nw = nw_ref[...]
        dshift_ref[...] += jnp.sum(dh, axis=0, keepdims=True)
        dscale_ref[...] += jnp.sum(dh * (xh * nw), axis=0, keepdims=True)
        dxnw = dh * (1.0 + scale_ref[...])
        dnw_ref[...] += jnp.sum(dxnw * xh, axis=0, keepdims=True)
        gx_ref[...] = dy_ref[...] + _rms_bwd(dxnw * nw, xh, r, D_MODEL)

    def tile(width):
        return pl.BlockSpec((TS, width), lambda i: (i, 0))

    def full(shape):
        return pl.BlockSpec(shape, lambda i: (0, 0))

    vec = (1, D_MODEL)
    return pl.pallas_call(
        body, name="dproj_bwd", grid=(s_len // TS,),
        out_shape=[jax.ShapeDtypeStruct((s_len, IN_COLS_R), BF16), jax.ShapeDtypeStruct((s_len, D_MODEL), F32)]
        + [jax.ShapeDtypeStruct(vec, F32)] * 3,
        in_specs=[tile(512)] * 4 + [tile(Q_LORA), tile(KV_LORA), tile(512), tile(LANES),
                                    full(w_in_r.shape), tile(D_MODEL), tile(D_MODEL), full(vec), full(vec)],
        out_specs=[tile(IN_COLS_R), tile(D_MODEL)] + [full(vec)] * 3,
        compiler_params=pltpu.CompilerParams(dimension_semantics=("arbitrary",), vmem_limit_bytes=VMEM_BIG),
    )(dq_sb, dk_sb, dv_sb, dg_sb, dcq, dckv, dg_mla, dkr, w_in_r, x, dy, norm_w, scale)


def _grad_w_in(hb, dproj):
    s_len = hb.shape[0]
    n_half = IN_COLS_R // 2

    def body(h_ref, d_ref, g_ref):
        @pl.when(pl.program_id(1) == 0)
        def _():
            g_ref[...] = jnp.zeros_like(g_ref)

        g_ref[...] += _dot_tn(h_ref[...], d_ref[...])

    return pl.pallas_call(
        body, name="grad_w_in", grid=(2, s_len // TN_S),
        out_shape=jax.ShapeDtypeStruct((D_MODEL, IN_COLS_R), F32),
        in_specs=[pl.BlockSpec((TN_S, D_MODEL), lambda n, s: (s, 0)),
                  pl.BlockSpec((TN_S, n_half), lambda n, s: (s, n))],
        out_specs=pl.BlockSpec((D_MODEL, n_half), lambda n, s: (0, n)),
        compiler_params=pltpu.CompilerParams(dimension_semantics=("parallel", "arbitrary"),
                                             vmem_limit_bytes=VMEM_BIG),
    )(hb, dproj)


def _final_exchange(gpack, ccol, wpack, mpack, vpack, w_ada, m_ada, v_ada, grads):
    n_sh = w_ada.shape[1]
    n = len(grads)

    def body(*refs):
        (g_ref, cc_ref, wp_ref, mp_ref, vp_ref, wa_ref, ma_ref, va_ref) = refs[:8]
        slabs_in = refs[8:8 + n]
        (og_ref, od_ref, om_ref, ov_ref, ag_ref, ad_ref, am_ref, av_ref) = refs[8 + n:16 + n]
        slabs_out = refs[16 + n:16 + 2 * n]
        gall_ref, call_ref, ssem, rsem, slab_ssem, slab_rsem, lsem = refs[16 + 2 * n:]
        pos = _mesh_pos()
        me = _lin(pos)
        own = [pltpu.make_async_copy(slabs_in[a].at[me], slabs_out[a].at[me], lsem.at[a]) for a in range(n)]
        for cp in own:
            cp.start()
        in_flight = [_all_to_all_start(pos, slabs_in[a], slabs_out[a], slab_ssem, slab_rsem, a * (N_DEV - 1))
                     for a in range(n)]
        gall_ref[me] = g_ref[...]
        call_ref[me] = cc_ref[...]
        _all_gather(pos, g_ref, gall_ref, ssem, rsem, 0)
        _all_gather(pos, cc_ref, call_ref, ssem, rsem, N_DEV - 1)

        tot = gall_ref[0]
        for j in range(1, N_DEV):
            tot = tot + gall_ref[j]
        og_ref[...] = tot
        od_ref[...], om_ref[...], ov_ref[...] = _adamw(wp_ref[...], tot, mp_ref[...], vp_ref[...])

        ga = jnp.zeros((D_MODEL, n_sh), F32)
        for j in range(N_DEV):
            d_mine = jnp.zeros((8, n_sh), F32)
            for k in range(N_DEV):
                d_mine = d_mine + jnp.where(me == k, gall_ref[j, :, PK_ADA + n_sh * k:PK_ADA + n_sh * (k + 1)], 0.0)
            col = _silu(call_ref[j])
            ga = ga + jnp.concatenate(
                [col * d_mine[0:1, LANES * a:LANES * (a + 1)] for a in range(n_sh // LANES)], axis=1)
        ag_ref[...] = ga
        ad_ref[...], am_ref[...], av_ref[...] = _adamw(wa_ref[...], ga, ma_ref[...], va_ref[...])

        for a in range(n):
            _all_to_all_wait(pos, slabs_in[a], slabs_out[a], slab_ssem, slab_rsem, a * (N_DEV - 1), in_flight[a])
        for cp in own:
            cp.wait()

    pk = jax.ShapeDtypeStruct((8, PK_END), F32)
    ada = jax.ShapeDtypeStruct((D_MODEL, n_sh), F32)
    return pl.pallas_call(
        body, name="final_exchange",
        out_shape=[pk] * 4 + [ada] * 4 + [jax.ShapeDtypeStruct(g.shape, g.dtype) for g in grads],
        in_specs=[_vmem_spec()] * 8 + [_any_spec()] * n,
        out_specs=[_vmem_spec()] * 8 + [_any_spec()] * n,
        scratch_shapes=[
            pltpu.VMEM((N_DEV, 8, PK_END), F32),
            pltpu.VMEM((N_DEV, D_MODEL, LANES), F32),
            pltpu.SemaphoreType.DMA((2 * (N_DEV - 1),)),
            pltpu.SemaphoreType.DMA((2 * (N_DEV - 1),)),
            pltpu.SemaphoreType.DMA((n * (N_DEV - 1),)),
            pltpu.SemaphoreType.DMA((n * (N_DEV - 1),)),
            pltpu.SemaphoreType.DMA((n,)),
        ],
        compiler_params=pltpu.CompilerParams(vmem_limit_bytes=VMEM_BIG),
    )(gpack, ccol, wpack, mpack, vpack, w_ada, m_ada, v_ada, *grads)


def _adamw_reduce(name, parts, w, m, v, row_tile):
    rows, cols = w.shape

    def body(p_ref, w_ref, m_ref, v_ref, g_ref, d_ref, mo_ref, vo_ref):
        g = p_ref[0].astype(F32)
        for j in range(1, N_DEV):
            g = g + p_ref[j].astype(F32)
        g_ref[...] = g
        d_ref[...], mo_ref[...], vo_ref[...] = _adamw(w_ref[...], g, m_ref[...], v_ref[...])

    tile = pl.BlockSpec((row_tile, cols), lambda i: (i, 0))
    return pl.pallas_call(
        body, name=name, grid=(rows // row_tile,),
        out_shape=[jax.ShapeDtypeStruct((rows, cols), F32)] * 4,
        in_specs=[pl.BlockSpec((N_DEV, row_tile, cols), lambda i: (0, i, 0)), tile, tile, tile],
        out_specs=[tile] * 4,
        compiler_params=pltpu.CompilerParams(dimension_semantics=("parallel",), vmem_limit_bytes=VMEM_BIG),
    )(parts, w, m, v)


def _rope_tables(positions):
    inv_freq = 10000.0 ** (-jnp.arange(0, ROPE, 2, dtype=F32) / ROPE)
    ang = positions.astype(F32)[:, None] * inv_freq
    cos, sin = jnp.cos(ang), jnp.sin(ang)
    s_len = positions.shape[0]
    ones = jnp.ones((s_len, NOPE), F32)
    zeros = jnp.zeros((s_len, NOPE), F32)
    z16 = jnp.zeros((s_len, ROPE // 2), F32)
    pad1 = jnp.ones((s_len, LANES - MLA_QK), F32)
    pad0 = jnp.zeros((s_len, LANES - MLA_QK), F32)
    cosf = jnp.concatenate([ones, cos, cos, pad1], axis=1)
    sin_a = jnp.concatenate([zeros, -sin, z16, pad0], axis=1)
    sin_b = jnp.concatenate([zeros, z16, sin, pad0], axis=1)
    return cosf, sin_a, sin_b


def _rearrange_cols(w):
    pad = jnp.zeros((w.shape[0], IN_COLS_R - IN_COLS), w.dtype)
    return jnp.concatenate([w[:, :2688], w[:, 2720:3232], w[:, 2688:2720], pad], axis=1)


def _restore_cols(g):
    return jnp.concatenate([g[:, :2688], g[:, 3200:3232], g[:, 2688:3200]], axis=1)


def _pad_heads(w):
    rows = w.shape[0]
    w = w.reshape(rows, HEADS, MLA_QK)
    return jnp.pad(w, ((0, 0), (0, 0), (0, LANES - MLA_QK))).reshape(rows, HEADS * LANES)


def _unpad_heads(g):
    rows = g.shape[0]
    return g.reshape(rows, HEADS, LANES)[:, :, :MLA_QK].reshape(rows, HEADS * MLA_QK)


def _pad_lanes(v):
    return jnp.pad(v, ((0, 0), (0, LANES - v.shape[1])))


def _col_shards(g):
    rows = g.shape[0]
    return g.reshape(rows, N_DEV, g.shape[1] // N_DEV).transpose(1, 0, 2)


def _from_col_shards(g):
    return g.transpose(1, 0, 2).reshape(g.shape[1], N_DEV * g.shape[2])


def _pack(norm_w, qln, kvln, qhn, khn, ada, loss_lanes=None):
    if loss_lanes is None:
        loss_lanes = jnp.zeros((1, PK_END - PK_LOSS), F32)
    row = jnp.concatenate([norm_w, qln, kvln, _pad_lanes(qhn), _pad_lanes(khn), ada, loss_lanes], axis=1)
    return jnp.broadcast_to(row, (8, PK_END))


def _unpack(p):
    row = p[0:1]
    return (row[:, PK_NORM:PK_QLN], row[:, PK_QLN:PK_KVLN], row[:, PK_KVLN:PK_QHN],
            row[:, PK_QHN:PK_QHN + MLA_QK], row[:, PK_KHN:PK_KHN + MLA_QK], row[:, PK_ADA:PK_LOSS])


def kernel(x, c, positions, w_ada, b_ada, norm_w, w_in, q_lora_norm, w_uq, kv_lora_norm, w_ukv, q_head_norm, k_head_norm, w_out, loss_target, m_w_ada, m_b_ada, m_norm_w, m_w_in, m_q_lora_norm, m_w_uq, m_kv_lora_norm, m_w_ukv, m_q_head_norm, m_k_head_norm, m_w_out, v_w_ada, v_b_ada, v_norm_w, v_w_in, v_q_lora_norm, v_w_uq, v_kv_lora_norm, v_w_ukv, v_q_head_norm, v_k_head_norm, v_w_out):
    s_len = x.shape[1]
    x2 = x.reshape(s_len, D_MODEL)
    tgt = loss_target.reshape(s_len, D_MODEL)
    w_ada_s, w_in_s, w_uq_s, w_ukv_s, w_out_s = w_ada[0], w_in[0], w_uq[0], w_ukv[0], w_out[0]

    ada8 = _ada_fwd(jnp.broadcast_to(c, (8, D_MODEL)), w_ada_s, b_ada.reshape(N_DEV, -1))
    ada = ada8.reshape(1, 3 * D_MODEL)
    shift, scale, gate = ada[:, :D_MODEL], ada[:, D_MODEL:2 * D_MODEL], ada[:, 2 * D_MODEL:]

    g_uq, g_ukv, g_out, g_in = _gather_weights([w_uq_s, w_ukv_s, w_out_s, w_in_s])
    w_in_r = _rearrange_cols(_from_col_shards(g_in))
    wuq_p = _pad_heads(_from_col_shards(g_uq))
    wukv_f = _from_col_shards(g_ukv)
    w_out_f = g_out.reshape(D_MODEL, D_MODEL)

    cosf, sin_a, sin_b = _rope_tables(positions[0])
    qhn_p, khn_p = _pad_lanes(q_head_norm), _pad_lanes(k_head_norm)

    hb, qkv, g_sb, cq, ckv, g_mla, kr, qm, km, vm = _fwd_pre(
        x2, shift, scale, norm_w, w_in_r, q_lora_norm, wuq_p, kv_lora_norm, wukv_f, qhn_p, khn_p,
        cosf, sin_a, sin_b)
    o_sb, w_saved, l_saved = _sb_fwd(qkv)
    o_mla, lse = _mla_fwd(qm, km, vm)

    dy, do_sb, dg_sb, do_mla, dg_mla, gw_out, d_gate, loss_acc = _mid(
        o_sb, g_sb, o_mla, g_mla, x2, tgt, gate, w_out_f)

    dq_sb, dk_sb_t, dv_sb_t = _sb_bwd(qkv, qkv[:, :SB_WIDTH].T, do_sb, do_sb.T, w_saved, l_saved)
    dk_sb, dv_sb = _from_key_tiles(dk_sb_t), _from_key_tiles(dv_sb_t)
    dq_m, dk_m_t, dv_m_t = _mla_bwd(qm, qm.T, km, vm, o_mla, do_mla, do_mla.T, lse)
    dk_m, dv_m = _from_key_tiles(dk_m_t), _from_key_tiles(dv_m_t)
    dcq, dckv, dkr, gw_uq_p, gw_ukv, g_qhn, g_khn, g_qln, g_kvln = _mla_pre_bwd(
        dq_m, dk_m, dv_m, cq, ckv, kr, q_lora_norm, wuq_p, kv_lora_norm, wukv_f, qhn_p, khn_p,
        cosf, sin_a, sin_b)
    dproj, grad_x, d_shift, d_scale, g_norm_w = _dproj_bwd(
        dq_sb, dk_sb, dv_sb, dg_sb, dcq, dckv, dg_mla, dkr, w_in_r, x2, dy, norm_w, scale)
    gw_in = _restore_cols(_grad_w_in(hb, dproj))

    d_ada = jnp.concatenate([d_shift, d_scale, d_gate], axis=1)
    gpack = _pack(g_norm_w, g_qln, g_kvln, g_qhn[:, :MLA_QK], g_khn[:, :MLA_QK], d_ada, loss_acc)
    wpack = _pack(norm_w, q_lora_norm, kv_lora_norm, q_head_norm, k_head_norm, b_ada)
    mpack = _pack(m_norm_w, m_q_lora_norm, m_kv_lora_norm, m_q_head_norm, m_k_head_norm, m_b_ada)
    vpack = _pack(v_norm_w, v_q_lora_norm, v_kv_lora_norm, v_q_head_norm, v_k_head_norm, v_b_ada)
    ccol = jnp.broadcast_to(c.reshape(D_MODEL, 1), (D_MODEL, LANES))
    slabs = [g.astype(BF16) for g in (
        _col_shards(gw_in), _col_shards(_unpad_heads(gw_uq_p)), _col_shards(gw_ukv),
        gw_out.reshape(N_DEV, D_MODEL // N_DEV, D_MODEL))]
    pg, pd, pm, pv, ada_g, ada_d, ada_m, ada_v, r_in, r_uq, r_ukv, r_out = _final_exchange(
        gpack, ccol, wpack, mpack, vpack, w_ada_s, m_w_ada[0], v_w_ada[0], slabs)
    loss = 0.5 * jnp.sum(pg[0, PK_LOSS:PK_END]) / D_MODEL

    in_g, in_d, in_m, in_v = _adamw_reduce("adamw_w_in", r_in, w_in_s, m_w_in[0], v_w_in[0], 256)
    uq_g, uq_d, uq_m, uq_v = _adamw_reduce("adamw_w_uq", r_uq, w_uq_s, m_w_uq[0], v_w_uq[0], w_uq_s.shape[0])
    ukv_g, ukv_d, ukv_m, ukv_v = _adamw_reduce(
        "adamw_w_ukv", r_ukv, w_ukv_s, m_w_ukv[0], v_w_ukv[0], w_ukv_s.shape[0])
    out_g, out_d, out_m, out_v = _adamw_reduce(
        "adamw_w_out", r_out, w_out_s, m_w_out[0], v_w_out[0], w_out_s.shape[0])

    def group(ada_t, pk, in_t, uq_t, ukv_t, out_t):
        nw, qln, kvln, qhn, khn, b = _unpack(pk)
        return (ada_t[None], b, nw, in_t[None], qln, uq_t[None], kvln, ukv_t[None], qhn, khn, out_t[None])

    return (loss, grad_x.reshape(1, s_len, D_MODEL),
            *group(ada_g, pg, in_g, uq_g, ukv_g, out_g),
            *group(ada_d, pd, in_d, uq_d, ukv_d, out_d),
            *group(ada_m, pm, in_m, uq_m, ukv_m, out_m),
            *group(ada_v, pv, in_v, uq_v, ukv_v, out_v))
```

```python
import functools
import math

import jax
import jax.numpy as jnp
from jax import lax
from jax.experimental import pallas as pl
from jax.experimental.pallas import tpu as pltpu

F32 = jnp.float32
BF16 = jnp.bfloat16

N_DEV = 8
D_MODEL = 1024
HEADS = 8
SB_WIDTH = 512
MLA_WIDTH = 512
Q_LORA = 384
KV_LORA = 256
ROPE = 32
NOPE = 64
MLA_QK = 96
LANES = 128
IN_COLS = 3232
IN_COLS_R = 3328
EPS = 1e-6
NEG = -1e30

ADAM_LR = 0.001
ADAM_B1 = 0.9
ADAM_B2 = 0.999
ADAM_EPS = 1e-08
ADAM_WD = 0.01
ADAM_STEP = 10

TS = 512
TS_FWD = 256
TQ = 512
KEY_UNROLL = 2
TN_S = 512
VMEM_BIG = 56 * 1024 * 1024

PK_NORM, PK_QLN, PK_KVLN, PK_QHN, PK_KHN, PK_ADA, PK_LOSS, PK_END = 0, 1024, 1408, 1664, 1792, 1920, 4992, 6016

MESH_ID = pl.DeviceIdType.MESH


def _dot_nn(a, b):
    return lax.dot_general(a, b, (((1,), (0,)), ((), ())), preferred_element_type=F32)


def _dot_nt(a, b):
    return lax.dot_general(a, b, (((1,), (1,)), ((), ())), preferred_element_type=F32)


def _dot_tn(a, b):
    return lax.dot_general(a, b, (((0,), (0,)), ((), ())), preferred_element_type=F32)


def _split_bf16(a):
    hi = a.astype(BF16)
    lo = (a - hi.astype(F32)).astype(BF16)
    return hi, lo


def _dot3(a, b):
    ah, al = _split_bf16(a)
    bh, bl = _split_bf16(b)
    return _dot_nn(ah, bh) + _dot_nn(ah, bl) + _dot_nn(al, bh)


def _sigmoid(g):
    return 1.0 / (1.0 + jnp.exp(-g))


def _silu(g):
    return g * _sigmoid(g)


def _lane_iota(shape):
    return lax.broadcasted_iota(jnp.int32, shape, len(shape) - 1)


def _adamw(w, g, m, v):
    m = ADAM_B1 * m + (1.0 - ADAM_B1) * g
    v = ADAM_B2 * v + (1.0 - ADAM_B2) * (g * g)
    m_hat = m / (1.0 - ADAM_B1 ** ADAM_STEP)
    v_hat = v / (1.0 - ADAM_B2 ** ADAM_STEP)
    delta = -ADAM_LR * (m_hat / (jnp.sqrt(v_hat) + ADAM_EPS) + ADAM_WD * w)
    return delta, m, v


def _mesh_pos():
    return lax.axis_index("x"), lax.axis_index("y"), lax.axis_index("c")


def _peer(pos, k):
    x, y, c = pos
    return (1 - x if k & 4 else x, 1 - y if k & 2 else y, 1 - c if k & 1 else c)


def _lin(pos):
    return 4 * pos[0] + 2 * pos[1] + pos[2]


def _remote(src, dst, send_sems, recv_sems, idx, peer):
    return pltpu.make_async_remote_copy(
        src_ref=src, dst_ref=dst, send_sem=send_sems.at[idx], recv_sem=recv_sems.at[idx],
        device_id=peer, device_id_type=MESH_ID)


def _all_gather_start(pos, src, buf, send_sems, recv_sems, base):
    me = _lin(pos)
    sent = []
    for k in range(1, N_DEV):
        cp = _remote(src, buf.at[me], send_sems, recv_sems, base + k - 1, _peer(pos, k))
        cp.start()
        sent.append(cp)
    return sent


def _all_gather_wait(pos, src, buf, send_sems, recv_sems, base, sent):
    for k in range(1, N_DEV):
        peer = _peer(pos, k)
        _remote(src, buf.at[_lin(peer)], send_sems, recv_sems, base + k - 1, peer).wait_recv()
    for cp in sent:
        cp.wait_send()


def _all_gather(pos, src, buf, send_sems, recv_sems, base):
    sent = _all_gather_start(pos, src, buf, send_sems, recv_sems, base)
    _all_gather_wait(pos, src, buf, send_sems, recv_sems, base, sent)


def _all_to_all_start(pos, src, buf, send_sems, recv_sems, base):
    me = _lin(pos)
    sent = []
    for k in range(1, N_DEV):
        peer = _peer(pos, k)
        cp = _remote(src.at[_lin(peer)], buf.at[me], send_sems, recv_sems, base + k - 1, peer)
        cp.start()
        sent.append(cp)
    return sent


def _all_to_all_wait(pos, src, buf, send_sems, recv_sems, base, sent):
    me = _lin(pos)
    for k in range(1, N_DEV):
        peer = _peer(pos, k)
        _remote(src.at[me], buf.at[_lin(peer)], send_sems, recv_sems, base + k - 1, peer).wait_recv()
    for cp in sent:
        cp.wait_send()


def _all_to_all(pos, src, buf, send_sems, recv_sems, base):
    sent = _all_to_all_start(pos, src, buf, send_sems, recv_sems, base)
    _all_to_all_wait(pos, src, buf, send_sems, recv_sems, base, sent)


def _two_level_gather(pos, bufs, send_sems, recv_sems):
    x, y, c = pos
    me, sibling = (x, y, c), (x, y, 1 - c)
    chips = [(1 - x, y), (x, 1 - y), (1 - x, 1 - y)]

    def copy(a, k, block, to):
        slot = bufs[a].at[_lin(block)]
        return _remote(slot, slot, send_sems, recv_sems, 7 * a + k, to)

    started = []
    for a in range(len(bufs)):
        first = [copy(a, 0, me, sibling)] + [copy(a, 1 + j, me, (*chip, c)) for j, chip in enumerate(chips)]
        for cp in first:
            cp.start()
        started += first
    for a in range(len(bufs)):
        for j, chip in enumerate(chips):
            copy(a, 1 + j, (*chip, c), me).wait_recv()
            passed = copy(a, 4 + j, (*chip, c), sibling)
            passed.start()
            started.append(passed)
    for a in range(len(bufs)):
        copy(a, 0, sibling, me).wait_recv()
        for j, chip in enumerate(chips):
            copy(a, 4 + j, (*chip, 1 - c), me).wait_recv()
    for cp in started:
        cp.wait_send()


def _vmem_spec():
    return pl.BlockSpec(memory_space=pltpu.VMEM)


def _any_spec():
    return pl.BlockSpec(memory_space=pl.ANY)


def _row_select(slots, n):
    r = lax.broadcasted_iota(jnp.int32, (N_DEV, n), 0)
    out = jnp.zeros((N_DEV, n), F32)
    for j in range(N_DEV):
        out = out + jnp.where(r == j, slots[j], 0.0)
    return out


def _ada_fwd(c8, w_ada, b_ada8):
    n_sh = w_ada.shape[1]

    def body(c_ref, w_ref, b_ref, out_ref, call_out_ref, call_ref, psend_ref, precv_ref, ssem, rsem):
        pos = _mesh_pos()
        me = _lin(pos)
        call_ref[me] = c_ref[...]
        _all_gather(pos, c_ref, call_ref, ssem, rsem, 0)
        call_out_ref[...] = _row_select([call_ref[j] for j in range(N_DEV)], D_MODEL)
        w = w_ref[...]
        for j in range(N_DEV):
            psend_ref[j] = _dot3(_silu(call_ref[j]), w)
        precv_ref[me] = psend_ref[me]
        _all_to_all(pos, psend_ref, precv_ref, ssem, rsem, N_DEV - 1)
        out_ref[...] = _row_select([precv_ref[j] for j in range(N_DEV)], n_sh) + b_ref[...]

    return pl.pallas_call(
        body, name="ada_fwd",
        out_shape=[jax.ShapeDtypeStruct((N_DEV, n_sh), F32), jax.ShapeDtypeStruct((N_DEV, D_MODEL), F32)],
        in_specs=[_vmem_spec()] * 3, out_specs=[_vmem_spec()] * 2,
        scratch_shapes=[
            pltpu.VMEM((N_DEV, 8, D_MODEL), F32),
            pltpu.VMEM((N_DEV, 8, n_sh), F32),
            pltpu.VMEM((N_DEV, 8, n_sh), F32),
            pltpu.SemaphoreType.DMA((2 * (N_DEV - 1),)),
            pltpu.SemaphoreType.DMA((2 * (N_DEV - 1),)),
        ],
    )(c8, w_ada, b_ada8)


def _gather_weights(shards):
    n = len(shards)

    def body(*refs):
        ins, outs = refs[:n], refs[n:2 * n]
        ssem, rsem = refs[2 * n], refs[2 * n + 1]
        pos = _mesh_pos()
        me = _lin(pos)
        for a in range(n):
            outs[a][me] = ins[a][...].astype(BF16)
        _two_level_gather(pos, outs, ssem, rsem)

    return pl.pallas_call(
        body, name="gather_weights",
        out_shape=[jax.ShapeDtypeStruct((N_DEV,) + s.shape, BF16) for s in shards],
        in_specs=[_vmem_spec()] * n, out_specs=[_vmem_spec()] * n,
        scratch_shapes=[
            pltpu.SemaphoreType.DMA((n * (N_DEV - 1),)),
            pltpu.SemaphoreType.DMA((n * (N_DEV - 1),)),
        ],
        compiler_params=pltpu.CompilerParams(vmem_limit_bytes=VMEM_BIG),
    )(*shards)


def _rope(t, cosf, sin_a, sin_b):
    return t * cosf + pltpu.roll(t, 112, 1) * sin_a + pltpu.roll(t, 16, 1) * sin_b


def _rope_t(d, cosf, sin_a, sin_b):
    return d * cosf + pltpu.roll(d * sin_a, 16, 1) + pltpu.roll(d * sin_b, 112, 1)


def _head_rms(t):
    return lax.rsqrt(jnp.sum(t * t, axis=1, keepdims=True) * (1.0 / MLA_QK) + EPS)


def _rms_bwd(dxhat_w, xhat, r, n):
    return r * (dxhat_w - xhat * (jnp.sum(dxhat_w * xhat, axis=1, keepdims=True) * (1.0 / n)))


def _mla_latents(cq, ckv, kr, qln, kvln, wuq, wukv):
    rq = lax.rsqrt(jnp.mean(cq * cq, axis=1, keepdims=True) + EPS)
    rkv = lax.rsqrt(jnp.mean(ckv * ckv, axis=1, keepdims=True) + EPS)
    cq_hat = cq * rq
    ckv_hat = ckv * rkv
    cqn = (cq_hat * qln).astype(BF16)
    ckvn = (ckv_hat * kvln).astype(BF16)
    q_all = _dot_nn(cqn, wuq)
    kv = _dot_nn(ckvn, wukv)
    kr64 = pltpu.roll(kr, 64, 1)
    return rq, rkv, cq_hat, ckv_hat, cqn, ckvn, q_all, kv, kr64


def _fwd_pre(x, shift, scale, norm_w, w_in_r, qln, wuq, kvln, wukv, qhn, khn, cosf, sin_a, sin_b):
    s_len = x.shape[0]

    def body(x_ref, shift_ref, scale_ref, nw_ref, w_ref, qln_ref, wuq_ref, kvln_ref, wukv_ref,
             qhn_ref, khn_ref, cos_ref, sa_ref, sb_ref,
             hb_ref, qkv_ref, gsb_ref, cq_ref, ckv_ref, gmla_ref, kr_ref, qm_ref, km_ref, vm_ref,
             qsbt_ref, qmt_ref):
        xv = x_ref[...]
        r = lax.rsqrt(jnp.mean(xv * xv, axis=1, keepdims=True) + EPS)
        h = (xv * r) * nw_ref[...] * (1.0 + scale_ref[...]) + shift_ref[...]
        hb = h.astype(BF16)
        hb_ref[...] = hb
        qkv = _dot_nn(hb, w_ref[:, 0:1536])
        qkv_ref[...] = qkv.astype(BF16)
        qsbt_ref[...] = qkv[:, :SB_WIDTH].T.astype(BF16)
        gsb_ref[...] = _dot_nn(hb, w_ref[:, 1536:2048])
        cq = _dot_nn(hb, w_ref[:, 2048:2432])
        ckv = _dot_nn(hb, w_ref[:, 2432:2688])
        gmla_ref[...] = _dot_nn(hb, w_ref[:, 2688:3200])
        kr = _dot_nn(hb, w_ref[:, 3200:3328])
        cq_ref[...] = cq
        ckv_ref[...] = ckv
        kr_ref[...] = kr
        _, _, _, _, _, _, q_all, kv, kr64 = _mla_latents(
            cq, ckv, kr, qln_ref[...], kvln_ref[...], wuq_ref[...], wukv_ref[...])
        cosf, sa, sb = cos_ref[...], sa_ref[...], sb_ref[...]
        qhn_v, khn_v = qhn_ref[...], khn_ref[...]
        low = _lane_iota((TS_FWD, LANES)) < NOPE
        for hd in range(HEADS):
            blk = slice(LANES * hd, LANES * (hd + 1))
            qb = q_all[:, blk]
            qr = _rope(qb * _head_rms(qb) * qhn_v, cosf, sa, sb) * MLA_SCALE
            qm_ref[:, blk] = qr.astype(BF16)
            qmt_ref[blk, :] = qr.T.astype(BF16)
            kb = jnp.where(low, kv[:, blk], kr64)
            km_ref[:, blk] = _rope(kb * _head_rms(kb) * khn_v, cosf, sa, sb).astype(BF16)
        for p in range(HEADS // 2):
            even = kv[:, LANES * 2 * p:LANES * (2 * p + 1)]
            odd = kv[:, LANES * (2 * p + 1):LANES * (2 * p + 2)]
            vm_ref[:, LANES * p:LANES * (p + 1)] = jnp.where(low, pltpu.roll(even, 64, 1), odd).astype(BF16)

    def tile(width):
        return pl.BlockSpec((TS_FWD, width), lambda i: (i, 0))

    def full(a):
        return pl.BlockSpec(a.shape, lambda i: (0, 0))

    out_widths = [(D_MODEL, BF16), (1536, BF16), (512, F32), (Q_LORA, F32), (KV_LORA, F32),
                  (512, F32), (LANES, F32), (1024, BF16), (1024, BF16), (512, BF16)]
    t_heights = [SB_WIDTH, HEADS * LANES]
    return pl.pallas_call(
        body, name="fwd_pre", grid=(s_len // TS_FWD,),
        out_shape=[jax.ShapeDtypeStruct((s_len, w), dt) for w, dt in out_widths]
        + [jax.ShapeDtypeStruct((hgt, s_len), BF16) for hgt in t_heights],
        in_specs=[tile(D_MODEL), full(shift), full(scale), full(norm_w), full(w_in_r), full(qln), full(wuq),
                  full(kvln), full(wukv), full(qhn), full(khn), tile(LANES), tile(LANES), tile(LANES)],
        out_specs=[tile(w) for w, _ in out_widths]
        + [pl.BlockSpec((hgt, TS_FWD), lambda i: (0, i)) for hgt in t_heights],
        compiler_params=pltpu.CompilerParams(dimension_semantics=("parallel",), vmem_limit_bytes=VMEM_BIG),
    )(x, shift, scale, norm_w, w_in_r, qln, wuq, kvln, wukv, qhn, khn, cosf, sin_a, sin_b)


CUM_W = 256


def _tri(strict):
    j = lax.broadcasted_iota(jnp.int32, (CUM_W, CUM_W), 0)
    s = lax.broadcasted_iota(jnp.int32, (CUM_W, CUM_W), 1)
    return (j > s if strict else j >= s).astype(BF16)


def _suffix_sums(a, a_bf16, tri_m, carry):
    n = a.shape[1] // CUM_W
    outs = [None] * n
    for i in reversed(range(n)):
        cols = slice(CUM_W * i, CUM_W * (i + 1))
        outs[i] = _dot_nn(a_bf16[:, cols], tri_m) + carry
        carry = carry + _rowsum(a[:, cols])
    return (outs[0] if n == 1 else jnp.concatenate(outs, axis=1)), carry


def _sb_weights(qm, kb, carry, tri_u, diag):
    z = _dot_nt(qm, kb)
    nz = -z
    lk = jnp.minimum(nz, 0.0) - jnp.log(1.0 + jnp.exp(jnp.minimum(z, nz)))
    if diag:
        t = lax.broadcasted_iota(jnp.int32, (TQ, TQ), 0)
        s = lax.broadcasted_iota(jnp.int32, (TQ, TQ), 1)
        valid = s < t
        lk = jnp.where(valid, lk, 0.0)
    lk_hi = lk.astype(BF16)
    after, carry = _suffix_sums(lk, lk_hi, tri_u, carry)
    logw = z + lk + after
    if diag:
        logw = jnp.where(valid, logw, NEG)
    return lk_hi, jnp.exp(logw), carry


SB_SCALE = 0.125


def _head_masks():
    lane = _lane_iota((1, LANES))
    return [lane < 64, lane >= 64]


def _masked(hm, a):
    return jnp.where(hm, a, jnp.zeros_like(a))


def _rowsum(a):
    return jnp.sum(a, axis=1, keepdims=True)


def _key_rows(kj):
    return pl.ds(pl.multiple_of(kj * TQ, TQ), TQ)


def _over_key_tiles(count, fn, st, ascending):
    n_full = count // KEY_UNROLL
    n_rest = count - n_full * KEY_UNROLL

    def group(g, s_):
        return fn([g * KEY_UNROLL + (u if ascending else KEY_UNROLL - 1 - u) for u in range(KEY_UNROLL)], s_)

    if ascending:
        st = lax.fori_loop(0, n_full, group, st)
        return lax.fori_loop(0, n_rest, lambda i, s_: fn([n_full * KEY_UNROLL + i], s_), st)
    st = lax.fori_loop(0, n_rest, lambda i, s_: fn([count - 1 - i], s_), st)
    return lax.fori_loop(0, n_full, lambda i, s_: group(n_full - 1 - i, s_), st)


def _each_tile(block):
    def trip(tiles, st):
        for kj in tiles:
            st = block(_key_rows(kj), st, False)
        return st
    return trip


STAGE_SLOTS = 2 * KEY_UNROLL


def _stage_copies(to_hbm, hbm_refs, scr_refs, sems, pair, qi, kj):
    slot = (qi - kj) % STAGE_SLOTS
    out = []
    for h in range(2):
        for a in range(2):
            hbm, scr = hbm_refs[a].at[2 * pair + h, qi, kj], scr_refs[a].at[slot, h]
            sem = sems.at[4 * slot + 2 * h + a]
            out.append(pltpu.make_async_copy(scr, hbm, sem) if to_hbm else pltpu.make_async_copy(hbm, scr, sem))
    return out


def _sb_fwd(qkv):
    s_len = qkv.shape[0]
    nq = s_len // TQ

    def body(q_ref, k_ref, v_ref, o_ref, w_hbm, l_hbm, w_scr, l_scr, sems):
        pair, qi = pl.program_id(0), pl.program_id(1)
        q = q_ref[...]
        tri_u = _tri(True)
        masks = _head_masks()
        qms = [_masked(hm, q) * SB_SCALE for hm in masks]

        def copies(kj, of_qi=qi):
            return _stage_copies(True, (w_hbm, l_hbm), (w_scr, l_scr), sems, pair, of_qi, kj)

        def drain(of_qi):
            for kj in range(STAGE_SLOTS):
                @pl.when(kj <= of_qi)
                def _():
                    for cp in copies(kj, of_qi):
                        cp.wait()

        def block(kj, st, diag, before_staging=None):
            rows = _key_rows(kj)
            slot = (qi - kj) % STAGE_SLOTS
            kb, vb = k_ref[rows, :], v_ref[rows, :]
            carries, acc = list(st[:2]), st[2]
            staged = []
            for h in range(2):
                lk_hi, w, carries[h] = _sb_weights(qms[h], kb, carries[h], tri_u, diag)
                wb = w.astype(BF16)
                staged.append((wb, lk_hi))
                acc = acc + _dot_nn(wb, _masked(masks[h], vb))
            if before_staging is not None:
                before_staging()
            for h in range(2):
                w_scr[slot, h], l_scr[slot, h] = staged[h]
            return carries[0], carries[1], acc

        def trip(tiles, st):
            for kj in tiles:
                @pl.when(qi - kj >= STAGE_SLOTS)
                def _():
                    for cp in copies(kj + STAGE_SLOTS):
                        cp.wait()
            for kj in tiles:
                st = block(kj, st, False)
            for kj in tiles:
                for cp in copies(kj):
                    cp.start()
            return st

        def drain_previous_step():
            @pl.when(jnp.logical_or(pair > 0, qi > 0))
            def _():
                drain(jnp.where(qi == 0, nq - 1, qi - 1))

        zc = jnp.zeros((TQ, 1), F32)
        st = block(qi, (zc, zc, jnp.zeros((TQ, LANES), F32)), True, drain_previous_step)
        for cp in copies(qi):
            cp.start()
        st = _over_key_tiles(qi, trip, st, ascending=False)
        o_ref[...] = st[2]

        @pl.when(jnp.logical_and(pair == HEADS // 2 - 1, qi == nq - 1))
        def _():
            drain(qi)

    saved = jax.ShapeDtypeStruct((HEADS, nq, nq, TQ, TQ), BF16)
    stage = pltpu.VMEM((STAGE_SLOTS, 2, TQ, TQ), BF16)
    return pl.pallas_call(
        body, name="sb_fwd", grid=(HEADS // 2, nq),
        out_shape=[jax.ShapeDtypeStruct((s_len, SB_WIDTH), F32), saved, saved],
        in_specs=[pl.BlockSpec((TQ, LANES), lambda p, i: (i, p)),
                  pl.BlockSpec((s_len, LANES), lambda p, i: (0, 4 + p)),
                  pl.BlockSpec((s_len, LANES), lambda p, i: (0, 8 + p))],
        out_specs=[pl.BlockSpec((TQ, LANES), lambda p, i: (i, p)), _any_spec(), _any_spec()],
        scratch_shapes=[stage, stage, pltpu.SemaphoreType.DMA((4 * STAGE_SLOTS,))],
        compiler_params=pltpu.CompilerParams(dimension_semantics=("arbitrary", "arbitrary"),
                                             vmem_limit_bytes=VMEM_BIG),
    )(qkv, qkv, qkv)


def _prefix_sums(a, tri_m, carry):
    n = a.shape[1] // CUM_W
    outs = [None] * n
    for i in range(n):
        cols = slice(CUM_W * i, CUM_W * (i + 1))
        outs[i] = _dot_nn(a[:, cols].astype(BF16), tri_m) + carry
        carry = carry + _rowsum(a[:, cols])
    return (outs[0] if n == 1 else jnp.concatenate(outs, axis=1)), carry


def _head_rows(a0, a1):
    sub = lax.broadcasted_iota(jnp.int32, a0.shape, 0)
    return jnp.where(sub < NOPE, a0, a1)


def _sb_bwd(qkv, q_t, do, do_t, w_saved, l_saved):
    s_len = qkv.shape[0]
    nq = s_len // TQ

    def body(qt_ref, k_ref, v_ref, do_ref, dot_ref, w_hbm, l_hbm, dq_ref, dk_ref, dv_ref, w_scr, l_scr, sems):
        pair, qi = pl.program_id(0), pl.program_id(1)

        def copies(kj, of_pair=pair, of_qi=qi):
            return _stage_copies(False, (w_hbm, l_hbm), (w_scr, l_scr), sems, of_pair, of_qi, kj)

        def start_first_tiles(of_pair, of_qi):
            for cp in copies(0, of_pair, of_qi):
                cp.start()

            @pl.when(of_qi >= 1)
            def _():
                for cp in copies(1, of_pair, of_qi):
                    cp.start()

        @pl.when(jnp.logical_and(pair == 0, qi == 0))
        def _():
            start_first_tiles(pair, qi)

        @pl.when(qi == 0)
        def _():
            dk_ref[...] = jnp.zeros_like(dk_ref)
            dv_ref[...] = jnp.zeros_like(dv_ref)

        qt = qt_ref[...] * SB_SCALE
        dot_v = dot_ref[...]
        do_v = do_ref[...]
        j = lax.broadcasted_iota(jnp.int32, (CUM_W, CUM_W), 0)
        s = lax.broadcasted_iota(jnp.int32, (CUM_W, CUM_W), 1)
        tri_before = (j < s).astype(BF16)
        masks = _head_masks()
        doms = [_masked(hm, do_v) for hm in masks]

        def block(kj, st):
            rows = _key_rows(kj)
            slot = (qi - kj) % STAGE_SLOTS
            kb, vb = k_ref[rows, :], v_ref[rows, :]
            carries, dqs = list(st[0:2]), list(st[2:4])
            dk_t, dv_t = [], []
            for h in range(2):
                wb = w_scr[slot, h]
                d_l = _dot_nt(doms[h], vb) * wb.astype(F32)
                before, carries[h] = _prefix_sums(d_l, tri_before, carries[h])
                keep = jnp.exp(l_scr[slot, h].astype(F32))
                dzb = (d_l * keep - before * (1.0 - keep)).astype(BF16)
                dk_t.append(_dot_nn(qt, dzb))
                dv_t.append(_dot_nn(dot_v, wb))
                dqs[h] = dqs[h] + _dot_nn(dzb, kb)
            dk_ref[kj] += _head_rows(*dk_t)
            dv_ref[kj] += _head_rows(*dv_t)
            return (*carries, *dqs)

        def trip(tiles, st):
            for kj in tiles:
                @pl.when(kj + 2 <= qi)
                def _():
                    for cp in copies(kj + 2):
                        cp.start()
            for kj in tiles:
                for cp in copies(kj):
                    cp.wait()
            for kj in tiles:
                st = block(kj, st)
            return st

        zc = jnp.zeros((TQ, 1), F32)
        za = jnp.zeros((TQ, LANES), F32)
        st = _over_key_tiles(qi + 1, trip, (zc, zc, za, za), ascending=True)
        dq_ref[...] = jnp.where(masks[0], st[2], st[3]) * SB_SCALE

        @pl.when(jnp.logical_or(pair < HEADS // 2 - 1, qi < nq - 1))
        def _():
            wraps = qi == nq - 1
            start_first_tiles(jnp.where(wraps, pair + 1, pair), jnp.where(wraps, 0, qi + 1))

    tile = pl.BlockSpec((TQ, LANES), lambda p, i: (i, p))
    tile_t = pl.BlockSpec((LANES, TQ), lambda p, i: (p, i))
    col_t = pl.BlockSpec((nq, LANES, TQ), lambda p, i: (0, p, 0))
    stage = pltpu.VMEM((STAGE_SLOTS, 2, TQ, TQ), BF16)
    key_t = jax.ShapeDtypeStruct((nq, SB_WIDTH, TQ), F32)
    return pl.pallas_call(
        body, name="sb_bwd", grid=(HEADS // 2, nq),
        out_shape=[jax.ShapeDtypeStruct((s_len, SB_WIDTH), F32), key_t, key_t],
        in_specs=[tile_t,
                  pl.BlockSpec((s_len, LANES), lambda p, i: (0, 4 + p)),
                  pl.BlockSpec((s_len, LANES), lambda p, i: (0, 8 + p)),
                  tile, tile_t, _any_spec(), _any_spec()],
        out_specs=[tile, col_t, col_t],
        scratch_shapes=[stage, stage, pltpu.SemaphoreType.DMA((4 * STAGE_SLOTS,))],
        compiler_params=pltpu.CompilerParams(dimension_semantics=("arbitrary", "arbitrary"),
                                             vmem_limit_bytes=VMEM_BIG),
    )(q_t, qkv, qkv, do, do_t, w_saved, l_saved)


def _from_key_tiles(a_t):
    return a_t.transpose(0, 2, 1).reshape(a_t.shape[0] * a_t.shape[2], a_t.shape[1])


MLA_SCALE = 1.0 / math.sqrt(MLA_QK)


def _causal_mask():
    t = lax.broadcasted_iota(jnp.int32, (TQ, TQ), 0)
    s = lax.broadcasted_iota(jnp.int32, (TQ, TQ), 1)
    return s <= t


def _head_lanes(h):
    return slice(LANES * h, LANES * (h + 1))


def _mla_fwd(qm, km, vm):
    s_len = qm.shape[0]

    def body(q_ref, k_ref, v_ref, o_ref, lse_ref):
        qi = pl.program_id(1)
        masks = _head_masks()
        qhs = [q_ref[:, _head_lanes(h)] for h in range(2)]

        def block(rows, st, diag):
            vb = v_ref[rows, :]
            ms, ls, acc = list(st[0:2]), list(st[2:4]), st[4]
            alphas, pvs = [], []
            for h in range(2):
                s = _dot_nt(qhs[h], k_ref[rows, _head_lanes(h)])
                if diag:
                    s = jnp.where(_causal_mask(), s, NEG)
                m_new = jnp.maximum(ms[h], jnp.max(s, axis=1, keepdims=True))
                p = jnp.exp(s - m_new)
                alphas.append(jnp.exp(ms[h] - m_new))
                ls[h] = alphas[h] * ls[h] + _rowsum(p)
                ms[h] = m_new
                pvs.append(_dot_nn(p.astype(BF16), _masked(masks[h], vb)))
            acc = jnp.where(masks[0], alphas[0], alphas[1]) * acc + pvs[0] + pvs[1]
            return (*ms, *ls, acc)

        neg = jnp.full((TQ, 1), NEG, F32)
        zc = jnp.zeros((TQ, 1), F32)
        st = (neg, neg, zc, zc, jnp.zeros((TQ, LANES), F32))
        st = _over_key_tiles(qi, _each_tile(block), st, ascending=True)
        m0, m1, l0, l1, acc = block(_key_rows(qi), st, True)
        o_ref[...] = acc / jnp.where(masks[0], l0, l1)
        lse_ref[0] = m0 + jnp.log(l0)
        lse_ref[1] = m1 + jnp.log(l1)

    return pl.pallas_call(
        body, name="mla_fwd", grid=(HEADS // 2, s_len // TQ),
        out_shape=[jax.ShapeDtypeStruct((s_len, MLA_WIDTH), F32),
                   jax.ShapeDtypeStruct((HEADS, s_len, 1), F32)],
        in_specs=[pl.BlockSpec((TQ, 2 * LANES), lambda p, i: (i, p)),
                  pl.BlockSpec((s_len, 2 * LANES), lambda p, i: (0, p)),
                  pl.BlockSpec((s_len, LANES), lambda p, i: (0, p))],
        out_specs=[pl.BlockSpec((TQ, LANES), lambda p, i: (i, p)),
                   pl.BlockSpec((2, TQ, 1), lambda p, i: (p, i, 0))],
        compiler_params=pltpu.CompilerParams(dimension_semantics=("parallel", "parallel"),
                                             vmem_limit_bytes=VMEM_BIG),
    )(qm, km, vm)


def _mla_bwd(qm, qm_t, km, vm, o, do, do_t, lse):
    s_len = qm.shape[0]

    nq = s_len // TQ

    def body(q_ref, qt_ref, k_ref, v_ref, o_ref, do_ref, dot_ref, lse_ref, dq_ref, dk_ref, dv_ref):
        qi = pl.program_id(1)

        @pl.when(qi == 0)
        def _():
            dk_ref[...] = jnp.zeros_like(dk_ref)
            dv_ref[...] = jnp.zeros_like(dv_ref)

        do_v = do_ref[...]
        dot_v = dot_ref[...]
        od = o_ref[...] * do_v.astype(F32)
        masks = _head_masks()
        qhs = [q_ref[:, _head_lanes(h)] for h in range(2)]
        qts = [qt_ref[_head_lanes(h), :] for h in range(2)]
        doms = [_masked(hm, do_v) for hm in masks]
        deltas = [_rowsum(jnp.where(hm, od, 0.0)) for hm in masks]
        lses = [lse_ref[h] for h in range(2)]

        def block(kj, dqs, diag):
            rows = _key_rows(kj)
            vb = v_ref[rows, :]
            dqs = list(dqs)
            dv_t = []
            for h in range(2):
                kb = k_ref[rows, _head_lanes(h)]
                s = _dot_nt(qhs[h], kb)
                if diag:
                    s = jnp.where(_causal_mask(), s, NEG)
                p = jnp.exp(s - lses[h])
                dp = _dot_nt(doms[h], vb)
                ds = (p * (dp - deltas[h])).astype(BF16)
                dk_ref[kj, _head_lanes(h), :] += _dot_nn(qts[h], ds)
                dv_t.append(_dot_nn(dot_v, p.astype(BF16)))
                dqs[h] = dqs[h] + _dot_nn(ds, kb)
            dv_ref[kj] += _head_rows(*dv_t)
            return tuple(dqs)

        def trip(tiles, dqs):
            for kj in tiles:
                dqs = block(kj, dqs, False)
            return dqs

        za = jnp.zeros((TQ, LANES), F32)
        dqs = _over_key_tiles(qi, trip, (za, za), ascending=True)
        dqs = block(qi, dqs, True)
        dq_ref[:, _head_lanes(0)] = dqs[0] * MLA_SCALE
        dq_ref[:, _head_lanes(1)] = dqs[1] * MLA_SCALE

    return pl.pallas_call(
        body, name="mla_bwd", grid=(HEADS // 2, nq),
        out_shape=[jax.ShapeDtypeStruct((s_len, HEADS * LANES), F32),
                   jax.ShapeDtypeStruct((nq, HEADS * LANES, TQ), F32),
                   jax.ShapeDtypeStruct((nq, MLA_WIDTH, TQ), F32)],
        in_specs=[pl.BlockSpec((TQ, 2 * LANES), lambda p, i: (i, p)),
                  pl.BlockSpec((2 * LANES, TQ), lambda p, i: (p, i)),
                  pl.BlockSpec((s_len, 2 * LANES), lambda p, i: (0, p)),
                  pl.BlockSpec((s_len, LANES), lambda p, i: (0, p)),
                  pl.BlockSpec((TQ, LANES), lambda p, i: (i, p)),
                  pl.BlockSpec((TQ, LANES), lambda p, i: (i, p)),
                  pl.BlockSpec((LANES, TQ), lambda p, i: (p, i)),
                  pl.BlockSpec((2, TQ, 1), lambda p, i: (p, i, 0))],
        out_specs=[pl.BlockSpec((TQ, 2 * LANES), lambda p, i: (i, p)),
                   pl.BlockSpec((nq, 2 * LANES, TQ), lambda p, i: (0, p, 0)),
                   pl.BlockSpec((nq, LANES, TQ), lambda p, i: (0, p, 0))],
        compiler_params=pltpu.CompilerParams(dimension_semantics=("parallel", "arbitrary"),
                                             vmem_limit_bytes=VMEM_BIG),
    )(qm, qm_t, km, vm, o, do, do_t, lse)


def _mid(o_sb, g_sb, o_mla, g_mla, x, target, gate, w_out):
    s_len = x.shape[0]

    def body(osb_ref, gsb_ref, omla_ref, gmla_ref, x_ref, t_ref, gate_ref, w_ref,
             dy_ref, dosb_ref, dgsb_ref, domla_ref, dgmla_ref, gw_ref, dgate_ref, loss_ref, dosbt_ref, domlat_ref):
        @pl.when(pl.program_id(0) == 0)
        def _():
            gw_ref[...] = jnp.zeros_like(gw_ref)
            dgate_ref[...] = jnp.zeros_like(dgate_ref)
            loss_ref[...] = jnp.zeros_like(loss_ref)

        o1, g1, o2, g2 = osb_ref[...], gsb_ref[...], omla_ref[...], gmla_ref[...]
        s1, s2 = _sigmoid(g1), _sigmoid(g2)
        mixed = jnp.concatenate([o1 * (g1 * s1), o2 * (g2 * s2)], axis=1).astype(BF16)
        w = w_ref[...]
        gate_v = gate_ref[...]
        u = _dot_nn(mixed, w)
        err = x_ref[...] + gate_v * u - t_ref[...]
        loss_ref[...] += jnp.sum(err * err, axis=0, keepdims=True)
        dy = err * (1.0 / D_MODEL)
        dy_ref[...] = dy
        dgate_ref[...] += jnp.sum(dy * u, axis=0, keepdims=True)
        du = (dy * gate_v).astype(BF16)
        gw_ref[...] += _dot_tn(mixed, du)
        dmixed = _dot_nt(du, w)
        d1, d2 = dmixed[:, :SB_WIDTH], dmixed[:, SB_WIDTH:]
        do1, do2 = d1 * (g1 * s1), d2 * (g2 * s2)
        dosb_ref[...] = do1.astype(BF16)
        dgsb_ref[...] = (d1 * o1 * (s1 * (1.0 + g1 * (1.0 - s1)))).astype(BF16)
        domla_ref[...] = do2.astype(BF16)
        dgmla_ref[...] = (d2 * o2 * (s2 * (1.0 + g2 * (1.0 - s2)))).astype(BF16)
        dosbt_ref[...] = do1.T.astype(BF16)
        domlat_ref[...] = do2.T.astype(BF16)

    def tile(width):
        return pl.BlockSpec((TS, width), lambda i: (i, 0))

    def full(shape):
        return pl.BlockSpec(shape, lambda i: (0, 0))

    return pl.pallas_call(
        body, name="mid", grid=(s_len // TS,),
        out_shape=[jax.ShapeDtypeStruct((s_len, D_MODEL), F32)]
        + [jax.ShapeDtypeStruct((s_len, 512), BF16)] * 4
        + [jax.ShapeDtypeStruct((D_MODEL, D_MODEL), F32),
           jax.ShapeDtypeStruct((1, D_MODEL), F32), jax.ShapeDtypeStruct((1, D_MODEL), F32)]
        + [jax.ShapeDtypeStruct((512, s_len), BF16)] * 2,
        in_specs=[tile(512)] * 4 + [tile(D_MODEL), tile(D_MODEL), full((1, D_MODEL)), full((D_MODEL, D_MODEL))],
        out_specs=[tile(D_MODEL)] + [tile(512)] * 4
        + [full((D_MODEL, D_MODEL)), full((1, D_MODEL)), full((1, D_MODEL))]
        + [pl.BlockSpec((512, TS), lambda i: (0, i))] * 2,
        compiler_params=pltpu.CompilerParams(dimension_semantics=("arbitrary",), vmem_limit_bytes=VMEM_BIG),
    )(o_sb, g_sb, o_mla, g_mla, x, target, gate, w_out)


def _mla_pre_bwd(dq, dk, dv, cq, ckv, kr, qln, wuq, kvln, wukv, qhn, khn, cosf, sin_a, sin_b):
    s_len = cq.shape[0]

    def body(dq_ref, dk_ref, dv_ref, cq_ref, ckv_ref, kr_ref, qln_ref, wuq_ref, kvln_ref, wukv_ref,
             qhn_ref, khn_ref, cos_ref, sa_ref, sb_ref,
             dcq_ref, dckv_ref, dkr_ref, gwuq_ref, gwukv_ref, gqhn_ref, gkhn_ref, gqln_ref, gkvln_ref,
             dqa_ref, dkv_ref):
        @pl.when(pl.program_id(0) == 0)
        def _():
            for r_ in (gwuq_ref, gwukv_ref, gqhn_ref, gkhn_ref, gqln_ref, gkvln_ref):
                r_[...] = jnp.zeros_like(r_)

        cq, ckv = cq_ref[...], ckv_ref[...]
        qln_v, kvln_v = qln_ref[...], kvln_ref[...]
        wuq_v, wukv_v = wuq_ref[...], wukv_ref[...]
        rq, rkv, cq_hat, ckv_hat, cqn, ckvn, q_all, kv, kr64 = _mla_latents(
            cq, ckv, kr_ref[...], qln_v, kvln_v, wuq_v, wukv_v)
        cosf, sa, sb = cos_ref[...], sa_ref[...], sb_ref[...]
        qhn_v, khn_v = qhn_ref[...], khn_ref[...]
        lane = _lane_iota((TS, LANES))
        low = lane < NOPE
        g_qhn = jnp.zeros((1, LANES), F32)
        g_khn = jnp.zeros((1, LANES), F32)
        dkr64 = jnp.zeros((TS, LANES), F32)
        for hd in range(HEADS):
            blk = slice(LANES * hd, LANES * (hd + 1))
            qb = q_all[:, blk]
            r = _head_rms(qb)
            xh = qb * r
            dn = _rope_t(dq_ref[:, blk], cosf, sa, sb)
            g_qhn = g_qhn + jnp.sum(dn * xh, axis=0, keepdims=True)
            dqa_ref[:, blk] = _rms_bwd(dn * qhn_v, xh, r, MLA_QK).astype(BF16)

            kb = jnp.where(low, kv[:, blk], kr64)
            r = _head_rms(kb)
            xh = kb * r
            dn = _rope_t(dk_ref[:, blk], cosf, sa, sb)
            g_khn = g_khn + jnp.sum(dn * xh, axis=0, keepdims=True)
            dkb = _rms_bwd(dn * khn_v, xh, r, MLA_QK)
            dkr64 = dkr64 + jnp.where(low, 0.0, dkb)
            dvp = dv_ref[:, LANES * (hd // 2):LANES * (hd // 2 + 1)]
            dvh = pltpu.roll(dvp, 64, 1) if hd % 2 == 0 else dvp
            dkv_ref[:, blk] = jnp.where(low, dkb, dvh).astype(BF16)
        gqhn_ref[...] += g_qhn
        gkhn_ref[...] += g_khn
        dkr_ref[...] = pltpu.roll(dkr64, 64, 1).astype(BF16)

        dqa = dqa_ref[...]
        gwuq_ref[...] += _dot_tn(cqn, dqa)
        dcqn = _dot_nt(dqa, wuq_v)
        gqln_ref[...] += jnp.sum(dcqn * cq_hat, axis=0, keepdims=True)
        dcq_ref[...] = _rms_bwd(dcqn * qln_v, cq_hat, rq, Q_LORA).astype(BF16)

        dkv = dkv_ref[...]
        gwukv_ref[...] += _dot_tn(ckvn, dkv)
        dckvn = _dot_nt(dkv, wukv_v)
        gkvln_ref[...] += jnp.sum(dckvn * ckv_hat, axis=0, keepdims=True)
        dckv_ref[...] = _rms_bwd(dckvn * kvln_v, ckv_hat, rkv, KV_LORA).astype(BF16)

    def tile(width):
        return pl.BlockSpec((TS, width), lambda i: (i, 0))

    def full(shape):
        return pl.BlockSpec(shape, lambda i: (0, 0))

    acc_shapes = [(Q_LORA, 1024), (KV_LORA, 1024), (1, LANES), (1, LANES), (1, Q_LORA), (1, KV_LORA)]
    return pl.pallas_call(
        body, name="mla_pre_bwd", grid=(s_len // TS,),
        out_shape=[jax.ShapeDtypeStruct((s_len, Q_LORA), BF16), jax.ShapeDtypeStruct((s_len, KV_LORA), BF16),
                   jax.ShapeDtypeStruct((s_len, LANES), BF16)]
        + [jax.ShapeDtypeStruct(s, F32) for s in acc_shapes],
        in_specs=[tile(1024), tile(1024), tile(512), tile(Q_LORA), tile(KV_LORA), tile(LANES),
                  full(qln.shape), full(wuq.shape), full(kvln.shape), full(wukv.shape),
                  full(qhn.shape), full(khn.shape), tile(LANES), tile(LANES), tile(LANES)],
        out_specs=[tile(Q_LORA), tile(KV_LORA), tile(LANES)] + [full(s) for s in acc_shapes],
        scratch_shapes=[pltpu.VMEM((TS, 1024), BF16), pltpu.VMEM((TS, 1024), BF16)],
        compiler_params=pltpu.CompilerParams(dimension_semantics=("arbitrary",), vmem_limit_bytes=VMEM_BIG),
    )(dq, dk, dv, cq, ckv, kr, qln, wuq, kvln, wukv, qhn, khn, cosf, sin_a, sin_b)


def _dproj_bwd(dq_sb, dk_sb, dv_sb, dg_sb, dcq, dckv, dg_mla, dkr, w_in_r, x, dy, norm_w, scale):
    s_len = x.shape[0]

    def body(dq_ref, dk_ref, dv_ref, dg_ref, dcq_ref, dckv_ref, dgm_ref, dkr_ref, w_ref, x_ref, dy_ref,
             nw_ref, scale_ref, dp_ref, gx_ref, dshift_ref, dscale_ref, dnw_ref):
        @pl.when(pl.program_id(0) == 0)
        def _():
            for r_ in (dshift_ref, dscale_ref, dnw_ref):
                r_[...] = jnp.zeros_like(r_)

        dp_ref[:, 0:512] = dq_ref[...].astype(BF16)
        dp_ref[:, 512:1024] = dk_ref[...].astype(BF16)
        dp_ref[:, 1024:1536] = dv_ref[...].astype(BF16)
        dp_ref[:, 1536:2048] = dg_ref[...]
        dp_ref[:, 2048:2432] = dcq_ref[...]
        dp_ref[:, 2432:2688] = dckv_ref[...]
        dp_ref[:, 2688:3200] = dgm_ref[...]
        dp_ref[:, 3200:3328] = dkr_ref[...]
        dh = _dot_nt(dp_ref[...], w_ref[...])
        xv = x_ref[...]
        r = lax.rsqrt(jnp.mean(xv * xv, axis=1, keepdims=True) + EPS)
        xh = xv * r
        nw = nw_ref[...]
        dshift_ref[...] += jnp.sum(dh, axis=0, keepdims=True)
        dscale_ref[...] += jnp.sum(dh * (xh * nw), axis=0, keepdims=True)
        dxnw = dh * (1.0 + scale_ref[...])
        dnw_ref[...] += jnp.sum(dxnw * xh, axis=0, keepdims=True)
        gx_ref[...] = dy_ref[...] + _rms_bwd(dxnw * nw, xh, r, D_MODEL)

    def tile(width):
        return pl.BlockSpec((TS, width), lambda i: (i, 0))

    def full(shape):
        return pl.BlockSpec(shape, lambda i: (0, 0))

    vec = (1, D_MODEL)
    return pl.pallas_call(
        body, name="dproj_bwd", grid=(s_len // TS,),
        out_shape=[jax.ShapeDtypeStruct((s_len, IN_COLS_R), BF16), jax.ShapeDtypeStruct((s_len, D_MODEL), F32)]
        + [jax.ShapeDtypeStruct(vec, F32)] * 3,
        in_specs=[tile(512)] * 4 + [tile(Q_LORA), tile(KV_LORA), tile(512), tile(LANES),
                                    full(w_in_r.shape), tile(D_MODEL), tile(D_MODEL), full(vec), full(vec)],
        out_specs=[tile(IN_COLS_R), tile(D_MODEL)] + [full(vec)] * 3,
        compiler_params=pltpu.CompilerParams(dimension_semantics=("arbitrary",), vmem_limit_bytes=VMEM_BIG),
    )(dq_sb, dk_sb, dv_sb, dg_sb, dcq, dckv, dg_mla, dkr, w_in_r, x, dy, norm_w, scale)


def _grad_w_in(hb, dproj):
    s_len = hb.shape[0]
    n_half = IN_COLS_R // 2

    def body(h_ref, d_ref, g_ref):
        @pl.when(pl.program_id(1) == 0)
        def _():
            g_ref[...] = jnp.zeros_like(g_ref)

        g_ref[...] += _dot_tn(h_ref[...], d_ref[...])

    return pl.pallas_call(
        body, name="grad_w_in", grid=(2, s_len // TN_S),
        out_shape=jax.ShapeDtypeStruct((D_MODEL, IN_COLS_R), F32),
        in_specs=[pl.BlockSpec((TN_S, D_MODEL), lambda n, s: (s, 0)),
                  pl.BlockSpec((TN_S, n_half), lambda n, s: (s, n))],
        out_specs=pl.BlockSpec((D_MODEL, n_half), lambda n, s: (0, n)),
        compiler_params=pltpu.CompilerParams(dimension_semantics=("parallel", "arbitrary"),
                                             vmem_limit_bytes=VMEM_BIG),
    )(hb, dproj)


def _final_exchange(gpack, ccol, wpack, mpack, vpack, w_ada, m_ada, v_ada, grads):
    n_sh = w_ada.shape[1]
    n = len(grads)

    def body(*refs):
        (g_ref, cc_ref, wp_ref, mp_ref, vp_ref, wa_ref, ma_ref, va_ref) = refs[:8]
        slabs_in = refs[8:8 + n]
        (og_ref, od_ref, om_ref, ov_ref, ag_ref, ad_ref, am_ref, av_ref) = refs[8 + n:16 + n]
        slabs_out = refs[16 + n:16 + 2 * n]
        gall_ref, ssem, rsem, slab_ssem, slab_rsem, lsem = refs[16 + 2 * n:]
        pos = _mesh_pos()
        me = _lin(pos)
        gall_ref[me] = g_ref[...]
        small = _all_gather_start(pos, g_ref, gall_ref, ssem, rsem, 0)
        own = [pltpu.make_async_copy(slabs_in[a].at[me], slabs_out[a].at[me], lsem.at[a]) for a in range(n)]
        for cp in own:
            cp.start()
        in_flight = [_all_to_all_start(pos, slabs_in[a], slabs_out[a], slab_ssem, slab_rsem, a * (N_DEV - 1))
                     for a in range(n)]
        _all_gather_wait(pos, g_ref, gall_ref, ssem, rsem, 0, small)

        tot = gall_ref[0]
        for j in range(1, N_DEV):
            tot = tot + gall_ref[j]
        og_ref[...] = tot
        od_ref[...], om_ref[...], ov_ref[...] = _adamw(wp_ref[...], tot, mp_ref[...], vp_ref[...])

        ga = jnp.zeros((D_MODEL, n_sh), F32)
        for j in range(N_DEV):
            d_mine = jnp.zeros((8, n_sh), F32)
            for k in range(N_DEV):
                d_mine = d_mine + jnp.where(me == k, gall_ref[j, :, PK_ADA + n_sh * k:PK_ADA + n_sh * (k + 1)], 0.0)
            col = _silu(cc_ref[j])
            ga = ga + jnp.concatenate(
                [col * d_mine[0:1, LANES * a:LANES * (a + 1)] for a in range(n_sh // LANES)], axis=1)
        ag_ref[...] = ga
        ad_ref[...], am_ref[...], av_ref[...] = _adamw(wa_ref[...], ga, ma_ref[...], va_ref[...])

        for a in range(n):
            _all_to_all_wait(pos, slabs_in[a], slabs_out[a], slab_ssem, slab_rsem, a * (N_DEV - 1), in_flight[a])
        for cp in own:
            cp.wait()

    pk = jax.ShapeDtypeStruct((8, PK_END), F32)
    ada = jax.ShapeDtypeStruct((D_MODEL, n_sh), F32)
    return pl.pallas_call(
        body, name="final_exchange",
        out_shape=[pk] * 4 + [ada] * 4 + [jax.ShapeDtypeStruct(g.shape, g.dtype) for g in grads],
        in_specs=[_vmem_spec()] * 8 + [_any_spec()] * n,
        out_specs=[_vmem_spec()] * 8 + [_any_spec()] * n,
        scratch_shapes=[
            pltpu.VMEM((N_DEV, 8, PK_END), F32),
            pltpu.SemaphoreType.DMA((N_DEV - 1,)),
            pltpu.SemaphoreType.DMA((N_DEV - 1,)),
            pltpu.SemaphoreType.DMA((n * (N_DEV - 1),)),
            pltpu.SemaphoreType.DMA((n * (N_DEV - 1),)),
            pltpu.SemaphoreType.DMA((n,)),
        ],
        compiler_params=pltpu.CompilerParams(vmem_limit_bytes=VMEM_BIG),
    )(gpack, ccol, wpack, mpack, vpack, w_ada, m_ada, v_ada, *grads)


def _adamw_reduce(name, parts, w, m, v, row_tile):
    rows, cols = w.shape

    def body(p_ref, w_ref, m_ref, v_ref, g_ref, d_ref, mo_ref, vo_ref):
        g = p_ref[0].astype(F32)
        for j in range(1, N_DEV):
            g = g + p_ref[j].astype(F32)
        g_ref[...] = g
        d_ref[...], mo_ref[...], vo_ref[...] = _adamw(w_ref[...], g, m_ref[...], v_ref[...])

    tile = pl.BlockSpec((row_tile, cols), lambda i: (i, 0))
    return pl.pallas_call(
        body, name=name, grid=(rows // row_tile,),
        out_shape=[jax.ShapeDtypeStruct((rows, cols), F32)] * 4,
        in_specs=[pl.BlockSpec((N_DEV, row_tile, cols), lambda i: (0, i, 0)), tile, tile, tile],
        out_specs=[tile] * 4,
        compiler_params=pltpu.CompilerParams(dimension_semantics=("parallel",), vmem_limit_bytes=VMEM_BIG),
    )(parts, w, m, v)


def _rope_tables(positions):
    inv_freq = 10000.0 ** (-jnp.arange(0, ROPE, 2, dtype=F32) / ROPE)
    ang = positions.astype(F32)[:, None] * inv_freq
    cos, sin = jnp.cos(ang), jnp.sin(ang)
    s_len = positions.shape[0]
    ones = jnp.ones((s_len, NOPE), F32)
    zeros = jnp.zeros((s_len, NOPE), F32)
    z16 = jnp.zeros((s_len, ROPE // 2), F32)
    pad1 = jnp.ones((s_len, LANES - MLA_QK), F32)
    pad0 = jnp.zeros((s_len, LANES - MLA_QK), F32)
    cosf = jnp.concatenate([ones, cos, cos, pad1], axis=1)
    sin_a = jnp.concatenate([zeros, -sin, z16, pad0], axis=1)
    sin_b = jnp.concatenate([zeros, z16, sin, pad0], axis=1)
    return cosf, sin_a, sin_b


def _rearrange_cols(w):
    pad = jnp.zeros((w.shape[0], IN_COLS_R - IN_COLS), w.dtype)
    return jnp.concatenate([w[:, :2688], w[:, 2720:3232], w[:, 2688:2720], pad], axis=1)


def _restore_cols(g):
    return jnp.concatenate([g[:, :2688], g[:, 3200:3232], g[:, 2688:3200]], axis=1)


def _pad_heads(w):
    rows = w.shape[0]
    w = w.reshape(rows, HEADS, MLA_QK)
    return jnp.pad(w, ((0, 0), (0, 0), (0, LANES - MLA_QK))).reshape(rows, HEADS * LANES)


def _unpad_heads(g):
    rows = g.shape[0]
    return g.reshape(rows, HEADS, LANES)[:, :, :MLA_QK].reshape(rows, HEADS * MLA_QK)


def _pad_lanes(v):
    return jnp.pad(v, ((0, 0), (0, LANES - v.shape[1])))


def _col_shards(g):
    rows = g.shape[0]
    return g.reshape(rows, N_DEV, g.shape[1] // N_DEV).transpose(1, 0, 2)


def _from_col_shards(g):
    return g.transpose(1, 0, 2).reshape(g.shape[1], N_DEV * g.shape[2])


def _pack(norm_w, qln, kvln, qhn, khn, ada, loss_lanes=None):
    if loss_lanes is None:
        loss_lanes = jnp.zeros((1, PK_END - PK_LOSS), F32)
    row = jnp.concatenate([norm_w, qln, kvln, _pad_lanes(qhn), _pad_lanes(khn), ada, loss_lanes], axis=1)
    return jnp.broadcast_to(row, (8, PK_END))


def _unpack(p):
    row = p[0:1]
    return (row[:, PK_NORM:PK_QLN], row[:, PK_QLN:PK_KVLN], row[:, PK_KVLN:PK_QHN],
            row[:, PK_QHN:PK_QHN + MLA_QK], row[:, PK_KHN:PK_KHN + MLA_QK], row[:, PK_ADA:PK_LOSS])


def kernel(x, c, positions, w_ada, b_ada, norm_w, w_in, q_lora_norm, w_uq, kv_lora_norm, w_ukv, q_head_norm, k_head_norm, w_out, loss_target, m_w_ada, m_b_ada, m_norm_w, m_w_in, m_q_lora_norm, m_w_uq, m_kv_lora_norm, m_w_ukv, m_q_head_norm, m_k_head_norm, m_w_out, v_w_ada, v_b_ada, v_norm_w, v_w_in, v_q_lora_norm, v_w_uq, v_kv_lora_norm, v_w_ukv, v_q_head_norm, v_k_head_norm, v_w_out):
    s_len = x.shape[1]
    x2 = x.reshape(s_len, D_MODEL)
    tgt = loss_target.reshape(s_len, D_MODEL)
    w_ada_s, w_in_s, w_uq_s, w_ukv_s, w_out_s = w_ada[0], w_in[0], w_uq[0], w_ukv[0], w_out[0]

    ada8, c_all = _ada_fwd(jnp.broadcast_to(c, (8, D_MODEL)), w_ada_s, b_ada.reshape(N_DEV, -1))
    ada = ada8.reshape(1, 3 * D_MODEL)
    shift, scale, gate = ada[:, :D_MODEL], ada[:, D_MODEL:2 * D_MODEL], ada[:, 2 * D_MODEL:]

    g_uq, g_ukv, g_out, g_in = _gather_weights([w_uq_s, w_ukv_s, w_out_s, w_in_s])
    w_in_r = _rearrange_cols(_from_col_shards(g_in))
    wuq_p = _pad_heads(_from_col_shards(g_uq))
    wukv_f = _from_col_shards(g_ukv)
    w_out_f = g_out.reshape(D_MODEL, D_MODEL)

    cosf, sin_a, sin_b = _rope_tables(positions[0])
    qhn_p, khn_p = _pad_lanes(q_head_norm), _pad_lanes(k_head_norm)

    hb, qkv, g_sb, cq, ckv, g_mla, kr, qm, km, vm, q_sb_t, qm_t = _fwd_pre(
        x2, shift, scale, norm_w, w_in_r, q_lora_norm, wuq_p, kv_lora_norm, wukv_f, qhn_p, khn_p,
        cosf, sin_a, sin_b)
    o_sb, w_saved, l_saved = _sb_fwd(qkv)
    o_mla, lse = _mla_fwd(qm, km, vm)

    dy, do_sb, dg_sb, do_mla, dg_mla, gw_out, d_gate, loss_acc, do_sb_t, do_mla_t = _mid(
        o_sb, g_sb, o_mla, g_mla, x2, tgt, gate, w_out_f)

    dq_sb, dk_sb_t, dv_sb_t = _sb_bwd(qkv, q_sb_t, do_sb, do_sb_t, w_saved, l_saved)
    dk_sb, dv_sb = _from_key_tiles(dk_sb_t), _from_key_tiles(dv_sb_t)
    dq_m, dk_m_t, dv_m_t = _mla_bwd(qm, qm_t, km, vm, o_mla, do_mla, do_mla_t, lse)
    dk_m, dv_m = _from_key_tiles(dk_m_t), _from_key_tiles(dv_m_t)
    dcq, dckv, dkr, gw_uq_p, gw_ukv, g_qhn, g_khn, g_qln, g_kvln = _mla_pre_bwd(
        dq_m, dk_m, dv_m, cq, ckv, kr, q_lora_norm, wuq_p, kv_lora_norm, wukv_f, qhn_p, khn_p,
        cosf, sin_a, sin_b)
    dproj, grad_x, d_shift, d_scale, g_norm_w = _dproj_bwd(
        dq_sb, dk_sb, dv_sb, dg_sb, dcq, dckv, dg_mla, dkr, w_in_r, x2, dy, norm_w, scale)
    gw_in = _restore_cols(_grad_w_in(hb, dproj))

    d_ada = jnp.concatenate([d_shift, d_scale, d_gate], axis=1)
    gpack = _pack(g_norm_w, g_qln, g_kvln, g_qhn[:, :MLA_QK], g_khn[:, :MLA_QK], d_ada, loss_acc)
    wpack = _pack(norm_w, q_lora_norm, kv_lora_norm, q_head_norm, k_head_norm, b_ada)
    mpack = _pack(m_norm_w, m_q_lora_norm, m_kv_lora_norm, m_q_head_norm, m_k_head_norm, m_b_ada)
    vpack = _pack(v_norm_w, v_q_lora_norm, v_kv_lora_norm, v_q_head_norm, v_k_head_norm, v_b_ada)
    ccol = jnp.broadcast_to(c_all[:, :, None], (N_DEV, D_MODEL, LANES))
    slabs = [g.astype(BF16) for g in (
        _col_shards(gw_in), _col_shards(_unpad_heads(gw_uq_p)), _col_shards(gw_ukv),
        gw_out.reshape(N_DEV, D_MODEL // N_DEV, D_MODEL))]
    pg, pd, pm, pv, ada_g, ada_d, ada_m, ada_v, r_in, r_uq, r_ukv, r_out = _final_exchange(
        gpack, ccol, wpack, mpack, vpack, w_ada_s, m_w_ada[0], v_w_ada[0], slabs)
    loss = 0.5 * jnp.sum(pg[0, PK_LOSS:PK_END]) / D_MODEL

    in_g, in_d, in_m, in_v = _adamw_reduce("adamw_w_in", r_in, w_in_s, m_w_in[0], v_w_in[0], 256)
    uq_g, uq_d, uq_m, uq_v = _adamw_reduce("adamw_w_uq", r_uq, w_uq_s, m_w_uq[0], v_w_uq[0], w_uq_s.shape[0])
    ukv_g, ukv_d, ukv_m, ukv_v = _adamw_reduce(
        "adamw_w_ukv", r_ukv, w_ukv_s, m_w_ukv[0], v_w_ukv[0], w_ukv_s.shape[0])
    out_g, out_d, out_m, out_v = _adamw_reduce(
        "adamw_w_out", r_out, w_out_s, m_w_out[0], v_w_out[0], w_out_s.shape[0])

    def group(ada_t, pk, in_t, uq_t, ukv_t, out_t):
        nw, qln, kvln, qhn, khn, b = _unpack(pk)
        return (ada_t[None], b, nw, in_t[None], qln, uq_t[None], kvln, ukv_t[None], qhn, khn, out_t[None])

    return (loss, grad_x.reshape(1, s_len, D_MODEL),
            *group(ada_g, pg, in_g, uq_g, ukv_g, out_g),
            *group(ada_d, pd, in_d, uq_d, ukv_d, out_d),
            *group(ada_m, pm, in_m, uq_m, ukv_m, out_m),
            *group(ada_v, pv, in_v, uq_v, ukv_v, out_v))
```

```python
import functools
import math

import jax
import jax.numpy as jnp
from jax import lax
from jax.experimental import pallas as pl
from jax.experimental.pallas import tpu as pltpu

F32 = jnp.float32
BF16 = jnp.bfloat16

N_DEV = 8
D_MODEL = 1024
HEADS = 8
SB_WIDTH = 512
MLA_WIDTH = 512
Q_LORA = 384
KV_LORA = 256
ROPE = 32
NOPE = 64
MLA_QK = 96
LANES = 128
IN_COLS = 3232
IN_COLS_R = 3328
EPS = 1e-6
NEG = -1e30

ADAM_LR = 0.001
ADAM_B1 = 0.9
ADAM_B2 = 0.999
ADAM_EPS = 1e-08
ADAM_WD = 0.01
ADAM_STEP = 10

TS = 512
TS_FWD = 256
TQ = 512
KEY_UNROLL = 2
TN_S = 512
VMEM_BIG = 56 * 1024 * 1024

PK_NORM, PK_QLN, PK_KVLN, PK_QHN, PK_KHN, PK_ADA, PK_LOSS, PK_END = 0, 1024, 1408, 1664, 1792, 1920, 4992, 6016

MESH_ID = pl.DeviceIdType.MESH


def _dot_nn(a, b):
    return lax.dot_general(a, b, (((1,), (0,)), ((), ())), preferred_element_type=F32)


def _dot_nt(a, b):
    return lax.dot_general(a, b, (((1,), (1,)), ((), ())), preferred_element_type=F32)


def _dot_tn(a, b):
    return lax.dot_general(a, b, (((0,), (0,)), ((), ())), preferred_element_type=F32)


def _split_bf16(a):
    hi = a.astype(BF16)
    lo = (a - hi.astype(F32)).astype(BF16)
    return hi, lo


def _dot3(a, b):
    ah, al = _split_bf16(a)
    bh, bl = _split_bf16(b)
    return _dot_nn(ah, bh) + _dot_nn(ah, bl) + _dot_nn(al, bh)


def _sigmoid(g):
    return 1.0 / (1.0 + jnp.exp(-g))


def _silu(g):
    return g * _sigmoid(g)


def _lane_iota(shape):
    return lax.broadcasted_iota(jnp.int32, shape, len(shape) - 1)


def _adamw(w, g, m, v):
    m = ADAM_B1 * m + (1.0 - ADAM_B1) * g
    v = ADAM_B2 * v + (1.0 - ADAM_B2) * (g * g)
    m_hat = m / (1.0 - ADAM_B1 ** ADAM_STEP)
    v_hat = v / (1.0 - ADAM_B2 ** ADAM_STEP)
    delta = -ADAM_LR * (m_hat / (jnp.sqrt(v_hat) + ADAM_EPS) + ADAM_WD * w)
    return delta, m, v


def _mesh_pos():
    return lax.axis_index("x"), lax.axis_index("y"), lax.axis_index("c")


def _peer(pos, k):
    x, y, c = pos
    return (1 - x if k & 4 else x, 1 - y if k & 2 else y, 1 - c if k & 1 else c)


def _lin(pos):
    return 4 * pos[0] + 2 * pos[1] + pos[2]


def _remote(src, dst, send_sems, recv_sems, idx, peer):
    return pltpu.make_async_remote_copy(
        src_ref=src, dst_ref=dst, send_sem=send_sems.at[idx], recv_sem=recv_sems.at[idx],
        device_id=peer, device_id_type=MESH_ID)


def _all_gather_start(pos, src, buf, send_sems, recv_sems, base):
    me = _lin(pos)
    sent = []
    for k in range(1, N_DEV):
        cp = _remote(src, buf.at[me], send_sems, recv_sems, base + k - 1, _peer(pos, k))
        cp.start()
        sent.append(cp)
    return sent


def _all_gather_wait(pos, src, buf, send_sems, recv_sems, base, sent):
    for k in range(1, N_DEV):
        peer = _peer(pos, k)
        _remote(src, buf.at[_lin(peer)], send_sems, recv_sems, base + k - 1, peer).wait_recv()
    for cp in sent:
        cp.wait_send()


def _all_gather(pos, src, buf, send_sems, recv_sems, base):
    sent = _all_gather_start(pos, src, buf, send_sems, recv_sems, base)
    _all_gather_wait(pos, src, buf, send_sems, recv_sems, base, sent)


def _all_to_all_start(pos, src, buf, send_sems, recv_sems, base):
    me = _lin(pos)
    sent = []
    for k in range(1, N_DEV):
        peer = _peer(pos, k)
        cp = _remote(src.at[_lin(peer)], buf.at[me], send_sems, recv_sems, base + k - 1, peer)
        cp.start()
        sent.append(cp)
    return sent


def _all_to_all_wait(pos, src, buf, send_sems, recv_sems, base, sent):
    me = _lin(pos)
    for k in range(1, N_DEV):
        peer = _peer(pos, k)
        _remote(src.at[me], buf.at[_lin(peer)], send_sems, recv_sems, base + k - 1, peer).wait_recv()
    for cp in sent:
        cp.wait_send()


def _all_to_all(pos, src, buf, send_sems, recv_sems, base):
    sent = _all_to_all_start(pos, src, buf, send_sems, recv_sems, base)
    _all_to_all_wait(pos, src, buf, send_sems, recv_sems, base, sent)


def _two_level_gather(pos, bufs, send_sems, recv_sems):
    x, y, c = pos
    me, sibling = (x, y, c), (x, y, 1 - c)
    chips = [(1 - x, y), (x, 1 - y), (1 - x, 1 - y)]

    def copy(a, k, block, to):
        slot = bufs[a].at[_lin(block)]
        return _remote(slot, slot, send_sems, recv_sems, 7 * a + k, to)

    started = []
    for a in range(len(bufs)):
        first = [copy(a, 0, me, sibling)] + [copy(a, 1 + j, me, (*chip, c)) for j, chip in enumerate(chips)]
        for cp in first:
            cp.start()
        started += first
    for a in range(len(bufs)):
        for j, chip in enumerate(chips):
            copy(a, 1 + j, (*chip, c), me).wait_recv()
            passed = copy(a, 4 + j, (*chip, c), sibling)
            passed.start()
            started.append(passed)
    for a in range(len(bufs)):
        copy(a, 0, sibling, me).wait_recv()
        for j, chip in enumerate(chips):
            copy(a, 4 + j, (*chip, 1 - c), me).wait_recv()
    for cp in started:
        cp.wait_send()


def _vmem_spec():
    return pl.BlockSpec(memory_space=pltpu.VMEM)


def _any_spec():
    return pl.BlockSpec(memory_space=pl.ANY)


def _row_select(slots, n):
    r = lax.broadcasted_iota(jnp.int32, (N_DEV, n), 0)
    out = jnp.zeros((N_DEV, n), F32)
    for j in range(N_DEV):
        out = out + jnp.where(r == j, slots[j], 0.0)
    return out


def _ada_fwd(c8, w_ada, b_ada8):
    n_sh = w_ada.shape[1]

    def body(c_ref, w_ref, b_ref, out_ref, call_out_ref, call_ref, psend_ref, precv_ref, ssem, rsem):
        pos = _mesh_pos()
        me = _lin(pos)
        call_ref[me] = c_ref[...]
        _all_gather(pos, c_ref, call_ref, ssem, rsem, 0)
        call_out_ref[...] = _row_select([call_ref[j] for j in range(N_DEV)], D_MODEL)
        w = w_ref[...]
        for j in range(N_DEV):
            psend_ref[j] = _dot3(_silu(call_ref[j]), w)
        precv_ref[me] = psend_ref[me]
        _all_to_all(pos, psend_ref, precv_ref, ssem, rsem, N_DEV - 1)
        out_ref[...] = _row_select([precv_ref[j] for j in range(N_DEV)], n_sh) + b_ref[...]

    return pl.pallas_call(
        body, name="ada_fwd",
        out_shape=[jax.ShapeDtypeStruct((N_DEV, n_sh), F32), jax.ShapeDtypeStruct((N_DEV, D_MODEL), F32)],
        in_specs=[_vmem_spec()] * 3, out_specs=[_vmem_spec()] * 2,
        scratch_shapes=[
            pltpu.VMEM((N_DEV, 8, D_MODEL), F32),
            pltpu.VMEM((N_DEV, 8, n_sh), F32),
            pltpu.VMEM((N_DEV, 8, n_sh), F32),
            pltpu.SemaphoreType.DMA((2 * (N_DEV - 1),)),
            pltpu.SemaphoreType.DMA((2 * (N_DEV - 1),)),
        ],
    )(c8, w_ada, b_ada8)


def _gather_weights(shards):
    n = len(shards)

    def body(*refs):
        ins, outs = refs[:n], refs[n:2 * n]
        ssem, rsem = refs[2 * n], refs[2 * n + 1]
        pos = _mesh_pos()
        me = _lin(pos)
        for a in range(n):
            outs[a][me] = ins[a][...].astype(BF16)
        _two_level_gather(pos, outs, ssem, rsem)

    return pl.pallas_call(
        body, name="gather_weights",
        out_shape=[jax.ShapeDtypeStruct((N_DEV,) + s.shape, BF16) for s in shards],
        in_specs=[_vmem_spec()] * n, out_specs=[_vmem_spec()] * n,
        scratch_shapes=[
            pltpu.SemaphoreType.DMA((n * (N_DEV - 1),)),
            pltpu.SemaphoreType.DMA((n * (N_DEV - 1),)),
        ],
        compiler_params=pltpu.CompilerParams(vmem_limit_bytes=VMEM_BIG),
    )(*shards)


def _rope(t, cosf, sin_a, sin_b):
    return t * cosf + pltpu.roll(t, 112, 1) * sin_a + pltpu.roll(t, 16, 1) * sin_b


def _rope_t(d, cosf, sin_a, sin_b):
    return d * cosf + pltpu.roll(d * sin_a, 16, 1) + pltpu.roll(d * sin_b, 112, 1)


def _head_rms(t):
    return lax.rsqrt(jnp.sum(t * t, axis=1, keepdims=True) * (1.0 / MLA_QK) + EPS)


def _rms_bwd(dxhat_w, xhat, r, n):
    return r * (dxhat_w - xhat * (jnp.sum(dxhat_w * xhat, axis=1, keepdims=True) * (1.0 / n)))


def _mla_latents(cq, ckv, kr, qln, kvln, wuq, wukv):
    rq = lax.rsqrt(jnp.mean(cq * cq, axis=1, keepdims=True) + EPS)
    rkv = lax.rsqrt(jnp.mean(ckv * ckv, axis=1, keepdims=True) + EPS)
    cq_hat = cq * rq
    ckv_hat = ckv * rkv
    cqn = (cq_hat * qln).astype(BF16)
    ckvn = (ckv_hat * kvln).astype(BF16)
    q_all = _dot_nn(cqn, wuq)
    kv = _dot_nn(ckvn, wukv)
    kr64 = pltpu.roll(kr, 64, 1)
    return rq, rkv, cq_hat, ckv_hat, cqn, ckvn, q_all, kv, kr64


def _fwd_pre(x, shift, scale, norm_w, w_in_r, qln, wuq, kvln, wukv, qhn, khn, cosf, sin_a, sin_b):
    s_len = x.shape[0]

    def body(x_ref, shift_ref, scale_ref, nw_ref, w_ref, qln_ref, wuq_ref, kvln_ref, wukv_ref,
             qhn_ref, khn_ref, cos_ref, sa_ref, sb_ref,
             hb_ref, qkv_ref, gsb_ref, cq_ref, ckv_ref, gmla_ref, kr_ref, qm_ref, km_ref, vm_ref,
             qsbt_ref, qmt_ref):
        xv = x_ref[...]
        r = lax.rsqrt(jnp.mean(xv * xv, axis=1, keepdims=True) + EPS)
        h = (xv * r) * nw_ref[...] * (1.0 + scale_ref[...]) + shift_ref[...]
        hb = h.astype(BF16)
        hb_ref[...] = hb
        qkv = _dot_nn(hb, w_ref[:, 0:1536])
        qkv_ref[...] = qkv.astype(BF16)
        qsbt_ref[...] = qkv[:, :SB_WIDTH].T.astype(BF16)
        gsb_ref[...] = _dot_nn(hb, w_ref[:, 1536:2048])
        cq = _dot_nn(hb, w_ref[:, 2048:2432])
        ckv = _dot_nn(hb, w_ref[:, 2432:2688])
        gmla_ref[...] = _dot_nn(hb, w_ref[:, 2688:3200])
        kr = _dot_nn(hb, w_ref[:, 3200:3328])
        cq_ref[...] = cq
        ckv_ref[...] = ckv
        kr_ref[...] = kr
        _, _, _, _, _, _, q_all, kv, kr64 = _mla_latents(
            cq, ckv, kr, qln_ref[...], kvln_ref[...], wuq_ref[...], wukv_ref[...])
        cosf, sa, sb = cos_ref[...], sa_ref[...], sb_ref[...]
        qhn_v, khn_v = qhn_ref[...], khn_ref[...]
        low = _lane_iota((TS_FWD, LANES)) < NOPE
        blks = [slice(LANES * hd, LANES * (hd + 1)) for hd in range(HEADS)]
        q_raw = [q_all[:, b] for b in blks]
        k_raw = [jnp.where(low, kv[:, b], kr64) for b in blks]
        q_rms = [_head_rms(t) for t in q_raw]
        k_rms = [_head_rms(t) for t in k_raw]
        q_n = [t * r * (qhn_v * MLA_SCALE) for t, r in zip(q_raw, q_rms)]
        q_roped = [_rope(t, cosf, sa, sb) for t in q_n]
        kr_roped = _rope(kr64 * khn_v, cosf, sa, sb)
        k_roped = [jnp.where(low, t * khn_v, kr_roped) * r for t, r in zip(k_raw, k_rms)]
        for hd, b in enumerate(blks):
            qm_ref[:, b] = q_roped[hd].astype(BF16)
            qmt_ref[b, :] = q_roped[hd].T.astype(BF16)
            km_ref[:, b] = k_roped[hd].astype(BF16)
        for p in range(HEADS // 2):
            even = kv[:, LANES * 2 * p:LANES * (2 * p + 1)]
            odd = kv[:, LANES * (2 * p + 1):LANES * (2 * p + 2)]
            vm_ref[:, LANES * p:LANES * (p + 1)] = jnp.where(low, pltpu.roll(even, 64, 1), odd).astype(BF16)

    def tile(width):
        return pl.BlockSpec((TS_FWD, width), lambda i: (i, 0))

    def full(a):
        return pl.BlockSpec(a.shape, lambda i: (0, 0))

    out_widths = [(D_MODEL, BF16), (1536, BF16), (512, F32), (Q_LORA, F32), (KV_LORA, F32),
                  (512, F32), (LANES, F32), (1024, BF16), (1024, BF16), (512, BF16)]
    t_heights = [SB_WIDTH, HEADS * LANES]
    return pl.pallas_call(
        body, name="fwd_pre", grid=(s_len // TS_FWD,),
        out_shape=[jax.ShapeDtypeStruct((s_len, w), dt) for w, dt in out_widths]
        + [jax.ShapeDtypeStruct((hgt, s_len), BF16) for hgt in t_heights],
        in_specs=[tile(D_MODEL), full(shift), full(scale), full(norm_w), full(w_in_r), full(qln), full(wuq),
                  full(kvln), full(wukv), full(qhn), full(khn), tile(LANES), tile(LANES), tile(LANES)],
        out_specs=[tile(w) for w, _ in out_widths]
        + [pl.BlockSpec((hgt, TS_FWD), lambda i: (0, i)) for hgt in t_heights],
        compiler_params=pltpu.CompilerParams(dimension_semantics=("parallel",), vmem_limit_bytes=VMEM_BIG),
    )(x, shift, scale, norm_w, w_in_r, qln, wuq, kvln, wukv, qhn, khn, cosf, sin_a, sin_b)


CUM_W = 256


def _tri(strict):
    j = lax.broadcasted_iota(jnp.int32, (CUM_W, CUM_W), 0)
    s = lax.broadcasted_iota(jnp.int32, (CUM_W, CUM_W), 1)
    return (j > s if strict else j >= s).astype(BF16)


def _suffix_sums(a, a_bf16, tri_m, carry):
    n = a.shape[1] // CUM_W
    outs = [None] * n
    for i in reversed(range(n)):
        cols = slice(CUM_W * i, CUM_W * (i + 1))
        outs[i] = _dot_nn(a_bf16[:, cols], tri_m) + carry
        carry = carry + _rowsum(a[:, cols])
    return (outs[0] if n == 1 else jnp.concatenate(outs, axis=1)), carry


def _sb_weights(qm, kb, carry, tri_u, diag):
    z = _dot_nt(qm, kb)
    nz = -z
    lk = jnp.minimum(nz, 0.0) - jnp.log(1.0 + jnp.exp(jnp.minimum(z, nz)))
    if diag:
        t = lax.broadcasted_iota(jnp.int32, (TQ, TQ), 0)
        s = lax.broadcasted_iota(jnp.int32, (TQ, TQ), 1)
        valid = s < t
        lk = jnp.where(valid, lk, 0.0)
    lk_hi = lk.astype(BF16)
    after, carry = _suffix_sums(lk, lk_hi, tri_u, carry)
    logw = z + lk + after
    if diag:
        logw = jnp.where(valid, logw, NEG)
    return lk_hi, jnp.exp(logw), carry


SB_SCALE = 0.125


def _head_masks():
    lane = _lane_iota((1, LANES))
    return [lane < 64, lane >= 64]


def _masked(hm, a):
    return jnp.where(hm, a, jnp.zeros_like(a))


def _rowsum(a):
    return jnp.sum(a, axis=1, keepdims=True)


def _key_rows(kj):
    return pl.ds(pl.multiple_of(kj * TQ, TQ), TQ)


def _over_key_tiles(count, fn, st, ascending):
    n_full = count // KEY_UNROLL
    n_rest = count - n_full * KEY_UNROLL

    def group(g, s_):
        return fn([g * KEY_UNROLL + (u if ascending else KEY_UNROLL - 1 - u) for u in range(KEY_UNROLL)], s_)

    if ascending:
        st = lax.fori_loop(0, n_full, group, st)
        return lax.fori_loop(0, n_rest, lambda i, s_: fn([n_full * KEY_UNROLL + i], s_), st)
    st = lax.fori_loop(0, n_rest, lambda i, s_: fn([count - 1 - i], s_), st)
    return lax.fori_loop(0, n_full, lambda i, s_: group(n_full - 1 - i, s_), st)


def _each_tile(block):
    def trip(tiles, st):
        for kj in tiles:
            st = block(_key_rows(kj), st, False)
        return st
    return trip


STAGE_SLOTS = 2 * KEY_UNROLL


def _stage_copies(to_hbm, hbm_refs, scr_refs, sems, pair, qi, kj):
    slot = (qi - kj) % STAGE_SLOTS
    out = []
    for h in range(2):
        for a in range(2):
            hbm, scr = hbm_refs[a].at[2 * pair + h, qi, kj], scr_refs[a].at[slot, h]
            sem = sems.at[4 * slot + 2 * h + a]
            out.append(pltpu.make_async_copy(scr, hbm, sem) if to_hbm else pltpu.make_async_copy(hbm, scr, sem))
    return out


def _sb_fwd(qkv):
    s_len = qkv.shape[0]
    nq = s_len // TQ

    def body(q_ref, k_ref, v_ref, o_ref, w_hbm, l_hbm, w_scr, l_scr, sems):
        pair, qi = pl.program_id(0), pl.program_id(1)
        q = q_ref[...]
        tri_u = _tri(True)
        masks = _head_masks()
        qms = [_masked(hm, q) * SB_SCALE for hm in masks]

        def copies(kj, of_qi=qi):
            return _stage_copies(True, (w_hbm, l_hbm), (w_scr, l_scr), sems, pair, of_qi, kj)

        def drain(of_qi):
            for kj in range(STAGE_SLOTS):
                @pl.when(kj <= of_qi)
                def _():
                    for cp in copies(kj, of_qi):
                        cp.wait()

        def block(kj, st, diag, before_staging=None):
            rows = _key_rows(kj)
            slot = (qi - kj) % STAGE_SLOTS
            kb, vb = k_ref[rows, :], v_ref[rows, :]
            carries, acc = list(st[:2]), st[2]
            staged = []
            for h in range(2):
                lk_hi, w, carries[h] = _sb_weights(qms[h], kb, carries[h], tri_u, diag)
                wb = w.astype(BF16)
                staged.append((wb, lk_hi))
                acc = acc + _dot_nn(wb, _masked(masks[h], vb))
            if before_staging is not None:
                before_staging()
            for h in range(2):
                w_scr[slot, h], l_scr[slot, h] = staged[h]
            return carries[0], carries[1], acc

        def trip(tiles, st):
            for kj in tiles:
                @pl.when(qi - kj >= STAGE_SLOTS)
                def _():
                    for cp in copies(kj + STAGE_SLOTS):
                        cp.wait()
            for kj in tiles:
                st = block(kj, st, False)
            for kj in tiles:
                for cp in copies(kj):
                    cp.start()
            return st

        def drain_previous_step():
            @pl.when(jnp.logical_or(pair > 0, qi > 0))
            def _():
                drain(jnp.where(qi == 0, nq - 1, qi - 1))

        zc = jnp.zeros((TQ, 1), F32)
        st = block(qi, (zc, zc, jnp.zeros((TQ, LANES), F32)), True, drain_previous_step)
        for cp in copies(qi):
            cp.start()
        st = _over_key_tiles(qi, trip, st, ascending=False)
        o_ref[...] = st[2]

        @pl.when(jnp.logical_and(pair == HEADS // 2 - 1, qi == nq - 1))
        def _():
            drain(qi)

    saved = jax.ShapeDtypeStruct((HEADS, nq, nq, TQ, TQ), BF16)
    stage = pltpu.VMEM((STAGE_SLOTS, 2, TQ, TQ), BF16)
    return pl.pallas_call(
        body, name="sb_fwd", grid=(HEADS // 2, nq),
        out_shape=[jax.ShapeDtypeStruct((s_len, SB_WIDTH), F32), saved, saved],
        in_specs=[pl.BlockSpec((TQ, LANES), lambda p, i: (i, p)),
                  pl.BlockSpec((s_len, LANES), lambda p, i: (0, 4 + p)),
                  pl.BlockSpec((s_len, LANES), lambda p, i: (0, 8 + p))],
        out_specs=[pl.BlockSpec((TQ, LANES), lambda p, i: (i, p)), _any_spec(), _any_spec()],
        scratch_shapes=[stage, stage, pltpu.SemaphoreType.DMA((4 * STAGE_SLOTS,))],
        compiler_params=pltpu.CompilerParams(dimension_semantics=("arbitrary", "arbitrary"),
                                             vmem_limit_bytes=VMEM_BIG),
    )(qkv, qkv, qkv)


def _prefix_sums(a, tri_m, carry):
    n = a.shape[1] // CUM_W
    outs = [None] * n
    for i in range(n):
        cols = slice(CUM_W * i, CUM_W * (i + 1))
        outs[i] = _dot_nn(a[:, cols].astype(BF16), tri_m) + carry
        carry = carry + _rowsum(a[:, cols])
    return (outs[0] if n == 1 else jnp.concatenate(outs, axis=1)), carry


def _head_rows(a0, a1):
    sub = lax.broadcasted_iota(jnp.int32, a0.shape, 0)
    return jnp.where(sub < NOPE, a0, a1)


def _sb_bwd(qkv, q_t, do, do_t, w_saved, l_saved):
    s_len = qkv.shape[0]
    nq = s_len // TQ

    def body(qt_ref, k_ref, v_ref, do_ref, dot_ref, w_hbm, l_hbm, dq_ref, dk_ref, dv_ref, w_scr, l_scr, sems):
        pair, qi = pl.program_id(0), pl.program_id(1)

        def copies(kj, of_pair=pair, of_qi=qi):
            return _stage_copies(False, (w_hbm, l_hbm), (w_scr, l_scr), sems, of_pair, of_qi, kj)

        def start_first_tiles(of_pair, of_qi):
            for cp in copies(0, of_pair, of_qi):
                cp.start()

            @pl.when(of_qi >= 1)
            def _():
                for cp in copies(1, of_pair, of_qi):
                    cp.start()

        @pl.when(jnp.logical_and(pair == 0, qi == 0))
        def _():
            start_first_tiles(pair, qi)

        @pl.when(qi == 0)
        def _():
            dk_ref[...] = jnp.zeros_like(dk_ref)
            dv_ref[...] = jnp.zeros_like(dv_ref)

        qt = qt_ref[...] * SB_SCALE
        dot_v = dot_ref[...]
        do_v = do_ref[...]
        j = lax.broadcasted_iota(jnp.int32, (CUM_W, CUM_W), 0)
        s = lax.broadcasted_iota(jnp.int32, (CUM_W, CUM_W), 1)
        tri_before = (j < s).astype(BF16)
        masks = _head_masks()
        doms = [_masked(hm, do_v) for hm in masks]

        def block(kj, st):
            rows = _key_rows(kj)
            slot = (qi - kj) % STAGE_SLOTS
            kb, vb = k_ref[rows, :], v_ref[rows, :]
            carries, dqs = list(st[0:2]), list(st[2:4])
            dk_t, dv_t = [], []
            for h in range(2):
                wb = w_scr[slot, h]
                d_l = _dot_nt(doms[h], vb) * wb.astype(F32)
                before, carries[h] = _prefix_sums(d_l, tri_before, carries[h])
                keep = jnp.exp(l_scr[slot, h].astype(F32))
                dzb = (d_l * keep - before * (1.0 - keep)).astype(BF16)
                dk_t.append(_dot_nn(qt, dzb))
                dv_t.append(_dot_nn(dot_v, wb))
                dqs[h] = dqs[h] + _dot_nn(dzb, kb)
            dk_ref[kj] += _head_rows(*dk_t)
            dv_ref[kj] += _head_rows(*dv_t)
            return (*carries, *dqs)

        def trip(tiles, st):
            for kj in tiles:
                @pl.when(kj + 2 <= qi)
                def _():
                    for cp in copies(kj + 2):
                        cp.start()
            for kj in tiles:
                for cp in copies(kj):
                    cp.wait()
            for kj in tiles:
                st = block(kj, st)
            return st

        zc = jnp.zeros((TQ, 1), F32)
        za = jnp.zeros((TQ, LANES), F32)
        st = _over_key_tiles(qi + 1, trip, (zc, zc, za, za), ascending=True)
        dq_ref[...] = jnp.where(masks[0], st[2], st[3]) * SB_SCALE

        @pl.when(jnp.logical_or(pair < HEADS // 2 - 1, qi < nq - 1))
        def _():
            wraps = qi == nq - 1
            start_first_tiles(jnp.where(wraps, pair + 1, pair), jnp.where(wraps, 0, qi + 1))

    tile = pl.BlockSpec((TQ, LANES), lambda p, i: (i, p))
    tile_t = pl.BlockSpec((LANES, TQ), lambda p, i: (p, i))
    col_t = pl.BlockSpec((nq, LANES, TQ), lambda p, i: (0, p, 0))
    stage = pltpu.VMEM((STAGE_SLOTS, 2, TQ, TQ), BF16)
    key_t = jax.ShapeDtypeStruct((nq, SB_WIDTH, TQ), F32)
    return pl.pallas_call(
        body, name="sb_bwd", grid=(HEADS // 2, nq),
        out_shape=[jax.ShapeDtypeStruct((s_len, SB_WIDTH), F32), key_t, key_t],
        in_specs=[tile_t,
                  pl.BlockSpec((s_len, LANES), lambda p, i: (0, 4 + p)),
                  pl.BlockSpec((s_len, LANES), lambda p, i: (0, 8 + p)),
                  tile, tile_t, _any_spec(), _any_spec()],
        out_specs=[tile, col_t, col_t],
        scratch_shapes=[stage, stage, pltpu.SemaphoreType.DMA((4 * STAGE_SLOTS,))],
        compiler_params=pltpu.CompilerParams(dimension_semantics=("arbitrary", "arbitrary"),
                                             vmem_limit_bytes=VMEM_BIG),
    )(q_t, qkv, qkv, do, do_t, w_saved, l_saved)


def _from_key_tiles(a_t):
    return a_t.transpose(0, 2, 1).reshape(a_t.shape[0] * a_t.shape[2], a_t.shape[1])


MLA_SCALE = 1.0 / math.sqrt(MLA_QK)


def _causal_mask():
    t = lax.broadcasted_iota(jnp.int32, (TQ, TQ), 0)
    s = lax.broadcasted_iota(jnp.int32, (TQ, TQ), 1)
    return s <= t


def _head_lanes(h):
    return slice(LANES * h, LANES * (h + 1))


def _mla_fwd(qm, km, vm):
    s_len = qm.shape[0]

    def body(q_ref, k_ref, v_ref, o_ref, lse_ref):
        qi = pl.program_id(1)
        masks = _head_masks()
        qhs = [q_ref[:, _head_lanes(h)] for h in range(2)]

        def block(rows, st, diag):
            vb = v_ref[rows, :]
            ms, ls, acc = list(st[0:2]), list(st[2:4]), st[4]
            alphas, pvs = [], []
            for h in range(2):
                s = _dot_nt(qhs[h], k_ref[rows, _head_lanes(h)])
                if diag:
                    s = jnp.where(_causal_mask(), s, NEG)
                m_new = jnp.maximum(ms[h], jnp.max(s, axis=1, keepdims=True))
                p = jnp.exp(s - m_new)
                alphas.append(jnp.exp(ms[h] - m_new))
                ls[h] = alphas[h] * ls[h] + _rowsum(p)
                ms[h] = m_new
                pvs.append(_dot_nn(p.astype(BF16), _masked(masks[h], vb)))
            acc = jnp.where(masks[0], alphas[0], alphas[1]) * acc + pvs[0] + pvs[1]
            return (*ms, *ls, acc)

        neg = jnp.full((TQ, 1), NEG, F32)
        zc = jnp.zeros((TQ, 1), F32)
        st = (neg, neg, zc, zc, jnp.zeros((TQ, LANES), F32))
        st = _over_key_tiles(qi, _each_tile(block), st, ascending=True)
        m0, m1, l0, l1, acc = block(_key_rows(qi), st, True)
        o_ref[...] = acc / jnp.where(masks[0], l0, l1)
        lse_ref[0] = m0 + jnp.log(l0)
        lse_ref[1] = m1 + jnp.log(l1)

    return pl.pallas_call(
        body, name="mla_fwd", grid=(HEADS // 2, s_len // TQ),
        out_shape=[jax.ShapeDtypeStruct((s_len, MLA_WIDTH), F32),
                   jax.ShapeDtypeStruct((HEADS, s_len, 1), F32)],
        in_specs=[pl.BlockSpec((TQ, 2 * LANES), lambda p, i: (i, p)),
                  pl.BlockSpec((s_len, 2 * LANES), lambda p, i: (0, p)),
                  pl.BlockSpec((s_len, LANES), lambda p, i: (0, p))],
        out_specs=[pl.BlockSpec((TQ, LANES), lambda p, i: (i, p)),
                   pl.BlockSpec((2, TQ, 1), lambda p, i: (p, i, 0))],
        compiler_params=pltpu.CompilerParams(dimension_semantics=("parallel", "parallel"),
                                             vmem_limit_bytes=VMEM_BIG),
    )(qm, km, vm)


def _mla_bwd(qm, qm_t, km, vm, o, do, do_t, lse):
    s_len = qm.shape[0]

    nq = s_len // TQ

    def body(q_ref, qt_ref, k_ref, v_ref, o_ref, do_ref, dot_ref, lse_ref, dq_ref, dk_ref, dv_ref):
        qi = pl.program_id(1)

        @pl.when(qi == 0)
        def _():
            dk_ref[...] = jnp.zeros_like(dk_ref)
            dv_ref[...] = jnp.zeros_like(dv_ref)

        do_v = do_ref[...]
        dot_v = dot_ref[...]
        od = o_ref[...] * do_v.astype(F32)
        masks = _head_masks()
        qhs = [q_ref[:, _head_lanes(h)] for h in range(2)]
        qts = [qt_ref[_head_lanes(h), :] for h in range(2)]
        doms = [_masked(hm, do_v) for hm in masks]
        deltas = [_rowsum(jnp.where(hm, od, 0.0)) for hm in masks]
        lses = [lse_ref[h] for h in range(2)]

        def block(kj, dqs, diag):
            rows = _key_rows(kj)
            vb = v_ref[rows, :]
            dqs = list(dqs)
            dv_t = []
            for h in range(2):
                kb = k_ref[rows, _head_lanes(h)]
                s = _dot_nt(qhs[h], kb)
                if diag:
                    s = jnp.where(_causal_mask(), s, NEG)
                p = jnp.exp(s - lses[h])
                dp = _dot_nt(doms[h], vb)
                ds = (p * (dp - deltas[h])).astype(BF16)
                dk_ref[kj, _head_lanes(h), :] += _dot_nn(qts[h], ds)
                dv_t.append(_dot_nn(dot_v, p.astype(BF16)))
                dqs[h] = dqs[h] + _dot_nn(ds, kb)
            dv_ref[kj] += _head_rows(*dv_t)
            return tuple(dqs)

        def trip(tiles, dqs):
            for kj in tiles:
                dqs = block(kj, dqs, False)
            return dqs

        za = jnp.zeros((TQ, LANES), F32)
        dqs = _over_key_tiles(qi, trip, (za, za), ascending=True)
        dqs = block(qi, dqs, True)
        dq_ref[:, _head_lanes(0)] = dqs[0] * MLA_SCALE
        dq_ref[:, _head_lanes(1)] = dqs[1] * MLA_SCALE

    return pl.pallas_call(
        body, name="mla_bwd", grid=(HEADS // 2, nq),
        out_shape=[jax.ShapeDtypeStruct((s_len, HEADS * LANES), F32),
                   jax.ShapeDtypeStruct((nq, HEADS * LANES, TQ), F32),
                   jax.ShapeDtypeStruct((nq, MLA_WIDTH, TQ), F32)],
        in_specs=[pl.BlockSpec((TQ, 2 * LANES), lambda p, i: (i, p)),
                  pl.BlockSpec((2 * LANES, TQ), lambda p, i: (p, i)),
                  pl.BlockSpec((s_len, 2 * LANES), lambda p, i: (0, p)),
                  pl.BlockSpec((s_len, LANES), lambda p, i: (0, p)),
                  pl.BlockSpec((TQ, LANES), lambda p, i: (i, p)),
                  pl.BlockSpec((TQ, LANES), lambda p, i: (i, p)),
                  pl.BlockSpec((LANES, TQ), lambda p, i: (p, i)),
                  pl.BlockSpec((2, TQ, 1), lambda p, i: (p, i, 0))],
        out_specs=[pl.BlockSpec((TQ, 2 * LANES), lambda p, i: (i, p)),
                   pl.BlockSpec((nq, 2 * LANES, TQ), lambda p, i: (0, p, 0)),
                   pl.BlockSpec((nq, LANES, TQ), lambda p, i: (0, p, 0))],
        compiler_params=pltpu.CompilerParams(dimension_semantics=("parallel", "arbitrary"),
                                             vmem_limit_bytes=VMEM_BIG),
    )(qm, qm_t, km, vm, o, do, do_t, lse)


def _mid(o_sb, g_sb, o_mla, g_mla, x, target, gate, w_out):
    s_len = x.shape[0]

    def body(osb_ref, gsb_ref, omla_ref, gmla_ref, x_ref, t_ref, gate_ref, w_ref,
             dy_ref, dosb_ref, dgsb_ref, domla_ref, dgmla_ref, gw_ref, dgate_ref, loss_ref, dosbt_ref, domlat_ref):
        @pl.when(pl.program_id(0) == 0)
        def _():
            gw_ref[...] = jnp.zeros_like(gw_ref)
            dgate_ref[...] = jnp.zeros_like(dgate_ref)
            loss_ref[...] = jnp.zeros_like(loss_ref)

        o1, g1, o2, g2 = osb_ref[...], gsb_ref[...], omla_ref[...], gmla_ref[...]
        s1, s2 = _sigmoid(g1), _sigmoid(g2)
        mixed = jnp.concatenate([o1 * (g1 * s1), o2 * (g2 * s2)], axis=1).astype(BF16)
        w = w_ref[...]
        gate_v = gate_ref[...]
        u = _dot_nn(mixed, w)
        err = x_ref[...] + gate_v * u - t_ref[...]
        loss_ref[...] += jnp.sum(err * err, axis=0, keepdims=True)
        dy = err * (1.0 / D_MODEL)
        dy_ref[...] = dy
        dgate_ref[...] += jnp.sum(dy * u, axis=0, keepdims=True)
        du = (dy * gate_v).astype(BF16)
        gw_ref[...] += _dot_tn(mixed, du)
        dmixed = _dot_nt(du, w)
        d1, d2 = dmixed[:, :SB_WIDTH], dmixed[:, SB_WIDTH:]
        do1, do2 = d1 * (g1 * s1), d2 * (g2 * s2)
        dosb_ref[...] = do1.astype(BF16)
        dgsb_ref[...] = (d1 * o1 * (s1 * (1.0 + g1 * (1.0 - s1)))).astype(BF16)
        domla_ref[...] = do2.astype(BF16)
        dgmla_ref[...] = (d2 * o2 * (s2 * (1.0 + g2 * (1.0 - s2)))).astype(BF16)
        dosbt_ref[...] = do1.T.astype(BF16)
        domlat_ref[...] = do2.T.astype(BF16)

    def tile(width):
        return pl.BlockSpec((TS, width), lambda i: (i, 0))

    def full(shape):
        return pl.BlockSpec(shape, lambda i: (0, 0))

    return pl.pallas_call(
        body, name="mid", grid=(s_len // TS,),
        out_shape=[jax.ShapeDtypeStruct((s_len, D_MODEL), F32)]
        + [jax.ShapeDtypeStruct((s_len, 512), BF16)] * 4
        + [jax.ShapeDtypeStruct((D_MODEL, D_MODEL), F32),
           jax.ShapeDtypeStruct((1, D_MODEL), F32), jax.ShapeDtypeStruct((1, D_MODEL), F32)]
        + [jax.ShapeDtypeStruct((512, s_len), BF16)] * 2,
        in_specs=[tile(512)] * 4 + [tile(D_MODEL), tile(D_MODEL), full((1, D_MODEL)), full((D_MODEL, D_MODEL))],
        out_specs=[tile(D_MODEL)] + [tile(512)] * 4
        + [full((D_MODEL, D_MODEL)), full((1, D_MODEL)), full((1, D_MODEL))]
        + [pl.BlockSpec((512, TS), lambda i: (0, i))] * 2,
        compiler_params=pltpu.CompilerParams(dimension_semantics=("arbitrary",), vmem_limit_bytes=VMEM_BIG),
    )(o_sb, g_sb, o_mla, g_mla, x, target, gate, w_out)


def _mla_pre_bwd(dq, dk, dv, cq, ckv, kr, qln, wuq, kvln, wukv, qhn, khn, cosf, sin_a, sin_b):
    s_len = cq.shape[0]

    def body(dq_ref, dk_ref, dv_ref, cq_ref, ckv_ref, kr_ref, qln_ref, wuq_ref, kvln_ref, wukv_ref,
             qhn_ref, khn_ref, cos_ref, sa_ref, sb_ref,
             dcq_ref, dckv_ref, dkr_ref, gwuq_ref, gwukv_ref, gqhn_ref, gkhn_ref, gqln_ref, gkvln_ref,
             dqa_ref, dkv_ref):
        @pl.when(pl.program_id(0) == 0)
        def _():
            for r_ in (gwuq_ref, gwukv_ref, gqhn_ref, gkhn_ref, gqln_ref, gkvln_ref):
                r_[...] = jnp.zeros_like(r_)

        cq, ckv = cq_ref[...], ckv_ref[...]
        qln_v, kvln_v = qln_ref[...], kvln_ref[...]
        wuq_v, wukv_v = wuq_ref[...], wukv_ref[...]
        rq, rkv, cq_hat, ckv_hat, cqn, ckvn, q_all, kv, kr64 = _mla_latents(
            cq, ckv, kr_ref[...], qln_v, kvln_v, wuq_v, wukv_v)
        cosf, sa, sb = cos_ref[...], sa_ref[...], sb_ref[...]
        qhn_v, khn_v = qhn_ref[...], khn_ref[...]
        lane = _lane_iota((TS, LANES))
        low = lane < NOPE
        blks = [slice(LANES * hd, LANES * (hd + 1)) for hd in range(HEADS)]
        raw = [q_all[:, b] for b in blks] + [jnp.where(low, kv[:, b], kr64) for b in blks]
        grads = [dq_ref[:, b] for b in blks] + [dk_ref[:, b] for b in blks]
        gains = [qhn_v] * HEADS + [khn_v] * HEADS
        rms = [_head_rms(t) for t in raw]
        xhs = [t * r for t, r in zip(raw, rms)]
        dns = [_rope_t(d, cosf, sa, sb) for d in grads]
        gain_g = [jnp.sum(dn * xh, axis=0, keepdims=True) for dn, xh in zip(dns, xhs)]
        dxs = [_rms_bwd(dn * g, xh, r, MLA_QK) for dn, g, xh, r in zip(dns, gains, xhs, rms)]
        dkr64 = jnp.zeros((TS, LANES), F32)
        for hd, b in enumerate(blks):
            dqa_ref[:, b] = dxs[hd].astype(BF16)
            dkb = dxs[HEADS + hd]
            dkr64 = dkr64 + jnp.where(low, 0.0, dkb)
            dvp = dv_ref[:, LANES * (hd // 2):LANES * (hd // 2 + 1)]
            dvh = pltpu.roll(dvp, 64, 1) if hd % 2 == 0 else dvp
            dkv_ref[:, b] = jnp.where(low, dkb, dvh).astype(BF16)
        gqhn_ref[...] += sum(gain_g[:HEADS])
        gkhn_ref[...] += sum(gain_g[HEADS:])
        dkr_ref[...] = pltpu.roll(dkr64, 64, 1).astype(BF16)

        dqa = dqa_ref[...]
        gwuq_ref[...] += _dot_tn(cqn, dqa)
        dcqn = _dot_nt(dqa, wuq_v)
        gqln_ref[...] += jnp.sum(dcqn * cq_hat, axis=0, keepdims=True)
        dcq_ref[...] = _rms_bwd(dcqn * qln_v, cq_hat, rq, Q_LORA).astype(BF16)

        dkv = dkv_ref[...]
        gwukv_ref[...] += _dot_tn(ckvn, dkv)
        dckvn = _dot_nt(dkv, wukv_v)
        gkvln_ref[...] += jnp.sum(dckvn * ckv_hat, axis=0, keepdims=True)
        dckv_ref[...] = _rms_bwd(dckvn * kvln_v, ckv_hat, rkv, KV_LORA).astype(BF16)

    def tile(width):
        return pl.BlockSpec((TS, width), lambda i: (i, 0))

    def full(shape):
        return pl.BlockSpec(shape, lambda i: (0, 0))

    acc_shapes = [(Q_LORA, 1024), (KV_LORA, 1024), (1, LANES), (1, LANES), (1, Q_LORA), (1, KV_LORA)]
    return pl.pallas_call(
        body, name="mla_pre_bwd", grid=(s_len // TS,),
        out_shape=[jax.ShapeDtypeStruct((s_len, Q_LORA), BF16), jax.ShapeDtypeStruct((s_len, KV_LORA), BF16),
                   jax.ShapeDtypeStruct((s_len, LANES), BF16)]
        + [jax.ShapeDtypeStruct(s, F32) for s in acc_shapes],
        in_specs=[tile(1024), tile(1024), tile(512), tile(Q_LORA), tile(KV_LORA), tile(LANES),
                  full(qln.shape), full(wuq.shape), full(kvln.shape), full(wukv.shape),
                  full(qhn.shape), full(khn.shape), tile(LANES), tile(LANES), tile(LANES)],
        out_specs=[tile(Q_LORA), tile(KV_LORA), tile(LANES)] + [full(s) for s in acc_shapes],
        scratch_shapes=[pltpu.VMEM((TS, 1024), BF16), pltpu.VMEM((TS, 1024), BF16)],
        compiler_params=pltpu.CompilerParams(dimension_semantics=("arbitrary",), vmem_limit_bytes=VMEM_BIG),
    )(dq, dk, dv, cq, ckv, kr, qln, wuq, kvln, wukv, qhn, khn, cosf, sin_a, sin_b)


def _dproj_bwd(dq_sb, dk_sb, dv_sb, dg_sb, dcq, dckv, dg_mla, dkr, w_in_r, x, dy, norm_w, scale):
    s_len = x.shape[0]

    def body(dq_ref, dk_ref, dv_ref, dg_ref, dcq_ref, dckv_ref, dgm_ref, dkr_ref, w_ref, x_ref, dy_ref,
             nw_ref, scale_ref, dp_ref, gx_ref, dshift_ref, dscale_ref, dnw_ref):
        @pl.when(pl.program_id(0) == 0)
        def _():
            for r_ in (dshift_ref, dscale_ref, dnw_ref):
                r_[...] = jnp.zeros_like(r_)

        dp_ref[:, 0:512] = dq_ref[...].astype(BF16)
        dp_ref[:, 512:1024] = dk_ref[...].astype(BF16)
        dp_ref[:, 1024:1536] = dv_ref[...].astype(BF16)
        dp_ref[:, 1536:2048] = dg_ref[...]
        dp_ref[:, 2048:2432] = dcq_ref[...]
        dp_ref[:, 2432:2688] = dckv_ref[...]
        dp_ref[:, 2688:3200] = dgm_ref[...]
        dp_ref[:, 3200:3328] = dkr_ref[...]
        dh = _dot_nt(dp_ref[...], w_ref[...])
        xv = x_ref[...]
        r = lax.rsqrt(jnp.mean(xv * xv, axis=1, keepdims=True) + EPS)
        xh = xv * r
        nw = nw_ref[...]
        dshift_ref[...] += jnp.sum(dh, axis=0, keepdims=True)
        dscale_ref[...] += jnp.sum(dh * (xh * nw), axis=0, keepdims=True)
        dxnw = dh * (1.0 + scale_ref[...])
        dnw_ref[...] += jnp.sum(dxnw * xh, axis=0, keepdims=True)
        gx_ref[...] = dy_ref[...] + _rms_bwd(dxnw * nw, xh, r, D_MODEL)

    def tile(width):
        return pl.BlockSpec((TS, width), lambda i: (i, 0))

    def full(shape):
        return pl.BlockSpec(shape, lambda i: (0, 0))

    vec = (1, D_MODEL)
    return pl.pallas_call(
        body, name="dproj_bwd", grid=(s_len // TS,),
        out_shape=[jax.ShapeDtypeStruct((s_len, IN_COLS_R), BF16), jax.ShapeDtypeStruct((s_len, D_MODEL), F32)]
        + [jax.ShapeDtypeStruct(vec, F32)] * 3,
        in_specs=[tile(512)] * 4 + [tile(Q_LORA), tile(KV_LORA), tile(512), tile(LANES),
                                    full(w_in_r.shape), tile(D_MODEL), tile(D_MODEL), full(vec), full(vec)],
        out_specs=[tile(IN_COLS_R), tile(D_MODEL)] + [full(vec)] * 3,
        compiler_params=pltpu.CompilerParams(dimension_semantics=("arbitrary",), vmem_limit_bytes=VMEM_BIG),
    )(dq_sb, dk_sb, dv_sb, dg_sb, dcq, dckv, dg_mla, dkr, w_in_r, x, dy, norm_w, scale)


def _grad_w_in(hb, dproj):
    s_len = hb.shape[0]
    n_half = IN_COLS_R // 2

    def body(h_ref, d_ref, g_ref):
        @pl.when(pl.program_id(1) == 0)
        def _():
            g_ref[...] = jnp.zeros_like(g_ref)

        g_ref[...] += _dot_tn(h_ref[...], d_ref[...])

    return pl.pallas_call(
        body, name="grad_w_in", grid=(2, s_len // TN_S),
        out_shape=jax.ShapeDtypeStruct((D_MODEL, IN_COLS_R), F32),
        in_specs=[pl.BlockSpec((TN_S, D_MODEL), lambda n, s: (s, 0)),
                  pl.BlockSpec((TN_S, n_half), lambda n, s: (s, n))],
        out_specs=pl.BlockSpec((D_MODEL, n_half), lambda n, s: (0, n)),
        compiler_params=pltpu.CompilerParams(dimension_semantics=("parallel", "arbitrary"),
                                             vmem_limit_bytes=VMEM_BIG),
    )(hb, dproj)


def _final_exchange(gpack, ccol, wpack, mpack, vpack, w_ada, m_ada, v_ada, grads):
    n_sh = w_ada.shape[1]
    n = len(grads)

    def body(*refs):
        (g_ref, cc_ref, wp_ref, mp_ref, vp_ref, wa_ref, ma_ref, va_ref) = refs[:8]
        slabs_in = refs[8:8 + n]
        (og_ref, od_ref, om_ref, ov_ref, ag_ref, ad_ref, am_ref, av_ref) = refs[8 + n:16 + n]
        slabs_out = refs[16 + n:16 + 2 * n]
        gall_ref, ssem, rsem, slab_ssem, slab_rsem, lsem = refs[16 + 2 * n:]
        pos = _mesh_pos()
        me = _lin(pos)
        gall_ref[me] = g_ref[...]
        small = _all_gather_start(pos, g_ref, gall_ref, ssem, rsem, 0)
        own = [pltpu.make_async_copy(slabs_in[a].at[me], slabs_out[a].at[me], lsem.at[a]) for a in range(n)]
        for cp in own:
            cp.start()
        in_flight = [_all_to_all_start(pos, slabs_in[a], slabs_out[a], slab_ssem, slab_rsem, a * (N_DEV - 1))
                     for a in range(n)]
        _all_gather_wait(pos, g_ref, gall_ref, ssem, rsem, 0, small)

        tot = gall_ref[0]
        for j in range(1, N_DEV):
            tot = tot + gall_ref[j]
        og_ref[...] = tot
        od_ref[...], om_ref[...], ov_ref[...] = _adamw(wp_ref[...], tot, mp_ref[...], vp_ref[...])

        ga = jnp.zeros((D_MODEL, n_sh), F32)
        for j in range(N_DEV):
            d_mine = jnp.zeros((8, n_sh), F32)
            for k in range(N_DEV):
                d_mine = d_mine + jnp.where(me == k, gall_ref[j, :, PK_ADA + n_sh * k:PK_ADA + n_sh * (k + 1)], 0.0)
            col = _silu(cc_ref[j])
            ga = ga + jnp.concatenate(
                [col * d_mine[0:1, LANES * a:LANES * (a + 1)] for a in range(n_sh // LANES)], axis=1)
        ag_ref[...] = ga
        ad_ref[...], am_ref[...], av_ref[...] = _adamw(wa_ref[...], ga, ma_ref[...], va_ref[...])

        for a in range(n):
            _all_to_all_wait(pos, slabs_in[a], slabs_out[a], slab_ssem, slab_rsem, a * (N_DEV - 1), in_flight[a])
        for cp in own:
            cp.wait()

    pk = jax.ShapeDtypeStruct((8, PK_END), F32)
    ada = jax.ShapeDtypeStruct((D_MODEL, n_sh), F32)
    return pl.pallas_call(
        body, name="final_exchange",
        out_shape=[pk] * 4 + [ada] * 4 + [jax.ShapeDtypeStruct(g.shape, g.dtype) for g in grads],
        in_specs=[_vmem_spec()] * 8 + [_any_spec()] * n,
        out_specs=[_vmem_spec()] * 8 + [_any_spec()] * n,
        scratch_shapes=[
            pltpu.VMEM((N_DEV, 8, PK_END), F32),
            pltpu.SemaphoreType.DMA((N_DEV - 1,)),
            pltpu.SemaphoreType.DMA((N_DEV - 1,)),
            pltpu.SemaphoreType.DMA((n * (N_DEV - 1),)),
            pltpu.SemaphoreType.DMA((n * (N_DEV - 1),)),
            pltpu.SemaphoreType.DMA((n,)),
        ],
        compiler_params=pltpu.CompilerParams(vmem_limit_bytes=VMEM_BIG),
    )(gpack, ccol, wpack, mpack, vpack, w_ada, m_ada, v_ada, *grads)


def _adamw_reduce(name, parts, w, m, v, row_tile):
    rows, cols = w.shape

    def body(p_ref, w_ref, m_ref, v_ref, g_ref, d_ref, mo_ref, vo_ref):
        g = p_ref[0].astype(F32)
        for j in range(1, N_DEV):
            g = g + p_ref[j].astype(F32)
        g_ref[...] = g
        d_ref[...], mo_ref[...], vo_ref[...] = _adamw(w_ref[...], g, m_ref[...], v_ref[...])

    tile = pl.BlockSpec((row_tile, cols), lambda i: (i, 0))
    return pl.pallas_call(
        body, name=name, grid=(rows // row_tile,),
        out_shape=[jax.ShapeDtypeStruct((rows, cols), F32)] * 4,
        in_specs=[pl.BlockSpec((N_DEV, row_tile, cols), lambda i: (0, i, 0)), tile, tile, tile],
        out_specs=[tile] * 4,
        compiler_params=pltpu.CompilerParams(dimension_semantics=("parallel",), vmem_limit_bytes=VMEM_BIG),
    )(parts, w, m, v)


def _rope_tables(positions):
    inv_freq = 10000.0 ** (-jnp.arange(0, ROPE, 2, dtype=F32) / ROPE)
    ang = positions.astype(F32)[:, None] * inv_freq
    cos, sin = jnp.cos(ang), jnp.sin(ang)
    s_len = positions.shape[0]
    ones = jnp.ones((s_len, NOPE), F32)
    zeros = jnp.zeros((s_len, NOPE), F32)
    z16 = jnp.zeros((s_len, ROPE // 2), F32)
    pad1 = jnp.ones((s_len, LANES - MLA_QK), F32)
    pad0 = jnp.zeros((s_len, LANES - MLA_QK), F32)
    cosf = jnp.concatenate([ones, cos, cos, pad1], axis=1)
    sin_a = jnp.concatenate([zeros, -sin, z16, pad0], axis=1)
    sin_b = jnp.concatenate([zeros, z16, sin, pad0], axis=1)
    return cosf, sin_a, sin_b


def _rearrange_cols(w):
    pad = jnp.zeros((w.shape[0], IN_COLS_R - IN_COLS), w.dtype)
    return jnp.concatenate([w[:, :2688], w[:, 2720:3232], w[:, 2688:2720], pad], axis=1)


def _restore_cols(g):
    return jnp.concatenate([g[:, :2688], g[:, 3200:3232], g[:, 2688:3200]], axis=1)


def _pad_heads(w):
    rows = w.shape[0]
    w = w.reshape(rows, HEADS, MLA_QK)
    return jnp.pad(w, ((0, 0), (0, 0), (0, LANES - MLA_QK))).reshape(rows, HEADS * LANES)


def _unpad_heads(g):
    rows = g.shape[0]
    return g.reshape(rows, HEADS, LANES)[:, :, :MLA_QK].reshape(rows, HEADS * MLA_QK)


def _pad_lanes(v):
    return jnp.pad(v, ((0, 0), (0, LANES - v.shape[1])))


def _col_shards(g):
    rows = g.shape[0]
    return g.reshape(rows, N_DEV, g.shape[1] // N_DEV).transpose(1, 0, 2)


def _from_col_shards(g):
    return g.transpose(1, 0, 2).reshape(g.shape[1], N_DEV * g.shape[2])


def _pack(norm_w, qln, kvln, qhn, khn, ada, loss_lanes=None):
    if loss_lanes is None:
        loss_lanes = jnp.zeros((1, PK_END - PK_LOSS), F32)
    row = jnp.concatenate([norm_w, qln, kvln, _pad_lanes(qhn), _pad_lanes(khn), ada, loss_lanes], axis=1)
    return jnp.broadcast_to(row, (8, PK_END))


def _unpack(p):
    row = p[0:1]
    return (row[:, PK_NORM:PK_QLN], row[:, PK_QLN:PK_KVLN], row[:, PK_KVLN:PK_QHN],
            row[:, PK_QHN:PK_QHN + MLA_QK], row[:, PK_KHN:PK_KHN + MLA_QK], row[:, PK_ADA:PK_LOSS])


def kernel(x, c, positions, w_ada, b_ada, norm_w, w_in, q_lora_norm, w_uq, kv_lora_norm, w_ukv, q_head_norm, k_head_norm, w_out, loss_target, m_w_ada, m_b_ada, m_norm_w, m_w_in, m_q_lora_norm, m_w_uq, m_kv_lora_norm, m_w_ukv, m_q_head_norm, m_k_head_norm, m_w_out, v_w_ada, v_b_ada, v_norm_w, v_w_in, v_q_lora_norm, v_w_uq, v_kv_lora_norm, v_w_ukv, v_q_head_norm, v_k_head_norm, v_w_out):
    s_len = x.shape[1]
    x2 = x.reshape(s_len, D_MODEL)
    tgt = loss_target.reshape(s_len, D_MODEL)
    w_ada_s, w_in_s, w_uq_s, w_ukv_s, w_out_s = w_ada[0], w_in[0], w_uq[0], w_ukv[0], w_out[0]

    ada8, c_all = _ada_fwd(jnp.broadcast_to(c, (8, D_MODEL)), w_ada_s, b_ada.reshape(N_DEV, -1))
    ada = ada8.reshape(1, 3 * D_MODEL)
    shift, scale, gate = ada[:, :D_MODEL], ada[:, D_MODEL:2 * D_MODEL], ada[:, 2 * D_MODEL:]

    g_uq, g_ukv, g_out, g_in = _gather_weights([w_uq_s, w_ukv_s, w_out_s, w_in_s])
    w_in_r = _rearrange_cols(_from_col_shards(g_in))
    wuq_p = _pad_heads(_from_col_shards(g_uq))
    wukv_f = _from_col_shards(g_ukv)
    w_out_f = g_out.reshape(D_MODEL, D_MODEL)

    cosf, sin_a, sin_b = _rope_tables(positions[0])
    qhn_p, khn_p = _pad_lanes(q_head_norm), _pad_lanes(k_head_norm)

    hb, qkv, g_sb, cq, ckv, g_mla, kr, qm, km, vm, q_sb_t, qm_t = _fwd_pre(
        x2, shift, scale, norm_w, w_in_r, q_lora_norm, wuq_p, kv_lora_norm, wukv_f, qhn_p, khn_p,
        cosf, sin_a, sin_b)
    o_sb, w_saved, l_saved = _sb_fwd(qkv)
    o_mla, lse = _mla_fwd(qm, km, vm)

    dy, do_sb, dg_sb, do_mla, dg_mla, gw_out, d_gate, loss_acc, do_sb_t, do_mla_t = _mid(
        o_sb, g_sb, o_mla, g_mla, x2, tgt, gate, w_out_f)

    dq_sb, dk_sb_t, dv_sb_t = _sb_bwd(qkv, q_sb_t, do_sb, do_sb_t, w_saved, l_saved)
    dk_sb, dv_sb = _from_key_tiles(dk_sb_t), _from_key_tiles(dv_sb_t)
    dq_m, dk_m_t, dv_m_t = _mla_bwd(qm, qm_t, km, vm, o_mla, do_mla, do_mla_t, lse)
    dk_m, dv_m = _from_key_tiles(dk_m_t), _from_key_tiles(dv_m_t)
    dcq, dckv, dkr, gw_uq_p, gw_ukv, g_qhn, g_khn, g_qln, g_kvln = _mla_pre_bwd(
        dq_m, dk_m, dv_m, cq, ckv, kr, q_lora_norm, wuq_p, kv_lora_norm, wukv_f, qhn_p, khn_p,
        cosf, sin_a, sin_b)
    dproj, grad_x, d_shift, d_scale, g_norm_w = _dproj_bwd(
        dq_sb, dk_sb, dv_sb, dg_sb, dcq, dckv, dg_mla, dkr, w_in_r, x2, dy, norm_w, scale)
    gw_in = _restore_cols(_grad_w_in(hb, dproj))

    d_ada = jnp.concatenate([d_shift, d_scale, d_gate], axis=1)
    gpack = _pack(g_norm_w, g_qln, g_kvln, g_qhn[:, :MLA_QK], g_khn[:, :MLA_QK], d_ada, loss_acc)
    wpack = _pack(norm_w, q_lora_norm, kv_lora_norm, q_head_norm, k_head_norm, b_ada)
    mpack = _pack(m_norm_w, m_q_lora_norm, m_kv_lora_norm, m_q_head_norm, m_k_head_norm, m_b_ada)
    vpack = _pack(v_norm_w, v_q_lora_norm, v_kv_lora_norm, v_q_head_norm, v_k_head_norm, v_b_ada)
    ccol = jnp.broadcast_to(c_all[:, :, None], (N_DEV, D_MODEL, LANES))
    slabs = [g.astype(BF16) for g in (
        _col_shards(gw_in), _col_shards(_unpad_heads(gw_uq_p)), _col_shards(gw_ukv),
        gw_out.reshape(N_DEV, D_MODEL // N_DEV, D_MODEL))]
    pg, pd, pm, pv, ada_g, ada_d, ada_m, ada_v, r_in, r_uq, r_ukv, r_out = _final_exchange(
        gpack, ccol, wpack, mpack, vpack, w_ada_s, m_w_ada[0], v_w_ada[0], slabs)
    loss = 0.5 * jnp.sum(pg[0, PK_LOSS:PK_END]) / D_MODEL

    in_g, in_d, in_m, in_v = _adamw_reduce("adamw_w_in", r_in, w_in_s, m_w_in[0], v_w_in[0], 256)
    uq_g, uq_d, uq_m, uq_v = _adamw_reduce("adamw_w_uq", r_uq, w_uq_s, m_w_uq[0], v_w_uq[0], w_uq_s.shape[0])
    ukv_g, ukv_d, ukv_m, ukv_v = _adamw_reduce(
        "adamw_w_ukv", r_ukv, w_ukv_s, m_w_ukv[0], v_w_ukv[0], w_ukv_s.shape[0])
    out_g, out_d, out_m, out_v = _adamw_reduce(
        "adamw_w_out", r_out, w_out_s, m_w_out[0], v_w_out[0], w_out_s.shape[0])

    def group(ada_t, pk, in_t, uq_t, ukv_t, out_t):
        nw, qln, kvln, qhn, khn, b = _unpack(pk)
        return (ada_t[None], b, nw, in_t[None], qln, uq_t[None], kvln, ukv_t[None], qhn, khn, out_t[None])

    return (loss, grad_x.reshape(1, s_len, D_MODEL),
            *group(ada_g, pg, in_g, uq_g, ukv_g, out_g),
            *group(ada_d, pd, in_d, uq_d, ukv_d, out_d),
            *group(ada_m, pm, in_m, uq_m, ukv_m, out_m),
            *group(ada_v, pv, in_v, uq_v, ukv_v, out_v))
```

```python
import functools
import math

import jax
import jax.numpy as jnp
from jax import lax
from jax.experimental import pallas as pl
from jax.experimental.pallas import tpu as pltpu

F32 = jnp.float32
BF16 = jnp.bfloat16

N_DEV = 8
D_MODEL = 1024
HEADS = 8
SB_WIDTH = 512
MLA_WIDTH = 512
Q_LORA = 384
KV_LORA = 256
ROPE = 32
NOPE = 64
MLA_QK = 96
LANES = 128
IN_COLS = 3232
IN_COLS_R = 3328
EPS = 1e-6
NEG = -1e30

ADAM_LR = 0.001
ADAM_B1 = 0.9
ADAM_B2 = 0.999
ADAM_EPS = 1e-08
ADAM_WD = 0.01
ADAM_STEP = 10

TS = 512
TS_FWD = 256
TQ = 512
KEY_UNROLL = 2
TN_S = 512
VMEM_BIG = 56 * 1024 * 1024

PK_NORM, PK_QLN, PK_KVLN, PK_QHN, PK_KHN, PK_ADA, PK_LOSS, PK_END = 0, 1024, 1408, 1664, 1792, 1920, 4992, 6016

MESH_ID = pl.DeviceIdType.MESH


def _dot_nn(a, b):
    return lax.dot_general(a, b, (((1,), (0,)), ((), ())), preferred_element_type=F32)


def _dot_nt(a, b):
    return lax.dot_general(a, b, (((1,), (1,)), ((), ())), preferred_element_type=F32)


def _dot_tn(a, b):
    return lax.dot_general(a, b, (((0,), (0,)), ((), ())), preferred_element_type=F32)


def _split_bf16(a):
    hi = a.astype(BF16)
    lo = (a - hi.astype(F32)).astype(BF16)
    return hi, lo


def _dot3(a, b):
    ah, al = _split_bf16(a)
    bh, bl = _split_bf16(b)
    return _dot_nn(ah, bh) + _dot_nn(ah, bl) + _dot_nn(al, bh)


def _sigmoid(g):
    return 1.0 / (1.0 + jnp.exp(-g))


def _silu(g):
    return g * _sigmoid(g)


def _lane_iota(shape):
    return lax.broadcasted_iota(jnp.int32, shape, len(shape) - 1)


def _adamw(w, g, m, v):
    m = ADAM_B1 * m + (1.0 - ADAM_B1) * g
    v = ADAM_B2 * v + (1.0 - ADAM_B2) * (g * g)
    m_hat = m / (1.0 - ADAM_B1 ** ADAM_STEP)
    v_hat = v / (1.0 - ADAM_B2 ** ADAM_STEP)
    delta = -ADAM_LR * (m_hat / (jnp.sqrt(v_hat) + ADAM_EPS) + ADAM_WD * w)
    return delta, m, v


def _mesh_pos():
    return lax.axis_index("x"), lax.axis_index("y"), lax.axis_index("c")


def _peer(pos, k):
    x, y, c = pos
    return (1 - x if k & 4 else x, 1 - y if k & 2 else y, 1 - c if k & 1 else c)


def _lin(pos):
    return 4 * pos[0] + 2 * pos[1] + pos[2]


def _remote(src, dst, send_sems, recv_sems, idx, peer):
    return pltpu.make_async_remote_copy(
        src_ref=src, dst_ref=dst, send_sem=send_sems.at[idx], recv_sem=recv_sems.at[idx],
        device_id=peer, device_id_type=MESH_ID)


def _all_gather_start(pos, src, buf, send_sems, recv_sems, base):
    me = _lin(pos)
    sent = []
    for k in range(1, N_DEV):
        cp = _remote(src, buf.at[me], send_sems, recv_sems, base + k - 1, _peer(pos, k))
        cp.start()
        sent.append(cp)
    return sent


def _all_gather_wait(pos, src, buf, send_sems, recv_sems, base, sent):
    for k in range(1, N_DEV):
        peer = _peer(pos, k)
        _remote(src, buf.at[_lin(peer)], send_sems, recv_sems, base + k - 1, peer).wait_recv()
    for cp in sent:
        cp.wait_send()


def _all_gather(pos, src, buf, send_sems, recv_sems, base):
    sent = _all_gather_start(pos, src, buf, send_sems, recv_sems, base)
    _all_gather_wait(pos, src, buf, send_sems, recv_sems, base, sent)


def _all_to_all_start(pos, src, buf, send_sems, recv_sems, base):
    me = _lin(pos)
    sent = []
    for k in range(1, N_DEV):
        peer = _peer(pos, k)
        cp = _remote(src.at[_lin(peer)], buf.at[me], send_sems, recv_sems, base + k - 1, peer)
        cp.start()
        sent.append(cp)
    return sent


def _all_to_all_wait(pos, src, buf, send_sems, recv_sems, base, sent):
    me = _lin(pos)
    for k in range(1, N_DEV):
        peer = _peer(pos, k)
        _remote(src.at[me], buf.at[_lin(peer)], send_sems, recv_sems, base + k - 1, peer).wait_recv()
    for cp in sent:
        cp.wait_send()


def _all_to_all(pos, src, buf, send_sems, recv_sems, base):
    sent = _all_to_all_start(pos, src, buf, send_sems, recv_sems, base)
    _all_to_all_wait(pos, src, buf, send_sems, recv_sems, base, sent)


def _two_level_gather(pos, bufs, send_sems, recv_sems):
    x, y, c = pos
    me, sibling = (x, y, c), (x, y, 1 - c)
    chips = [(1 - x, y), (x, 1 - y), (1 - x, 1 - y)]

    def copy(a, k, block, to):
        slot = bufs[a].at[_lin(block)]
        return _remote(slot, slot, send_sems, recv_sems, 7 * a + k, to)

    started = []
    for a in range(len(bufs)):
        first = [copy(a, 0, me, sibling)] + [copy(a, 1 + j, me, (*chip, c)) for j, chip in enumerate(chips)]
        for cp in first:
            cp.start()
        started += first
    for a in range(len(bufs)):
        for j, chip in enumerate(chips):
            copy(a, 1 + j, (*chip, c), me).wait_recv()
            passed = copy(a, 4 + j, (*chip, c), sibling)
            passed.start()
            started.append(passed)
    for a in range(len(bufs)):
        copy(a, 0, sibling, me).wait_recv()
        for j, chip in enumerate(chips):
            copy(a, 4 + j, (*chip, 1 - c), me).wait_recv()
    for cp in started:
        cp.wait_send()


def _vmem_spec():
    return pl.BlockSpec(memory_space=pltpu.VMEM)


def _any_spec():
    return pl.BlockSpec(memory_space=pl.ANY)


def _row_select(slots, n):
    r = lax.broadcasted_iota(jnp.int32, (N_DEV, n), 0)
    out = jnp.zeros((N_DEV, n), F32)
    for j in range(N_DEV):
        out = out + jnp.where(r == j, slots[j], 0.0)
    return out


def _ada_fwd(c8, w_ada, b_ada8):
    n_sh = w_ada.shape[1]

    def body(c_ref, w_ref, b_ref, out_ref, call_out_ref, call_ref, psend_ref, precv_ref, ssem, rsem):
        pos = _mesh_pos()
        me = _lin(pos)
        call_ref[me] = c_ref[...]
        _all_gather(pos, c_ref, call_ref, ssem, rsem, 0)
        call_out_ref[...] = _row_select([call_ref[j] for j in range(N_DEV)], D_MODEL)
        w = w_ref[...]
        for j in range(N_DEV):
            psend_ref[j] = _dot3(_silu(call_ref[j]), w)
        precv_ref[me] = psend_ref[me]
        _all_to_all(pos, psend_ref, precv_ref, ssem, rsem, N_DEV - 1)
        out_ref[...] = _row_select([precv_ref[j] for j in range(N_DEV)], n_sh) + b_ref[...]

    return pl.pallas_call(
        body, name="ada_fwd",
        out_shape=[jax.ShapeDtypeStruct((N_DEV, n_sh), F32), jax.ShapeDtypeStruct((N_DEV, D_MODEL), F32)],
        in_specs=[_vmem_spec()] * 3, out_specs=[_vmem_spec()] * 2,
        scratch_shapes=[
            pltpu.VMEM((N_DEV, 8, D_MODEL), F32),
            pltpu.VMEM((N_DEV, 8, n_sh), F32),
            pltpu.VMEM((N_DEV, 8, n_sh), F32),
            pltpu.SemaphoreType.DMA((2 * (N_DEV - 1),)),
            pltpu.SemaphoreType.DMA((2 * (N_DEV - 1),)),
        ],
    )(c8, w_ada, b_ada8)


def _gather_weights(shards):
    n = len(shards)

    def body(*refs):
        ins, outs = refs[:n], refs[n:2 * n]
        ssem, rsem = refs[2 * n], refs[2 * n + 1]
        pos = _mesh_pos()
        me = _lin(pos)
        for a in range(n):
            outs[a][me] = ins[a][...].astype(BF16)
        _two_level_gather(pos, outs, ssem, rsem)

    return pl.pallas_call(
        body, name="gather_weights",
        out_shape=[jax.ShapeDtypeStruct((N_DEV,) + s.shape, BF16) for s in shards],
        in_specs=[_vmem_spec()] * n, out_specs=[_vmem_spec()] * n,
        scratch_shapes=[
            pltpu.SemaphoreType.DMA((n * (N_DEV - 1),)),
            pltpu.SemaphoreType.DMA((n * (N_DEV - 1),)),
        ],
        compiler_params=pltpu.CompilerParams(vmem_limit_bytes=VMEM_BIG),
    )(*shards)


def _rope(t, cosf, sin_a, sin_b):
    return t * cosf + pltpu.roll(t, 112, 1) * sin_a + pltpu.roll(t, 16, 1) * sin_b


def _rope_t(d, cosf, sin_a, sin_b):
    return d * cosf + pltpu.roll(d * sin_a, 16, 1) + pltpu.roll(d * sin_b, 112, 1)


def _head_rms(t):
    return lax.rsqrt(jnp.sum(t * t, axis=1, keepdims=True) * (1.0 / MLA_QK) + EPS)


def _rms_bwd(dxhat_w, xhat, r, n):
    return r * (dxhat_w - xhat * (jnp.sum(dxhat_w * xhat, axis=1, keepdims=True) * (1.0 / n)))


def _mla_latents(cq, ckv, kr, qln, kvln, wuq, wukv):
    rq = lax.rsqrt(jnp.mean(cq * cq, axis=1, keepdims=True) + EPS)
    rkv = lax.rsqrt(jnp.mean(ckv * ckv, axis=1, keepdims=True) + EPS)
    cq_hat = cq * rq
    ckv_hat = ckv * rkv
    cqn = (cq_hat * qln).astype(BF16)
    ckvn = (ckv_hat * kvln).astype(BF16)
    q_all = _dot_nn(cqn, wuq)
    kv = _dot_nn(ckvn, wukv)
    kr64 = pltpu.roll(kr, 64, 1)
    return rq, rkv, cq_hat, ckv_hat, cqn, ckvn, q_all, kv, kr64


def _fwd_pre(x, shift, scale, norm_w, w_in_r, qln, wuq, kvln, wukv, qhn, khn, cosf, sin_a, sin_b):
    s_len = x.shape[0]

    def body(x_ref, shift_ref, scale_ref, nw_ref, w_ref, qln_ref, wuq_ref, kvln_ref, wukv_ref,
             qhn_ref, khn_ref, cos_ref, sa_ref, sb_ref,
             hb_ref, qkv_ref, gsb_ref, cq_ref, ckv_ref, gmla_ref, kr_ref, qm_ref, km_ref, vm_ref,
             qsbt_ref, qmt_ref):
        xv = x_ref[...]
        r = lax.rsqrt(jnp.mean(xv * xv, axis=1, keepdims=True) + EPS)
        h = (xv * r) * nw_ref[...] * (1.0 + scale_ref[...]) + shift_ref[...]
        hb = h.astype(BF16)
        hb_ref[...] = hb
        qkv = _dot_nn(hb, w_ref[:, 0:1536])
        qkv_ref[...] = qkv.astype(BF16)
        qsbt_ref[...] = qkv[:, :SB_WIDTH].T.astype(BF16)
        gsb_ref[...] = _dot_nn(hb, w_ref[:, 1536:2048])
        cq = _dot_nn(hb, w_ref[:, 2048:2432])
        ckv = _dot_nn(hb, w_ref[:, 2432:2688])
        gmla_ref[...] = _dot_nn(hb, w_ref[:, 2688:3200])
        kr = _dot_nn(hb, w_ref[:, 3200:3328])
        cq_ref[...] = cq
        ckv_ref[...] = ckv
        kr_ref[...] = kr
        _, _, _, _, _, _, q_all, kv, kr64 = _mla_latents(
            cq, ckv, kr, qln_ref[...], kvln_ref[...], wuq_ref[...], wukv_ref[...])
        cosf, sa, sb = cos_ref[...], sa_ref[...], sb_ref[...]
        qhn_v, khn_v = qhn_ref[...], khn_ref[...]
        low = _lane_iota((TS_FWD, LANES)) < NOPE
        blks = [slice(LANES * hd, LANES * (hd + 1)) for hd in range(HEADS)]
        q_raw = [q_all[:, b] for b in blks]
        k_raw = [jnp.where(low, kv[:, b], kr64) for b in blks]
        q_rms = [_head_rms(t) for t in q_raw]
        k_rms = [_head_rms(t) for t in k_raw]
        q_n = [t * r * (qhn_v * MLA_SCALE) for t, r in zip(q_raw, q_rms)]
        q_roped = [_rope(t, cosf, sa, sb) for t in q_n]
        kr_roped = _rope(kr64 * khn_v, cosf, sa, sb)
        k_roped = [jnp.where(low, t * khn_v, kr_roped) * r for t, r in zip(k_raw, k_rms)]
        for hd, b in enumerate(blks):
            qm_ref[:, b] = q_roped[hd].astype(BF16)
            qmt_ref[b, :] = q_roped[hd].T.astype(BF16)
            km_ref[:, b] = k_roped[hd].astype(BF16)
        for p in range(HEADS // 2):
            even = kv[:, LANES * 2 * p:LANES * (2 * p + 1)]
            odd = kv[:, LANES * (2 * p + 1):LANES * (2 * p + 2)]
            vm_ref[:, LANES * p:LANES * (p + 1)] = jnp.where(low, pltpu.roll(even, 64, 1), odd).astype(BF16)

    def tile(width):
        return pl.BlockSpec((TS_FWD, width), lambda i: (i, 0))

    def full(a):
        return pl.BlockSpec(a.shape, lambda i: (0, 0))

    out_widths = [(D_MODEL, BF16), (1536, BF16), (512, F32), (Q_LORA, F32), (KV_LORA, F32),
                  (512, F32), (LANES, F32), (1024, BF16), (1024, BF16), (512, BF16)]
    t_heights = [SB_WIDTH, HEADS * LANES]
    return pl.pallas_call(
        body, name="fwd_pre", grid=(s_len // TS_FWD,),
        out_shape=[jax.ShapeDtypeStruct((s_len, w), dt) for w, dt in out_widths]
        + [jax.ShapeDtypeStruct((hgt, s_len), BF16) for hgt in t_heights],
        in_specs=[tile(D_MODEL), full(shift), full(scale), full(norm_w), full(w_in_r), full(qln), full(wuq),
                  full(kvln), full(wukv), full(qhn), full(khn), tile(LANES), tile(LANES), tile(LANES)],
        out_specs=[tile(w) for w, _ in out_widths]
        + [pl.BlockSpec((hgt, TS_FWD), lambda i: (0, i)) for hgt in t_heights],
        compiler_params=pltpu.CompilerParams(dimension_semantics=("parallel",), vmem_limit_bytes=VMEM_BIG),
    )(x, shift, scale, norm_w, w_in_r, qln, wuq, kvln, wukv, qhn, khn, cosf, sin_a, sin_b)


CUM_W = 256


def _tri(strict):
    j = lax.broadcasted_iota(jnp.int32, (CUM_W, CUM_W), 0)
    s = lax.broadcasted_iota(jnp.int32, (CUM_W, CUM_W), 1)
    return (j > s if strict else j >= s).astype(BF16)


def _suffix_sums(a, a_bf16, tri_m, carry):
    n = a.shape[1] // CUM_W
    outs = [None] * n
    for i in reversed(range(n)):
        cols = slice(CUM_W * i, CUM_W * (i + 1))
        outs[i] = _dot_nn(a_bf16[:, cols], tri_m) + carry
        carry = carry + _rowsum(a[:, cols])
    return (outs[0] if n == 1 else jnp.concatenate(outs, axis=1)), carry


def _sb_weights(qm, kb, carry, tri_u, diag):
    z = _dot_nt(qm, kb)
    nz = -z
    lk = jnp.minimum(nz, 0.0) - jnp.log(1.0 + jnp.exp(jnp.minimum(z, nz)))
    if diag:
        t = lax.broadcasted_iota(jnp.int32, (TQ, TQ), 0)
        s = lax.broadcasted_iota(jnp.int32, (TQ, TQ), 1)
        valid = s < t
        lk = jnp.where(valid, lk, 0.0)
    lk_hi = lk.astype(BF16)
    log_beta = z + lk
    after, carry = _suffix_sums(lk, lk_hi, tri_u, carry)
    logw = log_beta + after
    if diag:
        logw = jnp.where(valid, logw, NEG)
    return lk_hi, jnp.exp(logw), carry


SB_SCALE = 0.125


def _head_masks():
    lane = _lane_iota((1, LANES))
    return [lane < 64, lane >= 64]


def _masked(hm, a):
    return jnp.where(hm, a, jnp.zeros_like(a))


def _rowsum(a):
    return jnp.sum(a, axis=1, keepdims=True)


def _key_rows(kj):
    return pl.ds(pl.multiple_of(kj * TQ, TQ), TQ)


def _over_key_tiles(count, fn, st, ascending):
    n_full = count // KEY_UNROLL
    n_rest = count - n_full * KEY_UNROLL

    def group(g, s_):
        return fn([g * KEY_UNROLL + (u if ascending else KEY_UNROLL - 1 - u) for u in range(KEY_UNROLL)], s_)

    if ascending:
        st = lax.fori_loop(0, n_full, group, st)
        return lax.fori_loop(0, n_rest, lambda i, s_: fn([n_full * KEY_UNROLL + i], s_), st)
    st = lax.fori_loop(0, n_rest, lambda i, s_: fn([count - 1 - i], s_), st)
    return lax.fori_loop(0, n_full, lambda i, s_: group(n_full - 1 - i, s_), st)


def _each_tile(block):
    def trip(tiles, st):
        for kj in tiles:
            st = block(_key_rows(kj), st, False)
        return st
    return trip


STAGE_SLOTS = 2 * KEY_UNROLL


def _stage_copies(to_hbm, hbm_refs, scr_refs, sems, pair, qi, kj):
    slot = (qi - kj) % STAGE_SLOTS
    out = []
    for h in range(2):
        for a in range(2):
            hbm, scr = hbm_refs[a].at[2 * pair + h, qi, kj], scr_refs[a].at[slot, h]
            sem = sems.at[4 * slot + 2 * h + a]
            out.append(pltpu.make_async_copy(scr, hbm, sem) if to_hbm else pltpu.make_async_copy(hbm, scr, sem))
    return out


def _sb_fwd(qkv):
    s_len = qkv.shape[0]
    nq = s_len // TQ

    def body(q_ref, k_ref, v_ref, o_ref, w_hbm, l_hbm, w_scr, l_scr, sems):
        pair, qi = pl.program_id(0), pl.program_id(1)
        q = q_ref[...]
        tri_u = _tri(True)
        masks = _head_masks()
        qms = [_masked(hm, q) * SB_SCALE for hm in masks]

        def copies(kj, of_qi=qi):
            return _stage_copies(True, (w_hbm, l_hbm), (w_scr, l_scr), sems, pair, of_qi, kj)

        def drain(of_qi):
            for kj in range(STAGE_SLOTS):
                @pl.when(kj <= of_qi)
                def _():
                    for cp in copies(kj, of_qi):
                        cp.wait()

        def block(kj, st, diag, before_staging=None):
            rows = _key_rows(kj)
            slot = (qi - kj) % STAGE_SLOTS
            kb, vb = k_ref[rows, :], v_ref[rows, :]
            carries, acc = list(st[:2]), st[2]
            staged = []
            for h in range(2):
                lk_hi, w, carries[h] = _sb_weights(qms[h], kb, carries[h], tri_u, diag)
                wb = w.astype(BF16)
                staged.append((wb, lk_hi))
                acc = acc + _dot_nn(wb, _masked(masks[h], vb))
            if before_staging is not None:
                before_staging()
            for h in range(2):
                w_scr[slot, h], l_scr[slot, h] = staged[h]
            return carries[0], carries[1], acc

        def trip(tiles, st):
            for kj in tiles:
                @pl.when(qi - kj >= STAGE_SLOTS)
                def _():
                    for cp in copies(kj + STAGE_SLOTS):
                        cp.wait()
            for kj in tiles:
                st = block(kj, st, False)
            for kj in tiles:
                for cp in copies(kj):
                    cp.start()
            return st

        def drain_previous_step():
            @pl.when(jnp.logical_or(pair > 0, qi > 0))
            def _():
                drain(jnp.where(qi == 0, nq - 1, qi - 1))

        zc = jnp.zeros((TQ, 1), F32)
        st = block(qi, (zc, zc, jnp.zeros((TQ, LANES), F32)), True, drain_previous_step)
        for cp in copies(qi):
            cp.start()
        st = _over_key_tiles(qi, trip, st, ascending=False)
        o_ref[...] = st[2]

        @pl.when(jnp.logical_and(pair == HEADS // 2 - 1, qi == nq - 1))
        def _():
            drain(qi)

    saved = jax.ShapeDtypeStruct((HEADS, nq, nq, TQ, TQ), BF16)
    stage = pltpu.VMEM((STAGE_SLOTS, 2, TQ, TQ), BF16)
    return pl.pallas_call(
        body, name="sb_fwd", grid=(HEADS // 2, nq),
        out_shape=[jax.ShapeDtypeStruct((s_len, SB_WIDTH), F32), saved, saved],
        in_specs=[pl.BlockSpec((TQ, LANES), lambda p, i: (i, p)),
                  pl.BlockSpec((s_len, LANES), lambda p, i: (0, 4 + p)),
                  pl.BlockSpec((s_len, LANES), lambda p, i: (0, 8 + p))],
        out_specs=[pl.BlockSpec((TQ, LANES), lambda p, i: (i, p)), _any_spec(), _any_spec()],
        scratch_shapes=[stage, stage, pltpu.SemaphoreType.DMA((4 * STAGE_SLOTS,))],
        compiler_params=pltpu.CompilerParams(dimension_semantics=("arbitrary", "arbitrary"),
                                             vmem_limit_bytes=VMEM_BIG),
    )(qkv, qkv, qkv)


def _prefix_sums(a, tri_m, carry):
    n = a.shape[1] // CUM_W
    outs = [None] * n
    for i in range(n):
        cols = slice(CUM_W * i, CUM_W * (i + 1))
        outs[i] = _dot_nn(a[:, cols].astype(BF16), tri_m) + carry
        carry = carry + _rowsum(a[:, cols])
    return (outs[0] if n == 1 else jnp.concatenate(outs, axis=1)), carry


def _head_rows(a0, a1):
    sub = lax.broadcasted_iota(jnp.int32, a0.shape, 0)
    return jnp.where(sub < NOPE, a0, a1)


def _sb_bwd(qkv, q_t, do, do_t, w_saved, l_saved):
    s_len = qkv.shape[0]
    nq = s_len // TQ

    def body(qt_ref, k_ref, v_ref, do_ref, dot_ref, w_hbm, l_hbm, dq_ref, dk_ref, dv_ref, w_scr, l_scr, sems):
        pair, qi = pl.program_id(0), pl.program_id(1)

        def copies(kj, of_pair=pair, of_qi=qi):
            return _stage_copies(False, (w_hbm, l_hbm), (w_scr, l_scr), sems, of_pair, of_qi, kj)

        def start_first_tiles(of_pair, of_qi):
            for cp in copies(0, of_pair, of_qi):
                cp.start()

            @pl.when(of_qi >= 1)
            def _():
                for cp in copies(1, of_pair, of_qi):
                    cp.start()

        @pl.when(jnp.logical_and(pair == 0, qi == 0))
        def _():
            start_first_tiles(pair, qi)

        @pl.when(qi == 0)
        def _():
            dk_ref[...] = jnp.zeros_like(dk_ref)
            dv_ref[...] = jnp.zeros_like(dv_ref)

        qt = qt_ref[...] * SB_SCALE
        dot_v = dot_ref[...]
        do_v = do_ref[...]
        j = lax.broadcasted_iota(jnp.int32, (CUM_W, CUM_W), 0)
        s = lax.broadcasted_iota(jnp.int32, (CUM_W, CUM_W), 1)
        tri_before = (j < s).astype(BF16)
        masks = _head_masks()
        doms = [_masked(hm, do_v) for hm in masks]

        def block(kj, st):
            rows = _key_rows(kj)
            slot = (qi - kj) % STAGE_SLOTS
            kb, vb = k_ref[rows, :], v_ref[rows, :]
            carries, dqs = list(st[0:2]), list(st[2:4])
            dk_t, dv_t = [], []
            for h in range(2):
                wb = w_scr[slot, h]
                d_l = _dot_nt(doms[h], vb) * wb.astype(F32)
                before, carries[h] = _prefix_sums(d_l, tri_before, carries[h])
                keep = jnp.exp(l_scr[slot, h].astype(F32))
                dzb = (d_l * keep - before * (1.0 - keep)).astype(BF16)
                dk_t.append(_dot_nn(qt, dzb))
                dv_t.append(_dot_nn(dot_v, wb))
                dqs[h] = dqs[h] + _dot_nn(dzb, kb)
            dk_ref[kj] += _head_rows(*dk_t)
            dv_ref[kj] += _head_rows(*dv_t)
            return (*carries, *dqs)

        def trip(tiles, st):
            for kj in tiles:
                @pl.when(kj + 2 <= qi)
                def _():
                    for cp in copies(kj + 2):
                        cp.start()
            for kj in tiles:
                for cp in copies(kj):
                    cp.wait()
            for kj in tiles:
                st = block(kj, st)
            return st

        zc = jnp.zeros((TQ, 1), F32)
        za = jnp.zeros((TQ, LANES), F32)
        st = _over_key_tiles(qi + 1, trip, (zc, zc, za, za), ascending=True)
        dq_ref[...] = jnp.where(masks[0], st[2], st[3]) * SB_SCALE

        @pl.when(jnp.logical_or(pair < HEADS // 2 - 1, qi < nq - 1))
        def _():
            wraps = qi == nq - 1
            start_first_tiles(jnp.where(wraps, pair + 1, pair), jnp.where(wraps, 0, qi + 1))

    tile = pl.BlockSpec((TQ, LANES), lambda p, i: (i, p))
    tile_t = pl.BlockSpec((LANES, TQ), lambda p, i: (p, i))
    col_t = pl.BlockSpec((nq, LANES, TQ), lambda p, i: (0, p, 0))
    stage = pltpu.VMEM((STAGE_SLOTS, 2, TQ, TQ), BF16)
    key_t = jax.ShapeDtypeStruct((nq, SB_WIDTH, TQ), F32)
    return pl.pallas_call(
        body, name="sb_bwd", grid=(HEADS // 2, nq),
        out_shape=[jax.ShapeDtypeStruct((s_len, SB_WIDTH), F32), key_t, key_t],
        in_specs=[tile_t,
                  pl.BlockSpec((s_len, LANES), lambda p, i: (0, 4 + p)),
                  pl.BlockSpec((s_len, LANES), lambda p, i: (0, 8 + p)),
                  tile, tile_t, _any_spec(), _any_spec()],
        out_specs=[tile, col_t, col_t],
        scratch_shapes=[stage, stage, pltpu.SemaphoreType.DMA((4 * STAGE_SLOTS,))],
        compiler_params=pltpu.CompilerParams(dimension_semantics=("arbitrary", "arbitrary"),
                                             vmem_limit_bytes=VMEM_BIG),
    )(q_t, qkv, qkv, do, do_t, w_saved, l_saved)


def _from_key_tiles(a_t):
    return a_t.transpose(0, 2, 1).reshape(a_t.shape[0] * a_t.shape[2], a_t.shape[1])


MLA_SCALE = 1.0 / math.sqrt(MLA_QK)


def _causal_mask():
    t = lax.broadcasted_iota(jnp.int32, (TQ, TQ), 0)
    s = lax.broadcasted_iota(jnp.int32, (TQ, TQ), 1)
    return s <= t


def _head_lanes(h):
    return slice(LANES * h, LANES * (h + 1))


P_SLOTS = 2 * KEY_UNROLL
P_COLS = TQ + LANES


def _tile_number(pair, qi, kj, nq):
    return pair * (nq * (nq + 1) // 2) + (qi * (qi + 1)) // 2 + kj


def _p_copy(to_hbm, p_hbm, p_scr, sems, pair, qi, kj, nq, h):
    slot = _tile_number(pair, qi, kj, nq) % P_SLOTS
    hbm, scr, sem = p_hbm.at[2 * pair + h, qi, kj], p_scr.at[slot, h], sems.at[2 * slot + h]
    return pltpu.make_async_copy(scr, hbm, sem) if to_hbm else pltpu.make_async_copy(hbm, scr, sem)


def _mla_fwd(qm, km, vm):
    s_len = qm.shape[0]
    nq = s_len // TQ

    def body(q_ref, k_ref, v_ref, o_ref, lse_ref, p_hbm, p_scr, sems):
        pair, qi = pl.program_id(0), pl.program_id(1)
        masks = _head_masks()
        qhs = [q_ref[:, _head_lanes(h)] for h in range(2)]
        lane = _lane_iota((TQ, LANES))

        def number(kj):
            return _tile_number(pair, qi, kj, nq)

        def copies(kj):
            return [_p_copy(True, p_hbm, p_scr, sems, pair, qi, kj, nq, h) for h in range(2)]

        def three_parts(m):
            hi = m.astype(BF16).astype(F32)
            mid = (m - hi).astype(BF16).astype(F32)
            lo = m - hi - mid
            return jnp.where(lane == 0, hi, jnp.where(lane == 1, mid, jnp.where(lane == 2, lo, 0.0))).astype(BF16)

        def block(kj, st, diag):
            rows = _key_rows(kj)
            slot = number(kj) % P_SLOTS
            vb = v_ref[rows, :]
            ms, ls, acc = list(st[0:2]), list(st[2:4]), st[4]
            alphas, pvs = [], []
            for h in range(2):
                s = _dot_nt(qhs[h], k_ref[rows, _head_lanes(h)])
                if diag:
                    s = jnp.where(_causal_mask(), s, NEG)
                m_new = jnp.maximum(ms[h], jnp.max(s, axis=1, keepdims=True))
                p = jnp.exp(s - m_new)
                pb = p.astype(BF16)
                p_scr[slot, h, :, 0:TQ] = pb
                p_scr[slot, h, :, TQ:P_COLS] = three_parts(m_new)
                alphas.append(jnp.exp(ms[h] - m_new))
                ls[h] = alphas[h] * ls[h] + _rowsum(p)
                ms[h] = m_new
                pvs.append(_dot_nn(pb, _masked(masks[h], vb)))
            acc = jnp.where(masks[0], alphas[0], alphas[1]) * acc + pvs[0] + pvs[1]
            return (*ms, *ls, acc)

        def trip(tiles, st, diag=False):
            for kj in tiles:
                @pl.when(number(kj) >= P_SLOTS)
                def _():
                    for cp in copies(kj):
                        cp.wait()
            for kj in tiles:
                st = block(kj, st, diag)
            for kj in tiles:
                for cp in copies(kj):
                    cp.start()
            return st

        neg = jnp.full((TQ, 1), NEG, F32)
        zc = jnp.zeros((TQ, 1), F32)
        st = (neg, neg, zc, zc, jnp.zeros((TQ, LANES), F32))
        st = _over_key_tiles(qi, trip, st, ascending=True)
        m0, m1, l0, l1, acc = trip([qi], st, True)
        o_ref[...] = acc / jnp.where(masks[0], l0, l1)
        lse_ref[0] = m0 + jnp.log(l0)
        lse_ref[1] = m1 + jnp.log(l1)

        @pl.when(jnp.logical_and(pair == HEADS // 2 - 1, qi == nq - 1))
        def _():
            for slot in range(P_SLOTS):
                for h in range(2):
                    pltpu.make_async_copy(p_scr.at[slot, h], p_hbm.at[0, 0, 0], sems.at[2 * slot + h]).wait()

    return pl.pallas_call(
        body, name="mla_fwd", grid=(HEADS // 2, nq),
        out_shape=[jax.ShapeDtypeStruct((s_len, MLA_WIDTH), F32),
                   jax.ShapeDtypeStruct((HEADS, s_len, 1), F32),
                   jax.ShapeDtypeStruct((HEADS, nq, nq, TQ, P_COLS), BF16)],
        in_specs=[pl.BlockSpec((TQ, 2 * LANES), lambda p, i: (i, p)),
                  pl.BlockSpec((s_len, 2 * LANES), lambda p, i: (0, p)),
                  pl.BlockSpec((s_len, LANES), lambda p, i: (0, p))],
        out_specs=[pl.BlockSpec((TQ, LANES), lambda p, i: (i, p)),
                   pl.BlockSpec((2, TQ, 1), lambda p, i: (p, i, 0)),
                   _any_spec()],
        scratch_shapes=[pltpu.VMEM((P_SLOTS, 2, TQ, P_COLS), BF16), pltpu.SemaphoreType.DMA((2 * P_SLOTS,))],
        compiler_params=pltpu.CompilerParams(dimension_semantics=("arbitrary", "arbitrary"),
                                             vmem_limit_bytes=VMEM_BIG),
    )(qm, km, vm)


def _mla_bwd(qm_t, km, vm, o, do, do_t, lse, p_saved):
    s_len = km.shape[0]
    nq = s_len // TQ
    total = (HEADS // 2) * (nq * (nq + 1) // 2)

    def body(qt_ref, k_ref, v_ref, o_ref, do_ref, dot_ref, lse_ref, p_hbm, dq_ref, dk_ref, dv_ref, p_scr, sems):
        pair, qi = pl.program_id(0), pl.program_id(1)

        def number(kj):
            return _tile_number(pair, qi, kj, nq)

        def fetch(p_, q_, k_):
            for h in range(2):
                _p_copy(False, p_hbm, p_scr, sems, p_, q_, k_, nq, h).start()

        def advance(p_, q_, k_):
            row_end = k_ == q_
            last_row = q_ == nq - 1
            return (jnp.where(jnp.logical_and(row_end, last_row), p_ + 1, p_),
                    jnp.where(row_end, jnp.where(last_row, 0, q_ + 1), q_),
                    jnp.where(row_end, 0, k_ + 1))

        @pl.when(jnp.logical_and(pair == 0, qi == 0))
        def _():
            fetch(pair, qi, 0)
            fetch(*advance(pair, qi, 0))

        @pl.when(qi == 0)
        def _():
            dk_ref[...] = jnp.zeros_like(dk_ref)
            dv_ref[...] = jnp.zeros_like(dv_ref)

        do_v = do_ref[...]
        dot_v = dot_ref[...]
        od = o_ref[...] * do_v.astype(F32)
        masks = _head_masks()
        qts = [qt_ref[_head_lanes(h), :] for h in range(2)]
        doms = [_masked(hm, do_v) for hm in masks]
        deltas = [_rowsum(jnp.where(hm, od, 0.0)) for hm in masks]
        lses = [lse_ref[h] for h in range(2)]

        def block(kj, dqs):
            rows = _key_rows(kj)
            slot = number(kj) % P_SLOTS
            vb = v_ref[rows, :]
            dqs = list(dqs)
            dv_t = []
            for h in range(2):
                kb = k_ref[rows, _head_lanes(h)]
                tile_max = _rowsum(p_scr[slot, h, :, TQ:P_COLS].astype(F32))
                p = p_scr[slot, h, :, 0:TQ].astype(F32) * jnp.exp(tile_max - lses[h])
                dp = _dot_nt(doms[h], vb)
                ds = (p * (dp - deltas[h])).astype(BF16)
                dk_ref[kj, _head_lanes(h), :] += _dot_nn(qts[h], ds)
                dv_t.append(_dot_nn(dot_v, p.astype(BF16)))
                dqs[h] = dqs[h] + _dot_nn(ds, kb)
            dv_ref[kj] += _head_rows(*dv_t)
            return tuple(dqs)

        def trip(tiles, dqs):
            for kj in tiles:
                @pl.when(number(kj) + 2 < total)
                def _():
                    fetch(*advance(*advance(pair, qi, kj)))
            for kj in tiles:
                for h in range(2):
                    _p_copy(False, p_hbm, p_scr, sems, pair, qi, kj, nq, h).wait()
            for kj in tiles:
                dqs = block(kj, dqs)
            return dqs

        za = jnp.zeros((TQ, LANES), F32)
        dqs = _over_key_tiles(qi + 1, trip, (za, za), ascending=True)
        dq_ref[:, _head_lanes(0)] = dqs[0] * MLA_SCALE
        dq_ref[:, _head_lanes(1)] = dqs[1] * MLA_SCALE

    return pl.pallas_call(
        body, name="mla_bwd", grid=(HEADS // 2, nq),
        out_shape=[jax.ShapeDtypeStruct((s_len, HEADS * LANES), F32),
                   jax.ShapeDtypeStruct((nq, HEADS * LANES, TQ), F32),
                   jax.ShapeDtypeStruct((nq, MLA_WIDTH, TQ), F32)],
        in_specs=[pl.BlockSpec((2 * LANES, TQ), lambda p, i: (p, i)),
                  pl.BlockSpec((s_len, 2 * LANES), lambda p, i: (0, p)),
                  pl.BlockSpec((s_len, LANES), lambda p, i: (0, p)),
                  pl.BlockSpec((TQ, LANES), lambda p, i: (i, p)),
                  pl.BlockSpec((TQ, LANES), lambda p, i: (i, p)),
                  pl.BlockSpec((LANES, TQ), lambda p, i: (p, i)),
                  pl.BlockSpec((2, TQ, 1), lambda p, i: (p, i, 0)),
                  _any_spec()],
        out_specs=[pl.BlockSpec((TQ, 2 * LANES), lambda p, i: (i, p)),
                   pl.BlockSpec((nq, 2 * LANES, TQ), lambda p, i: (0, p, 0)),
                   pl.BlockSpec((nq, LANES, TQ), lambda p, i: (0, p, 0))],
        scratch_shapes=[pltpu.VMEM((P_SLOTS, 2, TQ, P_COLS), BF16), pltpu.SemaphoreType.DMA((2 * P_SLOTS,))],
        compiler_params=pltpu.CompilerParams(dimension_semantics=("arbitrary", "arbitrary"),
                                             vmem_limit_bytes=VMEM_BIG),
    )(qm_t, km, vm, o, do, do_t, lse, p_saved)


def _mid(o_sb, g_sb, o_mla, g_mla, x, target, gate, w_out):
    s_len = x.shape[0]

    def body(osb_ref, gsb_ref, omla_ref, gmla_ref, x_ref, t_ref, gate_ref, w_ref,
             dy_ref, dosb_ref, dgsb_ref, domla_ref, dgmla_ref, gw_ref, dgate_ref, loss_ref, dosbt_ref, domlat_ref):
        @pl.when(pl.program_id(0) == 0)
        def _():
            gw_ref[...] = jnp.zeros_like(gw_ref)
            dgate_ref[...] = jnp.zeros_like(dgate_ref)
            loss_ref[...] = jnp.zeros_like(loss_ref)

        o1, g1, o2, g2 = osb_ref[...], gsb_ref[...], omla_ref[...], gmla_ref[...]
        s1, s2 = _sigmoid(g1), _sigmoid(g2)
        mixed = jnp.concatenate([o1 * (g1 * s1), o2 * (g2 * s2)], axis=1).astype(BF16)
        w = w_ref[...]
        gate_v = gate_ref[...]
        u = _dot_nn(mixed, w)
        err = x_ref[...] + gate_v * u - t_ref[...]
        loss_ref[...] += jnp.sum(err * err, axis=0, keepdims=True)
        dy = err * (1.0 / D_MODEL)
        dy_ref[...] = dy
        dgate_ref[...] += jnp.sum(dy * u, axis=0, keepdims=True)
        du = (dy * gate_v).astype(BF16)
        gw_ref[...] += _dot_tn(mixed, du)
        dmixed = _dot_nt(du, w)
        d1, d2 = dmixed[:, :SB_WIDTH], dmixed[:, SB_WIDTH:]
        do1, do2 = d1 * (g1 * s1), d2 * (g2 * s2)
        dosb_ref[...] = do1.astype(BF16)
        dgsb_ref[...] = (d1 * o1 * (s1 * (1.0 + g1 * (1.0 - s1)))).astype(BF16)
        domla_ref[...] = do2.astype(BF16)
        dgmla_ref[...] = (d2 * o2 * (s2 * (1.0 + g2 * (1.0 - s2)))).astype(BF16)
        dosbt_ref[...] = do1.T.astype(BF16)
        domlat_ref[...] = do2.T.astype(BF16)

    def tile(width):
        return pl.BlockSpec((TS, width), lambda i: (i, 0))

    def full(shape):
        return pl.BlockSpec(shape, lambda i: (0, 0))

    return pl.pallas_call(
        body, name="mid", grid=(s_len // TS,),
        out_shape=[jax.ShapeDtypeStruct((s_len, D_MODEL), F32)]
        + [jax.ShapeDtypeStruct((s_len, 512), BF16)] * 4
        + [jax.ShapeDtypeStruct((D_MODEL, D_MODEL), F32),
           jax.ShapeDtypeStruct((1, D_MODEL), F32), jax.ShapeDtypeStruct((1, D_MODEL), F32)]
        + [jax.ShapeDtypeStruct((512, s_len), BF16)] * 2,
        in_specs=[tile(512)] * 4 + [tile(D_MODEL), tile(D_MODEL), full((1, D_MODEL)), full((D_MODEL, D_MODEL))],
        out_specs=[tile(D_MODEL)] + [tile(512)] * 4
        + [full((D_MODEL, D_MODEL)), full((1, D_MODEL)), full((1, D_MODEL))]
        + [pl.BlockSpec((512, TS), lambda i: (0, i))] * 2,
        compiler_params=pltpu.CompilerParams(dimension_semantics=("arbitrary",), vmem_limit_bytes=VMEM_BIG),
    )(o_sb, g_sb, o_mla, g_mla, x, target, gate, w_out)


def _mla_pre_bwd(dq, dk, dv, cq, ckv, kr, qln, wuq, kvln, wukv, qhn, khn, cosf, sin_a, sin_b):
    s_len = cq.shape[0]

    def body(dq_ref, dk_ref, dv_ref, cq_ref, ckv_ref, kr_ref, qln_ref, wuq_ref, kvln_ref, wukv_ref,
             qhn_ref, khn_ref, cos_ref, sa_ref, sb_ref,
             dcq_ref, dckv_ref, dkr_ref, gwuq_ref, gwukv_ref, gqhn_ref, gkhn_ref, gqln_ref, gkvln_ref,
             dqa_ref, dkv_ref):
        @pl.when(pl.program_id(0) == 0)
        def _():
            for r_ in (gwuq_ref, gwukv_ref, gqhn_ref, gkhn_ref, gqln_ref, gkvln_ref):
                r_[...] = jnp.zeros_like(r_)

        cq, ckv = cq_ref[...], ckv_ref[...]
        qln_v, kvln_v = qln_ref[...], kvln_ref[...]
        wuq_v, wukv_v = wuq_ref[...], wukv_ref[...]
        rq, rkv, cq_hat, ckv_hat, cqn, ckvn, q_all, kv, kr64 = _mla_latents(
            cq, ckv, kr_ref[...], qln_v, kvln_v, wuq_v, wukv_v)
        cosf, sa, sb = cos_ref[...], sa_ref[...], sb_ref[...]
        qhn_v, khn_v = qhn_ref[...], khn_ref[...]
        lane = _lane_iota((TS, LANES))
        low = lane < NOPE
        blks = [slice(LANES * hd, LANES * (hd + 1)) for hd in range(HEADS)]
        raw = [q_all[:, b] for b in blks] + [jnp.where(low, kv[:, b], kr64) for b in blks]
        grads = [dq_ref[:, b] for b in blks] + [dk_ref[:, b] for b in blks]
        gains = [qhn_v] * HEADS + [khn_v] * HEADS
        rms = [_head_rms(t) for t in raw]
        xhs = [t * r for t, r in zip(raw, rms)]
        dns = [_rope_t(d, cosf, sa, sb) for d in grads]
        gain_g = [jnp.sum(dn * xh, axis=0, keepdims=True) for dn, xh in zip(dns, xhs)]
        dxs = [_rms_bwd(dn * g, xh, r, MLA_QK) for dn, g, xh, r in zip(dns, gains, xhs, rms)]
        dkr64 = jnp.zeros((TS, LANES), F32)
        for hd, b in enumerate(blks):
            dqa_ref[:, b] = dxs[hd].astype(BF16)
            dkb = dxs[HEADS + hd]
            dkr64 = dkr64 + jnp.where(low, 0.0, dkb)
            dvp = dv_ref[:, LANES * (hd // 2):LANES * (hd // 2 + 1)]
            dvh = pltpu.roll(dvp, 64, 1) if hd % 2 == 0 else dvp
            dkv_ref[:, b] = jnp.where(low, dkb, dvh).astype(BF16)
        gqhn_ref[...] += sum(gain_g[:HEADS])
        gkhn_ref[...] += sum(gain_g[HEADS:])
        dkr_ref[...] = pltpu.roll(dkr64, 64, 1).astype(BF16)

        dqa = dqa_ref[...]
        gwuq_ref[...] += _dot_tn(cqn, dqa)
        dcqn = _dot_nt(dqa, wuq_v)
        gqln_ref[...] += jnp.sum(dcqn * cq_hat, axis=0, keepdims=True)
        dcq_ref[...] = _rms_bwd(dcqn * qln_v, cq_hat, rq, Q_LORA).astype(BF16)

        dkv = dkv_ref[...]
        gwukv_ref[...] += _dot_tn(ckvn, dkv)
        dckvn = _dot_nt(dkv, wukv_v)
        gkvln_ref[...] += jnp.sum(dckvn * ckv_hat, axis=0, keepdims=True)
        dckv_ref[...] = _rms_bwd(dckvn * kvln_v, ckv_hat, rkv, KV_LORA).astype(BF16)

    def tile(width):
        return pl.BlockSpec((TS, width), lambda i: (i, 0))

    def full(shape):
        return pl.BlockSpec(shape, lambda i: (0, 0))

    acc_shapes = [(Q_LORA, 1024), (KV_LORA, 1024), (1, LANES), (1, LANES), (1, Q_LORA), (1, KV_LORA)]
    return pl.pallas_call(
        body, name="mla_pre_bwd", grid=(s_len // TS,),
        out_shape=[jax.ShapeDtypeStruct((s_len, Q_LORA), BF16), jax.ShapeDtypeStruct((s_len, KV_LORA), BF16),
                   jax.ShapeDtypeStruct((s_len, LANES), BF16)]
        + [jax.ShapeDtypeStruct(s, F32) for s in acc_shapes],
        in_specs=[tile(1024), tile(1024), tile(512), tile(Q_LORA), tile(KV_LORA), tile(LANES),
                  full(qln.shape), full(wuq.shape), full(kvln.shape), full(wukv.shape),
                  full(qhn.shape), full(khn.shape), tile(LANES), tile(LANES), tile(LANES)],
        out_specs=[tile(Q_LORA), tile(KV_LORA), tile(LANES)] + [full(s) for s in acc_shapes],
        scratch_shapes=[pltpu.VMEM((TS, 1024), BF16), pltpu.VMEM((TS, 1024), BF16)],
        compiler_params=pltpu.CompilerParams(dimension_semantics=("arbitrary",), vmem_limit_bytes=VMEM_BIG),
    )(dq, dk, dv, cq, ckv, kr, qln, wuq, kvln, wukv, qhn, khn, cosf, sin_a, sin_b)


def _dproj_bwd(dq_sb, dk_sb, dv_sb, dg_sb, dcq, dckv, dg_mla, dkr, w_in_r, x, dy, norm_w, scale):
    s_len = x.shape[0]

    def body(dq_ref, dk_ref, dv_ref, dg_ref, dcq_ref, dckv_ref, dgm_ref, dkr_ref, w_ref, x_ref, dy_ref,
             nw_ref, scale_ref, dp_ref, gx_ref, dshift_ref, dscale_ref, dnw_ref):
        @pl.when(pl.program_id(0) == 0)
        def _():
            for r_ in (dshift_ref, dscale_ref, dnw_ref):
                r_[...] = jnp.zeros_like(r_)

        dp_ref[:, 0:512] = dq_ref[...].astype(BF16)
        dp_ref[:, 512:1024] = dk_ref[...].astype(BF16)
        dp_ref[:, 1024:1536] = dv_ref[...].astype(BF16)
        dp_ref[:, 1536:2048] = dg_ref[...]
        dp_ref[:, 2048:2432] = dcq_ref[...]
        dp_ref[:, 2432:2688] = dckv_ref[...]
        dp_ref[:, 2688:3200] = dgm_ref[...]
        dp_ref[:, 3200:3328] = dkr_ref[...]
        dh = _dot_nt(dp_ref[...], w_ref[...])
        xv = x_ref[...]
        r = lax.rsqrt(jnp.mean(xv * xv, axis=1, keepdims=True) + EPS)
        xh = xv * r
        nw = nw_ref[...]
        dshift_ref[...] += jnp.sum(dh, axis=0, keepdims=True)
        dscale_ref[...] += jnp.sum(dh * (xh * nw), axis=0, keepdims=True)
        dxnw = dh * (1.0 + scale_ref[...])
        dnw_ref[...] += jnp.sum(dxnw * xh, axis=0, keepdims=True)
        gx_ref[...] = dy_ref[...] + _rms_bwd(dxnw * nw, xh, r, D_MODEL)

    def tile(width):
        return pl.BlockSpec((TS, width), lambda i: (i, 0))

    def full(shape):
        return pl.BlockSpec(shape, lambda i: (0, 0))

    vec = (1, D_MODEL)
    return pl.pallas_call(
        body, name="dproj_bwd", grid=(s_len // TS,),
        out_shape=[jax.ShapeDtypeStruct((s_len, IN_COLS_R), BF16), jax.ShapeDtypeStruct((s_len, D_MODEL), F32)]
        + [jax.ShapeDtypeStruct(vec, F32)] * 3,
        in_specs=[tile(512)] * 4 + [tile(Q_LORA), tile(KV_LORA), tile(512), tile(LANES),
                                    full(w_in_r.shape), tile(D_MODEL), tile(D_MODEL), full(vec), full(vec)],
        out_specs=[tile(IN_COLS_R), tile(D_MODEL)] + [full(vec)] * 3,
        compiler_params=pltpu.CompilerParams(dimension_semantics=("arbitrary",), vmem_limit_bytes=VMEM_BIG),
    )(dq_sb, dk_sb, dv_sb, dg_sb, dcq, dckv, dg_mla, dkr, w_in_r, x, dy, norm_w, scale)


def _grad_w_in(hb, dproj):
    s_len = hb.shape[0]
    n_half = IN_COLS_R // 2

    def body(h_ref, d_ref, g_ref):
        @pl.when(pl.program_id(1) == 0)
        def _():
            g_ref[...] = jnp.zeros_like(g_ref)

        g_ref[...] += _dot_tn(h_ref[...], d_ref[...])

    return pl.pallas_call(
        body, name="grad_w_in", grid=(2, s_len // TN_S),
        out_shape=jax.ShapeDtypeStruct((D_MODEL, IN_COLS_R), F32),
        in_specs=[pl.BlockSpec((TN_S, D_MODEL), lambda n, s: (s, 0)),
                  pl.BlockSpec((TN_S, n_half), lambda n, s: (s, n))],
        out_specs=pl.BlockSpec((D_MODEL, n_half), lambda n, s: (0, n)),
        compiler_params=pltpu.CompilerParams(dimension_semantics=("parallel", "arbitrary"),
                                             vmem_limit_bytes=VMEM_BIG),
    )(hb, dproj)


def _final_exchange(gpack, ccol, wpack, mpack, vpack, w_ada, m_ada, v_ada, grads):
    n_sh = w_ada.shape[1]
    n = len(grads)

    def body(*refs):
        (g_ref, cc_ref, wp_ref, mp_ref, vp_ref, wa_ref, ma_ref, va_ref) = refs[:8]
        slabs_in = refs[8:8 + n]
        (og_ref, od_ref, om_ref, ov_ref, ag_ref, ad_ref, am_ref, av_ref) = refs[8 + n:16 + n]
        slabs_out = refs[16 + n:16 + 2 * n]
        gall_ref, ssem, rsem, slab_ssem, slab_rsem, lsem = refs[16 + 2 * n:]
        pos = _mesh_pos()
        me = _lin(pos)
        gall_ref[me] = g_ref[...]
        small = _all_gather_start(pos, g_ref, gall_ref, ssem, rsem, 0)
        own = [pltpu.make_async_copy(slabs_in[a].at[me], slabs_out[a].at[me], lsem.at[a]) for a in range(n)]
        for cp in own:
            cp.start()
        in_flight = [_all_to_all_start(pos, slabs_in[a], slabs_out[a], slab_ssem, slab_rsem, a * (N_DEV - 1))
                     for a in range(n)]
        _all_gather_wait(pos, g_ref, gall_ref, ssem, rsem, 0, small)

        tot = gall_ref[0]
        for j in range(1, N_DEV):
            tot = tot + gall_ref[j]
        og_ref[...] = tot
        od_ref[...], om_ref[...], ov_ref[...] = _adamw(wp_ref[...], tot, mp_ref[...], vp_ref[...])

        ga = jnp.zeros((D_MODEL, n_sh), F32)
        for j in range(N_DEV):
            d_mine = jnp.zeros((8, n_sh), F32)
            for k in range(N_DEV):
                d_mine = d_mine + jnp.where(me == k, gall_ref[j, :, PK_ADA + n_sh * k:PK_ADA + n_sh * (k + 1)], 0.0)
            col = _silu(cc_ref[j])
            ga = ga + jnp.concatenate(
                [col * d_mine[0:1, LANES * a:LANES * (a + 1)] for a in range(n_sh // LANES)], axis=1)
        ag_ref[...] = ga
        ad_ref[...], am_ref[...], av_ref[...] = _adamw(wa_ref[...], ga, ma_ref[...], va_ref[...])

        for a in range(n):
            _all_to_all_wait(pos, slabs_in[a], slabs_out[a], slab_ssem, slab_rsem, a * (N_DEV - 1), in_flight[a])
        for cp in own:
            cp.wait()

    pk = jax.ShapeDtypeStruct((8, PK_END), F32)
    ada = jax.ShapeDtypeStruct((D_MODEL, n_sh), F32)
    return pl.pallas_call(
        body, name="final_exchange",
        out_shape=[pk] * 4 + [ada] * 4 + [jax.ShapeDtypeStruct(g.shape, g.dtype) for g in grads],
        in_specs=[_vmem_spec()] * 8 + [_any_spec()] * n,
        out_specs=[_vmem_spec()] * 8 + [_any_spec()] * n,
        scratch_shapes=[
            pltpu.VMEM((N_DEV, 8, PK_END), F32),
            pltpu.SemaphoreType.DMA((N_DEV - 1,)),
            pltpu.SemaphoreType.DMA((N_DEV - 1,)),
            pltpu.SemaphoreType.DMA((n * (N_DEV - 1),)),
            pltpu.SemaphoreType.DMA((n * (N_DEV - 1),)),
            pltpu.SemaphoreType.DMA((n,)),
        ],
        compiler_params=pltpu.CompilerParams(vmem_limit_bytes=VMEM_BIG),
    )(gpack, ccol, wpack, mpack, vpack, w_ada, m_ada, v_ada, *grads)


def _adamw_reduce(name, parts, w, m, v, row_tile):
    rows, cols = w.shape

    def body(p_ref, w_ref, m_ref, v_ref, g_ref, d_ref, mo_ref, vo_ref):
        g = p_ref[0].astype(F32)
        for j in range(1, N_DEV):
            g = g + p_ref[j].astype(F32)
        g_ref[...] = g
        d_ref[...], mo_ref[...], vo_ref[...] = _adamw(w_ref[...], g, m_ref[...], v_ref[...])

    tile = pl.BlockSpec((row_tile, cols), lambda i: (i, 0))
    return pl.pallas_call(
        body, name=name, grid=(rows // row_tile,),
        out_shape=[jax.ShapeDtypeStruct((rows, cols), F32)] * 4,
        in_specs=[pl.BlockSpec((N_DEV, row_tile, cols), lambda i: (0, i, 0)), tile, tile, tile],
        out_specs=[tile] * 4,
        compiler_params=pltpu.CompilerParams(dimension_semantics=("parallel",), vmem_limit_bytes=VMEM_BIG),
    )(parts, w, m, v)


def _rope_tables(positions):
    inv_freq = 10000.0 ** (-jnp.arange(0, ROPE, 2, dtype=F32) / ROPE)
    ang = positions.astype(F32)[:, None] * inv_freq
    cos, sin = jnp.cos(ang), jnp.sin(ang)
    s_len = positions.shape[0]
    ones = jnp.ones((s_len, NOPE), F32)
    zeros = jnp.zeros((s_len, NOPE), F32)
    z16 = jnp.zeros((s_len, ROPE // 2), F32)
    pad1 = jnp.ones((s_len, LANES - MLA_QK), F32)
    pad0 = jnp.zeros((s_len, LANES - MLA_QK), F32)
    cosf = jnp.concatenate([ones, cos, cos, pad1], axis=1)
    sin_a = jnp.concatenate([zeros, -sin, z16, pad0], axis=1)
    sin_b = jnp.concatenate([zeros, z16, sin, pad0], axis=1)
    return cosf, sin_a, sin_b


def _rearrange_cols(w):
    pad = jnp.zeros((w.shape[0], IN_COLS_R - IN_COLS), w.dtype)
    return jnp.concatenate([w[:, :2688], w[:, 2720:3232], w[:, 2688:2720], pad], axis=1)


def _restore_cols(g):
    return jnp.concatenate([g[:, :2688], g[:, 3200:3232], g[:, 2688:3200]], axis=1)


def _pad_heads(w):
    rows = w.shape[0]
    w = w.reshape(rows, HEADS, MLA_QK)
    return jnp.pad(w, ((0, 0), (0, 0), (0, LANES - MLA_QK))).reshape(rows, HEADS * LANES)


def _unpad_heads(g):
    rows = g.shape[0]
    return g.reshape(rows, HEADS, LANES)[:, :, :MLA_QK].reshape(rows, HEADS * MLA_QK)


def _pad_lanes(v):
    return jnp.pad(v, ((0, 0), (0, LANES - v.shape[1])))


def _col_shards(g):
    rows = g.shape[0]
    return g.reshape(rows, N_DEV, g.shape[1] // N_DEV).transpose(1, 0, 2)


def _from_col_shards(g):
    return g.transpose(1, 0, 2).reshape(g.shape[1], N_DEV * g.shape[2])


def _pack(norm_w, qln, kvln, qhn, khn, ada, loss_lanes=None):
    if loss_lanes is None:
        loss_lanes = jnp.zeros((1, PK_END - PK_LOSS), F32)
    row = jnp.concatenate([norm_w, qln, kvln, _pad_lanes(qhn), _pad_lanes(khn), ada, loss_lanes], axis=1)
    return jnp.broadcast_to(row, (8, PK_END))


def _unpack(p):
    row = p[0:1]
    return (row[:, PK_NORM:PK_QLN], row[:, PK_QLN:PK_KVLN], row[:, PK_KVLN:PK_QHN],
            row[:, PK_QHN:PK_QHN + MLA_QK], row[:, PK_KHN:PK_KHN + MLA_QK], row[:, PK_ADA:PK_LOSS])


def kernel(x, c, positions, w_ada, b_ada, norm_w, w_in, q_lora_norm, w_uq, kv_lora_norm, w_ukv, q_head_norm, k_head_norm, w_out, loss_target, m_w_ada, m_b_ada, m_norm_w, m_w_in, m_q_lora_norm, m_w_uq, m_kv_lora_norm, m_w_ukv, m_q_head_norm, m_k_head_norm, m_w_out, v_w_ada, v_b_ada, v_norm_w, v_w_in, v_q_lora_norm, v_w_uq, v_kv_lora_norm, v_w_ukv, v_q_head_norm, v_k_head_norm, v_w_out):
    s_len = x.shape[1]
    x2 = x.reshape(s_len, D_MODEL)
    tgt = loss_target.reshape(s_len, D_MODEL)
    w_ada_s, w_in_s, w_uq_s, w_ukv_s, w_out_s = w_ada[0], w_in[0], w_uq[0], w_ukv[0], w_out[0]

    ada8, c_all = _ada_fwd(jnp.broadcast_to(c, (8, D_MODEL)), w_ada_s, b_ada.reshape(N_DEV, -1))
    ada = ada8.reshape(1, 3 * D_MODEL)
    shift, scale, gate = ada[:, :D_MODEL], ada[:, D_MODEL:2 * D_MODEL], ada[:, 2 * D_MODEL:]

    g_uq, g_ukv, g_out, g_in = _gather_weights([w_uq_s, w_ukv_s, w_out_s, w_in_s])
    w_in_r = _rearrange_cols(_from_col_shards(g_in))
    wuq_p = _pad_heads(_from_col_shards(g_uq))
    wukv_f = _from_col_shards(g_ukv)
    w_out_f = g_out.reshape(D_MODEL, D_MODEL)

    cosf, sin_a, sin_b = _rope_tables(positions[0])
    qhn_p, khn_p = _pad_lanes(q_head_norm), _pad_lanes(k_head_norm)

    hb, qkv, g_sb, cq, ckv, g_mla, kr, qm, km, vm, q_sb_t, qm_t = _fwd_pre(
        x2, shift, scale, norm_w, w_in_r, q_lora_norm, wuq_p, kv_lora_norm, wukv_f, qhn_p, khn_p,
        cosf, sin_a, sin_b)
    o_sb, w_saved, l_saved = _sb_fwd(qkv)
    o_mla, lse, p_saved = _mla_fwd(qm, km, vm)

    dy, do_sb, dg_sb, do_mla, dg_mla, gw_out, d_gate, loss_acc, do_sb_t, do_mla_t = _mid(
        o_sb, g_sb, o_mla, g_mla, x2, tgt, gate, w_out_f)

    dq_sb, dk_sb_t, dv_sb_t = _sb_bwd(qkv, q_sb_t, do_sb, do_sb_t, w_saved, l_saved)
    dk_sb, dv_sb = _from_key_tiles(dk_sb_t), _from_key_tiles(dv_sb_t)
    dq_m, dk_m_t, dv_m_t = _mla_bwd(qm_t, km, vm, o_mla, do_mla, do_mla_t, lse, p_saved)
    dk_m, dv_m = _from_key_tiles(dk_m_t), _from_key_tiles(dv_m_t)
    dcq, dckv, dkr, gw_uq_p, gw_ukv, g_qhn, g_khn, g_qln, g_kvln = _mla_pre_bwd(
        dq_m, dk_m, dv_m, cq, ckv, kr, q_lora_norm, wuq_p, kv_lora_norm, wukv_f, qhn_p, khn_p,
        cosf, sin_a, sin_b)
    dproj, grad_x, d_shift, d_scale, g_norm_w = _dproj_bwd(
        dq_sb, dk_sb, dv_sb, dg_sb, dcq, dckv, dg_mla, dkr, w_in_r, x2, dy, norm_w, scale)
    gw_in = _restore_cols(_grad_w_in(hb, dproj))

    d_ada = jnp.concatenate([d_shift, d_scale, d_gate], axis=1)
    gpack = _pack(g_norm_w, g_qln, g_kvln, g_qhn[:, :MLA_QK], g_khn[:, :MLA_QK], d_ada, loss_acc)
    wpack = _pack(norm_w, q_lora_norm, kv_lora_norm, q_head_norm, k_head_norm, b_ada)
    mpack = _pack(m_norm_w, m_q_lora_norm, m_kv_lora_norm, m_q_head_norm, m_k_head_norm, m_b_ada)
    vpack = _pack(v_norm_w, v_q_lora_norm, v_kv_lora_norm, v_q_head_norm, v_k_head_norm, v_b_ada)
    ccol = jnp.broadcast_to(c_all[:, :, None], (N_DEV, D_MODEL, LANES))
    slabs = [g.astype(BF16) for g in (
        _col_shards(gw_in), _col_shards(_unpad_heads(gw_uq_p)), _col_shards(gw_ukv),
        gw_out.reshape(N_DEV, D_MODEL // N_DEV, D_MODEL))]
    pg, pd, pm, pv, ada_g, ada_d, ada_m, ada_v, r_in, r_uq, r_ukv, r_out = _final_exchange(
        gpack, ccol, wpack, mpack, vpack, w_ada_s, m_w_ada[0], v_w_ada[0], slabs)
    loss = 0.5 * jnp.sum(pg[0, PK_LOSS:PK_END]) / D_MODEL

    in_g, in_d, in_m, in_v = _adamw_reduce("adamw_w_in", r_in, w_in_s, m_w_in[0], v_w_in[0], 256)
    uq_g, uq_d, uq_m, uq_v = _adamw_reduce("adamw_w_uq", r_uq, w_uq_s, m_w_uq[0], v_w_uq[0], w_uq_s.shape[0])
    ukv_g, ukv_d, ukv_m, ukv_v = _adamw_reduce(
        "adamw_w_ukv", r_ukv, w_ukv_s, m_w_ukv[0], v_w_ukv[0], w_ukv_s.shape[0])
    out_g, out_d, out_m, out_v = _adamw_reduce(
        "adamw_w_out", r_out, w_out_s, m_w_out[0], v_w_out[0], w_out_s.shape[0])

    def group(ada_t, pk, in_t, uq_t, ukv_t, out_t):
        nw, qln, kvln, qhn, khn, b = _unpack(pk)
        return (ada_t[None], b, nw, in_t[None], qln, uq_t[None], kvln, ukv_t[None], qhn, khn, out_t[None])

    return (loss, grad_x.reshape(1, s_len, D_MODEL),
            *group(ada_g, pg, in_g, uq_g, ukv_g, out_g),
            *group(ada_d, pd, in_d, uq_d, ukv_d, out_d),
            *group(ada_m, pm, in_m, uq_m, ukv_m, out_m),
            *group(ada_v, pv, in_v, uq_v, ukv_v, out_v))
```

```python
import functools
import math

import jax
import jax.numpy as jnp
from jax import lax
from jax.experimental import pallas as pl
from jax.experimental.pallas import tpu as pltpu

F32 = jnp.float32
BF16 = jnp.bfloat16

N_DEV = 8
D_MODEL = 1024
HEADS = 8
SB_WIDTH = 512
MLA_WIDTH = 512
Q_LORA = 384
KV_LORA = 256
ROPE = 32
NOPE = 64
MLA_QK = 96
LANES = 128
IN_COLS = 3232
IN_COLS_R = 3328
EPS = 1e-6
NEG = -1e30

ADAM_LR = 0.001
ADAM_B1 = 0.9
ADAM_B2 = 0.999
ADAM_EPS = 1e-08
ADAM_WD = 0.01
ADAM_STEP = 10

TS = 512
TS_FWD = 256
TQ = 512
KEY_UNROLL = 2
TN_S = 512
VMEM_BIG = 56 * 1024 * 1024

PK_NORM, PK_QLN, PK_KVLN, PK_QHN, PK_KHN, PK_ADA, PK_LOSS, PK_END = 0, 1024, 1408, 1664, 1792, 1920, 4992, 6016

MESH_ID = pl.DeviceIdType.MESH


def _dot_nn(a, b):
    return lax.dot_general(a, b, (((1,), (0,)), ((), ())), preferred_element_type=F32)


def _dot_nt(a, b):
    return lax.dot_general(a, b, (((1,), (1,)), ((), ())), preferred_element_type=F32)


def _dot_tn(a, b):
    return lax.dot_general(a, b, (((0,), (0,)), ((), ())), preferred_element_type=F32)


def _split_bf16(a):
    hi = a.astype(BF16)
    lo = (a - hi.astype(F32)).astype(BF16)
    return hi, lo


def _dot3(a, b):
    ah, al = _split_bf16(a)
    bh, bl = _split_bf16(b)
    return _dot_nn(ah, bh) + _dot_nn(ah, bl) + _dot_nn(al, bh)


def _sigmoid(g):
    return 1.0 / (1.0 + jnp.exp(-g))


def _silu(g):
    return g * _sigmoid(g)


def _lane_iota(shape):
    return lax.broadcasted_iota(jnp.int32, shape, len(shape) - 1)


def _adamw(w, g, m, v):
    m = ADAM_B1 * m + (1.0 - ADAM_B1) * g
    v = ADAM_B2 * v + (1.0 - ADAM_B2) * (g * g)
    m_hat = m / (1.0 - ADAM_B1 ** ADAM_STEP)
    v_hat = v / (1.0 - ADAM_B2 ** ADAM_STEP)
    delta = -ADAM_LR * (m_hat / (jnp.sqrt(v_hat) + ADAM_EPS) + ADAM_WD * w)
    return delta, m, v


def _mesh_pos():
    return lax.axis_index("x"), lax.axis_index("y"), lax.axis_index("c")


def _peer(pos, k):
    x, y, c = pos
    return (1 - x if k & 4 else x, 1 - y if k & 2 else y, 1 - c if k & 1 else c)


def _lin(pos):
    return 4 * pos[0] + 2 * pos[1] + pos[2]


def _remote(src, dst, send_sems, recv_sems, idx, peer):
    return pltpu.make_async_remote_copy(
        src_ref=src, dst_ref=dst, send_sem=send_sems.at[idx], recv_sem=recv_sems.at[idx],
        device_id=peer, device_id_type=MESH_ID)


def _all_gather_start(pos, src, buf, send_sems, recv_sems, base):
    me = _lin(pos)
    sent = []
    for k in range(1, N_DEV):
        cp = _remote(src, buf.at[me], send_sems, recv_sems, base + k - 1, _peer(pos, k))
        cp.start()
        sent.append(cp)
    return sent


def _all_gather_wait(pos, src, buf, send_sems, recv_sems, base, sent):
    for k in range(1, N_DEV):
        peer = _peer(pos, k)
        _remote(src, buf.at[_lin(peer)], send_sems, recv_sems, base + k - 1, peer).wait_recv()
    for cp in sent:
        cp.wait_send()


def _all_gather(pos, src, buf, send_sems, recv_sems, base):
    sent = _all_gather_start(pos, src, buf, send_sems, recv_sems, base)
    _all_gather_wait(pos, src, buf, send_sems, recv_sems, base, sent)


def _all_to_all_start(pos, src, buf, send_sems, recv_sems, base):
    me = _lin(pos)
    sent = []
    for k in range(1, N_DEV):
        peer = _peer(pos, k)
        cp = _remote(src.at[_lin(peer)], buf.at[me], send_sems, recv_sems, base + k - 1, peer)
        cp.start()
        sent.append(cp)
    return sent


def _all_to_all_wait(pos, src, buf, send_sems, recv_sems, base, sent):
    me = _lin(pos)
    for k in range(1, N_DEV):
        peer = _peer(pos, k)
        _remote(src.at[me], buf.at[_lin(peer)], send_sems, recv_sems, base + k - 1, peer).wait_recv()
    for cp in sent:
        cp.wait_send()


def _all_to_all(pos, src, buf, send_sems, recv_sems, base):
    sent = _all_to_all_start(pos, src, buf, send_sems, recv_sems, base)
    _all_to_all_wait(pos, src, buf, send_sems, recv_sems, base, sent)


def _two_level_gather(pos, bufs, send_sems, recv_sems):
    x, y, c = pos
    me, sibling = (x, y, c), (x, y, 1 - c)
    chips = [(1 - x, y), (x, 1 - y), (1 - x, 1 - y)]

    def copy(a, k, block, to):
        slot = bufs[a].at[_lin(block)]
        return _remote(slot, slot, send_sems, recv_sems, 7 * a + k, to)

    started = []
    for a in range(len(bufs)):
        first = [copy(a, 0, me, sibling)] + [copy(a, 1 + j, me, (*chip, c)) for j, chip in enumerate(chips)]
        for cp in first:
            cp.start()
        started += first
    for a in range(len(bufs)):
        for j, chip in enumerate(chips):
            copy(a, 1 + j, (*chip, c), me).wait_recv()
            passed = copy(a, 4 + j, (*chip, c), sibling)
            passed.start()
            started.append(passed)
    for a in range(len(bufs)):
        copy(a, 0, sibling, me).wait_recv()
        for j, chip in enumerate(chips):
            copy(a, 4 + j, (*chip, 1 - c), me).wait_recv()
    for cp in started:
        cp.wait_send()


def _vmem_spec():
    return pl.BlockSpec(memory_space=pltpu.VMEM)


def _any_spec():
    return pl.BlockSpec(memory_space=pl.ANY)


def _row_select(slots, n):
    r = lax.broadcasted_iota(jnp.int32, (N_DEV, n), 0)
    out = jnp.zeros((N_DEV, n), F32)
    for j in range(N_DEV):
        out = out + jnp.where(r == j, slots[j], 0.0)
    return out


def _ada_fwd(c8, w_ada, b_ada8):
    n_sh = w_ada.shape[1]

    def body(c_ref, w_ref, b_ref, out_ref, call_out_ref, call_ref, psend_ref, precv_ref, ssem, rsem):
        pos = _mesh_pos()
        me = _lin(pos)
        call_ref[me] = c_ref[...]
        _all_gather(pos, c_ref, call_ref, ssem, rsem, 0)
        call_out_ref[...] = _row_select([call_ref[j] for j in range(N_DEV)], D_MODEL)
        w = w_ref[...]
        for j in range(N_DEV):
            psend_ref[j] = _dot3(_silu(call_ref[j]), w)
        precv_ref[me] = psend_ref[me]
        _all_to_all(pos, psend_ref, precv_ref, ssem, rsem, N_DEV - 1)
        out_ref[...] = _row_select([precv_ref[j] for j in range(N_DEV)], n_sh) + b_ref[...]

    return pl.pallas_call(
        body, name="ada_fwd",
        out_shape=[jax.ShapeDtypeStruct((N_DEV, n_sh), F32), jax.ShapeDtypeStruct((N_DEV, D_MODEL), F32)],
        in_specs=[_vmem_spec()] * 3, out_specs=[_vmem_spec()] * 2,
        scratch_shapes=[
            pltpu.VMEM((N_DEV, 8, D_MODEL), F32),
            pltpu.VMEM((N_DEV, 8, n_sh), F32),
            pltpu.VMEM((N_DEV, 8, n_sh), F32),
            pltpu.SemaphoreType.DMA((2 * (N_DEV - 1),)),
            pltpu.SemaphoreType.DMA((2 * (N_DEV - 1),)),
        ],
    )(c8, w_ada, b_ada8)


def _gather_weights(shards):
    n = len(shards)

    def body(*refs):
        ins, outs = refs[:n], refs[n:2 * n]
        ssem, rsem = refs[2 * n], refs[2 * n + 1]
        pos = _mesh_pos()
        me = _lin(pos)
        for a in range(n):
            outs[a][me] = ins[a][...].astype(BF16)
        _two_level_gather(pos, outs, ssem, rsem)

    return pl.pallas_call(
        body, name="gather_weights",
        out_shape=[jax.ShapeDtypeStruct((N_DEV,) + s.shape, BF16) for s in shards],
        in_specs=[_vmem_spec()] * n, out_specs=[_vmem_spec()] * n,
        scratch_shapes=[
            pltpu.SemaphoreType.DMA((n * (N_DEV - 1),)),
            pltpu.SemaphoreType.DMA((n * (N_DEV - 1),)),
        ],
        compiler_params=pltpu.CompilerParams(vmem_limit_bytes=VMEM_BIG),
    )(*shards)


def _rope(t, cosf, sin_a, sin_b):
    return t * cosf + pltpu.roll(t, 112, 1) * sin_a + pltpu.roll(t, 16, 1) * sin_b


def _rope_t(d, cosf, sin_a, sin_b):
    return d * cosf + pltpu.roll(d * sin_a, 16, 1) + pltpu.roll(d * sin_b, 112, 1)


def _head_rms(t):
    return lax.rsqrt(jnp.sum(t * t, axis=1, keepdims=True) * (1.0 / MLA_QK) + EPS)


def _rms_bwd(dxhat_w, xhat, r, n):
    return r * (dxhat_w - xhat * (jnp.sum(dxhat_w * xhat, axis=1, keepdims=True) * (1.0 / n)))


def _mla_latents(cq, ckv, kr, qln, kvln, wuq, wukv):
    rq = lax.rsqrt(jnp.mean(cq * cq, axis=1, keepdims=True) + EPS)
    rkv = lax.rsqrt(jnp.mean(ckv * ckv, axis=1, keepdims=True) + EPS)
    cq_hat = cq * rq
    ckv_hat = ckv * rkv
    cqn = (cq_hat * qln).astype(BF16)
    ckvn = (ckv_hat * kvln).astype(BF16)
    q_all = _dot_nn(cqn, wuq)
    kv = _dot_nn(ckvn, wukv)
    kr64 = pltpu.roll(kr, 64, 1)
    return rq, rkv, cq_hat, ckv_hat, cqn, ckvn, q_all, kv, kr64


def _fwd_pre(x, shift, scale, norm_w, w_in_r, qln, wuq, kvln, wukv, qhn, khn, cosf, sin_a, sin_b):
    s_len = x.shape[0]

    def body(x_ref, shift_ref, scale_ref, nw_ref, w_ref, qln_ref, wuq_ref, kvln_ref, wukv_ref,
             qhn_ref, khn_ref, cos_ref, sa_ref, sb_ref,
             hb_ref, qkv_ref, gsb_ref, cq_ref, ckv_ref, gmla_ref, kr_ref, qm_ref, km_ref, vm_ref,
             qsbt_ref, qmt_ref):
        xv = x_ref[...]
        r = lax.rsqrt(jnp.mean(xv * xv, axis=1, keepdims=True) + EPS)
        h = (xv * r) * nw_ref[...] * (1.0 + scale_ref[...]) + shift_ref[...]
        hb = h.astype(BF16)
        hb_ref[...] = hb
        qkv = _dot_nn(hb, w_ref[:, 0:1536])
        qkv_ref[...] = qkv.astype(BF16)
        qsbt_ref[...] = qkv[:, :SB_WIDTH].T.astype(BF16)
        gsb_ref[...] = _dot_nn(hb, w_ref[:, 1536:2048])
        cq = _dot_nn(hb, w_ref[:, 2048:2432])
        ckv = _dot_nn(hb, w_ref[:, 2432:2688])
        gmla_ref[...] = _dot_nn(hb, w_ref[:, 2688:3200])
        kr = _dot_nn(hb, w_ref[:, 3200:3328])
        cq_ref[...] = cq
        ckv_ref[...] = ckv
        kr_ref[...] = kr
        _, _, _, _, _, _, q_all, kv, kr64 = _mla_latents(
            cq, ckv, kr, qln_ref[...], kvln_ref[...], wuq_ref[...], wukv_ref[...])
        cosf, sa, sb = cos_ref[...], sa_ref[...], sb_ref[...]
        qhn_v, khn_v = qhn_ref[...], khn_ref[...]
        low = _lane_iota((TS_FWD, LANES)) < NOPE
        blks = [slice(LANES * hd, LANES * (hd + 1)) for hd in range(HEADS)]
        q_raw = [q_all[:, b] for b in blks]
        k_raw = [jnp.where(low, kv[:, b], kr64) for b in blks]
        q_rms = [_head_rms(t) for t in q_raw]
        k_rms = [_head_rms(t) for t in k_raw]
        q_n = [t * r * (qhn_v * MLA_SCALE) for t, r in zip(q_raw, q_rms)]
        q_roped = [_rope(t, cosf, sa, sb) for t in q_n]
        kr_roped = _rope(kr64 * khn_v, cosf, sa, sb)
        k_roped = [jnp.where(low, t * khn_v, kr_roped) * r for t, r in zip(k_raw, k_rms)]
        for hd, b in enumerate(blks):
            qm_ref[:, b] = q_roped[hd].astype(BF16)
            qmt_ref[b, :] = q_roped[hd].T.astype(BF16)
            km_ref[:, b] = k_roped[hd].astype(BF16)
        for p in range(HEADS // 2):
            even = kv[:, LANES * 2 * p:LANES * (2 * p + 1)]
            odd = kv[:, LANES * (2 * p + 1):LANES * (2 * p + 2)]
            vm_ref[:, LANES * p:LANES * (p + 1)] = jnp.where(low, pltpu.roll(even, 64, 1), odd).astype(BF16)

    def tile(width):
        return pl.BlockSpec((TS_FWD, width), lambda i: (i, 0))

    def full(a):
        return pl.BlockSpec(a.shape, lambda i: (0, 0))

    out_widths = [(D_MODEL, BF16), (1536, BF16), (512, F32), (Q_LORA, F32), (KV_LORA, F32),
                  (512, F32), (LANES, F32), (1024, BF16), (1024, BF16), (512, BF16)]
    t_heights = [SB_WIDTH, HEADS * LANES]
    return pl.pallas_call(
        body, name="fwd_pre", grid=(s_len // TS_FWD,),
        out_shape=[jax.ShapeDtypeStruct((s_len, w), dt) for w, dt in out_widths]
        + [jax.ShapeDtypeStruct((hgt, s_len), BF16) for hgt in t_heights],
        in_specs=[tile(D_MODEL), full(shift), full(scale), full(norm_w), full(w_in_r), full(qln), full(wuq),
                  full(kvln), full(wukv), full(qhn), full(khn), tile(LANES), tile(LANES), tile(LANES)],
        out_specs=[tile(w) for w, _ in out_widths]
        + [pl.BlockSpec((hgt, TS_FWD), lambda i: (0, i)) for hgt in t_heights],
        compiler_params=pltpu.CompilerParams(dimension_semantics=("parallel",), vmem_limit_bytes=VMEM_BIG),
    )(x, shift, scale, norm_w, w_in_r, qln, wuq, kvln, wukv, qhn, khn, cosf, sin_a, sin_b)


CUM_W = 256


def _tri(later):
    j = lax.broadcasted_iota(jnp.int32, (CUM_W, CUM_W), 0)
    s = lax.broadcasted_iota(jnp.int32, (CUM_W, CUM_W), 1)
    return (j > s if later else j < s).astype(BF16)


def _suffix_sums(a, a_bf16, tri_m, carry):
    n = a.shape[1] // CUM_W
    outs = [None] * n
    for i in reversed(range(n)):
        cols = slice(CUM_W * i, CUM_W * (i + 1))
        outs[i] = _dot_nn(a_bf16[:, cols], tri_m) + carry
        carry = carry + _rowsum(a[:, cols])
    return (outs[0] if n == 1 else jnp.concatenate(outs, axis=1)), carry


def _sb_weights(qm, kb, carry, tri_u, diag):
    z = _dot_nt(qm, kb)
    nz = -z
    lk = jnp.minimum(nz, 0.0) - jnp.log(1.0 + jnp.exp(jnp.minimum(z, nz)))
    if diag:
        t = lax.broadcasted_iota(jnp.int32, (TQ, TQ), 0)
        s = lax.broadcasted_iota(jnp.int32, (TQ, TQ), 1)
        valid = s < t
        lk = jnp.where(valid, lk, 0.0)
    lk_hi = lk.astype(BF16)
    log_beta = z + lk
    after, carry = _suffix_sums(lk, lk_hi, tri_u, carry)
    logw = log_beta + after
    if diag:
        logw = jnp.where(valid, logw, NEG)
    return lk_hi, jnp.exp(logw), carry


SB_SCALE = 0.125


def _head_masks():
    lane = _lane_iota((1, LANES))
    return [lane < 64, lane >= 64]


def _masked(hm, a):
    return jnp.where(hm, a, jnp.zeros_like(a))


def _rowsum(a):
    return jnp.sum(a, axis=1, keepdims=True)


def _key_rows(kj):
    return pl.ds(pl.multiple_of(kj * TQ, TQ), TQ)


def _over_key_tiles(count, fn, st, ascending):
    n_full = count // KEY_UNROLL
    n_rest = count - n_full * KEY_UNROLL

    def group(g, s_):
        return fn([g * KEY_UNROLL + (u if ascending else KEY_UNROLL - 1 - u) for u in range(KEY_UNROLL)], s_)

    if ascending:
        st = lax.fori_loop(0, n_full, group, st)
        return lax.fori_loop(0, n_rest, lambda i, s_: fn([n_full * KEY_UNROLL + i], s_), st)
    st = lax.fori_loop(0, n_rest, lambda i, s_: fn([count - 1 - i], s_), st)
    return lax.fori_loop(0, n_full, lambda i, s_: group(n_full - 1 - i, s_), st)


STAGE_SLOTS = 2 * KEY_UNROLL


def _stage_copies(to_hbm, hbm_refs, scr_refs, sems, pair, qi, kj):
    slot = (qi - kj) % STAGE_SLOTS
    out = []
    for h in range(2):
        for a in range(2):
            hbm, scr = hbm_refs[a].at[2 * pair + h, qi, kj], scr_refs[a].at[slot, h]
            sem = sems.at[4 * slot + 2 * h + a]
            out.append(pltpu.make_async_copy(scr, hbm, sem) if to_hbm else pltpu.make_async_copy(hbm, scr, sem))
    return out


def _sb_fwd(qkv):
    s_len = qkv.shape[0]
    nq = s_len // TQ

    def body(q_ref, k_ref, v_ref, o_ref, w_hbm, l_hbm, w_scr, l_scr, sems):
        pair, qi = pl.program_id(0), pl.program_id(1)
        q = q_ref[...]
        tri_u = _tri(True)
        masks = _head_masks()
        qms = [_masked(hm, q) * SB_SCALE for hm in masks]

        def copies(kj, of_qi=qi):
            return _stage_copies(True, (w_hbm, l_hbm), (w_scr, l_scr), sems, pair, of_qi, kj)

        def drain(of_qi):
            for kj in range(STAGE_SLOTS):
                @pl.when(kj <= of_qi)
                def _():
                    for cp in copies(kj, of_qi):
                        cp.wait()

        def block(kj, st, diag, before_staging=None):
            rows = _key_rows(kj)
            slot = (qi - kj) % STAGE_SLOTS
            kb, vb = k_ref[rows, :], v_ref[rows, :]
            carries, acc = list(st[:2]), st[2]
            staged = []
            for h in range(2):
                lk_hi, w, carries[h] = _sb_weights(qms[h], kb, carries[h], tri_u, diag)
                wb = w.astype(BF16)
                staged.append((wb, lk_hi))
                acc = acc + _dot_nn(wb, _masked(masks[h], vb))
            if before_staging is not None:
                before_staging()
            for h in range(2):
                w_scr[slot, h], l_scr[slot, h] = staged[h]
            return carries[0], carries[1], acc

        def trip(tiles, st):
            for kj in tiles:
                @pl.when(qi - kj >= STAGE_SLOTS)
                def _():
                    for cp in copies(kj + STAGE_SLOTS):
                        cp.wait()
            for kj in tiles:
                st = block(kj, st, False)
            for kj in tiles:
                for cp in copies(kj):
                    cp.start()
            return st

        def drain_previous_step():
            @pl.when(jnp.logical_or(pair > 0, qi > 0))
            def _():
                drain(jnp.where(qi == 0, nq - 1, qi - 1))

        zc = jnp.zeros((TQ, 1), F32)
        st = block(qi, (zc, zc, jnp.zeros((TQ, LANES), F32)), True, drain_previous_step)
        for cp in copies(qi):
            cp.start()
        st = _over_key_tiles(qi, trip, st, ascending=False)
        o_ref[...] = st[2]

        @pl.when(jnp.logical_and(pair == HEADS // 2 - 1, qi == nq - 1))
        def _():
            drain(qi)

    saved = jax.ShapeDtypeStruct((HEADS, nq, nq, TQ, TQ), BF16)
    stage = pltpu.VMEM((STAGE_SLOTS, 2, TQ, TQ), BF16)
    return pl.pallas_call(
        body, name="sb_fwd", grid=(HEADS // 2, nq),
        out_shape=[jax.ShapeDtypeStruct((s_len, SB_WIDTH), F32), saved, saved],
        in_specs=[pl.BlockSpec((TQ, LANES), lambda p, i: (i, p)),
                  pl.BlockSpec((s_len, LANES), lambda p, i: (0, 4 + p)),
                  pl.BlockSpec((s_len, LANES), lambda p, i: (0, 8 + p))],
        out_specs=[pl.BlockSpec((TQ, LANES), lambda p, i: (i, p)), _any_spec(), _any_spec()],
        scratch_shapes=[stage, stage, pltpu.SemaphoreType.DMA((4 * STAGE_SLOTS,))],
        compiler_params=pltpu.CompilerParams(dimension_semantics=("arbitrary", "arbitrary"),
                                             vmem_limit_bytes=VMEM_BIG),
    )(qkv, qkv, qkv)


def _prefix_sums(a, tri_m, carry):
    n = a.shape[1] // CUM_W
    outs = [None] * n
    for i in range(n):
        cols = slice(CUM_W * i, CUM_W * (i + 1))
        outs[i] = _dot_nn(a[:, cols].astype(BF16), tri_m) + carry
        carry = carry + _rowsum(a[:, cols])
    return (outs[0] if n == 1 else jnp.concatenate(outs, axis=1)), carry


def _head_rows(a0, a1):
    sub = lax.broadcasted_iota(jnp.int32, a0.shape, 0)
    return jnp.where(sub < NOPE, a0, a1)


def _sb_bwd(qkv, q_t, do, do_t, w_saved, l_saved):
    s_len = qkv.shape[0]
    nq = s_len // TQ

    def body(qt_ref, k_ref, v_ref, do_ref, dot_ref, w_hbm, l_hbm, dq_ref, dk_ref, dv_ref, w_scr, l_scr, sems):
        pair, qi = pl.program_id(0), pl.program_id(1)

        def copies(kj, of_pair=pair, of_qi=qi):
            return _stage_copies(False, (w_hbm, l_hbm), (w_scr, l_scr), sems, of_pair, of_qi, kj)

        def start_first_tiles(of_pair, of_qi):
            for cp in copies(0, of_pair, of_qi):
                cp.start()

            @pl.when(of_qi >= 1)
            def _():
                for cp in copies(1, of_pair, of_qi):
                    cp.start()

        @pl.when(jnp.logical_and(pair == 0, qi == 0))
        def _():
            start_first_tiles(pair, qi)

        @pl.when(qi == 0)
        def _():
            dk_ref[...] = jnp.zeros_like(dk_ref)
            dv_ref[...] = jnp.zeros_like(dv_ref)

        qt = qt_ref[...] * SB_SCALE
        dot_v = dot_ref[...]
        do_v = do_ref[...]
        tri_before = _tri(False)
        masks = _head_masks()
        doms = [_masked(hm, do_v) for hm in masks]

        def block(kj, st):
            rows = _key_rows(kj)
            slot = (qi - kj) % STAGE_SLOTS
            kb, vb = k_ref[rows, :], v_ref[rows, :]
            carries, dqs = list(st[0:2]), list(st[2:4])
            dk_t, dv_t = [], []
            for h in range(2):
                wb = w_scr[slot, h]
                d_l = _dot_nt(doms[h], vb) * wb.astype(F32)
                before, carries[h] = _prefix_sums(d_l, tri_before, carries[h])
                keep = jnp.exp(l_scr[slot, h].astype(F32))
                dzb = (d_l * keep - before * (1.0 - keep)).astype(BF16)
                dk_t.append(_dot_nn(qt, dzb))
                dv_t.append(_dot_nn(dot_v, wb))
                dqs[h] = dqs[h] + _dot_nn(dzb, kb)
            dk_ref[kj] += _head_rows(*dk_t)
            dv_ref[kj] += _head_rows(*dv_t)
            return (*carries, *dqs)

        def trip(tiles, st):
            for kj in tiles:
                @pl.when(kj + 2 <= qi)
                def _():
                    for cp in copies(kj + 2):
                        cp.start()
            for kj in tiles:
                for cp in copies(kj):
                    cp.wait()
            for kj in tiles:
                st = block(kj, st)
            return st

        zc = jnp.zeros((TQ, 1), F32)
        za = jnp.zeros((TQ, LANES), F32)
        st = _over_key_tiles(qi + 1, trip, (zc, zc, za, za), ascending=True)
        dq_ref[...] = jnp.where(masks[0], st[2], st[3]) * SB_SCALE

        @pl.when(jnp.logical_or(pair < HEADS // 2 - 1, qi < nq - 1))
        def _():
            wraps = qi == nq - 1
            start_first_tiles(jnp.where(wraps, pair + 1, pair), jnp.where(wraps, 0, qi + 1))

    tile = pl.BlockSpec((TQ, LANES), lambda p, i: (i, p))
    tile_t = pl.BlockSpec((LANES, TQ), lambda p, i: (p, i))
    col_t = pl.BlockSpec((nq, LANES, TQ), lambda p, i: (0, p, 0))
    stage = pltpu.VMEM((STAGE_SLOTS, 2, TQ, TQ), BF16)
    key_t = jax.ShapeDtypeStruct((nq, SB_WIDTH, TQ), F32)
    return pl.pallas_call(
        body, name="sb_bwd", grid=(HEADS // 2, nq),
        out_shape=[jax.ShapeDtypeStruct((s_len, SB_WIDTH), F32), key_t, key_t],
        in_specs=[tile_t,
                  pl.BlockSpec((s_len, LANES), lambda p, i: (0, 4 + p)),
                  pl.BlockSpec((s_len, LANES), lambda p, i: (0, 8 + p)),
                  tile, tile_t, _any_spec(), _any_spec()],
        out_specs=[tile, col_t, col_t],
        scratch_shapes=[stage, stage, pltpu.SemaphoreType.DMA((4 * STAGE_SLOTS,))],
        compiler_params=pltpu.CompilerParams(dimension_semantics=("arbitrary", "arbitrary"),
                                             vmem_limit_bytes=VMEM_BIG),
    )(q_t, qkv, qkv, do, do_t, w_saved, l_saved)


def _from_key_tiles(a_t):
    return a_t.transpose(0, 2, 1).reshape(a_t.shape[0] * a_t.shape[2], a_t.shape[1])


MLA_SCALE = 1.0 / math.sqrt(MLA_QK)


def _causal_mask():
    t = lax.broadcasted_iota(jnp.int32, (TQ, TQ), 0)
    s = lax.broadcasted_iota(jnp.int32, (TQ, TQ), 1)
    return s <= t


def _head_lanes(h):
    return slice(LANES * h, LANES * (h + 1))


P_SLOTS = 2 * KEY_UNROLL
P_COLS = TQ + LANES


def _tile_number(pair, qi, kj, nq):
    return pair * (nq * (nq + 1) // 2) + (qi * (qi + 1)) // 2 + kj


def _p_copy(to_hbm, p_hbm, p_scr, sems, pair, qi, kj, nq, h):
    slot = _tile_number(pair, qi, kj, nq) % P_SLOTS
    hbm, scr, sem = p_hbm.at[2 * pair + h, qi, kj], p_scr.at[slot, h], sems.at[2 * slot + h]
    return pltpu.make_async_copy(scr, hbm, sem) if to_hbm else pltpu.make_async_copy(hbm, scr, sem)


def _mla_fwd(qm, km, vm):
    s_len = qm.shape[0]
    nq = s_len // TQ

    def body(q_ref, k_ref, v_ref, o_ref, lse_ref, p_hbm, p_scr, sems):
        pair, qi = pl.program_id(0), pl.program_id(1)
        masks = _head_masks()
        qhs = [q_ref[:, _head_lanes(h)] for h in range(2)]
        lane = _lane_iota((TQ, LANES))

        def number(kj):
            return _tile_number(pair, qi, kj, nq)

        def copies(kj):
            return [_p_copy(True, p_hbm, p_scr, sems, pair, qi, kj, nq, h) for h in range(2)]

        def two_parts(m):
            hi = m.astype(BF16).astype(F32)
            return jnp.where(lane < 64, hi, m - hi).astype(BF16)

        def block(kj, st, diag):
            rows = _key_rows(kj)
            slot = number(kj) % P_SLOTS
            vb = v_ref[rows, :]
            ms, ls, acc = list(st[0:2]), list(st[2:4]), st[4]
            alphas, pvs = [], []
            for h in range(2):
                s = _dot_nt(qhs[h], k_ref[rows, _head_lanes(h)])
                if diag:
                    s = jnp.where(_causal_mask(), s, NEG)
                m_new = jnp.maximum(ms[h], jnp.max(s, axis=1, keepdims=True))
                p = jnp.exp(s - m_new)
                pb = p.astype(BF16)
                p_scr[slot, h, :, 0:TQ] = pb
                p_scr[slot, h, :, TQ:P_COLS] = two_parts(m_new)
                alphas.append(jnp.exp(ms[h] - m_new))
                ls[h] = alphas[h] * ls[h] + _rowsum(p)
                ms[h] = m_new
                pvs.append(_dot_nn(pb, _masked(masks[h], vb)))
            acc = jnp.where(masks[0], alphas[0], alphas[1]) * acc + pvs[0] + pvs[1]
            return (*ms, *ls, acc)

        def trip(tiles, st, diag=False):
            for kj in tiles:
                @pl.when(number(kj) >= P_SLOTS)
                def _():
                    for cp in copies(kj):
                        cp.wait()
            for kj in tiles:
                st = block(kj, st, diag)
            for kj in tiles:
                for cp in copies(kj):
                    cp.start()
            return st

        neg = jnp.full((TQ, 1), NEG, F32)
        zc = jnp.zeros((TQ, 1), F32)
        st = (neg, neg, zc, zc, jnp.zeros((TQ, LANES), F32))
        st = _over_key_tiles(qi, trip, st, ascending=True)
        m0, m1, l0, l1, acc = trip([qi], st, True)
        o_ref[...] = acc / jnp.where(masks[0], l0, l1)
        lse_ref[0] = m0 + jnp.log(l0)
        lse_ref[1] = m1 + jnp.log(l1)

        @pl.when(jnp.logical_and(pair == HEADS // 2 - 1, qi == nq - 1))
        def _():
            for slot in range(P_SLOTS):
                for h in range(2):
                    pltpu.make_async_copy(p_scr.at[slot, h], p_hbm.at[0, 0, 0], sems.at[2 * slot + h]).wait()

    return pl.pallas_call(
        body, name="mla_fwd", grid=(HEADS // 2, nq),
        out_shape=[jax.ShapeDtypeStruct((s_len, MLA_WIDTH), F32),
                   jax.ShapeDtypeStruct((HEADS, s_len, 1), F32),
                   jax.ShapeDtypeStruct((HEADS, nq, nq, TQ, P_COLS), BF16)],
        in_specs=[pl.BlockSpec((TQ, 2 * LANES), lambda p, i: (i, p)),
                  pl.BlockSpec((s_len, 2 * LANES), lambda p, i: (0, p)),
                  pl.BlockSpec((s_len, LANES), lambda p, i: (0, p))],
        out_specs=[pl.BlockSpec((TQ, LANES), lambda p, i: (i, p)),
                   pl.BlockSpec((2, TQ, 1), lambda p, i: (p, i, 0)),
                   _any_spec()],
        scratch_shapes=[pltpu.VMEM((P_SLOTS, 2, TQ, P_COLS), BF16), pltpu.SemaphoreType.DMA((2 * P_SLOTS,))],
        compiler_params=pltpu.CompilerParams(dimension_semantics=("arbitrary", "arbitrary"),
                                             vmem_limit_bytes=VMEM_BIG),
    )(qm, km, vm)


def _mla_bwd(qm_t, km, vm, o, do, do_t, lse, p_saved):
    s_len = km.shape[0]
    nq = s_len // TQ
    total = (HEADS // 2) * (nq * (nq + 1) // 2)

    def body(qt_ref, k_ref, v_ref, o_ref, do_ref, dot_ref, lse_ref, p_hbm, dq_ref, dk_ref, dv_ref, p_scr, sems):
        pair, qi = pl.program_id(0), pl.program_id(1)

        def number(kj):
            return _tile_number(pair, qi, kj, nq)

        def fetch(p_, q_, k_):
            for h in range(2):
                _p_copy(False, p_hbm, p_scr, sems, p_, q_, k_, nq, h).start()

        def advance(p_, q_, k_):
            row_end = k_ == q_
            last_row = q_ == nq - 1
            return (jnp.where(jnp.logical_and(row_end, last_row), p_ + 1, p_),
                    jnp.where(row_end, jnp.where(last_row, 0, q_ + 1), q_),
                    jnp.where(row_end, 0, k_ + 1))

        @pl.when(jnp.logical_and(pair == 0, qi == 0))
        def _():
            fetch(pair, qi, 0)
            fetch(*advance(pair, qi, 0))

        @pl.when(qi == 0)
        def _():
            dk_ref[...] = jnp.zeros_like(dk_ref)
            dv_ref[...] = jnp.zeros_like(dv_ref)

        do_v = do_ref[...]
        dot_v = dot_ref[...]
        od = o_ref[...] * do_v.astype(F32)
        masks = _head_masks()
        qts = [qt_ref[_head_lanes(h), :] for h in range(2)]
        doms = [_masked(hm, do_v) for hm in masks]
        deltas = [_rowsum(jnp.where(hm, od, 0.0)) for hm in masks]
        lses = [lse_ref[h] for h in range(2)]

        def block(kj, dqs):
            rows = _key_rows(kj)
            slot = number(kj) % P_SLOTS
            vb = v_ref[rows, :]
            dqs = list(dqs)
            dv_t = []
            for h in range(2):
                kb = k_ref[rows, _head_lanes(h)]
                tile_max = _rowsum(p_scr[slot, h, :, TQ:P_COLS].astype(F32)) * (1.0 / 64.0)
                p = p_scr[slot, h, :, 0:TQ].astype(F32) * jnp.exp(tile_max - lses[h])
                dp = _dot_nt(doms[h], vb)
                ds = (p * (dp - deltas[h])).astype(BF16)
                dk_ref[kj, _head_lanes(h), :] += _dot_nn(qts[h], ds)
                dv_t.append(_dot_nn(dot_v, p.astype(BF16)))
                dqs[h] = dqs[h] + _dot_nn(ds, kb)
            dv_ref[kj] += _head_rows(*dv_t)
            return tuple(dqs)

        def trip(tiles, dqs):
            for kj in tiles:
                @pl.when(number(kj) + 2 < total)
                def _():
                    fetch(*advance(*advance(pair, qi, kj)))
            for kj in tiles:
                for h in range(2):
                    _p_copy(False, p_hbm, p_scr, sems, pair, qi, kj, nq, h).wait()
            for kj in tiles:
                dqs = block(kj, dqs)
            return dqs

        za = jnp.zeros((TQ, LANES), F32)
        dqs = _over_key_tiles(qi + 1, trip, (za, za), ascending=True)
        dq_ref[:, _head_lanes(0)] = dqs[0] * MLA_SCALE
        dq_ref[:, _head_lanes(1)] = dqs[1] * MLA_SCALE

    return pl.pallas_call(
        body, name="mla_bwd", grid=(HEADS // 2, nq),
        out_shape=[jax.ShapeDtypeStruct((s_len, HEADS * LANES), F32),
                   jax.ShapeDtypeStruct((nq, HEADS * LANES, TQ), F32),
                   jax.ShapeDtypeStruct((nq, MLA_WIDTH, TQ), F32)],
        in_specs=[pl.BlockSpec((2 * LANES, TQ), lambda p, i: (p, i)),
                  pl.BlockSpec((s_len, 2 * LANES), lambda p, i: (0, p)),
                  pl.BlockSpec((s_len, LANES), lambda p, i: (0, p)),
                  pl.BlockSpec((TQ, LANES), lambda p, i: (i, p)),
                  pl.BlockSpec((TQ, LANES), lambda p, i: (i, p)),
                  pl.BlockSpec((LANES, TQ), lambda p, i: (p, i)),
                  pl.BlockSpec((2, TQ, 1), lambda p, i: (p, i, 0)),
                  _any_spec()],
        out_specs=[pl.BlockSpec((TQ, 2 * LANES), lambda p, i: (i, p)),
                   pl.BlockSpec((nq, 2 * LANES, TQ), lambda p, i: (0, p, 0)),
                   pl.BlockSpec((nq, LANES, TQ), lambda p, i: (0, p, 0))],
        scratch_shapes=[pltpu.VMEM((P_SLOTS, 2, TQ, P_COLS), BF16), pltpu.SemaphoreType.DMA((2 * P_SLOTS,))],
        compiler_params=pltpu.CompilerParams(dimension_semantics=("arbitrary", "arbitrary"),
                                             vmem_limit_bytes=VMEM_BIG),
    )(qm_t, km, vm, o, do, do_t, lse, p_saved)


def _mid(o_sb, g_sb, o_mla, g_mla, x, target, gate, w_out):
    s_len = x.shape[0]

    def body(osb_ref, gsb_ref, omla_ref, gmla_ref, x_ref, t_ref, gate_ref, w_ref,
             dy_ref, dosb_ref, dgsb_ref, domla_ref, dgmla_ref, gw_ref, dgate_ref, loss_ref, dosbt_ref, domlat_ref):
        @pl.when(pl.program_id(0) == 0)
        def _():
            gw_ref[...] = jnp.zeros_like(gw_ref)
            dgate_ref[...] = jnp.zeros_like(dgate_ref)
            loss_ref[...] = jnp.zeros_like(loss_ref)

        o1, g1, o2, g2 = osb_ref[...], gsb_ref[...], omla_ref[...], gmla_ref[...]
        s1, s2 = _sigmoid(g1), _sigmoid(g2)
        mixed = jnp.concatenate([o1 * (g1 * s1), o2 * (g2 * s2)], axis=1).astype(BF16)
        w = w_ref[...]
        gate_v = gate_ref[...]
        u = _dot_nn(mixed, w)
        err = x_ref[...] + gate_v * u - t_ref[...]
        loss_ref[...] += jnp.sum(err * err, axis=0, keepdims=True)
        dy = err * (1.0 / D_MODEL)
        dy_ref[...] = dy
        dgate_ref[...] += jnp.sum(dy * u, axis=0, keepdims=True)
        du = (dy * gate_v).astype(BF16)
        gw_ref[...] += _dot_tn(mixed, du)
        dmixed = _dot_nt(du, w)
        d1, d2 = dmixed[:, :SB_WIDTH], dmixed[:, SB_WIDTH:]
        do1, do2 = d1 * (g1 * s1), d2 * (g2 * s2)
        dosb_ref[...] = do1.astype(BF16)
        dgsb_ref[...] = (d1 * o1 * (s1 * (1.0 + g1 * (1.0 - s1)))).astype(BF16)
        domla_ref[...] = do2.astype(BF16)
        dgmla_ref[...] = (d2 * o2 * (s2 * (1.0 + g2 * (1.0 - s2)))).astype(BF16)
        dosbt_ref[...] = do1.T.astype(BF16)
        domlat_ref[...] = do2.T.astype(BF16)

    def tile(width):
        return pl.BlockSpec((TS, width), lambda i: (i, 0))

    def full(shape):
        return pl.BlockSpec(shape, lambda i: (0, 0))

    return pl.pallas_call(
        body, name="mid", grid=(s_len // TS,),
        out_shape=[jax.ShapeDtypeStruct((s_len, D_MODEL), F32)]
        + [jax.ShapeDtypeStruct((s_len, 512), BF16)] * 4
        + [jax.ShapeDtypeStruct((D_MODEL, D_MODEL), F32),
           jax.ShapeDtypeStruct((1, D_MODEL), F32), jax.ShapeDtypeStruct((1, D_MODEL), F32)]
        + [jax.ShapeDtypeStruct((512, s_len), BF16)] * 2,
        in_specs=[tile(512)] * 4 + [tile(D_MODEL), tile(D_MODEL), full((1, D_MODEL)), full((D_MODEL, D_MODEL))],
        out_specs=[tile(D_MODEL)] + [tile(512)] * 4
        + [full((D_MODEL, D_MODEL)), full((1, D_MODEL)), full((1, D_MODEL))]
        + [pl.BlockSpec((512, TS), lambda i: (0, i))] * 2,
        compiler_params=pltpu.CompilerParams(dimension_semantics=("arbitrary",), vmem_limit_bytes=VMEM_BIG),
    )(o_sb, g_sb, o_mla, g_mla, x, target, gate, w_out)


def _mla_pre_bwd(dq, dk, dv, cq, ckv, kr, qln, wuq, kvln, wukv, qhn, khn, cosf, sin_a, sin_b):
    s_len = cq.shape[0]

    def body(dq_ref, dk_ref, dv_ref, cq_ref, ckv_ref, kr_ref, qln_ref, wuq_ref, kvln_ref, wukv_ref,
             qhn_ref, khn_ref, cos_ref, sa_ref, sb_ref,
             dcq_ref, dckv_ref, dkr_ref, gwuq_ref, gwukv_ref, gqhn_ref, gkhn_ref, gqln_ref, gkvln_ref,
             dqa_ref, dkv_ref):
        @pl.when(pl.program_id(0) == 0)
        def _():
            for r_ in (gwuq_ref, gwukv_ref, gqhn_ref, gkhn_ref, gqln_ref, gkvln_ref):
                r_[...] = jnp.zeros_like(r_)

        cq, ckv = cq_ref[...], ckv_ref[...]
        qln_v, kvln_v = qln_ref[...], kvln_ref[...]
        wuq_v, wukv_v = wuq_ref[...], wukv_ref[...]
        rq, rkv, cq_hat, ckv_hat, cqn, ckvn, q_all, kv, kr64 = _mla_latents(
            cq, ckv, kr_ref[...], qln_v, kvln_v, wuq_v, wukv_v)
        cosf, sa, sb = cos_ref[...], sa_ref[...], sb_ref[...]
        qhn_v, khn_v = qhn_ref[...], khn_ref[...]
        lane = _lane_iota((TS, LANES))
        low = lane < NOPE
        blks = [slice(LANES * hd, LANES * (hd + 1)) for hd in range(HEADS)]
        raw = [q_all[:, b] for b in blks] + [jnp.where(low, kv[:, b], kr64) for b in blks]
        grads = [dq_ref[:, b] for b in blks] + [dk_ref[:, b] for b in blks]
        gains = [qhn_v] * HEADS + [khn_v] * HEADS
        rms = [_head_rms(t) for t in raw]
        xhs = [t * r for t, r in zip(raw, rms)]
        dns = [_rope_t(d, cosf, sa, sb) for d in grads]
        gain_g = [jnp.sum(dn * xh, axis=0, keepdims=True) for dn, xh in zip(dns, xhs)]
        dxs = [_rms_bwd(dn * g, xh, r, MLA_QK) for dn, g, xh, r in zip(dns, gains, xhs, rms)]
        dkr64 = jnp.zeros((TS, LANES), F32)
        for hd, b in enumerate(blks):
            dqa_ref[:, b] = dxs[hd].astype(BF16)
            dkb = dxs[HEADS + hd]
            dkr64 = dkr64 + jnp.where(low, 0.0, dkb)
            dvp = dv_ref[:, LANES * (hd // 2):LANES * (hd // 2 + 1)]
            dvh = pltpu.roll(dvp, 64, 1) if hd % 2 == 0 else dvp
            dkv_ref[:, b] = jnp.where(low, dkb, dvh).astype(BF16)
        gqhn_ref[...] += sum(gain_g[:HEADS])
        gkhn_ref[...] += sum(gain_g[HEADS:])
        dkr_ref[...] = pltpu.roll(dkr64, 64, 1).astype(BF16)

        dqa = dqa_ref[...]
        gwuq_ref[...] += _dot_tn(cqn, dqa)
        dcqn = _dot_nt(dqa, wuq_v)
        gqln_ref[...] += jnp.sum(dcqn * cq_hat, axis=0, keepdims=True)
        dcq_ref[...] = _rms_bwd(dcqn * qln_v, cq_hat, rq, Q_LORA).astype(BF16)

        dkv = dkv_ref[...]
        gwukv_ref[...] += _dot_tn(ckvn, dkv)
        dckvn = _dot_nt(dkv, wukv_v)
        gkvln_ref[...] += jnp.sum(dckvn * ckv_hat, axis=0, keepdims=True)
        dckv_ref[...] = _rms_bwd(dckvn * kvln_v, ckv_hat, rkv, KV_LORA).astype(BF16)

    def tile(width):
        return pl.BlockSpec((TS, width), lambda i: (i, 0))

    def full(shape):
        return pl.BlockSpec(shape, lambda i: (0, 0))

    acc_shapes = [(Q_LORA, 1024), (KV_LORA, 1024), (1, LANES), (1, LANES), (1, Q_LORA), (1, KV_LORA)]
    return pl.pallas_call(
        body, name="mla_pre_bwd", grid=(s_len // TS,),
        out_shape=[jax.ShapeDtypeStruct((s_len, Q_LORA), BF16), jax.ShapeDtypeStruct((s_len, KV_LORA), BF16),
                   jax.ShapeDtypeStruct((s_len, LANES), BF16)]
        + [jax.ShapeDtypeStruct(s, F32) for s in acc_shapes],
        in_specs=[tile(1024), tile(1024), tile(512), tile(Q_LORA), tile(KV_LORA), tile(LANES),
                  full(qln.shape), full(wuq.shape), full(kvln.shape), full(wukv.shape),
                  full(qhn.shape), full(khn.shape), tile(LANES), tile(LANES), tile(LANES)],
        out_specs=[tile(Q_LORA), tile(KV_LORA), tile(LANES)] + [full(s) for s in acc_shapes],
        scratch_shapes=[pltpu.VMEM((TS, 1024), BF16), pltpu.VMEM((TS, 1024), BF16)],
        compiler_params=pltpu.CompilerParams(dimension_semantics=("arbitrary",), vmem_limit_bytes=VMEM_BIG),
    )(dq, dk, dv, cq, ckv, kr, qln, wuq, kvln, wukv, qhn, khn, cosf, sin_a, sin_b)


def _dproj_bwd(dq_sb, dk_sb, dv_sb, dg_sb, dcq, dckv, dg_mla, dkr, w_in_r, x, dy, norm_w, scale):
    s_len = x.shape[0]

    def body(dq_ref, dk_ref, dv_ref, dg_ref, dcq_ref, dckv_ref, dgm_ref, dkr_ref, w_ref, x_ref, dy_ref,
             nw_ref, scale_ref, dp_ref, gx_ref, dshift_ref, dscale_ref, dnw_ref):
        @pl.when(pl.program_id(0) == 0)
        def _():
            for r_ in (dshift_ref, dscale_ref, dnw_ref):
                r_[...] = jnp.zeros_like(r_)

        dp_ref[:, 0:512] = dq_ref[...].astype(BF16)
        dp_ref[:, 512:1024] = dk_ref[...].astype(BF16)
        dp_ref[:, 1024:1536] = dv_ref[...].astype(BF16)
        dp_ref[:, 1536:2048] = dg_ref[...]
        dp_ref[:, 2048:2432] = dcq_ref[...]
        dp_ref[:, 2432:2688] = dckv_ref[...]
        dp_ref[:, 2688:3200] = dgm_ref[...]
        dp_ref[:, 3200:3328] = dkr_ref[...]
        dh = _dot_nt(dp_ref[...], w_ref[...])
        xv = x_ref[...]
        r = lax.rsqrt(jnp.mean(xv * xv, axis=1, keepdims=True) + EPS)
        xh = xv * r
        nw = nw_ref[...]
        dshift_ref[...] += jnp.sum(dh, axis=0, keepdims=True)
        dscale_ref[...] += jnp.sum(dh * (xh * nw), axis=0, keepdims=True)
        dxnw = dh * (1.0 + scale_ref[...])
        dnw_ref[...] += jnp.sum(dxnw * xh, axis=0, keepdims=True)
        gx_ref[...] = dy_ref[...] + _rms_bwd(dxnw * nw, xh, r, D_MODEL)

    def tile(width):
        return pl.BlockSpec((TS, width), lambda i: (i, 0))

    def full(shape):
        return pl.BlockSpec(shape, lambda i: (0, 0))

    vec = (1, D_MODEL)
    return pl.pallas_call(
        body, name="dproj_bwd", grid=(s_len // TS,),
        out_shape=[jax.ShapeDtypeStruct((s_len, IN_COLS_R), BF16), jax.ShapeDtypeStruct((s_len, D_MODEL), F32)]
        + [jax.ShapeDtypeStruct(vec, F32)] * 3,
        in_specs=[tile(512)] * 4 + [tile(Q_LORA), tile(KV_LORA), tile(512), tile(LANES),
                                    full(w_in_r.shape), tile(D_MODEL), tile(D_MODEL), full(vec), full(vec)],
        out_specs=[tile(IN_COLS_R), tile(D_MODEL)] + [full(vec)] * 3,
        compiler_params=pltpu.CompilerParams(dimension_semantics=("arbitrary",), vmem_limit_bytes=VMEM_BIG),
    )(dq_sb, dk_sb, dv_sb, dg_sb, dcq, dckv, dg_mla, dkr, w_in_r, x, dy, norm_w, scale)


def _grad_w_in(hb, dproj):
    s_len = hb.shape[0]
    n_half = IN_COLS_R // 2

    def body(h_ref, d_ref, g_ref):
        @pl.when(pl.program_id(1) == 0)
        def _():
            g_ref[...] = jnp.zeros_like(g_ref)

        g_ref[...] += _dot_tn(h_ref[...], d_ref[...])

    return pl.pallas_call(
        body, name="grad_w_in", grid=(2, s_len // TN_S),
        out_shape=jax.ShapeDtypeStruct((D_MODEL, IN_COLS_R), F32),
        in_specs=[pl.BlockSpec((TN_S, D_MODEL), lambda n, s: (s, 0)),
                  pl.BlockSpec((TN_S, n_half), lambda n, s: (s, n))],
        out_specs=pl.BlockSpec((D_MODEL, n_half), lambda n, s: (0, n)),
        compiler_params=pltpu.CompilerParams(dimension_semantics=("parallel", "arbitrary"),
                                             vmem_limit_bytes=VMEM_BIG),
    )(hb, dproj)


def _final_exchange(gpack, ccol, wpack, mpack, vpack, w_ada, m_ada, v_ada, grads):
    n_sh = w_ada.shape[1]
    n = len(grads)

    def body(*refs):
        (g_ref, cc_ref, wp_ref, mp_ref, vp_ref, wa_ref, ma_ref, va_ref) = refs[:8]
        slabs_in = refs[8:8 + n]
        (og_ref, od_ref, om_ref, ov_ref, ag_ref, ad_ref, am_ref, av_ref) = refs[8 + n:16 + n]
        slabs_out = refs[16 + n:16 + 2 * n]
        gall_ref, ssem, rsem, slab_ssem, slab_rsem, lsem = refs[16 + 2 * n:]
        pos = _mesh_pos()
        me = _lin(pos)
        gall_ref[me] = g_ref[...]
        small = _all_gather_start(pos, g_ref, gall_ref, ssem, rsem, 0)
        own = [pltpu.make_async_copy(slabs_in[a].at[me], slabs_out[a].at[me], lsem.at[a]) for a in range(n)]
        for cp in own:
            cp.start()
        in_flight = [_all_to_all_start(pos, slabs_in[a], slabs_out[a], slab_ssem, slab_rsem, a * (N_DEV - 1))
                     for a in range(n)]
        _all_gather_wait(pos, g_ref, gall_ref, ssem, rsem, 0, small)

        tot = gall_ref[0]
        for j in range(1, N_DEV):
            tot = tot + gall_ref[j]
        og_ref[...] = tot
        od_ref[...], om_ref[...], ov_ref[...] = _adamw(wp_ref[...], tot, mp_ref[...], vp_ref[...])

        ga = jnp.zeros((D_MODEL, n_sh), F32)
        for j in range(N_DEV):
            d_mine = jnp.zeros((8, n_sh), F32)
            for k in range(N_DEV):
                d_mine = d_mine + jnp.where(me == k, gall_ref[j, :, PK_ADA + n_sh * k:PK_ADA + n_sh * (k + 1)], 0.0)
            col = _silu(cc_ref[j])
            ga = ga + jnp.concatenate(
                [col * d_mine[0:1, LANES * a:LANES * (a + 1)] for a in range(n_sh // LANES)], axis=1)
        ag_ref[...] = ga
        ad_ref[...], am_ref[...], av_ref[...] = _adamw(wa_ref[...], ga, ma_ref[...], va_ref[...])

        for a in range(n):
            _all_to_all_wait(pos, slabs_in[a], slabs_out[a], slab_ssem, slab_rsem, a * (N_DEV - 1), in_flight[a])
        for cp in own:
            cp.wait()

    pk = jax.ShapeDtypeStruct((8, PK_END), F32)
    ada = jax.ShapeDtypeStruct((D_MODEL, n_sh), F32)
    return pl.pallas_call(
        body, name="final_exchange",
        out_shape=[pk] * 4 + [ada] * 4 + [jax.ShapeDtypeStruct(g.shape, g.dtype) for g in grads],
        in_specs=[_vmem_spec()] * 8 + [_any_spec()] * n,
        out_specs=[_vmem_spec()] * 8 + [_any_spec()] * n,
        scratch_shapes=[
            pltpu.VMEM((N_DEV, 8, PK_END), F32),
            pltpu.SemaphoreType.DMA((N_DEV - 1,)),
            pltpu.SemaphoreType.DMA((N_DEV - 1,)),
            pltpu.SemaphoreType.DMA((n * (N_DEV - 1),)),
            pltpu.SemaphoreType.DMA((n * (N_DEV - 1),)),
            pltpu.SemaphoreType.DMA((n,)),
        ],
        compiler_params=pltpu.CompilerParams(vmem_limit_bytes=VMEM_BIG),
    )(gpack, ccol, wpack, mpack, vpack, w_ada, m_ada, v_ada, *grads)


def _adamw_reduce(name, parts, w, m, v, row_tile):
    rows, cols = w.shape

    def body(p_ref, w_ref, m_ref, v_ref, g_ref, d_ref, mo_ref, vo_ref):
        g = p_ref[0].astype(F32)
        for j in range(1, N_DEV):
            g = g + p_ref[j].astype(F32)
        g_ref[...] = g
        d_ref[...], mo_ref[...], vo_ref[...] = _adamw(w_ref[...], g, m_ref[...], v_ref[...])

    tile = pl.BlockSpec((row_tile, cols), lambda i: (i, 0))
    return pl.pallas_call(
        body, name=name, grid=(rows // row_tile,),
        out_shape=[jax.ShapeDtypeStruct((rows, cols), F32)] * 4,
        in_specs=[pl.BlockSpec((N_DEV, row_tile, cols), lambda i: (0, i, 0)), tile, tile, tile],
        out_specs=[tile] * 4,
        compiler_params=pltpu.CompilerParams(dimension_semantics=("parallel",), vmem_limit_bytes=VMEM_BIG),
    )(parts, w, m, v)


def _rope_tables(positions):
    inv_freq = 10000.0 ** (-jnp.arange(0, ROPE, 2, dtype=F32) / ROPE)
    ang = positions.astype(F32)[:, None] * inv_freq
    cos, sin = jnp.cos(ang), jnp.sin(ang)
    s_len = positions.shape[0]
    ones = jnp.ones((s_len, NOPE), F32)
    zeros = jnp.zeros((s_len, NOPE), F32)
    z16 = jnp.zeros((s_len, ROPE // 2), F32)
    pad1 = jnp.ones((s_len, LANES - MLA_QK), F32)
    pad0 = jnp.zeros((s_len, LANES - MLA_QK), F32)
    cosf = jnp.concatenate([ones, cos, cos, pad1], axis=1)
    sin_a = jnp.concatenate([zeros, -sin, z16, pad0], axis=1)
    sin_b = jnp.concatenate([zeros, z16, sin, pad0], axis=1)
    return cosf, sin_a, sin_b


def _rearrange_cols(w):
    pad = jnp.zeros((w.shape[0], IN_COLS_R - IN_COLS), w.dtype)
    return jnp.concatenate([w[:, :2688], w[:, 2720:3232], w[:, 2688:2720], pad], axis=1)


def _restore_cols(g):
    return jnp.concatenate([g[:, :2688], g[:, 3200:3232], g[:, 2688:3200]], axis=1)


def _pad_heads(w):
    rows = w.shape[0]
    w = w.reshape(rows, HEADS, MLA_QK)
    return jnp.pad(w, ((0, 0), (0, 0), (0, LANES - MLA_QK))).reshape(rows, HEADS * LANES)


def _unpad_heads(g):
    rows = g.shape[0]
    return g.reshape(rows, HEADS, LANES)[:, :, :MLA_QK].reshape(rows, HEADS * MLA_QK)


def _pad_lanes(v):
    return jnp.pad(v, ((0, 0), (0, LANES - v.shape[1])))


def _col_shards(g):
    rows = g.shape[0]
    return g.reshape(rows, N_DEV, g.shape[1] // N_DEV).transpose(1, 0, 2)


def _from_col_shards(g):
    return g.transpose(1, 0, 2).reshape(g.shape[1], N_DEV * g.shape[2])


def _pack(norm_w, qln, kvln, qhn, khn, ada, loss_lanes=None):
    if loss_lanes is None:
        loss_lanes = jnp.zeros((1, PK_END - PK_LOSS), F32)
    row = jnp.concatenate([norm_w, qln, kvln, _pad_lanes(qhn), _pad_lanes(khn), ada, loss_lanes], axis=1)
    return jnp.broadcast_to(row, (8, PK_END))


def _unpack(p):
    row = p[0:1]
    return (row[:, PK_NORM:PK_QLN], row[:, PK_QLN:PK_KVLN], row[:, PK_KVLN:PK_QHN],
            row[:, PK_QHN:PK_QHN + MLA_QK], row[:, PK_KHN:PK_KHN + MLA_QK], row[:, PK_ADA:PK_LOSS])


def kernel(x, c, positions, w_ada, b_ada, norm_w, w_in, q_lora_norm, w_uq, kv_lora_norm, w_ukv, q_head_norm, k_head_norm, w_out, loss_target, m_w_ada, m_b_ada, m_norm_w, m_w_in, m_q_lora_norm, m_w_uq, m_kv_lora_norm, m_w_ukv, m_q_head_norm, m_k_head_norm, m_w_out, v_w_ada, v_b_ada, v_norm_w, v_w_in, v_q_lora_norm, v_w_uq, v_kv_lora_norm, v_w_ukv, v_q_head_norm, v_k_head_norm, v_w_out):
    s_len = x.shape[1]
    x2 = x.reshape(s_len, D_MODEL)
    tgt = loss_target.reshape(s_len, D_MODEL)
    w_ada_s, w_in_s, w_uq_s, w_ukv_s, w_out_s = w_ada[0], w_in[0], w_uq[0], w_ukv[0], w_out[0]

    ada8, c_all = _ada_fwd(jnp.broadcast_to(c, (8, D_MODEL)), w_ada_s, b_ada.reshape(N_DEV, -1))
    ada = ada8.reshape(1, 3 * D_MODEL)
    shift, scale, gate = ada[:, :D_MODEL], ada[:, D_MODEL:2 * D_MODEL], ada[:, 2 * D_MODEL:]

    g_uq, g_ukv, g_out, g_in = _gather_weights([w_uq_s, w_ukv_s, w_out_s, w_in_s])
    w_in_r = _rearrange_cols(_from_col_shards(g_in))
    wuq_p = _pad_heads(_from_col_shards(g_uq))
    wukv_f = _from_col_shards(g_ukv)
    w_out_f = g_out.reshape(D_MODEL, D_MODEL)

    cosf, sin_a, sin_b = _rope_tables(positions[0])
    qhn_p, khn_p = _pad_lanes(q_head_norm), _pad_lanes(k_head_norm)

    hb, qkv, g_sb, cq, ckv, g_mla, kr, qm, km, vm, q_sb_t, qm_t = _fwd_pre(
        x2, shift, scale, norm_w, w_in_r, q_lora_norm, wuq_p, kv_lora_norm, wukv_f, qhn_p, khn_p,
        cosf, sin_a, sin_b)
    o_sb, w_saved, l_saved = _sb_fwd(qkv)
    o_mla, lse, p_saved = _mla_fwd(qm, km, vm)

    dy, do_sb, dg_sb, do_mla, dg_mla, gw_out, d_gate, loss_acc, do_sb_t, do_mla_t = _mid(
        o_sb, g_sb, o_mla, g_mla, x2, tgt, gate, w_out_f)

    dq_sb, dk_sb_t, dv_sb_t = _sb_bwd(qkv, q_sb_t, do_sb, do_sb_t, w_saved, l_saved)
    dk_sb, dv_sb = _from_key_tiles(dk_sb_t), _from_key_tiles(dv_sb_t)
    dq_m, dk_m_t, dv_m_t = _mla_bwd(qm_t, km, vm, o_mla, do_mla, do_mla_t, lse, p_saved)
    dk_m, dv_m = _from_key_tiles(dk_m_t), _from_key_tiles(dv_m_t)
    dcq, dckv, dkr, gw_uq_p, gw_ukv, g_qhn, g_khn, g_qln, g_kvln = _mla_pre_bwd(
        dq_m, dk_m, dv_m, cq, ckv, kr, q_lora_norm, wuq_p, kv_lora_norm, wukv_f, qhn_p, khn_p,
        cosf, sin_a, sin_b)
    dproj, grad_x, d_shift, d_scale, g_norm_w = _dproj_bwd(
        dq_sb, dk_sb, dv_sb, dg_sb, dcq, dckv, dg_mla, dkr, w_in_r, x2, dy, norm_w, scale)
    gw_in = _restore_cols(_grad_w_in(hb, dproj))

    d_ada = jnp.concatenate([d_shift, d_scale, d_gate], axis=1)
    gpack = _pack(g_norm_w, g_qln, g_kvln, g_qhn[:, :MLA_QK], g_khn[:, :MLA_QK], d_ada, loss_acc)
    wpack = _pack(norm_w, q_lora_norm, kv_lora_norm, q_head_norm, k_head_norm, b_ada)
    mpack = _pack(m_norm_w, m_q_lora_norm, m_kv_lora_norm, m_q_head_norm, m_k_head_norm, m_b_ada)
    vpack = _pack(v_norm_w, v_q_lora_norm, v_kv_lora_norm, v_q_head_norm, v_k_head_norm, v_b_ada)
    ccol = jnp.broadcast_to(c_all[:, :, None], (N_DEV, D_MODEL, LANES))
    slabs = [g.astype(BF16) for g in (
        _col_shards(gw_in), _col_shards(_unpad_heads(gw_uq_p)), _col_shards(gw_ukv),
        gw_out.reshape(N_DEV, D_MODEL // N_DEV, D_MODEL))]
    pg, pd, pm, pv, ada_g, ada_d, ada_m, ada_v, r_in, r_uq, r_ukv, r_out = _final_exchange(
        gpack, ccol, wpack, mpack, vpack, w_ada_s, m_w_ada[0], v_w_ada[0], slabs)
    loss = 0.5 * jnp.sum(pg[0, PK_LOSS:PK_END]) / D_MODEL

    in_g, in_d, in_m, in_v = _adamw_reduce("adamw_w_in", r_in, w_in_s, m_w_in[0], v_w_in[0], 256)
    uq_g, uq_d, uq_m, uq_v = _adamw_reduce("adamw_w_uq", r_uq, w_uq_s, m_w_uq[0], v_w_uq[0], w_uq_s.shape[0])
    ukv_g, ukv_d, ukv_m, ukv_v = _adamw_reduce(
        "adamw_w_ukv", r_ukv, w_ukv_s, m_w_ukv[0], v_w_ukv[0], w_ukv_s.shape[0])
    out_g, out_d, out_m, out_v = _adamw_reduce(
        "adamw_w_out", r_out, w_out_s, m_w_out[0], v_w_out[0], w_out_s.shape[0])

    def group(ada_t, pk, in_t, uq_t, ukv_t, out_t):
        nw, qln, kvln, qhn, khn, b = _unpack(pk)
        return (ada_t[None], b, nw, in_t[None], qln, uq_t[None], kvln, ukv_t[None], qhn, khn, out_t[None])

    return (loss, grad_x.reshape(1, s_len, D_MODEL),
            *group(ada_g, pg, in_g, uq_g, ukv_g, out_g),
            *group(ada_d, pd, in_d, uq_d, ukv_d, out_d),
            *group(ada_m, pm, in_m, uq_m, ukv_m, out_m),
            *group(ada_v, pv, in_v, uq_v, ukv_v, out_v))
```

```python
import functools
import math

import jax
import jax.numpy as jnp
from jax import lax
from jax.experimental import pallas as pl
from jax.experimental.pallas import tpu as pltpu

F32 = jnp.float32
BF16 = jnp.bfloat16

N_DEV = 8
D_MODEL = 1024
HEADS = 8
SB_WIDTH = 512
MLA_WIDTH = 512
Q_LORA = 384
KV_LORA = 256
ROPE = 32
NOPE = 64
MLA_QK = 96
LANES = 128
IN_COLS = 3232
IN_COLS_R = 3328
EPS = 1e-6
NEG = -1e30

ADAM_LR = 0.001
ADAM_B1 = 0.9
ADAM_B2 = 0.999
ADAM_EPS = 1e-08
ADAM_WD = 0.01
ADAM_STEP = 10

TS = 512
TS_FWD = 256
TQ = 512
KEY_UNROLL = 2
TN_S = 512
VMEM_BIG = 56 * 1024 * 1024

PK_NORM, PK_QLN, PK_KVLN, PK_QHN, PK_KHN, PK_ADA, PK_LOSS, PK_END = 0, 1024, 1408, 1664, 1792, 1920, 4992, 6016

MESH_ID = pl.DeviceIdType.MESH


def _dot_nn(a, b):
    return lax.dot_general(a, b, (((1,), (0,)), ((), ())), preferred_element_type=F32)


def _dot_nt(a, b):
    return lax.dot_general(a, b, (((1,), (1,)), ((), ())), preferred_element_type=F32)


def _dot_tn(a, b):
    return lax.dot_general(a, b, (((0,), (0,)), ((), ())), preferred_element_type=F32)


def _split_bf16(a):
    hi = a.astype(BF16)
    lo = (a - hi.astype(F32)).astype(BF16)
    return hi, lo


def _dot3(a, b):
    ah, al = _split_bf16(a)
    bh, bl = _split_bf16(b)
    return _dot_nn(ah, bh) + _dot_nn(ah, bl) + _dot_nn(al, bh)


def _sigmoid(g):
    return 1.0 / (1.0 + jnp.exp(-g))


def _silu(g):
    return g * _sigmoid(g)


def _lane_iota(shape):
    return lax.broadcasted_iota(jnp.int32, shape, len(shape) - 1)


def _adamw(w, g, m, v):
    m = ADAM_B1 * m + (1.0 - ADAM_B1) * g
    v = ADAM_B2 * v + (1.0 - ADAM_B2) * (g * g)
    m_hat = m / (1.0 - ADAM_B1 ** ADAM_STEP)
    v_hat = v / (1.0 - ADAM_B2 ** ADAM_STEP)
    delta = -ADAM_LR * (m_hat / (jnp.sqrt(v_hat) + ADAM_EPS) + ADAM_WD * w)
    return delta, m, v


def _mesh_pos():
    return lax.axis_index("x"), lax.axis_index("y"), lax.axis_index("c")


def _peer(pos, k):
    x, y, c = pos
    return (1 - x if k & 4 else x, 1 - y if k & 2 else y, 1 - c if k & 1 else c)


def _lin(pos):
    return 4 * pos[0] + 2 * pos[1] + pos[2]


def _remote(src, dst, send_sems, recv_sems, idx, peer):
    return pltpu.make_async_remote_copy(
        src_ref=src, dst_ref=dst, send_sem=send_sems.at[idx], recv_sem=recv_sems.at[idx],
        device_id=peer, device_id_type=MESH_ID)


def _all_gather_start(pos, src, buf, send_sems, recv_sems, base):
    me = _lin(pos)
    sent = []
    for k in range(1, N_DEV):
        cp = _remote(src, buf.at[me], send_sems, recv_sems, base + k - 1, _peer(pos, k))
        cp.start()
        sent.append(cp)
    return sent


def _all_gather_wait(pos, src, buf, send_sems, recv_sems, base, sent):
    for k in range(1, N_DEV):
        peer = _peer(pos, k)
        _remote(src, buf.at[_lin(peer)], send_sems, recv_sems, base + k - 1, peer).wait_recv()
    for cp in sent:
        cp.wait_send()


def _all_gather(pos, src, buf, send_sems, recv_sems, base):
    sent = _all_gather_start(pos, src, buf, send_sems, recv_sems, base)
    _all_gather_wait(pos, src, buf, send_sems, recv_sems, base, sent)


def _all_to_all_start(pos, src, buf, send_sems, recv_sems, base):
    me = _lin(pos)
    sent = []
    for k in range(1, N_DEV):
        peer = _peer(pos, k)
        cp = _remote(src.at[_lin(peer)], buf.at[me], send_sems, recv_sems, base + k - 1, peer)
        cp.start()
        sent.append(cp)
    return sent


def _all_to_all_wait(pos, src, buf, send_sems, recv_sems, base, sent):
    me = _lin(pos)
    for k in range(1, N_DEV):
        peer = _peer(pos, k)
        _remote(src.at[me], buf.at[_lin(peer)], send_sems, recv_sems, base + k - 1, peer).wait_recv()
    for cp in sent:
        cp.wait_send()


def _all_to_all(pos, src, buf, send_sems, recv_sems, base):
    sent = _all_to_all_start(pos, src, buf, send_sems, recv_sems, base)
    _all_to_all_wait(pos, src, buf, send_sems, recv_sems, base, sent)


def _two_level_gather(pos, bufs, send_sems, recv_sems):
    x, y, c = pos
    me, sibling = (x, y, c), (x, y, 1 - c)
    chips = [(1 - x, y), (x, 1 - y), (1 - x, 1 - y)]

    def copy(a, k, block, to):
        slot = bufs[a].at[_lin(block)]
        return _remote(slot, slot, send_sems, recv_sems, 7 * a + k, to)

    started = []
    for a in range(len(bufs)):
        first = [copy(a, 0, me, sibling)] + [copy(a, 1 + j, me, (*chip, c)) for j, chip in enumerate(chips)]
        for cp in first:
            cp.start()
        started += first
    for a in range(len(bufs)):
        for j, chip in enumerate(chips):
            copy(a, 1 + j, (*chip, c), me).wait_recv()
            passed = copy(a, 4 + j, (*chip, c), sibling)
            passed.start()
            started.append(passed)
    for a in range(len(bufs)):
        copy(a, 0, sibling, me).wait_recv()
        for j, chip in enumerate(chips):
            copy(a, 4 + j, (*chip, 1 - c), me).wait_recv()
    for cp in started:
        cp.wait_send()


def _vmem_spec():
    return pl.BlockSpec(memory_space=pltpu.VMEM)


def _any_spec():
    return pl.BlockSpec(memory_space=pl.ANY)


def _row_select(slots, n):
    r = lax.broadcasted_iota(jnp.int32, (N_DEV, n), 0)
    out = jnp.zeros((N_DEV, n), F32)
    for j in range(N_DEV):
        out = out + jnp.where(r == j, slots[j], 0.0)
    return out


def _ada_fwd(c8, w_ada, b_ada8):
    n_sh = w_ada.shape[1]

    def body(c_ref, w_ref, b_ref, out_ref, call_out_ref, call_ref, psend_ref, precv_ref, ssem, rsem):
        pos = _mesh_pos()
        me = _lin(pos)
        call_ref[me] = c_ref[...]
        _all_gather(pos, c_ref, call_ref, ssem, rsem, 0)
        call_out_ref[...] = _row_select([call_ref[j] for j in range(N_DEV)], D_MODEL)
        w = w_ref[...]
        for j in range(N_DEV):
            psend_ref[j] = _dot3(_silu(call_ref[j]), w)
        precv_ref[me] = psend_ref[me]
        _all_to_all(pos, psend_ref, precv_ref, ssem, rsem, N_DEV - 1)
        out_ref[...] = _row_select([precv_ref[j] for j in range(N_DEV)], n_sh) + b_ref[...]

    return pl.pallas_call(
        body, name="ada_fwd",
        out_shape=[jax.ShapeDtypeStruct((N_DEV, n_sh), F32), jax.ShapeDtypeStruct((N_DEV, D_MODEL), F32)],
        in_specs=[_vmem_spec()] * 3, out_specs=[_vmem_spec()] * 2,
        scratch_shapes=[
            pltpu.VMEM((N_DEV, 8, D_MODEL), F32),
            pltpu.VMEM((N_DEV, 8, n_sh), F32),
            pltpu.VMEM((N_DEV, 8, n_sh), F32),
            pltpu.SemaphoreType.DMA((2 * (N_DEV - 1),)),
            pltpu.SemaphoreType.DMA((2 * (N_DEV - 1),)),
        ],
    )(c8, w_ada, b_ada8)


def _gather_weights(shards):
    n = len(shards)

    def body(*refs):
        ins, outs = refs[:n], refs[n:2 * n]
        ssem, rsem = refs[2 * n], refs[2 * n + 1]
        pos = _mesh_pos()
        me = _lin(pos)
        for a in range(n):
            outs[a][me] = ins[a][...].astype(BF16)
        _two_level_gather(pos, outs, ssem, rsem)

    return pl.pallas_call(
        body, name="gather_weights",
        out_shape=[jax.ShapeDtypeStruct((N_DEV,) + s.shape, BF16) for s in shards],
        in_specs=[_vmem_spec()] * n, out_specs=[_vmem_spec()] * n,
        scratch_shapes=[
            pltpu.SemaphoreType.DMA((n * (N_DEV - 1),)),
            pltpu.SemaphoreType.DMA((n * (N_DEV - 1),)),
        ],
        compiler_params=pltpu.CompilerParams(vmem_limit_bytes=VMEM_BIG),
    )(*shards)


def _rope(t, cosf, sin_a, sin_b):
    return t * cosf + pltpu.roll(t, 112, 1) * sin_a + pltpu.roll(t, 16, 1) * sin_b


def _expand_rope(table):
    lane = _lane_iota(table.shape)
    by64, by32 = pltpu.roll(table, 64, 1), pltpu.roll(table, 32, 1)
    cosf = jnp.where(jnp.logical_and(lane >= NOPE, lane < MLA_QK), by64, 1.0)
    sin_a = jnp.where(jnp.logical_and(lane >= NOPE, lane < NOPE + ROPE // 2), by32, 0.0)
    sin_b = jnp.where(jnp.logical_and(lane >= NOPE + ROPE // 2, lane < MLA_QK), by32, 0.0)
    return cosf, sin_a, sin_b


def _rope_t(d, cosf, sin_a, sin_b):
    return d * cosf + pltpu.roll(d * sin_a, 16, 1) + pltpu.roll(d * sin_b, 112, 1)


def _head_rms(t):
    return lax.rsqrt(jnp.sum(t * t, axis=1, keepdims=True) * (1.0 / MLA_QK) + EPS)


def _rms_bwd(dxhat_w, xhat, r, n):
    return r * (dxhat_w - xhat * (jnp.sum(dxhat_w * xhat, axis=1, keepdims=True) * (1.0 / n)))


def _mla_latents(cq, ckv, kr, qln, kvln, wuq, wukv):
    rq = lax.rsqrt(jnp.mean(cq * cq, axis=1, keepdims=True) + EPS)
    rkv = lax.rsqrt(jnp.mean(ckv * ckv, axis=1, keepdims=True) + EPS)
    cq_hat = cq * rq
    ckv_hat = ckv * rkv
    cqn = (cq_hat * qln).astype(BF16)
    ckvn = (ckv_hat * kvln).astype(BF16)
    q_all = _dot_nn(cqn, wuq)
    kv = _dot_nn(ckvn, wukv)
    kr64 = pltpu.roll(kr, 64, 1)
    return rq, rkv, cq_hat, ckv_hat, cqn, ckvn, q_all, kv, kr64


def _fwd_pre(x, shift, scale, norm_w, w_in_r, qln, wuq, kvln, wukv, qhn, khn, rope_table):
    s_len = x.shape[0]

    def body(x_ref, shift_ref, scale_ref, nw_ref, w_ref, qln_ref, wuq_ref, kvln_ref, wukv_ref,
             qhn_ref, khn_ref, rt_ref,
             hb_ref, qkv_ref, gsb_ref, cq_ref, ckv_ref, gmla_ref, kr_ref, qm_ref, km_ref, vm_ref,
             qsbt_ref, qmt_ref):
        xv = x_ref[...]
        r = lax.rsqrt(jnp.mean(xv * xv, axis=1, keepdims=True) + EPS)
        h = (xv * r) * nw_ref[...] * (1.0 + scale_ref[...]) + shift_ref[...]
        hb = h.astype(BF16)
        hb_ref[...] = hb
        qkv = _dot_nn(hb, w_ref[:, 0:1536])
        qkv_ref[...] = qkv.astype(BF16)
        qsbt_ref[...] = qkv[:, :SB_WIDTH].T.astype(BF16)
        gsb_ref[...] = _dot_nn(hb, w_ref[:, 1536:2048])
        cq = _dot_nn(hb, w_ref[:, 2048:2432])
        ckv = _dot_nn(hb, w_ref[:, 2432:2688])
        gmla_ref[...] = _dot_nn(hb, w_ref[:, 2688:3200])
        kr = _dot_nn(hb, w_ref[:, 3200:3328])
        cq_ref[...] = cq
        ckv_ref[...] = ckv
        kr_ref[...] = kr
        _, _, _, _, _, _, q_all, kv, kr64 = _mla_latents(
            cq, ckv, kr, qln_ref[...], kvln_ref[...], wuq_ref[...], wukv_ref[...])
        cosf, sa, sb = _expand_rope(rt_ref[...])
        qhn_v, khn_v = qhn_ref[...], khn_ref[...]
        low = _lane_iota((TS_FWD, LANES)) < NOPE
        blks = [slice(LANES * hd, LANES * (hd + 1)) for hd in range(HEADS)]
        q_raw = [q_all[:, b] for b in blks]
        k_raw = [jnp.where(low, kv[:, b], kr64) for b in blks]
        q_rms = [_head_rms(t) for t in q_raw]
        k_rms = [_head_rms(t) for t in k_raw]
        q_n = [t * r * (qhn_v * MLA_SCALE) for t, r in zip(q_raw, q_rms)]
        q_roped = [_rope(t, cosf, sa, sb) for t in q_n]
        kr_roped = _rope(kr64 * khn_v, cosf, sa, sb)
        k_roped = [jnp.where(low, t * khn_v, kr_roped) * r for t, r in zip(k_raw, k_rms)]
        for hd, b in enumerate(blks):
            qm_ref[:, b] = q_roped[hd].astype(BF16)
            qmt_ref[b, :] = q_roped[hd].T.astype(BF16)
            km_ref[:, b] = k_roped[hd].astype(BF16)
        for p in range(HEADS // 2):
            even = kv[:, LANES * 2 * p:LANES * (2 * p + 1)]
            odd = kv[:, LANES * (2 * p + 1):LANES * (2 * p + 2)]
            vm_ref[:, LANES * p:LANES * (p + 1)] = jnp.where(low, pltpu.roll(even, 64, 1), odd).astype(BF16)

    def tile(width):
        return pl.BlockSpec((TS_FWD, width), lambda i: (i, 0))

    def full(a):
        return pl.BlockSpec(a.shape, lambda i: (0, 0))

    out_widths = [(D_MODEL, BF16), (1536, BF16), (512, F32), (Q_LORA, F32), (KV_LORA, F32),
                  (512, F32), (LANES, F32), (1024, BF16), (1024, BF16), (512, BF16)]
    t_heights = [SB_WIDTH, HEADS * LANES]
    return pl.pallas_call(
        body, name="fwd_pre", grid=(s_len // TS_FWD,),
        out_shape=[jax.ShapeDtypeStruct((s_len, w), dt) for w, dt in out_widths]
        + [jax.ShapeDtypeStruct((hgt, s_len), BF16) for hgt in t_heights],
        in_specs=[tile(D_MODEL), full(shift), full(scale), full(norm_w), full(w_in_r), full(qln), full(wuq),
                  full(kvln), full(wukv), full(qhn), full(khn), tile(LANES)],
        out_specs=[tile(w) for w, _ in out_widths]
        + [pl.BlockSpec((hgt, TS_FWD), lambda i: (0, i)) for hgt in t_heights],
        compiler_params=pltpu.CompilerParams(dimension_semantics=("parallel",), vmem_limit_bytes=VMEM_BIG),
    )(x, shift, scale, norm_w, w_in_r, qln, wuq, kvln, wukv, qhn, khn, rope_table)


CUM_W = 256


def _tri(later):
    j = lax.broadcasted_iota(jnp.int32, (CUM_W, CUM_W), 0)
    s = lax.broadcasted_iota(jnp.int32, (CUM_W, CUM_W), 1)
    return (j > s if later else j < s).astype(BF16)


def _suffix_sums(a, a_bf16, tri_m, carry):
    n = a.shape[1] // CUM_W
    outs = [None] * n
    for i in reversed(range(n)):
        cols = slice(CUM_W * i, CUM_W * (i + 1))
        outs[i] = _dot_nn(a_bf16[:, cols], tri_m) + carry
        carry = carry + _rowsum(a[:, cols])
    return (outs[0] if n == 1 else jnp.concatenate(outs, axis=1)), carry


def _sb_weights(qm, kb, carry, tri_u, diag):
    z = _dot_nt(qm, kb)
    nz = -z
    lk = jnp.minimum(nz, 0.0) - jnp.log(1.0 + jnp.exp(jnp.minimum(z, nz)))
    if diag:
        t = lax.broadcasted_iota(jnp.int32, (TQ, TQ), 0)
        s = lax.broadcasted_iota(jnp.int32, (TQ, TQ), 1)
        valid = s < t
        lk = jnp.where(valid, lk, 0.0)
    lk_hi = lk.astype(BF16)
    log_beta = z + lk
    after, carry = _suffix_sums(lk, lk_hi, tri_u, carry)
    logw = log_beta + after
    if diag:
        logw = jnp.where(valid, logw, NEG)
    return lk_hi, jnp.exp(logw), carry


SB_SCALE = 0.125


def _head_masks():
    lane = _lane_iota((1, LANES))
    return [lane < 64, lane >= 64]


def _masked(hm, a):
    return jnp.where(hm, a, jnp.zeros_like(a))


def _rowsum(a):
    return jnp.sum(a, axis=1, keepdims=True)


def _key_rows(kj):
    return pl.ds(pl.multiple_of(kj * TQ, TQ), TQ)


def _over_key_tiles(count, fn, st, ascending):
    n_full = count // KEY_UNROLL
    n_rest = count - n_full * KEY_UNROLL

    def group(g, s_):
        return fn([g * KEY_UNROLL + (u if ascending else KEY_UNROLL - 1 - u) for u in range(KEY_UNROLL)], s_)

    if ascending:
        st = lax.fori_loop(0, n_full, group, st)
        return lax.fori_loop(0, n_rest, lambda i, s_: fn([n_full * KEY_UNROLL + i], s_), st)
    st = lax.fori_loop(0, n_rest, lambda i, s_: fn([count - 1 - i], s_), st)
    return lax.fori_loop(0, n_full, lambda i, s_: group(n_full - 1 - i, s_), st)


STAGE_SLOTS = 2 * KEY_UNROLL


def _stage_copies(to_hbm, hbm_refs, scr_refs, sems, pair, qi, kj):
    slot = (qi - kj) % STAGE_SLOTS
    out = []
    for h in range(2):
        for a in range(2):
            hbm, scr = hbm_refs[a].at[2 * pair + h, qi, kj], scr_refs[a].at[slot, h]
            sem = sems.at[4 * slot + 2 * h + a]
            out.append(pltpu.make_async_copy(scr, hbm, sem) if to_hbm else pltpu.make_async_copy(hbm, scr, sem))
    return out


def _sb_fwd(qkv):
    s_len = qkv.shape[0]
    nq = s_len // TQ

    def body(q_ref, k_ref, v_ref, o_ref, w_hbm, l_hbm, w_scr, l_scr, sems):
        pair, qi = pl.program_id(0), pl.program_id(1)
        q = q_ref[...]
        tri_u = _tri(True)
        masks = _head_masks()
        qms = [_masked(hm, q) * SB_SCALE for hm in masks]

        def copies(kj, of_qi=qi):
            return _stage_copies(True, (w_hbm, l_hbm), (w_scr, l_scr), sems, pair, of_qi, kj)

        def drain(of_qi):
            for kj in range(STAGE_SLOTS):
                @pl.when(kj <= of_qi)
                def _():
                    for cp in copies(kj, of_qi):
                        cp.wait()

        def block(kj, st, diag, before_staging=None):
            rows = _key_rows(kj)
            slot = (qi - kj) % STAGE_SLOTS
            kb, vb = k_ref[rows, :], v_ref[rows, :]
            carries, acc = list(st[:2]), st[2]
            staged = []
            for h in range(2):
                lk_hi, w, carries[h] = _sb_weights(qms[h], kb, carries[h], tri_u, diag)
                wb = w.astype(BF16)
                staged.append((wb, lk_hi))
                acc = acc + _dot_nn(wb, _masked(masks[h], vb))
            if before_staging is not None:
                before_staging()
            for h in range(2):
                w_scr[slot, h], l_scr[slot, h] = staged[h]
            return carries[0], carries[1], acc

        def trip(tiles, st):
            for kj in tiles:
                @pl.when(qi - kj >= STAGE_SLOTS)
                def _():
                    for cp in copies(kj + STAGE_SLOTS):
                        cp.wait()
            for kj in tiles:
                st = block(kj, st, False)
            for kj in tiles:
                for cp in copies(kj):
                    cp.start()
            return st

        def drain_previous_step():
            @pl.when(jnp.logical_or(pair > 0, qi > 0))
            def _():
                drain(jnp.where(qi == 0, nq - 1, qi - 1))

        zc = jnp.zeros((TQ, 1), F32)
        st = block(qi, (zc, zc, jnp.zeros((TQ, LANES), F32)), True, drain_previous_step)
        for cp in copies(qi):
            cp.start()
        st = _over_key_tiles(qi, trip, st, ascending=False)
        o_ref[...] = st[2]

        @pl.when(jnp.logical_and(pair == HEADS // 2 - 1, qi == nq - 1))
        def _():
            drain(qi)

    saved = jax.ShapeDtypeStruct((HEADS, nq, nq, TQ, TQ), BF16)
    stage = pltpu.VMEM((STAGE_SLOTS, 2, TQ, TQ), BF16)
    return pl.pallas_call(
        body, name="sb_fwd", grid=(HEADS // 2, nq),
        out_shape=[jax.ShapeDtypeStruct((s_len, SB_WIDTH), F32), saved, saved],
        in_specs=[pl.BlockSpec((TQ, LANES), lambda p, i: (i, p)),
                  pl.BlockSpec((s_len, LANES), lambda p, i: (0, 4 + p)),
                  pl.BlockSpec((s_len, LANES), lambda p, i: (0, 8 + p))],
        out_specs=[pl.BlockSpec((TQ, LANES), lambda p, i: (i, p)), _any_spec(), _any_spec()],
        scratch_shapes=[stage, stage, pltpu.SemaphoreType.DMA((4 * STAGE_SLOTS,))],
        compiler_params=pltpu.CompilerParams(dimension_semantics=("arbitrary", "arbitrary"),
                                             vmem_limit_bytes=VMEM_BIG),
    )(qkv, qkv, qkv)


def _prefix_sums(a, tri_m, carry):
    n = a.shape[1] // CUM_W
    outs = [None] * n
    for i in range(n):
        cols = slice(CUM_W * i, CUM_W * (i + 1))
        outs[i] = _dot_nn(a[:, cols].astype(BF16), tri_m) + carry
        carry = carry + _rowsum(a[:, cols])
    return (outs[0] if n == 1 else jnp.concatenate(outs, axis=1)), carry


def _head_rows(a0, a1):
    sub = lax.broadcasted_iota(jnp.int32, a0.shape, 0)
    return jnp.where(sub < NOPE, a0, a1)


def _sb_bwd(qkv, q_t, do, do_t, w_saved, l_saved):
    s_len = qkv.shape[0]
    nq = s_len // TQ

    def body(qt_ref, k_ref, v_ref, do_ref, dot_ref, w_hbm, l_hbm, dq_ref, dk_ref, dv_ref, w_scr, l_scr, sems):
        pair, qi = pl.program_id(0), pl.program_id(1)

        def copies(kj, of_pair=pair, of_qi=qi):
            return _stage_copies(False, (w_hbm, l_hbm), (w_scr, l_scr), sems, of_pair, of_qi, kj)

        def start_first_tiles(of_pair, of_qi):
            for cp in copies(0, of_pair, of_qi):
                cp.start()

            @pl.when(of_qi >= 1)
            def _():
                for cp in copies(1, of_pair, of_qi):
                    cp.start()

        @pl.when(jnp.logical_and(pair == 0, qi == 0))
        def _():
            start_first_tiles(pair, qi)

        @pl.when(qi == 0)
        def _():
            dk_ref[...] = jnp.zeros_like(dk_ref)
            dv_ref[...] = jnp.zeros_like(dv_ref)

        qt = qt_ref[...] * SB_SCALE
        dot_v = dot_ref[...]
        do_v = do_ref[...]
        tri_before = _tri(False)
        masks = _head_masks()
        doms = [_masked(hm, do_v) for hm in masks]

        def block(kj, st):
            rows = _key_rows(kj)
            slot = (qi - kj) % STAGE_SLOTS
            kb, vb = k_ref[rows, :], v_ref[rows, :]
            carries, dqs = list(st[0:2]), list(st[2:4])
            dk_t, dv_t = [], []
            for h in range(2):
                wb = w_scr[slot, h]
                d_l = _dot_nt(doms[h], vb) * wb.astype(F32)
                before, carries[h] = _prefix_sums(d_l, tri_before, carries[h])
                keep = jnp.exp(l_scr[slot, h].astype(F32))
                dzb = (d_l * keep - before * (1.0 - keep)).astype(BF16)
                dk_t.append(_dot_nn(qt, dzb))
                dv_t.append(_dot_nn(dot_v, wb))
                dqs[h] = dqs[h] + _dot_nn(dzb, kb)
            dk_ref[kj] += _head_rows(*dk_t)
            dv_ref[kj] += _head_rows(*dv_t)
            return (*carries, *dqs)

        def trip(tiles, st):
            for kj in tiles:
                @pl.when(kj + 2 <= qi)
                def _():
                    for cp in copies(kj + 2):
                        cp.start()
            for kj in tiles:
                for cp in copies(kj):
                    cp.wait()
            for kj in tiles:
                st = block(kj, st)
            return st

        zc = jnp.zeros((TQ, 1), F32)
        za = jnp.zeros((TQ, LANES), F32)
        st = _over_key_tiles(qi + 1, trip, (zc, zc, za, za), ascending=True)
        dq_ref[...] = jnp.where(masks[0], st[2], st[3]) * SB_SCALE

        @pl.when(jnp.logical_or(pair < HEADS // 2 - 1, qi < nq - 1))
        def _():
            wraps = qi == nq - 1
            start_first_tiles(jnp.where(wraps, pair + 1, pair), jnp.where(wraps, 0, qi + 1))

    tile = pl.BlockSpec((TQ, LANES), lambda p, i: (i, p))
    tile_t = pl.BlockSpec((LANES, TQ), lambda p, i: (p, i))
    col_t = pl.BlockSpec((nq, LANES, TQ), lambda p, i: (0, p, 0))
    stage = pltpu.VMEM((STAGE_SLOTS, 2, TQ, TQ), BF16)
    key_t = jax.ShapeDtypeStruct((nq, SB_WIDTH, TQ), F32)
    return pl.pallas_call(
        body, name="sb_bwd", grid=(HEADS // 2, nq),
        out_shape=[jax.ShapeDtypeStruct((s_len, SB_WIDTH), F32), key_t, key_t],
        in_specs=[tile_t,
                  pl.BlockSpec((s_len, LANES), lambda p, i: (0, 4 + p)),
                  pl.BlockSpec((s_len, LANES), lambda p, i: (0, 8 + p)),
                  tile, tile_t, _any_spec(), _any_spec()],
        out_specs=[tile, col_t, col_t],
        scratch_shapes=[stage, stage, pltpu.SemaphoreType.DMA((4 * STAGE_SLOTS,))],
        compiler_params=pltpu.CompilerParams(dimension_semantics=("arbitrary", "arbitrary"),
                                             vmem_limit_bytes=VMEM_BIG),
    )(q_t, qkv, qkv, do, do_t, w_saved, l_saved)


def _from_key_tiles(a_t):
    return a_t.transpose(0, 2, 1).reshape(a_t.shape[0] * a_t.shape[2], a_t.shape[1])


MLA_SCALE = 1.0 / math.sqrt(MLA_QK)


def _causal_mask():
    t = lax.broadcasted_iota(jnp.int32, (TQ, TQ), 0)
    s = lax.broadcasted_iota(jnp.int32, (TQ, TQ), 1)
    return s <= t


def _head_lanes(h):
    return slice(LANES * h, LANES * (h + 1))


P_SLOTS = 2 * KEY_UNROLL
P_COLS = TQ + LANES


def _tile_number(pair, qi, kj, nq):
    return pair * (nq * (nq + 1) // 2) + (qi * (qi + 1)) // 2 + kj


def _p_copy(to_hbm, p_hbm, p_scr, sems, pair, qi, kj, nq, h):
    slot = _tile_number(pair, qi, kj, nq) % P_SLOTS
    hbm, scr, sem = p_hbm.at[2 * pair + h, qi, kj], p_scr.at[slot, h], sems.at[2 * slot + h]
    return pltpu.make_async_copy(scr, hbm, sem) if to_hbm else pltpu.make_async_copy(hbm, scr, sem)


def _mla_fwd(qm, km, vm):
    s_len = qm.shape[0]
    nq = s_len // TQ

    def body(q_ref, k_ref, v_ref, o_ref, lse_ref, p_hbm, p_scr, sems):
        pair, qi = pl.program_id(0), pl.program_id(1)
        masks = _head_masks()
        qhs = [q_ref[:, _head_lanes(h)] for h in range(2)]
        lane = _lane_iota((TQ, LANES))

        def number(kj):
            return _tile_number(pair, qi, kj, nq)

        def copies(kj):
            return [_p_copy(True, p_hbm, p_scr, sems, pair, qi, kj, nq, h) for h in range(2)]

        def two_parts(m):
            hi = m.astype(BF16).astype(F32)
            return jnp.where(lane < 64, hi, m - hi).astype(BF16)

        def block(kj, st, diag):
            rows = _key_rows(kj)
            slot = number(kj) % P_SLOTS
            vb = v_ref[rows, :]
            ms, ls, acc = list(st[0:2]), list(st[2:4]), st[4]
            alphas, pvs = [], []
            for h in range(2):
                s = _dot_nt(qhs[h], k_ref[rows, _head_lanes(h)])
                if diag:
                    s = jnp.where(_causal_mask(), s, NEG)
                m_new = jnp.maximum(ms[h], jnp.max(s, axis=1, keepdims=True))
                p = jnp.exp(s - m_new)
                pb = p.astype(BF16)
                p_scr[slot, h, :, 0:TQ] = pb
                p_scr[slot, h, :, TQ:P_COLS] = two_parts(m_new)
                alphas.append(jnp.exp(ms[h] - m_new))
                ls[h] = alphas[h] * ls[h] + _rowsum(p)
                ms[h] = m_new
                pvs.append(_dot_nn(pb, _masked(masks[h], vb)))
            acc = jnp.where(masks[0], alphas[0], alphas[1]) * acc + pvs[0] + pvs[1]
            return (*ms, *ls, acc)

        def trip(tiles, st, diag=False):
            for kj in tiles:
                @pl.when(number(kj) >= P_SLOTS)
                def _():
                    for cp in copies(kj):
                        cp.wait()
            for kj in tiles:
                st = block(kj, st, diag)
            for kj in tiles:
                for cp in copies(kj):
                    cp.start()
            return st

        neg = jnp.full((TQ, 1), NEG, F32)
        zc = jnp.zeros((TQ, 1), F32)
        st = (neg, neg, zc, zc, jnp.zeros((TQ, LANES), F32))
        st = _over_key_tiles(qi, trip, st, ascending=True)
        m0, m1, l0, l1, acc = trip([qi], st, True)
        o_ref[...] = acc / jnp.where(masks[0], l0, l1)
        lse_ref[0] = m0 + jnp.log(l0)
        lse_ref[1] = m1 + jnp.log(l1)

        @pl.when(jnp.logical_and(pair == HEADS // 2 - 1, qi == nq - 1))
        def _():
            for slot in range(P_SLOTS):
                for h in range(2):
                    pltpu.make_async_copy(p_scr.at[slot, h], p_hbm.at[0, 0, 0], sems.at[2 * slot + h]).wait()

    return pl.pallas_call(
        body, name="mla_fwd", grid=(HEADS // 2, nq),
        out_shape=[jax.ShapeDtypeStruct((s_len, MLA_WIDTH), F32),
                   jax.ShapeDtypeStruct((HEADS, s_len, 1), F32),
                   jax.ShapeDtypeStruct((HEADS, nq, nq, TQ, P_COLS), BF16)],
        in_specs=[pl.BlockSpec((TQ, 2 * LANES), lambda p, i: (i, p)),
                  pl.BlockSpec((s_len, 2 * LANES), lambda p, i: (0, p)),
                  pl.BlockSpec((s_len, LANES), lambda p, i: (0, p))],
        out_specs=[pl.BlockSpec((TQ, LANES), lambda p, i: (i, p)),
                   pl.BlockSpec((2, TQ, 1), lambda p, i: (p, i, 0)),
                   _any_spec()],
        scratch_shapes=[pltpu.VMEM((P_SLOTS, 2, TQ, P_COLS), BF16), pltpu.SemaphoreType.DMA((2 * P_SLOTS,))],
        compiler_params=pltpu.CompilerParams(dimension_semantics=("arbitrary", "arbitrary"),
                                             vmem_limit_bytes=VMEM_BIG),
    )(qm, km, vm)


def _mla_bwd(qm_t, km, vm, o, do, do_t, lse, p_saved):
    s_len = km.shape[0]
    nq = s_len // TQ
    total = (HEADS // 2) * (nq * (nq + 1) // 2)

    def body(qt_ref, k_ref, v_ref, o_ref, do_ref, dot_ref, lse_ref, p_hbm, dq_ref, dk_ref, dv_ref, p_scr, sems):
        pair, qi = pl.program_id(0), pl.program_id(1)

        def number(kj):
            return _tile_number(pair, qi, kj, nq)

        def fetch(p_, q_, k_):
            for h in range(2):
                _p_copy(False, p_hbm, p_scr, sems, p_, q_, k_, nq, h).start()

        def advance(p_, q_, k_):
            row_end = k_ == q_
            last_row = q_ == nq - 1
            return (jnp.where(jnp.logical_and(row_end, last_row), p_ + 1, p_),
                    jnp.where(row_end, jnp.where(last_row, 0, q_ + 1), q_),
                    jnp.where(row_end, 0, k_ + 1))

        @pl.when(jnp.logical_and(pair == 0, qi == 0))
        def _():
            fetch(pair, qi, 0)
            fetch(*advance(pair, qi, 0))

        @pl.when(qi == 0)
        def _():
            dk_ref[...] = jnp.zeros_like(dk_ref)
            dv_ref[...] = jnp.zeros_like(dv_ref)

        do_v = do_ref[...]
        dot_v = dot_ref[...]
        od = o_ref[...] * do_v.astype(F32)
        masks = _head_masks()
        qts = [qt_ref[_head_lanes(h), :] for h in range(2)]
        doms = [_masked(hm, do_v) for hm in masks]
        deltas = [_rowsum(jnp.where(hm, od, 0.0)) for hm in masks]
        lses = [lse_ref[h] for h in range(2)]

        def block(kj, dqs):
            rows = _key_rows(kj)
            slot = number(kj) % P_SLOTS
            vb = v_ref[rows, :]
            dqs = list(dqs)
            dv_t = []
            for h in range(2):
                kb = k_ref[rows, _head_lanes(h)]
                tile_max = _rowsum(p_scr[slot, h, :, TQ:P_COLS].astype(F32)) * (1.0 / 64.0)
                p = p_scr[slot, h, :, 0:TQ].astype(F32) * jnp.exp(tile_max - lses[h])
                dp = _dot_nt(doms[h], vb)
                ds = (p * (dp - deltas[h])).astype(BF16)
                dk_ref[kj, _head_lanes(h), :] += _dot_nn(qts[h], ds)
                dv_t.append(_dot_nn(dot_v, p.astype(BF16)))
                dqs[h] = dqs[h] + _dot_nn(ds, kb)
            dv_ref[kj] += _head_rows(*dv_t)
            return tuple(dqs)

        def trip(tiles, dqs):
            for kj in tiles:
                @pl.when(number(kj) + 2 < total)
                def _():
                    fetch(*advance(*advance(pair, qi, kj)))
            for kj in tiles:
                for h in range(2):
                    _p_copy(False, p_hbm, p_scr, sems, pair, qi, kj, nq, h).wait()
            for kj in tiles:
                dqs = block(kj, dqs)
            return dqs

        za = jnp.zeros((TQ, LANES), F32)
        dqs = _over_key_tiles(qi + 1, trip, (za, za), ascending=True)
        dq_ref[:, _head_lanes(0)] = dqs[0] * MLA_SCALE
        dq_ref[:, _head_lanes(1)] = dqs[1] * MLA_SCALE

    return pl.pallas_call(
        body, name="mla_bwd", grid=(HEADS // 2, nq),
        out_shape=[jax.ShapeDtypeStruct((s_len, HEADS * LANES), F32),
                   jax.ShapeDtypeStruct((nq, HEADS * LANES, TQ), F32),
                   jax.ShapeDtypeStruct((nq, MLA_WIDTH, TQ), F32)],
        in_specs=[pl.BlockSpec((2 * LANES, TQ), lambda p, i: (p, i)),
                  pl.BlockSpec((s_len, 2 * LANES), lambda p, i: (0, p)),
                  pl.BlockSpec((s_len, LANES), lambda p, i: (0, p)),
                  pl.BlockSpec((TQ, LANES), lambda p, i: (i, p)),
                  pl.BlockSpec((TQ, LANES), lambda p, i: (i, p)),
                  pl.BlockSpec((LANES, TQ), lambda p, i: (p, i)),
                  pl.BlockSpec((2, TQ, 1), lambda p, i: (p, i, 0)),
                  _any_spec()],
        out_specs=[pl.BlockSpec((TQ, 2 * LANES), lambda p, i: (i, p)),
                   pl.BlockSpec((nq, 2 * LANES, TQ), lambda p, i: (0, p, 0)),
                   pl.BlockSpec((nq, LANES, TQ), lambda p, i: (0, p, 0))],
        scratch_shapes=[pltpu.VMEM((P_SLOTS, 2, TQ, P_COLS), BF16), pltpu.SemaphoreType.DMA((2 * P_SLOTS,))],
        compiler_params=pltpu.CompilerParams(dimension_semantics=("arbitrary", "arbitrary"),
                                             vmem_limit_bytes=VMEM_BIG),
    )(qm_t, km, vm, o, do, do_t, lse, p_saved)


def _mid(o_sb, g_sb, o_mla, g_mla, x, target, gate, w_out):
    s_len = x.shape[0]

    def body(osb_ref, gsb_ref, omla_ref, gmla_ref, x_ref, t_ref, gate_ref, w_ref,
             dy_ref, dosb_ref, dgsb_ref, domla_ref, dgmla_ref, gw_ref, dgate_ref, loss_ref, dosbt_ref, domlat_ref):
        @pl.when(pl.program_id(0) == 0)
        def _():
            gw_ref[...] = jnp.zeros_like(gw_ref)
            dgate_ref[...] = jnp.zeros_like(dgate_ref)
            loss_ref[...] = jnp.zeros_like(loss_ref)

        o1, g1, o2, g2 = osb_ref[...], gsb_ref[...], omla_ref[...], gmla_ref[...]
        s1, s2 = _sigmoid(g1), _sigmoid(g2)
        mixed = jnp.concatenate([o1 * (g1 * s1), o2 * (g2 * s2)], axis=1).astype(BF16)
        w = w_ref[...]
        gate_v = gate_ref[...]
        u = _dot_nn(mixed, w)
        err = x_ref[...] + gate_v * u - t_ref[...]
        loss_ref[...] += jnp.sum(err * err, axis=0, keepdims=True)
        dy = err * (1.0 / D_MODEL)
        dy_ref[...] = dy
        dgate_ref[...] += jnp.sum(dy * u, axis=0, keepdims=True)
        du = (dy * gate_v).astype(BF16)
        gw_ref[...] += _dot_tn(mixed, du)
        dmixed = _dot_nt(du, w)
        d1, d2 = dmixed[:, :SB_WIDTH], dmixed[:, SB_WIDTH:]
        do1, do2 = d1 * (g1 * s1), d2 * (g2 * s2)
        dosb_ref[...] = do1.astype(BF16)
        dgsb_ref[...] = (d1 * o1 * (s1 * (1.0 + g1 * (1.0 - s1)))).astype(BF16)
        domla_ref[...] = do2.astype(BF16)
        dgmla_ref[...] = (d2 * o2 * (s2 * (1.0 + g2 * (1.0 - s2)))).astype(BF16)
        dosbt_ref[...] = do1.T.astype(BF16)
        domlat_ref[...] = do2.T.astype(BF16)

    def tile(width):
        return pl.BlockSpec((TS, width), lambda i: (i, 0))

    def full(shape):
        return pl.BlockSpec(shape, lambda i: (0, 0))

    return pl.pallas_call(
        body, name="mid", grid=(s_len // TS,),
        out_shape=[jax.ShapeDtypeStruct((s_len, D_MODEL), F32)]
        + [jax.ShapeDtypeStruct((s_len, 512), BF16)] * 4
        + [jax.ShapeDtypeStruct((D_MODEL, D_MODEL), F32),
           jax.ShapeDtypeStruct((1, D_MODEL), F32), jax.ShapeDtypeStruct((1, D_MODEL), F32)]
        + [jax.ShapeDtypeStruct((512, s_len), BF16)] * 2,
        in_specs=[tile(512)] * 4 + [tile(D_MODEL), tile(D_MODEL), full((1, D_MODEL)), full((D_MODEL, D_MODEL))],
        out_specs=[tile(D_MODEL)] + [tile(512)] * 4
        + [full((D_MODEL, D_MODEL)), full((1, D_MODEL)), full((1, D_MODEL))]
        + [pl.BlockSpec((512, TS), lambda i: (0, i))] * 2,
        compiler_params=pltpu.CompilerParams(dimension_semantics=("arbitrary",), vmem_limit_bytes=VMEM_BIG),
    )(o_sb, g_sb, o_mla, g_mla, x, target, gate, w_out)


def _mla_pre_bwd(dq, dk, dv, cq, ckv, kr, qln, wuq, kvln, wukv, qhn, khn, rope_table):
    s_len = cq.shape[0]

    def body(dq_ref, dk_ref, dv_ref, cq_ref, ckv_ref, kr_ref, qln_ref, wuq_ref, kvln_ref, wukv_ref,
             qhn_ref, khn_ref, rt_ref,
             dcq_ref, dckv_ref, dkr_ref, gwuq_ref, gwukv_ref, gqhn_ref, gkhn_ref, gqln_ref, gkvln_ref,
             dqa_ref, dkv_ref):
        @pl.when(pl.program_id(0) == 0)
        def _():
            for r_ in (gwuq_ref, gwukv_ref, gqhn_ref, gkhn_ref, gqln_ref, gkvln_ref):
                r_[...] = jnp.zeros_like(r_)

        cq, ckv = cq_ref[...], ckv_ref[...]
        qln_v, kvln_v = qln_ref[...], kvln_ref[...]
        wuq_v, wukv_v = wuq_ref[...], wukv_ref[...]
        rq, rkv, cq_hat, ckv_hat, cqn, ckvn, q_all, kv, kr64 = _mla_latents(
            cq, ckv, kr_ref[...], qln_v, kvln_v, wuq_v, wukv_v)
        cosf, sa, sb = _expand_rope(rt_ref[...])
        qhn_v, khn_v = qhn_ref[...], khn_ref[...]
        lane = _lane_iota((TS, LANES))
        low = lane < NOPE
        blks = [slice(LANES * hd, LANES * (hd + 1)) for hd in range(HEADS)]
        raw = [q_all[:, b] for b in blks] + [jnp.where(low, kv[:, b], kr64) for b in blks]
        grads = [dq_ref[:, b] for b in blks] + [dk_ref[:, b] for b in blks]
        gains = [qhn_v] * HEADS + [khn_v] * HEADS
        rms = [_head_rms(t) for t in raw]
        xhs = [t * r for t, r in zip(raw, rms)]
        dns = [_rope_t(d, cosf, sa, sb) for d in grads]
        gain_g = [jnp.sum(dn * xh, axis=0, keepdims=True) for dn, xh in zip(dns, xhs)]
        dxs = [_rms_bwd(dn * g, xh, r, MLA_QK) for dn, g, xh, r in zip(dns, gains, xhs, rms)]
        dkr64 = jnp.zeros((TS, LANES), F32)
        for hd, b in enumerate(blks):
            dqa_ref[:, b] = dxs[hd].astype(BF16)
            dkb = dxs[HEADS + hd]
            dkr64 = dkr64 + jnp.where(low, 0.0, dkb)
            dvp = dv_ref[:, LANES * (hd // 2):LANES * (hd // 2 + 1)]
            dvh = pltpu.roll(dvp, 64, 1) if hd % 2 == 0 else dvp
            dkv_ref[:, b] = jnp.where(low, dkb, dvh).astype(BF16)
        gqhn_ref[...] += sum(gain_g[:HEADS])
        gkhn_ref[...] += sum(gain_g[HEADS:])
        dkr_ref[...] = pltpu.roll(dkr64, 64, 1).astype(BF16)

        dqa = dqa_ref[...]
        gwuq_ref[...] += _dot_tn(cqn, dqa)
        dcqn = _dot_nt(dqa, wuq_v)
        gqln_ref[...] += jnp.sum(dcqn * cq_hat, axis=0, keepdims=True)
        dcq_ref[...] = _rms_bwd(dcqn * qln_v, cq_hat, rq, Q_LORA).astype(BF16)

        dkv = dkv_ref[...]
        gwukv_ref[...] += _dot_tn(ckvn, dkv)
        dckvn = _dot_nt(dkv, wukv_v)
        gkvln_ref[...] += jnp.sum(dckvn * ckv_hat, axis=0, keepdims=True)
        dckv_ref[...] = _rms_bwd(dckvn * kvln_v, ckv_hat, rkv, KV_LORA).astype(BF16)

    def tile(width):
        return pl.BlockSpec((TS, width), lambda i: (i, 0))

    def full(shape):
        return pl.BlockSpec(shape, lambda i: (0, 0))

    acc_shapes = [(Q_LORA, 1024), (KV_LORA, 1024), (1, LANES), (1, LANES), (1, Q_LORA), (1, KV_LORA)]
    return pl.pallas_call(
        body, name="mla_pre_bwd", grid=(s_len // TS,),
        out_shape=[jax.ShapeDtypeStruct((s_len, Q_LORA), BF16), jax.ShapeDtypeStruct((s_len, KV_LORA), BF16),
                   jax.ShapeDtypeStruct((s_len, LANES), BF16)]
        + [jax.ShapeDtypeStruct(s, F32) for s in acc_shapes],
        in_specs=[tile(1024), tile(1024), tile(512), tile(Q_LORA), tile(KV_LORA), tile(LANES),
                  full(qln.shape), full(wuq.shape), full(kvln.shape), full(wukv.shape),
                  full(qhn.shape), full(khn.shape), tile(LANES)],
        out_specs=[tile(Q_LORA), tile(KV_LORA), tile(LANES)] + [full(s) for s in acc_shapes],
        scratch_shapes=[pltpu.VMEM((TS, 1024), BF16), pltpu.VMEM((TS, 1024), BF16)],
        compiler_params=pltpu.CompilerParams(dimension_semantics=("arbitrary",), vmem_limit_bytes=VMEM_BIG),
    )(dq, dk, dv, cq, ckv, kr, qln, wuq, kvln, wukv, qhn, khn, rope_table)


def _dproj_bwd(dq_sb, dk_sb, dv_sb, dg_sb, dcq, dckv, dg_mla, dkr, w_in_r, x, dy, norm_w, scale):
    s_len = x.shape[0]

    def body(dq_ref, dk_ref, dv_ref, dg_ref, dcq_ref, dckv_ref, dgm_ref, dkr_ref, w_ref, x_ref, dy_ref,
             nw_ref, scale_ref, dp_ref, gx_ref, dshift_ref, dscale_ref, dnw_ref):
        @pl.when(pl.program_id(0) == 0)
        def _():
            for r_ in (dshift_ref, dscale_ref, dnw_ref):
                r_[...] = jnp.zeros_like(r_)

        dp_ref[:, 0:512] = dq_ref[...].astype(BF16)
        dp_ref[:, 512:1024] = dk_ref[...].astype(BF16)
        dp_ref[:, 1024:1536] = dv_ref[...].astype(BF16)
        dp_ref[:, 1536:2048] = dg_ref[...]
        dp_ref[:, 2048:2432] = dcq_ref[...]
        dp_ref[:, 2432:2688] = dckv_ref[...]
        dp_ref[:, 2688:3200] = dgm_ref[...]
        dp_ref[:, 3200:3328] = dkr_ref[...]
        dh = _dot_nt(dp_ref[...], w_ref[...])
        xv = x_ref[...]
        r = lax.rsqrt(jnp.mean(xv * xv, axis=1, keepdims=True) + EPS)
        xh = xv * r
        nw = nw_ref[...]
        dshift_ref[...] += jnp.sum(dh, axis=0, keepdims=True)
        dscale_ref[...] += jnp.sum(dh * (xh * nw), axis=0, keepdims=True)
        dxnw = dh * (1.0 + scale_ref[...])
        dnw_ref[...] += jnp.sum(dxnw * xh, axis=0, keepdims=True)
        gx_ref[...] = dy_ref[...] + _rms_bwd(dxnw * nw, xh, r, D_MODEL)

    def tile(width):
        return pl.BlockSpec((TS, width), lambda i: (i, 0))

    def full(shape):
        return pl.BlockSpec(shape, lambda i: (0, 0))

    vec = (1, D_MODEL)
    return pl.pallas_call(
        body, name="dproj_bwd", grid=(s_len // TS,),
        out_shape=[jax.ShapeDtypeStruct((s_len, IN_COLS_R), BF16), jax.ShapeDtypeStruct((s_len, D_MODEL), F32)]
        + [jax.ShapeDtypeStruct(vec, F32)] * 3,
        in_specs=[tile(512)] * 4 + [tile(Q_LORA), tile(KV_LORA), tile(512), tile(LANES),
                                    full(w_in_r.shape), tile(D_MODEL), tile(D_MODEL), full(vec), full(vec)],
        out_specs=[tile(IN_COLS_R), tile(D_MODEL)] + [full(vec)] * 3,
        compiler_params=pltpu.CompilerParams(dimension_semantics=("arbitrary",), vmem_limit_bytes=VMEM_BIG),
    )(dq_sb, dk_sb, dv_sb, dg_sb, dcq, dckv, dg_mla, dkr, w_in_r, x, dy, norm_w, scale)


def _grad_w_in(hb, dproj):
    s_len = hb.shape[0]
    n_half = IN_COLS_R // 2
    n_steps = s_len // TN_S

    def body(h_ref, d_ref, g_ref, acc_ref):
        @pl.when(pl.program_id(1) == 0)
        def _():
            acc_ref[...] = jnp.zeros_like(acc_ref)

        acc_ref[...] += _dot_tn(h_ref[...], d_ref[...])

        @pl.when(pl.program_id(1) == n_steps - 1)
        def _():
            g_ref[...] = acc_ref[...].astype(BF16)

    return pl.pallas_call(
        body, name="grad_w_in", grid=(2, n_steps),
        out_shape=jax.ShapeDtypeStruct((D_MODEL, IN_COLS_R), BF16),
        in_specs=[pl.BlockSpec((TN_S, D_MODEL), lambda n, s: (s, 0)),
                  pl.BlockSpec((TN_S, n_half), lambda n, s: (s, n))],
        out_specs=pl.BlockSpec((D_MODEL, n_half), lambda n, s: (0, n)),
        scratch_shapes=[pltpu.VMEM((D_MODEL, n_half), F32)],
        compiler_params=pltpu.CompilerParams(dimension_semantics=("parallel", "arbitrary"),
                                             vmem_limit_bytes=VMEM_BIG),
    )(hb, dproj)


def _final_exchange(gpack, ccol, wpack, mpack, vpack, n_sh, grads):
    n = len(grads)

    def body(*refs):
        (g_ref, cc_ref, wp_ref, mp_ref, vp_ref) = refs[:5]
        slabs_in = refs[5:5 + n]
        (og_ref, od_ref, om_ref, ov_ref, ag_ref) = refs[5 + n:10 + n]
        slabs_out = refs[10 + n:10 + 2 * n]
        gall_ref, ssem, rsem, slab_ssem, slab_rsem, lsem = refs[10 + 2 * n:]
        pos = _mesh_pos()
        me = _lin(pos)
        gall_ref[me] = g_ref[...]
        small = _all_gather_start(pos, g_ref, gall_ref, ssem, rsem, 0)
        own = [pltpu.make_async_copy(slabs_in[a].at[me], slabs_out[a].at[me], lsem.at[a]) for a in range(n)]
        for cp in own:
            cp.start()
        in_flight = [_all_to_all_start(pos, slabs_in[a], slabs_out[a], slab_ssem, slab_rsem, a * (N_DEV - 1))
                     for a in range(n)]
        _all_gather_wait(pos, g_ref, gall_ref, ssem, rsem, 0, small)

        tot = gall_ref[0]
        for j in range(1, N_DEV):
            tot = tot + gall_ref[j]
        og_ref[...] = tot
        od_ref[...], om_ref[...], ov_ref[...] = _adamw(wp_ref[...], tot, mp_ref[...], vp_ref[...])

        ga = jnp.zeros((D_MODEL, n_sh), F32)
        for j in range(N_DEV):
            d_mine = jnp.zeros((8, n_sh), F32)
            for k in range(N_DEV):
                d_mine = d_mine + jnp.where(me == k, gall_ref[j, :, PK_ADA + n_sh * k:PK_ADA + n_sh * (k + 1)], 0.0)
            col = _silu(cc_ref[j])
            ga = ga + jnp.concatenate(
                [col * d_mine[0:1, LANES * a:LANES * (a + 1)] for a in range(n_sh // LANES)], axis=1)
        ag_ref[...] = ga

        for a in range(n):
            _all_to_all_wait(pos, slabs_in[a], slabs_out[a], slab_ssem, slab_rsem, a * (N_DEV - 1), in_flight[a])
        for cp in own:
            cp.wait()

    pk = jax.ShapeDtypeStruct((8, PK_END), F32)
    ada = jax.ShapeDtypeStruct((D_MODEL, n_sh), F32)
    return pl.pallas_call(
        body, name="final_exchange",
        out_shape=[pk] * 4 + [ada] + [jax.ShapeDtypeStruct(g.shape, g.dtype) for g in grads],
        in_specs=[_vmem_spec()] * 5 + [_any_spec()] * n,
        out_specs=[_vmem_spec()] * 5 + [_any_spec()] * n,
        scratch_shapes=[
            pltpu.VMEM((N_DEV, 8, PK_END), F32),
            pltpu.SemaphoreType.DMA((N_DEV - 1,)),
            pltpu.SemaphoreType.DMA((N_DEV - 1,)),
            pltpu.SemaphoreType.DMA((n * (N_DEV - 1),)),
            pltpu.SemaphoreType.DMA((n * (N_DEV - 1),)),
            pltpu.SemaphoreType.DMA((n,)),
        ],
        compiler_params=pltpu.CompilerParams(vmem_limit_bytes=VMEM_BIG),
    )(gpack, ccol, wpack, mpack, vpack, *grads)


def _adamw_reduce(name, parts, w, m, v, row_tile):
    rows, cols = w.shape
    n_parts = parts.shape[0]

    def body(p_ref, w_ref, m_ref, v_ref, g_ref, d_ref, mo_ref, vo_ref):
        g = p_ref[0].astype(F32)
        for j in range(1, n_parts):
            g = g + p_ref[j].astype(F32)
        g_ref[...] = g
        d_ref[...], mo_ref[...], vo_ref[...] = _adamw(w_ref[...], g, m_ref[...], v_ref[...])

    tile = pl.BlockSpec((row_tile, cols), lambda i: (i, 0))
    return pl.pallas_call(
        body, name=name, grid=(rows // row_tile,),
        out_shape=[jax.ShapeDtypeStruct((rows, cols), F32)] * 4,
        in_specs=[pl.BlockSpec((n_parts, row_tile, cols), lambda i: (0, i, 0)), tile, tile, tile],
        out_specs=[tile] * 4,
        compiler_params=pltpu.CompilerParams(dimension_semantics=("parallel",), vmem_limit_bytes=VMEM_BIG),
    )(parts, w, m, v)


def _rope_table(positions):
    inv_freq = 10000.0 ** (-jnp.arange(0, ROPE, 2, dtype=F32) / ROPE)
    ang = positions.astype(F32)[:, None] * inv_freq
    cos, sin = jnp.cos(ang), jnp.sin(ang)
    pad = jnp.zeros((positions.shape[0], LANES - 2 * ROPE), F32)
    return jnp.concatenate([cos, cos, -sin, sin, pad], axis=1)


def _rearrange_cols(w):
    pad = jnp.zeros((w.shape[0], IN_COLS_R - IN_COLS), w.dtype)
    return jnp.concatenate([w[:, :2688], w[:, 2720:3232], w[:, 2688:2720], pad], axis=1)


def _restore_cols(g):
    return jnp.concatenate([g[:, :2688], g[:, 3200:3232], g[:, 2688:3200]], axis=1)


def _pad_heads(w):
    rows = w.shape[0]
    w = w.reshape(rows, HEADS, MLA_QK)
    return jnp.pad(w, ((0, 0), (0, 0), (0, LANES - MLA_QK))).reshape(rows, HEADS * LANES)


def _unpad_heads(g):
    rows = g.shape[0]
    return g.reshape(rows, HEADS, LANES)[:, :, :MLA_QK].reshape(rows, HEADS * MLA_QK)


def _pad_lanes(v):
    return jnp.pad(v, ((0, 0), (0, LANES - v.shape[1])))


def _col_shards(g):
    rows = g.shape[0]
    return g.reshape(rows, N_DEV, g.shape[1] // N_DEV).transpose(1, 0, 2)


def _from_col_shards(g):
    return g.transpose(1, 0, 2).reshape(g.shape[1], N_DEV * g.shape[2])


def _pack(norm_w, qln, kvln, qhn, khn, ada, loss_lanes=None):
    if loss_lanes is None:
        loss_lanes = jnp.zeros((1, PK_END - PK_LOSS), F32)
    row = jnp.concatenate([norm_w, qln, kvln, _pad_lanes(qhn), _pad_lanes(khn), ada, loss_lanes], axis=1)
    return jnp.broadcast_to(row, (8, PK_END))


def _unpack(p):
    row = p[0:1]
    return (row[:, PK_NORM:PK_QLN], row[:, PK_QLN:PK_KVLN], row[:, PK_KVLN:PK_QHN],
            row[:, PK_QHN:PK_QHN + MLA_QK], row[:, PK_KHN:PK_KHN + MLA_QK], row[:, PK_ADA:PK_LOSS])


def kernel(x, c, positions, w_ada, b_ada, norm_w, w_in, q_lora_norm, w_uq, kv_lora_norm, w_ukv, q_head_norm, k_head_norm, w_out, loss_target, m_w_ada, m_b_ada, m_norm_w, m_w_in, m_q_lora_norm, m_w_uq, m_kv_lora_norm, m_w_ukv, m_q_head_norm, m_k_head_norm, m_w_out, v_w_ada, v_b_ada, v_norm_w, v_w_in, v_q_lora_norm, v_w_uq, v_kv_lora_norm, v_w_ukv, v_q_head_norm, v_k_head_norm, v_w_out):
    s_len = x.shape[1]
    x2 = x.reshape(s_len, D_MODEL)
    tgt = loss_target.reshape(s_len, D_MODEL)
    w_ada_s, w_in_s, w_uq_s, w_ukv_s, w_out_s = w_ada[0], w_in[0], w_uq[0], w_ukv[0], w_out[0]

    ada8, c_all = _ada_fwd(jnp.broadcast_to(c, (8, D_MODEL)), w_ada_s, b_ada.reshape(N_DEV, -1))
    ada = ada8.reshape(1, 3 * D_MODEL)
    shift, scale, gate = ada[:, :D_MODEL], ada[:, D_MODEL:2 * D_MODEL], ada[:, 2 * D_MODEL:]

    g_uq, g_ukv, g_out, g_in = _gather_weights([w_uq_s, w_ukv_s, w_out_s, w_in_s])
    w_in_r = _rearrange_cols(_from_col_shards(g_in))
    wuq_p = _pad_heads(_from_col_shards(g_uq))
    wukv_f = _from_col_shards(g_ukv)
    w_out_f = g_out.reshape(D_MODEL, D_MODEL)

    rope_table = _rope_table(positions[0])
    qhn_p, khn_p = _pad_lanes(q_head_norm), _pad_lanes(k_head_norm)

    hb, qkv, g_sb, cq, ckv, g_mla, kr, qm, km, vm, q_sb_t, qm_t = _fwd_pre(
        x2, shift, scale, norm_w, w_in_r, q_lora_norm, wuq_p, kv_lora_norm, wukv_f, qhn_p, khn_p, rope_table)
    o_sb, w_saved, l_saved = _sb_fwd(qkv)
    o_mla, lse, p_saved = _mla_fwd(qm, km, vm)

    dy, do_sb, dg_sb, do_mla, dg_mla, gw_out, d_gate, loss_acc, do_sb_t, do_mla_t = _mid(
        o_sb, g_sb, o_mla, g_mla, x2, tgt, gate, w_out_f)

    dq_sb, dk_sb_t, dv_sb_t = _sb_bwd(qkv, q_sb_t, do_sb, do_sb_t, w_saved, l_saved)
    dk_sb, dv_sb = _from_key_tiles(dk_sb_t), _from_key_tiles(dv_sb_t)
    dq_m, dk_m_t, dv_m_t = _mla_bwd(qm_t, km, vm, o_mla, do_mla, do_mla_t, lse, p_saved)
    dk_m, dv_m = _from_key_tiles(dk_m_t), _from_key_tiles(dv_m_t)
    dcq, dckv, dkr, gw_uq_p, gw_ukv, g_qhn, g_khn, g_qln, g_kvln = _mla_pre_bwd(
        dq_m, dk_m, dv_m, cq, ckv, kr, q_lora_norm, wuq_p, kv_lora_norm, wukv_f, qhn_p, khn_p, rope_table)
    dproj, grad_x, d_shift, d_scale, g_norm_w = _dproj_bwd(
        dq_sb, dk_sb, dv_sb, dg_sb, dcq, dckv, dg_mla, dkr, w_in_r, x2, dy, norm_w, scale)
    gw_in = _restore_cols(_grad_w_in(hb, dproj))

    d_ada = jnp.concatenate([d_shift, d_scale, d_gate], axis=1)
    gpack = _pack(g_norm_w, g_qln, g_kvln, g_qhn[:, :MLA_QK], g_khn[:, :MLA_QK], d_ada, loss_acc)
    wpack = _pack(norm_w, q_lora_norm, kv_lora_norm, q_head_norm, k_head_norm, b_ada)
    mpack = _pack(m_norm_w, m_q_lora_norm, m_kv_lora_norm, m_q_head_norm, m_k_head_norm, m_b_ada)
    vpack = _pack(v_norm_w, v_q_lora_norm, v_kv_lora_norm, v_q_head_norm, v_k_head_norm, v_b_ada)
    ccol = jnp.broadcast_to(c_all[:, :, None], (N_DEV, D_MODEL, LANES))
    slabs = [g.astype(BF16) for g in (
        _col_shards(gw_in), _col_shards(_unpad_heads(gw_uq_p)), _col_shards(gw_ukv),
        gw_out.reshape(N_DEV, D_MODEL // N_DEV, D_MODEL))]
    pg, pd, pm, pv, gw_ada, r_in, r_uq, r_ukv, r_out = _final_exchange(
        gpack, ccol, wpack, mpack, vpack, w_ada_s.shape[1], slabs)
    loss = 0.5 * jnp.sum(pg[0, PK_LOSS:PK_END]) / D_MODEL

    ada_g, ada_d, ada_m, ada_v = _adamw_reduce("adamw_w_ada", gw_ada[None], w_ada_s, m_w_ada[0], v_w_ada[0], 256)
    in_g, in_d, in_m, in_v = _adamw_reduce("adamw_w_in", r_in, w_in_s, m_w_in[0], v_w_in[0], 256)
    uq_g, uq_d, uq_m, uq_v = _adamw_reduce("adamw_w_uq", r_uq, w_uq_s, m_w_uq[0], v_w_uq[0], w_uq_s.shape[0])
    ukv_g, ukv_d, ukv_m, ukv_v = _adamw_reduce(
        "adamw_w_ukv", r_ukv, w_ukv_s, m_w_ukv[0], v_w_ukv[0], w_ukv_s.shape[0])
    out_g, out_d, out_m, out_v = _adamw_reduce(
        "adamw_w_out", r_out, w_out_s, m_w_out[0], v_w_out[0], w_out_s.shape[0])

    def group(ada_t, pk, in_t, uq_t, ukv_t, out_t):
        nw, qln, kvln, qhn, khn, b = _unpack(pk)
        return (ada_t[None], b, nw, in_t[None], qln, uq_t[None], kvln, ukv_t[None], qhn, khn, out_t[None])

    return (loss, grad_x.reshape(1, s_len, D_MODEL),
            *group(ada_g, pg, in_g, uq_g, ukv_g, out_g),
            *group(ada_d, pd, in_d, uq_d, ukv_d, out_d),
            *group(ada_m, pm, in_m, uq_m, ukv_m, out_m),
            *group(ada_v, pv, in_v, uq_v, ukv_v, out_v))
```

```python
import functools
import math

import jax
import jax.numpy as jnp
from jax import lax
from jax.experimental import pallas as pl
from jax.experimental.pallas import tpu as pltpu

F32 = jnp.float32
BF16 = jnp.bfloat16

N_DEV = 8
D_MODEL = 1024
HEADS = 8
SB_WIDTH = 512
MLA_WIDTH = 512
Q_LORA = 384
KV_LORA = 256
ROPE = 32
NOPE = 64
MLA_QK = 96
LANES = 128
IN_COLS = 3232
IN_COLS_R = 3328
EPS = 1e-6
NEG = -1e30

ADAM_LR = 0.001
ADAM_B1 = 0.9
ADAM_B2 = 0.999
ADAM_EPS = 1e-08
ADAM_WD = 0.01
ADAM_STEP = 10

TS = 512
TS_FWD = 256
TQ = 512
KEY_UNROLL = 3
TN_S = 512
VMEM_BIG = 56 * 1024 * 1024

PK_NORM, PK_QLN, PK_KVLN, PK_QHN, PK_KHN, PK_ADA, PK_LOSS, PK_END = 0, 1024, 1408, 1664, 1792, 1920, 4992, 6016

MESH_ID = pl.DeviceIdType.MESH


def _dot_nn(a, b):
    return lax.dot_general(a, b, (((1,), (0,)), ((), ())), preferred_element_type=F32)


def _dot_nt(a, b):
    return lax.dot_general(a, b, (((1,), (1,)), ((), ())), preferred_element_type=F32)


def _dot_tn(a, b):
    return lax.dot_general(a, b, (((0,), (0,)), ((), ())), preferred_element_type=F32)


def _split_bf16(a):
    hi = a.astype(BF16)
    lo = (a - hi.astype(F32)).astype(BF16)
    return hi, lo


def _dot3(a, b):
    ah, al = _split_bf16(a)
    bh, bl = _split_bf16(b)
    return _dot_nn(ah, bh) + _dot_nn(ah, bl) + _dot_nn(al, bh)


def _sigmoid(g):
    return 1.0 / (1.0 + jnp.exp(-g))


def _silu(g):
    return g * _sigmoid(g)


def _lane_iota(shape):
    return lax.broadcasted_iota(jnp.int32, shape, len(shape) - 1)


def _adamw(w, g, m, v):
    m = ADAM_B1 * m + (1.0 - ADAM_B1) * g
    v = ADAM_B2 * v + (1.0 - ADAM_B2) * (g * g)
    m_hat = m / (1.0 - ADAM_B1 ** ADAM_STEP)
    v_hat = v / (1.0 - ADAM_B2 ** ADAM_STEP)
    delta = -ADAM_LR * (m_hat / (jnp.sqrt(v_hat) + ADAM_EPS) + ADAM_WD * w)
    return delta, m, v


def _mesh_pos():
    return lax.axis_index("x"), lax.axis_index("y"), lax.axis_index("c")


def _peer(pos, k):
    x, y, c = pos
    return (1 - x if k & 4 else x, 1 - y if k & 2 else y, 1 - c if k & 1 else c)


def _lin(pos):
    return 4 * pos[0] + 2 * pos[1] + pos[2]


def _remote(src, dst, send_sems, recv_sems, idx, peer):
    return pltpu.make_async_remote_copy(
        src_ref=src, dst_ref=dst, send_sem=send_sems.at[idx], recv_sem=recv_sems.at[idx],
        device_id=peer, device_id_type=MESH_ID)


def _all_gather_start(pos, src, buf, send_sems, recv_sems, base):
    me = _lin(pos)
    sent = []
    for k in range(1, N_DEV):
        cp = _remote(src, buf.at[me], send_sems, recv_sems, base + k - 1, _peer(pos, k))
        cp.start()
        sent.append(cp)
    return sent


def _all_gather_wait(pos, src, buf, send_sems, recv_sems, base, sent):
    for k in range(1, N_DEV):
        peer = _peer(pos, k)
        _remote(src, buf.at[_lin(peer)], send_sems, recv_sems, base + k - 1, peer).wait_recv()
    for cp in sent:
        cp.wait_send()


def _all_gather(pos, src, buf, send_sems, recv_sems, base):
    sent = _all_gather_start(pos, src, buf, send_sems, recv_sems, base)
    _all_gather_wait(pos, src, buf, send_sems, recv_sems, base, sent)


def _all_to_all_start(pos, src, buf, send_sems, recv_sems, base):
    me = _lin(pos)
    sent = []
    for k in range(1, N_DEV):
        peer = _peer(pos, k)
        cp = _remote(src.at[_lin(peer)], buf.at[me], send_sems, recv_sems, base + k - 1, peer)
        cp.start()
        sent.append(cp)
    return sent


def _all_to_all_wait(pos, src, buf, send_sems, recv_sems, base, sent):
    me = _lin(pos)
    for k in range(1, N_DEV):
        peer = _peer(pos, k)
        _remote(src.at[me], buf.at[_lin(peer)], send_sems, recv_sems, base + k - 1, peer).wait_recv()
    for cp in sent:
        cp.wait_send()


def _all_to_all(pos, src, buf, send_sems, recv_sems, base):
    sent = _all_to_all_start(pos, src, buf, send_sems, recv_sems, base)
    _all_to_all_wait(pos, src, buf, send_sems, recv_sems, base, sent)


def _two_level_gather(pos, bufs, send_sems, recv_sems):
    x, y, c = pos
    me, sibling = (x, y, c), (x, y, 1 - c)
    chips = [(1 - x, y), (x, 1 - y), (1 - x, 1 - y)]

    def copy(a, k, block, to):
        slot = bufs[a].at[_lin(block)]
        return _remote(slot, slot, send_sems, recv_sems, 7 * a + k, to)

    started = []
    for a in range(len(bufs)):
        first = [copy(a, 0, me, sibling)] + [copy(a, 1 + j, me, (*chip, c)) for j, chip in enumerate(chips)]
        for cp in first:
            cp.start()
        started += first
    for a in range(len(bufs)):
        for j, chip in enumerate(chips):
            copy(a, 1 + j, (*chip, c), me).wait_recv()
            passed = copy(a, 4 + j, (*chip, c), sibling)
            passed.start()
            started.append(passed)
    for a in range(len(bufs)):
        copy(a, 0, sibling, me).wait_recv()
        for j, chip in enumerate(chips):
            copy(a, 4 + j, (*chip, 1 - c), me).wait_recv()
    for cp in started:
        cp.wait_send()


def _vmem_spec():
    return pl.BlockSpec(memory_space=pltpu.VMEM)


def _any_spec():
    return pl.BlockSpec(memory_space=pl.ANY)


def _row_select(slots, n):
    r = lax.broadcasted_iota(jnp.int32, (N_DEV, n), 0)
    out = jnp.zeros((N_DEV, n), F32)
    for j in range(N_DEV):
        out = out + jnp.where(r == j, slots[j], 0.0)
    return out


def _ada_fwd(c8, w_ada, b_ada8):
    n_sh = w_ada.shape[1]

    def body(c_ref, w_ref, b_ref, out_ref, call_out_ref, call_ref, psend_ref, precv_ref, ssem, rsem):
        pos = _mesh_pos()
        me = _lin(pos)
        call_ref[me] = c_ref[...]
        _all_gather(pos, c_ref, call_ref, ssem, rsem, 0)
        call_out_ref[...] = _row_select([call_ref[j] for j in range(N_DEV)], D_MODEL)
        w = w_ref[...]
        for j in range(N_DEV):
            psend_ref[j] = _dot3(_silu(call_ref[j]), w)
        precv_ref[me] = psend_ref[me]
        _all_to_all(pos, psend_ref, precv_ref, ssem, rsem, N_DEV - 1)
        out_ref[...] = _row_select([precv_ref[j] for j in range(N_DEV)], n_sh) + b_ref[...]

    return pl.pallas_call(
        body, name="ada_fwd",
        out_shape=[jax.ShapeDtypeStruct((N_DEV, n_sh), F32), jax.ShapeDtypeStruct((N_DEV, D_MODEL), F32)],
        in_specs=[_vmem_spec()] * 3, out_specs=[_vmem_spec()] * 2,
        scratch_shapes=[
            pltpu.VMEM((N_DEV, 8, D_MODEL), F32),
            pltpu.VMEM((N_DEV, 8, n_sh), F32),
            pltpu.VMEM((N_DEV, 8, n_sh), F32),
            pltpu.SemaphoreType.DMA((2 * (N_DEV - 1),)),
            pltpu.SemaphoreType.DMA((2 * (N_DEV - 1),)),
        ],
    )(c8, w_ada, b_ada8)


def _gather_weights(shards):
    n = len(shards)

    def body(*refs):
        ins, outs = refs[:n], refs[n:2 * n]
        ssem, rsem = refs[2 * n], refs[2 * n + 1]
        pos = _mesh_pos()
        me = _lin(pos)
        for a in range(n):
            outs[a][me] = ins[a][...].astype(BF16)
        _two_level_gather(pos, outs, ssem, rsem)

    return pl.pallas_call(
        body, name="gather_weights",
        out_shape=[jax.ShapeDtypeStruct((N_DEV,) + s.shape, BF16) for s in shards],
        in_specs=[_vmem_spec()] * n, out_specs=[_vmem_spec()] * n,
        scratch_shapes=[
            pltpu.SemaphoreType.DMA((n * (N_DEV - 1),)),
            pltpu.SemaphoreType.DMA((n * (N_DEV - 1),)),
        ],
        compiler_params=pltpu.CompilerParams(vmem_limit_bytes=VMEM_BIG),
    )(*shards)


def _rope(t, cosf, sin_a, sin_b):
    return t * cosf + pltpu.roll(t, 112, 1) * sin_a + pltpu.roll(t, 16, 1) * sin_b


def _expand_rope(table):
    lane = _lane_iota(table.shape)
    by64, by32 = pltpu.roll(table, 64, 1), pltpu.roll(table, 32, 1)
    cosf = jnp.where(jnp.logical_and(lane >= NOPE, lane < MLA_QK), by64, 1.0)
    sin_a = jnp.where(jnp.logical_and(lane >= NOPE, lane < NOPE + ROPE // 2), by32, 0.0)
    sin_b = jnp.where(jnp.logical_and(lane >= NOPE + ROPE // 2, lane < MLA_QK), by32, 0.0)
    return cosf, sin_a, sin_b


def _rope_t(d, cosf, sin_a, sin_b):
    return d * cosf + pltpu.roll(d * sin_a, 16, 1) + pltpu.roll(d * sin_b, 112, 1)


def _head_rms(t):
    return lax.rsqrt(jnp.sum(t * t, axis=1, keepdims=True) * (1.0 / MLA_QK) + EPS)


def _rms_bwd(dxhat_w, xhat, r, n):
    return r * (dxhat_w - xhat * (jnp.sum(dxhat_w * xhat, axis=1, keepdims=True) * (1.0 / n)))


def _mla_latents(cq, ckv, kr, qln, kvln, wuq, wukv):
    rq = lax.rsqrt(jnp.mean(cq * cq, axis=1, keepdims=True) + EPS)
    rkv = lax.rsqrt(jnp.mean(ckv * ckv, axis=1, keepdims=True) + EPS)
    cq_hat = cq * rq
    ckv_hat = ckv * rkv
    cqn = (cq_hat * qln).astype(BF16)
    ckvn = (ckv_hat * kvln).astype(BF16)
    q_all = _dot_nn(cqn, wuq)
    kv = _dot_nn(ckvn, wukv)
    kr64 = pltpu.roll(kr, 64, 1)
    return rq, rkv, cq_hat, ckv_hat, cqn, ckvn, q_all, kv, kr64


def _fwd_pre(x, shift, scale, norm_w, w_in_r, qln, wuq, kvln, wukv, qhn, khn, rope_table):
    s_len = x.shape[0]

    def body(x_ref, shift_ref, scale_ref, nw_ref, w_ref, qln_ref, wuq_ref, kvln_ref, wukv_ref,
             qhn_ref, khn_ref, rt_ref,
             hb_ref, qkv_ref, gsb_ref, cq_ref, ckv_ref, gmla_ref, kr_ref, qm_ref, km_ref, vm_ref,
             qsbt_ref, qmt_ref):
        xv = x_ref[...]
        r = lax.rsqrt(jnp.mean(xv * xv, axis=1, keepdims=True) + EPS)
        h = (xv * r) * nw_ref[...] * (1.0 + scale_ref[...]) + shift_ref[...]
        hb = h.astype(BF16)
        hb_ref[...] = hb
        qkv = _dot_nn(hb, w_ref[:, 0:1536])
        qkv_ref[...] = qkv.astype(BF16)
        qsbt_ref[...] = qkv[:, :SB_WIDTH].T.astype(BF16)
        gsb_ref[...] = _dot_nn(hb, w_ref[:, 1536:2048])
        cq = _dot_nn(hb, w_ref[:, 2048:2432])
        ckv = _dot_nn(hb, w_ref[:, 2432:2688])
        gmla_ref[...] = _dot_nn(hb, w_ref[:, 2688:3200])
        kr = _dot_nn(hb, w_ref[:, 3200:3328])
        cq_ref[...] = cq
        ckv_ref[...] = ckv
        kr_ref[...] = kr
        _, _, _, _, _, _, q_all, kv, kr64 = _mla_latents(
            cq, ckv, kr, qln_ref[...], kvln_ref[...], wuq_ref[...], wukv_ref[...])
        cosf, sa, sb = _expand_rope(rt_ref[...])
        qhn_v, khn_v = qhn_ref[...], khn_ref[...]
        low = _lane_iota((TS_FWD, LANES)) < NOPE
        blks = [slice(LANES * hd, LANES * (hd + 1)) for hd in range(HEADS)]
        q_raw = [q_all[:, b] for b in blks]
        k_raw = [jnp.where(low, kv[:, b], kr64) for b in blks]
        q_rms = [_head_rms(t) for t in q_raw]
        k_rms = [_head_rms(t) for t in k_raw]
        q_n = [t * r * (qhn_v * MLA_SCALE) for t, r in zip(q_raw, q_rms)]
        q_roped = [_rope(t, cosf, sa, sb) for t in q_n]
        kr_roped = _rope(kr64 * khn_v, cosf, sa, sb)
        k_roped = [jnp.where(low, t * khn_v, kr_roped) * r for t, r in zip(k_raw, k_rms)]
        for hd, b in enumerate(blks):
            qm_ref[:, b] = q_roped[hd].astype(BF16)
            qmt_ref[b, :] = q_roped[hd].T.astype(BF16)
            km_ref[:, b] = k_roped[hd].astype(BF16)
        for p in range(HEADS // 2):
            even = kv[:, LANES * 2 * p:LANES * (2 * p + 1)]
            odd = kv[:, LANES * (2 * p + 1):LANES * (2 * p + 2)]
            vm_ref[:, LANES * p:LANES * (p + 1)] = jnp.where(low, pltpu.roll(even, 64, 1), odd).astype(BF16)

    def tile(width):
        return pl.BlockSpec((TS_FWD, width), lambda i: (i, 0))

    def full(a):
        return pl.BlockSpec(a.shape, lambda i: (0, 0))

    out_widths = [(D_MODEL, BF16), (1536, BF16), (512, F32), (Q_LORA, F32), (KV_LORA, F32),
                  (512, F32), (LANES, F32), (1024, BF16), (1024, BF16), (512, BF16)]
    t_heights = [SB_WIDTH, HEADS * LANES]
    return pl.pallas_call(
        body, name="fwd_pre", grid=(s_len // TS_FWD,),
        out_shape=[jax.ShapeDtypeStruct((s_len, w), dt) for w, dt in out_widths]
        + [jax.ShapeDtypeStruct((hgt, s_len), BF16) for hgt in t_heights],
        in_specs=[tile(D_MODEL), full(shift), full(scale), full(norm_w), full(w_in_r), full(qln), full(wuq),
                  full(kvln), full(wukv), full(qhn), full(khn), tile(LANES)],
        out_specs=[tile(w) for w, _ in out_widths]
        + [pl.BlockSpec((hgt, TS_FWD), lambda i: (0, i)) for hgt in t_heights],
        compiler_params=pltpu.CompilerParams(dimension_semantics=("parallel",), vmem_limit_bytes=VMEM_BIG),
    )(x, shift, scale, norm_w, w_in_r, qln, wuq, kvln, wukv, qhn, khn, rope_table)


CUM_W = 256


def _tri(later):
    j = lax.broadcasted_iota(jnp.int32, (CUM_W, CUM_W), 0)
    s = lax.broadcasted_iota(jnp.int32, (CUM_W, CUM_W), 1)
    return (j > s if later else j < s).astype(BF16)


def _suffix_sums(a, a_bf16, tri_m, carry):
    n = a.shape[1] // CUM_W
    outs = [None] * n
    for i in reversed(range(n)):
        cols = slice(CUM_W * i, CUM_W * (i + 1))
        outs[i] = _dot_nn(a_bf16[:, cols], tri_m) + carry
        carry = carry + _rowsum(a[:, cols])
    return (outs[0] if n == 1 else jnp.concatenate(outs, axis=1)), carry


def _sb_weights(qm, kb, carry, tri_u, diag):
    z = _dot_nt(qm, kb)
    nz = -z
    lk = jnp.minimum(nz, 0.0) - jnp.log(1.0 + jnp.exp(jnp.minimum(z, nz)))
    if diag:
        t = lax.broadcasted_iota(jnp.int32, (TQ, TQ), 0)
        s = lax.broadcasted_iota(jnp.int32, (TQ, TQ), 1)
        valid = s < t
        lk = jnp.where(valid, lk, 0.0)
    lk_hi = lk.astype(BF16)
    log_beta = z + lk
    after, carry = _suffix_sums(lk, lk_hi, tri_u, carry)
    logw = log_beta + after
    if diag:
        logw = jnp.where(valid, logw, NEG)
    return lk_hi, jnp.exp(logw), carry


SB_SCALE = 0.125


def _head_masks():
    lane = _lane_iota((1, LANES))
    return [lane < 64, lane >= 64]


def _masked(hm, a):
    return jnp.where(hm, a, jnp.zeros_like(a))


def _rowsum(a):
    return jnp.sum(a, axis=1, keepdims=True)


def _key_rows(kj):
    return pl.ds(pl.multiple_of(kj * TQ, TQ), TQ)


def _over_key_tiles(count, fn, st, ascending):
    n_full = count // KEY_UNROLL
    n_rest = count - n_full * KEY_UNROLL

    def group(g, s_):
        return fn([g * KEY_UNROLL + (u if ascending else KEY_UNROLL - 1 - u) for u in range(KEY_UNROLL)], s_)

    def left_over(s_):
        for r in range(1, KEY_UNROLL):
            tiles = [n_full * KEY_UNROLL + u if ascending else count - 1 - u for u in range(r)]
            s_ = lax.cond(n_rest == r, lambda a, tiles=tiles: fn(tiles, a), lambda a: a, s_)
        return s_

    if ascending:
        return left_over(lax.fori_loop(0, n_full, group, st))
    return lax.fori_loop(0, n_full, lambda i, s_: group(n_full - 1 - i, s_), left_over(st))


STAGE_SLOTS = 2 * KEY_UNROLL


def _stage_copies(to_hbm, hbm_refs, scr_refs, sems, pair, qi, kj):
    slot = (qi - kj) % STAGE_SLOTS
    out = []
    for h in range(2):
        for a in range(2):
            hbm, scr = hbm_refs[a].at[2 * pair + h, qi, kj], scr_refs[a].at[slot, h]
            sem = sems.at[4 * slot + 2 * h + a]
            out.append(pltpu.make_async_copy(scr, hbm, sem) if to_hbm else pltpu.make_async_copy(hbm, scr, sem))
    return out


def _sb_fwd(qkv):
    s_len = qkv.shape[0]
    nq = s_len // TQ

    def body(q_ref, k_ref, v_ref, o_ref, w_hbm, l_hbm, w_scr, l_scr, sems):
        pair, qi = pl.program_id(0), pl.program_id(1)
        q = q_ref[...]
        tri_u = _tri(True)
        masks = _head_masks()
        qms = [_masked(hm, q) * SB_SCALE for hm in masks]

        def copies(kj, of_qi=qi):
            return _stage_copies(True, (w_hbm, l_hbm), (w_scr, l_scr), sems, pair, of_qi, kj)

        def drain(of_qi):
            for kj in range(STAGE_SLOTS):
                @pl.when(kj <= of_qi)
                def _():
                    for cp in copies(kj, of_qi):
                        cp.wait()

        def block(kj, st, diag, before_staging=None):
            rows = _key_rows(kj)
            slot = (qi - kj) % STAGE_SLOTS
            kb, vb = k_ref[rows, :], v_ref[rows, :]
            carries, acc = list(st[:2]), st[2]
            staged = []
            for h in range(2):
                lk_hi, w, carries[h] = _sb_weights(qms[h], kb, carries[h], tri_u, diag)
                wb = w.astype(BF16)
                staged.append((wb, lk_hi))
                acc = acc + _dot_nn(wb, _masked(masks[h], vb))
            if before_staging is not None:
                before_staging()
            for h in range(2):
                w_scr[slot, h], l_scr[slot, h] = staged[h]
            return carries[0], carries[1], acc

        def trip(tiles, st):
            for kj in tiles:
                @pl.when(qi - kj >= STAGE_SLOTS)
                def _():
                    for cp in copies(kj + STAGE_SLOTS):
                        cp.wait()
            for kj in tiles:
                st = block(kj, st, False)
            for kj in tiles:
                for cp in copies(kj):
                    cp.start()
            return st

        def drain_previous_step():
            @pl.when(jnp.logical_or(pair > 0, qi > 0))
            def _():
                drain(jnp.where(qi == 0, nq - 1, qi - 1))

        zc = jnp.zeros((TQ, 1), F32)
        st = block(qi, (zc, zc, jnp.zeros((TQ, LANES), F32)), True, drain_previous_step)
        for cp in copies(qi):
            cp.start()
        st = _over_key_tiles(qi, trip, st, ascending=False)
        o_ref[...] = st[2]

        @pl.when(jnp.logical_and(pair == HEADS // 2 - 1, qi == nq - 1))
        def _():
            drain(qi)

    saved = jax.ShapeDtypeStruct((HEADS, nq, nq, TQ, TQ), BF16)
    stage = pltpu.VMEM((STAGE_SLOTS, 2, TQ, TQ), BF16)
    return pl.pallas_call(
        body, name="sb_fwd", grid=(HEADS // 2, nq),
        out_shape=[jax.ShapeDtypeStruct((s_len, SB_WIDTH), F32), saved, saved],
        in_specs=[pl.BlockSpec((TQ, LANES), lambda p, i: (i, p)),
                  pl.BlockSpec((s_len, LANES), lambda p, i: (0, 4 + p)),
                  pl.BlockSpec((s_len, LANES), lambda p, i: (0, 8 + p))],
        out_specs=[pl.BlockSpec((TQ, LANES), lambda p, i: (i, p)), _any_spec(), _any_spec()],
        scratch_shapes=[stage, stage, pltpu.SemaphoreType.DMA((4 * STAGE_SLOTS,))],
        compiler_params=pltpu.CompilerParams(dimension_semantics=("arbitrary", "arbitrary"),
                                             vmem_limit_bytes=VMEM_BIG),
    )(qkv, qkv, qkv)


def _prefix_sums(a, tri_m, carry):
    n = a.shape[1] // CUM_W
    outs = [None] * n
    for i in range(n):
        cols = slice(CUM_W * i, CUM_W * (i + 1))
        outs[i] = _dot_nn(a[:, cols].astype(BF16), tri_m) + carry
        carry = carry + _rowsum(a[:, cols])
    return (outs[0] if n == 1 else jnp.concatenate(outs, axis=1)), carry


def _head_rows(a0, a1):
    sub = lax.broadcasted_iota(jnp.int32, a0.shape, 0)
    return jnp.where(sub < NOPE, a0, a1)


def _sb_bwd(qkv, q_t, do, do_t, w_saved, l_saved):
    s_len = qkv.shape[0]
    nq = s_len // TQ

    def body(qt_ref, k_ref, v_ref, do_ref, dot_ref, w_hbm, l_hbm, dq_ref, dk_ref, dv_ref, w_scr, l_scr, sems):
        pair, qi = pl.program_id(0), pl.program_id(1)

        def copies(kj, of_pair=pair, of_qi=qi):
            return _stage_copies(False, (w_hbm, l_hbm), (w_scr, l_scr), sems, of_pair, of_qi, kj)

        def start_first_tiles(of_pair, of_qi):
            for cp in copies(0, of_pair, of_qi):
                cp.start()

            @pl.when(of_qi >= 1)
            def _():
                for cp in copies(1, of_pair, of_qi):
                    cp.start()

        @pl.when(jnp.logical_and(pair == 0, qi == 0))
        def _():
            start_first_tiles(pair, qi)

        @pl.when(qi == 0)
        def _():
            dk_ref[...] = jnp.zeros_like(dk_ref)
            dv_ref[...] = jnp.zeros_like(dv_ref)

        qt = qt_ref[...] * SB_SCALE
        dot_v = dot_ref[...]
        do_v = do_ref[...]
        tri_before = _tri(False)
        masks = _head_masks()
        doms = [_masked(hm, do_v) for hm in masks]

        def block(kj, st):
            rows = _key_rows(kj)
            slot = (qi - kj) % STAGE_SLOTS
            kb, vb = k_ref[rows, :], v_ref[rows, :]
            carries, dqs = list(st[0:2]), list(st[2:4])
            dk_t, dv_t = [], []
            for h in range(2):
                wb = w_scr[slot, h]
                d_l = _dot_nt(doms[h], vb) * wb.astype(F32)
                before, carries[h] = _prefix_sums(d_l, tri_before, carries[h])
                keep = jnp.exp(l_scr[slot, h].astype(F32))
                dzb = (d_l * keep - before * (1.0 - keep)).astype(BF16)
                dk_t.append(_dot_nn(qt, dzb))
                dv_t.append(_dot_nn(dot_v, wb))
                dqs[h] = dqs[h] + _dot_nn(dzb, kb)
            dk_ref[kj] += _head_rows(*dk_t)
            dv_ref[kj] += _head_rows(*dv_t)
            return (*carries, *dqs)

        def trip(tiles, st):
            for kj in tiles:
                @pl.when(kj + 2 <= qi)
                def _():
                    for cp in copies(kj + 2):
                        cp.start()
            for kj in tiles:
                for cp in copies(kj):
                    cp.wait()
            for kj in tiles:
                st = block(kj, st)
            return st

        zc = jnp.zeros((TQ, 1), F32)
        za = jnp.zeros((TQ, LANES), F32)
        st = _over_key_tiles(qi + 1, trip, (zc, zc, za, za), ascending=True)
        dq_ref[...] = jnp.where(masks[0], st[2], st[3]) * SB_SCALE

        @pl.when(jnp.logical_or(pair < HEADS // 2 - 1, qi < nq - 1))
        def _():
            wraps = qi == nq - 1
            start_first_tiles(jnp.where(wraps, pair + 1, pair), jnp.where(wraps, 0, qi + 1))

    tile = pl.BlockSpec((TQ, LANES), lambda p, i: (i, p))
    tile_t = pl.BlockSpec((LANES, TQ), lambda p, i: (p, i))
    col_t = pl.BlockSpec((nq, LANES, TQ), lambda p, i: (0, p, 0))
    stage = pltpu.VMEM((STAGE_SLOTS, 2, TQ, TQ), BF16)
    key_t = jax.ShapeDtypeStruct((nq, SB_WIDTH, TQ), F32)
    return pl.pallas_call(
        body, name="sb_bwd", grid=(HEADS // 2, nq),
        out_shape=[jax.ShapeDtypeStruct((s_len, SB_WIDTH), F32), key_t, key_t],
        in_specs=[tile_t,
                  pl.BlockSpec((s_len, LANES), lambda p, i: (0, 4 + p)),
                  pl.BlockSpec((s_len, LANES), lambda p, i: (0, 8 + p)),
                  tile, tile_t, _any_spec(), _any_spec()],
        out_specs=[tile, col_t, col_t],
        scratch_shapes=[stage, stage, pltpu.SemaphoreType.DMA((4 * STAGE_SLOTS,))],
        compiler_params=pltpu.CompilerParams(dimension_semantics=("arbitrary", "arbitrary"),
                                             vmem_limit_bytes=VMEM_BIG),
    )(q_t, qkv, qkv, do, do_t, w_saved, l_saved)


def _from_key_tiles(a_t):
    return a_t.transpose(0, 2, 1).reshape(a_t.shape[0] * a_t.shape[2], a_t.shape[1])


MLA_SCALE = 1.0 / math.sqrt(MLA_QK)


def _causal_mask():
    t = lax.broadcasted_iota(jnp.int32, (TQ, TQ), 0)
    s = lax.broadcasted_iota(jnp.int32, (TQ, TQ), 1)
    return s <= t


def _head_lanes(h):
    return slice(LANES * h, LANES * (h + 1))


P_SLOTS = 2 * KEY_UNROLL
P_COLS = TQ + LANES


def _tile_number(pair, qi, kj, nq):
    return pair * (nq * (nq + 1) // 2) + (qi * (qi + 1)) // 2 + kj


def _p_copy(to_hbm, p_hbm, p_scr, sems, pair, qi, kj, nq, h):
    slot = _tile_number(pair, qi, kj, nq) % P_SLOTS
    hbm, scr, sem = p_hbm.at[2 * pair + h, qi, kj], p_scr.at[slot, h], sems.at[2 * slot + h]
    return pltpu.make_async_copy(scr, hbm, sem) if to_hbm else pltpu.make_async_copy(hbm, scr, sem)


def _mla_fwd(qm, km, vm):
    s_len = qm.shape[0]
    nq = s_len // TQ

    def body(q_ref, k_ref, v_ref, o_ref, lse_ref, p_hbm, p_scr, sems):
        pair, qi = pl.program_id(0), pl.program_id(1)
        masks = _head_masks()
        qhs = [q_ref[:, _head_lanes(h)] for h in range(2)]
        lane = _lane_iota((TQ, LANES))

        def number(kj):
            return _tile_number(pair, qi, kj, nq)

        def copies(kj):
            return [_p_copy(True, p_hbm, p_scr, sems, pair, qi, kj, nq, h) for h in range(2)]

        def two_parts(m):
            hi = m.astype(BF16).astype(F32)
            return jnp.where(lane < 64, hi, m - hi).astype(BF16)

        def block(kj, st, diag):
            rows = _key_rows(kj)
            slot = number(kj) % P_SLOTS
            vb = v_ref[rows, :]
            ms, ls, acc = list(st[0:2]), list(st[2:4]), st[4]
            alphas, pvs = [], []
            for h in range(2):
                s = _dot_nt(qhs[h], k_ref[rows, _head_lanes(h)])
                if diag:
                    s = jnp.where(_causal_mask(), s, NEG)
                m_new = jnp.maximum(ms[h], jnp.max(s, axis=1, keepdims=True))
                p = jnp.exp(s - m_new)
                pb = p.astype(BF16)
                p_scr[slot, h, :, 0:TQ] = pb
                p_scr[slot, h, :, TQ:P_COLS] = two_parts(m_new)
                alphas.append(jnp.exp(ms[h] - m_new))
                ls[h] = alphas[h] * ls[h] + _rowsum(p)
                ms[h] = m_new
                pvs.append(_dot_nn(pb, _masked(masks[h], vb)))
            acc = jnp.where(masks[0], alphas[0], alphas[1]) * acc + pvs[0] + pvs[1]
            return (*ms, *ls, acc)

        def trip(tiles, st, diag=False):
            for kj in tiles:
                @pl.when(number(kj) >= P_SLOTS)
                def _():
                    for cp in copies(kj):
                        cp.wait()
            for kj in tiles:
                st = block(kj, st, diag)
            for kj in tiles:
                for cp in copies(kj):
                    cp.start()
            return st

        neg = jnp.full((TQ, 1), NEG, F32)
        zc = jnp.zeros((TQ, 1), F32)
        st = (neg, neg, zc, zc, jnp.zeros((TQ, LANES), F32))
        st = _over_key_tiles(qi, trip, st, ascending=True)
        m0, m1, l0, l1, acc = trip([qi], st, True)
        o_ref[...] = acc / jnp.where(masks[0], l0, l1)
        lse_ref[0] = m0 + jnp.log(l0)
        lse_ref[1] = m1 + jnp.log(l1)

        @pl.when(jnp.logical_and(pair == HEADS // 2 - 1, qi == nq - 1))
        def _():
            for slot in range(P_SLOTS):
                for h in range(2):
                    pltpu.make_async_copy(p_scr.at[slot, h], p_hbm.at[0, 0, 0], sems.at[2 * slot + h]).wait()

    return pl.pallas_call(
        body, name="mla_fwd", grid=(HEADS // 2, nq),
        out_shape=[jax.ShapeDtypeStruct((s_len, MLA_WIDTH), F32),
                   jax.ShapeDtypeStruct((HEADS, s_len, 1), F32),
                   jax.ShapeDtypeStruct((HEADS, nq, nq, TQ, P_COLS), BF16)],
        in_specs=[pl.BlockSpec((TQ, 2 * LANES), lambda p, i: (i, p)),
                  pl.BlockSpec((s_len, 2 * LANES), lambda p, i: (0, p)),
                  pl.BlockSpec((s_len, LANES), lambda p, i: (0, p))],
        out_specs=[pl.BlockSpec((TQ, LANES), lambda p, i: (i, p)),
                   pl.BlockSpec((2, TQ, 1), lambda p, i: (p, i, 0)),
                   _any_spec()],
        scratch_shapes=[pltpu.VMEM((P_SLOTS, 2, TQ, P_COLS), BF16), pltpu.SemaphoreType.DMA((2 * P_SLOTS,))],
        compiler_params=pltpu.CompilerParams(dimension_semantics=("arbitrary", "arbitrary"),
                                             vmem_limit_bytes=VMEM_BIG),
    )(qm, km, vm)


def _mla_bwd(qm_t, km, vm, o, do, do_t, lse, p_saved):
    s_len = km.shape[0]
    nq = s_len // TQ
    total = (HEADS // 2) * (nq * (nq + 1) // 2)

    def body(qt_ref, k_ref, v_ref, o_ref, do_ref, dot_ref, lse_ref, p_hbm, dq_ref, dk_ref, dv_ref, p_scr, sems):
        pair, qi = pl.program_id(0), pl.program_id(1)

        def number(kj):
            return _tile_number(pair, qi, kj, nq)

        def fetch(p_, q_, k_):
            for h in range(2):
                _p_copy(False, p_hbm, p_scr, sems, p_, q_, k_, nq, h).start()

        def advance(p_, q_, k_):
            row_end = k_ == q_
            last_row = q_ == nq - 1
            return (jnp.where(jnp.logical_and(row_end, last_row), p_ + 1, p_),
                    jnp.where(row_end, jnp.where(last_row, 0, q_ + 1), q_),
                    jnp.where(row_end, 0, k_ + 1))

        @pl.when(jnp.logical_and(pair == 0, qi == 0))
        def _():
            fetch(pair, qi, 0)
            fetch(*advance(pair, qi, 0))

        @pl.when(qi == 0)
        def _():
            dk_ref[...] = jnp.zeros_like(dk_ref)
            dv_ref[...] = jnp.zeros_like(dv_ref)

        do_v = do_ref[...]
        dot_v = dot_ref[...]
        od = o_ref[...] * do_v.astype(F32)
        masks = _head_masks()
        qts = [qt_ref[_head_lanes(h), :] for h in range(2)]
        doms = [_masked(hm, do_v) for hm in masks]
        deltas = [_rowsum(jnp.where(hm, od, 0.0)) for hm in masks]
        lses = [lse_ref[h] for h in range(2)]

        def block(kj, dqs):
            rows = _key_rows(kj)
            slot = number(kj) % P_SLOTS
            vb = v_ref[rows, :]
            dqs = list(dqs)
            dv_t = []
            for h in range(2):
                kb = k_ref[rows, _head_lanes(h)]
                tile_max = _rowsum(p_scr[slot, h, :, TQ:P_COLS].astype(F32)) * (1.0 / 64.0)
                p = p_scr[slot, h, :, 0:TQ].astype(F32) * jnp.exp(tile_max - lses[h])
                dp = _dot_nt(doms[h], vb)
                ds = (p * (dp - deltas[h])).astype(BF16)
                dk_ref[kj, _head_lanes(h), :] += _dot_nn(qts[h], ds)
                dv_t.append(_dot_nn(dot_v, p.astype(BF16)))
                dqs[h] = dqs[h] + _dot_nn(ds, kb)
            dv_ref[kj] += _head_rows(*dv_t)
            return tuple(dqs)

        def trip(tiles, dqs):
            for kj in tiles:
                @pl.when(number(kj) + 2 < total)
                def _():
                    fetch(*advance(*advance(pair, qi, kj)))
            for kj in tiles:
                for h in range(2):
                    _p_copy(False, p_hbm, p_scr, sems, pair, qi, kj, nq, h).wait()
            for kj in tiles:
                dqs = block(kj, dqs)
            return dqs

        za = jnp.zeros((TQ, LANES), F32)
        dqs = _over_key_tiles(qi + 1, trip, (za, za), ascending=True)
        dq_ref[:, _head_lanes(0)] = dqs[0] * MLA_SCALE
        dq_ref[:, _head_lanes(1)] = dqs[1] * MLA_SCALE

    return pl.pallas_call(
        body, name="mla_bwd", grid=(HEADS // 2, nq),
        out_shape=[jax.ShapeDtypeStruct((s_len, HEADS * LANES), F32),
                   jax.ShapeDtypeStruct((nq, HEADS * LANES, TQ), F32),
                   jax.ShapeDtypeStruct((nq, MLA_WIDTH, TQ), F32)],
        in_specs=[pl.BlockSpec((2 * LANES, TQ), lambda p, i: (p, i)),
                  pl.BlockSpec((s_len, 2 * LANES), lambda p, i: (0, p)),
                  pl.BlockSpec((s_len, LANES), lambda p, i: (0, p)),
                  pl.BlockSpec((TQ, LANES), lambda p, i: (i, p)),
                  pl.BlockSpec((TQ, LANES), lambda p, i: (i, p)),
                  pl.BlockSpec((LANES, TQ), lambda p, i: (p, i)),
                  pl.BlockSpec((2, TQ, 1), lambda p, i: (p, i, 0)),
                  _any_spec()],
        out_specs=[pl.BlockSpec((TQ, 2 * LANES), lambda p, i: (i, p)),
                   pl.BlockSpec((nq, 2 * LANES, TQ), lambda p, i: (0, p, 0)),
                   pl.BlockSpec((nq, LANES, TQ), lambda p, i: (0, p, 0))],
        scratch_shapes=[pltpu.VMEM((P_SLOTS, 2, TQ, P_COLS), BF16), pltpu.SemaphoreType.DMA((2 * P_SLOTS,))],
        compiler_params=pltpu.CompilerParams(dimension_semantics=("arbitrary", "arbitrary"),
                                             vmem_limit_bytes=VMEM_BIG),
    )(qm_t, km, vm, o, do, do_t, lse, p_saved)


def _mid(o_sb, g_sb, o_mla, g_mla, x, target, gate, w_out):
    s_len = x.shape[0]

    def body(osb_ref, gsb_ref, omla_ref, gmla_ref, x_ref, t_ref, gate_ref, w_ref,
             dy_ref, dosb_ref, dgsb_ref, domla_ref, dgmla_ref, gw_ref, dgate_ref, loss_ref, dosbt_ref, domlat_ref):
        @pl.when(pl.program_id(0) == 0)
        def _():
            gw_ref[...] = jnp.zeros_like(gw_ref)
            dgate_ref[...] = jnp.zeros_like(dgate_ref)
            loss_ref[...] = jnp.zeros_like(loss_ref)

        o1, g1, o2, g2 = osb_ref[...], gsb_ref[...], omla_ref[...], gmla_ref[...]
        s1, s2 = _sigmoid(g1), _sigmoid(g2)
        mixed = jnp.concatenate([o1 * (g1 * s1), o2 * (g2 * s2)], axis=1).astype(BF16)
        w = w_ref[...]
        gate_v = gate_ref[...]
        u = _dot_nn(mixed, w)
        err = x_ref[...] + gate_v * u - t_ref[...]
        loss_ref[...] += jnp.sum(err * err, axis=0, keepdims=True)
        dy = err * (1.0 / D_MODEL)
        dy_ref[...] = dy
        dgate_ref[...] += jnp.sum(dy * u, axis=0, keepdims=True)
        du = (dy * gate_v).astype(BF16)
        gw_ref[...] += _dot_tn(mixed, du)
        dmixed = _dot_nt(du, w)
        d1, d2 = dmixed[:, :SB_WIDTH], dmixed[:, SB_WIDTH:]
        do1, do2 = d1 * (g1 * s1), d2 * (g2 * s2)
        dosb_ref[...] = do1.astype(BF16)
        dgsb_ref[...] = (d1 * o1 * (s1 * (1.0 + g1 * (1.0 - s1)))).astype(BF16)
        domla_ref[...] = do2.astype(BF16)
        dgmla_ref[...] = (d2 * o2 * (s2 * (1.0 + g2 * (1.0 - s2)))).astype(BF16)
        dosbt_ref[...] = do1.T.astype(BF16)
        domlat_ref[...] = do2.T.astype(BF16)

    def tile(width):
        return pl.BlockSpec((TS, width), lambda i: (i, 0))

    def full(shape):
        return pl.BlockSpec(shape, lambda i: (0, 0))

    return pl.pallas_call(
        body, name="mid", grid=(s_len // TS,),
        out_shape=[jax.ShapeDtypeStruct((s_len, D_MODEL), F32)]
        + [jax.ShapeDtypeStruct((s_len, 512), BF16)] * 4
        + [jax.ShapeDtypeStruct((D_MODEL, D_MODEL), F32),
           jax.ShapeDtypeStruct((1, D_MODEL), F32), jax.ShapeDtypeStruct((1, D_MODEL), F32)]
        + [jax.ShapeDtypeStruct((512, s_len), BF16)] * 2,
        in_specs=[tile(512)] * 4 + [tile(D_MODEL), tile(D_MODEL), full((1, D_MODEL)), full((D_MODEL, D_MODEL))],
        out_specs=[tile(D_MODEL)] + [tile(512)] * 4
        + [full((D_MODEL, D_MODEL)), full((1, D_MODEL)), full((1, D_MODEL))]
        + [pl.BlockSpec((512, TS), lambda i: (0, i))] * 2,
        compiler_params=pltpu.CompilerParams(dimension_semantics=("arbitrary",), vmem_limit_bytes=VMEM_BIG),
    )(o_sb, g_sb, o_mla, g_mla, x, target, gate, w_out)


def _mla_pre_bwd(dq, dk, dv, cq, ckv, kr, qln, wuq, kvln, wukv, qhn, khn, rope_table):
    s_len = cq.shape[0]

    def body(dq_ref, dk_ref, dv_ref, cq_ref, ckv_ref, kr_ref, qln_ref, wuq_ref, kvln_ref, wukv_ref,
             qhn_ref, khn_ref, rt_ref,
             dcq_ref, dckv_ref, dkr_ref, gwuq_ref, gwukv_ref, gqhn_ref, gkhn_ref, gqln_ref, gkvln_ref,
             dqa_ref, dkv_ref):
        @pl.when(pl.program_id(0) == 0)
        def _():
            for r_ in (gwuq_ref, gwukv_ref, gqhn_ref, gkhn_ref, gqln_ref, gkvln_ref):
                r_[...] = jnp.zeros_like(r_)

        cq, ckv = cq_ref[...], ckv_ref[...]
        qln_v, kvln_v = qln_ref[...], kvln_ref[...]
        wuq_v, wukv_v = wuq_ref[...], wukv_ref[...]
        rq, rkv, cq_hat, ckv_hat, cqn, ckvn, q_all, kv, kr64 = _mla_latents(
            cq, ckv, kr_ref[...], qln_v, kvln_v, wuq_v, wukv_v)
        cosf, sa, sb = _expand_rope(rt_ref[...])
        qhn_v, khn_v = qhn_ref[...], khn_ref[...]
        lane = _lane_iota((TS, LANES))
        low = lane < NOPE
        blks = [slice(LANES * hd, LANES * (hd + 1)) for hd in range(HEADS)]
        raw = [q_all[:, b] for b in blks] + [jnp.where(low, kv[:, b], kr64) for b in blks]
        grads = [dq_ref[:, b] for b in blks] + [dk_ref[:, b] for b in blks]
        gains = [qhn_v] * HEADS + [khn_v] * HEADS
        rms = [_head_rms(t) for t in raw]
        xhs = [t * r for t, r in zip(raw, rms)]
        dns = [_rope_t(d, cosf, sa, sb) for d in grads]
        gain_g = [jnp.sum(dn * xh, axis=0, keepdims=True) for dn, xh in zip(dns, xhs)]
        dxs = [_rms_bwd(dn * g, xh, r, MLA_QK) for dn, g, xh, r in zip(dns, gains, xhs, rms)]
        dkr64 = jnp.zeros((TS, LANES), F32)
        for hd, b in enumerate(blks):
            dqa_ref[:, b] = dxs[hd].astype(BF16)
            dkb = dxs[HEADS + hd]
            dkr64 = dkr64 + jnp.where(low, 0.0, dkb)
            dvp = dv_ref[:, LANES * (hd // 2):LANES * (hd // 2 + 1)]
            dvh = pltpu.roll(dvp, 64, 1) if hd % 2 == 0 else dvp
            dkv_ref[:, b] = jnp.where(low, dkb, dvh).astype(BF16)
        gqhn_ref[...] += sum(gain_g[:HEADS])
        gkhn_ref[...] += sum(gain_g[HEADS:])
        dkr_ref[...] = pltpu.roll(dkr64, 64, 1).astype(BF16)

        dqa = dqa_ref[...]
        gwuq_ref[...] += _dot_tn(cqn, dqa)
        dcqn = _dot_nt(dqa, wuq_v)
        gqln_ref[...] += jnp.sum(dcqn * cq_hat, axis=0, keepdims=True)
        dcq_ref[...] = _rms_bwd(dcqn * qln_v, cq_hat, rq, Q_LORA).astype(BF16)

        dkv = dkv_ref[...]
        gwukv_ref[...] += _dot_tn(ckvn, dkv)
        dckvn = _dot_nt(dkv, wukv_v)
        gkvln_ref[...] += jnp.sum(dckvn * ckv_hat, axis=0, keepdims=True)
        dckv_ref[...] = _rms_bwd(dckvn * kvln_v, ckv_hat, rkv, KV_LORA).astype(BF16)

    def tile(width):
        return pl.BlockSpec((TS, width), lambda i: (i, 0))

    def full(shape):
        return pl.BlockSpec(shape, lambda i: (0, 0))

    acc_shapes = [(Q_LORA, 1024), (KV_LORA, 1024), (1, LANES), (1, LANES), (1, Q_LORA), (1, KV_LORA)]
    return pl.pallas_call(
        body, name="mla_pre_bwd", grid=(s_len // TS,),
        out_shape=[jax.ShapeDtypeStruct((s_len, Q_LORA), BF16), jax.ShapeDtypeStruct((s_len, KV_LORA), BF16),
                   jax.ShapeDtypeStruct((s_len, LANES), BF16)]
        + [jax.ShapeDtypeStruct(s, F32) for s in acc_shapes],
        in_specs=[tile(1024), tile(1024), tile(512), tile(Q_LORA), tile(KV_LORA), tile(LANES),
                  full(qln.shape), full(wuq.shape), full(kvln.shape), full(wukv.shape),
                  full(qhn.shape), full(khn.shape), tile(LANES)],
        out_specs=[tile(Q_LORA), tile(KV_LORA), tile(LANES)] + [full(s) for s in acc_shapes],
        scratch_shapes=[pltpu.VMEM((TS, 1024), BF16), pltpu.VMEM((TS, 1024), BF16)],
        compiler_params=pltpu.CompilerParams(dimension_semantics=("arbitrary",), vmem_limit_bytes=VMEM_BIG),
    )(dq, dk, dv, cq, ckv, kr, qln, wuq, kvln, wukv, qhn, khn, rope_table)


def _dproj_bwd(dq_sb, dk_sb, dv_sb, dg_sb, dcq, dckv, dg_mla, dkr, w_in_r, x, dy, norm_w, scale):
    s_len = x.shape[0]

    def body(dq_ref, dk_ref, dv_ref, dg_ref, dcq_ref, dckv_ref, dgm_ref, dkr_ref, w_ref, x_ref, dy_ref,
             nw_ref, scale_ref, dp_ref, gx_ref, dshift_ref, dscale_ref, dnw_ref):
        @pl.when(pl.program_id(0) == 0)
        def _():
            for r_ in (dshift_ref, dscale_ref, dnw_ref):
                r_[...] = jnp.zeros_like(r_)

        dp_ref[:, 0:512] = dq_ref[...].astype(BF16)
        dp_ref[:, 512:1024] = dk_ref[...].astype(BF16)
        dp_ref[:, 1024:1536] = dv_ref[...].astype(BF16)
        dp_ref[:, 1536:2048] = dg_ref[...]
        dp_ref[:, 2048:2432] = dcq_ref[...]
        dp_ref[:, 2432:2688] = dckv_ref[...]
        dp_ref[:, 2688:3200] = dgm_ref[...]
        dp_ref[:, 3200:3328] = dkr_ref[...]
        dh = _dot_nt(dp_ref[...], w_ref[...])
        xv = x_ref[...]
        r = lax.rsqrt(jnp.mean(xv * xv, axis=1, keepdims=True) + EPS)
        xh = xv * r
        nw = nw_ref[...]
        dshift_ref[...] += jnp.sum(dh, axis=0, keepdims=True)
        dscale_ref[...] += jnp.sum(dh * (xh * nw), axis=0, keepdims=True)
        dxnw = dh * (1.0 + scale_ref[...])
        dnw_ref[...] += jnp.sum(dxnw * xh, axis=0, keepdims=True)
        gx_ref[...] = dy_ref[...] + _rms_bwd(dxnw * nw, xh, r, D_MODEL)

    def tile(width):
        return pl.BlockSpec((TS, width), lambda i: (i, 0))

    def full(shape):
        return pl.BlockSpec(shape, lambda i: (0, 0))

    vec = (1, D_MODEL)
    return pl.pallas_call(
        body, name="dproj_bwd", grid=(s_len // TS,),
        out_shape=[jax.ShapeDtypeStruct((s_len, IN_COLS_R), BF16), jax.ShapeDtypeStruct((s_len, D_MODEL), F32)]
        + [jax.ShapeDtypeStruct(vec, F32)] * 3,
        in_specs=[tile(512)] * 4 + [tile(Q_LORA), tile(KV_LORA), tile(512), tile(LANES),
                                    full(w_in_r.shape), tile(D_MODEL), tile(D_MODEL), full(vec), full(vec)],
        out_specs=[tile(IN_COLS_R), tile(D_MODEL)] + [full(vec)] * 3,
        compiler_params=pltpu.CompilerParams(dimension_semantics=("arbitrary",), vmem_limit_bytes=VMEM_BIG),
    )(dq_sb, dk_sb, dv_sb, dg_sb, dcq, dckv, dg_mla, dkr, w_in_r, x, dy, norm_w, scale)


def _grad_w_in(hb, dproj):
    s_len = hb.shape[0]
    n_half = IN_COLS_R // 2
    n_steps = s_len // TN_S

    def body(h_ref, d_ref, g_ref, acc_ref):
        @pl.when(pl.program_id(1) == 0)
        def _():
            acc_ref[...] = jnp.zeros_like(acc_ref)

        acc_ref[...] += _dot_tn(h_ref[...], d_ref[...])

        @pl.when(pl.program_id(1) == n_steps - 1)
        def _():
            g_ref[...] = acc_ref[...].astype(BF16)

    return pl.pallas_call(
        body, name="grad_w_in", grid=(2, n_steps),
        out_shape=jax.ShapeDtypeStruct((D_MODEL, IN_COLS_R), BF16),
        in_specs=[pl.BlockSpec((TN_S, D_MODEL), lambda n, s: (s, 0)),
                  pl.BlockSpec((TN_S, n_half), lambda n, s: (s, n))],
        out_specs=pl.BlockSpec((D_MODEL, n_half), lambda n, s: (0, n)),
        scratch_shapes=[pltpu.VMEM((D_MODEL, n_half), F32)],
        compiler_params=pltpu.CompilerParams(dimension_semantics=("parallel", "arbitrary"),
                                             vmem_limit_bytes=VMEM_BIG),
    )(hb, dproj)


def _final_exchange(gpack, ccol, wpack, mpack, vpack, n_sh, grads):
    n = len(grads)

    def body(*refs):
        (g_ref, cc_ref, wp_ref, mp_ref, vp_ref) = refs[:5]
        slabs_in = refs[5:5 + n]
        (og_ref, od_ref, om_ref, ov_ref, ag_ref) = refs[5 + n:10 + n]
        slabs_out = refs[10 + n:10 + 2 * n]
        gall_ref, ssem, rsem, slab_ssem, slab_rsem, lsem = refs[10 + 2 * n:]
        pos = _mesh_pos()
        me = _lin(pos)
        gall_ref[me] = g_ref[...]
        small = _all_gather_start(pos, g_ref, gall_ref, ssem, rsem, 0)
        own = [pltpu.make_async_copy(slabs_in[a].at[me], slabs_out[a].at[me], lsem.at[a]) for a in range(n)]
        for cp in own:
            cp.start()
        in_flight = [_all_to_all_start(pos, slabs_in[a], slabs_out[a], slab_ssem, slab_rsem, a * (N_DEV - 1))
                     for a in range(n)]
        _all_gather_wait(pos, g_ref, gall_ref, ssem, rsem, 0, small)

        tot = gall_ref[0]
        for j in range(1, N_DEV):
            tot = tot + gall_ref[j]
        og_ref[...] = tot
        od_ref[...], om_ref[...], ov_ref[...] = _adamw(wp_ref[...], tot, mp_ref[...], vp_ref[...])

        ga = jnp.zeros((D_MODEL, n_sh), F32)
        for j in range(N_DEV):
            d_mine = jnp.zeros((8, n_sh), F32)
            for k in range(N_DEV):
                d_mine = d_mine + jnp.where(me == k, gall_ref[j, :, PK_ADA + n_sh * k:PK_ADA + n_sh * (k + 1)], 0.0)
            col = _silu(cc_ref[j])
            ga = ga + jnp.concatenate(
                [col * d_mine[0:1, LANES * a:LANES * (a + 1)] for a in range(n_sh // LANES)], axis=1)
        ag_ref[...] = ga

        for a in range(n):
            _all_to_all_wait(pos, slabs_in[a], slabs_out[a], slab_ssem, slab_rsem, a * (N_DEV - 1), in_flight[a])
        for cp in own:
            cp.wait()

    pk = jax.ShapeDtypeStruct((8, PK_END), F32)
    ada = jax.ShapeDtypeStruct((D_MODEL, n_sh), F32)
    return pl.pallas_call(
        body, name="final_exchange",
        out_shape=[pk] * 4 + [ada] + [jax.ShapeDtypeStruct(g.shape, g.dtype) for g in grads],
        in_specs=[_vmem_spec()] * 5 + [_any_spec()] * n,
        out_specs=[_vmem_spec()] * 5 + [_any_spec()] * n,
        scratch_shapes=[
            pltpu.VMEM((N_DEV, 8, PK_END), F32),
            pltpu.SemaphoreType.DMA((N_DEV - 1,)),
            pltpu.SemaphoreType.DMA((N_DEV - 1,)),
            pltpu.SemaphoreType.DMA((n * (N_DEV - 1),)),
            pltpu.SemaphoreType.DMA((n * (N_DEV - 1),)),
            pltpu.SemaphoreType.DMA((n,)),
        ],
        compiler_params=pltpu.CompilerParams(vmem_limit_bytes=VMEM_BIG),
    )(gpack, ccol, wpack, mpack, vpack, *grads)


def _adamw_reduce(name, parts, w, m, v, row_tile):
    rows, cols = w.shape
    n_parts = parts.shape[0]

    def body(p_ref, w_ref, m_ref, v_ref, g_ref, d_ref, mo_ref, vo_ref):
        g = p_ref[0].astype(F32)
        for j in range(1, n_parts):
            g = g + p_ref[j].astype(F32)
        g_ref[...] = g
        d_ref[...], mo_ref[...], vo_ref[...] = _adamw(w_ref[...], g, m_ref[...], v_ref[...])

    tile = pl.BlockSpec((row_tile, cols), lambda i: (i, 0))
    return pl.pallas_call(
        body, name=name, grid=(rows // row_tile,),
        out_shape=[jax.ShapeDtypeStruct((rows, cols), F32)] * 4,
        in_specs=[pl.BlockSpec((n_parts, row_tile, cols), lambda i: (0, i, 0)), tile, tile, tile],
        out_specs=[tile] * 4,
        compiler_params=pltpu.CompilerParams(dimension_semantics=("parallel",), vmem_limit_bytes=VMEM_BIG),
    )(parts, w, m, v)


def _rope_table(positions):
    inv_freq = 10000.0 ** (-jnp.arange(0, ROPE, 2, dtype=F32) / ROPE)
    ang = positions.astype(F32)[:, None] * inv_freq
    cos, sin = jnp.cos(ang), jnp.sin(ang)
    pad = jnp.zeros((positions.shape[0], LANES - 2 * ROPE), F32)
    return jnp.concatenate([cos, cos, -sin, sin, pad], axis=1)


def _rearrange_cols(w):
    pad = jnp.zeros((w.shape[0], IN_COLS_R - IN_COLS), w.dtype)
    return jnp.concatenate([w[:, :2688], w[:, 2720:3232], w[:, 2688:2720], pad], axis=1)


def _restore_cols(g):
    return jnp.concatenate([g[:, :2688], g[:, 3200:3232], g[:, 2688:3200]], axis=1)


def _pad_heads(w):
    rows = w.shape[0]
    w = w.reshape(rows, HEADS, MLA_QK)
    return jnp.pad(w, ((0, 0), (0, 0), (0, LANES - MLA_QK))).reshape(rows, HEADS * LANES)


def _unpad_heads(g):
    rows = g.shape[0]
    return g.reshape(rows, HEADS, LANES)[:, :, :MLA_QK].reshape(rows, HEADS * MLA_QK)


def _pad_lanes(v):
    return jnp.pad(v, ((0, 0), (0, LANES - v.shape[1])))


def _col_shards(g):
    rows = g.shape[0]
    return g.reshape(rows, N_DEV, g.shape[1] // N_DEV).transpose(1, 0, 2)


def _from_col_shards(g):
    return g.transpose(1, 0, 2).reshape(g.shape[1], N_DEV * g.shape[2])


def _pack(norm_w, qln, kvln, qhn, khn, ada, loss_lanes=None):
    if loss_lanes is None:
        loss_lanes = jnp.zeros((1, PK_END - PK_LOSS), F32)
    row = jnp.concatenate([norm_w, qln, kvln, _pad_lanes(qhn), _pad_lanes(khn), ada, loss_lanes], axis=1)
    return jnp.broadcast_to(row, (8, PK_END))


def _unpack(p):
    row = p[0:1]
    return (row[:, PK_NORM:PK_QLN], row[:, PK_QLN:PK_KVLN], row[:, PK_KVLN:PK_QHN],
            row[:, PK_QHN:PK_QHN + MLA_QK], row[:, PK_KHN:PK_KHN + MLA_QK], row[:, PK_ADA:PK_LOSS])


def kernel(x, c, positions, w_ada, b_ada, norm_w, w_in, q_lora_norm, w_uq, kv_lora_norm, w_ukv, q_head_norm, k_head_norm, w_out, loss_target, m_w_ada, m_b_ada, m_norm_w, m_w_in, m_q_lora_norm, m_w_uq, m_kv_lora_norm, m_w_ukv, m_q_head_norm, m_k_head_norm, m_w_out, v_w_ada, v_b_ada, v_norm_w, v_w_in, v_q_lora_norm, v_w_uq, v_kv_lora_norm, v_w_ukv, v_q_head_norm, v_k_head_norm, v_w_out):
    s_len = x.shape[1]
    x2 = x.reshape(s_len, D_MODEL)
    tgt = loss_target.reshape(s_len, D_MODEL)
    w_ada_s, w_in_s, w_uq_s, w_ukv_s, w_out_s = w_ada[0], w_in[0], w_uq[0], w_ukv[0], w_out[0]

    ada8, c_all = _ada_fwd(jnp.broadcast_to(c, (8, D_MODEL)), w_ada_s, b_ada.reshape(N_DEV, -1))
    ada = ada8.reshape(1, 3 * D_MODEL)
    shift, scale, gate = ada[:, :D_MODEL], ada[:, D_MODEL:2 * D_MODEL], ada[:, 2 * D_MODEL:]

    g_uq, g_ukv, g_out, g_in = _gather_weights([w_uq_s, w_ukv_s, w_out_s, w_in_s])
    w_in_r = _rearrange_cols(_from_col_shards(g_in))
    wuq_p = _pad_heads(_from_col_shards(g_uq))
    wukv_f = _from_col_shards(g_ukv)
    w_out_f = g_out.reshape(D_MODEL, D_MODEL)

    rope_table = _rope_table(positions[0])
    qhn_p, khn_p = _pad_lanes(q_head_norm), _pad_lanes(k_head_norm)

    hb, qkv, g_sb, cq, ckv, g_mla, kr, qm, km, vm, q_sb_t, qm_t = _fwd_pre(
        x2, shift, scale, norm_w, w_in_r, q_lora_norm, wuq_p, kv_lora_norm, wukv_f, qhn_p, khn_p, rope_table)
    o_sb, w_saved, l_saved = _sb_fwd(qkv)
    o_mla, lse, p_saved = _mla_fwd(qm, km, vm)

    dy, do_sb, dg_sb, do_mla, dg_mla, gw_out, d_gate, loss_acc, do_sb_t, do_mla_t = _mid(
        o_sb, g_sb, o_mla, g_mla, x2, tgt, gate, w_out_f)

    dq_sb, dk_sb_t, dv_sb_t = _sb_bwd(qkv, q_sb_t, do_sb, do_sb_t, w_saved, l_saved)
    dk_sb, dv_sb = _from_key_tiles(dk_sb_t), _from_key_tiles(dv_sb_t)
    dq_m, dk_m_t, dv_m_t = _mla_bwd(qm_t, km, vm, o_mla, do_mla, do_mla_t, lse, p_saved)
    dk_m, dv_m = _from_key_tiles(dk_m_t), _from_key_tiles(dv_m_t)
    dcq, dckv, dkr, gw_uq_p, gw_ukv, g_qhn, g_khn, g_qln, g_kvln = _mla_pre_bwd(
        dq_m, dk_m, dv_m, cq, ckv, kr, q_lora_norm, wuq_p, kv_lora_norm, wukv_f, qhn_p, khn_p, rope_table)
    dproj, grad_x, d_shift, d_scale, g_norm_w = _dproj_bwd(
        dq_sb, dk_sb, dv_sb, dg_sb, dcq, dckv, dg_mla, dkr, w_in_r, x2, dy, norm_w, scale)
    gw_in = _restore_cols(_grad_w_in(hb, dproj))

    d_ada = jnp.concatenate([d_shift, d_scale, d_gate], axis=1)
    gpack = _pack(g_norm_w, g_qln, g_kvln, g_qhn[:, :MLA_QK], g_khn[:, :MLA_QK], d_ada, loss_acc)
    wpack = _pack(norm_w, q_lora_norm, kv_lora_norm, q_head_norm, k_head_norm, b_ada)
    mpack = _pack(m_norm_w, m_q_lora_norm, m_kv_lora_norm, m_q_head_norm, m_k_head_norm, m_b_ada)
    vpack = _pack(v_norm_w, v_q_lora_norm, v_kv_lora_norm, v_q_head_norm, v_k_head_norm, v_b_ada)
    ccol = jnp.broadcast_to(c_all[:, :, None], (N_DEV, D_MODEL, LANES))
    slabs = [g.astype(BF16) for g in (
        _col_shards(gw_in), _col_shards(_unpad_heads(gw_uq_p)), _col_shards(gw_ukv),
        gw_out.reshape(N_DEV, D_MODEL // N_DEV, D_MODEL))]
    pg, pd, pm, pv, gw_ada, r_in, r_uq, r_ukv, r_out = _final_exchange(
        gpack, ccol, wpack, mpack, vpack, w_ada_s.shape[1], slabs)
    loss = 0.5 * jnp.sum(pg[0, PK_LOSS:PK_END]) / D_MODEL

    ada_g, ada_d, ada_m, ada_v = _adamw_reduce("adamw_w_ada", gw_ada[None], w_ada_s, m_w_ada[0], v_w_ada[0], 256)
    in_g, in_d, in_m, in_v = _adamw_reduce("adamw_w_in", r_in, w_in_s, m_w_in[0], v_w_in[0], 256)
    uq_g, uq_d, uq_m, uq_v = _adamw_reduce("adamw_w_uq", r_uq, w_uq_s, m_w_uq[0], v_w_uq[0], w_uq_s.shape[0])
    ukv_g, ukv_d, ukv_m, ukv_v = _adamw_reduce(
        "adamw_w_ukv", r_ukv, w_ukv_s, m_w_ukv[0], v_w_ukv[0], w_ukv_s.shape[0])
    out_g, out_d, out_m, out_v = _adamw_reduce(
        "adamw_w_out", r_out, w_out_s, m_w_out[0], v_w_out[0], w_out_s.shape[0])

    def group(ada_t, pk, in_t, uq_t, ukv_t, out_t):
        nw, qln, kvln, qhn, khn, b = _unpack(pk)
        return (ada_t[None], b, nw, in_t[None], qln, uq_t[None], kvln, ukv_t[None], qhn, khn, out_t[None])

    return (loss, grad_x.reshape(1, s_len, D_MODEL),
            *group(ada_g, pg, in_g, uq_g, ukv_g, out_g),
            *group(ada_d, pd, in_d, uq_d, ukv_d, out_d),
            *group(ada_m, pm, in_m, uq_m, ukv_m, out_m),
            *group(ada_v, pv, in_v, uq_v, ukv_v, out_v))
```

```python
import functools
import math

import jax
import jax.numpy as jnp
from jax import lax
from jax.experimental import pallas as pl
from jax.experimental.pallas import tpu as pltpu

F32 = jnp.float32
BF16 = jnp.bfloat16

N_DEV = 8
D_MODEL = 1024
HEADS = 8
SB_WIDTH = 512
MLA_WIDTH = 512
Q_LORA = 384
KV_LORA = 256
ROPE = 32
NOPE = 64
MLA_QK = 96
LANES = 128
IN_COLS = 3232
IN_COLS_R = 3328
EPS = 1e-6
NEG = -1e30

ADAM_LR = 0.001
ADAM_B1 = 0.9
ADAM_B2 = 0.999
ADAM_EPS = 1e-08
ADAM_WD = 0.01
ADAM_STEP = 10

TS = 512
TS_FWD = 256
TQ = 512
KEY_UNROLL = 3
TN_S = 512
VMEM_BIG = 56 * 1024 * 1024

PK_NORM, PK_QLN, PK_KVLN, PK_QHN, PK_KHN, PK_ADA, PK_LOSS, PK_END = 0, 1024, 1408, 1664, 1792, 1920, 4992, 6016

MESH_ID = pl.DeviceIdType.MESH


def _dot_nn(a, b):
    return lax.dot_general(a, b, (((1,), (0,)), ((), ())), preferred_element_type=F32)


def _dot_nt(a, b):
    return lax.dot_general(a, b, (((1,), (1,)), ((), ())), preferred_element_type=F32)


def _dot_tn(a, b):
    return lax.dot_general(a, b, (((0,), (0,)), ((), ())), preferred_element_type=F32)


def _split_bf16(a):
    hi = a.astype(BF16)
    lo = (a - hi.astype(F32)).astype(BF16)
    return hi, lo


def _dot3(a, b):
    ah, al = _split_bf16(a)
    bh, bl = _split_bf16(b)
    return _dot_nn(ah, bh) + _dot_nn(ah, bl) + _dot_nn(al, bh)


def _sigmoid(g):
    return 1.0 / (1.0 + jnp.exp(-g))


def _silu(g):
    return g * _sigmoid(g)


def _lane_iota(shape):
    return lax.broadcasted_iota(jnp.int32, shape, len(shape) - 1)


def _adamw(w, g, m, v):
    m = ADAM_B1 * m + (1.0 - ADAM_B1) * g
    v = ADAM_B2 * v + (1.0 - ADAM_B2) * (g * g)
    m_hat = m / (1.0 - ADAM_B1 ** ADAM_STEP)
    v_hat = v / (1.0 - ADAM_B2 ** ADAM_STEP)
    delta = -ADAM_LR * (m_hat / (jnp.sqrt(v_hat) + ADAM_EPS) + ADAM_WD * w)
    return delta, m, v


def _mesh_pos():
    return lax.axis_index("x"), lax.axis_index("y"), lax.axis_index("c")


def _peer(pos, k):
    x, y, c = pos
    return (1 - x if k & 4 else x, 1 - y if k & 2 else y, 1 - c if k & 1 else c)


def _lin(pos):
    return 4 * pos[0] + 2 * pos[1] + pos[2]


def _remote(src, dst, send_sems, recv_sems, idx, peer):
    return pltpu.make_async_remote_copy(
        src_ref=src, dst_ref=dst, send_sem=send_sems.at[idx], recv_sem=recv_sems.at[idx],
        device_id=peer, device_id_type=MESH_ID)


def _all_gather_start(pos, src, buf, send_sems, recv_sems, base):
    me = _lin(pos)
    sent = []
    for k in range(1, N_DEV):
        cp = _remote(src, buf.at[me], send_sems, recv_sems, base + k - 1, _peer(pos, k))
        cp.start()
        sent.append(cp)
    return sent


def _all_gather_wait(pos, src, buf, send_sems, recv_sems, base, sent):
    for k in range(1, N_DEV):
        peer = _peer(pos, k)
        _remote(src, buf.at[_lin(peer)], send_sems, recv_sems, base + k - 1, peer).wait_recv()
    for cp in sent:
        cp.wait_send()


def _all_gather(pos, src, buf, send_sems, recv_sems, base):
    sent = _all_gather_start(pos, src, buf, send_sems, recv_sems, base)
    _all_gather_wait(pos, src, buf, send_sems, recv_sems, base, sent)


def _all_to_all_start(pos, src, buf, send_sems, recv_sems, base):
    me = _lin(pos)
    sent = []
    for k in range(1, N_DEV):
        peer = _peer(pos, k)
        cp = _remote(src.at[_lin(peer)], buf.at[me], send_sems, recv_sems, base + k - 1, peer)
        cp.start()
        sent.append(cp)
    return sent


def _all_to_all_wait(pos, src, buf, send_sems, recv_sems, base, sent):
    me = _lin(pos)
    for k in range(1, N_DEV):
        peer = _peer(pos, k)
        _remote(src.at[me], buf.at[_lin(peer)], send_sems, recv_sems, base + k - 1, peer).wait_recv()
    for cp in sent:
        cp.wait_send()


def _all_to_all(pos, src, buf, send_sems, recv_sems, base):
    sent = _all_to_all_start(pos, src, buf, send_sems, recv_sems, base)
    _all_to_all_wait(pos, src, buf, send_sems, recv_sems, base, sent)


def _two_level_gather(pos, bufs, send_sems, recv_sems):
    x, y, c = pos
    me, sibling = (x, y, c), (x, y, 1 - c)
    chips = [(1 - x, y), (x, 1 - y), (1 - x, 1 - y)]

    def copy(a, k, block, to):
        slot = bufs[a].at[_lin(block)]
        return _remote(slot, slot, send_sems, recv_sems, 7 * a + k, to)

    started = []
    for a in range(len(bufs)):
        first = [copy(a, 0, me, sibling)] + [copy(a, 1 + j, me, (*chip, c)) for j, chip in enumerate(chips)]
        for cp in first:
            cp.start()
        started += first
    for a in range(len(bufs)):
        for j, chip in enumerate(chips):
            copy(a, 1 + j, (*chip, c), me).wait_recv()
            passed = copy(a, 4 + j, (*chip, c), sibling)
            passed.start()
            started.append(passed)
    for a in range(len(bufs)):
        copy(a, 0, sibling, me).wait_recv()
        for j, chip in enumerate(chips):
            copy(a, 4 + j, (*chip, 1 - c), me).wait_recv()
    for cp in started:
        cp.wait_send()


def _vmem_spec():
    return pl.BlockSpec(memory_space=pltpu.VMEM)


def _any_spec():
    return pl.BlockSpec(memory_space=pl.ANY)


def _row_select(slots, n):
    r = lax.broadcasted_iota(jnp.int32, (N_DEV, n), 0)
    out = jnp.zeros((N_DEV, n), F32)
    for j in range(N_DEV):
        out = out + jnp.where(r == j, slots[j], 0.0)
    return out


def _ada_fwd(c8, w_ada, b_ada8):
    n_sh = w_ada.shape[1]

    def body(c_ref, w_ref, b_ref, out_ref, call_out_ref, call_ref, psend_ref, precv_ref, ssem, rsem):
        pos = _mesh_pos()
        me = _lin(pos)
        call_ref[me] = c_ref[...]
        _all_gather(pos, c_ref, call_ref, ssem, rsem, 0)
        call_out_ref[...] = _row_select([call_ref[j] for j in range(N_DEV)], D_MODEL)
        w = w_ref[...]
        for j in range(N_DEV):
            psend_ref[j] = _dot3(_silu(call_ref[j]), w)
        precv_ref[me] = psend_ref[me]
        _all_to_all(pos, psend_ref, precv_ref, ssem, rsem, N_DEV - 1)
        out_ref[...] = _row_select([precv_ref[j] for j in range(N_DEV)], n_sh) + b_ref[...]

    return pl.pallas_call(
        body, name="ada_fwd",
        out_shape=[jax.ShapeDtypeStruct((N_DEV, n_sh), F32), jax.ShapeDtypeStruct((N_DEV, D_MODEL), F32)],
        in_specs=[_vmem_spec()] * 3, out_specs=[_vmem_spec()] * 2,
        scratch_shapes=[
            pltpu.VMEM((N_DEV, 8, D_MODEL), F32),
            pltpu.VMEM((N_DEV, 8, n_sh), F32),
            pltpu.VMEM((N_DEV, 8, n_sh), F32),
            pltpu.SemaphoreType.DMA((2 * (N_DEV - 1),)),
            pltpu.SemaphoreType.DMA((2 * (N_DEV - 1),)),
        ],
    )(c8, w_ada, b_ada8)


def _gather_weights(shards):
    n = len(shards)

    def body(*refs):
        ins, outs = refs[:n], refs[n:2 * n]
        ssem, rsem = refs[2 * n], refs[2 * n + 1]
        pos = _mesh_pos()
        me = _lin(pos)
        for a in range(n):
            outs[a][me] = ins[a][...].astype(BF16)
        _two_level_gather(pos, outs, ssem, rsem)

    return pl.pallas_call(
        body, name="gather_weights",
        out_shape=[jax.ShapeDtypeStruct((N_DEV,) + s.shape, BF16) for s in shards],
        in_specs=[_vmem_spec()] * n, out_specs=[_vmem_spec()] * n,
        scratch_shapes=[
            pltpu.SemaphoreType.DMA((n * (N_DEV - 1),)),
            pltpu.SemaphoreType.DMA((n * (N_DEV - 1),)),
        ],
        compiler_params=pltpu.CompilerParams(vmem_limit_bytes=VMEM_BIG),
    )(*shards)


def _rope(t, cosf, sin_a, sin_b):
    return t * cosf + pltpu.roll(t, 112, 1) * sin_a + pltpu.roll(t, 16, 1) * sin_b


def _expand_rope(table):
    lane = _lane_iota(table.shape)
    by64, by32 = pltpu.roll(table, 64, 1), pltpu.roll(table, 32, 1)
    cosf = jnp.where(jnp.logical_and(lane >= NOPE, lane < MLA_QK), by64, 1.0)
    sin_a = jnp.where(jnp.logical_and(lane >= NOPE, lane < NOPE + ROPE // 2), by32, 0.0)
    sin_b = jnp.where(jnp.logical_and(lane >= NOPE + ROPE // 2, lane < MLA_QK), by32, 0.0)
    return cosf, sin_a, sin_b


def _rope_t(d, cosf, sin_a, sin_b):
    return d * cosf + pltpu.roll(d * sin_a, 16, 1) + pltpu.roll(d * sin_b, 112, 1)


def _head_rms(t):
    return lax.rsqrt(jnp.sum(t * t, axis=1, keepdims=True) * (1.0 / MLA_QK) + EPS)


def _rms_bwd(dxhat_w, xhat, r, n):
    return r * (dxhat_w - xhat * (jnp.sum(dxhat_w * xhat, axis=1, keepdims=True) * (1.0 / n)))


def _mla_latents(cq, ckv, kr, qln, kvln, wuq, wukv):
    rq = lax.rsqrt(jnp.mean(cq * cq, axis=1, keepdims=True) + EPS)
    rkv = lax.rsqrt(jnp.mean(ckv * ckv, axis=1, keepdims=True) + EPS)
    cq_hat = cq * rq
    ckv_hat = ckv * rkv
    cqn = (cq_hat * qln).astype(BF16)
    ckvn = (ckv_hat * kvln).astype(BF16)
    q_all = _dot_nn(cqn, wuq)
    kv = _dot_nn(ckvn, wukv)
    kr64 = pltpu.roll(kr, 64, 1)
    return rq, rkv, cq_hat, ckv_hat, cqn, ckvn, q_all, kv, kr64


def _fwd_pre(x, shift, scale, norm_w, w_in_r, qln, wuq, kvln, wukv, qhn, khn, rope_table):
    s_len = x.shape[0]

    def body(x_ref, shift_ref, scale_ref, nw_ref, w_ref, qln_ref, wuq_ref, kvln_ref, wukv_ref,
             qhn_ref, khn_ref, rt_ref,
             hb_ref, qkv_ref, gsb_ref, cq_ref, ckv_ref, gmla_ref, kr_ref, qm_ref, km_ref, vm_ref,
             qsbt_ref, qmt_ref):
        xv = x_ref[...]
        r = lax.rsqrt(jnp.mean(xv * xv, axis=1, keepdims=True) + EPS)
        h = (xv * r) * nw_ref[...] * (1.0 + scale_ref[...]) + shift_ref[...]
        hb = h.astype(BF16)
        hb_ref[...] = hb
        qkv = _dot_nn(hb, w_ref[:, 0:1536])
        qkv_ref[...] = qkv.astype(BF16)
        qsbt_ref[...] = qkv[:, :SB_WIDTH].T.astype(BF16)
        gsb_ref[...] = _dot_nn(hb, w_ref[:, 1536:2048])
        cq = _dot_nn(hb, w_ref[:, 2048:2432])
        ckv = _dot_nn(hb, w_ref[:, 2432:2688])
        gmla_ref[...] = _dot_nn(hb, w_ref[:, 2688:3200])
        kr = _dot_nn(hb, w_ref[:, 3200:3328])
        cq_ref[...] = cq
        ckv_ref[...] = ckv
        kr_ref[...] = kr
        _, _, _, _, _, _, q_all, kv, kr64 = _mla_latents(
            cq, ckv, kr, qln_ref[...], kvln_ref[...], wuq_ref[...], wukv_ref[...])
        cosf, sa, sb = _expand_rope(rt_ref[...])
        qhn_v, khn_v = qhn_ref[...], khn_ref[...]
        low = _lane_iota((TS_FWD, LANES)) < NOPE
        blks = [slice(LANES * hd, LANES * (hd + 1)) for hd in range(HEADS)]
        q_raw = [q_all[:, b] for b in blks]
        k_raw = [jnp.where(low, kv[:, b], kr64) for b in blks]
        q_rms = [_head_rms(t) for t in q_raw]
        k_rms = [_head_rms(t) for t in k_raw]
        q_n = [t * r * (qhn_v * MLA_SCALE) for t, r in zip(q_raw, q_rms)]
        q_roped = [_rope(t, cosf, sa, sb) for t in q_n]
        kr_roped = _rope(kr64 * khn_v, cosf, sa, sb)
        k_roped = [jnp.where(low, t * khn_v, kr_roped) * r for t, r in zip(k_raw, k_rms)]
        for hd, b in enumerate(blks):
            qm_ref[:, b] = q_roped[hd].astype(BF16)
            qmt_ref[b, :] = q_roped[hd].T.astype(BF16)
            km_ref[:, b] = k_roped[hd].astype(BF16)
        for p in range(HEADS // 2):
            even = kv[:, LANES * 2 * p:LANES * (2 * p + 1)]
            odd = kv[:, LANES * (2 * p + 1):LANES * (2 * p + 2)]
            vm_ref[:, LANES * p:LANES * (p + 1)] = jnp.where(low, pltpu.roll(even, 64, 1), odd).astype(BF16)

    def tile(width):
        return pl.BlockSpec((TS_FWD, width), lambda i: (i, 0))

    def full(a):
        return pl.BlockSpec(a.shape, lambda i: (0, 0))

    out_widths = [(D_MODEL, BF16), (1536, BF16), (512, F32), (Q_LORA, F32), (KV_LORA, F32),
                  (512, F32), (LANES, F32), (1024, BF16), (1024, BF16), (512, BF16)]
    t_heights = [SB_WIDTH, HEADS * LANES]
    return pl.pallas_call(
        body, name="fwd_pre", grid=(s_len // TS_FWD,),
        out_shape=[jax.ShapeDtypeStruct((s_len, w), dt) for w, dt in out_widths]
        + [jax.ShapeDtypeStruct((hgt, s_len), BF16) for hgt in t_heights],
        in_specs=[tile(D_MODEL), full(shift), full(scale), full(norm_w), full(w_in_r), full(qln), full(wuq),
                  full(kvln), full(wukv), full(qhn), full(khn), tile(LANES)],
        out_specs=[tile(w) for w, _ in out_widths]
        + [pl.BlockSpec((hgt, TS_FWD), lambda i: (0, i)) for hgt in t_heights],
        compiler_params=pltpu.CompilerParams(dimension_semantics=("parallel",), vmem_limit_bytes=VMEM_BIG),
    )(x, shift, scale, norm_w, w_in_r, qln, wuq, kvln, wukv, qhn, khn, rope_table)


CUM_W = 256


def _tri(later):
    j = lax.broadcasted_iota(jnp.int32, (CUM_W, CUM_W), 0)
    s = lax.broadcasted_iota(jnp.int32, (CUM_W, CUM_W), 1)
    return (j > s if later else j < s).astype(BF16)


def _suffix_sums(a, a_bf16, tri_m, carry):
    n = a.shape[1] // CUM_W
    outs = [None] * n
    for i in reversed(range(n)):
        cols = slice(CUM_W * i, CUM_W * (i + 1))
        outs[i] = _dot_nn(a_bf16[:, cols], tri_m) + carry
        carry = carry + _rowsum(a[:, cols])
    return (outs[0] if n == 1 else jnp.concatenate(outs, axis=1)), carry


def _sb_weights(qm, kb, carry, tri_u, diag):
    z = _dot_nt(qm, kb)
    nz = -z
    lk = jnp.minimum(nz, 0.0) - jnp.log(1.0 + jnp.exp(jnp.minimum(z, nz)))
    if diag:
        t = lax.broadcasted_iota(jnp.int32, (TQ, TQ), 0)
        s = lax.broadcasted_iota(jnp.int32, (TQ, TQ), 1)
        valid = s < t
        lk = jnp.where(valid, lk, 0.0)
    lk_hi = lk.astype(BF16)
    log_beta = z + lk
    after, carry = _suffix_sums(lk, lk_hi, tri_u, carry)
    logw = log_beta + after
    if diag:
        logw = jnp.where(valid, logw, NEG)
    return lk_hi, jnp.exp(logw), carry


SB_SCALE = 0.125


def _head_masks():
    lane = _lane_iota((1, LANES))
    return [lane < 64, lane >= 64]


def _masked(hm, a):
    return jnp.where(hm, a, jnp.zeros_like(a))


def _rowsum(a):
    return jnp.sum(a, axis=1, keepdims=True)


def _key_rows(kj):
    return pl.ds(pl.multiple_of(kj * TQ, TQ), TQ)


def _over_key_tiles(count, fn, st, ascending):
    n_full = count // KEY_UNROLL
    n_rest = count - n_full * KEY_UNROLL

    def group(g, s_):
        return fn([g * KEY_UNROLL + (u if ascending else KEY_UNROLL - 1 - u) for u in range(KEY_UNROLL)], s_)

    def left_over(s_):
        for r in range(1, KEY_UNROLL):
            tiles = [n_full * KEY_UNROLL + u if ascending else count - 1 - u for u in range(r)]
            s_ = lax.cond(n_rest == r, lambda a, tiles=tiles: fn(tiles, a), lambda a: a, s_)
        return s_

    if ascending:
        return left_over(lax.fori_loop(0, n_full, group, st))
    return lax.fori_loop(0, n_full, lambda i, s_: group(n_full - 1 - i, s_), left_over(st))


STAGE_SLOTS = 2 * KEY_UNROLL


def _stage_copies(to_hbm, hbm_refs, scr_refs, sems, pair, qi, kj):
    slot = (qi - kj) % STAGE_SLOTS
    out = []
    for h in range(2):
        for a in range(2):
            hbm, scr = hbm_refs[a].at[2 * pair + h, qi, kj], scr_refs[a].at[slot, h]
            sem = sems.at[4 * slot + 2 * h + a]
            out.append(pltpu.make_async_copy(scr, hbm, sem) if to_hbm else pltpu.make_async_copy(hbm, scr, sem))
    return out


def _sb_fwd(qkv):
    s_len = qkv.shape[0]
    nq = s_len // TQ

    def body(q_ref, k_ref, v_ref, o_ref, w_hbm, l_hbm, w_scr, l_scr, sems):
        pair, qi = pl.program_id(0), pl.program_id(1)
        q = q_ref[...]
        tri_u = _tri(True)
        masks = _head_masks()
        qms = [_masked(hm, q) * SB_SCALE for hm in masks]

        def copies(kj, of_qi=qi):
            return _stage_copies(True, (w_hbm, l_hbm), (w_scr, l_scr), sems, pair, of_qi, kj)

        def drain(of_qi):
            for kj in range(STAGE_SLOTS):
                @pl.when(kj <= of_qi)
                def _():
                    for cp in copies(kj, of_qi):
                        cp.wait()

        def block(kj, st, diag, before_staging=None):
            rows = _key_rows(kj)
            slot = (qi - kj) % STAGE_SLOTS
            kb, vb = k_ref[rows, :], v_ref[rows, :]
            carries, acc = list(st[:2]), st[2]
            staged = []
            for h in range(2):
                lk_hi, w, carries[h] = _sb_weights(qms[h], kb, carries[h], tri_u, diag)
                wb = w.astype(BF16)
                staged.append((wb, lk_hi))
                acc = acc + _dot_nn(wb, _masked(masks[h], vb))
            if before_staging is not None:
                before_staging()
            for h in range(2):
                w_scr[slot, h], l_scr[slot, h] = staged[h]
            return carries[0], carries[1], acc

        def trip(tiles, st):
            for kj in tiles:
                @pl.when(qi - kj >= STAGE_SLOTS)
                def _():
                    for cp in copies(kj + STAGE_SLOTS):
                        cp.wait()
            for kj in tiles:
                st = block(kj, st, False)
            for kj in tiles:
                for cp in copies(kj):
                    cp.start()
            return st

        def drain_previous_step():
            @pl.when(jnp.logical_or(pair > 0, qi > 0))
            def _():
                drain(jnp.where(qi == 0, nq - 1, qi - 1))

        zc = jnp.zeros((TQ, 1), F32)
        st = block(qi, (zc, zc, jnp.zeros((TQ, LANES), F32)), True, drain_previous_step)
        for cp in copies(qi):
            cp.start()
        st = _over_key_tiles(qi, trip, st, ascending=False)
        o_ref[...] = st[2]

        @pl.when(jnp.logical_and(pair == HEADS // 2 - 1, qi == nq - 1))
        def _():
            drain(qi)

    saved = jax.ShapeDtypeStruct((HEADS, nq, nq, TQ, TQ), BF16)
    stage = pltpu.VMEM((STAGE_SLOTS, 2, TQ, TQ), BF16)
    return pl.pallas_call(
        body, name="sb_fwd", grid=(HEADS // 2, nq),
        out_shape=[jax.ShapeDtypeStruct((s_len, SB_WIDTH), F32), saved, saved],
        in_specs=[pl.BlockSpec((TQ, LANES), lambda p, i: (i, p)),
                  pl.BlockSpec((s_len, LANES), lambda p, i: (0, 4 + p)),
                  pl.BlockSpec((s_len, LANES), lambda p, i: (0, 8 + p))],
        out_specs=[pl.BlockSpec((TQ, LANES), lambda p, i: (i, p)), _any_spec(), _any_spec()],
        scratch_shapes=[stage, stage, pltpu.SemaphoreType.DMA((4 * STAGE_SLOTS,))],
        compiler_params=pltpu.CompilerParams(dimension_semantics=("arbitrary", "arbitrary"),
                                             vmem_limit_bytes=VMEM_BIG),
    )(qkv, qkv, qkv)


def _prefix_sums(a, tri_m, carry):
    n = a.shape[1] // CUM_W
    outs = [None] * n
    for i in range(n):
        cols = slice(CUM_W * i, CUM_W * (i + 1))
        outs[i] = _dot_nn(a[:, cols].astype(BF16), tri_m) + carry
        carry = carry + _rowsum(a[:, cols])
    return (outs[0] if n == 1 else jnp.concatenate(outs, axis=1)), carry


def _head_rows(a0, a1):
    sub = lax.broadcasted_iota(jnp.int32, a0.shape, 0)
    return jnp.where(sub < NOPE, a0, a1)


def _sb_bwd(qkv, q_t, do, do_t, w_saved, l_saved):
    s_len = qkv.shape[0]
    nq = s_len // TQ

    def body(qt_ref, k_ref, v_ref, do_ref, dot_ref, w_hbm, l_hbm, dq_ref, dk_ref, dv_ref, w_scr, l_scr, sems):
        pair, qi = pl.program_id(0), pl.program_id(1)

        def copies(kj, of_pair=pair, of_qi=qi):
            return _stage_copies(False, (w_hbm, l_hbm), (w_scr, l_scr), sems, of_pair, of_qi, kj)

        def start_first_tiles(of_pair, of_qi):
            for first in range(KEY_UNROLL):
                @pl.when(first <= of_qi)
                def _():
                    for cp in copies(first, of_pair, of_qi):
                        cp.start()

        @pl.when(jnp.logical_and(pair == 0, qi == 0))
        def _():
            start_first_tiles(pair, qi)

        @pl.when(qi == 0)
        def _():
            dk_ref[...] = jnp.zeros_like(dk_ref)
            dv_ref[...] = jnp.zeros_like(dv_ref)

        qt = qt_ref[...] * SB_SCALE
        dot_v = dot_ref[...]
        do_v = do_ref[...]
        tri_before = _tri(False)
        masks = _head_masks()
        doms = [_masked(hm, do_v) for hm in masks]

        def block(kj, st):
            rows = _key_rows(kj)
            slot = (qi - kj) % STAGE_SLOTS
            kb, vb = k_ref[rows, :], v_ref[rows, :]
            carries, dqs = list(st[0:2]), list(st[2:4])
            dk_t, dv_t = [], []
            for h in range(2):
                wb = w_scr[slot, h]
                d_l = _dot_nt(doms[h], vb) * wb.astype(F32)
                before, carries[h] = _prefix_sums(d_l, tri_before, carries[h])
                keep = jnp.exp(l_scr[slot, h].astype(F32))
                dzb = (d_l * keep - before * (1.0 - keep)).astype(BF16)
                dk_t.append(_dot_nn(qt, dzb))
                dv_t.append(_dot_nn(dot_v, wb))
                dqs[h] = dqs[h] + _dot_nn(dzb, kb)
            dk_ref[kj] += _head_rows(*dk_t)
            dv_ref[kj] += _head_rows(*dv_t)
            return (*carries, *dqs)

        def trip(tiles, st):
            for kj in tiles:
                @pl.when(kj + KEY_UNROLL <= qi)
                def _():
                    for cp in copies(kj + KEY_UNROLL):
                        cp.start()
            for kj in tiles:
                for cp in copies(kj):
                    cp.wait()
            for kj in tiles:
                st = block(kj, st)
            return st

        zc = jnp.zeros((TQ, 1), F32)
        za = jnp.zeros((TQ, LANES), F32)
        st = _over_key_tiles(qi + 1, trip, (zc, zc, za, za), ascending=True)
        dq_ref[...] = jnp.where(masks[0], st[2], st[3]) * SB_SCALE

        @pl.when(jnp.logical_or(pair < HEADS // 2 - 1, qi < nq - 1))
        def _():
            wraps = qi == nq - 1
            start_first_tiles(jnp.where(wraps, pair + 1, pair), jnp.where(wraps, 0, qi + 1))

    tile = pl.BlockSpec((TQ, LANES), lambda p, i: (i, p))
    tile_t = pl.BlockSpec((LANES, TQ), lambda p, i: (p, i))
    col_t = pl.BlockSpec((nq, LANES, TQ), lambda p, i: (0, p, 0))
    stage = pltpu.VMEM((STAGE_SLOTS, 2, TQ, TQ), BF16)
    key_t = jax.ShapeDtypeStruct((nq, SB_WIDTH, TQ), F32)
    return pl.pallas_call(
        body, name="sb_bwd", grid=(HEADS // 2, nq),
        out_shape=[jax.ShapeDtypeStruct((s_len, SB_WIDTH), F32), key_t, key_t],
        in_specs=[tile_t,
                  pl.BlockSpec((s_len, LANES), lambda p, i: (0, 4 + p)),
                  pl.BlockSpec((s_len, LANES), lambda p, i: (0, 8 + p)),
                  tile, tile_t, _any_spec(), _any_spec()],
        out_specs=[tile, col_t, col_t],
        scratch_shapes=[stage, stage, pltpu.SemaphoreType.DMA((4 * STAGE_SLOTS,))],
        compiler_params=pltpu.CompilerParams(dimension_semantics=("arbitrary", "arbitrary"),
                                             vmem_limit_bytes=VMEM_BIG),
    )(q_t, qkv, qkv, do, do_t, w_saved, l_saved)


def _from_key_tiles(a_t):
    return a_t.transpose(0, 2, 1).reshape(a_t.shape[0] * a_t.shape[2], a_t.shape[1])


MLA_SCALE = 1.0 / math.sqrt(MLA_QK)


def _causal_mask():
    t = lax.broadcasted_iota(jnp.int32, (TQ, TQ), 0)
    s = lax.broadcasted_iota(jnp.int32, (TQ, TQ), 1)
    return s <= t


def _head_lanes(h):
    return slice(LANES * h, LANES * (h + 1))


P_SLOTS = 2 * KEY_UNROLL
P_COLS = TQ + LANES


def _tile_number(pair, qi, kj, nq):
    return pair * (nq * (nq + 1) // 2) + (qi * (qi + 1)) // 2 + kj


def _p_copy(to_hbm, p_hbm, p_scr, sems, pair, qi, kj, nq, h):
    slot = _tile_number(pair, qi, kj, nq) % P_SLOTS
    hbm, scr, sem = p_hbm.at[2 * pair + h, qi, kj], p_scr.at[slot, h], sems.at[2 * slot + h]
    return pltpu.make_async_copy(scr, hbm, sem) if to_hbm else pltpu.make_async_copy(hbm, scr, sem)


def _mla_fwd(qm, km, vm):
    s_len = qm.shape[0]
    nq = s_len // TQ

    def body(q_ref, k_ref, v_ref, o_ref, lse_ref, p_hbm, p_scr, sems):
        pair, qi = pl.program_id(0), pl.program_id(1)
        masks = _head_masks()
        qhs = [q_ref[:, _head_lanes(h)] for h in range(2)]
        lane = _lane_iota((TQ, LANES))

        def number(kj):
            return _tile_number(pair, qi, kj, nq)

        def copies(kj):
            return [_p_copy(True, p_hbm, p_scr, sems, pair, qi, kj, nq, h) for h in range(2)]

        def two_parts(m):
            hi = m.astype(BF16).astype(F32)
            return jnp.where(lane < 64, hi, m - hi).astype(BF16)

        def block(kj, st, diag):
            rows = _key_rows(kj)
            slot = number(kj) % P_SLOTS
            vb = v_ref[rows, :]
            ms, ls, acc = list(st[0:2]), list(st[2:4]), st[4]
            alphas, pvs = [], []
            for h in range(2):
                s = _dot_nt(qhs[h], k_ref[rows, _head_lanes(h)])
                if diag:
                    s = jnp.where(_causal_mask(), s, NEG)
                m_new = jnp.maximum(ms[h], jnp.max(s, axis=1, keepdims=True))
                p = jnp.exp(s - m_new)
                pb = p.astype(BF16)
                p_scr[slot, h, :, 0:TQ] = pb
                p_scr[slot, h, :, TQ:P_COLS] = two_parts(m_new)
                alphas.append(jnp.exp(ms[h] - m_new))
                ls[h] = alphas[h] * ls[h] + _rowsum(p)
                ms[h] = m_new
                pvs.append(_dot_nn(pb, _masked(masks[h], vb)))
            acc = jnp.where(masks[0], alphas[0], alphas[1]) * acc + pvs[0] + pvs[1]
            return (*ms, *ls, acc)

        def trip(tiles, st, diag=False):
            for kj in tiles:
                @pl.when(number(kj) >= P_SLOTS)
                def _():
                    for cp in copies(kj):
                        cp.wait()
            for kj in tiles:
                st = block(kj, st, diag)
            for kj in tiles:
                for cp in copies(kj):
                    cp.start()
            return st

        neg = jnp.full((TQ, 1), NEG, F32)
        zc = jnp.zeros((TQ, 1), F32)
        st = (neg, neg, zc, zc, jnp.zeros((TQ, LANES), F32))
        st = _over_key_tiles(qi, trip, st, ascending=True)
        m0, m1, l0, l1, acc = trip([qi], st, True)
        o_ref[...] = acc / jnp.where(masks[0], l0, l1)
        lse_ref[0] = m0 + jnp.log(l0)
        lse_ref[1] = m1 + jnp.log(l1)

        @pl.when(jnp.logical_and(pair == HEADS // 2 - 1, qi == nq - 1))
        def _():
            for slot in range(P_SLOTS):
                for h in range(2):
                    pltpu.make_async_copy(p_scr.at[slot, h], p_hbm.at[0, 0, 0], sems.at[2 * slot + h]).wait()

    return pl.pallas_call(
        body, name="mla_fwd", grid=(HEADS // 2, nq),
        out_shape=[jax.ShapeDtypeStruct((s_len, MLA_WIDTH), F32),
                   jax.ShapeDtypeStruct((HEADS, s_len, 1), F32),
                   jax.ShapeDtypeStruct((HEADS, nq, nq, TQ, P_COLS), BF16)],
        in_specs=[pl.BlockSpec((TQ, 2 * LANES), lambda p, i: (i, p)),
                  pl.BlockSpec((s_len, 2 * LANES), lambda p, i: (0, p)),
                  pl.BlockSpec((s_len, LANES), lambda p, i: (0, p))],
        out_specs=[pl.BlockSpec((TQ, LANES), lambda p, i: (i, p)),
                   pl.BlockSpec((2, TQ, 1), lambda p, i: (p, i, 0)),
                   _any_spec()],
        scratch_shapes=[pltpu.VMEM((P_SLOTS, 2, TQ, P_COLS), BF16), pltpu.SemaphoreType.DMA((2 * P_SLOTS,))],
        compiler_params=pltpu.CompilerParams(dimension_semantics=("arbitrary", "arbitrary"),
                                             vmem_limit_bytes=VMEM_BIG),
    )(qm, km, vm)


def _mla_bwd(qm_t, km, vm, o, do, do_t, lse, p_saved):
    s_len = km.shape[0]
    nq = s_len // TQ
    total = (HEADS // 2) * (nq * (nq + 1) // 2)

    def body(qt_ref, k_ref, v_ref, o_ref, do_ref, dot_ref, lse_ref, p_hbm, dq_ref, dk_ref, dv_ref, p_scr, sems):
        pair, qi = pl.program_id(0), pl.program_id(1)

        def number(kj):
            return _tile_number(pair, qi, kj, nq)

        def fetch(p_, q_, k_):
            for h in range(2):
                _p_copy(False, p_hbm, p_scr, sems, p_, q_, k_, nq, h).start()

        def advance(p_, q_, k_):
            row_end = k_ == q_
            last_row = q_ == nq - 1
            return (jnp.where(jnp.logical_and(row_end, last_row), p_ + 1, p_),
                    jnp.where(row_end, jnp.where(last_row, 0, q_ + 1), q_),
                    jnp.where(row_end, 0, k_ + 1))

        def ahead(tile, steps):
            for _ in range(steps):
                tile = advance(*tile)
            return tile

        @pl.when(jnp.logical_and(pair == 0, qi == 0))
        def _():
            for first in range(KEY_UNROLL):
                fetch(*ahead((pair, qi, 0), first))

        @pl.when(qi == 0)
        def _():
            dk_ref[...] = jnp.zeros_like(dk_ref)
            dv_ref[...] = jnp.zeros_like(dv_ref)

        do_v = do_ref[...]
        dot_v = dot_ref[...]
        od = o_ref[...] * do_v.astype(F32)
        masks = _head_masks()
        qts = [qt_ref[_head_lanes(h), :] for h in range(2)]
        doms = [_masked(hm, do_v) for hm in masks]
        deltas = [_rowsum(jnp.where(hm, od, 0.0)) for hm in masks]
        lses = [lse_ref[h] for h in range(2)]

        def block(kj, dqs):
            rows = _key_rows(kj)
            slot = number(kj) % P_SLOTS
            vb = v_ref[rows, :]
            dqs = list(dqs)
            dv_t = []
            for h in range(2):
                kb = k_ref[rows, _head_lanes(h)]
                tile_max = _rowsum(p_scr[slot, h, :, TQ:P_COLS].astype(F32)) * (1.0 / 64.0)
                p = p_scr[slot, h, :, 0:TQ].astype(F32) * jnp.exp(tile_max - lses[h])
                dp = _dot_nt(doms[h], vb)
                ds = (p * (dp - deltas[h])).astype(BF16)
                dk_ref[kj, _head_lanes(h), :] += _dot_nn(qts[h], ds)
                dv_t.append(_dot_nn(dot_v, p.astype(BF16)))
                dqs[h] = dqs[h] + _dot_nn(ds, kb)
            dv_ref[kj] += _head_rows(*dv_t)
            return tuple(dqs)

        def trip(tiles, dqs):
            for kj in tiles:
                @pl.when(number(kj) + KEY_UNROLL < total)
                def _():
                    fetch(*ahead((pair, qi, kj), KEY_UNROLL))
            for kj in tiles:
                for h in range(2):
                    _p_copy(False, p_hbm, p_scr, sems, pair, qi, kj, nq, h).wait()
            for kj in tiles:
                dqs = block(kj, dqs)
            return dqs

        za = jnp.zeros((TQ, LANES), F32)
        dqs = _over_key_tiles(qi + 1, trip, (za, za), ascending=True)
        dq_ref[:, _head_lanes(0)] = dqs[0] * MLA_SCALE
        dq_ref[:, _head_lanes(1)] = dqs[1] * MLA_SCALE

    return pl.pallas_call(
        body, name="mla_bwd", grid=(HEADS // 2, nq),
        out_shape=[jax.ShapeDtypeStruct((s_len, HEADS * LANES), F32),
                   jax.ShapeDtypeStruct((nq, HEADS * LANES, TQ), F32),
                   jax.ShapeDtypeStruct((nq, MLA_WIDTH, TQ), F32)],
        in_specs=[pl.BlockSpec((2 * LANES, TQ), lambda p, i: (p, i)),
                  pl.BlockSpec((s_len, 2 * LANES), lambda p, i: (0, p)),
                  pl.BlockSpec((s_len, LANES), lambda p, i: (0, p)),
                  pl.BlockSpec((TQ, LANES), lambda p, i: (i, p)),
                  pl.BlockSpec((TQ, LANES), lambda p, i: (i, p)),
                  pl.BlockSpec((LANES, TQ), lambda p, i: (p, i)),
                  pl.BlockSpec((2, TQ, 1), lambda p, i: (p, i, 0)),
                  _any_spec()],
        out_specs=[pl.BlockSpec((TQ, 2 * LANES), lambda p, i: (i, p)),
                   pl.BlockSpec((nq, 2 * LANES, TQ), lambda p, i: (0, p, 0)),
                   pl.BlockSpec((nq, LANES, TQ), lambda p, i: (0, p, 0))],
        scratch_shapes=[pltpu.VMEM((P_SLOTS, 2, TQ, P_COLS), BF16), pltpu.SemaphoreType.DMA((2 * P_SLOTS,))],
        compiler_params=pltpu.CompilerParams(dimension_semantics=("arbitrary", "arbitrary"),
                                             vmem_limit_bytes=VMEM_BIG),
    )(qm_t, km, vm, o, do, do_t, lse, p_saved)


def _mid(o_sb, g_sb, o_mla, g_mla, x, target, gate, w_out):
    s_len = x.shape[0]

    def body(osb_ref, gsb_ref, omla_ref, gmla_ref, x_ref, t_ref, gate_ref, w_ref,
             dy_ref, dosb_ref, dgsb_ref, domla_ref, dgmla_ref, gw_ref, dgate_ref, loss_ref, dosbt_ref, domlat_ref):
        @pl.when(pl.program_id(0) == 0)
        def _():
            gw_ref[...] = jnp.zeros_like(gw_ref)
            dgate_ref[...] = jnp.zeros_like(dgate_ref)
            loss_ref[...] = jnp.zeros_like(loss_ref)

        o1, g1, o2, g2 = osb_ref[...], gsb_ref[...], omla_ref[...], gmla_ref[...]
        s1, s2 = _sigmoid(g1), _sigmoid(g2)
        mixed = jnp.concatenate([o1 * (g1 * s1), o2 * (g2 * s2)], axis=1).astype(BF16)
        w = w_ref[...]
        gate_v = gate_ref[...]
        u = _dot_nn(mixed, w)
        err = x_ref[...] + gate_v * u - t_ref[...]
        loss_ref[...] += jnp.sum(err * err, axis=0, keepdims=True)
        dy = err * (1.0 / D_MODEL)
        dy_ref[...] = dy
        dgate_ref[...] += jnp.sum(dy * u, axis=0, keepdims=True)
        du = (dy * gate_v).astype(BF16)
        gw_ref[...] += _dot_tn(mixed, du)
        dmixed = _dot_nt(du, w)
        d1, d2 = dmixed[:, :SB_WIDTH], dmixed[:, SB_WIDTH:]
        do1, do2 = d1 * (g1 * s1), d2 * (g2 * s2)
        dosb_ref[...] = do1.astype(BF16)
        dgsb_ref[...] = (d1 * o1 * (s1 * (1.0 + g1 * (1.0 - s1)))).astype(BF16)
        domla_ref[...] = do2.astype(BF16)
        dgmla_ref[...] = (d2 * o2 * (s2 * (1.0 + g2 * (1.0 - s2)))).astype(BF16)
        dosbt_ref[...] = do1.T.astype(BF16)
        domlat_ref[...] = do2.T.astype(BF16)

    def tile(width):
        return pl.BlockSpec((TS, width), lambda i: (i, 0))

    def full(shape):
        return pl.BlockSpec(shape, lambda i: (0, 0))

    return pl.pallas_call(
        body, name="mid", grid=(s_len // TS,),
        out_shape=[jax.ShapeDtypeStruct((s_len, D_MODEL), F32)]
        + [jax.ShapeDtypeStruct((s_len, 512), BF16)] * 4
        + [jax.ShapeDtypeStruct((D_MODEL, D_MODEL), F32),
           jax.ShapeDtypeStruct((1, D_MODEL), F32), jax.ShapeDtypeStruct((1, D_MODEL), F32)]
        + [jax.ShapeDtypeStruct((512, s_len), BF16)] * 2,
        in_specs=[tile(512)] * 4 + [tile(D_MODEL), tile(D_MODEL), full((1, D_MODEL)), full((D_MODEL, D_MODEL))],
        out_specs=[tile(D_MODEL)] + [tile(512)] * 4
        + [full((D_MODEL, D_MODEL)), full((1, D_MODEL)), full((1, D_MODEL))]
        + [pl.BlockSpec((512, TS), lambda i: (0, i))] * 2,
        compiler_params=pltpu.CompilerParams(dimension_semantics=("arbitrary",), vmem_limit_bytes=VMEM_BIG),
    )(o_sb, g_sb, o_mla, g_mla, x, target, gate, w_out)


def _mla_pre_bwd(dq, dk, dv, cq, ckv, kr, qln, wuq, kvln, wukv, qhn, khn, rope_table):
    s_len = cq.shape[0]

    def body(dq_ref, dk_ref, dv_ref, cq_ref, ckv_ref, kr_ref, qln_ref, wuq_ref, kvln_ref, wukv_ref,
             qhn_ref, khn_ref, rt_ref,
             dcq_ref, dckv_ref, dkr_ref, gwuq_ref, gwukv_ref, gqhn_ref, gkhn_ref, gqln_ref, gkvln_ref,
             dqa_ref, dkv_ref):
        @pl.when(pl.program_id(0) == 0)
        def _():
            for r_ in (gwuq_ref, gwukv_ref, gqhn_ref, gkhn_ref, gqln_ref, gkvln_ref):
                r_[...] = jnp.zeros_like(r_)

        cq, ckv = cq_ref[...], ckv_ref[...]
        qln_v, kvln_v = qln_ref[...], kvln_ref[...]
        wuq_v, wukv_v = wuq_ref[...], wukv_ref[...]
        rq, rkv, cq_hat, ckv_hat, cqn, ckvn, q_all, kv, kr64 = _mla_latents(
            cq, ckv, kr_ref[...], qln_v, kvln_v, wuq_v, wukv_v)
        cosf, sa, sb = _expand_rope(rt_ref[...])
        qhn_v, khn_v = qhn_ref[...], khn_ref[...]
        lane = _lane_iota((TS, LANES))
        low = lane < NOPE
        blks = [slice(LANES * hd, LANES * (hd + 1)) for hd in range(HEADS)]
        raw = [q_all[:, b] for b in blks] + [jnp.where(low, kv[:, b], kr64) for b in blks]
        grads = [dq_ref[:, b] for b in blks] + [dk_ref[:, b] for b in blks]
        gains = [qhn_v] * HEADS + [khn_v] * HEADS
        rms = [_head_rms(t) for t in raw]
        xhs = [t * r for t, r in zip(raw, rms)]
        dns = [_rope_t(d, cosf, sa, sb) for d in grads]
        gain_g = [jnp.sum(dn * xh, axis=0, keepdims=True) for dn, xh in zip(dns, xhs)]
        dxs = [_rms_bwd(dn * g, xh, r, MLA_QK) for dn, g, xh, r in zip(dns, gains, xhs, rms)]
        dkr64 = jnp.zeros((TS, LANES), F32)
        for hd, b in enumerate(blks):
            dqa_ref[:, b] = dxs[hd].astype(BF16)
            dkb = dxs[HEADS + hd]
            dkr64 = dkr64 + jnp.where(low, 0.0, dkb)
            dvp = dv_ref[:, LANES * (hd // 2):LANES * (hd // 2 + 1)]
            dvh = pltpu.roll(dvp, 64, 1) if hd % 2 == 0 else dvp
            dkv_ref[:, b] = jnp.where(low, dkb, dvh).astype(BF16)
        gqhn_ref[...] += sum(gain_g[:HEADS])
        gkhn_ref[...] += sum(gain_g[HEADS:])
        dkr_ref[...] = pltpu.roll(dkr64, 64, 1).astype(BF16)

        dqa = dqa_ref[...]
        gwuq_ref[...] += _dot_tn(cqn, dqa)
        dcqn = _dot_nt(dqa, wuq_v)
        gqln_ref[...] += jnp.sum(dcqn * cq_hat, axis=0, keepdims=True)
        dcq_ref[...] = _rms_bwd(dcqn * qln_v, cq_hat, rq, Q_LORA).astype(BF16)

        dkv = dkv_ref[...]
        gwukv_ref[...] += _dot_tn(ckvn, dkv)
        dckvn = _dot_nt(dkv, wukv_v)
        gkvln_ref[...] += jnp.sum(dckvn * ckv_hat, axis=0, keepdims=True)
        dckv_ref[...] = _rms_bwd(dckvn * kvln_v, ckv_hat, rkv, KV_LORA).astype(BF16)

    def tile(width):
        return pl.BlockSpec((TS, width), lambda i: (i, 0))

    def full(shape):
        return pl.BlockSpec(shape, lambda i: (0, 0))

    acc_shapes = [(Q_LORA, 1024), (KV_LORA, 1024), (1, LANES), (1, LANES), (1, Q_LORA), (1, KV_LORA)]
    return pl.pallas_call(
        body, name="mla_pre_bwd", grid=(s_len // TS,),
        out_shape=[jax.ShapeDtypeStruct((s_len, Q_LORA), BF16), jax.ShapeDtypeStruct((s_len, KV_LORA), BF16),
                   jax.ShapeDtypeStruct((s_len, LANES), BF16)]
        + [jax.ShapeDtypeStruct(s, F32) for s in acc_shapes],
        in_specs=[tile(1024), tile(1024), tile(512), tile(Q_LORA), tile(KV_LORA), tile(LANES),
                  full(qln.shape), full(wuq.shape), full(kvln.shape), full(wukv.shape),
                  full(qhn.shape), full(khn.shape), tile(LANES)],
        out_specs=[tile(Q_LORA), tile(KV_LORA), tile(LANES)] + [full(s) for s in acc_shapes],
        scratch_shapes=[pltpu.VMEM((TS, 1024), BF16), pltpu.VMEM((TS, 1024), BF16)],
        compiler_params=pltpu.CompilerParams(dimension_semantics=("arbitrary",), vmem_limit_bytes=VMEM_BIG),
    )(dq, dk, dv, cq, ckv, kr, qln, wuq, kvln, wukv, qhn, khn, rope_table)


def _dproj_bwd(dq_sb, dk_sb, dv_sb, dg_sb, dcq, dckv, dg_mla, dkr, w_in_r, x, dy, norm_w, scale):
    s_len = x.shape[0]

    def body(dq_ref, dk_ref, dv_ref, dg_ref, dcq_ref, dckv_ref, dgm_ref, dkr_ref, w_ref, x_ref, dy_ref,
             nw_ref, scale_ref, dp_ref, gx_ref, dshift_ref, dscale_ref, dnw_ref):
        @pl.when(pl.program_id(0) == 0)
        def _():
            for r_ in (dshift_ref, dscale_ref, dnw_ref):
                r_[...] = jnp.zeros_like(r_)

        dp_ref[:, 0:512] = dq_ref[...].astype(BF16)
        dp_ref[:, 512:1024] = dk_ref[...].astype(BF16)
        dp_ref[:, 1024:1536] = dv_ref[...].astype(BF16)
        dp_ref[:, 1536:2048] = dg_ref[...]
        dp_ref[:, 2048:2432] = dcq_ref[...]
        dp_ref[:, 2432:2688] = dckv_ref[...]
        dp_ref[:, 2688:3200] = dgm_ref[...]
        dp_ref[:, 3200:3328] = dkr_ref[...]
        dh = _dot_nt(dp_ref[...], w_ref[...])
        xv = x_ref[...]
        r = lax.rsqrt(jnp.mean(xv * xv, axis=1, keepdims=True) + EPS)
        xh = xv * r
        nw = nw_ref[...]
        dshift_ref[...] += jnp.sum(dh, axis=0, keepdims=True)
        dscale_ref[...] += jnp.sum(dh * (xh * nw), axis=0, keepdims=True)
        dxnw = dh * (1.0 + scale_ref[...])
        dnw_ref[...] += jnp.sum(dxnw * xh, axis=0, keepdims=True)
        gx_ref[...] = dy_ref[...] + _rms_bwd(dxnw * nw, xh, r, D_MODEL)

    def tile(width):
        return pl.BlockSpec((TS, width), lambda i: (i, 0))

    def full(shape):
        return pl.BlockSpec(shape, lambda i: (0, 0))

    vec = (1, D_MODEL)
    return pl.pallas_call(
        body, name="dproj_bwd", grid=(s_len // TS,),
        out_shape=[jax.ShapeDtypeStruct((s_len, IN_COLS_R), BF16), jax.ShapeDtypeStruct((s_len, D_MODEL), F32)]
        + [jax.ShapeDtypeStruct(vec, F32)] * 3,
        in_specs=[tile(512)] * 4 + [tile(Q_LORA), tile(KV_LORA), tile(512), tile(LANES),
                                    full(w_in_r.shape), tile(D_MODEL), tile(D_MODEL), full(vec), full(vec)],
        out_specs=[tile(IN_COLS_R), tile(D_MODEL)] + [full(vec)] * 3,
        compiler_params=pltpu.CompilerParams(dimension_semantics=("arbitrary",), vmem_limit_bytes=VMEM_BIG),
    )(dq_sb, dk_sb, dv_sb, dg_sb, dcq, dckv, dg_mla, dkr, w_in_r, x, dy, norm_w, scale)


def _grad_w_in(hb, dproj):
    s_len = hb.shape[0]
    n_half = IN_COLS_R // 2
    n_steps = s_len // TN_S

    def body(h_ref, d_ref, g_ref, acc_ref):
        @pl.when(pl.program_id(1) == 0)
        def _():
            acc_ref[...] = jnp.zeros_like(acc_ref)

        acc_ref[...] += _dot_tn(h_ref[...], d_ref[...])

        @pl.when(pl.program_id(1) == n_steps - 1)
        def _():
            g_ref[...] = acc_ref[...].astype(BF16)

    return pl.pallas_call(
        body, name="grad_w_in", grid=(2, n_steps),
        out_shape=jax.ShapeDtypeStruct((D_MODEL, IN_COLS_R), BF16),
        in_specs=[pl.BlockSpec((TN_S, D_MODEL), lambda n, s: (s, 0)),
                  pl.BlockSpec((TN_S, n_half), lambda n, s: (s, n))],
        out_specs=pl.BlockSpec((D_MODEL, n_half), lambda n, s: (0, n)),
        scratch_shapes=[pltpu.VMEM((D_MODEL, n_half), F32)],
        compiler_params=pltpu.CompilerParams(dimension_semantics=("parallel", "arbitrary"),
                                             vmem_limit_bytes=VMEM_BIG),
    )(hb, dproj)


def _final_exchange(gpack, ccol, wpack, mpack, vpack, n_sh, grads):
    n = len(grads)

    def body(*refs):
        (g_ref, cc_ref, wp_ref, mp_ref, vp_ref) = refs[:5]
        slabs_in = refs[5:5 + n]
        (og_ref, od_ref, om_ref, ov_ref, ag_ref) = refs[5 + n:10 + n]
        slabs_out = refs[10 + n:10 + 2 * n]
        gall_ref, ssem, rsem, slab_ssem, slab_rsem, lsem = refs[10 + 2 * n:]
        pos = _mesh_pos()
        me = _lin(pos)
        gall_ref[me] = g_ref[...]
        small = _all_gather_start(pos, g_ref, gall_ref, ssem, rsem, 0)
        own = [pltpu.make_async_copy(slabs_in[a].at[me], slabs_out[a].at[me], lsem.at[a]) for a in range(n)]
        for cp in own:
            cp.start()
        in_flight = [_all_to_all_start(pos, slabs_in[a], slabs_out[a], slab_ssem, slab_rsem, a * (N_DEV - 1))
                     for a in range(n)]
        _all_gather_wait(pos, g_ref, gall_ref, ssem, rsem, 0, small)

        tot = gall_ref[0]
        for j in range(1, N_DEV):
            tot = tot + gall_ref[j]
        og_ref[...] = tot
        od_ref[...], om_ref[...], ov_ref[...] = _adamw(wp_ref[...], tot, mp_ref[...], vp_ref[...])

        ga = jnp.zeros((D_MODEL, n_sh), F32)
        for j in range(N_DEV):
            d_mine = jnp.zeros((8, n_sh), F32)
            for k in range(N_DEV):
                d_mine = d_mine + jnp.where(me == k, gall_ref[j, :, PK_ADA + n_sh * k:PK_ADA + n_sh * (k + 1)], 0.0)
            col = _silu(cc_ref[j])
            ga = ga + jnp.concatenate(
                [col * d_mine[0:1, LANES * a:LANES * (a + 1)] for a in range(n_sh // LANES)], axis=1)
        ag_ref[...] = ga

        for a in range(n):
            _all_to_all_wait(pos, slabs_in[a], slabs_out[a], slab_ssem, slab_rsem, a * (N_DEV - 1), in_flight[a])
        for cp in own:
            cp.wait()

    pk = jax.ShapeDtypeStruct((8, PK_END), F32)
    ada = jax.ShapeDtypeStruct((D_MODEL, n_sh), F32)
    return pl.pallas_call(
        body, name="final_exchange",
        out_shape=[pk] * 4 + [ada] + [jax.ShapeDtypeStruct(g.shape, g.dtype) for g in grads],
        in_specs=[_vmem_spec()] * 5 + [_any_spec()] * n,
        out_specs=[_vmem_spec()] * 5 + [_any_spec()] * n,
        scratch_shapes=[
            pltpu.VMEM((N_DEV, 8, PK_END), F32),
            pltpu.SemaphoreType.DMA((N_DEV - 1,)),
            pltpu.SemaphoreType.DMA((N_DEV - 1,)),
            pltpu.SemaphoreType.DMA((n * (N_DEV - 1),)),
            pltpu.SemaphoreType.DMA((n * (N_DEV - 1),)),
            pltpu.SemaphoreType.DMA((n,)),
        ],
        compiler_params=pltpu.CompilerParams(vmem_limit_bytes=VMEM_BIG),
    )(gpack, ccol, wpack, mpack, vpack, *grads)


def _adamw_reduce(name, parts, w, m, v, row_tile):
    rows, cols = w.shape
    n_parts = parts.shape[0]

    def body(p_ref, w_ref, m_ref, v_ref, g_ref, d_ref, mo_ref, vo_ref):
        g = p_ref[0].astype(F32)
        for j in range(1, n_parts):
            g = g + p_ref[j].astype(F32)
        g_ref[...] = g
        d_ref[...], mo_ref[...], vo_ref[...] = _adamw(w_ref[...], g, m_ref[...], v_ref[...])

    tile = pl.BlockSpec((row_tile, cols), lambda i: (i, 0))
    return pl.pallas_call(
        body, name=name, grid=(rows // row_tile,),
        out_shape=[jax.ShapeDtypeStruct((rows, cols), F32)] * 4,
        in_specs=[pl.BlockSpec((n_parts, row_tile, cols), lambda i: (0, i, 0)), tile, tile, tile],
        out_specs=[tile] * 4,
        compiler_params=pltpu.CompilerParams(dimension_semantics=("parallel",), vmem_limit_bytes=VMEM_BIG),
    )(parts, w, m, v)


def _rope_table(positions):
    inv_freq = 10000.0 ** (-jnp.arange(0, ROPE, 2, dtype=F32) / ROPE)
    ang = positions.astype(F32)[:, None] * inv_freq
    cos, sin = jnp.cos(ang), jnp.sin(ang)
    pad = jnp.zeros((positions.shape[0], LANES - 2 * ROPE), F32)
    return jnp.concatenate([cos, cos, -sin, sin, pad], axis=1)


def _rearrange_cols(w):
    pad = jnp.zeros((w.shape[0], IN_COLS_R - IN_COLS), w.dtype)
    return jnp.concatenate([w[:, :2688], w[:, 2720:3232], w[:, 2688:2720], pad], axis=1)


def _restore_cols(g):
    return jnp.concatenate([g[:, :2688], g[:, 3200:3232], g[:, 2688:3200]], axis=1)


def _pad_heads(w):
    rows = w.shape[0]
    w = w.reshape(rows, HEADS, MLA_QK)
    return jnp.pad(w, ((0, 0), (0, 0), (0, LANES - MLA_QK))).reshape(rows, HEADS * LANES)


def _unpad_heads(g):
    rows = g.shape[0]
    return g.reshape(rows, HEADS, LANES)[:, :, :MLA_QK].reshape(rows, HEADS * MLA_QK)


def _pad_lanes(v):
    return jnp.pad(v, ((0, 0), (0, LANES - v.shape[1])))


def _col_shards(g):
    rows = g.shape[0]
    return g.reshape(rows, N_DEV, g.shape[1] // N_DEV).transpose(1, 0, 2)


def _from_col_shards(g):
    return g.transpose(1, 0, 2).reshape(g.shape[1], N_DEV * g.shape[2])


def _pack(norm_w, qln, kvln, qhn, khn, ada, loss_lanes=None):
    if loss_lanes is None:
        loss_lanes = jnp.zeros((1, PK_END - PK_LOSS), F32)
    row = jnp.concatenate([norm_w, qln, kvln, _pad_lanes(qhn), _pad_lanes(khn), ada, loss_lanes], axis=1)
    return jnp.broadcast_to(row, (8, PK_END))


def _unpack(p):
    row = p[0:1]
    return (row[:, PK_NORM:PK_QLN], row[:, PK_QLN:PK_KVLN], row[:, PK_KVLN:PK_QHN],
            row[:, PK_QHN:PK_QHN + MLA_QK], row[:, PK_KHN:PK_KHN + MLA_QK], row[:, PK_ADA:PK_LOSS])


def kernel(x, c, positions, w_ada, b_ada, norm_w, w_in, q_lora_norm, w_uq, kv_lora_norm, w_ukv, q_head_norm, k_head_norm, w_out, loss_target, m_w_ada, m_b_ada, m_norm_w, m_w_in, m_q_lora_norm, m_w_uq, m_kv_lora_norm, m_w_ukv, m_q_head_norm, m_k_head_norm, m_w_out, v_w_ada, v_b_ada, v_norm_w, v_w_in, v_q_lora_norm, v_w_uq, v_kv_lora_norm, v_w_ukv, v_q_head_norm, v_k_head_norm, v_w_out):
    s_len = x.shape[1]
    x2 = x.reshape(s_len, D_MODEL)
    tgt = loss_target.reshape(s_len, D_MODEL)
    w_ada_s, w_in_s, w_uq_s, w_ukv_s, w_out_s = w_ada[0], w_in[0], w_uq[0], w_ukv[0], w_out[0]

    ada8, c_all = _ada_fwd(jnp.broadcast_to(c, (8, D_MODEL)), w_ada_s, b_ada.reshape(N_DEV, -1))
    ada = ada8.reshape(1, 3 * D_MODEL)
    shift, scale, gate = ada[:, :D_MODEL], ada[:, D_MODEL:2 * D_MODEL], ada[:, 2 * D_MODEL:]

    g_uq, g_ukv, g_out, g_in = _gather_weights([w_uq_s, w_ukv_s, w_out_s, w_in_s])
    w_in_r = _rearrange_cols(_from_col_shards(g_in))
    wuq_p = _pad_heads(_from_col_shards(g_uq))
    wukv_f = _from_col_shards(g_ukv)
    w_out_f = g_out.reshape(D_MODEL, D_MODEL)

    rope_table = _rope_table(positions[0])
    qhn_p, khn_p = _pad_lanes(q_head_norm), _pad_lanes(k_head_norm)

    hb, qkv, g_sb, cq, ckv, g_mla, kr, qm, km, vm, q_sb_t, qm_t = _fwd_pre(
        x2, shift, scale, norm_w, w_in_r, q_lora_norm, wuq_p, kv_lora_norm, wukv_f, qhn_p, khn_p, rope_table)
    o_sb, w_saved, l_saved = _sb_fwd(qkv)
    o_mla, lse, p_saved = _mla_fwd(qm, km, vm)

    dy, do_sb, dg_sb, do_mla, dg_mla, gw_out, d_gate, loss_acc, do_sb_t, do_mla_t = _mid(
        o_sb, g_sb, o_mla, g_mla, x2, tgt, gate, w_out_f)

    dq_sb, dk_sb_t, dv_sb_t = _sb_bwd(qkv, q_sb_t, do_sb, do_sb_t, w_saved, l_saved)
    dk_sb, dv_sb = _from_key_tiles(dk_sb_t), _from_key_tiles(dv_sb_t)
    dq_m, dk_m_t, dv_m_t = _mla_bwd(qm_t, km, vm, o_mla, do_mla, do_mla_t, lse, p_saved)
    dk_m, dv_m = _from_key_tiles(dk_m_t), _from_key_tiles(dv_m_t)
    dcq, dckv, dkr, gw_uq_p, gw_ukv, g_qhn, g_khn, g_qln, g_kvln = _mla_pre_bwd(
        dq_m, dk_m, dv_m, cq, ckv, kr, q_lora_norm, wuq_p, kv_lora_norm, wukv_f, qhn_p, khn_p, rope_table)
    dproj, grad_x, d_shift, d_scale, g_norm_w = _dproj_bwd(
        dq_sb, dk_sb, dv_sb, dg_sb, dcq, dckv, dg_mla, dkr, w_in_r, x2, dy, norm_w, scale)
    gw_in = _restore_cols(_grad_w_in(hb, dproj))

    d_ada = jnp.concatenate([d_shift, d_scale, d_gate], axis=1)
    gpack = _pack(g_norm_w, g_qln, g_kvln, g_qhn[:, :MLA_QK], g_khn[:, :MLA_QK], d_ada, loss_acc)
    wpack = _pack(norm_w, q_lora_norm, kv_lora_norm, q_head_norm, k_head_norm, b_ada)
    mpack = _pack(m_norm_w, m_q_lora_norm, m_kv_lora_norm, m_q_head_norm, m_k_head_norm, m_b_ada)
    vpack = _pack(v_norm_w, v_q_lora_norm, v_kv_lora_norm, v_q_head_norm, v_k_head_norm, v_b_ada)
    ccol = jnp.broadcast_to(c_all[:, :, None], (N_DEV, D_MODEL, LANES))
    slabs = [g.astype(BF16) for g in (
        _col_shards(gw_in), _col_shards(_unpad_heads(gw_uq_p)), _col_shards(gw_ukv),
        gw_out.reshape(N_DEV, D_MODEL // N_DEV, D_MODEL))]
    pg, pd, pm, pv, gw_ada, r_in, r_uq, r_ukv, r_out = _final_exchange(
        gpack, ccol, wpack, mpack, vpack, w_ada_s.shape[1], slabs)
    loss = 0.5 * jnp.sum(pg[0, PK_LOSS:PK_END]) / D_MODEL

    ada_g, ada_d, ada_m, ada_v = _adamw_reduce("adamw_w_ada", gw_ada[None], w_ada_s, m_w_ada[0], v_w_ada[0], 256)
    in_g, in_d, in_m, in_v = _adamw_reduce("adamw_w_in", r_in, w_in_s, m_w_in[0], v_w_in[0], 256)
    uq_g, uq_d, uq_m, uq_v = _adamw_reduce("adamw_w_uq", r_uq, w_uq_s, m_w_uq[0], v_w_uq[0], w_uq_s.shape[0])
    ukv_g, ukv_d, ukv_m, ukv_v = _adamw_reduce(
        "adamw_w_ukv", r_ukv, w_ukv_s, m_w_ukv[0], v_w_ukv[0], w_ukv_s.shape[0])
    out_g, out_d, out_m, out_v = _adamw_reduce(
        "adamw_w_out", r_out, w_out_s, m_w_out[0], v_w_out[0], w_out_s.shape[0])

    def group(ada_t, pk, in_t, uq_t, ukv_t, out_t):
        nw, qln, kvln, qhn, khn, b = _unpack(pk)
        return (ada_t[None], b, nw, in_t[None], qln, uq_t[None], kvln, ukv_t[None], qhn, khn, out_t[None])

    return (loss, grad_x.reshape(1, s_len, D_MODEL),
            *group(ada_g, pg, in_g, uq_g, ukv_g, out_g),
            *group(ada_d, pd, in_d, uq_d, ukv_d, out_d),
            *group(ada_m, pm, in_m, uq_m, ukv_m, out_m),
            *group(ada_v, pv, in_v, uq_v, ukv_v, out_v))
```

```python
import functools
import math

import jax
import jax.numpy as jnp
from jax import lax
from jax.experimental import pallas as pl
from jax.experimental.pallas import tpu as pltpu

F32 = jnp.float32
BF16 = jnp.bfloat16

N_DEV = 8
D_MODEL = 1024
HEADS = 8
SB_WIDTH = 512
MLA_WIDTH = 512
Q_LORA = 384
KV_LORA = 256
ROPE = 32
NOPE = 64
MLA_QK = 96
LANES = 128
IN_COLS = 3232
IN_COLS_R = 3328
EPS = 1e-6
NEG = -1e30

ADAM_LR = 0.001
ADAM_B1 = 0.9
ADAM_B2 = 0.999
ADAM_EPS = 1e-08
ADAM_WD = 0.01
ADAM_STEP = 10

TS = 512
TS_FWD = 256
TQ = 512
SB_UNROLL = 2
MLA_UNROLL = 3
TN_S = 512
VMEM_BIG = 56 * 1024 * 1024

PK_NORM, PK_QLN, PK_KVLN, PK_QHN, PK_KHN, PK_ADA, PK_LOSS, PK_END = 0, 1024, 1408, 1664, 1792, 1920, 4992, 6016

MESH_ID = pl.DeviceIdType.MESH


def _dot_nn(a, b):
    return lax.dot_general(a, b, (((1,), (0,)), ((), ())), preferred_element_type=F32)


def _dot_nt(a, b):
    return lax.dot_general(a, b, (((1,), (1,)), ((), ())), preferred_element_type=F32)


def _dot_tn(a, b):
    return lax.dot_general(a, b, (((0,), (0,)), ((), ())), preferred_element_type=F32)


def _split_bf16(a):
    hi = a.astype(BF16)
    lo = (a - hi.astype(F32)).astype(BF16)
    return hi, lo


def _dot3(a, b):
    ah, al = _split_bf16(a)
    bh, bl = _split_bf16(b)
    return _dot_nn(ah, bh) + _dot_nn(ah, bl) + _dot_nn(al, bh)


def _sigmoid(g):
    return 1.0 / (1.0 + jnp.exp(-g))


def _silu(g):
    return g * _sigmoid(g)


def _lane_iota(shape):
    return lax.broadcasted_iota(jnp.int32, shape, len(shape) - 1)


def _adamw(w, g, m, v):
    m = ADAM_B1 * m + (1.0 - ADAM_B1) * g
    v = ADAM_B2 * v + (1.0 - ADAM_B2) * (g * g)
    m_hat = m / (1.0 - ADAM_B1 ** ADAM_STEP)
    v_hat = v / (1.0 - ADAM_B2 ** ADAM_STEP)
    delta = -ADAM_LR * (m_hat / (jnp.sqrt(v_hat) + ADAM_EPS) + ADAM_WD * w)
    return delta, m, v


def _mesh_pos():
    return lax.axis_index("x"), lax.axis_index("y"), lax.axis_index("c")


def _peer(pos, k):
    x, y, c = pos
    return (1 - x if k & 4 else x, 1 - y if k & 2 else y, 1 - c if k & 1 else c)


def _lin(pos):
    return 4 * pos[0] + 2 * pos[1] + pos[2]


def _remote(src, dst, send_sems, recv_sems, idx, peer):
    return pltpu.make_async_remote_copy(
        src_ref=src, dst_ref=dst, send_sem=send_sems.at[idx], recv_sem=recv_sems.at[idx],
        device_id=peer, device_id_type=MESH_ID)


def _all_gather_start(pos, src, buf, send_sems, recv_sems, base):
    me = _lin(pos)
    sent = []
    for k in range(1, N_DEV):
        cp = _remote(src, buf.at[me], send_sems, recv_sems, base + k - 1, _peer(pos, k))
        cp.start()
        sent.append(cp)
    return sent


def _all_gather_wait(pos, src, buf, send_sems, recv_sems, base, sent):
    for k in range(1, N_DEV):
        peer = _peer(pos, k)
        _remote(src, buf.at[_lin(peer)], send_sems, recv_sems, base + k - 1, peer).wait_recv()
    for cp in sent:
        cp.wait_send()


def _all_gather(pos, src, buf, send_sems, recv_sems, base):
    sent = _all_gather_start(pos, src, buf, send_sems, recv_sems, base)
    _all_gather_wait(pos, src, buf, send_sems, recv_sems, base, sent)


def _all_to_all_start(pos, src, buf, send_sems, recv_sems, base):
    me = _lin(pos)
    sent = []
    for k in range(1, N_DEV):
        peer = _peer(pos, k)
        cp = _remote(src.at[_lin(peer)], buf.at[me], send_sems, recv_sems, base + k - 1, peer)
        cp.start()
        sent.append(cp)
    return sent


def _all_to_all_wait(pos, src, buf, send_sems, recv_sems, base, sent):
    me = _lin(pos)
    for k in range(1, N_DEV):
        peer = _peer(pos, k)
        _remote(src.at[me], buf.at[_lin(peer)], send_sems, recv_sems, base + k - 1, peer).wait_recv()
    for cp in sent:
        cp.wait_send()


def _all_to_all(pos, src, buf, send_sems, recv_sems, base):
    sent = _all_to_all_start(pos, src, buf, send_sems, recv_sems, base)
    _all_to_all_wait(pos, src, buf, send_sems, recv_sems, base, sent)


def _two_level_gather(pos, bufs, send_sems, recv_sems):
    x, y, c = pos
    me, sibling = (x, y, c), (x, y, 1 - c)
    chips = [(1 - x, y), (x, 1 - y), (1 - x, 1 - y)]

    def copy(a, k, block, to):
        slot = bufs[a].at[_lin(block)]
        return _remote(slot, slot, send_sems, recv_sems, 7 * a + k, to)

    started = []
    for a in range(len(bufs)):
        first = [copy(a, 0, me, sibling)] + [copy(a, 1 + j, me, (*chip, c)) for j, chip in enumerate(chips)]
        for cp in first:
            cp.start()
        started += first
    for a in range(len(bufs)):
        for j, chip in enumerate(chips):
            copy(a, 1 + j, (*chip, c), me).wait_recv()
            passed = copy(a, 4 + j, (*chip, c), sibling)
            passed.start()
            started.append(passed)
    for a in range(len(bufs)):
        copy(a, 0, sibling, me).wait_recv()
        for j, chip in enumerate(chips):
            copy(a, 4 + j, (*chip, 1 - c), me).wait_recv()
    for cp in started:
        cp.wait_send()


def _vmem_spec():
    return pl.BlockSpec(memory_space=pltpu.VMEM)


def _any_spec():
    return pl.BlockSpec(memory_space=pl.ANY)


def _row_select(slots, n):
    r = lax.broadcasted_iota(jnp.int32, (N_DEV, n), 0)
    out = jnp.zeros((N_DEV, n), F32)
    for j in range(N_DEV):
        out = out + jnp.where(r == j, slots[j], 0.0)
    return out


def _ada_fwd(c8, w_ada, b_ada8):
    n_sh = w_ada.shape[1]

    def body(c_ref, w_ref, b_ref, out_ref, call_out_ref, call_ref, psend_ref, precv_ref, ssem, rsem):
        pos = _mesh_pos()
        me = _lin(pos)
        call_ref[me] = c_ref[...]
        _all_gather(pos, c_ref, call_ref, ssem, rsem, 0)
        call_out_ref[...] = _row_select([call_ref[j] for j in range(N_DEV)], D_MODEL)
        w = w_ref[...]
        for j in range(N_DEV):
            psend_ref[j] = _dot3(_silu(call_ref[j]), w)
        precv_ref[me] = psend_ref[me]
        _all_to_all(pos, psend_ref, precv_ref, ssem, rsem, N_DEV - 1)
        out_ref[...] = _row_select([precv_ref[j] for j in range(N_DEV)], n_sh) + b_ref[...]

    return pl.pallas_call(
        body, name="ada_fwd",
        out_shape=[jax.ShapeDtypeStruct((N_DEV, n_sh), F32), jax.ShapeDtypeStruct((N_DEV, D_MODEL), F32)],
        in_specs=[_vmem_spec()] * 3, out_specs=[_vmem_spec()] * 2,
        scratch_shapes=[
            pltpu.VMEM((N_DEV, 8, D_MODEL), F32),
            pltpu.VMEM((N_DEV, 8, n_sh), F32),
            pltpu.VMEM((N_DEV, 8, n_sh), F32),
            pltpu.SemaphoreType.DMA((2 * (N_DEV - 1),)),
            pltpu.SemaphoreType.DMA((2 * (N_DEV - 1),)),
        ],
    )(c8, w_ada, b_ada8)


def _gather_weights(shards):
    n = len(shards)

    def body(*refs):
        ins, outs = refs[:n], refs[n:2 * n]
        ssem, rsem = refs[2 * n], refs[2 * n + 1]
        pos = _mesh_pos()
        me = _lin(pos)
        for a in range(n):
            outs[a][me] = ins[a][...].astype(BF16)
        _two_level_gather(pos, outs, ssem, rsem)

    return pl.pallas_call(
        body, name="gather_weights",
        out_shape=[jax.ShapeDtypeStruct((N_DEV,) + s.shape, BF16) for s in shards],
        in_specs=[_vmem_spec()] * n, out_specs=[_vmem_spec()] * n,
        scratch_shapes=[
            pltpu.SemaphoreType.DMA((n * (N_DEV - 1),)),
            pltpu.SemaphoreType.DMA((n * (N_DEV - 1),)),
        ],
        compiler_params=pltpu.CompilerParams(vmem_limit_bytes=VMEM_BIG),
    )(*shards)


def _rope(t, cosf, sin_a, sin_b):
    return t * cosf + pltpu.roll(t, 112, 1) * sin_a + pltpu.roll(t, 16, 1) * sin_b


def _expand_rope(table):
    lane = _lane_iota(table.shape)
    by64, by32 = pltpu.roll(table, 64, 1), pltpu.roll(table, 32, 1)
    cosf = jnp.where(jnp.logical_and(lane >= NOPE, lane < MLA_QK), by64, 1.0)
    sin_a = jnp.where(jnp.logical_and(lane >= NOPE, lane < NOPE + ROPE // 2), by32, 0.0)
    sin_b = jnp.where(jnp.logical_and(lane >= NOPE + ROPE // 2, lane < MLA_QK), by32, 0.0)
    return cosf, sin_a, sin_b


def _rope_t(d, cosf, sin_a, sin_b):
    return d * cosf + pltpu.roll(d * sin_a, 16, 1) + pltpu.roll(d * sin_b, 112, 1)


def _head_rms(t):
    return lax.rsqrt(jnp.sum(t * t, axis=1, keepdims=True) * (1.0 / MLA_QK) + EPS)


def _rms_bwd(dxhat_w, xhat, r, n):
    return r * (dxhat_w - xhat * (jnp.sum(dxhat_w * xhat, axis=1, keepdims=True) * (1.0 / n)))


def _mla_latents(cq, ckv, kr, qln, kvln, wuq, wukv):
    rq = lax.rsqrt(jnp.mean(cq * cq, axis=1, keepdims=True) + EPS)
    rkv = lax.rsqrt(jnp.mean(ckv * ckv, axis=1, keepdims=True) + EPS)
    cq_hat = cq * rq
    ckv_hat = ckv * rkv
    cqn = (cq_hat * qln).astype(BF16)
    ckvn = (ckv_hat * kvln).astype(BF16)
    q_all = _dot_nn(cqn, wuq)
    kv = _dot_nn(ckvn, wukv)
    kr64 = pltpu.roll(kr, 64, 1)
    return rq, rkv, cq_hat, ckv_hat, cqn, ckvn, q_all, kv, kr64


def _fwd_pre(x, shift, scale, norm_w, w_in_r, qln, wuq, kvln, wukv, qhn, khn, rope_table):
    s_len = x.shape[0]

    def body(x_ref, shift_ref, scale_ref, nw_ref, w_ref, qln_ref, wuq_ref, kvln_ref, wukv_ref,
             qhn_ref, khn_ref, rt_ref,
             hb_ref, qkv_ref, gsb_ref, cq_ref, ckv_ref, gmla_ref, kr_ref, qm_ref, km_ref, vm_ref,
             qsbt_ref, qmt_ref):
        xv = x_ref[...]
        r = lax.rsqrt(jnp.mean(xv * xv, axis=1, keepdims=True) + EPS)
        h = (xv * r) * nw_ref[...] * (1.0 + scale_ref[...]) + shift_ref[...]
        hb = h.astype(BF16)
        hb_ref[...] = hb
        qkv = _dot_nn(hb, w_ref[:, 0:1536])
        qkv_ref[...] = qkv.astype(BF16)
        qsbt_ref[...] = qkv[:, :SB_WIDTH].T.astype(BF16)
        gsb_ref[...] = _dot_nn(hb, w_ref[:, 1536:2048])
        cq = _dot_nn(hb, w_ref[:, 2048:2432])
        ckv = _dot_nn(hb, w_ref[:, 2432:2688])
        gmla_ref[...] = _dot_nn(hb, w_ref[:, 2688:3200])
        kr = _dot_nn(hb, w_ref[:, 3200:3328])
        cq_ref[...] = cq
        ckv_ref[...] = ckv
        kr_ref[...] = kr
        _, _, _, _, _, _, q_all, kv, kr64 = _mla_latents(
            cq, ckv, kr, qln_ref[...], kvln_ref[...], wuq_ref[...], wukv_ref[...])
        cosf, sa, sb = _expand_rope(rt_ref[...])
        qhn_v, khn_v = qhn_ref[...], khn_ref[...]
        low = _lane_iota((TS_FWD, LANES)) < NOPE
        blks = [slice(LANES * hd, LANES * (hd + 1)) for hd in range(HEADS)]
        q_raw = [q_all[:, b] for b in blks]
        k_raw = [jnp.where(low, kv[:, b], kr64) for b in blks]
        q_rms = [_head_rms(t) for t in q_raw]
        k_rms = [_head_rms(t) for t in k_raw]
        q_n = [t * r * (qhn_v * MLA_SCALE) for t, r in zip(q_raw, q_rms)]
        q_roped = [_rope(t, cosf, sa, sb) for t in q_n]
        kr_roped = _rope(kr64 * khn_v, cosf, sa, sb)
        k_roped = [jnp.where(low, t * khn_v, kr_roped) * r for t, r in zip(k_raw, k_rms)]
        for hd, b in enumerate(blks):
            qm_ref[:, b] = q_roped[hd].astype(BF16)
            qmt_ref[b, :] = q_roped[hd].T.astype(BF16)
            km_ref[:, b] = k_roped[hd].astype(BF16)
        for p in range(HEADS // 2):
            even = kv[:, LANES * 2 * p:LANES * (2 * p + 1)]
            odd = kv[:, LANES * (2 * p + 1):LANES * (2 * p + 2)]
            vm_ref[:, LANES * p:LANES * (p + 1)] = jnp.where(low, pltpu.roll(even, 64, 1), odd).astype(BF16)

    def tile(width):
        return pl.BlockSpec((TS_FWD, width), lambda i: (i, 0))

    def full(a):
        return pl.BlockSpec(a.shape, lambda i: (0, 0))

    out_widths = [(D_MODEL, BF16), (1536, BF16), (512, F32), (Q_LORA, F32), (KV_LORA, F32),
                  (512, F32), (LANES, F32), (1024, BF16), (1024, BF16), (512, BF16)]
    t_heights = [SB_WIDTH, HEADS * LANES]
    return pl.pallas_call(
        body, name="fwd_pre", grid=(s_len // TS_FWD,),
        out_shape=[jax.ShapeDtypeStruct((s_len, w), dt) for w, dt in out_widths]
        + [jax.ShapeDtypeStruct((hgt, s_len), BF16) for hgt in t_heights],
        in_specs=[tile(D_MODEL), full(shift), full(scale), full(norm_w), full(w_in_r), full(qln), full(wuq),
                  full(kvln), full(wukv), full(qhn), full(khn), tile(LANES)],
        out_specs=[tile(w) for w, _ in out_widths]
        + [pl.BlockSpec((hgt, TS_FWD), lambda i: (0, i)) for hgt in t_heights],
        compiler_params=pltpu.CompilerParams(dimension_semantics=("parallel",), vmem_limit_bytes=VMEM_BIG),
    )(x, shift, scale, norm_w, w_in_r, qln, wuq, kvln, wukv, qhn, khn, rope_table)


CUM_W = 256


def _tri(later):
    j = lax.broadcasted_iota(jnp.int32, (CUM_W, CUM_W), 0)
    s = lax.broadcasted_iota(jnp.int32, (CUM_W, CUM_W), 1)
    return (j > s if later else j < s).astype(BF16)


def _suffix_sums(a, a_bf16, tri_m, carry):
    n = a.shape[1] // CUM_W
    outs = [None] * n
    for i in reversed(range(n)):
        cols = slice(CUM_W * i, CUM_W * (i + 1))
        outs[i] = _dot_nn(a_bf16[:, cols], tri_m) + carry
        carry = carry + _rowsum(a[:, cols])
    return (outs[0] if n == 1 else jnp.concatenate(outs, axis=1)), carry


def _sb_weights(qm, kb, carry, tri_u, diag):
    z = _dot_nt(qm, kb)
    nz = -z
    lk = jnp.minimum(nz, 0.0) - jnp.log(1.0 + jnp.exp(jnp.minimum(z, nz)))
    if diag:
        t = lax.broadcasted_iota(jnp.int32, (TQ, TQ), 0)
        s = lax.broadcasted_iota(jnp.int32, (TQ, TQ), 1)
        valid = s < t
        lk = jnp.where(valid, lk, 0.0)
    lk_hi = lk.astype(BF16)
    log_beta = z + lk
    after, carry = _suffix_sums(lk, lk_hi, tri_u, carry)
    logw = log_beta + after
    if diag:
        logw = jnp.where(valid, logw, NEG)
    return lk_hi, jnp.exp(logw), carry


SB_SCALE = 0.125


def _head_masks():
    lane = _lane_iota((1, LANES))
    return [lane < 64, lane >= 64]


def _masked(hm, a):
    return jnp.where(hm, a, jnp.zeros_like(a))


def _rowsum(a):
    return jnp.sum(a, axis=1, keepdims=True)


def _key_rows(kj):
    return pl.ds(pl.multiple_of(kj * TQ, TQ), TQ)


def _over_key_tiles(count, fn, st, ascending, unroll):
    n_full = count // unroll
    n_rest = count - n_full * unroll

    def group(g, s_):
        return fn([g * unroll + (u if ascending else unroll - 1 - u) for u in range(unroll)], s_)

    def left_over(s_):
        for r in range(1, unroll):
            tiles = [n_full * unroll + u if ascending else count - 1 - u for u in range(r)]
            s_ = lax.cond(n_rest == r, lambda a, tiles=tiles: fn(tiles, a), lambda a: a, s_)
        return s_

    if ascending:
        return left_over(lax.fori_loop(0, n_full, group, st))
    return lax.fori_loop(0, n_full, lambda i, s_: group(n_full - 1 - i, s_), left_over(st))


STAGE_SLOTS = 2 * SB_UNROLL


def _stage_copies(to_hbm, hbm_refs, scr_refs, sems, pair, qi, kj):
    slot = (qi - kj) % STAGE_SLOTS
    out = []
    for h in range(2):
        for a in range(2):
            hbm, scr = hbm_refs[a].at[2 * pair + h, qi, kj], scr_refs[a].at[slot, h]
            sem = sems.at[4 * slot + 2 * h + a]
            out.append(pltpu.make_async_copy(scr, hbm, sem) if to_hbm else pltpu.make_async_copy(hbm, scr, sem))
    return out


def _sb_fwd(qkv):
    s_len = qkv.shape[0]
    nq = s_len // TQ

    def body(q_ref, k_ref, v_ref, o_ref, w_hbm, l_hbm, w_scr, l_scr, sems):
        pair, qi = pl.program_id(0), pl.program_id(1)
        q = q_ref[...]
        tri_u = _tri(True)
        masks = _head_masks()
        qms = [_masked(hm, q) * SB_SCALE for hm in masks]

        def copies(kj, of_qi=qi):
            return _stage_copies(True, (w_hbm, l_hbm), (w_scr, l_scr), sems, pair, of_qi, kj)

        def drain(of_qi):
            for kj in range(STAGE_SLOTS):
                @pl.when(kj <= of_qi)
                def _():
                    for cp in copies(kj, of_qi):
                        cp.wait()

        def block(kj, st, diag, before_staging=None):
            rows = _key_rows(kj)
            slot = (qi - kj) % STAGE_SLOTS
            kb, vb = k_ref[rows, :], v_ref[rows, :]
            carries, acc = list(st[:2]), st[2]
            staged = []
            for h in range(2):
                lk_hi, w, carries[h] = _sb_weights(qms[h], kb, carries[h], tri_u, diag)
                wb = w.astype(BF16)
                staged.append((wb, lk_hi))
                acc = acc + _dot_nn(wb, _masked(masks[h], vb))
            if before_staging is not None:
                before_staging()
            for h in range(2):
                w_scr[slot, h], l_scr[slot, h] = staged[h]
            return carries[0], carries[1], acc

        def trip(tiles, st):
            for kj in tiles:
                @pl.when(qi - kj >= STAGE_SLOTS)
                def _():
                    for cp in copies(kj + STAGE_SLOTS):
                        cp.wait()
            for kj in tiles:
                st = block(kj, st, False)
            for kj in tiles:
                for cp in copies(kj):
                    cp.start()
            return st

        def drain_previous_step():
            @pl.when(jnp.logical_or(pair > 0, qi > 0))
            def _():
                drain(jnp.where(qi == 0, nq - 1, qi - 1))

        zc = jnp.zeros((TQ, 1), F32)
        st = block(qi, (zc, zc, jnp.zeros((TQ, LANES), F32)), True, drain_previous_step)
        for cp in copies(qi):
            cp.start()
        st = _over_key_tiles(qi, trip, st, ascending=False, unroll=SB_UNROLL)
        o_ref[...] = st[2]

        @pl.when(jnp.logical_and(pair == HEADS // 2 - 1, qi == nq - 1))
        def _():
            drain(qi)

    saved = jax.ShapeDtypeStruct((HEADS, nq, nq, TQ, TQ), BF16)
    stage = pltpu.VMEM((STAGE_SLOTS, 2, TQ, TQ), BF16)
    return pl.pallas_call(
        body, name="sb_fwd", grid=(HEADS // 2, nq),
        out_shape=[jax.ShapeDtypeStruct((s_len, SB_WIDTH), F32), saved, saved],
        in_specs=[pl.BlockSpec((TQ, LANES), lambda p, i: (i, p)),
                  pl.BlockSpec((s_len, LANES), lambda p, i: (0, 4 + p)),
                  pl.BlockSpec((s_len, LANES), lambda p, i: (0, 8 + p))],
        out_specs=[pl.BlockSpec((TQ, LANES), lambda p, i: (i, p)), _any_spec(), _any_spec()],
        scratch_shapes=[stage, stage, pltpu.SemaphoreType.DMA((4 * STAGE_SLOTS,))],
        compiler_params=pltpu.CompilerParams(dimension_semantics=("arbitrary", "arbitrary"),
                                             vmem_limit_bytes=VMEM_BIG),
    )(qkv, qkv, qkv)


def _prefix_sums(a, tri_m, carry):
    n = a.shape[1] // CUM_W
    outs = [None] * n
    for i in range(n):
        cols = slice(CUM_W * i, CUM_W * (i + 1))
        outs[i] = _dot_nn(a[:, cols].astype(BF16), tri_m) + carry
        carry = carry + _rowsum(a[:, cols])
    return (outs[0] if n == 1 else jnp.concatenate(outs, axis=1)), carry


def _head_rows(a0, a1):
    sub = lax.broadcasted_iota(jnp.int32, a0.shape, 0)
    return jnp.where(sub < NOPE, a0, a1)


def _sb_bwd(qkv, q_t, do, do_t, w_saved, l_saved):
    s_len = qkv.shape[0]
    nq = s_len // TQ

    def body(qt_ref, k_ref, v_ref, do_ref, dot_ref, w_hbm, l_hbm, dq_ref, dk_ref, dv_ref, w_scr, l_scr, sems):
        pair, qi = pl.program_id(0), pl.program_id(1)

        def copies(kj, of_pair=pair, of_qi=qi):
            return _stage_copies(False, (w_hbm, l_hbm), (w_scr, l_scr), sems, of_pair, of_qi, kj)

        def start_first_tiles(of_pair, of_qi):
            for first in range(SB_UNROLL):
                @pl.when(first <= of_qi)
                def _():
                    for cp in copies(first, of_pair, of_qi):
                        cp.start()

        @pl.when(jnp.logical_and(pair == 0, qi == 0))
        def _():
            start_first_tiles(pair, qi)

        @pl.when(qi == 0)
        def _():
            dk_ref[...] = jnp.zeros_like(dk_ref)
            dv_ref[...] = jnp.zeros_like(dv_ref)

        qt = qt_ref[...] * SB_SCALE
        dot_v = dot_ref[...]
        do_v = do_ref[...]
        tri_before = _tri(False)
        masks = _head_masks()
        doms = [_masked(hm, do_v) for hm in masks]

        def block(kj, st):
            rows = _key_rows(kj)
            slot = (qi - kj) % STAGE_SLOTS
            kb, vb = k_ref[rows, :], v_ref[rows, :]
            carries, dqs = list(st[0:2]), list(st[2:4])
            dk_t, dv_t = [], []
            for h in range(2):
                wb = w_scr[slot, h]
                d_l = _dot_nt(doms[h], vb) * wb.astype(F32)
                before, carries[h] = _prefix_sums(d_l, tri_before, carries[h])
                keep = jnp.exp(l_scr[slot, h].astype(F32))
                dzb = (d_l * keep - before * (1.0 - keep)).astype(BF16)
                dk_t.append(_dot_nn(qt, dzb))
                dv_t.append(_dot_nn(dot_v, wb))
                dqs[h] = dqs[h] + _dot_nn(dzb, kb)
            dk_ref[kj] += _head_rows(*dk_t)
            dv_ref[kj] += _head_rows(*dv_t)
            return (*carries, *dqs)

        def trip(tiles, st):
            for kj in tiles:
                @pl.when(kj + SB_UNROLL <= qi)
                def _():
                    for cp in copies(kj + SB_UNROLL):
                        cp.start()
            for kj in tiles:
                for cp in copies(kj):
                    cp.wait()
            for kj in tiles:
                st = block(kj, st)
            return st

        zc = jnp.zeros((TQ, 1), F32)
        za = jnp.zeros((TQ, LANES), F32)
        st = _over_key_tiles(qi + 1, trip, (zc, zc, za, za), ascending=True, unroll=SB_UNROLL)
        dq_ref[...] = jnp.where(masks[0], st[2], st[3]) * SB_SCALE

        @pl.when(jnp.logical_or(pair < HEADS // 2 - 1, qi < nq - 1))
        def _():
            wraps = qi == nq - 1
            start_first_tiles(jnp.where(wraps, pair + 1, pair), jnp.where(wraps, 0, qi + 1))

    tile = pl.BlockSpec((TQ, LANES), lambda p, i: (i, p))
    tile_t = pl.BlockSpec((LANES, TQ), lambda p, i: (p, i))
    col_t = pl.BlockSpec((nq, LANES, TQ), lambda p, i: (0, p, 0))
    stage = pltpu.VMEM((STAGE_SLOTS, 2, TQ, TQ), BF16)
    key_t = jax.ShapeDtypeStruct((nq, SB_WIDTH, TQ), F32)
    return pl.pallas_call(
        body, name="sb_bwd", grid=(HEADS // 2, nq),
        out_shape=[jax.ShapeDtypeStruct((s_len, SB_WIDTH), F32), key_t, key_t],
        in_specs=[tile_t,
                  pl.BlockSpec((s_len, LANES), lambda p, i: (0, 4 + p)),
                  pl.BlockSpec((s_len, LANES), lambda p, i: (0, 8 + p)),
                  tile, tile_t, _any_spec(), _any_spec()],
        out_specs=[tile, col_t, col_t],
        scratch_shapes=[stage, stage, pltpu.SemaphoreType.DMA((4 * STAGE_SLOTS,))],
        compiler_params=pltpu.CompilerParams(dimension_semantics=("arbitrary", "arbitrary"),
                                             vmem_limit_bytes=VMEM_BIG),
    )(q_t, qkv, qkv, do, do_t, w_saved, l_saved)


def _from_key_tiles(a_t):
    return a_t.transpose(0, 2, 1).reshape(a_t.shape[0] * a_t.shape[2], a_t.shape[1])


MLA_SCALE = 1.0 / math.sqrt(MLA_QK)


def _causal_mask():
    t = lax.broadcasted_iota(jnp.int32, (TQ, TQ), 0)
    s = lax.broadcasted_iota(jnp.int32, (TQ, TQ), 1)
    return s <= t


def _head_lanes(h):
    return slice(LANES * h, LANES * (h + 1))


P_SLOTS = 2 * MLA_UNROLL
P_COLS = TQ + LANES


def _tile_number(pair, qi, kj, nq):
    return pair * (nq * (nq + 1) // 2) + (qi * (qi + 1)) // 2 + kj


def _p_copy(to_hbm, p_hbm, p_scr, sems, pair, qi, kj, nq, h):
    slot = _tile_number(pair, qi, kj, nq) % P_SLOTS
    hbm, scr, sem = p_hbm.at[2 * pair + h, qi, kj], p_scr.at[slot, h], sems.at[2 * slot + h]
    return pltpu.make_async_copy(scr, hbm, sem) if to_hbm else pltpu.make_async_copy(hbm, scr, sem)


def _mla_fwd(qm, km, vm):
    s_len = qm.shape[0]
    nq = s_len // TQ

    def body(q_ref, k_ref, v_ref, o_ref, lse_ref, p_hbm, p_scr, sems):
        pair, qi = pl.program_id(0), pl.program_id(1)
        masks = _head_masks()
        qhs = [q_ref[:, _head_lanes(h)] for h in range(2)]
        lane = _lane_iota((TQ, LANES))

        def number(kj):
            return _tile_number(pair, qi, kj, nq)

        def copies(kj):
            return [_p_copy(True, p_hbm, p_scr, sems, pair, qi, kj, nq, h) for h in range(2)]

        def two_parts(m):
            hi = m.astype(BF16).astype(F32)
            return jnp.where(lane < 64, hi, m - hi).astype(BF16)

        def block(kj, st, diag):
            rows = _key_rows(kj)
            slot = number(kj) % P_SLOTS
            vb = v_ref[rows, :]
            ms, ls, acc = list(st[0:2]), list(st[2:4]), st[4]
            alphas, pvs = [], []
            for h in range(2):
                s = _dot_nt(qhs[h], k_ref[rows, _head_lanes(h)])
                if diag:
                    s = jnp.where(_causal_mask(), s, NEG)
                m_new = jnp.maximum(ms[h], jnp.max(s, axis=1, keepdims=True))
                p = jnp.exp(s - m_new)
                pb = p.astype(BF16)
                p_scr[slot, h, :, 0:TQ] = pb
                p_scr[slot, h, :, TQ:P_COLS] = two_parts(m_new)
                alphas.append(jnp.exp(ms[h] - m_new))
                ls[h] = alphas[h] * ls[h] + _rowsum(p)
                ms[h] = m_new
                pvs.append(_dot_nn(pb, _masked(masks[h], vb)))
            acc = jnp.where(masks[0], alphas[0], alphas[1]) * acc + pvs[0] + pvs[1]
            return (*ms, *ls, acc)

        def trip(tiles, st, diag=False):
            for kj in tiles:
                @pl.when(number(kj) >= P_SLOTS)
                def _():
                    for cp in copies(kj):
                        cp.wait()
            for kj in tiles:
                st = block(kj, st, diag)
            for kj in tiles:
                for cp in copies(kj):
                    cp.start()
            return st

        neg = jnp.full((TQ, 1), NEG, F32)
        zc = jnp.zeros((TQ, 1), F32)
        st = (neg, neg, zc, zc, jnp.zeros((TQ, LANES), F32))
        st = _over_key_tiles(qi, trip, st, ascending=True, unroll=MLA_UNROLL)
        m0, m1, l0, l1, acc = trip([qi], st, True)
        o_ref[...] = acc / jnp.where(masks[0], l0, l1)
        lse_ref[0] = m0 + jnp.log(l0)
        lse_ref[1] = m1 + jnp.log(l1)

        @pl.when(jnp.logical_and(pair == HEADS // 2 - 1, qi == nq - 1))
        def _():
            for slot in range(P_SLOTS):
                for h in range(2):
                    pltpu.make_async_copy(p_scr.at[slot, h], p_hbm.at[0, 0, 0], sems.at[2 * slot + h]).wait()

    return pl.pallas_call(
        body, name="mla_fwd", grid=(HEADS // 2, nq),
        out_shape=[jax.ShapeDtypeStruct((s_len, MLA_WIDTH), F32),
                   jax.ShapeDtypeStruct((HEADS, s_len, 1), F32),
                   jax.ShapeDtypeStruct((HEADS, nq, nq, TQ, P_COLS), BF16)],
        in_specs=[pl.BlockSpec((TQ, 2 * LANES), lambda p, i: (i, p)),
                  pl.BlockSpec((s_len, 2 * LANES), lambda p, i: (0, p)),
                  pl.BlockSpec((s_len, LANES), lambda p, i: (0, p))],
        out_specs=[pl.BlockSpec((TQ, LANES), lambda p, i: (i, p)),
                   pl.BlockSpec((2, TQ, 1), lambda p, i: (p, i, 0)),
                   _any_spec()],
        scratch_shapes=[pltpu.VMEM((P_SLOTS, 2, TQ, P_COLS), BF16), pltpu.SemaphoreType.DMA((2 * P_SLOTS,))],
        compiler_params=pltpu.CompilerParams(dimension_semantics=("arbitrary", "arbitrary"),
                                             vmem_limit_bytes=VMEM_BIG),
    )(qm, km, vm)


def _mla_bwd(qm_t, km, vm, o, do, do_t, lse, p_saved):
    s_len = km.shape[0]
    nq = s_len // TQ
    total = (HEADS // 2) * (nq * (nq + 1) // 2)

    def body(qt_ref, k_ref, v_ref, o_ref, do_ref, dot_ref, lse_ref, p_hbm, dq_ref, dk_ref, dv_ref, p_scr, sems):
        pair, qi = pl.program_id(0), pl.program_id(1)

        def number(kj):
            return _tile_number(pair, qi, kj, nq)

        def fetch(p_, q_, k_):
            for h in range(2):
                _p_copy(False, p_hbm, p_scr, sems, p_, q_, k_, nq, h).start()

        def advance(p_, q_, k_):
            row_end = k_ == q_
            last_row = q_ == nq - 1
            return (jnp.where(jnp.logical_and(row_end, last_row), p_ + 1, p_),
                    jnp.where(row_end, jnp.where(last_row, 0, q_ + 1), q_),
                    jnp.where(row_end, 0, k_ + 1))

        def ahead(tile, steps):
            for _ in range(steps):
                tile = advance(*tile)
            return tile

        @pl.when(jnp.logical_and(pair == 0, qi == 0))
        def _():
            for first in range(MLA_UNROLL):
                fetch(*ahead((pair, qi, 0), first))

        @pl.when(qi == 0)
        def _():
            dk_ref[...] = jnp.zeros_like(dk_ref)
            dv_ref[...] = jnp.zeros_like(dv_ref)

        do_v = do_ref[...]
        dot_v = dot_ref[...]
        od = o_ref[...] * do_v.astype(F32)
        masks = _head_masks()
        qts = [qt_ref[_head_lanes(h), :] for h in range(2)]
        doms = [_masked(hm, do_v) for hm in masks]
        deltas = [_rowsum(jnp.where(hm, od, 0.0)) for hm in masks]
        lses = [lse_ref[h] for h in range(2)]

        def block(kj, dqs):
            rows = _key_rows(kj)
            slot = number(kj) % P_SLOTS
            vb = v_ref[rows, :]
            dqs = list(dqs)
            dv_t = []
            for h in range(2):
                kb = k_ref[rows, _head_lanes(h)]
                tile_max = _rowsum(p_scr[slot, h, :, TQ:P_COLS].astype(F32)) * (1.0 / 64.0)
                p = p_scr[slot, h, :, 0:TQ].astype(F32) * jnp.exp(tile_max - lses[h])
                dp = _dot_nt(doms[h], vb)
                ds = (p * (dp - deltas[h])).astype(BF16)
                dk_ref[kj, _head_lanes(h), :] += _dot_nn(qts[h], ds)
                dv_t.append(_dot_nn(dot_v, p.astype(BF16)))
                dqs[h] = dqs[h] + _dot_nn(ds, kb)
            dv_ref[kj] += _head_rows(*dv_t)
            return tuple(dqs)

        def trip(tiles, dqs):
            for kj in tiles:
                @pl.when(number(kj) + MLA_UNROLL < total)
                def _():
                    fetch(*ahead((pair, qi, kj), MLA_UNROLL))
            for kj in tiles:
                for h in range(2):
                    _p_copy(False, p_hbm, p_scr, sems, pair, qi, kj, nq, h).wait()
            for kj in tiles:
                dqs = block(kj, dqs)
            return dqs

        za = jnp.zeros((TQ, LANES), F32)
        dqs = _over_key_tiles(qi + 1, trip, (za, za), ascending=True, unroll=MLA_UNROLL)
        dq_ref[:, _head_lanes(0)] = dqs[0] * MLA_SCALE
        dq_ref[:, _head_lanes(1)] = dqs[1] * MLA_SCALE

    return pl.pallas_call(
        body, name="mla_bwd", grid=(HEADS // 2, nq),
        out_shape=[jax.ShapeDtypeStruct((s_len, HEADS * LANES), F32),
                   jax.ShapeDtypeStruct((nq, HEADS * LANES, TQ), F32),
                   jax.ShapeDtypeStruct((nq, MLA_WIDTH, TQ), F32)],
        in_specs=[pl.BlockSpec((2 * LANES, TQ), lambda p, i: (p, i)),
                  pl.BlockSpec((s_len, 2 * LANES), lambda p, i: (0, p)),
                  pl.BlockSpec((s_len, LANES), lambda p, i: (0, p)),
                  pl.BlockSpec((TQ, LANES), lambda p, i: (i, p)),
                  pl.BlockSpec((TQ, LANES), lambda p, i: (i, p)),
                  pl.BlockSpec((LANES, TQ), lambda p, i: (p, i)),
                  pl.BlockSpec((2, TQ, 1), lambda p, i: (p, i, 0)),
                  _any_spec()],
        out_specs=[pl.BlockSpec((TQ, 2 * LANES), lambda p, i: (i, p)),
                   pl.BlockSpec((nq, 2 * LANES, TQ), lambda p, i: (0, p, 0)),
                   pl.BlockSpec((nq, LANES, TQ), lambda p, i: (0, p, 0))],
        scratch_shapes=[pltpu.VMEM((P_SLOTS, 2, TQ, P_COLS), BF16), pltpu.SemaphoreType.DMA((2 * P_SLOTS,))],
        compiler_params=pltpu.CompilerParams(dimension_semantics=("arbitrary", "arbitrary"),
                                             vmem_limit_bytes=VMEM_BIG),
    )(qm_t, km, vm, o, do, do_t, lse, p_saved)


def _mid(o_sb, g_sb, o_mla, g_mla, x, target, gate, w_out):
    s_len = x.shape[0]

    def body(osb_ref, gsb_ref, omla_ref, gmla_ref, x_ref, t_ref, gate_ref, w_ref,
             dy_ref, dosb_ref, dgsb_ref, domla_ref, dgmla_ref, gw_ref, dgate_ref, loss_ref, dosbt_ref, domlat_ref):
        @pl.when(pl.program_id(0) == 0)
        def _():
            gw_ref[...] = jnp.zeros_like(gw_ref)
            dgate_ref[...] = jnp.zeros_like(dgate_ref)
            loss_ref[...] = jnp.zeros_like(loss_ref)

        o1, g1, o2, g2 = osb_ref[...], gsb_ref[...], omla_ref[...], gmla_ref[...]
        s1, s2 = _sigmoid(g1), _sigmoid(g2)
        mixed = jnp.concatenate([o1 * (g1 * s1), o2 * (g2 * s2)], axis=1).astype(BF16)
        w = w_ref[...]
        gate_v = gate_ref[...]
        u = _dot_nn(mixed, w)
        err = x_ref[...] + gate_v * u - t_ref[...]
        loss_ref[...] += jnp.sum(err * err, axis=0, keepdims=True)
        dy = err * (1.0 / D_MODEL)
        dy_ref[...] = dy
        dgate_ref[...] += jnp.sum(dy * u, axis=0, keepdims=True)
        du = (dy * gate_v).astype(BF16)
        gw_ref[...] += _dot_tn(mixed, du)
        dmixed = _dot_nt(du, w)
        d1, d2 = dmixed[:, :SB_WIDTH], dmixed[:, SB_WIDTH:]
        do1, do2 = d1 * (g1 * s1), d2 * (g2 * s2)
        dosb_ref[...] = do1.astype(BF16)
        dgsb_ref[...] = (d1 * o1 * (s1 * (1.0 + g1 * (1.0 - s1)))).astype(BF16)
        domla_ref[...] = do2.astype(BF16)
        dgmla_ref[...] = (d2 * o2 * (s2 * (1.0 + g2 * (1.0 - s2)))).astype(BF16)
        dosbt_ref[...] = do1.T.astype(BF16)
        domlat_ref[...] = do2.T.astype(BF16)

    def tile(width):
        return pl.BlockSpec((TS, width), lambda i: (i, 0))

    def full(shape):
        return pl.BlockSpec(shape, lambda i: (0, 0))

    return pl.pallas_call(
        body, name="mid", grid=(s_len // TS,),
        out_shape=[jax.ShapeDtypeStruct((s_len, D_MODEL), F32)]
        + [jax.ShapeDtypeStruct((s_len, 512), BF16)] * 4
        + [jax.ShapeDtypeStruct((D_MODEL, D_MODEL), F32),
           jax.ShapeDtypeStruct((1, D_MODEL), F32), jax.ShapeDtypeStruct((1, D_MODEL), F32)]
        + [jax.ShapeDtypeStruct((512, s_len), BF16)] * 2,
        in_specs=[tile(512)] * 4 + [tile(D_MODEL), tile(D_MODEL), full((1, D_MODEL)), full((D_MODEL, D_MODEL))],
        out_specs=[tile(D_MODEL)] + [tile(512)] * 4
        + [full((D_MODEL, D_MODEL)), full((1, D_MODEL)), full((1, D_MODEL))]
        + [pl.BlockSpec((512, TS), lambda i: (0, i))] * 2,
        compiler_params=pltpu.CompilerParams(dimension_semantics=("arbitrary",), vmem_limit_bytes=VMEM_BIG),
    )(o_sb, g_sb, o_mla, g_mla, x, target, gate, w_out)


def _mla_pre_bwd(dq, dk, dv, cq, ckv, kr, qln, wuq, kvln, wukv, qhn, khn, rope_table):
    s_len = cq.shape[0]

    def body(dq_ref, dk_ref, dv_ref, cq_ref, ckv_ref, kr_ref, qln_ref, wuq_ref, kvln_ref, wukv_ref,
             qhn_ref, khn_ref, rt_ref,
             dcq_ref, dckv_ref, dkr_ref, gwuq_ref, gwukv_ref, gqhn_ref, gkhn_ref, gqln_ref, gkvln_ref,
             dqa_ref, dkv_ref):
        @pl.when(pl.program_id(0) == 0)
        def _():
            for r_ in (gwuq_ref, gwukv_ref, gqhn_ref, gkhn_ref, gqln_ref, gkvln_ref):
                r_[...] = jnp.zeros_like(r_)

        cq, ckv = cq_ref[...], ckv_ref[...]
        qln_v, kvln_v = qln_ref[...], kvln_ref[...]
        wuq_v, wukv_v = wuq_ref[...], wukv_ref[...]
        rq, rkv, cq_hat, ckv_hat, cqn, ckvn, q_all, kv, kr64 = _mla_latents(
            cq, ckv, kr_ref[...], qln_v, kvln_v, wuq_v, wukv_v)
        cosf, sa, sb = _expand_rope(rt_ref[...])
        qhn_v, khn_v = qhn_ref[...], khn_ref[...]
        lane = _lane_iota((TS, LANES))
        low = lane < NOPE
        blks = [slice(LANES * hd, LANES * (hd + 1)) for hd in range(HEADS)]
        raw = [q_all[:, b] for b in blks] + [jnp.where(low, kv[:, b], kr64) for b in blks]
        grads = [dq_ref[:, b] for b in blks] + [dk_ref[:, b] for b in blks]
        gains = [qhn_v] * HEADS + [khn_v] * HEADS
        rms = [_head_rms(t) for t in raw]
        xhs = [t * r for t, r in zip(raw, rms)]
        dns = [_rope_t(d, cosf, sa, sb) for d in grads]
        gain_g = [jnp.sum(dn * xh, axis=0, keepdims=True) for dn, xh in zip(dns, xhs)]
        dxs = [_rms_bwd(dn * g, xh, r, MLA_QK) for dn, g, xh, r in zip(dns, gains, xhs, rms)]
        dkr64 = jnp.zeros((TS, LANES), F32)
        for hd, b in enumerate(blks):
            dqa_ref[:, b] = dxs[hd].astype(BF16)
            dkb = dxs[HEADS + hd]
            dkr64 = dkr64 + jnp.where(low, 0.0, dkb)
            dvp = dv_ref[:, LANES * (hd // 2):LANES * (hd // 2 + 1)]
            dvh = pltpu.roll(dvp, 64, 1) if hd % 2 == 0 else dvp
            dkv_ref[:, b] = jnp.where(low, dkb, dvh).astype(BF16)
        gqhn_ref[...] += sum(gain_g[:HEADS])
        gkhn_ref[...] += sum(gain_g[HEADS:])
        dkr_ref[...] = pltpu.roll(dkr64, 64, 1).astype(BF16)

        dqa = dqa_ref[...]
        gwuq_ref[...] += _dot_tn(cqn, dqa)
        dcqn = _dot_nt(dqa, wuq_v)
        gqln_ref[...] += jnp.sum(dcqn * cq_hat, axis=0, keepdims=True)
        dcq_ref[...] = _rms_bwd(dcqn * qln_v, cq_hat, rq, Q_LORA).astype(BF16)

        dkv = dkv_ref[...]
        gwukv_ref[...] += _dot_tn(ckvn, dkv)
        dckvn = _dot_nt(dkv, wukv_v)
        gkvln_ref[...] += jnp.sum(dckvn * ckv_hat, axis=0, keepdims=True)
        dckv_ref[...] = _rms_bwd(dckvn * kvln_v, ckv_hat, rkv, KV_LORA).astype(BF16)

    def tile(width):
        return pl.BlockSpec((TS, width), lambda i: (i, 0))

    def full(shape):
        return pl.BlockSpec(shape, lambda i: (0, 0))

    acc_shapes = [(Q_LORA, 1024), (KV_LORA, 1024), (1, LANES), (1, LANES), (1, Q_LORA), (1, KV_LORA)]
    return pl.pallas_call(
        body, name="mla_pre_bwd", grid=(s_len // TS,),
        out_shape=[jax.ShapeDtypeStruct((s_len, Q_LORA), BF16), jax.ShapeDtypeStruct((s_len, KV_LORA), BF16),
                   jax.ShapeDtypeStruct((s_len, LANES), BF16)]
        + [jax.ShapeDtypeStruct(s, F32) for s in acc_shapes],
        in_specs=[tile(1024), tile(1024), tile(512), tile(Q_LORA), tile(KV_LORA), tile(LANES),
                  full(qln.shape), full(wuq.shape), full(kvln.shape), full(wukv.shape),
                  full(qhn.shape), full(khn.shape), tile(LANES)],
        out_specs=[tile(Q_LORA), tile(KV_LORA), tile(LANES)] + [full(s) for s in acc_shapes],
        scratch_shapes=[pltpu.VMEM((TS, 1024), BF16), pltpu.VMEM((TS, 1024), BF16)],
        compiler_params=pltpu.CompilerParams(dimension_semantics=("arbitrary",), vmem_limit_bytes=VMEM_BIG),
    )(dq, dk, dv, cq, ckv, kr, qln, wuq, kvln, wukv, qhn, khn, rope_table)


def _dproj_bwd(dq_sb, dk_sb, dv_sb, dg_sb, dcq, dckv, dg_mla, dkr, w_in_r, x, dy, norm_w, scale):
    s_len = x.shape[0]

    def body(dq_ref, dk_ref, dv_ref, dg_ref, dcq_ref, dckv_ref, dgm_ref, dkr_ref, w_ref, x_ref, dy_ref,
             nw_ref, scale_ref, dp_ref, gx_ref, dshift_ref, dscale_ref, dnw_ref):
        @pl.when(pl.program_id(0) == 0)
        def _():
            for r_ in (dshift_ref, dscale_ref, dnw_ref):
                r_[...] = jnp.zeros_like(r_)

        dp_ref[:, 0:512] = dq_ref[...].astype(BF16)
        dp_ref[:, 512:1024] = dk_ref[...].astype(BF16)
        dp_ref[:, 1024:1536] = dv_ref[...].astype(BF16)
        dp_ref[:, 1536:2048] = dg_ref[...]
        dp_ref[:, 2048:2432] = dcq_ref[...]
        dp_ref[:, 2432:2688] = dckv_ref[...]
        dp_ref[:, 2688:3200] = dgm_ref[...]
        dp_ref[:, 3200:3328] = dkr_ref[...]
        dh = _dot_nt(dp_ref[...], w_ref[...])
        xv = x_ref[...]
        r = lax.rsqrt(jnp.mean(xv * xv, axis=1, keepdims=True) + EPS)
        xh = xv * r
        nw = nw_ref[...]
        dshift_ref[...] += jnp.sum(dh, axis=0, keepdims=True)
        dscale_ref[...] += jnp.sum(dh * (xh * nw), axis=0, keepdims=True)
        dxnw = dh * (1.0 + scale_ref[...])
        dnw_ref[...] += jnp.sum(dxnw * xh, axis=0, keepdims=True)
        gx_ref[...] = dy_ref[...] + _rms_bwd(dxnw * nw, xh, r, D_MODEL)

    def tile(width):
        return pl.BlockSpec((TS, width), lambda i: (i, 0))

    def full(shape):
        return pl.BlockSpec(shape, lambda i: (0, 0))

    vec = (1, D_MODEL)
    return pl.pallas_call(
        body, name="dproj_bwd", grid=(s_len // TS,),
        out_shape=[jax.ShapeDtypeStruct((s_len, IN_COLS_R), BF16), jax.ShapeDtypeStruct((s_len, D_MODEL), F32)]
        + [jax.ShapeDtypeStruct(vec, F32)] * 3,
        in_specs=[tile(512)] * 4 + [tile(Q_LORA), tile(KV_LORA), tile(512), tile(LANES),
                                    full(w_in_r.shape), tile(D_MODEL), tile(D_MODEL), full(vec), full(vec)],
        out_specs=[tile(IN_COLS_R), tile(D_MODEL)] + [full(vec)] * 3,
        compiler_params=pltpu.CompilerParams(dimension_semantics=("arbitrary",), vmem_limit_bytes=VMEM_BIG),
    )(dq_sb, dk_sb, dv_sb, dg_sb, dcq, dckv, dg_mla, dkr, w_in_r, x, dy, norm_w, scale)


def _grad_w_in(hb, dproj):
    s_len = hb.shape[0]
    n_half = IN_COLS_R // 2
    n_steps = s_len // TN_S

    def body(h_ref, d_ref, g_ref, acc_ref):
        @pl.when(pl.program_id(1) == 0)
        def _():
            acc_ref[...] = jnp.zeros_like(acc_ref)

        acc_ref[...] += _dot_tn(h_ref[...], d_ref[...])

        @pl.when(pl.program_id(1) == n_steps - 1)
        def _():
            g_ref[...] = acc_ref[...].astype(BF16)

    return pl.pallas_call(
        body, name="grad_w_in", grid=(2, n_steps),
        out_shape=jax.ShapeDtypeStruct((D_MODEL, IN_COLS_R), BF16),
        in_specs=[pl.BlockSpec((TN_S, D_MODEL), lambda n, s: (s, 0)),
                  pl.BlockSpec((TN_S, n_half), lambda n, s: (s, n))],
        out_specs=pl.BlockSpec((D_MODEL, n_half), lambda n, s: (0, n)),
        scratch_shapes=[pltpu.VMEM((D_MODEL, n_half), F32)],
        compiler_params=pltpu.CompilerParams(dimension_semantics=("parallel", "arbitrary"),
                                             vmem_limit_bytes=VMEM_BIG),
    )(hb, dproj)


def _final_exchange(gpack, ccol, wpack, mpack, vpack, n_sh, grads):
    n = len(grads)

    def body(*refs):
        (g_ref, cc_ref, wp_ref, mp_ref, vp_ref) = refs[:5]
        slabs_in = refs[5:5 + n]
        (og_ref, od_ref, om_ref, ov_ref, ag_ref) = refs[5 + n:10 + n]
        slabs_out = refs[10 + n:10 + 2 * n]
        gall_ref, ssem, rsem, slab_ssem, slab_rsem, lsem = refs[10 + 2 * n:]
        pos = _mesh_pos()
        me = _lin(pos)
        gall_ref[me] = g_ref[...]
        small = _all_gather_start(pos, g_ref, gall_ref, ssem, rsem, 0)
        own = [pltpu.make_async_copy(slabs_in[a].at[me], slabs_out[a].at[me], lsem.at[a]) for a in range(n)]
        for cp in own:
            cp.start()
        in_flight = [_all_to_all_start(pos, slabs_in[a], slabs_out[a], slab_ssem, slab_rsem, a * (N_DEV - 1))
                     for a in range(n)]
        _all_gather_wait(pos, g_ref, gall_ref, ssem, rsem, 0, small)

        tot = gall_ref[0]
        for j in range(1, N_DEV):
            tot = tot + gall_ref[j]
        og_ref[...] = tot
        od_ref[...], om_ref[...], ov_ref[...] = _adamw(wp_ref[...], tot, mp_ref[...], vp_ref[...])

        ga = jnp.zeros((D_MODEL, n_sh), F32)
        for j in range(N_DEV):
            d_mine = jnp.zeros((8, n_sh), F32)
            for k in range(N_DEV):
                d_mine = d_mine + jnp.where(me == k, gall_ref[j, :, PK_ADA + n_sh * k:PK_ADA + n_sh * (k + 1)], 0.0)
            col = _silu(cc_ref[j])
            ga = ga + jnp.concatenate(
                [col * d_mine[0:1, LANES * a:LANES * (a + 1)] for a in range(n_sh // LANES)], axis=1)
        ag_ref[...] = ga

        for a in range(n):
            _all_to_all_wait(pos, slabs_in[a], slabs_out[a], slab_ssem, slab_rsem, a * (N_DEV - 1), in_flight[a])
        for cp in own:
            cp.wait()

    pk = jax.ShapeDtypeStruct((8, PK_END), F32)
    ada = jax.ShapeDtypeStruct((D_MODEL, n_sh), F32)
    return pl.pallas_call(
        body, name="final_exchange",
        out_shape=[pk] * 4 + [ada] + [jax.ShapeDtypeStruct(g.shape, g.dtype) for g in grads],
        in_specs=[_vmem_spec()] * 5 + [_any_spec()] * n,
        out_specs=[_vmem_spec()] * 5 + [_any_spec()] * n,
        scratch_shapes=[
            pltpu.VMEM((N_DEV, 8, PK_END), F32),
            pltpu.SemaphoreType.DMA((N_DEV - 1,)),
            pltpu.SemaphoreType.DMA((N_DEV - 1,)),
            pltpu.SemaphoreType.DMA((n * (N_DEV - 1),)),
            pltpu.SemaphoreType.DMA((n * (N_DEV - 1),)),
            pltpu.SemaphoreType.DMA((n,)),
        ],
        compiler_params=pltpu.CompilerParams(vmem_limit_bytes=VMEM_BIG),
    )(gpack, ccol, wpack, mpack, vpack, *grads)


def _adamw_reduce(name, parts, w, m, v, row_tile):
    rows, cols = w.shape
    n_parts = parts.shape[0]

    def body(p_ref, w_ref, m_ref, v_ref, g_ref, d_ref, mo_ref, vo_ref):
        g = p_ref[0].astype(F32)
        for j in range(1, n_parts):
            g = g + p_ref[j].astype(F32)
        g_ref[...] = g
        d_ref[...], mo_ref[...], vo_ref[...] = _adamw(w_ref[...], g, m_ref[...], v_ref[...])

    tile = pl.BlockSpec((row_tile, cols), lambda i: (i, 0))
    return pl.pallas_call(
        body, name=name, grid=(rows // row_tile,),
        out_shape=[jax.ShapeDtypeStruct((rows, cols), F32)] * 4,
        in_specs=[pl.BlockSpec((n_parts, row_tile, cols), lambda i: (0, i, 0)), tile, tile, tile],
        out_specs=[tile] * 4,
        compiler_params=pltpu.CompilerParams(dimension_semantics=("parallel",), vmem_limit_bytes=VMEM_BIG),
    )(parts, w, m, v)


def _rope_table(positions):
    inv_freq = 10000.0 ** (-jnp.arange(0, ROPE, 2, dtype=F32) / ROPE)
    ang = positions.astype(F32)[:, None] * inv_freq
    cos, sin = jnp.cos(ang), jnp.sin(ang)
    pad = jnp.zeros((positions.shape[0], LANES - 2 * ROPE), F32)
    return jnp.concatenate([cos, cos, -sin, sin, pad], axis=1)


def _rearrange_cols(w):
    pad = jnp.zeros((w.shape[0], IN_COLS_R - IN_COLS), w.dtype)
    return jnp.concatenate([w[:, :2688], w[:, 2720:3232], w[:, 2688:2720], pad], axis=1)


def _restore_cols(g):
    return jnp.concatenate([g[:, :2688], g[:, 3200:3232], g[:, 2688:3200]], axis=1)


def _pad_heads(w):
    rows = w.shape[0]
    w = w.reshape(rows, HEADS, MLA_QK)
    return jnp.pad(w, ((0, 0), (0, 0), (0, LANES - MLA_QK))).reshape(rows, HEADS * LANES)


def _unpad_heads(g):
    rows = g.shape[0]
    return g.reshape(rows, HEADS, LANES)[:, :, :MLA_QK].reshape(rows, HEADS * MLA_QK)


def _pad_lanes(v):
    return jnp.pad(v, ((0, 0), (0, LANES - v.shape[1])))


def _col_shards(g):
    rows = g.shape[0]
    return g.reshape(rows, N_DEV, g.shape[1] // N_DEV).transpose(1, 0, 2)


def _from_col_shards(g):
    return g.transpose(1, 0, 2).reshape(g.shape[1], N_DEV * g.shape[2])


def _pack(norm_w, qln, kvln, qhn, khn, ada, loss_lanes=None):
    if loss_lanes is None:
        loss_lanes = jnp.zeros((1, PK_END - PK_LOSS), F32)
    row = jnp.concatenate([norm_w, qln, kvln, _pad_lanes(qhn), _pad_lanes(khn), ada, loss_lanes], axis=1)
    return jnp.broadcast_to(row, (8, PK_END))


def _unpack(p):
    row = p[0:1]
    return (row[:, PK_NORM:PK_QLN], row[:, PK_QLN:PK_KVLN], row[:, PK_KVLN:PK_QHN],
            row[:, PK_QHN:PK_QHN + MLA_QK], row[:, PK_KHN:PK_KHN + MLA_QK], row[:, PK_ADA:PK_LOSS])


def kernel(x, c, positions, w_ada, b_ada, norm_w, w_in, q_lora_norm, w_uq, kv_lora_norm, w_ukv, q_head_norm, k_head_norm, w_out, loss_target, m_w_ada, m_b_ada, m_norm_w, m_w_in, m_q_lora_norm, m_w_uq, m_kv_lora_norm, m_w_ukv, m_q_head_norm, m_k_head_norm, m_w_out, v_w_ada, v_b_ada, v_norm_w, v_w_in, v_q_lora_norm, v_w_uq, v_kv_lora_norm, v_w_ukv, v_q_head_norm, v_k_head_norm, v_w_out):
    s_len = x.shape[1]
    x2 = x.reshape(s_len, D_MODEL)
    tgt = loss_target.reshape(s_len, D_MODEL)
    w_ada_s, w_in_s, w_uq_s, w_ukv_s, w_out_s = w_ada[0], w_in[0], w_uq[0], w_ukv[0], w_out[0]

    ada8, c_all = _ada_fwd(jnp.broadcast_to(c, (8, D_MODEL)), w_ada_s, b_ada.reshape(N_DEV, -1))
    ada = ada8.reshape(1, 3 * D_MODEL)
    shift, scale, gate = ada[:, :D_MODEL], ada[:, D_MODEL:2 * D_MODEL], ada[:, 2 * D_MODEL:]

    g_uq, g_ukv, g_out, g_in = _gather_weights([w_uq_s, w_ukv_s, w_out_s, w_in_s])
    w_in_r = _rearrange_cols(_from_col_shards(g_in))
    wuq_p = _pad_heads(_from_col_shards(g_uq))
    wukv_f = _from_col_shards(g_ukv)
    w_out_f = g_out.reshape(D_MODEL, D_MODEL)

    rope_table = _rope_table(positions[0])
    qhn_p, khn_p = _pad_lanes(q_head_norm), _pad_lanes(k_head_norm)

    hb, qkv, g_sb, cq, ckv, g_mla, kr, qm, km, vm, q_sb_t, qm_t = _fwd_pre(
        x2, shift, scale, norm_w, w_in_r, q_lora_norm, wuq_p, kv_lora_norm, wukv_f, qhn_p, khn_p, rope_table)
    o_sb, w_saved, l_saved = _sb_fwd(qkv)
    o_mla, lse, p_saved = _mla_fwd(qm, km, vm)

    dy, do_sb, dg_sb, do_mla, dg_mla, gw_out, d_gate, loss_acc, do_sb_t, do_mla_t = _mid(
        o_sb, g_sb, o_mla, g_mla, x2, tgt, gate, w_out_f)

    dq_sb, dk_sb_t, dv_sb_t = _sb_bwd(qkv, q_sb_t, do_sb, do_sb_t, w_saved, l_saved)
    dk_sb, dv_sb = _from_key_tiles(dk_sb_t), _from_key_tiles(dv_sb_t)
    dq_m, dk_m_t, dv_m_t = _mla_bwd(qm_t, km, vm, o_mla, do_mla, do_mla_t, lse, p_saved)
    dk_m, dv_m = _from_key_tiles(dk_m_t), _from_key_tiles(dv_m_t)
    dcq, dckv, dkr, gw_uq_p, gw_ukv, g_qhn, g_khn, g_qln, g_kvln = _mla_pre_bwd(
        dq_m, dk_m, dv_m, cq, ckv, kr, q_lora_norm, wuq_p, kv_lora_norm, wukv_f, qhn_p, khn_p, rope_table)
    dproj, grad_x, d_shift, d_scale, g_norm_w = _dproj_bwd(
        dq_sb, dk_sb, dv_sb, dg_sb, dcq, dckv, dg_mla, dkr, w_in_r, x2, dy, norm_w, scale)
    gw_in = _restore_cols(_grad_w_in(hb, dproj))

    d_ada = jnp.concatenate([d_shift, d_scale, d_gate], axis=1)
    gpack = _pack(g_norm_w, g_qln, g_kvln, g_qhn[:, :MLA_QK], g_khn[:, :MLA_QK], d_ada, loss_acc)
    wpack = _pack(norm_w, q_lora_norm, kv_lora_norm, q_head_norm, k_head_norm, b_ada)
    mpack = _pack(m_norm_w, m_q_lora_norm, m_kv_lora_norm, m_q_head_norm, m_k_head_norm, m_b_ada)
    vpack = _pack(v_norm_w, v_q_lora_norm, v_kv_lora_norm, v_q_head_norm, v_k_head_norm, v_b_ada)
    ccol = jnp.broadcast_to(c_all[:, :, None], (N_DEV, D_MODEL, LANES))
    slabs = [g.astype(BF16) for g in (
        _col_shards(gw_in), _col_shards(_unpad_heads(gw_uq_p)), _col_shards(gw_ukv),
        gw_out.reshape(N_DEV, D_MODEL // N_DEV, D_MODEL))]
    pg, pd, pm, pv, gw_ada, r_in, r_uq, r_ukv, r_out = _final_exchange(
        gpack, ccol, wpack, mpack, vpack, w_ada_s.shape[1], slabs)
    loss = 0.5 * jnp.sum(pg[0, PK_LOSS:PK_END]) / D_MODEL

    ada_g, ada_d, ada_m, ada_v = _adamw_reduce("adamw_w_ada", gw_ada[None], w_ada_s, m_w_ada[0], v_w_ada[0], 256)
    in_g, in_d, in_m, in_v = _adamw_reduce("adamw_w_in", r_in, w_in_s, m_w_in[0], v_w_in[0], 256)
    uq_g, uq_d, uq_m, uq_v = _adamw_reduce("adamw_w_uq", r_uq, w_uq_s, m_w_uq[0], v_w_uq[0], w_uq_s.shape[0])
    ukv_g, ukv_d, ukv_m, ukv_v = _adamw_reduce(
        "adamw_w_ukv", r_ukv, w_ukv_s, m_w_ukv[0], v_w_ukv[0], w_ukv_s.shape[0])
    out_g, out_d, out_m, out_v = _adamw_reduce(
        "adamw_w_out", r_out, w_out_s, m_w_out[0], v_w_out[0], w_out_s.shape[0])

    def group(ada_t, pk, in_t, uq_t, ukv_t, out_t):
        nw, qln, kvln, qhn, khn, b = _unpack(pk)
        return (ada_t[None], b, nw, in_t[None], qln, uq_t[None], kvln, ukv_t[None], qhn, khn, out_t[None])

    return (loss, grad_x.reshape(1, s_len, D_MODEL),
            *group(ada_g, pg, in_g, uq_g, ukv_g, out_g),
            *group(ada_d, pd, in_d, uq_d, ukv_d, out_d),
            *group(ada_m, pm, in_m, uq_m, ukv_m, out_m),
            *group(ada_v, pv, in_v, uq_v, ukv_v, out_v))
```

```python
import functools
import math

import jax
import jax.numpy as jnp
from jax import lax
from jax.experimental import pallas as pl
from jax.experimental.pallas import tpu as pltpu

F32 = jnp.float32
BF16 = jnp.bfloat16

N_DEV = 8
D_MODEL = 1024
HEADS = 8
SB_WIDTH = 512
MLA_WIDTH = 512
Q_LORA = 384
KV_LORA = 256
ROPE = 32
NOPE = 64
MLA_QK = 96
LANES = 128
IN_COLS = 3232
IN_COLS_R = 3328
EPS = 1e-6
NEG = -1e30

ADAM_LR = 0.001
ADAM_B1 = 0.9
ADAM_B2 = 0.999
ADAM_EPS = 1e-08
ADAM_WD = 0.01
ADAM_STEP = 10

TS = 512
TS_FWD = 256
TQ = 512
SB_UNROLL = 2
MLA_UNROLL = 4
TN_S = 512
VMEM_BIG = 56 * 1024 * 1024

PK_NORM, PK_QLN, PK_KVLN, PK_QHN, PK_KHN, PK_ADA, PK_LOSS, PK_END = 0, 1024, 1408, 1664, 1792, 1920, 4992, 6016

MESH_ID = pl.DeviceIdType.MESH


def _dot_nn(a, b):
    return lax.dot_general(a, b, (((1,), (0,)), ((), ())), preferred_element_type=F32)


def _dot_nt(a, b):
    return lax.dot_general(a, b, (((1,), (1,)), ((), ())), preferred_element_type=F32)


def _dot_tn(a, b):
    return lax.dot_general(a, b, (((0,), (0,)), ((), ())), preferred_element_type=F32)


def _split_bf16(a):
    hi = a.astype(BF16)
    lo = (a - hi.astype(F32)).astype(BF16)
    return hi, lo


def _dot3(a, b):
    ah, al = _split_bf16(a)
    bh, bl = _split_bf16(b)
    return _dot_nn(ah, bh) + _dot_nn(ah, bl) + _dot_nn(al, bh)


def _sigmoid(g):
    return 1.0 / (1.0 + jnp.exp(-g))


def _silu(g):
    return g * _sigmoid(g)


def _lane_iota(shape):
    return lax.broadcasted_iota(jnp.int32, shape, len(shape) - 1)


def _adamw(w, g, m, v):
    m = ADAM_B1 * m + (1.0 - ADAM_B1) * g
    v = ADAM_B2 * v + (1.0 - ADAM_B2) * (g * g)
    m_hat = m / (1.0 - ADAM_B1 ** ADAM_STEP)
    v_hat = v / (1.0 - ADAM_B2 ** ADAM_STEP)
    delta = -ADAM_LR * (m_hat / (jnp.sqrt(v_hat) + ADAM_EPS) + ADAM_WD * w)
    return delta, m, v


def _mesh_pos():
    return lax.axis_index("x"), lax.axis_index("y"), lax.axis_index("c")


def _peer(pos, k):
    x, y, c = pos
    return (1 - x if k & 4 else x, 1 - y if k & 2 else y, 1 - c if k & 1 else c)


def _lin(pos):
    return 4 * pos[0] + 2 * pos[1] + pos[2]


def _remote(src, dst, send_sems, recv_sems, idx, peer):
    return pltpu.make_async_remote_copy(
        src_ref=src, dst_ref=dst, send_sem=send_sems.at[idx], recv_sem=recv_sems.at[idx],
        device_id=peer, device_id_type=MESH_ID)


def _all_gather_start(pos, src, buf, send_sems, recv_sems, base):
    me = _lin(pos)
    sent = []
    for k in range(1, N_DEV):
        cp = _remote(src, buf.at[me], send_sems, recv_sems, base + k - 1, _peer(pos, k))
        cp.start()
        sent.append(cp)
    return sent


def _all_gather_wait(pos, src, buf, send_sems, recv_sems, base, sent):
    for k in range(1, N_DEV):
        peer = _peer(pos, k)
        _remote(src, buf.at[_lin(peer)], send_sems, recv_sems, base + k - 1, peer).wait_recv()
    for cp in sent:
        cp.wait_send()


def _all_gather(pos, src, buf, send_sems, recv_sems, base):
    sent = _all_gather_start(pos, src, buf, send_sems, recv_sems, base)
    _all_gather_wait(pos, src, buf, send_sems, recv_sems, base, sent)


def _all_to_all_start(pos, src, buf, send_sems, recv_sems, base):
    me = _lin(pos)
    sent = []
    for k in range(1, N_DEV):
        peer = _peer(pos, k)
        cp = _remote(src.at[_lin(peer)], buf.at[me], send_sems, recv_sems, base + k - 1, peer)
        cp.start()
        sent.append(cp)
    return sent


def _all_to_all_wait(pos, src, buf, send_sems, recv_sems, base, sent):
    me = _lin(pos)
    for k in range(1, N_DEV):
        peer = _peer(pos, k)
        _remote(src.at[me], buf.at[_lin(peer)], send_sems, recv_sems, base + k - 1, peer).wait_recv()
    for cp in sent:
        cp.wait_send()


def _all_to_all(pos, src, buf, send_sems, recv_sems, base):
    sent = _all_to_all_start(pos, src, buf, send_sems, recv_sems, base)
    _all_to_all_wait(pos, src, buf, send_sems, recv_sems, base, sent)


def _two_level_gather(pos, bufs, send_sems, recv_sems):
    x, y, c = pos
    me, sibling = (x, y, c), (x, y, 1 - c)
    chips = [(1 - x, y), (x, 1 - y), (1 - x, 1 - y)]

    def copy(a, k, block, to):
        slot = bufs[a].at[_lin(block)]
        return _remote(slot, slot, send_sems, recv_sems, 7 * a + k, to)

    started = []
    for a in range(len(bufs)):
        first = [copy(a, 0, me, sibling)] + [copy(a, 1 + j, me, (*chip, c)) for j, chip in enumerate(chips)]
        for cp in first:
            cp.start()
        started += first
    for a in range(len(bufs)):
        for j, chip in enumerate(chips):
            copy(a, 1 + j, (*chip, c), me).wait_recv()
            passed = copy(a, 4 + j, (*chip, c), sibling)
            passed.start()
            started.append(passed)
    for a in range(len(bufs)):
        copy(a, 0, sibling, me).wait_recv()
        for j, chip in enumerate(chips):
            copy(a, 4 + j, (*chip, 1 - c), me).wait_recv()
    for cp in started:
        cp.wait_send()


def _vmem_spec():
    return pl.BlockSpec(memory_space=pltpu.VMEM)


def _any_spec():
    return pl.BlockSpec(memory_space=pl.ANY)


def _row_select(slots, n):
    r = lax.broadcasted_iota(jnp.int32, (N_DEV, n), 0)
    out = jnp.zeros((N_DEV, n), F32)
    for j in range(N_DEV):
        out = out + jnp.where(r == j, slots[j], 0.0)
    return out


def _ada_fwd(c8, w_ada, b_ada8):
    n_sh = w_ada.shape[1]

    def body(c_ref, w_ref, b_ref, out_ref, call_out_ref, call_ref, psend_ref, precv_ref, ssem, rsem):
        pos = _mesh_pos()
        me = _lin(pos)
        call_ref[me] = c_ref[...]
        _all_gather(pos, c_ref, call_ref, ssem, rsem, 0)
        call_out_ref[...] = _row_select([call_ref[j] for j in range(N_DEV)], D_MODEL)
        w = w_ref[...]
        for j in range(N_DEV):
            psend_ref[j] = _dot3(_silu(call_ref[j]), w)
        precv_ref[me] = psend_ref[me]
        _all_to_all(pos, psend_ref, precv_ref, ssem, rsem, N_DEV - 1)
        out_ref[...] = _row_select([precv_ref[j] for j in range(N_DEV)], n_sh) + b_ref[...]

    return pl.pallas_call(
        body, name="ada_fwd",
        out_shape=[jax.ShapeDtypeStruct((N_DEV, n_sh), F32), jax.ShapeDtypeStruct((N_DEV, D_MODEL), F32)],
        in_specs=[_vmem_spec()] * 3, out_specs=[_vmem_spec()] * 2,
        scratch_shapes=[
            pltpu.VMEM((N_DEV, 8, D_MODEL), F32),
            pltpu.VMEM((N_DEV, 8, n_sh), F32),
            pltpu.VMEM((N_DEV, 8, n_sh), F32),
            pltpu.SemaphoreType.DMA((2 * (N_DEV - 1),)),
            pltpu.SemaphoreType.DMA((2 * (N_DEV - 1),)),
        ],
    )(c8, w_ada, b_ada8)


def _gather_weights(shards):
    n = len(shards)

    def body(*refs):
        ins, outs = refs[:n], refs[n:2 * n]
        ssem, rsem = refs[2 * n], refs[2 * n + 1]
        pos = _mesh_pos()
        me = _lin(pos)
        for a in range(n):
            outs[a][me] = ins[a][...].astype(BF16)
        _two_level_gather(pos, outs, ssem, rsem)

    return pl.pallas_call(
        body, name="gather_weights",
        out_shape=[jax.ShapeDtypeStruct((N_DEV,) + s.shape, BF16) for s in shards],
        in_specs=[_vmem_spec()] * n, out_specs=[_vmem_spec()] * n,
        scratch_shapes=[
            pltpu.SemaphoreType.DMA((n * (N_DEV - 1),)),
            pltpu.SemaphoreType.DMA((n * (N_DEV - 1),)),
        ],
        compiler_params=pltpu.CompilerParams(vmem_limit_bytes=VMEM_BIG),
    )(*shards)


def _rope(t, cosf, sin_a, sin_b):
    return t * cosf + pltpu.roll(t, 112, 1) * sin_a + pltpu.roll(t, 16, 1) * sin_b


def _expand_rope(table):
    lane = _lane_iota(table.shape)
    by64, by32 = pltpu.roll(table, 64, 1), pltpu.roll(table, 32, 1)
    cosf = jnp.where(jnp.logical_and(lane >= NOPE, lane < MLA_QK), by64, 1.0)
    sin_a = jnp.where(jnp.logical_and(lane >= NOPE, lane < NOPE + ROPE // 2), by32, 0.0)
    sin_b = jnp.where(jnp.logical_and(lane >= NOPE + ROPE // 2, lane < MLA_QK), by32, 0.0)
    return cosf, sin_a, sin_b


def _rope_t(d, cosf, sin_a, sin_b):
    return d * cosf + pltpu.roll(d * sin_a, 16, 1) + pltpu.roll(d * sin_b, 112, 1)


def _head_rms(t):
    return lax.rsqrt(jnp.sum(t * t, axis=1, keepdims=True) * (1.0 / MLA_QK) + EPS)


def _rms_bwd(dxhat_w, xhat, r, n):
    return r * (dxhat_w - xhat * (jnp.sum(dxhat_w * xhat, axis=1, keepdims=True) * (1.0 / n)))


def _mla_latents(cq, ckv, kr, qln, kvln, wuq, wukv):
    rq = lax.rsqrt(jnp.mean(cq * cq, axis=1, keepdims=True) + EPS)
    rkv = lax.rsqrt(jnp.mean(ckv * ckv, axis=1, keepdims=True) + EPS)
    cq_hat = cq * rq
    ckv_hat = ckv * rkv
    cqn = (cq_hat * qln).astype(BF16)
    ckvn = (ckv_hat * kvln).astype(BF16)
    q_all = _dot_nn(cqn, wuq)
    kv = _dot_nn(ckvn, wukv)
    kr64 = pltpu.roll(kr, 64, 1)
    return rq, rkv, cq_hat, ckv_hat, cqn, ckvn, q_all, kv, kr64


def _fwd_pre(x, shift, scale, norm_w, w_in_r, qln, wuq, kvln, wukv, qhn, khn, rope_table):
    s_len = x.shape[0]

    def body(x_ref, shift_ref, scale_ref, nw_ref, w_ref, qln_ref, wuq_ref, kvln_ref, wukv_ref,
             qhn_ref, khn_ref, rt_ref,
             hb_ref, qkv_ref, gsb_ref, cq_ref, ckv_ref, gmla_ref, kr_ref, qm_ref, km_ref, vm_ref,
             qsbt_ref, qmt_ref):
        xv = x_ref[...]
        r = lax.rsqrt(jnp.mean(xv * xv, axis=1, keepdims=True) + EPS)
        h = (xv * r) * nw_ref[...] * (1.0 + scale_ref[...]) + shift_ref[...]
        hb = h.astype(BF16)
        hb_ref[...] = hb
        qkv = _dot_nn(hb, w_ref[:, 0:1536])
        qkv_ref[...] = qkv.astype(BF16)
        qsbt_ref[...] = qkv[:, :SB_WIDTH].T.astype(BF16)
        gsb_ref[...] = _dot_nn(hb, w_ref[:, 1536:2048])
        cq = _dot_nn(hb, w_ref[:, 2048:2432])
        ckv = _dot_nn(hb, w_ref[:, 2432:2688])
        gmla_ref[...] = _dot_nn(hb, w_ref[:, 2688:3200])
        kr = _dot_nn(hb, w_ref[:, 3200:3328])
        cq_ref[...] = cq
        ckv_ref[...] = ckv
        kr_ref[...] = kr
        _, _, _, _, _, _, q_all, kv, kr64 = _mla_latents(
            cq, ckv, kr, qln_ref[...], kvln_ref[...], wuq_ref[...], wukv_ref[...])
        cosf, sa, sb = _expand_rope(rt_ref[...])
        qhn_v, khn_v = qhn_ref[...], khn_ref[...]
        low = _lane_iota((TS_FWD, LANES)) < NOPE
        blks = [slice(LANES * hd, LANES * (hd + 1)) for hd in range(HEADS)]
        q_raw = [q_all[:, b] for b in blks]
        k_raw = [jnp.where(low, kv[:, b], kr64) for b in blks]
        q_rms = [_head_rms(t) for t in q_raw]
        k_rms = [_head_rms(t) for t in k_raw]
        q_n = [t * r * (qhn_v * MLA_SCALE) for t, r in zip(q_raw, q_rms)]
        q_roped = [_rope(t, cosf, sa, sb) for t in q_n]
        kr_roped = _rope(kr64 * khn_v, cosf, sa, sb)
        k_roped = [jnp.where(low, t * khn_v, kr_roped) * r for t, r in zip(k_raw, k_rms)]
        for hd, b in enumerate(blks):
            qm_ref[:, b] = q_roped[hd].astype(BF16)
            qmt_ref[b, :] = q_roped[hd].T.astype(BF16)
            km_ref[:, b] = k_roped[hd].astype(BF16)
        for p in range(HEADS // 2):
            even = kv[:, LANES * 2 * p:LANES * (2 * p + 1)]
            odd = kv[:, LANES * (2 * p + 1):LANES * (2 * p + 2)]
            vm_ref[:, LANES * p:LANES * (p + 1)] = jnp.where(low, pltpu.roll(even, 64, 1), odd).astype(BF16)

    def tile(width):
        return pl.BlockSpec((TS_FWD, width), lambda i: (i, 0))

    def full(a):
        return pl.BlockSpec(a.shape, lambda i: (0, 0))

    out_widths = [(D_MODEL, BF16), (1536, BF16), (512, F32), (Q_LORA, F32), (KV_LORA, F32),
                  (512, F32), (LANES, F32), (1024, BF16), (1024, BF16), (512, BF16)]
    t_heights = [SB_WIDTH, HEADS * LANES]
    return pl.pallas_call(
        body, name="fwd_pre", grid=(s_len // TS_FWD,),
        out_shape=[jax.ShapeDtypeStruct((s_len, w), dt) for w, dt in out_widths]
        + [jax.ShapeDtypeStruct((hgt, s_len), BF16) for hgt in t_heights],
        in_specs=[tile(D_MODEL), full(shift), full(scale), full(norm_w), full(w_in_r), full(qln), full(wuq),
                  full(kvln), full(wukv), full(qhn), full(khn), tile(LANES)],
        out_specs=[tile(w) for w, _ in out_widths]
        + [pl.BlockSpec((hgt, TS_FWD), lambda i: (0, i)) for hgt in t_heights],
        compiler_params=pltpu.CompilerParams(dimension_semantics=("parallel",), vmem_limit_bytes=VMEM_BIG),
    )(x, shift, scale, norm_w, w_in_r, qln, wuq, kvln, wukv, qhn, khn, rope_table)


CUM_W = 256


def _tri(later):
    j = lax.broadcasted_iota(jnp.int32, (CUM_W, CUM_W), 0)
    s = lax.broadcasted_iota(jnp.int32, (CUM_W, CUM_W), 1)
    return (j > s if later else j < s).astype(BF16)


def _suffix_sums(a, a_bf16, tri_m, carry):
    n = a.shape[1] // CUM_W
    outs = [None] * n
    for i in reversed(range(n)):
        cols = slice(CUM_W * i, CUM_W * (i + 1))
        outs[i] = _dot_nn(a_bf16[:, cols], tri_m) + carry
        carry = carry + _rowsum(a[:, cols])
    return (outs[0] if n == 1 else jnp.concatenate(outs, axis=1)), carry


def _sb_weights(qm, kb, carry, tri_u, diag):
    z = _dot_nt(qm, kb)
    nz = -z
    lk = jnp.minimum(nz, 0.0) - jnp.log(1.0 + jnp.exp(jnp.minimum(z, nz)))
    if diag:
        t = lax.broadcasted_iota(jnp.int32, (TQ, TQ), 0)
        s = lax.broadcasted_iota(jnp.int32, (TQ, TQ), 1)
        valid = s < t
        lk = jnp.where(valid, lk, 0.0)
    lk_hi = lk.astype(BF16)
    log_beta = z + lk
    after, carry = _suffix_sums(lk, lk_hi, tri_u, carry)
    logw = log_beta + after
    if diag:
        logw = jnp.where(valid, logw, NEG)
    return lk_hi, jnp.exp(logw), carry


SB_SCALE = 0.125


def _head_masks():
    lane = _lane_iota((1, LANES))
    return [lane < 64, lane >= 64]


def _masked(hm, a):
    return jnp.where(hm, a, jnp.zeros_like(a))


def _rowsum(a):
    return jnp.sum(a, axis=1, keepdims=True)


def _key_rows(kj):
    return pl.ds(pl.multiple_of(kj * TQ, TQ), TQ)


def _over_key_tiles(count, fn, st, ascending, unroll):
    n_full = count // unroll
    n_rest = count - n_full * unroll

    def group(g, s_):
        return fn([g * unroll + (u if ascending else unroll - 1 - u) for u in range(unroll)], s_)

    def left_over(s_):
        for r in range(1, unroll):
            tiles = [n_full * unroll + u if ascending else count - 1 - u for u in range(r)]
            s_ = lax.cond(n_rest == r, lambda a, tiles=tiles: fn(tiles, a), lambda a: a, s_)
        return s_

    if ascending:
        return left_over(lax.fori_loop(0, n_full, group, st))
    return lax.fori_loop(0, n_full, lambda i, s_: group(n_full - 1 - i, s_), left_over(st))


STAGE_SLOTS = 2 * SB_UNROLL


def _stage_copies(to_hbm, hbm_refs, scr_refs, sems, pair, qi, kj):
    slot = (qi - kj) % STAGE_SLOTS
    out = []
    for h in range(2):
        for a in range(2):
            hbm, scr = hbm_refs[a].at[2 * pair + h, qi, kj], scr_refs[a].at[slot, h]
            sem = sems.at[4 * slot + 2 * h + a]
            out.append(pltpu.make_async_copy(scr, hbm, sem) if to_hbm else pltpu.make_async_copy(hbm, scr, sem))
    return out


def _sb_fwd(qkv):
    s_len = qkv.shape[0]
    nq = s_len // TQ

    def body(q_ref, k_ref, v_ref, o_ref, w_hbm, l_hbm, w_scr, l_scr, sems):
        pair, qi = pl.program_id(0), pl.program_id(1)
        q = q_ref[...]
        tri_u = _tri(True)
        masks = _head_masks()
        qms = [_masked(hm, q) * SB_SCALE for hm in masks]

        def copies(kj, of_qi=qi):
            return _stage_copies(True, (w_hbm, l_hbm), (w_scr, l_scr), sems, pair, of_qi, kj)

        def drain(of_qi):
            for kj in range(STAGE_SLOTS):
                @pl.when(kj <= of_qi)
                def _():
                    for cp in copies(kj, of_qi):
                        cp.wait()

        def block(kj, st, diag, before_staging=None):
            rows = _key_rows(kj)
            slot = (qi - kj) % STAGE_SLOTS
            kb, vb = k_ref[rows, :], v_ref[rows, :]
            carries, acc = list(st[:2]), st[2]
            staged = []
            for h in range(2):
                lk_hi, w, carries[h] = _sb_weights(qms[h], kb, carries[h], tri_u, diag)
                wb = w.astype(BF16)
                staged.append((wb, lk_hi))
                acc = acc + _dot_nn(wb, _masked(masks[h], vb))
            if before_staging is not None:
                before_staging()
            for h in range(2):
                w_scr[slot, h], l_scr[slot, h] = staged[h]
            return carries[0], carries[1], acc

        def trip(tiles, st):
            for kj in tiles:
                @pl.when(qi - kj >= STAGE_SLOTS)
                def _():
                    for cp in copies(kj + STAGE_SLOTS):
                        cp.wait()
            for kj in tiles:
                st = block(kj, st, False)
            for kj in tiles:
                for cp in copies(kj):
                    cp.start()
            return st

        def drain_previous_step():
            @pl.when(jnp.logical_or(pair > 0, qi > 0))
            def _():
                drain(jnp.where(qi == 0, nq - 1, qi - 1))

        zc = jnp.zeros((TQ, 1), F32)
        st = block(qi, (zc, zc, jnp.zeros((TQ, LANES), F32)), True, drain_previous_step)
        for cp in copies(qi):
            cp.start()
        st = _over_key_tiles(qi, trip, st, ascending=False, unroll=SB_UNROLL)
        o_ref[...] = st[2]

        @pl.when(jnp.logical_and(pair == HEADS // 2 - 1, qi == nq - 1))
        def _():
            drain(qi)

    saved = jax.ShapeDtypeStruct((HEADS, nq, nq, TQ, TQ), BF16)
    stage = pltpu.VMEM((STAGE_SLOTS, 2, TQ, TQ), BF16)
    return pl.pallas_call(
        body, name="sb_fwd", grid=(HEADS // 2, nq),
        out_shape=[jax.ShapeDtypeStruct((s_len, SB_WIDTH), F32), saved, saved],
        in_specs=[pl.BlockSpec((TQ, LANES), lambda p, i: (i, p)),
                  pl.BlockSpec((s_len, LANES), lambda p, i: (0, 4 + p)),
                  pl.BlockSpec((s_len, LANES), lambda p, i: (0, 8 + p))],
        out_specs=[pl.BlockSpec((TQ, LANES), lambda p, i: (i, p)), _any_spec(), _any_spec()],
        scratch_shapes=[stage, stage, pltpu.SemaphoreType.DMA((4 * STAGE_SLOTS,))],
        compiler_params=pltpu.CompilerParams(dimension_semantics=("arbitrary", "arbitrary"),
                                             vmem_limit_bytes=VMEM_BIG),
    )(qkv, qkv, qkv)


def _prefix_sums(a, tri_m, carry):
    n = a.shape[1] // CUM_W
    outs = [None] * n
    for i in range(n):
        cols = slice(CUM_W * i, CUM_W * (i + 1))
        outs[i] = _dot_nn(a[:, cols].astype(BF16), tri_m) + carry
        carry = carry + _rowsum(a[:, cols])
    return (outs[0] if n == 1 else jnp.concatenate(outs, axis=1)), carry


def _head_rows(a0, a1):
    sub = lax.broadcasted_iota(jnp.int32, a0.shape, 0)
    return jnp.where(sub < NOPE, a0, a1)


def _sb_bwd(qkv, q_t, do, do_t, w_saved, l_saved):
    s_len = qkv.shape[0]
    nq = s_len // TQ

    def body(qt_ref, k_ref, v_ref, do_ref, dot_ref, w_hbm, l_hbm, dq_ref, dk_ref, dv_ref, w_scr, l_scr, sems):
        pair, qi = pl.program_id(0), pl.program_id(1)

        def copies(kj, of_pair=pair, of_qi=qi):
            return _stage_copies(False, (w_hbm, l_hbm), (w_scr, l_scr), sems, of_pair, of_qi, kj)

        def start_first_tiles(of_pair, of_qi):
            for first in range(SB_UNROLL):
                @pl.when(first <= of_qi)
                def _():
                    for cp in copies(first, of_pair, of_qi):
                        cp.start()

        @pl.when(jnp.logical_and(pair == 0, qi == 0))
        def _():
            start_first_tiles(pair, qi)

        @pl.when(qi == 0)
        def _():
            dk_ref[...] = jnp.zeros_like(dk_ref)
            dv_ref[...] = jnp.zeros_like(dv_ref)

        qt = qt_ref[...] * SB_SCALE
        dot_v = dot_ref[...]
        do_v = do_ref[...]
        tri_before = _tri(False)
        masks = _head_masks()
        doms = [_masked(hm, do_v) for hm in masks]

        def block(kj, st):
            rows = _key_rows(kj)
            slot = (qi - kj) % STAGE_SLOTS
            kb, vb = k_ref[rows, :], v_ref[rows, :]
            carries, dqs = list(st[0:2]), list(st[2:4])
            dk_t, dv_t = [], []
            for h in range(2):
                wb = w_scr[slot, h]
                d_l = _dot_nt(doms[h], vb) * wb.astype(F32)
                before, carries[h] = _prefix_sums(d_l, tri_before, carries[h])
                keep = jnp.exp(l_scr[slot, h].astype(F32))
                dzb = (d_l * keep - before * (1.0 - keep)).astype(BF16)
                dk_t.append(_dot_nn(qt, dzb))
                dv_t.append(_dot_nn(dot_v, wb))
                dqs[h] = dqs[h] + _dot_nn(dzb, kb)
            dk_ref[kj] += _head_rows(*dk_t)
            dv_ref[kj] += _head_rows(*dv_t)
            return (*carries, *dqs)

        def trip(tiles, st):
            for kj in tiles:
                @pl.when(kj + SB_UNROLL <= qi)
                def _():
                    for cp in copies(kj + SB_UNROLL):
                        cp.start()
            for kj in tiles:
                for cp in copies(kj):
                    cp.wait()
            for kj in tiles:
                st = block(kj, st)
            return st

        zc = jnp.zeros((TQ, 1), F32)
        za = jnp.zeros((TQ, LANES), F32)
        st = _over_key_tiles(qi + 1, trip, (zc, zc, za, za), ascending=True, unroll=SB_UNROLL)
        dq_ref[...] = jnp.where(masks[0], st[2], st[3]) * SB_SCALE

        @pl.when(jnp.logical_or(pair < HEADS // 2 - 1, qi < nq - 1))
        def _():
            wraps = qi == nq - 1
            start_first_tiles(jnp.where(wraps, pair + 1, pair), jnp.where(wraps, 0, qi + 1))

    tile = pl.BlockSpec((TQ, LANES), lambda p, i: (i, p))
    tile_t = pl.BlockSpec((LANES, TQ), lambda p, i: (p, i))
    col_t = pl.BlockSpec((nq, LANES, TQ), lambda p, i: (0, p, 0))
    stage = pltpu.VMEM((STAGE_SLOTS, 2, TQ, TQ), BF16)
    key_t = jax.ShapeDtypeStruct((nq, SB_WIDTH, TQ), F32)
    return pl.pallas_call(
        body, name="sb_bwd", grid=(HEADS // 2, nq),
        out_shape=[jax.ShapeDtypeStruct((s_len, SB_WIDTH), F32), key_t, key_t],
        in_specs=[tile_t,
                  pl.BlockSpec((s_len, LANES), lambda p, i: (0, 4 + p)),
                  pl.BlockSpec((s_len, LANES), lambda p, i: (0, 8 + p)),
                  tile, tile_t, _any_spec(), _any_spec()],
        out_specs=[tile, col_t, col_t],
        scratch_shapes=[stage, stage, pltpu.SemaphoreType.DMA((4 * STAGE_SLOTS,))],
        compiler_params=pltpu.CompilerParams(dimension_semantics=("arbitrary", "arbitrary"),
                                             vmem_limit_bytes=VMEM_BIG),
    )(q_t, qkv, qkv, do, do_t, w_saved, l_saved)


def _from_key_tiles(a_t):
    return a_t.transpose(0, 2, 1).reshape(a_t.shape[0] * a_t.shape[2], a_t.shape[1])


MLA_SCALE = 1.0 / math.sqrt(MLA_QK)


def _causal_mask():
    t = lax.broadcasted_iota(jnp.int32, (TQ, TQ), 0)
    s = lax.broadcasted_iota(jnp.int32, (TQ, TQ), 1)
    return s <= t


def _head_lanes(h):
    return slice(LANES * h, LANES * (h + 1))


P_SLOTS = 2 * MLA_UNROLL
P_COLS = TQ + LANES


def _tile_number(pair, qi, kj, nq):
    return pair * (nq * (nq + 1) // 2) + (qi * (qi + 1)) // 2 + kj


def _p_copy(to_hbm, p_hbm, p_scr, sems, pair, qi, kj, nq, h):
    slot = _tile_number(pair, qi, kj, nq) % P_SLOTS
    hbm, scr, sem = p_hbm.at[2 * pair + h, qi, kj], p_scr.at[slot, h], sems.at[2 * slot + h]
    return pltpu.make_async_copy(scr, hbm, sem) if to_hbm else pltpu.make_async_copy(hbm, scr, sem)


def _mla_fwd(qm, km, vm):
    s_len = qm.shape[0]
    nq = s_len // TQ

    def body(q_ref, k_ref, v_ref, o_ref, lse_ref, p_hbm, p_scr, sems):
        pair, qi = pl.program_id(0), pl.program_id(1)
        masks = _head_masks()
        qhs = [q_ref[:, _head_lanes(h)] for h in range(2)]
        lane = _lane_iota((TQ, LANES))

        def number(kj):
            return _tile_number(pair, qi, kj, nq)

        def copies(kj):
            return [_p_copy(True, p_hbm, p_scr, sems, pair, qi, kj, nq, h) for h in range(2)]

        def two_parts(m):
            hi = m.astype(BF16).astype(F32)
            return jnp.where(lane < 64, hi, m - hi).astype(BF16)

        def block(kj, st, diag):
            rows = _key_rows(kj)
            slot = number(kj) % P_SLOTS
            vb = v_ref[rows, :]
            ms, ls, acc = list(st[0:2]), list(st[2:4]), st[4]
            alphas, pvs = [], []
            for h in range(2):
                s = _dot_nt(qhs[h], k_ref[rows, _head_lanes(h)])
                if diag:
                    s = jnp.where(_causal_mask(), s, NEG)
                m_new = jnp.maximum(ms[h], jnp.max(s, axis=1, keepdims=True))
                p = jnp.exp(s - m_new)
                pb = p.astype(BF16)
                p_scr[slot, h, :, 0:TQ] = pb
                p_scr[slot, h, :, TQ:P_COLS] = two_parts(m_new)
                alphas.append(jnp.exp(ms[h] - m_new))
                ls[h] = alphas[h] * ls[h] + _rowsum(p)
                ms[h] = m_new
                pvs.append(_dot_nn(pb, _masked(masks[h], vb)))
            acc = jnp.where(masks[0], alphas[0], alphas[1]) * acc + pvs[0] + pvs[1]
            return (*ms, *ls, acc)

        def trip(tiles, st, diag=False):
            for kj in tiles:
                @pl.when(number(kj) >= P_SLOTS)
                def _():
                    for cp in copies(kj):
                        cp.wait()
            for kj in tiles:
                st = block(kj, st, diag)
            for kj in tiles:
                for cp in copies(kj):
                    cp.start()
            return st

        neg = jnp.full((TQ, 1), NEG, F32)
        zc = jnp.zeros((TQ, 1), F32)
        st = (neg, neg, zc, zc, jnp.zeros((TQ, LANES), F32))
        st = _over_key_tiles(qi, trip, st, ascending=True, unroll=MLA_UNROLL)
        m0, m1, l0, l1, acc = trip([qi], st, True)
        o_ref[...] = acc / jnp.where(masks[0], l0, l1)
        lse_ref[0] = m0 + jnp.log(l0)
        lse_ref[1] = m1 + jnp.log(l1)

        @pl.when(jnp.logical_and(pair == HEADS // 2 - 1, qi == nq - 1))
        def _():
            for slot in range(P_SLOTS):
                for h in range(2):
                    pltpu.make_async_copy(p_scr.at[slot, h], p_hbm.at[0, 0, 0], sems.at[2 * slot + h]).wait()

    return pl.pallas_call(
        body, name="mla_fwd", grid=(HEADS // 2, nq),
        out_shape=[jax.ShapeDtypeStruct((s_len, MLA_WIDTH), F32),
                   jax.ShapeDtypeStruct((HEADS, s_len, 1), F32),
                   jax.ShapeDtypeStruct((HEADS, nq, nq, TQ, P_COLS), BF16)],
        in_specs=[pl.BlockSpec((TQ, 2 * LANES), lambda p, i: (i, p)),
                  pl.BlockSpec((s_len, 2 * LANES), lambda p, i: (0, p)),
                  pl.BlockSpec((s_len, LANES), lambda p, i: (0, p))],
        out_specs=[pl.BlockSpec((TQ, LANES), lambda p, i: (i, p)),
                   pl.BlockSpec((2, TQ, 1), lambda p, i: (p, i, 0)),
                   _any_spec()],
        scratch_shapes=[pltpu.VMEM((P_SLOTS, 2, TQ, P_COLS), BF16), pltpu.SemaphoreType.DMA((2 * P_SLOTS,))],
        compiler_params=pltpu.CompilerParams(dimension_semantics=("arbitrary", "arbitrary"),
                                             vmem_limit_bytes=VMEM_BIG),
    )(qm, km, vm)


def _mla_bwd(qm_t, km, vm, o, do, do_t, lse, p_saved):
    s_len = km.shape[0]
    nq = s_len // TQ
    total = (HEADS // 2) * (nq * (nq + 1) // 2)

    def body(qt_ref, k_ref, v_ref, o_ref, do_ref, dot_ref, lse_ref, p_hbm, dq_ref, dk_ref, dv_ref, p_scr, sems):
        pair, qi = pl.program_id(0), pl.program_id(1)

        def number(kj):
            return _tile_number(pair, qi, kj, nq)

        def fetch(p_, q_, k_):
            for h in range(2):
                _p_copy(False, p_hbm, p_scr, sems, p_, q_, k_, nq, h).start()

        def advance(p_, q_, k_):
            row_end = k_ == q_
            last_row = q_ == nq - 1
            return (jnp.where(jnp.logical_and(row_end, last_row), p_ + 1, p_),
                    jnp.where(row_end, jnp.where(last_row, 0, q_ + 1), q_),
                    jnp.where(row_end, 0, k_ + 1))

        def ahead(tile, steps):
            for _ in range(steps):
                tile = advance(*tile)
            return tile

        @pl.when(jnp.logical_and(pair == 0, qi == 0))
        def _():
            for first in range(MLA_UNROLL):
                fetch(*ahead((pair, qi, 0), first))

        @pl.when(qi == 0)
        def _():
            dk_ref[...] = jnp.zeros_like(dk_ref)
            dv_ref[...] = jnp.zeros_like(dv_ref)

        do_v = do_ref[...]
        dot_v = dot_ref[...]
        od = o_ref[...] * do_v.astype(F32)
        masks = _head_masks()
        qts = [qt_ref[_head_lanes(h), :] for h in range(2)]
        doms = [_masked(hm, do_v) for hm in masks]
        deltas = [_rowsum(jnp.where(hm, od, 0.0)) for hm in masks]
        lses = [lse_ref[h] for h in range(2)]

        def block(kj, dqs):
            rows = _key_rows(kj)
            slot = number(kj) % P_SLOTS
            vb = v_ref[rows, :]
            dqs = list(dqs)
            dv_t = []
            for h in range(2):
                kb = k_ref[rows, _head_lanes(h)]
                tile_max = _rowsum(p_scr[slot, h, :, TQ:P_COLS].astype(F32)) * (1.0 / 64.0)
                p = p_scr[slot, h, :, 0:TQ].astype(F32) * jnp.exp(tile_max - lses[h])
                dp = _dot_nt(doms[h], vb)
                ds = (p * (dp - deltas[h])).astype(BF16)
                dk_ref[kj, _head_lanes(h), :] += _dot_nn(qts[h], ds)
                dv_t.append(_dot_nn(dot_v, p.astype(BF16)))
                dqs[h] = dqs[h] + _dot_nn(ds, kb)
            dv_ref[kj] += _head_rows(*dv_t)
            return tuple(dqs)

        def trip(tiles, dqs):
            for kj in tiles:
                @pl.when(number(kj) + MLA_UNROLL < total)
                def _():
                    fetch(*ahead((pair, qi, kj), MLA_UNROLL))
            for kj in tiles:
                for h in range(2):
                    _p_copy(False, p_hbm, p_scr, sems, pair, qi, kj, nq, h).wait()
            for kj in tiles:
                dqs = block(kj, dqs)
            return dqs

        za = jnp.zeros((TQ, LANES), F32)
        dqs = _over_key_tiles(qi + 1, trip, (za, za), ascending=True, unroll=MLA_UNROLL)
        dq_ref[:, _head_lanes(0)] = dqs[0] * MLA_SCALE
        dq_ref[:, _head_lanes(1)] = dqs[1] * MLA_SCALE

    return pl.pallas_call(
        body, name="mla_bwd", grid=(HEADS // 2, nq),
        out_shape=[jax.ShapeDtypeStruct((s_len, HEADS * LANES), F32),
                   jax.ShapeDtypeStruct((nq, HEADS * LANES, TQ), F32),
                   jax.ShapeDtypeStruct((nq, MLA_WIDTH, TQ), F32)],
        in_specs=[pl.BlockSpec((2 * LANES, TQ), lambda p, i: (p, i)),
                  pl.BlockSpec((s_len, 2 * LANES), lambda p, i: (0, p)),
                  pl.BlockSpec((s_len, LANES), lambda p, i: (0, p)),
                  pl.BlockSpec((TQ, LANES), lambda p, i: (i, p)),
                  pl.BlockSpec((TQ, LANES), lambda p, i: (i, p)),
                  pl.BlockSpec((LANES, TQ), lambda p, i: (p, i)),
                  pl.BlockSpec((2, TQ, 1), lambda p, i: (p, i, 0)),
                  _any_spec()],
        out_specs=[pl.BlockSpec((TQ, 2 * LANES), lambda p, i: (i, p)),
                   pl.BlockSpec((nq, 2 * LANES, TQ), lambda p, i: (0, p, 0)),
                   pl.BlockSpec((nq, LANES, TQ), lambda p, i: (0, p, 0))],
        scratch_shapes=[pltpu.VMEM((P_SLOTS, 2, TQ, P_COLS), BF16), pltpu.SemaphoreType.DMA((2 * P_SLOTS,))],
        compiler_params=pltpu.CompilerParams(dimension_semantics=("arbitrary", "arbitrary"),
                                             vmem_limit_bytes=VMEM_BIG),
    )(qm_t, km, vm, o, do, do_t, lse, p_saved)


def _mid(o_sb, g_sb, o_mla, g_mla, x, target, gate, w_out):
    s_len = x.shape[0]

    def body(osb_ref, gsb_ref, omla_ref, gmla_ref, x_ref, t_ref, gate_ref, w_ref,
             dy_ref, dosb_ref, dgsb_ref, domla_ref, dgmla_ref, gw_ref, dgate_ref, loss_ref, dosbt_ref, domlat_ref):
        @pl.when(pl.program_id(0) == 0)
        def _():
            gw_ref[...] = jnp.zeros_like(gw_ref)
            dgate_ref[...] = jnp.zeros_like(dgate_ref)
            loss_ref[...] = jnp.zeros_like(loss_ref)

        o1, g1, o2, g2 = osb_ref[...], gsb_ref[...], omla_ref[...], gmla_ref[...]
        s1, s2 = _sigmoid(g1), _sigmoid(g2)
        mixed = jnp.concatenate([o1 * (g1 * s1), o2 * (g2 * s2)], axis=1).astype(BF16)
        w = w_ref[...]
        gate_v = gate_ref[...]
        u = _dot_nn(mixed, w)
        err = x_ref[...] + gate_v * u - t_ref[...]
        loss_ref[...] += jnp.sum(err * err, axis=0, keepdims=True)
        dy = err * (1.0 / D_MODEL)
        dy_ref[...] = dy
        dgate_ref[...] += jnp.sum(dy * u, axis=0, keepdims=True)
        du = (dy * gate_v).astype(BF16)
        gw_ref[...] += _dot_tn(mixed, du)
        dmixed = _dot_nt(du, w)
        d1, d2 = dmixed[:, :SB_WIDTH], dmixed[:, SB_WIDTH:]
        do1, do2 = d1 * (g1 * s1), d2 * (g2 * s2)
        dosb_ref[...] = do1.astype(BF16)
        dgsb_ref[...] = (d1 * o1 * (s1 * (1.0 + g1 * (1.0 - s1)))).astype(BF16)
        domla_ref[...] = do2.astype(BF16)
        dgmla_ref[...] = (d2 * o2 * (s2 * (1.0 + g2 * (1.0 - s2)))).astype(BF16)
        dosbt_ref[...] = do1.T.astype(BF16)
        domlat_ref[...] = do2.T.astype(BF16)

    def tile(width):
        return pl.BlockSpec((TS, width), lambda i: (i, 0))

    def full(shape):
        return pl.BlockSpec(shape, lambda i: (0, 0))

    return pl.pallas_call(
        body, name="mid", grid=(s_len // TS,),
        out_shape=[jax.ShapeDtypeStruct((s_len, D_MODEL), F32)]
        + [jax.ShapeDtypeStruct((s_len, 512), BF16)] * 4
        + [jax.ShapeDtypeStruct((D_MODEL, D_MODEL), F32),
           jax.ShapeDtypeStruct((1, D_MODEL), F32), jax.ShapeDtypeStruct((1, D_MODEL), F32)]
        + [jax.ShapeDtypeStruct((512, s_len), BF16)] * 2,
        in_specs=[tile(512)] * 4 + [tile(D_MODEL), tile(D_MODEL), full((1, D_MODEL)), full((D_MODEL, D_MODEL))],
        out_specs=[tile(D_MODEL)] + [tile(512)] * 4
        + [full((D_MODEL, D_MODEL)), full((1, D_MODEL)), full((1, D_MODEL))]
        + [pl.BlockSpec((512, TS), lambda i: (0, i))] * 2,
        compiler_params=pltpu.CompilerParams(dimension_semantics=("arbitrary",), vmem_limit_bytes=VMEM_BIG),
    )(o_sb, g_sb, o_mla, g_mla, x, target, gate, w_out)


def _mla_pre_bwd(dq, dk, dv, cq, ckv, kr, qln, wuq, kvln, wukv, qhn, khn, rope_table):
    s_len = cq.shape[0]

    def body(dq_ref, dk_ref, dv_ref, cq_ref, ckv_ref, kr_ref, qln_ref, wuq_ref, kvln_ref, wukv_ref,
             qhn_ref, khn_ref, rt_ref,
             dcq_ref, dckv_ref, dkr_ref, gwuq_ref, gwukv_ref, gqhn_ref, gkhn_ref, gqln_ref, gkvln_ref,
             dqa_ref, dkv_ref):
        @pl.when(pl.program_id(0) == 0)
        def _():
            for r_ in (gwuq_ref, gwukv_ref, gqhn_ref, gkhn_ref, gqln_ref, gkvln_ref):
                r_[...] = jnp.zeros_like(r_)

        cq, ckv = cq_ref[...], ckv_ref[...]
        qln_v, kvln_v = qln_ref[...], kvln_ref[...]
        wuq_v, wukv_v = wuq_ref[...], wukv_ref[...]
        rq, rkv, cq_hat, ckv_hat, cqn, ckvn, q_all, kv, kr64 = _mla_latents(
            cq, ckv, kr_ref[...], qln_v, kvln_v, wuq_v, wukv_v)
        cosf, sa, sb = _expand_rope(rt_ref[...])
        qhn_v, khn_v = qhn_ref[...], khn_ref[...]
        lane = _lane_iota((TS, LANES))
        low = lane < NOPE
        blks = [slice(LANES * hd, LANES * (hd + 1)) for hd in range(HEADS)]
        raw = [q_all[:, b] for b in blks] + [jnp.where(low, kv[:, b], kr64) for b in blks]
        grads = [dq_ref[:, b] for b in blks] + [dk_ref[:, b] for b in blks]
        gains = [qhn_v] * HEADS + [khn_v] * HEADS
        rms = [_head_rms(t) for t in raw]
        xhs = [t * r for t, r in zip(raw, rms)]
        dns = [_rope_t(d, cosf, sa, sb) for d in grads]
        gain_g = [jnp.sum(dn * xh, axis=0, keepdims=True) for dn, xh in zip(dns, xhs)]
        dxs = [_rms_bwd(dn * g, xh, r, MLA_QK) for dn, g, xh, r in zip(dns, gains, xhs, rms)]
        dkr64 = jnp.zeros((TS, LANES), F32)
        for hd, b in enumerate(blks):
            dqa_ref[:, b] = dxs[hd].astype(BF16)
            dkb = dxs[HEADS + hd]
            dkr64 = dkr64 + jnp.where(low, 0.0, dkb)
            dvp = dv_ref[:, LANES * (hd // 2):LANES * (hd // 2 + 1)]
            dvh = pltpu.roll(dvp, 64, 1) if hd % 2 == 0 else dvp
            dkv_ref[:, b] = jnp.where(low, dkb, dvh).astype(BF16)
        gqhn_ref[...] += sum(gain_g[:HEADS])
        gkhn_ref[...] += sum(gain_g[HEADS:])
        dkr_ref[...] = pltpu.roll(dkr64, 64, 1).astype(BF16)

        dqa = dqa_ref[...]
        gwuq_ref[...] += _dot_tn(cqn, dqa)
        dcqn = _dot_nt(dqa, wuq_v)
        gqln_ref[...] += jnp.sum(dcqn * cq_hat, axis=0, keepdims=True)
        dcq_ref[...] = _rms_bwd(dcqn * qln_v, cq_hat, rq, Q_LORA).astype(BF16)

        dkv = dkv_ref[...]
        gwukv_ref[...] += _dot_tn(ckvn, dkv)
        dckvn = _dot_nt(dkv, wukv_v)
        gkvln_ref[...] += jnp.sum(dckvn * ckv_hat, axis=0, keepdims=True)
        dckv_ref[...] = _rms_bwd(dckvn * kvln_v, ckv_hat, rkv, KV_LORA).astype(BF16)

    def tile(width):
        return pl.BlockSpec((TS, width), lambda i: (i, 0))

    def full(shape):
        return pl.BlockSpec(shape, lambda i: (0, 0))

    acc_shapes = [(Q_LORA, 1024), (KV_LORA, 1024), (1, LANES), (1, LANES), (1, Q_LORA), (1, KV_LORA)]
    return pl.pallas_call(
        body, name="mla_pre_bwd", grid=(s_len // TS,),
        out_shape=[jax.ShapeDtypeStruct((s_len, Q_LORA), BF16), jax.ShapeDtypeStruct((s_len, KV_LORA), BF16),
                   jax.ShapeDtypeStruct((s_len, LANES), BF16)]
        + [jax.ShapeDtypeStruct(s, F32) for s in acc_shapes],
        in_specs=[tile(1024), tile(1024), tile(512), tile(Q_LORA), tile(KV_LORA), tile(LANES),
                  full(qln.shape), full(wuq.shape), full(kvln.shape), full(wukv.shape),
                  full(qhn.shape), full(khn.shape), tile(LANES)],
        out_specs=[tile(Q_LORA), tile(KV_LORA), tile(LANES)] + [full(s) for s in acc_shapes],
        scratch_shapes=[pltpu.VMEM((TS, 1024), BF16), pltpu.VMEM((TS, 1024), BF16)],
        compiler_params=pltpu.CompilerParams(dimension_semantics=("arbitrary",), vmem_limit_bytes=VMEM_BIG),
    )(dq, dk, dv, cq, ckv, kr, qln, wuq, kvln, wukv, qhn, khn, rope_table)


def _dproj_bwd(dq_sb, dk_sb, dv_sb, dg_sb, dcq, dckv, dg_mla, dkr, w_in_r, x, dy, norm_w, scale):
    s_len = x.shape[0]

    def body(dq_ref, dk_ref, dv_ref, dg_ref, dcq_ref, dckv_ref, dgm_ref, dkr_ref, w_ref, x_ref, dy_ref,
             nw_ref, scale_ref, dp_ref, gx_ref, dshift_ref, dscale_ref, dnw_ref):
        @pl.when(pl.program_id(0) == 0)
        def _():
            for r_ in (dshift_ref, dscale_ref, dnw_ref):
                r_[...] = jnp.zeros_like(r_)

        dp_ref[:, 0:512] = dq_ref[...].astype(BF16)
        dp_ref[:, 512:1024] = dk_ref[...].astype(BF16)
        dp_ref[:, 1024:1536] = dv_ref[...].astype(BF16)
        dp_ref[:, 1536:2048] = dg_ref[...]
        dp_ref[:, 2048:2432] = dcq_ref[...]
        dp_ref[:, 2432:2688] = dckv_ref[...]
        dp_ref[:, 2688:3200] = dgm_ref[...]
        dp_ref[:, 3200:3328] = dkr_ref[...]
        dh = _dot_nt(dp_ref[...], w_ref[...])
        xv = x_ref[...]
        r = lax.rsqrt(jnp.mean(xv * xv, axis=1, keepdims=True) + EPS)
        xh = xv * r
        nw = nw_ref[...]
        dshift_ref[...] += jnp.sum(dh, axis=0, keepdims=True)
        dscale_ref[...] += jnp.sum(dh * (xh * nw), axis=0, keepdims=True)
        dxnw = dh * (1.0 + scale_ref[...])
        dnw_ref[...] += jnp.sum(dxnw * xh, axis=0, keepdims=True)
        gx_ref[...] = dy_ref[...] + _rms_bwd(dxnw * nw, xh, r, D_MODEL)

    def tile(width):
        return pl.BlockSpec((TS, width), lambda i: (i, 0))

    def full(shape):
        return pl.BlockSpec(shape, lambda i: (0, 0))

    vec = (1, D_MODEL)
    return pl.pallas_call(
        body, name="dproj_bwd", grid=(s_len // TS,),
        out_shape=[jax.ShapeDtypeStruct((s_len, IN_COLS_R), BF16), jax.ShapeDtypeStruct((s_len, D_MODEL), F32)]
        + [jax.ShapeDtypeStruct(vec, F32)] * 3,
        in_specs=[tile(512)] * 4 + [tile(Q_LORA), tile(KV_LORA), tile(512), tile(LANES),
                                    full(w_in_r.shape), tile(D_MODEL), tile(D_MODEL), full(vec), full(vec)],
        out_specs=[tile(IN_COLS_R), tile(D_MODEL)] + [full(vec)] * 3,
        compiler_params=pltpu.CompilerParams(dimension_semantics=("arbitrary",), vmem_limit_bytes=VMEM_BIG),
    )(dq_sb, dk_sb, dv_sb, dg_sb, dcq, dckv, dg_mla, dkr, w_in_r, x, dy, norm_w, scale)


def _grad_w_in(hb, dproj):
    s_len = hb.shape[0]
    n_half = IN_COLS_R // 2
    n_steps = s_len // TN_S

    def body(h_ref, d_ref, g_ref, acc_ref):
        @pl.when(pl.program_id(1) == 0)
        def _():
            acc_ref[...] = jnp.zeros_like(acc_ref)

        acc_ref[...] += _dot_tn(h_ref[...], d_ref[...])

        @pl.when(pl.program_id(1) == n_steps - 1)
        def _():
            g_ref[...] = acc_ref[...].astype(BF16)

    return pl.pallas_call(
        body, name="grad_w_in", grid=(2, n_steps),
        out_shape=jax.ShapeDtypeStruct((D_MODEL, IN_COLS_R), BF16),
        in_specs=[pl.BlockSpec((TN_S, D_MODEL), lambda n, s: (s, 0)),
                  pl.BlockSpec((TN_S, n_half), lambda n, s: (s, n))],
        out_specs=pl.BlockSpec((D_MODEL, n_half), lambda n, s: (0, n)),
        scratch_shapes=[pltpu.VMEM((D_MODEL, n_half), F32)],
        compiler_params=pltpu.CompilerParams(dimension_semantics=("parallel", "arbitrary"),
                                             vmem_limit_bytes=VMEM_BIG),
    )(hb, dproj)


def _final_exchange(gpack, ccol, wpack, mpack, vpack, n_sh, grads):
    n = len(grads)

    def body(*refs):
        (g_ref, cc_ref, wp_ref, mp_ref, vp_ref) = refs[:5]
        slabs_in = refs[5:5 + n]
        (og_ref, od_ref, om_ref, ov_ref, ag_ref) = refs[5 + n:10 + n]
        slabs_out = refs[10 + n:10 + 2 * n]
        gall_ref, ssem, rsem, slab_ssem, slab_rsem, lsem = refs[10 + 2 * n:]
        pos = _mesh_pos()
        me = _lin(pos)
        gall_ref[me] = g_ref[...]
        small = _all_gather_start(pos, g_ref, gall_ref, ssem, rsem, 0)
        own = [pltpu.make_async_copy(slabs_in[a].at[me], slabs_out[a].at[me], lsem.at[a]) for a in range(n)]
        for cp in own:
            cp.start()
        in_flight = [_all_to_all_start(pos, slabs_in[a], slabs_out[a], slab_ssem, slab_rsem, a * (N_DEV - 1))
                     for a in range(n)]
        _all_gather_wait(pos, g_ref, gall_ref, ssem, rsem, 0, small)

        tot = gall_ref[0]
        for j in range(1, N_DEV):
            tot = tot + gall_ref[j]
        og_ref[...] = tot
        od_ref[...], om_ref[...], ov_ref[...] = _adamw(wp_ref[...], tot, mp_ref[...], vp_ref[...])

        ga = jnp.zeros((D_MODEL, n_sh), F32)
        for j in range(N_DEV):
            d_mine = jnp.zeros((8, n_sh), F32)
            for k in range(N_DEV):
                d_mine = d_mine + jnp.where(me == k, gall_ref[j, :, PK_ADA + n_sh * k:PK_ADA + n_sh * (k + 1)], 0.0)
            col = _silu(cc_ref[j])
            ga = ga + jnp.concatenate(
                [col * d_mine[0:1, LANES * a:LANES * (a + 1)] for a in range(n_sh // LANES)], axis=1)
        ag_ref[...] = ga

        for a in range(n):
            _all_to_all_wait(pos, slabs_in[a], slabs_out[a], slab_ssem, slab_rsem, a * (N_DEV - 1), in_flight[a])
        for cp in own:
            cp.wait()

    pk = jax.ShapeDtypeStruct((8, PK_END), F32)
    ada = jax.ShapeDtypeStruct((D_MODEL, n_sh), F32)
    return pl.pallas_call(
        body, name="final_exchange",
        out_shape=[pk] * 4 + [ada] + [jax.ShapeDtypeStruct(g.shape, g.dtype) for g in grads],
        in_specs=[_vmem_spec()] * 5 + [_any_spec()] * n,
        out_specs=[_vmem_spec()] * 5 + [_any_spec()] * n,
        scratch_shapes=[
            pltpu.VMEM((N_DEV, 8, PK_END), F32),
            pltpu.SemaphoreType.DMA((N_DEV - 1,)),
            pltpu.SemaphoreType.DMA((N_DEV - 1,)),
            pltpu.SemaphoreType.DMA((n * (N_DEV - 1),)),
            pltpu.SemaphoreType.DMA((n * (N_DEV - 1),)),
            pltpu.SemaphoreType.DMA((n,)),
        ],
        compiler_params=pltpu.CompilerParams(vmem_limit_bytes=VMEM_BIG),
    )(gpack, ccol, wpack, mpack, vpack, *grads)


def _adamw_reduce(name, parts, w, m, v, row_tile):
    rows, cols = w.shape
    n_parts = parts.shape[0]

    def body(p_ref, w_ref, m_ref, v_ref, g_ref, d_ref, mo_ref, vo_ref):
        g = p_ref[0].astype(F32)
        for j in range(1, n_parts):
            g = g + p_ref[j].astype(F32)
        g_ref[...] = g
        d_ref[...], mo_ref[...], vo_ref[...] = _adamw(w_ref[...], g, m_ref[...], v_ref[...])

    tile = pl.BlockSpec((row_tile, cols), lambda i: (i, 0))
    return pl.pallas_call(
        body, name=name, grid=(rows // row_tile,),
        out_shape=[jax.ShapeDtypeStruct((rows, cols), F32)] * 4,
        in_specs=[pl.BlockSpec((n_parts, row_tile, cols), lambda i: (0, i, 0)), tile, tile, tile],
        out_specs=[tile] * 4,
        compiler_params=pltpu.CompilerParams(dimension_semantics=("parallel",), vmem_limit_bytes=VMEM_BIG),
    )(parts, w, m, v)


def _rope_table(positions):
    inv_freq = 10000.0 ** (-jnp.arange(0, ROPE, 2, dtype=F32) / ROPE)
    ang = positions.astype(F32)[:, None] * inv_freq
    cos, sin = jnp.cos(ang), jnp.sin(ang)
    pad = jnp.zeros((positions.shape[0], LANES - 2 * ROPE), F32)
    return jnp.concatenate([cos, cos, -sin, sin, pad], axis=1)


def _rearrange_cols(w):
    pad = jnp.zeros((w.shape[0], IN_COLS_R - IN_COLS), w.dtype)
    return jnp.concatenate([w[:, :2688], w[:, 2720:3232], w[:, 2688:2720], pad], axis=1)


def _restore_cols(g):
    return jnp.concatenate([g[:, :2688], g[:, 3200:3232], g[:, 2688:3200]], axis=1)


def _pad_heads(w):
    rows = w.shape[0]
    w = w.reshape(rows, HEADS, MLA_QK)
    return jnp.pad(w, ((0, 0), (0, 0), (0, LANES - MLA_QK))).reshape(rows, HEADS * LANES)


def _unpad_heads(g):
    rows = g.shape[0]
    return g.reshape(rows, HEADS, LANES)[:, :, :MLA_QK].reshape(rows, HEADS * MLA_QK)


def _pad_lanes(v):
    return jnp.pad(v, ((0, 0), (0, LANES - v.shape[1])))


def _col_shards(g):
    rows = g.shape[0]
    return g.reshape(rows, N_DEV, g.shape[1] // N_DEV).transpose(1, 0, 2)


def _from_col_shards(g):
    return g.transpose(1, 0, 2).reshape(g.shape[1], N_DEV * g.shape[2])


def _pack(norm_w, qln, kvln, qhn, khn, ada, loss_lanes=None):
    if loss_lanes is None:
        loss_lanes = jnp.zeros((1, PK_END - PK_LOSS), F32)
    row = jnp.concatenate([norm_w, qln, kvln, _pad_lanes(qhn), _pad_lanes(khn), ada, loss_lanes], axis=1)
    return jnp.broadcast_to(row, (8, PK_END))


def _unpack(p):
    row = p[0:1]
    return (row[:, PK_NORM:PK_QLN], row[:, PK_QLN:PK_KVLN], row[:, PK_KVLN:PK_QHN],
            row[:, PK_QHN:PK_QHN + MLA_QK], row[:, PK_KHN:PK_KHN + MLA_QK], row[:, PK_ADA:PK_LOSS])


def kernel(x, c, positions, w_ada, b_ada, norm_w, w_in, q_lora_norm, w_uq, kv_lora_norm, w_ukv, q_head_norm, k_head_norm, w_out, loss_target, m_w_ada, m_b_ada, m_norm_w, m_w_in, m_q_lora_norm, m_w_uq, m_kv_lora_norm, m_w_ukv, m_q_head_norm, m_k_head_norm, m_w_out, v_w_ada, v_b_ada, v_norm_w, v_w_in, v_q_lora_norm, v_w_uq, v_kv_lora_norm, v_w_ukv, v_q_head_norm, v_k_head_norm, v_w_out):
    s_len = x.shape[1]
    x2 = x.reshape(s_len, D_MODEL)
    tgt = loss_target.reshape(s_len, D_MODEL)
    w_ada_s, w_in_s, w_uq_s, w_ukv_s, w_out_s = w_ada[0], w_in[0], w_uq[0], w_ukv[0], w_out[0]

    ada8, c_all = _ada_fwd(jnp.broadcast_to(c, (8, D_MODEL)), w_ada_s, b_ada.reshape(N_DEV, -1))
    ada = ada8.reshape(1, 3 * D_MODEL)
    shift, scale, gate = ada[:, :D_MODEL], ada[:, D_MODEL:2 * D_MODEL], ada[:, 2 * D_MODEL:]

    g_uq, g_ukv, g_out, g_in = _gather_weights([w_uq_s, w_ukv_s, w_out_s, w_in_s])
    w_in_r = _rearrange_cols(_from_col_shards(g_in))
    wuq_p = _pad_heads(_from_col_shards(g_uq))
    wukv_f = _from_col_shards(g_ukv)
    w_out_f = g_out.reshape(D_MODEL, D_MODEL)

    rope_table = _rope_table(positions[0])
    qhn_p, khn_p = _pad_lanes(q_head_norm), _pad_lanes(k_head_norm)

    hb, qkv, g_sb, cq, ckv, g_mla, kr, qm, km, vm, q_sb_t, qm_t = _fwd_pre(
        x2, shift, scale, norm_w, w_in_r, q_lora_norm, wuq_p, kv_lora_norm, wukv_f, qhn_p, khn_p, rope_table)
    o_sb, w_saved, l_saved = _sb_fwd(qkv)
    o_mla, lse, p_saved = _mla_fwd(qm, km, vm)

    dy, do_sb, dg_sb, do_mla, dg_mla, gw_out, d_gate, loss_acc, do_sb_t, do_mla_t = _mid(
        o_sb, g_sb, o_mla, g_mla, x2, tgt, gate, w_out_f)

    dq_sb, dk_sb_t, dv_sb_t = _sb_bwd(qkv, q_sb_t, do_sb, do_sb_t, w_saved, l_saved)
    dk_sb, dv_sb = _from_key_tiles(dk_sb_t), _from_key_tiles(dv_sb_t)
    dq_m, dk_m_t, dv_m_t = _mla_bwd(qm_t, km, vm, o_mla, do_mla, do_mla_t, lse, p_saved)
    dk_m, dv_m = _from_key_tiles(dk_m_t), _from_key_tiles(dv_m_t)
    dcq, dckv, dkr, gw_uq_p, gw_ukv, g_qhn, g_khn, g_qln, g_kvln = _mla_pre_bwd(
        dq_m, dk_m, dv_m, cq, ckv, kr, q_lora_norm, wuq_p, kv_lora_norm, wukv_f, qhn_p, khn_p, rope_table)
    dproj, grad_x, d_shift, d_scale, g_norm_w = _dproj_bwd(
        dq_sb, dk_sb, dv_sb, dg_sb, dcq, dckv, dg_mla, dkr, w_in_r, x2, dy, norm_w, scale)
    gw_in = _restore_cols(_grad_w_in(hb, dproj))

    d_ada = jnp.concatenate([d_shift, d_scale, d_gate], axis=1)
    gpack = _pack(g_norm_w, g_qln, g_kvln, g_qhn[:, :MLA_QK], g_khn[:, :MLA_QK], d_ada, loss_acc)
    wpack = _pack(norm_w, q_lora_norm, kv_lora_norm, q_head_norm, k_head_norm, b_ada)
    mpack = _pack(m_norm_w, m_q_lora_norm, m_kv_lora_norm, m_q_head_norm, m_k_head_norm, m_b_ada)
    vpack = _pack(v_norm_w, v_q_lora_norm, v_kv_lora_norm, v_q_head_norm, v_k_head_norm, v_b_ada)
    ccol = jnp.broadcast_to(c_all[:, :, None], (N_DEV, D_MODEL, LANES))
    slabs = [g.astype(BF16) for g in (
        _col_shards(gw_in), _col_shards(_unpad_heads(gw_uq_p)), _col_shards(gw_ukv),
        gw_out.reshape(N_DEV, D_MODEL // N_DEV, D_MODEL))]
    pg, pd, pm, pv, gw_ada, r_in, r_uq, r_ukv, r_out = _final_exchange(
        gpack, ccol, wpack, mpack, vpack, w_ada_s.shape[1], slabs)
    loss = 0.5 * jnp.sum(pg[0, PK_LOSS:PK_END]) / D_MODEL

    ada_g, ada_d, ada_m, ada_v = _adamw_reduce("adamw_w_ada", gw_ada[None], w_ada_s, m_w_ada[0], v_w_ada[0], 256)
    in_g, in_d, in_m, in_v = _adamw_reduce("adamw_w_in", r_in, w_in_s, m_w_in[0], v_w_in[0], 256)
    uq_g, uq_d, uq_m, uq_v = _adamw_reduce("adamw_w_uq", r_uq, w_uq_s, m_w_uq[0], v_w_uq[0], w_uq_s.shape[0])
    ukv_g, ukv_d, ukv_m, ukv_v = _adamw_reduce(
        "adamw_w_ukv", r_ukv, w_ukv_s, m_w_ukv[0], v_w_ukv[0], w_ukv_s.shape[0])
    out_g, out_d, out_m, out_v = _adamw_reduce(
        "adamw_w_out", r_out, w_out_s, m_w_out[0], v_w_out[0], w_out_s.shape[0])

    def group(ada_t, pk, in_t, uq_t, ukv_t, out_t):
        nw, qln, kvln, qhn, khn, b = _unpack(pk)
        return (ada_t[None], b, nw, in_t[None], qln, uq_t[None], kvln, ukv_t[None], qhn, khn, out_t[None])

    return (loss, grad_x.reshape(1, s_len, D_MODEL),
            *group(ada_g, pg, in_g, uq_g, ukv_g, out_g),
            *group(ada_d, pd, in_d, uq_d, ukv_d, out_d),
            *group(ada_m, pm, in_m, uq_m, ukv_m, out_m),
            *group(ada_v, pv, in_v, uq_v, ukv_v, out_v))
```
